```python
import jax, jax.numpy as jnp
from jax import lax
import numpy as np

D_MODEL = 1024
BATCH = 8
SEQ = 8192
DEPTH = 1

D_A = D_MODEL
D_B = D_MODEL
CONV_A_WIDTH = 3
CONV_B_WIDTH = 31
D_FF = 4 * D_MODEL
RMS_EPS = 1e-6
LN_EPS = 1e-5
N_IN = 3 * D_A + 2 * D_B + 2 * D_MODEL

kernel_name = "hybrid_shortconv_conformer_conv_block"


def rms_norm(x, g):
    xf = x.astype(jnp.float32)
    y = xf * lax.rsqrt(jnp.mean(xf * xf, axis=-1, keepdims=True) + RMS_EPS)
    return (y * g.astype(jnp.float32)).astype(x.dtype)


def layer_norm(x, g, b):
    xf = x.astype(jnp.float32)
    mu = jnp.mean(xf, axis=-1, keepdims=True)
    var = jnp.mean(jnp.square(xf - mu), axis=-1, keepdims=True)
    y = (xf - mu) * lax.rsqrt(var + LN_EPS)
    return (y * g.astype(jnp.float32) + b.astype(jnp.float32)).astype(x.dtype)


def depthwise_conv_centred(u, w, b):
    k, c = w.shape
    pad = (k - 1) // 2
    y = lax.conv_general_dilated(
        u, w[:, None, :].astype(u.dtype), window_strides=(1,), padding=[(pad, pad)],
        dimension_numbers=("NWC", "WIO", "NWC"), feature_group_count=c)
    return y + b.astype(u.dtype)


def _fwd_setup_inputs(seed: int = 0) -> dict:
    key = jax.random.key(seed)
    ks = jax.random.split(key, 24)
    f32 = jnp.float32

    def nrm(k, shape, scale):
        return jax.random.normal(k, shape, f32) * scale

    def gain(k, n):
        return jnp.ones((n,), f32) + 0.05 * jax.random.normal(k, (n,), f32)

    return {
        "x": jax.random.normal(ks[0], (BATCH, SEQ, D_MODEL), f32),
        "norm1_pre_g": gain(ks[1], D_MODEL),
        "w_in": nrm(ks[2], (D_MODEL, N_IN), D_MODEL ** -0.5),
        "b_in": nrm(ks[3], (N_IN,), 0.02),
        "conv_a_w": nrm(ks[4], (CONV_A_WIDTH, D_A), CONV_A_WIDTH ** -0.5),
        "conv_a_b": nrm(ks[5], (D_A,), 0.02),
        "w_a_out": nrm(ks[6], (D_A, D_MODEL), D_A ** -0.5),
        "conv_b_w": nrm(ks[7], (CONV_B_WIDTH, D_B), CONV_B_WIDTH ** -0.5),
        "conv_b_b": nrm(ks[8], (D_B,), 0.02),
        "ln_b_g": gain(ks[9], D_B),
        "ln_b_b": nrm(ks[10], (D_B,), 0.02),
        "w_b_out": nrm(ks[11], (D_B, D_MODEL), D_B ** -0.5),
        "w_o": nrm(ks[12], (D_MODEL, D_MODEL), D_MODEL ** -0.5),
        "norm1_post_g": gain(ks[13], D_MODEL),
        "norm2_pre_g": gain(ks[14], D_MODEL),
        "w_mlp_in": nrm(ks[15], (D_MODEL, D_FF), D_MODEL ** -0.5),
        "w_mlp_out": nrm(ks[16], (D_FF, D_MODEL), D_FF ** -0.5),
        "norm2_post_g": gain(ks[17], D_MODEL),
    }


def _fwd_reference(x, norm1_pre_g, w_in, b_in, conv_a_w, conv_a_b, w_a_out,
              conv_b_w, conv_b_b, ln_b_g, ln_b_b, w_b_out, w_o, norm1_post_g,
              norm2_pre_g, w_mlp_in, w_mlp_out, norm2_post_g):
    for _ in range(DEPTH):
        h = rms_norm(x, norm1_pre_g)
        proj = jnp.einsum("bsd,dn->bsn", h, w_in) + b_in
        splits = np.cumsum([D_A, D_A, D_A, D_B, D_B, D_MODEL])
        b_gate, c_gate, h_a, a_b, g_b, z_a, z_b = jnp.split(proj, splits, axis=-1)

        v_a = depthwise_conv_centred(c_gate * h_a, conv_a_w, conv_a_b)
        y_a = jnp.einsum("bsc,cd->bsd", b_gate * v_a, w_a_out)

        u_b = a_b * jax.nn.sigmoid(g_b)
        v_b = depthwise_conv_centred(u_b, conv_b_w, conv_b_b)
        v_b = jax.nn.silu(layer_norm(v_b, ln_b_g, ln_b_b))
        y_b = jnp.einsum("bsc,cd->bsd", v_b, w_b_out)

        merged = jax.nn.sigmoid(z_a) * y_a + jax.nn.sigmoid(z_b) * y_b
        mix_out = jnp.einsum("bsd,de->bse", merged, w_o)
        x = x + rms_norm(mix_out, norm1_post_g)

        h2 = rms_norm(x, norm2_pre_g)
        f = jnp.square(jax.nn.relu(jnp.einsum("bsd,df->bsf", h2, w_mlp_in)))
        f = jnp.einsum("bsf,fd->bsd", f, w_mlp_out)
        x = x + rms_norm(f, norm2_post_g)
    return x


import jax as _jax
import jax.numpy as _jnp

TWIN_FORMAT = 'train_step'
FWD_PARAMS = ['x', 'norm1_pre_g', 'w_in', 'b_in', 'conv_a_w', 'conv_a_b', 'w_a_out', 'conv_b_w', 'conv_b_b', 'ln_b_g', 'ln_b_b', 'w_b_out', 'w_o', 'norm1_post_g', 'norm2_pre_g', 'w_mlp_in', 'w_mlp_out', 'norm2_post_g']
TWIN_WEIGHTS = ['norm1_pre_g', 'w_in', 'b_in', 'conv_a_w', 'conv_a_b', 'w_a_out', 'conv_b_w', 'conv_b_b', 'ln_b_g', 'ln_b_b', 'w_b_out', 'w_o', 'norm1_post_g', 'norm2_pre_g', 'w_mlp_in', 'w_mlp_out', 'norm2_post_g']
TWIN_DIFF_INPUT = 'x'
TWIN_INPUTS = ['x', 'norm1_pre_g', 'w_in', 'b_in', 'conv_a_w', 'conv_a_b', 'w_a_out', 'conv_b_w', 'conv_b_b', 'ln_b_g', 'ln_b_b', 'w_b_out', 'w_o', 'norm1_post_g', 'norm2_pre_g', 'w_mlp_in', 'w_mlp_out', 'norm2_post_g', 'loss_target', 'm_norm1_pre_g', 'm_w_in', 'm_b_in', 'm_conv_a_w', 'm_conv_a_b', 'm_w_a_out', 'm_conv_b_w', 'm_conv_b_b', 'm_ln_b_g', 'm_ln_b_b', 'm_w_b_out', 'm_w_o', 'm_norm1_post_g', 'm_norm2_pre_g', 'm_w_mlp_in', 'm_w_mlp_out', 'm_norm2_post_g', 'v_norm1_pre_g', 'v_w_in', 'v_b_in', 'v_conv_a_w', 'v_conv_a_b', 'v_w_a_out', 'v_conv_b_w', 'v_conv_b_b', 'v_ln_b_g', 'v_ln_b_b', 'v_w_b_out', 'v_w_o', 'v_norm1_post_g', 'v_norm2_pre_g', 'v_w_mlp_in', 'v_w_mlp_out', 'v_norm2_post_g']
TWIN_OUTPUTS = ['loss', 'grad_x', 'grad_norm1_pre_g', 'grad_w_in', 'grad_b_in', 'grad_conv_a_w', 'grad_conv_a_b', 'grad_w_a_out', 'grad_conv_b_w', 'grad_conv_b_b', 'grad_ln_b_g', 'grad_ln_b_b', 'grad_w_b_out', 'grad_w_o', 'grad_norm1_post_g', 'grad_norm2_pre_g', 'grad_w_mlp_in', 'grad_w_mlp_out', 'grad_norm2_post_g', 'delta_norm1_pre_g', 'delta_w_in', 'delta_b_in', 'delta_conv_a_w', 'delta_conv_a_b', 'delta_w_a_out', 'delta_conv_b_w', 'delta_conv_b_b', 'delta_ln_b_g', 'delta_ln_b_b', 'delta_w_b_out', 'delta_w_o', 'delta_norm1_post_g', 'delta_norm2_pre_g', 'delta_w_mlp_in', 'delta_w_mlp_out', 'delta_norm2_post_g', 'new_m_norm1_pre_g', 'new_m_w_in', 'new_m_b_in', 'new_m_conv_a_w', 'new_m_conv_a_b', 'new_m_w_a_out', 'new_m_conv_b_w', 'new_m_conv_b_b', 'new_m_ln_b_g', 'new_m_ln_b_b', 'new_m_w_b_out', 'new_m_w_o', 'new_m_norm1_post_g', 'new_m_norm2_pre_g', 'new_m_w_mlp_in', 'new_m_w_mlp_out', 'new_m_norm2_post_g', 'new_v_norm1_pre_g', 'new_v_w_in', 'new_v_b_in', 'new_v_conv_a_w', 'new_v_conv_a_b', 'new_v_w_a_out', 'new_v_conv_b_w', 'new_v_conv_b_b', 'new_v_ln_b_g', 'new_v_ln_b_b', 'new_v_w_b_out', 'new_v_w_o', 'new_v_norm1_post_g', 'new_v_norm2_pre_g', 'new_v_w_mlp_in', 'new_v_w_mlp_out', 'new_v_norm2_post_g']
TWIN_LEAF_KINDS = {'loss': 'loss', 'grad_x': 'grad_x', 'grad_norm1_pre_g': 'grad_w', 'grad_w_in': 'grad_w', 'grad_b_in': 'grad_w', 'grad_conv_a_w': 'grad_w', 'grad_conv_a_b': 'grad_w', 'grad_w_a_out': 'grad_w', 'grad_conv_b_w': 'grad_w', 'grad_conv_b_b': 'grad_w', 'grad_ln_b_g': 'grad_w', 'grad_ln_b_b': 'grad_w', 'grad_w_b_out': 'grad_w', 'grad_w_o': 'grad_w', 'grad_norm1_post_g': 'grad_w', 'grad_norm2_pre_g': 'grad_w', 'grad_w_mlp_in': 'grad_w', 'grad_w_mlp_out': 'grad_w', 'grad_norm2_post_g': 'grad_w', 'delta_norm1_pre_g': 'delta_w', 'delta_w_in': 'delta_w', 'delta_b_in': 'delta_w', 'delta_conv_a_w': 'delta_w', 'delta_conv_a_b': 'delta_w', 'delta_w_a_out': 'delta_w', 'delta_conv_b_w': 'delta_w', 'delta_conv_b_b': 'delta_w', 'delta_ln_b_g': 'delta_w', 'delta_ln_b_b': 'delta_w', 'delta_w_b_out': 'delta_w', 'delta_w_o': 'delta_w', 'delta_norm1_post_g': 'delta_w', 'delta_norm2_pre_g': 'delta_w', 'delta_w_mlp_in': 'delta_w', 'delta_w_mlp_out': 'delta_w', 'delta_norm2_post_g': 'delta_w', 'new_m_norm1_pre_g': 'new_m', 'new_m_w_in': 'new_m', 'new_m_b_in': 'new_m', 'new_m_conv_a_w': 'new_m', 'new_m_conv_a_b': 'new_m', 'new_m_w_a_out': 'new_m', 'new_m_conv_b_w': 'new_m', 'new_m_conv_b_b': 'new_m', 'new_m_ln_b_g': 'new_m', 'new_m_ln_b_b': 'new_m', 'new_m_w_b_out': 'new_m', 'new_m_w_o': 'new_m', 'new_m_norm1_post_g': 'new_m', 'new_m_norm2_pre_g': 'new_m', 'new_m_w_mlp_in': 'new_m', 'new_m_w_mlp_out': 'new_m', 'new_m_norm2_post_g': 'new_m', 'new_v_norm1_pre_g': 'new_v', 'new_v_w_in': 'new_v', 'new_v_b_in': 'new_v', 'new_v_conv_a_w': 'new_v', 'new_v_conv_a_b': 'new_v', 'new_v_w_a_out': 'new_v', 'new_v_conv_b_w': 'new_v', 'new_v_conv_b_b': 'new_v', 'new_v_ln_b_g': 'new_v', 'new_v_ln_b_b': 'new_v', 'new_v_w_b_out': 'new_v', 'new_v_w_o': 'new_v', 'new_v_norm1_post_g': 'new_v', 'new_v_norm2_pre_g': 'new_v', 'new_v_w_mlp_in': 'new_v', 'new_v_w_mlp_out': 'new_v', 'new_v_norm2_post_g': 'new_v'}


def _forward(args):
    return _fwd_reference(*[args[k] for k in FWD_PARAMS])


def _output_shape():
    def fwd():
        inp = _fwd_setup_inputs(0)
        return _fwd_reference(*[inp[k] for k in FWD_PARAMS])
    out = _jax.eval_shape(fwd)
    return out.shape, out.dtype

N_MICROBATCH = 1
ADAM_LR = 0.001
ADAM_B1 = 0.9
ADAM_B2 = 0.999
ADAM_EPS = 1e-08
ADAM_WD = 0.01
ADAM_STEP = 10
PER_EXAMPLE_BATCH_AXIS = {'x': 0, 'loss_target': 0}
SHARED_INPUTS = []
_WEIGHT_DTYPES = {'norm1_pre_g': _jnp.float32, 'w_in': _jnp.float32, 'b_in': _jnp.float32, 'conv_a_w': _jnp.float32, 'conv_a_b': _jnp.float32, 'w_a_out': _jnp.float32, 'conv_b_w': _jnp.float32, 'conv_b_b': _jnp.float32, 'ln_b_g': _jnp.float32, 'ln_b_b': _jnp.float32, 'w_b_out': _jnp.float32, 'w_o': _jnp.float32, 'norm1_post_g': _jnp.float32, 'norm2_pre_g': _jnp.float32, 'w_mlp_in': _jnp.float32, 'w_mlp_out': _jnp.float32, 'norm2_post_g': _jnp.float32}
MOMENT_SCALE = {'norm1_pre_g': 9.553542e-01, 'w_in': 3.550705e-01, 'b_in': 3.723680e+00, 'conv_a_w': 5.686575e-01, 'conv_a_b': 1.355982e+00, 'w_a_out': 7.956886e-01, 'conv_b_w': 1.101696e+00, 'conv_b_b': 2.039978e+01, 'ln_b_g': 7.623165e+00, 'ln_b_b': 1.123023e+01, 'w_b_out': 4.355844e+00, 'w_o': 4.490123e+00, 'norm1_post_g': 6.461226e+01, 'norm2_pre_g': 1.869936e+00, 'w_mlp_in': 9.559849e-01, 'w_mlp_out': 4.928599e+00, 'norm2_post_g': 6.622637e+01}


def _to_microbatches(a, axis):
    t = _jnp.moveaxis(a, axis, 0)
    t = t.reshape((N_MICROBATCH, t.shape[0] // N_MICROBATCH) + t.shape[1:])
    return _jnp.moveaxis(t, 1, axis + 1)


def setup_inputs(seed: int = 0) -> dict:
    inp = _fwd_setup_inputs(seed)
    key = _jax.random.fold_in(_jax.random.key(seed), 7919)
    shape, _ = _output_shape()
    out = dict(inp)
    out["loss_target"] = _jax.random.normal(_jax.random.fold_in(key, 0), shape, _jnp.float32)
    for i, name in enumerate(TWIN_WEIGHTS):
        w = inp[name].astype(_jnp.float32)
        if MOMENT_SCALE is None:
            s = _jnp.sqrt(_jnp.mean(_jnp.square(w)) + 1e-30)
        else:
            s = MOMENT_SCALE[name]
        km, kv = _jax.random.split(_jax.random.fold_in(key, i + 1))
        out[name] = w
        out["m_" + name] = s * _jax.random.normal(km, w.shape, _jnp.float32)
        out["v_" + name] = (s * s) * _jax.random.uniform(kv, w.shape, _jnp.float32, 0.5, 1.5)
    if N_MICROBATCH > 1:
        for name, axis in PER_EXAMPLE_BATCH_AXIS.items():
            out[name] = _to_microbatches(out[name], axis)
    return {'x': out['x'], 'norm1_pre_g': out['norm1_pre_g'], 'w_in': out['w_in'], 'b_in': out['b_in'], 'conv_a_w': out['conv_a_w'], 'conv_a_b': out['conv_a_b'], 'w_a_out': out['w_a_out'], 'conv_b_w': out['conv_b_w'], 'conv_b_b': out['conv_b_b'], 'ln_b_g': out['ln_b_g'], 'ln_b_b': out['ln_b_b'], 'w_b_out': out['w_b_out'], 'w_o': out['w_o'], 'norm1_post_g': out['norm1_post_g'], 'norm2_pre_g': out['norm2_pre_g'], 'w_mlp_in': out['w_mlp_in'], 'w_mlp_out': out['w_mlp_out'], 'norm2_post_g': out['norm2_post_g'], 'loss_target': out['loss_target'], 'm_norm1_pre_g': out['m_norm1_pre_g'], 'm_w_in': out['m_w_in'], 'm_b_in': out['m_b_in'], 'm_conv_a_w': out['m_conv_a_w'], 'm_conv_a_b': out['m_conv_a_b'], 'm_w_a_out': out['m_w_a_out'], 'm_conv_b_w': out['m_conv_b_w'], 'm_conv_b_b': out['m_conv_b_b'], 'm_ln_b_g': out['m_ln_b_g'], 'm_ln_b_b': out['m_ln_b_b'], 'm_w_b_out': out['m_w_b_out'], 'm_w_o': out['m_w_o'], 'm_norm1_post_g': out['m_norm1_post_g'], 'm_norm2_pre_g': out['m_norm2_pre_g'], 'm_w_mlp_in': out['m_w_mlp_in'], 'm_w_mlp_out': out['m_w_mlp_out'], 'm_norm2_post_g': out['m_norm2_post_g'], 'v_norm1_pre_g': out['v_norm1_pre_g'], 'v_w_in': out['v_w_in'], 'v_b_in': out['v_b_in'], 'v_conv_a_w': out['v_conv_a_w'], 'v_conv_a_b': out['v_conv_a_b'], 'v_w_a_out': out['v_w_a_out'], 'v_conv_b_w': out['v_conv_b_w'], 'v_conv_b_b': out['v_conv_b_b'], 'v_ln_b_g': out['v_ln_b_g'], 'v_ln_b_b': out['v_ln_b_b'], 'v_w_b_out': out['v_w_b_out'], 'v_w_o': out['v_w_o'], 'v_norm1_post_g': out['v_norm1_post_g'], 'v_norm2_pre_g': out['v_norm2_pre_g'], 'v_w_mlp_in': out['v_w_mlp_in'], 'v_w_mlp_out': out['v_w_mlp_out'], 'v_norm2_post_g': out['v_norm2_post_g']}


def _loss(weights, diff, rest, loss_target):
    with _jax.named_scope("forward"):
        args = {**rest, TWIN_DIFF_INPUT: diff, **{k: w.astype(_WEIGHT_DTYPES[k]) for k, w in weights.items()}}
        y = _forward(args)
    with _jax.named_scope("loss_head"):
        err = _jnp.square(y.astype(_jnp.float32) - loss_target)
        return 0.5 * _jnp.sum(_jnp.mean(err, axis=-1)) if err.ndim else 0.5 * err


def _adamw(w, g, m, v):
    m = ADAM_B1 * m + (1.0 - ADAM_B1) * g
    v = ADAM_B2 * v + (1.0 - ADAM_B2) * _jnp.square(g)
    m_hat = m / (1.0 - ADAM_B1 ** ADAM_STEP)
    v_hat = v / (1.0 - ADAM_B2 ** ADAM_STEP)
    delta = -ADAM_LR * (m_hat / (_jnp.sqrt(v_hat) + ADAM_EPS) + ADAM_WD * w)
    return delta, m, v


def reference(x, norm1_pre_g, w_in, b_in, conv_a_w, conv_a_b, w_a_out, conv_b_w, conv_b_b, ln_b_g, ln_b_b, w_b_out, w_o, norm1_post_g, norm2_pre_g, w_mlp_in, w_mlp_out, norm2_post_g, loss_target, m_norm1_pre_g, m_w_in, m_b_in, m_conv_a_w, m_conv_a_b, m_w_a_out, m_conv_b_w, m_conv_b_b, m_ln_b_g, m_ln_b_b, m_w_b_out, m_w_o, m_norm1_post_g, m_norm2_pre_g, m_w_mlp_in, m_w_mlp_out, m_norm2_post_g, v_norm1_pre_g, v_w_in, v_b_in, v_conv_a_w, v_conv_a_b, v_w_a_out, v_conv_b_w, v_conv_b_b, v_ln_b_g, v_ln_b_b, v_w_b_out, v_w_o, v_norm1_post_g, v_norm2_pre_g, v_w_mlp_in, v_w_mlp_out, v_norm2_post_g):
    given = dict(x=x, norm1_pre_g=norm1_pre_g, w_in=w_in, b_in=b_in, conv_a_w=conv_a_w, conv_a_b=conv_a_b, w_a_out=w_a_out, conv_b_w=conv_b_w, conv_b_b=conv_b_b, ln_b_g=ln_b_g, ln_b_b=ln_b_b, w_b_out=w_b_out, w_o=w_o, norm1_post_g=norm1_post_g, norm2_pre_g=norm2_pre_g, w_mlp_in=w_mlp_in, w_mlp_out=w_mlp_out, norm2_post_g=norm2_post_g, loss_target=loss_target, m_norm1_pre_g=m_norm1_pre_g, m_w_in=m_w_in, m_b_in=m_b_in, m_conv_a_w=m_conv_a_w, m_conv_a_b=m_conv_a_b, m_w_a_out=m_w_a_out, m_conv_b_w=m_conv_b_w, m_conv_b_b=m_conv_b_b, m_ln_b_g=m_ln_b_g, m_ln_b_b=m_ln_b_b, m_w_b_out=m_w_b_out, m_w_o=m_w_o, m_norm1_post_g=m_norm1_post_g, m_norm2_pre_g=m_norm2_pre_g, m_w_mlp_in=m_w_mlp_in, m_w_mlp_out=m_w_mlp_out, m_norm2_post_g=m_norm2_post_g, v_norm1_pre_g=v_norm1_pre_g, v_w_in=v_w_in, v_b_in=v_b_in, v_conv_a_w=v_conv_a_w, v_conv_a_b=v_conv_a_b, v_w_a_out=v_w_a_out, v_conv_b_w=v_conv_b_w, v_conv_b_b=v_conv_b_b, v_ln_b_g=v_ln_b_g, v_ln_b_b=v_ln_b_b, v_w_b_out=v_w_b_out, v_w_o=v_w_o, v_norm1_post_g=v_norm1_post_g, v_norm2_pre_g=v_norm2_pre_g, v_w_mlp_in=v_w_mlp_in, v_w_mlp_out=v_w_mlp_out, v_norm2_post_g=v_norm2_post_g)
    weights = {n: given[n] for n in TWIN_WEIGHTS}
    shared = {n: given[n] for n in SHARED_INPUTS}
    per_example = {n: given[n] for n in ['x']}
    grad_fn = _jax.value_and_grad(_loss, argnums=(0, 1))

    def one_microbatch(ex, loss_target):
        ex = dict(ex)
        diff = ex.pop(TWIN_DIFF_INPUT)
        return grad_fn(weights, diff, {**shared, **ex}, loss_target)

    if N_MICROBATCH == 1:
        loss, (grad_w, grad_x) = one_microbatch(per_example, given["loss_target"])
    else:
        def body(carry, xs):
            loss_sum, grad_sum = carry
            l_k, (gw_k, gx_k) = one_microbatch(xs[0], xs[1])
            with _jax.named_scope("update"):
                return (loss_sum + l_k, _jax.tree.map(_jnp.add, grad_sum, gw_k)), gx_k

        init = (_jnp.zeros((), _jnp.float32), _jax.tree.map(_jnp.zeros_like, weights))
        (loss, grad_w), grad_x = _jax.lax.scan(body, init, (per_example, given["loss_target"]))
    with _jax.named_scope("update"):
        delta_w, new_m, new_v = {}, {}, {}
        for n in TWIN_WEIGHTS:
            delta_w[n], new_m[n], new_v[n] = _adamw(weights[n], grad_w[n], given["m_" + n], given["v_" + n])
    return (loss, grad_x, *[grad_w[n] for n in TWIN_WEIGHTS], *[delta_w[n] for n in TWIN_WEIGHTS],
            *[new_m[n] for n in TWIN_WEIGHTS], *[new_v[n] for n in TWIN_WEIGHTS])
```

```python
import functools

import jax
import jax.numpy as jnp
from jax import lax
from jax.experimental import pallas as pl
from jax.experimental.pallas import tpu as pltpu

RMS_EPS = 1e-6
LN_EPS = 1e-5
ADAM_LR = 0.001
ADAM_B1 = 0.9
ADAM_B2 = 0.999
ADAM_EPS = 1e-08
ADAM_WD = 0.01
ADAM_STEP = 10

F32 = jnp.float32
BF16 = jnp.bfloat16
MESH = pl.DeviceIdType.MESH
ANY = pl.BlockSpec(memory_space=pl.ANY)
VMEM_FULL = pl.BlockSpec(memory_space=pltpu.VMEM)

V7X_VMEM_BYTES = 64 * 1024 * 1024
VMEM_LIMIT = V7X_VMEM_BYTES - 8 * 1024 * 1024
SUBLANES = 8
N_CHIPS = 4
N_DEV = 8
HALO_A = 8
HALO_B = 16
CONV_ROWS = 16
ROW_CHUNK = 32

NT_DIMS = (((1,), (1,)), ((), ()))
TN_DIMS = (((0,), (0,)), ((), ()))


def _params(*sem):
    return pltpu.CompilerParams(dimension_semantics=sem, vmem_limit_bytes=VMEM_LIMIT)


def _rows(tm, d):
    return pl.BlockSpec((tm, d), lambda i: (i, 0))


def _const(shape):
    return pl.BlockSpec(shape, lambda i: (0,) * len(shape))


def _halo_prev(tm, hb, d):
    return pl.BlockSpec((hb, d), lambda i: (jnp.maximum(i * (tm // hb) - 1, 0), 0))


def _halo_next(tm, hb, d, t):
    return pl.BlockSpec((hb, d), lambda i: (jnp.minimum((i + 1) * (tm // hb), t // hb - 1), 0))


def _for_chunks(n_rows, rc, fn):
    def body(ci, carry):
        fn(pl.ds(pl.multiple_of(ci * rc, rc), rc))
        return carry
    lax.fori_loop(0, n_rows // rc, body, 0)


def _fold8(v):
    return v.reshape(v.shape[0] // SUBLANES, SUBLANES, v.shape[1]).sum(axis=0)


def _mean_lanes(v):
    return jnp.mean(v, axis=-1, keepdims=True)


def _load_blocks_once(w_hbm, w_vmem, sem):
    nb, _, n = w_hbm.shape

    @pl.when(pl.program_id(0) == 0)
    def _():
        copies = [pltpu.make_async_copy(w_hbm.at[j], w_vmem.at[:, pl.ds(j * n, n)], sem.at[j])
                  for j in range(nb)]
        for cp in copies:
            cp.start()
        for cp in copies:
            cp.wait()


def _load_once(w_hbm, w_vmem, sem):
    @pl.when(pl.program_id(0) == 0)
    def _():
        cp = pltpu.make_async_copy(w_hbm, w_vmem, sem)
        cp.start()
        cp.wait()


def _write_row_sums(acc_ref, out_ref, n_steps):
    @pl.when(pl.program_id(0) == n_steps - 1)
    def _():
        out_ref[...] = jnp.sum(acc_ref[...], axis=0, keepdims=True)


def _cast_bf16(w, name):
    r, c = w.shape
    tr = min(r, 256)

    def body(w_ref, o_ref):
        o_ref[...] = w_ref[...].astype(BF16)

    return pl.pallas_call(
        body, name=name, grid=(r // tr,), in_specs=[_rows(tr, c)], out_specs=_rows(tr, c),
        out_shape=jax.ShapeDtypeStruct((r, c), BF16), compiler_params=_params("parallel"))(w)


def _proj_call(x, g1pre, w_in_g, b_in, tm):
    t, d = x.shape
    nb, _, n4 = w_in_g.shape
    ni = nb * n4
    assert ni == 7 * d

    def body(x_ref, g_ref, b_ref, w_hbm, h_ref, ua_ref, ub_ref, bg_ref, cg_ref, ha_ref, a_ref,
             sg_ref, sa_ref, sb_ref, w_v, p0, p1, sem):
        _load_blocks_once(w_hbm, w_v, sem)

        def norm(rows):
            xv = x_ref[rows, :]
            r = lax.rsqrt(_mean_lanes(xv * xv) + RMS_EPS)
            h_ref[rows, :] = (xv * r * g_ref[...]).astype(BF16)
        _for_chunks(tm, ROW_CHUNK, norm)

        def group(i, dst):
            cols = pl.ds(i * d, d)
            dst[...] = jnp.dot(h_ref[...], w_v[:, cols], preferred_element_type=F32) + b_ref[:, cols]

        group(0, p0)

        def bgate(rows):
            bg_ref[rows, :] = p0[rows, :].astype(BF16)
        _for_chunks(tm, ROW_CHUNK, bgate)

        group(1, p0)
        group(2, p1)

        def branch_a(rows):
            cg, ha = p0[rows, :], p1[rows, :]
            ua_ref[rows, :] = cg * ha
            cg_ref[rows, :] = cg.astype(BF16)
            ha_ref[rows, :] = ha.astype(BF16)
        _for_chunks(tm, ROW_CHUNK, branch_a)

        group(3, p0)
        group(4, p1)

        def branch_b(rows):
            a, sg = p0[rows, :], jax.nn.sigmoid(p1[rows, :])
            ub_ref[rows, :] = a * sg
            a_ref[rows, :] = a.astype(BF16)
            sg_ref[rows, :] = sg.astype(BF16)
        _for_chunks(tm, ROW_CHUNK, branch_b)

        group(5, p0)
        group(6, p1)

        def gates(rows):
            sa_ref[rows, :] = jax.nn.sigmoid(p0[rows, :]).astype(BF16)
            sb_ref[rows, :] = jax.nn.sigmoid(p1[rows, :]).astype(BF16)
        _for_chunks(tm, ROW_CHUNK, gates)

    bf = jax.ShapeDtypeStruct((t, d), BF16)
    f32 = jax.ShapeDtypeStruct((t, d), F32)
    return pl.pallas_call(
        body, name="proj_fwd", grid=(t // tm,),
        in_specs=[_rows(tm, d), _const((1, d)), _const((1, ni)), ANY],
        out_specs=[_rows(tm, d)] * 10,
        out_shape=[bf, f32, f32, bf, bf, bf, bf, bf, bf, bf],
        scratch_shapes=[pltpu.VMEM((d, ni), BF16), pltpu.VMEM((tm, d), F32), pltpu.VMEM((tm, d), F32),
                        pltpu.SemaphoreType.DMA((nb,))],
        compiler_params=_params("arbitrary"))(x, g1pre, b_in, w_in_g)


def _fill_ext(ext, prev_ref, cur_ref, next_ref, hb, tm, i, n_steps):
    ext[pl.ds(0, hb), :] = jnp.where(i > 0, prev_ref[...], 0.0)
    ext[pl.ds(hb, tm), :] = cur_ref[...]
    ext[pl.ds(hb + tm, hb), :] = jnp.where(i < n_steps - 1, next_ref[...], 0.0)


def _conv_taps(ext, w_ref, r0, first_offset, step, n_taps, init):
    acc = init
    for k in range(n_taps):
        acc = acc + w_ref[k:k + 1, :] * ext[pl.ds(r0 + first_offset + step * k, CONV_ROWS), :]
    return acc


def _mixer_fwd_call(ua, ub, bg, sa, sb, x, conv_a_w, conv_a_b, conv_b_w, conv_b_b, ln_g, ln_b,
                    w_a, w_b, w_o, g1post, tm):
    t, d = x.shape
    n_steps = t // tm
    ka, kb = conv_a_w.shape[0], conv_b_w.shape[0]

    def body(uap, uac, uan, ubp, ubc, ubn, bg_ref, sa_ref, sb_ref, x_ref, wa_c, ba_c, wb_c, bb_c,
             lng, lnb, wa_hbm, wb_hbm, wo_hbm, g_ref,
             x1_ref, va_ref, pa_ref, cb_ref, sbo_ref, ya_ref, yb_ref, mg_ref, mix_ref,
             ext_a, ext_b, wa_v, wb_v, wo_v, y0, y1, sem):
        i = pl.program_id(0)
        _load_once(wa_hbm, wa_v, sem.at[0])
        _load_once(wb_hbm, wb_v, sem.at[1])
        _load_once(wo_hbm, wo_v, sem.at[2])
        _fill_ext(ext_a, uap, uac, uan, HALO_A, tm, i, n_steps)
        _fill_ext(ext_b, ubp, ubc, ubn, HALO_B, tm, i, n_steps)

        for r0 in range(0, tm, CONV_ROWS):
            rows = pl.ds(r0, CONV_ROWS)
            va = _conv_taps(ext_a, wa_c, r0, HALO_A - (ka - 1) // 2, 1, ka,
                            jnp.broadcast_to(ba_c[...], (CONV_ROWS, d)))
            va_ref[rows, :] = va.astype(BF16)
            pa_ref[rows, :] = (bg_ref[rows, :].astype(F32) * va).astype(BF16)
            cb = _conv_taps(ext_b, wb_c, r0, HALO_B - (kb - 1) // 2, 1, kb,
                            jnp.broadcast_to(bb_c[...], (CONV_ROWS, d)))
            cb_ref[rows, :] = cb
            mu = _mean_lanes(cb)
            cen = cb - mu
            rstd = lax.rsqrt(_mean_lanes(cen * cen) + LN_EPS)
            ln = cen * rstd * lng[...] + lnb[...]
            sbo_ref[rows, :] = (ln * jax.nn.sigmoid(ln)).astype(BF16)

        y0[...] = jnp.dot(pa_ref[...], wa_v[...], preferred_element_type=F32)
        y1[...] = jnp.dot(sbo_ref[...], wb_v[...], preferred_element_type=F32)

        def merge(rows):
            ya, yb = y0[rows, :], y1[rows, :]
            ya_ref[rows, :] = ya.astype(BF16)
            yb_ref[rows, :] = yb.astype(BF16)
            mg_ref[rows, :] = (sa_ref[rows, :].astype(F32) * ya + sb_ref[rows, :].astype(F32) * yb).astype(BF16)
        _for_chunks(tm, ROW_CHUNK, merge)

        mix_ref[...] = jnp.dot(mg_ref[...], wo_v[...], preferred_element_type=F32)

        def resid(rows):
            mix = mix_ref[rows, :]
            r = lax.rsqrt(_mean_lanes(mix * mix) + RMS_EPS)
            x1_ref[rows, :] = x_ref[rows, :] + mix * r * g_ref[...]
        _for_chunks(tm, ROW_CHUNK, resid)

    bf = jax.ShapeDtypeStruct((t, d), BF16)
    f32 = jax.ShapeDtypeStruct((t, d), F32)
    return pl.pallas_call(
        body, name="mixer_fwd", grid=(n_steps,),
        in_specs=[_halo_prev(tm, HALO_A, d), _rows(tm, d), _halo_next(tm, HALO_A, d, t),
                  _halo_prev(tm, HALO_B, d), _rows(tm, d), _halo_next(tm, HALO_B, d, t),
                  _rows(tm, d), _rows(tm, d), _rows(tm, d), _rows(tm, d),
                  _const((ka, d)), _const((1, d)), _const((kb, d)), _const((1, d)),
                  _const((1, d)), _const((1, d)), ANY, ANY, ANY, _const((1, d))],
        out_specs=[_rows(tm, d)] * 9,
        out_shape=[f32, bf, bf, f32, bf, bf, bf, bf, f32],
        scratch_shapes=[pltpu.VMEM((tm + 2 * HALO_A, d), F32), pltpu.VMEM((tm + 2 * HALO_B, d), F32),
                        pltpu.VMEM((d, d), BF16), pltpu.VMEM((d, d), BF16), pltpu.VMEM((d, d), BF16),
                        pltpu.VMEM((tm, d), F32), pltpu.VMEM((tm, d), F32),
                        pltpu.SemaphoreType.DMA((3,))],
        compiler_params=_params("arbitrary"))(
            ua, ua, ua, ub, ub, ub, bg, sa, sb, x, conv_a_w, conv_a_b, conv_b_w, conv_b_b,
            ln_g, ln_b, w_a, w_b, w_o, g1post)


def _mlp_call(x1, target, g2pre, g2post, w1_g, w2, tm):
    t, d = x1.shape
    nb, _, fq = w1_g.shape
    f = nb * fq
    n_steps = t // tm
    inv_d = 1.0 / d

    def body(x1_ref, t_ref, gpre, gpost, w1_hbm, w2_hbm,
             dx1_ref, f_ref, df2_ref, h2_ref, df1_ref, dgpost_ref, dgpre_ref, loss_ref,
             w1_v, w2_v, f1_s, blk_s, f2_s, acc_post, acc_pre, acc_loss, sem):
        _load_blocks_once(w1_hbm, w1_v, sem)
        _load_once(w2_hbm, w2_v, sem.at[nb])

        @pl.when(pl.program_id(0) == 0)
        def _():
            acc_post[...] = jnp.zeros_like(acc_post)
            acc_pre[...] = jnp.zeros_like(acc_pre)
            acc_loss[...] = jnp.zeros_like(acc_loss)

        def norm(rows):
            xv = x1_ref[rows, :]
            r = lax.rsqrt(_mean_lanes(xv * xv) + RMS_EPS)
            h2_ref[rows, :] = (xv * r * gpre[...]).astype(BF16)
        _for_chunks(tm, ROW_CHUNK, norm)

        for j in range(nb):
            cols = pl.ds(j * fq, fq)
            f1_s[:, cols] = jnp.dot(h2_ref[...], w1_v[:, cols], preferred_element_type=F32)

        def act(rows):
            relu = jnp.maximum(f1_s[rows, :], 0.0)
            f_ref[rows, :] = (relu * relu).astype(BF16)
        _for_chunks(tm, ROW_CHUNK, act)

        f2_s[...] = jnp.dot(f_ref[...], w2_v[...], preferred_element_type=F32)

        def head(rows):
            f2 = f2_s[rows, :]
            rf = lax.rsqrt(_mean_lanes(f2 * f2) + RMS_EPS)
            y = x1_ref[rows, :] + f2 * rf * gpost[...]
            err = y - t_ref[rows, :]
            acc_loss[...] += _fold8(err * err)
            dy = err * inv_d
            gdy = dy * gpost[...]
            df2 = rf * gdy - f2 * (rf * rf * rf * _mean_lanes(gdy * f2))
            df2_ref[rows, :] = df2.astype(BF16)
            acc_post[...] += _fold8(dy * f2 * rf)
            dx1_ref[rows, :] = dy
        _for_chunks(tm, ROW_CHUNK, head)

        for j in range(nb):
            cols = pl.ds(j * fq, fq)
            blk_s[...] = lax.dot_general(df2_ref[...], w2_v[cols, :], NT_DIMS, preferred_element_type=F32)

            def dact(rows):
                relu = jnp.maximum(f1_s[rows, cols], 0.0)
                df1_ref[rows, cols] = (blk_s[rows, :] * (2.0 * relu)).astype(BF16)
            _for_chunks(tm, ROW_CHUNK, dact)

        f2_s[...] = lax.dot_general(df1_ref[...], w1_v[...], NT_DIMS, preferred_element_type=F32)

        def dnorm(rows):
            dh2 = f2_s[rows, :]
            xv = x1_ref[rows, :]
            r = lax.rsqrt(_mean_lanes(xv * xv) + RMS_EPS)
            gd = dh2 * gpre[...]
            dx1_ref[rows, :] = dx1_ref[rows, :] + r * gd - xv * (r * r * r * _mean_lanes(gd * xv))
            acc_pre[...] += _fold8(dh2 * xv * r)
        _for_chunks(tm, ROW_CHUNK, dnorm)

        _write_row_sums(acc_post, dgpost_ref, n_steps)
        _write_row_sums(acc_pre, dgpre_ref, n_steps)
        _write_row_sums(acc_loss, loss_ref, n_steps)

    row = jax.ShapeDtypeStruct((1, d), F32)
    return pl.pallas_call(
        body, name="mlp_fwd_bwd", grid=(n_steps,),
        in_specs=[_rows(tm, d), _rows(tm, d), _const((1, d)), _const((1, d)), ANY, ANY],
        out_specs=[_rows(tm, d), _rows(tm, f), _rows(tm, d), _rows(tm, d), _rows(tm, f),
                   _const((1, d)), _const((1, d)), _const((1, d))],
        out_shape=[jax.ShapeDtypeStruct((t, d), F32), jax.ShapeDtypeStruct((t, f), BF16),
                   jax.ShapeDtypeStruct((t, d), BF16), jax.ShapeDtypeStruct((t, d), BF16),
                   jax.ShapeDtypeStruct((t, f), BF16), row, row, row],
        scratch_shapes=[pltpu.VMEM((d, f), BF16), pltpu.VMEM((f, d), BF16),
                        pltpu.VMEM((tm, f), F32), pltpu.VMEM((tm, fq), F32), pltpu.VMEM((tm, d), F32),
                        pltpu.VMEM((SUBLANES, d), F32), pltpu.VMEM((SUBLANES, d), F32),
                        pltpu.VMEM((SUBLANES, d), F32), pltpu.SemaphoreType.DMA((nb + 1,))],
        compiler_params=_params("arbitrary"))(x1, target, g2pre, g2post, w1_g, w2)


def _mixer_bwd_call(dx1, mix, sa, sb, ya, yb, bg, va, cb, g1post, ln_g, ln_b, w_a, w_b, w_o, tm):
    t, d = dx1.shape
    n_steps = t // tm

    def body(dx1_ref, mix_ref, sa_ref, sb_ref, ya_ref, yb_ref, bg_ref, va_ref, cb_ref, g_ref, lng, lnb,
             wa_hbm, wb_hbm, wo_hbm,
             dmix_ref, dya_ref, dyb_ref, dva_ref, dcb_ref, dbg_ref, dza_ref, dzb_ref,
             dg_ref, dlng_ref, dlnb_ref, dba_ref, dbb_ref,
             wa_v, wb_v, wo_v, s0, s1, acc_g, acc_lng, acc_lnb, acc_ba, acc_bb, sem):
        _load_once(wa_hbm, wa_v, sem.at[0])
        _load_once(wb_hbm, wb_v, sem.at[1])
        _load_once(wo_hbm, wo_v, sem.at[2])
        accs = (acc_g, acc_lng, acc_lnb, acc_ba, acc_bb)

        @pl.when(pl.program_id(0) == 0)
        def _():
            for acc in accs:
                acc[...] = jnp.zeros_like(acc)

        def dnorm(rows):
            mix = mix_ref[rows, :]
            dxv = dx1_ref[rows, :]
            r = lax.rsqrt(_mean_lanes(mix * mix) + RMS_EPS)
            gd = dxv * g_ref[...]
            dmix_ref[rows, :] = (r * gd - mix * (r * r * r * _mean_lanes(gd * mix))).astype(BF16)
            acc_g[...] += _fold8(dxv * mix * r)
        _for_chunks(tm, ROW_CHUNK, dnorm)

        s0[...] = lax.dot_general(dmix_ref[...], wo_v[...], NT_DIMS, preferred_element_type=F32)

        def dmerge(rows):
            dm = s0[rows, :]
            sav, sbv = sa_ref[rows, :].astype(F32), sb_ref[rows, :].astype(F32)
            dya_ref[rows, :] = (dm * sav).astype(BF16)
            dyb_ref[rows, :] = (dm * sbv).astype(BF16)
            dza_ref[rows, :] = (dm * ya_ref[rows, :].astype(F32) * sav * (1.0 - sav)).astype(BF16)
            dzb_ref[rows, :] = (dm * yb_ref[rows, :].astype(F32) * sbv * (1.0 - sbv)).astype(BF16)
        _for_chunks(tm, ROW_CHUNK, dmerge)

        s0[...] = lax.dot_general(dya_ref[...], wa_v[...], NT_DIMS, preferred_element_type=F32)
        s1[...] = lax.dot_general(dyb_ref[...], wb_v[...], NT_DIMS, preferred_element_type=F32)

        def dbranches(rows):
            dpa = s0[rows, :]
            dbg_ref[rows, :] = (dpa * va_ref[rows, :].astype(F32)).astype(BF16)
            dva = dpa * bg_ref[rows, :].astype(F32)
            dva_ref[rows, :] = dva
            acc_ba[...] += _fold8(dva)
            cbv = cb_ref[rows, :]
            mu = _mean_lanes(cbv)
            cen = cbv - mu
            rstd = lax.rsqrt(_mean_lanes(cen * cen) + LN_EPS)
            xhat = cen * rstd
            ln = xhat * lng[...] + lnb[...]
            sig = jax.nn.sigmoid(ln)
            dln = s1[rows, :] * (sig * (1.0 + ln * (1.0 - sig)))
            acc_lng[...] += _fold8(dln * xhat)
            acc_lnb[...] += _fold8(dln)
            dxh = dln * lng[...]
            dcb = rstd * (dxh - _mean_lanes(dxh) - xhat * _mean_lanes(dxh * xhat))
            dcb_ref[rows, :] = dcb
            acc_bb[...] += _fold8(dcb)
        _for_chunks(tm, ROW_CHUNK, dbranches)

        _write_row_sums(acc_g, dg_ref, n_steps)
        _write_row_sums(acc_lng, dlng_ref, n_steps)
        _write_row_sums(acc_lnb, dlnb_ref, n_steps)
        _write_row_sums(acc_ba, dba_ref, n_steps)
        _write_row_sums(acc_bb, dbb_ref, n_steps)

    bf = jax.ShapeDtypeStruct((t, d), BF16)
    f32 = jax.ShapeDtypeStruct((t, d), F32)
    row = jax.ShapeDtypeStruct((1, d), F32)
    return pl.pallas_call(
        body, name="mixer_bwd", grid=(n_steps,),
        in_specs=[_rows(tm, d)] * 9 + [_const((1, d))] * 3 + [ANY, ANY, ANY],
        out_specs=[_rows(tm, d)] * 8 + [_const((1, d))] * 5,
        out_shape=[bf, bf, bf, f32, f32, bf, bf, bf, row, row, row, row, row],
        scratch_shapes=[pltpu.VMEM((d, d), BF16), pltpu.VMEM((d, d), BF16), pltpu.VMEM((d, d), BF16),
                        pltpu.VMEM((tm, d), F32), pltpu.VMEM((tm, d), F32)]
        + [pltpu.VMEM((SUBLANES, d), F32)] * 5 + [pltpu.SemaphoreType.DMA((3,))],
        compiler_params=_params("arbitrary"))(
            dx1, mix, sa, sb, ya, yb, bg, va, cb, g1post, ln_g, ln_b, w_a, w_b, w_o)


def _conv_bwd_call(dva, dcb, ua, ub, cg, ha, a, sg, dbg, dza, dzb, conv_a_w, conv_b_w, tm):
    t, d = dva.shape
    n_steps = t // tm
    ka, kb = conv_a_w.shape[0], conv_b_w.shape[0]
    pa, pb = (ka - 1) // 2, (kb - 1) // 2

    def body(dvap, dvac, dvan, dcbp, dcbc, dcbn, uap, uac, uan, ubp, ubc, ubn,
             cg_ref, ha_ref, a_ref, sg_ref, dbg_ref, dza_ref, dzb_ref, wa_c, wb_c,
             dproj_ref, dwa_ref, dwb_ref, dbin_ref,
             e_dva, e_dcb, e_ua, e_ub, acc_wa, acc_wb, acc_bin):
        i = pl.program_id(0)

        @pl.when(i == 0)
        def _():
            acc_wa[...] = jnp.zeros_like(acc_wa)
            acc_wb[...] = jnp.zeros_like(acc_wb)
            acc_bin[...] = jnp.zeros_like(acc_bin)

        _fill_ext(e_dva, dvap, dvac, dvan, HALO_A, tm, i, n_steps)
        _fill_ext(e_dcb, dcbp, dcbc, dcbn, HALO_B, tm, i, n_steps)
        _fill_ext(e_ua, uap, uac, uan, HALO_A, tm, i, n_steps)
        _fill_ext(e_ub, ubp, ubc, ubn, HALO_B, tm, i, n_steps)

        def put(col, rows, val_f32):
            dproj_ref[rows, pl.ds(col * d, d)] = val_f32.astype(BF16)
            acc_bin[:, pl.ds(col * d, d)] += _fold8(val_f32)

        zero = jnp.zeros((CONV_ROWS, d), F32)
        for r0 in range(0, tm, CONV_ROWS):
            rows = pl.ds(r0, CONV_ROWS)
            dua = _conv_taps(e_dva, wa_c, r0, HALO_A + pa, -1, ka, zero)
            dub = _conv_taps(e_dcb, wb_c, r0, HALO_B + pb, -1, kb, zero)
            dva_c = dvac[rows, :]
            dcb_c = dcbc[rows, :]
            for k in range(ka):
                acc_wa[pl.ds(k * SUBLANES, SUBLANES), :] += _fold8(
                    dva_c * e_ua[pl.ds(r0 + HALO_A - pa + k, CONV_ROWS), :])
            for k in range(kb):
                acc_wb[pl.ds(k * SUBLANES, SUBLANES), :] += _fold8(
                    dcb_c * e_ub[pl.ds(r0 + HALO_B - pb + k, CONV_ROWS), :])
            cgv, hav = cg_ref[rows, :].astype(F32), ha_ref[rows, :].astype(F32)
            av, sgv = a_ref[rows, :].astype(F32), sg_ref[rows, :].astype(F32)
            put(0, rows, dbg_ref[rows, :].astype(F32))
            put(1, rows, dua * hav)
            put(2, rows, dua * cgv)
            put(3, rows, dub * sgv)
            put(4, rows, dub * av * sgv * (1.0 - sgv))
            put(5, rows, dza_ref[rows, :].astype(F32))
            put(6, rows, dzb_ref[rows, :].astype(F32))

        @pl.when(i == n_steps - 1)
        def _():
            for k in range(ka):
                dwa_ref[k:k + 1, :] = jnp.sum(acc_wa[pl.ds(k * SUBLANES, SUBLANES), :], axis=0, keepdims=True)
            for k in range(kb):
                dwb_ref[k:k + 1, :] = jnp.sum(acc_wb[pl.ds(k * SUBLANES, SUBLANES), :], axis=0, keepdims=True)
            dbin_ref[...] = jnp.sum(acc_bin[...], axis=0, keepdims=True)

    halo_a = [_halo_prev(tm, HALO_A, d), _rows(tm, d), _halo_next(tm, HALO_A, d, t)]
    halo_b = [_halo_prev(tm, HALO_B, d), _rows(tm, d), _halo_next(tm, HALO_B, d, t)]
    return pl.pallas_call(
        body, name="conv_bwd", grid=(n_steps,),
        in_specs=halo_a + halo_b + halo_a + halo_b + [_rows(tm, d)] * 7 + [_const((ka, d)), _const((kb, d))],
        out_specs=[_rows(tm, 7 * d), _const((ka, d)), _const((kb, d)), _const((1, 7 * d))],
        out_shape=[jax.ShapeDtypeStruct((t, 7 * d), BF16), jax.ShapeDtypeStruct((ka, d), F32),
                   jax.ShapeDtypeStruct((kb, d), F32), jax.ShapeDtypeStruct((1, 7 * d), F32)],
        scratch_shapes=[pltpu.VMEM((tm + 2 * HALO_A, d), F32), pltpu.VMEM((tm + 2 * HALO_B, d), F32),
                        pltpu.VMEM((tm + 2 * HALO_A, d), F32), pltpu.VMEM((tm + 2 * HALO_B, d), F32),
                        pltpu.VMEM((ka * SUBLANES, d), F32), pltpu.VMEM((kb * SUBLANES, d), F32),
                        pltpu.VMEM((SUBLANES, 7 * d), F32)],
        compiler_params=_params("arbitrary"))(
            dva, dva, dva, dcb, dcb, dcb, ua, ua, ua, ub, ub, ub, cg, ha, a, sg, dbg, dza, dzb,
            conv_a_w, conv_b_w)


def _dx_call(dproj, x, dx1, g1pre, w_in_g, tm):
    t, d = x.shape
    nb, _, n4 = w_in_g.shape
    ni = nb * n4
    n_steps = t // tm

    def body(dp_ref, x_ref, dx1_ref, g_ref, w_hbm, dx_ref, dg_ref, w_v, dh_s, acc_g, sem):
        _load_blocks_once(w_hbm, w_v, sem)

        @pl.when(pl.program_id(0) == 0)
        def _():
            acc_g[...] = jnp.zeros_like(acc_g)

        dh_s[...] = lax.dot_general(dp_ref[...], w_v[...], NT_DIMS, preferred_element_type=F32)

        def dnorm(rows):
            dh = dh_s[rows, :]
            xv = x_ref[rows, :]
            r = lax.rsqrt(_mean_lanes(xv * xv) + RMS_EPS)
            gd = dh * g_ref[...]
            dx_ref[rows, :] = dx1_ref[rows, :] + r * gd - xv * (r * r * r * _mean_lanes(gd * xv))
            acc_g[...] += _fold8(dh * xv * r)
        _for_chunks(tm, ROW_CHUNK, dnorm)
        _write_row_sums(acc_g, dg_ref, n_steps)

    return pl.pallas_call(
        body, name="dx_bwd", grid=(n_steps,),
        in_specs=[_rows(tm, ni), _rows(tm, d), _rows(tm, d), _const((1, d)), ANY],
        out_specs=[_rows(tm, d), _const((1, d))],
        out_shape=[jax.ShapeDtypeStruct((t, d), F32), jax.ShapeDtypeStruct((1, d), F32)],
        scratch_shapes=[pltpu.VMEM((d, ni), BF16), pltpu.VMEM((tm, d), F32),
                        pltpu.VMEM((SUBLANES, d), F32), pltpu.SemaphoreType.DMA((nb,))],
        compiler_params=_params("arbitrary"))(dproj, x, dx1, g1pre, w_in_g)


def _tn_matmul(a, g, nblk, a_cols, g_cols, a_blocked, g_blocked, tt, name):
    t = a.shape[0]

    def body(a_ref, g_ref, o_ref):
        @pl.when(pl.program_id(1) == 0)
        def _():
            o_ref[...] = jnp.zeros_like(o_ref)
        o_ref[0] += lax.dot_general(a_ref[...], g_ref[...], TN_DIMS, preferred_element_type=F32)

    return pl.pallas_call(
        body, name=name, grid=(nblk, t // tt),
        in_specs=[pl.BlockSpec((tt, a_cols), (lambda b, s: (s, b)) if a_blocked else (lambda b, s: (s, 0))),
                  pl.BlockSpec((tt, g_cols), (lambda b, s: (s, b)) if g_blocked else (lambda b, s: (s, 0)))],
        out_specs=pl.BlockSpec((1, a_cols, g_cols), lambda b, s: (b, 0, 0)),
        out_shape=jax.ShapeDtypeStruct((nblk, a_cols, g_cols), F32),
        compiler_params=_params("parallel", "arbitrary"))(a, g)


def _pair_sum_call(g_full, from_sibling, core, name):
    nblk, r, c = g_full.shape
    hr = r // 2
    tr = min(hr, 256)
    n = hr // tr

    def body(core_ref, g_ref, p_ref, o_ref):
        o_ref[...] = g_ref[...] + p_ref[...]

    return pl.pallas_call(
        body, name=name,
        grid_spec=pltpu.PrefetchScalarGridSpec(
            num_scalar_prefetch=1, grid=(nblk, n),
            in_specs=[pl.BlockSpec((1, tr, c), lambda j, i, cr: (j, cr[0] * n + i, 0)),
                      pl.BlockSpec((1, tr, c), lambda j, i, cr: (j, i, 0))],
            out_specs=pl.BlockSpec((1, tr, c), lambda j, i, cr: (j, i, 0))),
        out_shape=jax.ShapeDtypeStruct((nblk, hr, c), F32),
        compiler_params=_params("parallel", "parallel"))(core, g_full, from_sibling)


def _chip_sum_call(pair, received, chip, name):
    _, hr, c = pair.shape
    tr = min(hr, 256)

    def body(chip_ref, own_ref, r_ref, o_ref):
        o_ref[...] = ((own_ref[0] + r_ref[0]) + r_ref[1]) + r_ref[2]

    return pl.pallas_call(
        body, name=name,
        grid_spec=pltpu.PrefetchScalarGridSpec(
            num_scalar_prefetch=1, grid=(hr // tr,),
            in_specs=[pl.BlockSpec((1, tr, c), lambda i, kr: (kr[0], i, 0)),
                      pl.BlockSpec((N_CHIPS - 1, tr, c), lambda i, kr: (0, i, 0))],
            out_specs=pl.BlockSpec((tr, c), lambda i, kr: (i, 0))),
        out_shape=jax.ShapeDtypeStruct((hr, c), F32),
        compiler_params=_params("parallel"))(chip, pair, received)


def _adamw(w, g, m, v):
    m = ADAM_B1 * m + (1.0 - ADAM_B1) * g
    v = ADAM_B2 * v + (1.0 - ADAM_B2) * (g * g)
    m_hat = m / (1.0 - ADAM_B1 ** ADAM_STEP)
    v_hat = v / (1.0 - ADAM_B2 ** ADAM_STEP)
    delta = -ADAM_LR * (m_hat / (jnp.sqrt(v_hat) + ADAM_EPS) + ADAM_WD * w)
    return delta, m, v


def _adam_call(w, g, m, v, name):
    r, c = w.shape
    tr = min(r, 256)

    def body(w_ref, g_ref, m_ref, v_ref, d_ref, mo_ref, vo_ref):
        d_ref[...], mo_ref[...], vo_ref[...] = _adamw(w_ref[...], g_ref[...], m_ref[...], v_ref[...])

    shape = jax.ShapeDtypeStruct((r, c), F32)
    return pl.pallas_call(
        body, name=name, grid=(r // tr,), in_specs=[_rows(tr, c)] * 4, out_specs=[_rows(tr, c)] * 3,
        out_shape=[shape] * 3, compiler_params=_params("parallel"))(w, g, m, v)


def _place():
    return lax.axis_index("x"), lax.axis_index("y"), lax.axis_index("c")


def _other_chips(x, y):
    rel = [(x, 1 - y), (1 - x, y), (1 - x, 1 - y)]
    return [(px, py, 2 * px + py) for px, py in rel]


def _gather_weights_call(shards):
    n = len(shards)

    def body(*refs):
        ins, outs = refs[:n], refs[n:2 * n]
        ici_send, ici_recv, fwd_send, fwd_recv, own_sem = refs[2 * n:]
        x, y, c = _place()
        me = 2 * x + y
        chips = _other_chips(x, y)
        started = []
        own = [pltpu.make_async_copy(ins[a], outs[a].at[me], own_sem.at[a]) for a in range(n)]
        for cp in own:
            cp.start()

        def half(ref, chip, core):
            hr = ref.shape[1] // 2
            return ref.at[chip, pl.ds(core * hr, hr), :]

        for a in range(n):
            hr = ins[a].shape[0] // 2
            for j, (px, py, _) in enumerate(chips):
                cp = pltpu.make_async_remote_copy(
                    src_ref=ins[a].at[pl.ds(c * hr, hr), :], dst_ref=half(outs[a], me, c),
                    send_sem=ici_send.at[a, j], recv_sem=ici_recv.at[a, j],
                    device_id=(px, py, c), device_id_type=MESH)
                cp.start()
                started.append(cp)
        for a in range(n):
            for j, (px, py, pk) in enumerate(chips):
                landed = half(outs[a], pk, c)
                pltpu.make_async_remote_copy(
                    src_ref=landed, dst_ref=landed, send_sem=ici_send.at[a, j], recv_sem=ici_recv.at[a, j],
                    device_id=(px, py, c), device_id_type=MESH).wait_recv()
                cp = pltpu.make_async_remote_copy(
                    src_ref=landed, dst_ref=landed, send_sem=fwd_send.at[a, j], recv_sem=fwd_recv.at[a, j],
                    device_id=(x, y, 1 - c), device_id_type=MESH)
                cp.start()
                started.append(cp)
        for a in range(n):
            for j, (px, py, pk) in enumerate(chips):
                passed = half(outs[a], pk, 1 - c)
                pltpu.make_async_remote_copy(
                    src_ref=passed, dst_ref=passed, send_sem=fwd_send.at[a, j], recv_sem=fwd_recv.at[a, j],
                    device_id=(x, y, 1 - c), device_id_type=MESH).wait_recv()
        for cp in started:
            cp.wait_send()
        for cp in own:
            cp.wait()

    return pl.pallas_call(
        body, name="gather_weights", in_specs=[ANY] * n, out_specs=[ANY] * n,
        out_shape=[jax.ShapeDtypeStruct((N_CHIPS,) + s.shape, s.dtype) for s in shards],
        scratch_shapes=[pltpu.SemaphoreType.DMA((n, N_CHIPS - 1)), pltpu.SemaphoreType.DMA((n, N_CHIPS - 1)),
                        pltpu.SemaphoreType.DMA((n, N_CHIPS - 1)), pltpu.SemaphoreType.DMA((n, N_CHIPS - 1)),
                        pltpu.SemaphoreType.DMA((n,))],
        compiler_params=pltpu.CompilerParams(has_side_effects=True))(*shards)


def _gather_conv_weights(conv_a_w, conv_b_w, d):
    ka, dq = conv_a_w.shape
    kb = conv_b_w.shape[0]
    ra = -(-ka // SUBLANES) * SUBLANES
    rb = -(-kb // SUBLANES) * SUBLANES
    a_pad = jnp.pad(conv_a_w, ((0, ra - ka), (0, 0)))
    b_pad = jnp.pad(conv_b_w, ((0, rb - kb), (0, 0)))

    def body(a_ref, b_ref, oa_ref, ob_ref, pack, slots, send_sem, recv_sem):
        x, y, c = _place()
        me = 2 * x + y
        chips = _other_chips(x, y)
        pack[pl.ds(0, ra), :] = a_ref[...]
        pack[pl.ds(ra, rb), :] = b_ref[...]
        copies = []
        for j, (px, py, _) in enumerate(chips):
            cp = pltpu.make_async_remote_copy(
                src_ref=pack, dst_ref=slots.at[me], send_sem=send_sem.at[j], recv_sem=recv_sem.at[j],
                device_id=(px, py, c), device_id_type=MESH)
            cp.start()
            copies.append(cp)
        for j, (px, py, pk) in enumerate(chips):
            pltpu.make_async_remote_copy(
                src_ref=pack, dst_ref=slots.at[pk], send_sem=send_sem.at[j], recv_sem=recv_sem.at[j],
                device_id=(px, py, c), device_id_type=MESH).wait_recv()
        for cp in copies:
            cp.wait_send()
        slots[me] = pack[...]
        for k in range(N_CHIPS):
            oa_ref[:, pl.ds(k * dq, dq)] = slots[k, pl.ds(0, ra), :]
            ob_ref[:, pl.ds(k * dq, dq)] = slots[k, pl.ds(ra, rb), :]

    oa, ob = pl.pallas_call(
        body, name="gather_conv_weights", in_specs=[VMEM_FULL] * 2, out_specs=[VMEM_FULL] * 2,
        out_shape=[jax.ShapeDtypeStruct((ra, d), F32), jax.ShapeDtypeStruct((rb, d), F32)],
        scratch_shapes=[pltpu.VMEM((ra + rb, dq), F32), pltpu.VMEM((N_CHIPS, ra + rb, dq), F32),
                        pltpu.SemaphoreType.DMA((N_CHIPS - 1,)), pltpu.SemaphoreType.DMA((N_CHIPS - 1,))],
        compiler_params=pltpu.CompilerParams(has_side_effects=True))(a_pad, b_pad)
    return oa[:ka], ob[:kb]


def _sibling_halves_call(grads):
    n = len(grads)

    def body(*refs):
        ins, outs = refs[:n], refs[n:2 * n]
        send_sem, recv_sem = refs[2 * n:]
        x, y, c = _place()
        copies = []
        for a in range(n):
            hr = ins[a].shape[1] // 2
            cp = pltpu.make_async_remote_copy(
                src_ref=ins[a].at[:, pl.ds((1 - c) * hr, hr), :], dst_ref=outs[a],
                send_sem=send_sem.at[a], recv_sem=recv_sem.at[a],
                device_id=(x, y, 1 - c), device_id_type=MESH)
            cp.start()
            copies.append(cp)
        for cp in copies:
            cp.wait()

    return pl.pallas_call(
        body, name="grads_to_sibling", in_specs=[ANY] * n, out_specs=[ANY] * n,
        out_shape=[jax.ShapeDtypeStruct((g.shape[0], g.shape[1] // 2, g.shape[2]), g.dtype) for g in grads],
        scratch_shapes=[pltpu.SemaphoreType.DMA((n,)), pltpu.SemaphoreType.DMA((n,))],
        compiler_params=pltpu.CompilerParams(has_side_effects=True))(*grads)


def _scatter_to_owner_call(pairs):
    n = len(pairs)

    def body(*refs):
        ins, outs = refs[:n], refs[n:2 * n]
        send_sem, recv_sem = refs[2 * n:]
        x, y, c = _place()
        chips = _other_chips(x, y)
        copies = []
        for a in range(n):
            for j, (px, py, pk) in enumerate(chips):
                cp = pltpu.make_async_remote_copy(
                    src_ref=ins[a].at[pk], dst_ref=outs[a].at[j],
                    send_sem=send_sem.at[a, j], recv_sem=recv_sem.at[a, j],
                    device_id=(px, py, c), device_id_type=MESH)
                cp.start()
                copies.append(cp)
        for cp in copies:
            cp.wait()

    return pl.pallas_call(
        body, name="grads_to_owner", in_specs=[ANY] * n, out_specs=[ANY] * n,
        out_shape=[jax.ShapeDtypeStruct((N_CHIPS - 1,) + p.shape[1:], p.dtype) for p in pairs],
        scratch_shapes=[pltpu.SemaphoreType.DMA((n, N_CHIPS - 1)), pltpu.SemaphoreType.DMA((n, N_CHIPS - 1))],
        compiler_params=pltpu.CompilerParams(has_side_effects=True))(*pairs)


def _share_halves_call(halves):
    n = len(halves)

    def body(*refs):
        ins, outs = refs[:n], refs[n:2 * n]
        send_sem, recv_sem, own_sem = refs[2 * n:]
        x, y, c = _place()
        copies, own = [], []
        for a in range(n):
            hr = ins[a].shape[0]
            mine = outs[a].at[pl.ds(c * hr, hr), :]
            cp = pltpu.make_async_copy(ins[a], mine, own_sem.at[a])
            cp.start()
            own.append(cp)
            cp = pltpu.make_async_remote_copy(
                src_ref=ins[a], dst_ref=mine, send_sem=send_sem.at[a], recv_sem=recv_sem.at[a],
                device_id=(x, y, 1 - c), device_id_type=MESH)
            cp.start()
            copies.append(cp)
        for a in range(n):
            hr = ins[a].shape[0]
            theirs = outs[a].at[pl.ds((1 - c) * hr, hr), :]
            pltpu.make_async_remote_copy(
                src_ref=ins[a], dst_ref=theirs, send_sem=send_sem.at[a], recv_sem=recv_sem.at[a],
                device_id=(x, y, 1 - c), device_id_type=MESH).wait_recv()
        for cp in copies:
            cp.wait_send()
        for cp in own:
            cp.wait()

    return pl.pallas_call(
        body, name="halves_to_sibling", in_specs=[ANY] * n, out_specs=[ANY] * n,
        out_shape=[jax.ShapeDtypeStruct((2 * h.shape[0], h.shape[1]), h.dtype) for h in halves],
        scratch_shapes=[pltpu.SemaphoreType.DMA((n,)), pltpu.SemaphoreType.DMA((n,)),
                        pltpu.SemaphoreType.DMA((n,))],
        compiler_params=pltpu.CompilerParams(has_side_effects=True))(*halves)


def _small_step_call(partials, loss_rows, weights, m_s, v_s, sharded, d):
    n = len(partials)
    row_counts = [p.shape[0] for p in partials]
    starts = [sum(row_counts[:i]) for i in range(n)]
    loss_row = sum(row_counts)
    pack_rows = -(-(loss_row + 1) // SUBLANES) * SUBLANES
    dq = d // N_CHIPS

    def body(*refs):
        p_refs = refs[:n]
        loss_in = refs[n]
        w_refs = refs[n + 1:2 * n + 1]
        m_refs = refs[2 * n + 1:3 * n + 1]
        v_refs = refs[3 * n + 1:4 * n + 1]
        o = 4 * n + 1
        g_out = refs[o:o + n]
        d_out = refs[o + n:o + 2 * n]
        m_out = refs[o + 2 * n:o + 3 * n]
        v_out = refs[o + 3 * n:o + 4 * n]
        loss_out = refs[o + 4 * n]
        pack, slots, send_sem, recv_sem = refs[o + 4 * n + 1:]
        x, y, c = _place()
        me = 4 * x + 2 * y + c

        pack[...] = jnp.zeros_like(pack)
        for i in range(n):
            pack[pl.ds(starts[i], row_counts[i]), :] = p_refs[i][...]
        pack[pl.ds(loss_row, 1), :] = loss_in[...]

        copies = []
        for rel in range(1, N_DEV):
            peer = (x ^ (rel >> 2), y ^ ((rel >> 1) & 1), c ^ (rel & 1))
            cp = pltpu.make_async_remote_copy(
                src_ref=pack, dst_ref=slots.at[me], send_sem=send_sem.at[rel - 1], recv_sem=recv_sem.at[rel - 1],
                device_id=peer, device_id_type=MESH)
            cp.start()
            copies.append(cp)
        for rel in range(1, N_DEV):
            peer_idx = me ^ rel
            pltpu.make_async_remote_copy(
                src_ref=pack, dst_ref=slots.at[peer_idx], send_sem=send_sem.at[rel - 1],
                recv_sem=recv_sem.at[rel - 1], device_id=(x, y, c), device_id_type=MESH).wait_recv()
        for cp in copies:
            cp.wait_send()

        slots[me] = pack[...]
        total = slots[0]
        for dev in range(1, N_DEV):
            total = total + slots[dev]
        pack[...] = total

        loss_out[...] = jnp.broadcast_to(
            (0.5 / d) * jnp.sum(pack[pl.ds(loss_row, 1), :], axis=-1, keepdims=True), loss_out.shape)
        chip = 2 * x + y
        for i in range(n):
            rows = pl.ds(starts[i], row_counts[i])
            if sharded[i]:
                for k in range(N_CHIPS):
                    @pl.when(chip == k)
                    def _():
                        g_out[i][...] = pack[rows, pl.ds(k * dq, dq)]
            else:
                g_out[i][...] = pack[rows, :]
            d_out[i][...], m_out[i][...], v_out[i][...] = _adamw(
                w_refs[i][...], g_out[i][...], m_refs[i][...], v_refs[i][...])

    w_shapes = [jax.ShapeDtypeStruct(w.shape, F32) for w in weights]
    n_in = 4 * n + 1
    return pl.pallas_call(
        body, name="small_grads_allreduce_adamw",
        in_specs=[VMEM_FULL] * n_in, out_specs=[VMEM_FULL] * (4 * n + 1),
        out_shape=w_shapes * 4 + [jax.ShapeDtypeStruct((SUBLANES, 128), F32)],
        scratch_shapes=[pltpu.VMEM((pack_rows, d), F32), pltpu.VMEM((N_DEV, pack_rows, d), F32),
                        pltpu.SemaphoreType.DMA((N_DEV - 1,)), pltpu.SemaphoreType.DMA((N_DEV - 1,))],
        compiler_params=pltpu.CompilerParams(has_side_effects=True, vmem_limit_bytes=VMEM_LIMIT))(
            *partials, loss_rows, *weights, *m_s, *v_s)


def _tile(t, want):
    return min(t, want)


def kernel(x, norm1_pre_g, w_in, b_in, conv_a_w, conv_a_b, w_a_out, conv_b_w, conv_b_b, ln_b_g, ln_b_b, w_b_out, w_o, norm1_post_g, norm2_pre_g, w_mlp_in, w_mlp_out, norm2_post_g, loss_target, m_norm1_pre_g, m_w_in, m_b_in, m_conv_a_w, m_conv_a_b, m_w_a_out, m_conv_b_w, m_conv_b_b, m_ln_b_g, m_ln_b_b, m_w_b_out, m_w_o, m_norm1_post_g, m_norm2_pre_g, m_w_mlp_in, m_w_mlp_out, m_norm2_post_g, v_norm1_pre_g, v_w_in, v_b_in, v_conv_a_w, v_conv_a_b, v_w_a_out, v_conv_b_w, v_conv_b_b, v_ln_b_g, v_ln_b_b, v_w_b_out, v_w_o, v_norm1_post_g, v_norm2_pre_g, v_w_mlp_in, v_w_mlp_out, v_norm2_post_g):
    _, t, d = x.shape
    xt = x.reshape(t, d)
    tgt = loss_target.reshape(t, d)
    row = lambda vec: vec.reshape(1, -1)
    cx, cy, cc = _place()
    core = cc.astype(jnp.int32).reshape(1)
    chip = (2 * cx + cy).astype(jnp.int32).reshape(1)

    big = dict(w_in=w_in, w_a_out=w_a_out, w_b_out=w_b_out, w_o=w_o, w_mlp_in=w_mlp_in, w_mlp_out=w_mlp_out)
    names = list(big)
    gathered = _gather_weights_call([_cast_bf16(big[k], "cast_" + k) for k in names])
    wg = dict(zip(names, gathered))
    rows_of = lambda k: wg[k].reshape(-1, wg[k].shape[-1])
    w_a_full, w_b_full, w_o_full, w2_full = (rows_of(k) for k in ("w_a_out", "w_b_out", "w_o", "w_mlp_out"))

    g1pre, g1post, g2pre, g2post = row(norm1_pre_g), row(norm1_post_g), row(norm2_pre_g), row(norm2_post_g)
    lng, lnb, ba, bb = row(ln_b_g), row(ln_b_b), row(conv_a_b), row(conv_b_b)
    conv_a_full, conv_b_full = _gather_conv_weights(conv_a_w, conv_b_w, d)

    h, ua, ub, bg, cg, ha, a, sg, sa, sb = _proj_call(xt, g1pre, wg["w_in"], row(b_in), _tile(t, 512))
    x1, va, pa, cb, sbo, ya, yb, mg, mix = _mixer_fwd_call(
        ua, ub, bg, sa, sb, xt, conv_a_full, ba, conv_b_full, bb, lng, lnb,
        w_a_full, w_b_full, w_o_full, g1post, _tile(t, 256))
    dx1, f, df2, h2, df1, dg2post, dg2pre, loss_rows = _mlp_call(
        x1, tgt, g2pre, g2post, wg["w_mlp_in"], w2_full, _tile(t, 256))
    (dmix, dya, dyb, dva, dcb, dbg, dza, dzb, dg1post, dlng, dlnb, dba, dbb) = _mixer_bwd_call(
        dx1, mix, sa, sb, ya, yb, bg, va, cb, g1post, lng, lnb, w_a_full, w_b_full, w_o_full, _tile(t, 256))
    dproj, dwa_conv, dwb_conv, dbin = _conv_bwd_call(
        dva, dcb, ua, ub, cg, ha, a, sg, dbg, dza, dzb, conv_a_full, conv_b_full, _tile(t, 256))
    grad_x, dg1pre = _dx_call(dproj, xt, dx1, g1pre, wg["w_in"], _tile(t, 512))

    tt = _tile(t, 1024)
    n4, fq, dq = w_in.shape[1], w_mlp_in.shape[1], d // N_CHIPS
    grads = dict(
        w_in=_tn_matmul(h, dproj, N_CHIPS, d, n4, False, True, tt, "dw_in"),
        w_a_out=_tn_matmul(pa, dya, 1, d, d, False, False, tt, "dw_a_out").reshape(N_CHIPS, dq, d),
        w_b_out=_tn_matmul(sbo, dyb, 1, d, d, False, False, tt, "dw_b_out").reshape(N_CHIPS, dq, d),
        w_o=_tn_matmul(mg, dmix, 1, d, d, False, False, tt, "dw_o").reshape(N_CHIPS, dq, d),
        w_mlp_in=_tn_matmul(h2, df1, N_CHIPS, d, fq, False, True, tt, "dw_mlp_in"),
        w_mlp_out=_tn_matmul(f, df2, N_CHIPS, fq, d, True, False, tt, "dw_mlp_out"),
    )
    full = [grads[k] for k in names]
    from_sibling = _sibling_halves_call(full)
    pairs = [_pair_sum_call(g, p, core, "pair_sum_" + k) for g, p, k in zip(full, from_sibling, names)]
    received = _scatter_to_owner_call(pairs)
    halves = [_chip_sum_call(p, r, chip, "chip_sum_" + k) for p, r, k in zip(pairs, received, names)]
    reduced = dict(zip(names, _share_halves_call(halves)))

    moments = dict(w_in=(m_w_in, v_w_in), w_a_out=(m_w_a_out, v_w_a_out), w_b_out=(m_w_b_out, v_w_b_out),
                   w_o=(m_w_o, v_w_o), w_mlp_in=(m_w_mlp_in, v_w_mlp_in), w_mlp_out=(m_w_mlp_out, v_w_mlp_out))
    out = {}
    for k in names:
        delta, new_m, new_v = _adam_call(big[k], reduced[k], *moments[k], "adamw_" + k)
        out[k] = (reduced[k], delta, new_m, new_v)

    small = [
        ("conv_b_w", dwb_conv, conv_b_w, m_conv_b_w, v_conv_b_w, True),
        ("conv_b_b", dbb, bb, row(m_conv_b_b), row(v_conv_b_b), False),
        ("b_in", dbin.reshape(7, d), b_in.reshape(7, d), m_b_in.reshape(7, d), v_b_in.reshape(7, d), False),
        ("norm1_pre_g", dg1pre, row(norm1_pre_g), row(m_norm1_pre_g), row(v_norm1_pre_g), False),
        ("conv_a_w", dwa_conv, conv_a_w, m_conv_a_w, v_conv_a_w, True),
        ("conv_a_b", dba, ba, row(m_conv_a_b), row(v_conv_a_b), False),
        ("ln_b_g", dlng, lng, row(m_ln_b_g), row(v_ln_b_g), False),
        ("ln_b_b", dlnb, lnb, row(m_ln_b_b), row(v_ln_b_b), False),
        ("norm1_post_g", dg1post, g1post, row(m_norm1_post_g), row(v_norm1_post_g), False),
        ("norm2_pre_g", dg2pre, g2pre, row(m_norm2_pre_g), row(v_norm2_pre_g), False),
        ("norm2_post_g", dg2post, g2post, row(m_norm2_post_g), row(v_norm2_post_g), False),
    ]
    res = _small_step_call([s[1] for s in small], loss_rows, [s[2] for s in small], [s[3] for s in small],
                           [s[4] for s in small], [s[5] for s in small], d)
    ns = len(small)
    loss = res[4 * ns][0, 0]
    shapes = dict(norm1_pre_g=norm1_pre_g.shape, b_in=b_in.shape, conv_a_w=conv_a_w.shape,
                  conv_a_b=conv_a_b.shape, conv_b_w=conv_b_w.shape, conv_b_b=conv_b_b.shape,
                  ln_b_g=ln_b_g.shape, ln_b_b=ln_b_b.shape, norm1_post_g=norm1_post_g.shape,
                  norm2_pre_g=norm2_pre_g.shape, norm2_post_g=norm2_post_g.shape)
    for i, s in enumerate(small):
        out[s[0]] = tuple(res[q * ns + i].reshape(shapes[s[0]]) for q in range(4))

    order = ["norm1_pre_g", "w_in", "b_in", "conv_a_w", "conv_a_b", "w_a_out", "conv_b_w", "conv_b_b",
             "ln_b_g", "ln_b_b", "w_b_out", "w_o", "norm1_post_g", "norm2_pre_g", "w_mlp_in", "w_mlp_out",
             "norm2_post_g"]
    return (loss, grad_x.reshape(x.shape), *[out[k][0] for k in order], *[out[k][1] for k in order],
            *[out[k][2] for k in order], *[out[k][3] for k in order])
```

```python
import functools

import jax
import jax.numpy as jnp
from jax import lax
from jax.experimental import pallas as pl
from jax.experimental.pallas import tpu as pltpu

RMS_EPS = 1e-6
LN_EPS = 1e-5
ADAM_LR = 0.001
ADAM_B1 = 0.9
ADAM_B2 = 0.999
ADAM_EPS = 1e-08
ADAM_WD = 0.01
ADAM_STEP = 10

F32 = jnp.float32
BF16 = jnp.bfloat16
MESH = pl.DeviceIdType.MESH
ANY = pl.BlockSpec(memory_space=pl.ANY)
VMEM_FULL = pl.BlockSpec(memory_space=pltpu.VMEM)

V7X_VMEM_BYTES = 64 * 1024 * 1024
VMEM_LIMIT = V7X_VMEM_BYTES - 8 * 1024 * 1024
SUBLANES = 8
N_CHIPS = 4
N_DEV = 8
HALO_A = 8
HALO_B = 16
CONV_ROWS = 16
ROW_CHUNK = 32

NT_DIMS = (((1,), (1,)), ((), ()))
TN_DIMS = (((0,), (0,)), ((), ()))


def _params(*sem):
    return pltpu.CompilerParams(dimension_semantics=sem, vmem_limit_bytes=VMEM_LIMIT)


def _rows(tm, d):
    return pl.BlockSpec((tm, d), lambda i: (i, 0))


def _const(shape):
    return pl.BlockSpec(shape, lambda i: (0,) * len(shape))


def _halo_prev(tm, hb, d):
    return pl.BlockSpec((hb, d), lambda i: (jnp.maximum(i * (tm // hb) - 1, 0), 0))


def _halo_next(tm, hb, d, t):
    return pl.BlockSpec((hb, d), lambda i: (jnp.minimum((i + 1) * (tm // hb), t // hb - 1), 0))


def _for_chunks(n_rows, rc, fn):
    def body(ci, carry):
        fn(pl.ds(pl.multiple_of(ci * rc, rc), rc))
        return carry
    lax.fori_loop(0, n_rows // rc, body, 0)


def _fold8(v):
    return v.reshape(v.shape[0] // SUBLANES, SUBLANES, v.shape[1]).sum(axis=0)


def _mean_lanes(v):
    return jnp.mean(v, axis=-1, keepdims=True)


def _load_blocks_once(w_hbm, w_vmem, sem):
    nb, _, n = w_hbm.shape

    @pl.when(pl.program_id(0) == 0)
    def _():
        copies = [pltpu.make_async_copy(w_hbm.at[j], w_vmem.at[:, pl.ds(j * n, n)], sem.at[j])
                  for j in range(nb)]
        for cp in copies:
            cp.start()
        for cp in copies:
            cp.wait()


def _load_once(w_hbm, w_vmem, sem):
    @pl.when(pl.program_id(0) == 0)
    def _():
        cp = pltpu.make_async_copy(w_hbm, w_vmem, sem)
        cp.start()
        cp.wait()


def _write_row_sums(acc_ref, out_ref, n_steps):
    @pl.when(pl.program_id(0) == n_steps - 1)
    def _():
        out_ref[...] = jnp.sum(acc_ref[...], axis=0, keepdims=True)


def _place():
    return lax.axis_index("x"), lax.axis_index("y"), lax.axis_index("c")


def _other_chips(x, y):
    rel = [(x, 1 - y), (1 - x, y), (1 - x, 1 - y)]
    return [(px, py, 2 * px + py) for px, py in rel]


class _Exchange:
    def __init__(self, inputs, out_shapes, aliases, n_sems, copies):
        self.inputs = list(inputs)
        self.out_shapes = list(out_shapes)
        self.aliases = dict(aliases)
        self.n_sems = n_sems
        self.copies = copies


def _remote(src, dst, send, recv, device):
    return pltpu.make_async_remote_copy(src_ref=src, dst_ref=dst, send_sem=send, recv_sem=recv,
                                        device_id=device, device_id_type=MESH)


def _sds(a):
    return jax.ShapeDtypeStruct(a.shape, a.dtype)


def _ex_gather_ici(bufs):
    n = len(bufs)

    def copies(xin, xout, send, recv):
        x, y, c = _place()
        me = 2 * x + y
        out = []
        for a in range(n):
            hr = xin[a].shape[1] // 2
            rows = pl.ds(c * hr, hr)
            for j, (px, py, _) in enumerate(_other_chips(x, y)):
                k = a * (N_CHIPS - 1) + j
                out.append(_remote(xin[a].at[me, rows, :], xout[a].at[me, rows, :], send(k), recv(k), (px, py, c)))
        return out

    return _Exchange(bufs, [_sds(b) for b in bufs], {a: a for a in range(n)}, n * (N_CHIPS - 1), copies)


def _ex_gather_forward(bufs):
    n = len(bufs)

    def copies(xin, xout, send, recv):
        x, y, c = _place()
        out = []
        for a in range(n):
            hr = xin[a].shape[1] // 2
            rows = pl.ds(c * hr, hr)
            for j, (_, _, pk) in enumerate(_other_chips(x, y)):
                k = a * (N_CHIPS - 1) + j
                out.append(_remote(xin[a].at[pk, rows, :], xout[a].at[pk, rows, :], send(k), recv(k), (x, y, 1 - c)))
        return out

    return _Exchange(bufs, [_sds(b) for b in bufs], {a: a for a in range(n)}, n * (N_CHIPS - 1), copies)


def _ex_sibling_halves(grads):
    n = len(grads)

    def copies(xin, xout, send, recv):
        x, y, c = _place()
        out = []
        for a in range(n):
            hr = xin[a].shape[1] // 2
            out.append(_remote(xin[a].at[:, pl.ds((1 - c) * hr, hr), :], xout[a], send(a), recv(a), (x, y, 1 - c)))
        return out

    shapes = [jax.ShapeDtypeStruct((g.shape[0], g.shape[1] // 2, g.shape[2]), g.dtype) for g in grads]
    return _Exchange(grads, shapes, {}, n, copies)


def _ex_scatter_to_owner(pairs):
    n = len(pairs)

    def copies(xin, xout, send, recv):
        x, y, c = _place()
        out = []
        for a in range(n):
            for j, (px, py, pk) in enumerate(_other_chips(x, y)):
                k = a * (N_CHIPS - 1) + j
                out.append(_remote(xin[a].at[pk], xout[a].at[j], send(k), recv(k), (px, py, c)))
        return out

    shapes = [jax.ShapeDtypeStruct((N_CHIPS - 1,) + p.shape[1:], p.dtype) for p in pairs]
    return _Exchange(pairs, shapes, {}, n * (N_CHIPS - 1), copies)


def _ex_share_halves(reduced):
    n = len(reduced)

    def copies(xin, xout, send, recv):
        x, y, c = _place()
        out = []
        for a in range(n):
            hr = xin[a].shape[0] // 2
            rows = pl.ds(c * hr, hr)
            out.append(_remote(xin[a].at[rows, :], xout[a].at[rows, :], send(a), recv(a), (x, y, 1 - c)))
        return out

    return _Exchange(reduced, [_sds(r) for r in reduced], {a: a for a in range(n)}, n, copies)


def _merge(*exs):
    exs = [e for e in exs if e is not None]
    if not exs:
        return None
    inputs, shapes, aliases = [], [], {}
    in_off, out_off, sem_off = [], [], []
    n_sems = 0
    for e in exs:
        in_off.append(len(inputs))
        out_off.append(len(shapes))
        sem_off.append(n_sems)
        aliases.update({len(inputs) + i: len(shapes) + o for i, o in e.aliases.items()})
        inputs += e.inputs
        shapes += e.out_shapes
        n_sems += e.n_sems

    def copies(xin, xout, send, recv):
        out = []
        for e, io, oo, so in zip(exs, in_off, out_off, sem_off):
            out += e.copies(xin[io:io + len(e.inputs)], xout[oo:oo + len(e.out_shapes)],
                            lambda i, so=so: send(so + i), lambda i, so=so: recv(so + i))
        return out

    return _Exchange(inputs, shapes, aliases, n_sems, copies)


def _split(ex_outs, *exs):
    parts, o = [], 0
    for e in exs:
        parts.append(list(ex_outs[o:o + len(e.out_shapes)]))
        o += len(e.out_shapes)
    return parts


def _call(body, *, name, grid, in_specs, out_specs, out_shape, scratch_shapes, args, ex=None):
    n_in, n_out, n_scr = len(in_specs), len(out_specs), len(scratch_shapes)
    seq = ("arbitrary",) * len(grid)
    if ex is None:
        outs = pl.pallas_call(
            body, name=name, grid=grid, in_specs=list(in_specs), out_specs=list(out_specs),
            out_shape=list(out_shape), scratch_shapes=list(scratch_shapes), compiler_params=_params(*seq))(*args)
        return list(outs), []
    n_xi, n_xo = len(ex.inputs), len(ex.out_shapes)

    def full(*refs):
        ins, xin = refs[:n_in], refs[n_in:n_in + n_xi]
        o = n_in + n_xi
        outs, xout = refs[o:o + n_out], refs[o + n_out:o + n_out + n_xo]
        s = o + n_out + n_xo
        scr = refs[s:s + n_scr]
        send_sems, recv_sems = refs[s + n_scr:]
        send = lambda i: send_sems.at[i]
        recv = lambda i: recv_sems.at[i]
        first = functools.reduce(jnp.logical_and, [pl.program_id(a) == 0 for a in range(len(grid))])
        last = functools.reduce(jnp.logical_and, [pl.program_id(a) == grid[a] - 1 for a in range(len(grid))])

        @pl.when(first)
        def _():
            for cp in ex.copies(xin, xout, send, recv):
                cp.start()

        body(*ins, *outs, *scr)

        @pl.when(last)
        def _():
            for cp in ex.copies(xin, xout, send, recv):
                cp.wait()

    res = pl.pallas_call(
        full, name=name, grid=grid, in_specs=list(in_specs) + [ANY] * n_xi,
        out_specs=list(out_specs) + [ANY] * n_xo, out_shape=list(out_shape) + ex.out_shapes,
        scratch_shapes=list(scratch_shapes) + [pltpu.SemaphoreType.DMA((ex.n_sems,)),
                                               pltpu.SemaphoreType.DMA((ex.n_sems,))],
        input_output_aliases={n_in + i: n_out + o for i, o in ex.aliases.items()},
        compiler_params=pltpu.CompilerParams(dimension_semantics=seq, vmem_limit_bytes=VMEM_LIMIT,
                                             has_side_effects=True))(*args, *ex.inputs)
    return list(res[:n_out]), list(res[n_out:])


def _exchange_call(name, phases):
    first = phases[0]
    n_xi, n_xo = len(first.inputs), len(first.out_shapes)

    def body(*refs):
        xin, xout = refs[:n_xi], refs[n_xi:n_xi + n_xo]
        sems = refs[n_xi + n_xo:]
        for p, ex in enumerate(phases):
            send_sems, recv_sems = sems[2 * p], sems[2 * p + 1]
            cps = ex.copies(xin, xout, lambda i: send_sems.at[i], lambda i: recv_sems.at[i])
            for cp in cps:
                cp.start()
            for cp in cps:
                cp.wait()

    sems = []
    for ex in phases:
        sems += [pltpu.SemaphoreType.DMA((ex.n_sems,)), pltpu.SemaphoreType.DMA((ex.n_sems,))]
    return list(pl.pallas_call(
        body, name=name, in_specs=[ANY] * n_xi, out_specs=[ANY] * n_xo, out_shape=first.out_shapes,
        scratch_shapes=sems, input_output_aliases=dict(first.aliases),
        compiler_params=pltpu.CompilerParams(has_side_effects=True))(*first.inputs))


def _cast_to_slot(w, chip, name):
    r, c = w.shape
    tr = min(r, 256)

    def body(chip_ref, w_ref, o_ref):
        o_ref[0] = w_ref[...].astype(BF16)

    return pl.pallas_call(
        body, name=name,
        grid_spec=pltpu.PrefetchScalarGridSpec(
            num_scalar_prefetch=1, grid=(r // tr,),
            in_specs=[pl.BlockSpec((tr, c), lambda i, k: (i, 0))],
            out_specs=pl.BlockSpec((1, tr, c), lambda i, k: (k[0], i, 0))),
        out_shape=jax.ShapeDtypeStruct((N_CHIPS, r, c), BF16),
        compiler_params=_params("parallel"))(chip, w)


def _proj_call(x, g1pre, w_in_g, b_in, tm, ex=None):
    t, d = x.shape
    nb, _, n4 = w_in_g.shape
    ni = nb * n4
    assert ni == 7 * d

    def body(x_ref, g_ref, b_ref, w_hbm, h_ref, ua_ref, ub_ref, bg_ref, cg_ref, ha_ref, a_ref,
             sg_ref, sa_ref, sb_ref, w_v, p0, p1, sem):
        _load_blocks_once(w_hbm, w_v, sem)

        def norm(rows):
            xv = x_ref[rows, :]
            r = lax.rsqrt(_mean_lanes(xv * xv) + RMS_EPS)
            h_ref[rows, :] = (xv * r * g_ref[...]).astype(BF16)
        _for_chunks(tm, ROW_CHUNK, norm)

        def group(i, dst):
            cols = pl.ds(i * d, d)
            dst[...] = jnp.dot(h_ref[...], w_v[:, cols], preferred_element_type=F32) + b_ref[:, cols]

        group(0, p0)

        def bgate(rows):
            bg_ref[rows, :] = p0[rows, :].astype(BF16)
        _for_chunks(tm, ROW_CHUNK, bgate)

        group(1, p0)
        group(2, p1)

        def branch_a(rows):
            cg, ha = p0[rows, :], p1[rows, :]
            ua_ref[rows, :] = cg * ha
            cg_ref[rows, :] = cg.astype(BF16)
            ha_ref[rows, :] = ha.astype(BF16)
        _for_chunks(tm, ROW_CHUNK, branch_a)

        group(3, p0)
        group(4, p1)

        def branch_b(rows):
            a, sg = p0[rows, :], jax.nn.sigmoid(p1[rows, :])
            ub_ref[rows, :] = a * sg
            a_ref[rows, :] = a.astype(BF16)
            sg_ref[rows, :] = sg.astype(BF16)
        _for_chunks(tm, ROW_CHUNK, branch_b)

        group(5, p0)
        group(6, p1)

        def gates(rows):
            sa_ref[rows, :] = jax.nn.sigmoid(p0[rows, :]).astype(BF16)
            sb_ref[rows, :] = jax.nn.sigmoid(p1[rows, :]).astype(BF16)
        _for_chunks(tm, ROW_CHUNK, gates)

    bf = jax.ShapeDtypeStruct((t, d), BF16)
    f32 = jax.ShapeDtypeStruct((t, d), F32)
    return _call(
        body, name="proj_fwd", grid=(t // tm,),
        in_specs=[_rows(tm, d), _const((1, d)), _const((1, ni)), ANY],
        out_specs=[_rows(tm, d)] * 10,
        out_shape=[bf, f32, f32, bf, bf, bf, bf, bf, bf, bf],
        scratch_shapes=[pltpu.VMEM((d, ni), BF16), pltpu.VMEM((tm, d), F32), pltpu.VMEM((tm, d), F32),
                        pltpu.SemaphoreType.DMA((nb,))],
        args=(x, g1pre, b_in, w_in_g), ex=ex)


def _fill_ext(ext, prev_ref, cur_ref, next_ref, hb, tm, i, n_steps):
    ext[pl.ds(0, hb), :] = jnp.where(i > 0, prev_ref[...], 0.0)
    ext[pl.ds(hb, tm), :] = cur_ref[...]
    ext[pl.ds(hb + tm, hb), :] = jnp.where(i < n_steps - 1, next_ref[...], 0.0)


def _conv_taps(ext, w_ref, r0, first_offset, step, n_taps, init):
    acc = init
    for k in range(n_taps):
        acc = acc + w_ref[k:k + 1, :] * ext[pl.ds(r0 + first_offset + step * k, CONV_ROWS), :]
    return acc


def _mixer_fwd_call(ua, ub, bg, sa, sb, x, conv_a_w, conv_a_b, conv_b_w, conv_b_b, ln_g, ln_b,
                    w_a, w_b, w_o, g1post, tm, ex=None):
    t, d = x.shape
    n_steps = t // tm
    ka, kb = conv_a_w.shape[0], conv_b_w.shape[0]

    def body(uap, uac, uan, ubp, ubc, ubn, bg_ref, sa_ref, sb_ref, x_ref, wa_c, ba_c, wb_c, bb_c,
             lng, lnb, wa_hbm, wb_hbm, wo_hbm, g_ref,
             x1_ref, va_ref, pa_ref, cb_ref, sbo_ref, ya_ref, yb_ref, mg_ref, mix_ref,
             ext_a, ext_b, wa_v, wb_v, wo_v, y0, y1, sem):
        i = pl.program_id(0)
        _load_once(wa_hbm, wa_v, sem.at[0])
        _load_once(wb_hbm, wb_v, sem.at[1])
        _load_once(wo_hbm, wo_v, sem.at[2])
        _fill_ext(ext_a, uap, uac, uan, HALO_A, tm, i, n_steps)
        _fill_ext(ext_b, ubp, ubc, ubn, HALO_B, tm, i, n_steps)

        for r0 in range(0, tm, CONV_ROWS):
            rows = pl.ds(r0, CONV_ROWS)
            va = _conv_taps(ext_a, wa_c, r0, HALO_A - (ka - 1) // 2, 1, ka,
                            jnp.broadcast_to(ba_c[...], (CONV_ROWS, d)))
            va_ref[rows, :] = va.astype(BF16)
            pa_ref[rows, :] = (bg_ref[rows, :].astype(F32) * va).astype(BF16)
            cb = _conv_taps(ext_b, wb_c, r0, HALO_B - (kb - 1) // 2, 1, kb,
                            jnp.broadcast_to(bb_c[...], (CONV_ROWS, d)))
            cb_ref[rows, :] = cb
            mu = _mean_lanes(cb)
            cen = cb - mu
            rstd = lax.rsqrt(_mean_lanes(cen * cen) + LN_EPS)
            ln = cen * rstd * lng[...] + lnb[...]
            sbo_ref[rows, :] = (ln * jax.nn.sigmoid(ln)).astype(BF16)

        y0[...] = jnp.dot(pa_ref[...], wa_v[...], preferred_element_type=F32)
        y1[...] = jnp.dot(sbo_ref[...], wb_v[...], preferred_element_type=F32)

        def merge(rows):
            ya, yb = y0[rows, :], y1[rows, :]
            ya_ref[rows, :] = ya.astype(BF16)
            yb_ref[rows, :] = yb.astype(BF16)
            mg_ref[rows, :] = (sa_ref[rows, :].astype(F32) * ya + sb_ref[rows, :].astype(F32) * yb).astype(BF16)
        _for_chunks(tm, ROW_CHUNK, merge)

        mix_ref[...] = jnp.dot(mg_ref[...], wo_v[...], preferred_element_type=F32)

        def resid(rows):
            mix = mix_ref[rows, :]
            r = lax.rsqrt(_mean_lanes(mix * mix) + RMS_EPS)
            x1_ref[rows, :] = x_ref[rows, :] + mix * r * g_ref[...]
        _for_chunks(tm, ROW_CHUNK, resid)

    bf = jax.ShapeDtypeStruct((t, d), BF16)
    f32 = jax.ShapeDtypeStruct((t, d), F32)
    return _call(
        body, name="mixer_fwd", grid=(n_steps,),
        in_specs=[_halo_prev(tm, HALO_A, d), _rows(tm, d), _halo_next(tm, HALO_A, d, t),
                  _halo_prev(tm, HALO_B, d), _rows(tm, d), _halo_next(tm, HALO_B, d, t),
                  _rows(tm, d), _rows(tm, d), _rows(tm, d), _rows(tm, d),
                  _const((ka, d)), _const((1, d)), _const((kb, d)), _const((1, d)),
                  _const((1, d)), _const((1, d)), ANY, ANY, ANY, _const((1, d))],
        out_specs=[_rows(tm, d)] * 9,
        out_shape=[f32, bf, bf, f32, bf, bf, bf, bf, f32],
        scratch_shapes=[pltpu.VMEM((tm + 2 * HALO_A, d), F32), pltpu.VMEM((tm + 2 * HALO_B, d), F32),
                        pltpu.VMEM((d, d), BF16), pltpu.VMEM((d, d), BF16), pltpu.VMEM((d, d), BF16),
                        pltpu.VMEM((tm, d), F32), pltpu.VMEM((tm, d), F32),
                        pltpu.SemaphoreType.DMA((3,))],
        args=(ua, ua, ua, ub, ub, ub, bg, sa, sb, x, conv_a_w, conv_a_b, conv_b_w, conv_b_b,
              ln_g, ln_b, w_a, w_b, w_o, g1post), ex=ex)


def _mlp_call(x1, target, g2pre, g2post, w1_g, w2, tm, ex=None):
    t, d = x1.shape
    nb, _, fq = w1_g.shape
    f = nb * fq
    n_steps = t // tm
    inv_d = 1.0 / d

    def body(x1_ref, t_ref, gpre, gpost, w1_hbm, w2_hbm,
             dx1_ref, f_ref, df2_ref, h2_ref, df1_ref, dgpost_ref, dgpre_ref, loss_ref,
             w1_v, w2_v, f1_s, blk_s, f2_s, acc_post, acc_pre, acc_loss, sem):
        _load_blocks_once(w1_hbm, w1_v, sem)
        _load_once(w2_hbm, w2_v, sem.at[nb])

        @pl.when(pl.program_id(0) == 0)
        def _():
            acc_post[...] = jnp.zeros_like(acc_post)
            acc_pre[...] = jnp.zeros_like(acc_pre)
            acc_loss[...] = jnp.zeros_like(acc_loss)

        def norm(rows):
            xv = x1_ref[rows, :]
            r = lax.rsqrt(_mean_lanes(xv * xv) + RMS_EPS)
            h2_ref[rows, :] = (xv * r * gpre[...]).astype(BF16)
        _for_chunks(tm, ROW_CHUNK, norm)

        for j in range(nb):
            cols = pl.ds(j * fq, fq)
            f1_s[:, cols] = jnp.dot(h2_ref[...], w1_v[:, cols], preferred_element_type=F32)

        def act(rows):
            relu = jnp.maximum(f1_s[rows, :], 0.0)
            f_ref[rows, :] = (relu * relu).astype(BF16)
        _for_chunks(tm, ROW_CHUNK, act)

        f2_s[...] = jnp.dot(f_ref[...], w2_v[...], preferred_element_type=F32)

        def head(rows):
            f2 = f2_s[rows, :]
            rf = lax.rsqrt(_mean_lanes(f2 * f2) + RMS_EPS)
            y = x1_ref[rows, :] + f2 * rf * gpost[...]
            err = y - t_ref[rows, :]
            acc_loss[...] += _fold8(err * err)
            dy = err * inv_d
            gdy = dy * gpost[...]
            df2 = rf * gdy - f2 * (rf * rf * rf * _mean_lanes(gdy * f2))
            df2_ref[rows, :] = df2.astype(BF16)
            acc_post[...] += _fold8(dy * f2 * rf)
            dx1_ref[rows, :] = dy
        _for_chunks(tm, ROW_CHUNK, head)

        for j in range(nb):
            cols = pl.ds(j * fq, fq)
            blk_s[...] = lax.dot_general(df2_ref[...], w2_v[cols, :], NT_DIMS, preferred_element_type=F32)

            def dact(rows):
                relu = jnp.maximum(f1_s[rows, cols], 0.0)
                df1_ref[rows, cols] = (blk_s[rows, :] * (2.0 * relu)).astype(BF16)
            _for_chunks(tm, ROW_CHUNK, dact)

        f2_s[...] = lax.dot_general(df1_ref[...], w1_v[...], NT_DIMS, preferred_element_type=F32)

        def dnorm(rows):
            dh2 = f2_s[rows, :]
            xv = x1_ref[rows, :]
            r = lax.rsqrt(_mean_lanes(xv * xv) + RMS_EPS)
            gd = dh2 * gpre[...]
            dx1_ref[rows, :] = dx1_ref[rows, :] + r * gd - xv * (r * r * r * _mean_lanes(gd * xv))
            acc_pre[...] += _fold8(dh2 * xv * r)
        _for_chunks(tm, ROW_CHUNK, dnorm)

        _write_row_sums(acc_post, dgpost_ref, n_steps)
        _write_row_sums(acc_pre, dgpre_ref, n_steps)
        _write_row_sums(acc_loss, loss_ref, n_steps)

    row = jax.ShapeDtypeStruct((1, d), F32)
    return _call(
        body, name="mlp_fwd_bwd", grid=(n_steps,),
        in_specs=[_rows(tm, d), _rows(tm, d), _const((1, d)), _const((1, d)), ANY, ANY],
        out_specs=[_rows(tm, d), _rows(tm, f), _rows(tm, d), _rows(tm, d), _rows(tm, f),
                   _const((1, d)), _const((1, d)), _const((1, d))],
        out_shape=[jax.ShapeDtypeStruct((t, d), F32), jax.ShapeDtypeStruct((t, f), BF16),
                   jax.ShapeDtypeStruct((t, d), BF16), jax.ShapeDtypeStruct((t, d), BF16),
                   jax.ShapeDtypeStruct((t, f), BF16), row, row, row],
        scratch_shapes=[pltpu.VMEM((d, f), BF16), pltpu.VMEM((f, d), BF16),
                        pltpu.VMEM((tm, f), F32), pltpu.VMEM((tm, fq), F32), pltpu.VMEM((tm, d), F32),
                        pltpu.VMEM((SUBLANES, d), F32), pltpu.VMEM((SUBLANES, d), F32),
                        pltpu.VMEM((SUBLANES, d), F32), pltpu.SemaphoreType.DMA((nb + 1,))],
        args=(x1, target, g2pre, g2post, w1_g, w2), ex=ex)


def _mixer_bwd_call(dx1, mix, sa, sb, ya, yb, bg, va, cb, g1post, ln_g, ln_b, w_a, w_b, w_o, tm, ex=None):
    t, d = dx1.shape
    n_steps = t // tm

    def body(dx1_ref, mix_ref, sa_ref, sb_ref, ya_ref, yb_ref, bg_ref, va_ref, cb_ref, g_ref, lng, lnb,
             wa_hbm, wb_hbm, wo_hbm,
             dmix_ref, dya_ref, dyb_ref, dva_ref, dcb_ref, dbg_ref, dza_ref, dzb_ref,
             dg_ref, dlng_ref, dlnb_ref, dba_ref, dbb_ref,
             wa_v, wb_v, wo_v, s0, s1, acc_g, acc_lng, acc_lnb, acc_ba, acc_bb, sem):
        _load_once(wa_hbm, wa_v, sem.at[0])
        _load_once(wb_hbm, wb_v, sem.at[1])
        _load_once(wo_hbm, wo_v, sem.at[2])
        accs = (acc_g, acc_lng, acc_lnb, acc_ba, acc_bb)

        @pl.when(pl.program_id(0) == 0)
        def _():
            for acc in accs:
                acc[...] = jnp.zeros_like(acc)

        def dnorm(rows):
            mix = mix_ref[rows, :]
            dxv = dx1_ref[rows, :]
            r = lax.rsqrt(_mean_lanes(mix * mix) + RMS_EPS)
            gd = dxv * g_ref[...]
            dmix_ref[rows, :] = (r * gd - mix * (r * r * r * _mean_lanes(gd * mix))).astype(BF16)
            acc_g[...] += _fold8(dxv * mix * r)
        _for_chunks(tm, ROW_CHUNK, dnorm)

        s0[...] = lax.dot_general(dmix_ref[...], wo_v[...], NT_DIMS, preferred_element_type=F32)

        def dmerge(rows):
            dm = s0[rows, :]
            sav, sbv = sa_ref[rows, :].astype(F32), sb_ref[rows, :].astype(F32)
            dya_ref[rows, :] = (dm * sav).astype(BF16)
            dyb_ref[rows, :] = (dm * sbv).astype(BF16)
            dza_ref[rows, :] = (dm * ya_ref[rows, :].astype(F32) * sav * (1.0 - sav)).astype(BF16)
            dzb_ref[rows, :] = (dm * yb_ref[rows, :].astype(F32) * sbv * (1.0 - sbv)).astype(BF16)
        _for_chunks(tm, ROW_CHUNK, dmerge)

        s0[...] = lax.dot_general(dya_ref[...], wa_v[...], NT_DIMS, preferred_element_type=F32)
        s1[...] = lax.dot_general(dyb_ref[...], wb_v[...], NT_DIMS, preferred_element_type=F32)

        def dbranches(rows):
            dpa = s0[rows, :]
            dbg_ref[rows, :] = (dpa * va_ref[rows, :].astype(F32)).astype(BF16)
            dva = dpa * bg_ref[rows, :].astype(F32)
            dva_ref[rows, :] = dva
            acc_ba[...] += _fold8(dva)
            cbv = cb_ref[rows, :]
            mu = _mean_lanes(cbv)
            cen = cbv - mu
            rstd = lax.rsqrt(_mean_lanes(cen * cen) + LN_EPS)
            xhat = cen * rstd
            ln = xhat * lng[...] + lnb[...]
            sig = jax.nn.sigmoid(ln)
            dln = s1[rows, :] * (sig * (1.0 + ln * (1.0 - sig)))
            acc_lng[...] += _fold8(dln * xhat)
            acc_lnb[...] += _fold8(dln)
            dxh = dln * lng[...]
            dcb = rstd * (dxh - _mean_lanes(dxh) - xhat * _mean_lanes(dxh * xhat))
            dcb_ref[rows, :] = dcb
            acc_bb[...] += _fold8(dcb)
        _for_chunks(tm, ROW_CHUNK, dbranches)

        _write_row_sums(acc_g, dg_ref, n_steps)
        _write_row_sums(acc_lng, dlng_ref, n_steps)
        _write_row_sums(acc_lnb, dlnb_ref, n_steps)
        _write_row_sums(acc_ba, dba_ref, n_steps)
        _write_row_sums(acc_bb, dbb_ref, n_steps)

    bf = jax.ShapeDtypeStruct((t, d), BF16)
    f32 = jax.ShapeDtypeStruct((t, d), F32)
    row = jax.ShapeDtypeStruct((1, d), F32)
    return _call(
        body, name="mixer_bwd", grid=(n_steps,),
        in_specs=[_rows(tm, d)] * 9 + [_const((1, d))] * 3 + [ANY, ANY, ANY],
        out_specs=[_rows(tm, d)] * 8 + [_const((1, d))] * 5,
        out_shape=[bf, bf, bf, f32, f32, bf, bf, bf, row, row, row, row, row],
        scratch_shapes=[pltpu.VMEM((d, d), BF16), pltpu.VMEM((d, d), BF16), pltpu.VMEM((d, d), BF16),
                        pltpu.VMEM((tm, d), F32), pltpu.VMEM((tm, d), F32)]
        + [pltpu.VMEM((SUBLANES, d), F32)] * 5 + [pltpu.SemaphoreType.DMA((3,))],
        args=(dx1, mix, sa, sb, ya, yb, bg, va, cb, g1post, ln_g, ln_b, w_a, w_b, w_o), ex=ex)


def _conv_bwd_call(dva, dcb, ua, ub, cg, ha, a, sg, dbg, dza, dzb, conv_a_w, conv_b_w, tm, ex=None):
    t, d = dva.shape
    n_steps = t // tm
    ka, kb = conv_a_w.shape[0], conv_b_w.shape[0]
    pa, pb = (ka - 1) // 2, (kb - 1) // 2

    def body(dvap, dvac, dvan, dcbp, dcbc, dcbn, uap, uac, uan, ubp, ubc, ubn,
             cg_ref, ha_ref, a_ref, sg_ref, dbg_ref, dza_ref, dzb_ref, wa_c, wb_c,
             dproj_ref, dwa_ref, dwb_ref, dbin_ref,
             e_dva, e_dcb, e_ua, e_ub, acc_wa, acc_wb, acc_bin):
        i = pl.program_id(0)

        @pl.when(i == 0)
        def _():
            acc_wa[...] = jnp.zeros_like(acc_wa)
            acc_wb[...] = jnp.zeros_like(acc_wb)
            acc_bin[...] = jnp.zeros_like(acc_bin)

        _fill_ext(e_dva, dvap, dvac, dvan, HALO_A, tm, i, n_steps)
        _fill_ext(e_dcb, dcbp, dcbc, dcbn, HALO_B, tm, i, n_steps)
        _fill_ext(e_ua, uap, uac, uan, HALO_A, tm, i, n_steps)
        _fill_ext(e_ub, ubp, ubc, ubn, HALO_B, tm, i, n_steps)

        def put(col, rows, val_f32):
            dproj_ref[rows, pl.ds(col * d, d)] = val_f32.astype(BF16)
            acc_bin[:, pl.ds(col * d, d)] += _fold8(val_f32)

        zero = jnp.zeros((CONV_ROWS, d), F32)
        for r0 in range(0, tm, CONV_ROWS):
            rows = pl.ds(r0, CONV_ROWS)
            dua = _conv_taps(e_dva, wa_c, r0, HALO_A + pa, -1, ka, zero)
            dub = _conv_taps(e_dcb, wb_c, r0, HALO_B + pb, -1, kb, zero)
            dva_c = dvac[rows, :]
            dcb_c = dcbc[rows, :]
            for k in range(ka):
                acc_wa[pl.ds(k * SUBLANES, SUBLANES), :] += _fold8(
                    dva_c * e_ua[pl.ds(r0 + HALO_A - pa + k, CONV_ROWS), :])
            for k in range(kb):
                acc_wb[pl.ds(k * SUBLANES, SUBLANES), :] += _fold8(
                    dcb_c * e_ub[pl.ds(r0 + HALO_B - pb + k, CONV_ROWS), :])
            cgv, hav = cg_ref[rows, :].astype(F32), ha_ref[rows, :].astype(F32)
            av, sgv = a_ref[rows, :].astype(F32), sg_ref[rows, :].astype(F32)
            put(0, rows, dbg_ref[rows, :].astype(F32))
            put(1, rows, dua * hav)
            put(2, rows, dua * cgv)
            put(3, rows, dub * sgv)
            put(4, rows, dub * av * sgv * (1.0 - sgv))
            put(5, rows, dza_ref[rows, :].astype(F32))
            put(6, rows, dzb_ref[rows, :].astype(F32))

        @pl.when(i == n_steps - 1)
        def _():
            for k in range(ka):
                dwa_ref[k:k + 1, :] = jnp.sum(acc_wa[pl.ds(k * SUBLANES, SUBLANES), :], axis=0, keepdims=True)
            for k in range(kb):
                dwb_ref[k:k + 1, :] = jnp.sum(acc_wb[pl.ds(k * SUBLANES, SUBLANES), :], axis=0, keepdims=True)
            dbin_ref[...] = jnp.sum(acc_bin[...], axis=0, keepdims=True)

    halo_a = [_halo_prev(tm, HALO_A, d), _rows(tm, d), _halo_next(tm, HALO_A, d, t)]
    halo_b = [_halo_prev(tm, HALO_B, d), _rows(tm, d), _halo_next(tm, HALO_B, d, t)]
    return _call(
        body, name="conv_bwd", grid=(n_steps,),
        in_specs=halo_a + halo_b + halo_a + halo_b + [_rows(tm, d)] * 7 + [_const((ka, d)), _const((kb, d))],
        out_specs=[_rows(tm, 7 * d), _const((ka, d)), _const((kb, d)), _const((1, 7 * d))],
        out_shape=[jax.ShapeDtypeStruct((t, 7 * d), BF16), jax.ShapeDtypeStruct((ka, d), F32),
                   jax.ShapeDtypeStruct((kb, d), F32), jax.ShapeDtypeStruct((1, 7 * d), F32)],
        scratch_shapes=[pltpu.VMEM((tm + 2 * HALO_A, d), F32), pltpu.VMEM((tm + 2 * HALO_B, d), F32),
                        pltpu.VMEM((tm + 2 * HALO_A, d), F32), pltpu.VMEM((tm + 2 * HALO_B, d), F32),
                        pltpu.VMEM((ka * SUBLANES, d), F32), pltpu.VMEM((kb * SUBLANES, d), F32),
                        pltpu.VMEM((SUBLANES, 7 * d), F32)],
        args=(dva, dva, dva, dcb, dcb, dcb, ua, ua, ua, ub, ub, ub, cg, ha, a, sg, dbg, dza, dzb,
              conv_a_w, conv_b_w), ex=ex)


def _dx_call(dproj, x, dx1, g1pre, w_in_g, tm, ex=None):
    t, d = x.shape
    nb, _, n4 = w_in_g.shape
    ni = nb * n4
    n_steps = t // tm

    def body(dp_ref, x_ref, dx1_ref, g_ref, w_hbm, dx_ref, dg_ref, w_v, dh_s, acc_g, sem):
        _load_blocks_once(w_hbm, w_v, sem)

        @pl.when(pl.program_id(0) == 0)
        def _():
            acc_g[...] = jnp.zeros_like(acc_g)

        dh_s[...] = lax.dot_general(dp_ref[...], w_v[...], NT_DIMS, preferred_element_type=F32)

        def dnorm(rows):
            dh = dh_s[rows, :]
            xv = x_ref[rows, :]
            r = lax.rsqrt(_mean_lanes(xv * xv) + RMS_EPS)
            gd = dh * g_ref[...]
            dx_ref[rows, :] = dx1_ref[rows, :] + r * gd - xv * (r * r * r * _mean_lanes(gd * xv))
            acc_g[...] += _fold8(dh * xv * r)
        _for_chunks(tm, ROW_CHUNK, dnorm)
        _write_row_sums(acc_g, dg_ref, n_steps)

    return _call(
        body, name="dx_bwd", grid=(n_steps,),
        in_specs=[_rows(tm, ni), _rows(tm, d), _rows(tm, d), _const((1, d)), ANY],
        out_specs=[_rows(tm, d), _const((1, d))],
        out_shape=[jax.ShapeDtypeStruct((t, d), F32), jax.ShapeDtypeStruct((1, d), F32)],
        scratch_shapes=[pltpu.VMEM((d, ni), BF16), pltpu.VMEM((tm, d), F32),
                        pltpu.VMEM((SUBLANES, d), F32), pltpu.SemaphoreType.DMA((nb,))],
        args=(dproj, x, dx1, g1pre, w_in_g), ex=ex)


def _tn_matmul(a, g, nblk, a_cols, g_cols, a_blocked, g_blocked, tt, name, ex=None):
    t = a.shape[0]

    def body(a_ref, g_ref, o_ref):
        @pl.when(pl.program_id(1) == 0)
        def _():
            o_ref[...] = jnp.zeros_like(o_ref)
        o_ref[0] += lax.dot_general(a_ref[...], g_ref[...], TN_DIMS, preferred_element_type=F32)

    (out,), xouts = _call(
        body, name=name, grid=(nblk, t // tt),
        in_specs=[pl.BlockSpec((tt, a_cols), (lambda b, s: (s, b)) if a_blocked else (lambda b, s: (s, 0))),
                  pl.BlockSpec((tt, g_cols), (lambda b, s: (s, b)) if g_blocked else (lambda b, s: (s, 0)))],
        out_specs=[pl.BlockSpec((1, a_cols, g_cols), lambda b, s: (b, 0, 0))],
        out_shape=[jax.ShapeDtypeStruct((nblk, a_cols, g_cols), F32)],
        scratch_shapes=[], args=(a, g), ex=ex)
    return out, xouts


def _pair_sum_call(g_full, from_sibling, core, name):
    nblk, r, c = g_full.shape
    hr = r // 2
    tr = min(hr, 256)
    n = hr // tr

    def body(core_ref, g_ref, p_ref, o_ref):
        o_ref[...] = (g_ref[...] + p_ref[...]).astype(BF16)

    return pl.pallas_call(
        body, name=name,
        grid_spec=pltpu.PrefetchScalarGridSpec(
            num_scalar_prefetch=1, grid=(nblk, n),
            in_specs=[pl.BlockSpec((1, tr, c), lambda j, i, cr: (j, cr[0] * n + i, 0)),
                      pl.BlockSpec((1, tr, c), lambda j, i, cr: (j, i, 0))],
            out_specs=pl.BlockSpec((1, tr, c), lambda j, i, cr: (j, i, 0))),
        out_shape=jax.ShapeDtypeStruct((nblk, hr, c), BF16),
        compiler_params=_params("parallel", "parallel"))(core, g_full, from_sibling)


def _chip_sum_call(pair, received, chip_core, name):
    _, hr, c = pair.shape
    tr = min(hr, 256)
    n = hr // tr

    def body(cc_ref, own_ref, r_ref, o_ref):
        o_ref[...] = ((own_ref[0].astype(F32) + r_ref[0].astype(F32)) + r_ref[1].astype(F32)) + r_ref[2].astype(F32)

    return pl.pallas_call(
        body, name=name,
        grid_spec=pltpu.PrefetchScalarGridSpec(
            num_scalar_prefetch=1, grid=(n,),
            in_specs=[pl.BlockSpec((1, tr, c), lambda i, cc: (cc[0], i, 0)),
                      pl.BlockSpec((N_CHIPS - 1, tr, c), lambda i, cc: (0, i, 0))],
            out_specs=pl.BlockSpec((tr, c), lambda i, cc: (cc[1] * n + i, 0))),
        out_shape=jax.ShapeDtypeStruct((2 * hr, c), F32),
        compiler_params=_params("parallel"))(chip_core, pair, received)


def _adamw(w, g, m, v):
    m = ADAM_B1 * m + (1.0 - ADAM_B1) * g
    v = ADAM_B2 * v + (1.0 - ADAM_B2) * (g * g)
    m_hat = m / (1.0 - ADAM_B1 ** ADAM_STEP)
    v_hat = v / (1.0 - ADAM_B2 ** ADAM_STEP)
    delta = -ADAM_LR * (m_hat / (jnp.sqrt(v_hat) + ADAM_EPS) + ADAM_WD * w)
    return delta, m, v


def _adam_call(w, g, m, v, name):
    r, c = w.shape
    tr = min(r, 256)

    def body(w_ref, g_ref, m_ref, v_ref, d_ref, mo_ref, vo_ref):
        d_ref[...], mo_ref[...], vo_ref[...] = _adamw(w_ref[...], g_ref[...], m_ref[...], v_ref[...])

    shape = jax.ShapeDtypeStruct((r, c), F32)
    return pl.pallas_call(
        body, name=name, grid=(r // tr,), in_specs=[_rows(tr, c)] * 4, out_specs=[_rows(tr, c)] * 3,
        out_shape=[shape] * 3, compiler_params=_params("parallel"))(w, g, m, v)


def _place():
    return lax.axis_index("x"), lax.axis_index("y"), lax.axis_index("c")


def _other_chips(x, y):
    rel = [(x, 1 - y), (1 - x, y), (1 - x, 1 - y)]
    return [(px, py, 2 * px + py) for px, py in rel]


def _gather_weights_call(shards):
    n = len(shards)

    def body(*refs):
        ins, outs = refs[:n], refs[n:2 * n]
        ici_send, ici_recv, fwd_send, fwd_recv, own_sem = refs[2 * n:]
        x, y, c = _place()
        me = 2 * x + y
        chips = _other_chips(x, y)
        started = []
        own = [pltpu.make_async_copy(ins[a], outs[a].at[me], own_sem.at[a]) for a in range(n)]
        for cp in own:
            cp.start()

        def half(ref, chip, core):
            hr = ref.shape[1] // 2
            return ref.at[chip, pl.ds(core * hr, hr), :]

        for a in range(n):
            hr = ins[a].shape[0] // 2
            for j, (px, py, _) in enumerate(chips):
                cp = pltpu.make_async_remote_copy(
                    src_ref=ins[a].at[pl.ds(c * hr, hr), :], dst_ref=half(outs[a], me, c),
                    send_sem=ici_send.at[a, j], recv_sem=ici_recv.at[a, j],
                    device_id=(px, py, c), device_id_type=MESH)
                cp.start()
                started.append(cp)
        for a in range(n):
            for j, (px, py, pk) in enumerate(chips):
                landed = half(outs[a], pk, c)
                pltpu.make_async_remote_copy(
                    src_ref=landed, dst_ref=landed, send_sem=ici_send.at[a, j], recv_sem=ici_recv.at[a, j],
                    device_id=(px, py, c), device_id_type=MESH).wait_recv()
                cp = pltpu.make_async_remote_copy(
                    src_ref=landed, dst_ref=landed, send_sem=fwd_send.at[a, j], recv_sem=fwd_recv.at[a, j],
                    device_id=(x, y, 1 - c), device_id_type=MESH)
                cp.start()
                started.append(cp)
        for a in range(n):
            for j, (px, py, pk) in enumerate(chips):
                passed = half(outs[a], pk, 1 - c)
                pltpu.make_async_remote_copy(
                    src_ref=passed, dst_ref=passed, send_sem=fwd_send.at[a, j], recv_sem=fwd_recv.at[a, j],
                    device_id=(x, y, 1 - c), device_id_type=MESH).wait_recv()
        for cp in started:
            cp.wait_send()
        for cp in own:
            cp.wait()

    return pl.pallas_call(
        body, name="gather_weights", in_specs=[ANY] * n, out_specs=[ANY] * n,
        out_shape=[jax.ShapeDtypeStruct((N_CHIPS,) + s.shape, s.dtype) for s in shards],
        scratch_shapes=[pltpu.SemaphoreType.DMA((n, N_CHIPS - 1)), pltpu.SemaphoreType.DMA((n, N_CHIPS - 1)),
                        pltpu.SemaphoreType.DMA((n, N_CHIPS - 1)), pltpu.SemaphoreType.DMA((n, N_CHIPS - 1)),
                        pltpu.SemaphoreType.DMA((n,))],
        compiler_params=pltpu.CompilerParams(has_side_effects=True))(*shards)


def _gather_conv_weights(conv_a_w, conv_b_w, d):
    ka, dq = conv_a_w.shape
    kb = conv_b_w.shape[0]
    ra = -(-ka // SUBLANES) * SUBLANES
    rb = -(-kb // SUBLANES) * SUBLANES
    a_pad = jnp.pad(conv_a_w, ((0, ra - ka), (0, 0)))
    b_pad = jnp.pad(conv_b_w, ((0, rb - kb), (0, 0)))

    def body(a_ref, b_ref, oa_ref, ob_ref, pack, slots, send_sem, recv_sem):
        x, y, c = _place()
        me = 2 * x + y
        chips = _other_chips(x, y)
        pack[pl.ds(0, ra), :] = a_ref[...]
        pack[pl.ds(ra, rb), :] = b_ref[...]
        copies = []
        for j, (px, py, _) in enumerate(chips):
            cp = pltpu.make_async_remote_copy(
                src_ref=pack, dst_ref=slots.at[me], send_sem=send_sem.at[j], recv_sem=recv_sem.at[j],
                device_id=(px, py, c), device_id_type=MESH)
            cp.start()
            copies.append(cp)
        for j, (px, py, pk) in enumerate(chips):
            pltpu.make_async_remote_copy(
                src_ref=pack, dst_ref=slots.at[pk], send_sem=send_sem.at[j], recv_sem=recv_sem.at[j],
                device_id=(px, py, c), device_id_type=MESH).wait_recv()
        for cp in copies:
            cp.wait_send()
        slots[me] = pack[...]
        for k in range(N_CHIPS):
            oa_ref[:, pl.ds(k * dq, dq)] = slots[k, pl.ds(0, ra), :]
            ob_ref[:, pl.ds(k * dq, dq)] = slots[k, pl.ds(ra, rb), :]

    oa, ob = pl.pallas_call(
        body, name="gather_conv_weights", in_specs=[VMEM_FULL] * 2, out_specs=[VMEM_FULL] * 2,
        out_shape=[jax.ShapeDtypeStruct((ra, d), F32), jax.ShapeDtypeStruct((rb, d), F32)],
        scratch_shapes=[pltpu.VMEM((ra + rb, dq), F32), pltpu.VMEM((N_CHIPS, ra + rb, dq), F32),
                        pltpu.SemaphoreType.DMA((N_CHIPS - 1,)), pltpu.SemaphoreType.DMA((N_CHIPS - 1,))],
        compiler_params=pltpu.CompilerParams(has_side_effects=True))(a_pad, b_pad)
    return oa[:ka], ob[:kb]


def _sibling_halves_call(grads):
    n = len(grads)

    def body(*refs):
        ins, outs = refs[:n], refs[n:2 * n]
        send_sem, recv_sem = refs[2 * n:]
        x, y, c = _place()
        copies = []
        for a in range(n):
            hr = ins[a].shape[1] // 2
            cp = pltpu.make_async_remote_copy(
                src_ref=ins[a].at[:, pl.ds((1 - c) * hr, hr), :], dst_ref=outs[a],
                send_sem=send_sem.at[a], recv_sem=recv_sem.at[a],
                device_id=(x, y, 1 - c), device_id_type=MESH)
            cp.start()
            copies.append(cp)
        for cp in copies:
            cp.wait()

    return pl.pallas_call(
        body, name="grads_to_sibling", in_specs=[ANY] * n, out_specs=[ANY] * n,
        out_shape=[jax.ShapeDtypeStruct((g.shape[0], g.shape[1] // 2, g.shape[2]), g.dtype) for g in grads],
        scratch_shapes=[pltpu.SemaphoreType.DMA((n,)), pltpu.SemaphoreType.DMA((n,))],
        compiler_params=pltpu.CompilerParams(has_side_effects=True))(*grads)


def _scatter_to_owner_call(pairs):
    n = len(pairs)

    def body(*refs):
        ins, outs = refs[:n], refs[n:2 * n]
        send_sem, recv_sem = refs[2 * n:]
        x, y, c = _place()
        chips = _other_chips(x, y)
        copies = []
        for a in range(n):
            for j, (px, py, pk) in enumerate(chips):
                cp = pltpu.make_async_remote_copy(
                    src_ref=ins[a].at[pk], dst_ref=outs[a].at[j],
                    send_sem=send_sem.at[a, j], recv_sem=recv_sem.at[a, j],
                    device_id=(px, py, c), device_id_type=MESH)
                cp.start()
                copies.append(cp)
        for cp in copies:
            cp.wait()

    return pl.pallas_call(
        body, name="grads_to_owner", in_specs=[ANY] * n, out_specs=[ANY] * n,
        out_shape=[jax.ShapeDtypeStruct((N_CHIPS - 1,) + p.shape[1:], p.dtype) for p in pairs],
        scratch_shapes=[pltpu.SemaphoreType.DMA((n, N_CHIPS - 1)), pltpu.SemaphoreType.DMA((n, N_CHIPS - 1))],
        compiler_params=pltpu.CompilerParams(has_side_effects=True))(*pairs)


def _share_halves_call(halves):
    n = len(halves)

    def body(*refs):
        ins, outs = refs[:n], refs[n:2 * n]
        send_sem, recv_sem, own_sem = refs[2 * n:]
        x, y, c = _place()
        copies, own = [], []
        for a in range(n):
            hr = ins[a].shape[0]
            mine = outs[a].at[pl.ds(c * hr, hr), :]
            cp = pltpu.make_async_copy(ins[a], mine, own_sem.at[a])
            cp.start()
            own.append(cp)
            cp = pltpu.make_async_remote_copy(
                src_ref=ins[a], dst_ref=mine, send_sem=send_sem.at[a], recv_sem=recv_sem.at[a],
                device_id=(x, y, 1 - c), device_id_type=MESH)
            cp.start()
            copies.append(cp)
        for a in range(n):
            hr = ins[a].shape[0]
            theirs = outs[a].at[pl.ds((1 - c) * hr, hr), :]
            pltpu.make_async_remote_copy(
                src_ref=ins[a], dst_ref=theirs, send_sem=send_sem.at[a], recv_sem=recv_sem.at[a],
                device_id=(x, y, 1 - c), device_id_type=MESH).wait_recv()
        for cp in copies:
            cp.wait_send()
        for cp in own:
            cp.wait()

    return pl.pallas_call(
        body, name="halves_to_sibling", in_specs=[ANY] * n, out_specs=[ANY] * n,
        out_shape=[jax.ShapeDtypeStruct((2 * h.shape[0], h.shape[1]), h.dtype) for h in halves],
        scratch_shapes=[pltpu.SemaphoreType.DMA((n,)), pltpu.SemaphoreType.DMA((n,)),
                        pltpu.SemaphoreType.DMA((n,))],
        compiler_params=pltpu.CompilerParams(has_side_effects=True))(*halves)


def _small_step_call(partials, loss_rows, weights, m_s, v_s, sharded, d):
    n = len(partials)
    row_counts = [p.shape[0] for p in partials]
    starts = [sum(row_counts[:i]) for i in range(n)]
    loss_row = sum(row_counts)
    pack_rows = -(-(loss_row + 1) // SUBLANES) * SUBLANES
    dq = d // N_CHIPS

    def body(*refs):
        p_refs = refs[:n]
        loss_in = refs[n]
        w_refs = refs[n + 1:2 * n + 1]
        m_refs = refs[2 * n + 1:3 * n + 1]
        v_refs = refs[3 * n + 1:4 * n + 1]
        o = 4 * n + 1
        g_out = refs[o:o + n]
        d_out = refs[o + n:o + 2 * n]
        m_out = refs[o + 2 * n:o + 3 * n]
        v_out = refs[o + 3 * n:o + 4 * n]
        loss_out = refs[o + 4 * n]
        pack, slots, send_sem, recv_sem = refs[o + 4 * n + 1:]
        x, y, c = _place()
        me = 4 * x + 2 * y + c

        pack[...] = jnp.zeros_like(pack)
        for i in range(n):
            pack[pl.ds(starts[i], row_counts[i]), :] = p_refs[i][...]
        pack[pl.ds(loss_row, 1), :] = loss_in[...]

        copies = []
        for rel in range(1, N_DEV):
            peer = (x ^ (rel >> 2), y ^ ((rel >> 1) & 1), c ^ (rel & 1))
            cp = pltpu.make_async_remote_copy(
                src_ref=pack, dst_ref=slots.at[me], send_sem=send_sem.at[rel - 1], recv_sem=recv_sem.at[rel - 1],
                device_id=peer, device_id_type=MESH)
            cp.start()
            copies.append(cp)
        for rel in range(1, N_DEV):
            peer_idx = me ^ rel
            pltpu.make_async_remote_copy(
                src_ref=pack, dst_ref=slots.at[peer_idx], send_sem=send_sem.at[rel - 1],
                recv_sem=recv_sem.at[rel - 1], device_id=(x, y, c), device_id_type=MESH).wait_recv()
        for cp in copies:
            cp.wait_send()

        slots[me] = pack[...]
        total = slots[0]
        for dev in range(1, N_DEV):
            total = total + slots[dev]
        pack[...] = total

        loss_out[...] = jnp.broadcast_to(
            (0.5 / d) * jnp.sum(pack[pl.ds(loss_row, 1), :], axis=-1, keepdims=True), loss_out.shape)
        chip = 2 * x + y
        for i in range(n):
            rows = pl.ds(starts[i], row_counts[i])
            if sharded[i]:
                for k in range(N_CHIPS):
                    @pl.when(chip == k)
                    def _():
                        g_out[i][...] = pack[rows, pl.ds(k * dq, dq)]
            else:
                g_out[i][...] = pack[rows, :]
            d_out[i][...], m_out[i][...], v_out[i][...] = _adamw(
                w_refs[i][...], g_out[i][...], m_refs[i][...], v_refs[i][...])

    w_shapes = [jax.ShapeDtypeStruct(w.shape, F32) for w in weights]
    n_in = 4 * n + 1
    return pl.pallas_call(
        body, name="small_grads_allreduce_adamw",
        in_specs=[VMEM_FULL] * n_in, out_specs=[VMEM_FULL] * (4 * n + 1),
        out_shape=w_shapes * 4 + [jax.ShapeDtypeStruct((SUBLANES, 128), F32)],
        scratch_shapes=[pltpu.VMEM((pack_rows, d), F32), pltpu.VMEM((N_DEV, pack_rows, d), F32),
                        pltpu.SemaphoreType.DMA((N_DEV - 1,)), pltpu.SemaphoreType.DMA((N_DEV - 1,))],
        compiler_params=pltpu.CompilerParams(has_side_effects=True, vmem_limit_bytes=VMEM_LIMIT))(
            *partials, loss_rows, *weights, *m_s, *v_s)


def _tile(t, want):
    return min(t, want)


def kernel(x, norm1_pre_g, w_in, b_in, conv_a_w, conv_a_b, w_a_out, conv_b_w, conv_b_b, ln_b_g, ln_b_b, w_b_out, w_o, norm1_post_g, norm2_pre_g, w_mlp_in, w_mlp_out, norm2_post_g, loss_target, m_norm1_pre_g, m_w_in, m_b_in, m_conv_a_w, m_conv_a_b, m_w_a_out, m_conv_b_w, m_conv_b_b, m_ln_b_g, m_ln_b_b, m_w_b_out, m_w_o, m_norm1_post_g, m_norm2_pre_g, m_w_mlp_in, m_w_mlp_out, m_norm2_post_g, v_norm1_pre_g, v_w_in, v_b_in, v_conv_a_w, v_conv_a_b, v_w_a_out, v_conv_b_w, v_conv_b_b, v_ln_b_g, v_ln_b_b, v_w_b_out, v_w_o, v_norm1_post_g, v_norm2_pre_g, v_w_mlp_in, v_w_mlp_out, v_norm2_post_g):
    _, t, d = x.shape
    xt = x.reshape(t, d)
    tgt = loss_target.reshape(t, d)
    row = lambda vec: vec.reshape(1, -1)
    cx, cy, cc = _place()
    core = cc.astype(jnp.int32).reshape(1)
    chip = (2 * cx + cy).astype(jnp.int32).reshape(1)

    big = dict(w_in=w_in, w_a_out=w_a_out, w_b_out=w_b_out, w_o=w_o, w_mlp_in=w_mlp_in, w_mlp_out=w_mlp_out)
    names = list(big)
    chip_core = jnp.concatenate([chip, core])
    slot = {k: _cast_to_slot(big[k], chip, "cast_" + k) for k in names}
    mixer_w, mlp_w = ["w_a_out", "w_b_out", "w_o"], ["w_mlp_in", "w_mlp_out"]
    rows_of = lambda buf: buf.reshape(-1, buf.shape[-1])

    def pair_sums(keys, full, from_sibling):
        return [_pair_sum_call(g, p, core, "pair_sum_" + k) for k, g, p in zip(keys, full, from_sibling)]

    def chip_sums(keys, pairs, received):
        return [_chip_sum_call(p, r, chip_core, "chip_sum_" + k) for k, p, r in zip(keys, pairs, received)]

    (w_in_g,) = _exchange_call("gather_w_in", [_ex_gather_ici([slot["w_in"]]), _ex_gather_forward([slot["w_in"]])])

    g1pre, g1post, g2pre, g2post = row(norm1_pre_g), row(norm1_post_g), row(norm2_pre_g), row(norm2_post_g)
    lng, lnb, ba, bb = row(ln_b_g), row(ln_b_b), row(conv_a_b), row(conv_b_b)
    conv_a_full, conv_b_full = _gather_conv_weights(conv_a_w, conv_b_w, d)

    (h, ua, ub, bg, cg, ha, a, sg, sa, sb), landed = _proj_call(
        xt, g1pre, w_in_g, row(b_in), _tile(t, 512), ex=_ex_gather_ici([slot[k] for k in mixer_w + mlp_w]))
    w_a_g, w_b_g, w_o_g = _exchange_call("forward_mixer_weights", [_ex_gather_forward(landed[:3])])
    w_a_full, w_b_full, w_o_full = rows_of(w_a_g), rows_of(w_b_g), rows_of(w_o_g)
    (x1, va, pa, cb, sbo, ya, yb, mg, mix), (w1_g, w2_g) = _mixer_fwd_call(
        ua, ub, bg, sa, sb, xt, conv_a_full, ba, conv_b_full, bb, lng, lnb,
        w_a_full, w_b_full, w_o_full, g1post, _tile(t, 256), ex=_ex_gather_forward(landed[3:]))
    (dx1, f, df2, h2, df1, dg2post, dg2pre, loss_rows), _ = _mlp_call(
        x1, tgt, g2pre, g2post, w1_g, rows_of(w2_g), _tile(t, 256))

    tt = _tile(t, 1024)
    n4, fq, dq = w_in.shape[1], w_mlp_in.shape[1], d // N_CHIPS
    g_mlp = [_tn_matmul(h2, df1, N_CHIPS, d, fq, False, True, tt, "dw_mlp_in")[0],
             _tn_matmul(f, df2, N_CHIPS, fq, d, True, False, tt, "dw_mlp_out")[0]]
    (dmix, dya, dyb, dva, dcb, dbg, dza, dzb, dg1post, dlng, dlnb, dba, dbb), sib_mlp = _mixer_bwd_call(
        dx1, mix, sa, sb, ya, yb, bg, va, cb, g1post, lng, lnb, w_a_full, w_b_full, w_o_full, _tile(t, 256),
        ex=_ex_sibling_halves(g_mlp))
    p_mlp = pair_sums(mlp_w, g_mlp, sib_mlp)
    g_mix = [_tn_matmul(pa, dya, 1, d, d, False, False, tt, "dw_a_out")[0].reshape(N_CHIPS, dq, d),
             _tn_matmul(sbo, dyb, 1, d, d, False, False, tt, "dw_b_out")[0].reshape(N_CHIPS, dq, d),
             _tn_matmul(mg, dmix, 1, d, d, False, False, tt, "dw_o")[0].reshape(N_CHIPS, dq, d)]
    ex_a, ex_b = _ex_scatter_to_owner(p_mlp), _ex_sibling_halves(g_mix)
    (dproj, dwa_conv, dwb_conv, dbin), xo = _conv_bwd_call(
        dva, dcb, ua, ub, cg, ha, a, sg, dbg, dza, dzb, conv_a_full, conv_b_full, _tile(t, 256),
        ex=_merge(ex_a, ex_b))
    recv_mlp, sib_mix = _split(xo, ex_a, ex_b)
    r_mlp = chip_sums(mlp_w, p_mlp, recv_mlp)
    p_mix = pair_sums(mixer_w, g_mix, sib_mix)
    ex_a, ex_b = _ex_share_halves(r_mlp), _ex_scatter_to_owner(p_mix)
    g_in, xo = _tn_matmul(h, dproj, N_CHIPS, d, n4, False, True, tt, "dw_in", ex=_merge(ex_a, ex_b))
    red_mlp, recv_mix = _split(xo, ex_a, ex_b)
    r_mix = chip_sums(mixer_w, p_mix, recv_mix)
    ex_a, ex_b = _ex_sibling_halves([g_in]), _ex_share_halves(r_mix)
    (grad_x, dg1pre), xo = _dx_call(dproj, xt, dx1, g1pre, w_in_g, _tile(t, 512), ex=_merge(ex_a, ex_b))
    sib_in, red_mix = _split(xo, ex_a, ex_b)
    p_in = pair_sums(["w_in"], [g_in], sib_in)
    recv_in = _exchange_call("w_in_grad_to_owner", [_ex_scatter_to_owner(p_in)])
    r_in = chip_sums(["w_in"], p_in, recv_in)
    red_in = _exchange_call("w_in_grad_to_sibling", [_ex_share_halves(r_in)])
    reduced = dict(zip(mlp_w + mixer_w + ["w_in"], red_mlp + red_mix + red_in))

    moments = dict(w_in=(m_w_in, v_w_in), w_a_out=(m_w_a_out, v_w_a_out), w_b_out=(m_w_b_out, v_w_b_out),
                   w_o=(m_w_o, v_w_o), w_mlp_in=(m_w_mlp_in, v_w_mlp_in), w_mlp_out=(m_w_mlp_out, v_w_mlp_out))
    out = {}
    for k in names:
        delta, new_m, new_v = _adam_call(big[k], reduced[k], *moments[k], "adamw_" + k)
        out[k] = (reduced[k], delta, new_m, new_v)

    small = [
        ("conv_b_w", dwb_conv, conv_b_w, m_conv_b_w, v_conv_b_w, True),
        ("conv_b_b", dbb, bb, row(m_conv_b_b), row(v_conv_b_b), False),
        ("b_in", dbin.reshape(7, d), b_in.reshape(7, d), m_b_in.reshape(7, d), v_b_in.reshape(7, d), False),
        ("norm1_pre_g", dg1pre, row(norm1_pre_g), row(m_norm1_pre_g), row(v_norm1_pre_g), False),
        ("conv_a_w", dwa_conv, conv_a_w, m_conv_a_w, v_conv_a_w, True),
        ("conv_a_b", dba, ba, row(m_conv_a_b), row(v_conv_a_b), False),
        ("ln_b_g", dlng, lng, row(m_ln_b_g), row(v_ln_b_g), False),
        ("ln_b_b", dlnb, lnb, row(m_ln_b_b), row(v_ln_b_b), False),
        ("norm1_post_g", dg1post, g1post, row(m_norm1_post_g), row(v_norm1_post_g), False),
        ("norm2_pre_g", dg2pre, g2pre, row(m_norm2_pre_g), row(v_norm2_pre_g), False),
        ("norm2_post_g", dg2post, g2post, row(m_norm2_post_g), row(v_norm2_post_g), False),
    ]
    res = _small_step_call([s[1] for s in small], loss_rows, [s[2] for s in small], [s[3] for s in small],
                           [s[4] for s in small], [s[5] for s in small], d)
    ns = len(small)
    loss = res[4 * ns][0, 0]
    shapes = dict(norm1_pre_g=norm1_pre_g.shape, b_in=b_in.shape, conv_a_w=conv_a_w.shape,
                  conv_a_b=conv_a_b.shape, conv_b_w=conv_b_w.shape, conv_b_b=conv_b_b.shape,
                  ln_b_g=ln_b_g.shape, ln_b_b=ln_b_b.shape, norm1_post_g=norm1_post_g.shape,
                  norm2_pre_g=norm2_pre_g.shape, norm2_post_g=norm2_post_g.shape)
    for i, s in enumerate(small):
        out[s[0]] = tuple(res[q * ns + i].reshape(shapes[s[0]]) for q in range(4))

    order = ["norm1_pre_g", "w_in", "b_in", "conv_a_w", "conv_a_b", "w_a_out", "conv_b_w", "conv_b_b",
             "ln_b_g", "ln_b_b", "w_b_out", "w_o", "norm1_post_g", "norm2_pre_g", "w_mlp_in", "w_mlp_out",
             "norm2_post_g"]
    return (loss, grad_x.reshape(x.shape), *[out[k][0] for k in order], *[out[k][1] for k in order],
            *[out[k][2] for k in order], *[out[k][3] for k in order])
```

```python
import functools

import jax
import jax.numpy as jnp
from jax import lax
from jax.experimental import pallas as pl
from jax.experimental.pallas import tpu as pltpu

RMS_EPS = 1e-6
LN_EPS = 1e-5
ADAM_LR = 0.001
ADAM_B1 = 0.9
ADAM_B2 = 0.999
ADAM_EPS = 1e-08
ADAM_WD = 0.01
ADAM_STEP = 10

F32 = jnp.float32
BF16 = jnp.bfloat16
MESH = pl.DeviceIdType.MESH
ANY = pl.BlockSpec(memory_space=pl.ANY)
VMEM_FULL = pl.BlockSpec(memory_space=pltpu.VMEM)

V7X_VMEM_BYTES = 64 * 1024 * 1024
VMEM_LIMIT = V7X_VMEM_BYTES - 8 * 1024 * 1024
SUBLANES = 8
N_CHIPS = 4
N_DEV = 8
HALO_A = 8
HALO_B = 16
CONV_ROWS = 16
ROW_CHUNK = 32

NT_DIMS = (((1,), (1,)), ((), ()))
TN_DIMS = (((0,), (0,)), ((), ()))


def _params(*sem):
    return pltpu.CompilerParams(dimension_semantics=sem, vmem_limit_bytes=VMEM_LIMIT)


def _rows(tm, d):
    return pl.BlockSpec((tm, d), lambda i: (i, 0))


def _const(shape):
    return pl.BlockSpec(shape, lambda i: (0,) * len(shape))


def _halo_prev(tm, hb, d):
    return pl.BlockSpec((hb, d), lambda i: (jnp.maximum(i * (tm // hb) - 1, 0), 0))


def _halo_next(tm, hb, d, t):
    return pl.BlockSpec((hb, d), lambda i: (jnp.minimum((i + 1) * (tm // hb), t // hb - 1), 0))


def _for_chunks(n_rows, rc, fn):
    for r0 in range(0, n_rows, rc):
        fn(pl.ds(r0, rc))


def _fold8(v):
    return v.reshape(v.shape[0] // SUBLANES, SUBLANES, v.shape[1]).sum(axis=0)


def _mean_lanes(v):
    return jnp.mean(v, axis=-1, keepdims=True)


def _load_blocks_once(w_hbm, w_vmem, sem):
    nb, _, n = w_hbm.shape

    @pl.when(pl.program_id(0) == 0)
    def _():
        copies = [pltpu.make_async_copy(w_hbm.at[j], w_vmem.at[:, pl.ds(j * n, n)], sem.at[j])
                  for j in range(nb)]
        for cp in copies:
            cp.start()
        for cp in copies:
            cp.wait()


def _load_once(w_hbm, w_vmem, sem):
    @pl.when(pl.program_id(0) == 0)
    def _():
        cp = pltpu.make_async_copy(w_hbm, w_vmem, sem)
        cp.start()
        cp.wait()


def _write_row_sums(acc_ref, out_ref, n_steps):
    @pl.when(pl.program_id(0) == n_steps - 1)
    def _():
        out_ref[...] = jnp.sum(acc_ref[...], axis=0, keepdims=True)


def _place():
    return lax.axis_index("x"), lax.axis_index("y"), lax.axis_index("c")


def _other_chips(x, y):
    rel = [(x, 1 - y), (1 - x, y), (1 - x, 1 - y)]
    return [(px, py, 2 * px + py) for px, py in rel]


class _Exchange:
    def __init__(self, inputs, out_shapes, aliases, n_sems, copies):
        self.inputs = list(inputs)
        self.out_shapes = list(out_shapes)
        self.aliases = dict(aliases)
        self.n_sems = n_sems
        self.copies = copies


def _remote(src, dst, send, recv, device):
    return pltpu.make_async_remote_copy(src_ref=src, dst_ref=dst, send_sem=send, recv_sem=recv,
                                        device_id=device, device_id_type=MESH)


def _sds(a):
    return jax.ShapeDtypeStruct(a.shape, a.dtype)


def _ex_gather_ici(bufs):
    n = len(bufs)

    def copies(xin, xout, send, recv):
        x, y, c = _place()
        me = 2 * x + y
        out = []
        for a in range(n):
            hr = xin[a].shape[1] // 2
            rows = pl.ds(c * hr, hr)
            for j, (px, py, _) in enumerate(_other_chips(x, y)):
                k = a * (N_CHIPS - 1) + j
                out.append(_remote(xin[a].at[me, rows, :], xout[a].at[me, rows, :], send(k), recv(k), (px, py, c)))
        return out

    return _Exchange(bufs, [_sds(b) for b in bufs], {a: a for a in range(n)}, n * (N_CHIPS - 1), copies)


def _ex_gather_forward(bufs):
    n = len(bufs)

    def copies(xin, xout, send, recv):
        x, y, c = _place()
        out = []
        for a in range(n):
            hr = xin[a].shape[1] // 2
            rows = pl.ds(c * hr, hr)
            for j, (_, _, pk) in enumerate(_other_chips(x, y)):
                k = a * (N_CHIPS - 1) + j
                out.append(_remote(xin[a].at[pk, rows, :], xout[a].at[pk, rows, :], send(k), recv(k), (x, y, 1 - c)))
        return out

    return _Exchange(bufs, [_sds(b) for b in bufs], {a: a for a in range(n)}, n * (N_CHIPS - 1), copies)


def _ex_sibling_halves(grads):
    n = len(grads)

    def copies(xin, xout, send, recv):
        x, y, c = _place()
        out = []
        for a in range(n):
            hr = xin[a].shape[1] // 2
            out.append(_remote(xin[a].at[:, pl.ds((1 - c) * hr, hr), :], xout[a], send(a), recv(a), (x, y, 1 - c)))
        return out

    shapes = [jax.ShapeDtypeStruct((g.shape[0], g.shape[1] // 2, g.shape[2]), g.dtype) for g in grads]
    return _Exchange(grads, shapes, {}, n, copies)


def _ex_scatter_to_owner(pairs):
    n = len(pairs)

    def copies(xin, xout, send, recv):
        x, y, c = _place()
        out = []
        for a in range(n):
            for j, (px, py, pk) in enumerate(_other_chips(x, y)):
                k = a * (N_CHIPS - 1) + j
                out.append(_remote(xin[a].at[pk], xout[a].at[j], send(k), recv(k), (px, py, c)))
        return out

    shapes = [jax.ShapeDtypeStruct((N_CHIPS - 1,) + p.shape[1:], p.dtype) for p in pairs]
    return _Exchange(pairs, shapes, {}, n * (N_CHIPS - 1), copies)


def _ex_share_halves(reduced):
    n = len(reduced)

    def copies(xin, xout, send, recv):
        x, y, c = _place()
        out = []
        for a in range(n):
            hr = xin[a].shape[0] // 2
            rows = pl.ds(c * hr, hr)
            out.append(_remote(xin[a].at[rows, :], xout[a].at[rows, :], send(a), recv(a), (x, y, 1 - c)))
        return out

    return _Exchange(reduced, [_sds(r) for r in reduced], {a: a for a in range(n)}, n, copies)


def _merge(*exs):
    exs = [e for e in exs if e is not None]
    if not exs:
        return None
    inputs, shapes, aliases = [], [], {}
    in_off, out_off, sem_off = [], [], []
    n_sems = 0
    for e in exs:
        in_off.append(len(inputs))
        out_off.append(len(shapes))
        sem_off.append(n_sems)
        aliases.update({len(inputs) + i: len(shapes) + o for i, o in e.aliases.items()})
        inputs += e.inputs
        shapes += e.out_shapes
        n_sems += e.n_sems

    def copies(xin, xout, send, recv):
        out = []
        for e, io, oo, so in zip(exs, in_off, out_off, sem_off):
            out += e.copies(xin[io:io + len(e.inputs)], xout[oo:oo + len(e.out_shapes)],
                            lambda i, so=so: send(so + i), lambda i, so=so: recv(so + i))
        return out

    return _Exchange(inputs, shapes, aliases, n_sems, copies)


def _split(ex_outs, *exs):
    parts, o = [], 0
    for e in exs:
        parts.append(list(ex_outs[o:o + len(e.out_shapes)]))
        o += len(e.out_shapes)
    return parts


def _call(body, *, name, grid, in_specs, out_specs, out_shape, scratch_shapes, args, ex=None):
    n_in, n_out, n_scr = len(in_specs), len(out_specs), len(scratch_shapes)
    seq = ("arbitrary",) * len(grid)
    if ex is None:
        outs = pl.pallas_call(
            body, name=name, grid=grid, in_specs=list(in_specs), out_specs=list(out_specs),
            out_shape=list(out_shape), scratch_shapes=list(scratch_shapes), compiler_params=_params(*seq))(*args)
        return list(outs), []
    n_xi, n_xo = len(ex.inputs), len(ex.out_shapes)

    def full(*refs):
        ins, xin = refs[:n_in], refs[n_in:n_in + n_xi]
        o = n_in + n_xi
        outs, xout = refs[o:o + n_out], refs[o + n_out:o + n_out + n_xo]
        s = o + n_out + n_xo
        scr = refs[s:s + n_scr]
        send_sems, recv_sems = refs[s + n_scr:]
        send = lambda i: send_sems.at[i]
        recv = lambda i: recv_sems.at[i]
        first = functools.reduce(jnp.logical_and, [pl.program_id(a) == 0 for a in range(len(grid))])
        last = functools.reduce(jnp.logical_and, [pl.program_id(a) == grid[a] - 1 for a in range(len(grid))])

        @pl.when(first)
        def _():
            for cp in ex.copies(xin, xout, send, recv):
                cp.start()

        body(*ins, *outs, *scr)

        @pl.when(last)
        def _():
            for cp in ex.copies(xin, xout, send, recv):
                cp.wait()

    res = pl.pallas_call(
        full, name=name, grid=grid, in_specs=list(in_specs) + [ANY] * n_xi,
        out_specs=list(out_specs) + [ANY] * n_xo, out_shape=list(out_shape) + ex.out_shapes,
        scratch_shapes=list(scratch_shapes) + [pltpu.SemaphoreType.DMA((ex.n_sems,)),
                                               pltpu.SemaphoreType.DMA((ex.n_sems,))],
        input_output_aliases={n_in + i: n_out + o for i, o in ex.aliases.items()},
        compiler_params=pltpu.CompilerParams(dimension_semantics=seq, vmem_limit_bytes=VMEM_LIMIT,
                                             has_side_effects=True))(*args, *ex.inputs)
    return list(res[:n_out]), list(res[n_out:])


def _exchange_call(name, phases):
    first = phases[0]
    n_xi, n_xo = len(first.inputs), len(first.out_shapes)

    def body(*refs):
        xin, xout = refs[:n_xi], refs[n_xi:n_xi + n_xo]
        sems = refs[n_xi + n_xo:]
        for p, ex in enumerate(phases):
            send_sems, recv_sems = sems[2 * p], sems[2 * p + 1]
            cps = ex.copies(xin, xout, lambda i: send_sems.at[i], lambda i: recv_sems.at[i])
            for cp in cps:
                cp.start()
            for cp in cps:
                cp.wait()

    sems = []
    for ex in phases:
        sems += [pltpu.SemaphoreType.DMA((ex.n_sems,)), pltpu.SemaphoreType.DMA((ex.n_sems,))]
    return list(pl.pallas_call(
        body, name=name, in_specs=[ANY] * n_xi, out_specs=[ANY] * n_xo, out_shape=first.out_shapes,
        scratch_shapes=sems, input_output_aliases=dict(first.aliases),
        compiler_params=pltpu.CompilerParams(has_side_effects=True))(*first.inputs))


def _cast_to_slot(w, chip, name):
    r, c = w.shape
    tr = min(r, 256)

    def body(chip_ref, w_ref, o_ref):
        o_ref[0] = w_ref[...].astype(BF16)

    return pl.pallas_call(
        body, name=name,
        grid_spec=pltpu.PrefetchScalarGridSpec(
            num_scalar_prefetch=1, grid=(r // tr,),
            in_specs=[pl.BlockSpec((tr, c), lambda i, k: (i, 0))],
            out_specs=pl.BlockSpec((1, tr, c), lambda i, k: (k[0], i, 0))),
        out_shape=jax.ShapeDtypeStruct((N_CHIPS, r, c), BF16),
        compiler_params=_params("parallel"))(chip, w)


def _proj_call(x, g1pre, w_in_g, b_in, tm, ex=None):
    t, d = x.shape
    nb, _, n4 = w_in_g.shape
    ni = nb * n4
    assert ni == 7 * d

    def body(x_ref, g_ref, b_ref, w_hbm, h_ref, ua_ref, ub_ref, bg_ref, cg_ref, ha_ref, a_ref,
             sg_ref, sa_ref, sb_ref, w_v, p0, p1, sem):
        _load_blocks_once(w_hbm, w_v, sem)

        def norm(rows):
            xv = x_ref[rows, :]
            r = lax.rsqrt(_mean_lanes(xv * xv) + RMS_EPS)
            h_ref[rows, :] = (xv * r * g_ref[...]).astype(BF16)
        _for_chunks(tm, ROW_CHUNK, norm)

        def group(i, dst):
            cols = pl.ds(i * d, d)
            dst[...] = jnp.dot(h_ref[...], w_v[:, cols], preferred_element_type=F32) + b_ref[:, cols]

        group(0, p0)

        def bgate(rows):
            bg_ref[rows, :] = p0[rows, :].astype(BF16)
        _for_chunks(tm, ROW_CHUNK, bgate)

        group(1, p0)
        group(2, p1)

        def branch_a(rows):
            cg, ha = p0[rows, :], p1[rows, :]
            ua_ref[rows, :] = cg * ha
            cg_ref[rows, :] = cg.astype(BF16)
            ha_ref[rows, :] = ha.astype(BF16)
        _for_chunks(tm, ROW_CHUNK, branch_a)

        group(3, p0)
        group(4, p1)

        def branch_b(rows):
            a, sg = p0[rows, :], jax.nn.sigmoid(p1[rows, :])
            ub_ref[rows, :] = a * sg
            a_ref[rows, :] = a.astype(BF16)
            sg_ref[rows, :] = sg.astype(BF16)
        _for_chunks(tm, ROW_CHUNK, branch_b)

        group(5, p0)
        group(6, p1)

        def gates(rows):
            sa_ref[rows, :] = jax.nn.sigmoid(p0[rows, :]).astype(BF16)
            sb_ref[rows, :] = jax.nn.sigmoid(p1[rows, :]).astype(BF16)
        _for_chunks(tm, ROW_CHUNK, gates)

    bf = jax.ShapeDtypeStruct((t, d), BF16)
    f32 = jax.ShapeDtypeStruct((t, d), F32)
    return _call(
        body, name="proj_fwd", grid=(t // tm,),
        in_specs=[_rows(tm, d), _const((1, d)), _const((1, ni)), ANY],
        out_specs=[_rows(tm, d)] * 10,
        out_shape=[bf, f32, f32, bf, bf, bf, bf, bf, bf, bf],
        scratch_shapes=[pltpu.VMEM((d, ni), BF16), pltpu.VMEM((tm, d), F32), pltpu.VMEM((tm, d), F32),
                        pltpu.SemaphoreType.DMA((nb,))],
        args=(x, g1pre, b_in, w_in_g), ex=ex)


def _fill_ext(ext, prev_ref, cur_ref, next_ref, hb, tm, i, n_steps):
    ext[pl.ds(0, hb), :] = jnp.where(i > 0, prev_ref[...], 0.0)
    ext[pl.ds(hb, tm), :] = cur_ref[...]
    ext[pl.ds(hb + tm, hb), :] = jnp.where(i < n_steps - 1, next_ref[...], 0.0)


def _shift_plan(offsets):
    shifts = sorted({o % SUBLANES for o in offsets if o % SUBLANES})
    return {s: i for i, s in enumerate(shifts)}


def _shifted_rows(tm, offsets):
    return tm + SUBLANES * max(o // SUBLANES for o in offsets)


def _fill_shifted(ext, sh, plan):
    n = sh.shape[1]
    for s, i in plan.items():
        sh[i, :, :] = ext[pl.ds(s, n), :]


def _window(ext, sh, plan, offset, r0):
    q, s = divmod(offset, SUBLANES)
    if s == 0:
        return ext[pl.ds(offset + r0, CONV_ROWS), :]
    return sh[plan[s], pl.ds(SUBLANES * q + r0, CONV_ROWS), :]


def _mixer_fwd_call(ua, ub, bg, sa, sb, x, conv_a_w, conv_a_b, conv_b_w, conv_b_b, ln_g, ln_b,
                    w_a, w_b, w_o, g1post, tm, ex=None):
    t, d = x.shape
    n_steps = t // tm
    ka, kb = conv_a_w.shape[0], conv_b_w.shape[0]
    off_a = [HALO_A - (ka - 1) // 2 + k for k in range(ka)]
    off_b = [HALO_B - (kb - 1) // 2 + k for k in range(kb)]
    plan_a, plan_b = _shift_plan(off_a), _shift_plan(off_b)

    def body(uap, uac, uan, ubp, ubc, ubn, bg_ref, sa_ref, sb_ref, x_ref, wa_c, ba_c, wb_c, bb_c,
             lng, lnb, wa_hbm, wb_hbm, wo_hbm, g_ref,
             x1_ref, va_ref, pa_ref, cb_ref, sbo_ref, ya_ref, yb_ref, mg_ref, mix_ref,
             ext_a, ext_b, sh_a, sh_b, wa_v, wb_v, wo_v, y0, y1, sem):
        i = pl.program_id(0)
        _load_once(wa_hbm, wa_v, sem.at[0])
        _load_once(wb_hbm, wb_v, sem.at[1])
        _load_once(wo_hbm, wo_v, sem.at[2])
        _fill_ext(ext_a, uap, uac, uan, HALO_A, tm, i, n_steps)
        _fill_ext(ext_b, ubp, ubc, ubn, HALO_B, tm, i, n_steps)
        _fill_shifted(ext_a, sh_a, plan_a)
        _fill_shifted(ext_b, sh_b, plan_b)

        for r0 in range(0, tm, CONV_ROWS):
            rows = pl.ds(r0, CONV_ROWS)
            va = jnp.broadcast_to(ba_c[...], (CONV_ROWS, d))
            for k in range(ka):
                va = va + wa_c[k:k + 1, :] * _window(ext_a, sh_a, plan_a, off_a[k], r0)
            va_ref[rows, :] = va.astype(BF16)
            pa_ref[rows, :] = (bg_ref[rows, :].astype(F32) * va).astype(BF16)
            cb = jnp.broadcast_to(bb_c[...], (CONV_ROWS, d))
            for k in range(kb):
                cb = cb + wb_c[k:k + 1, :] * _window(ext_b, sh_b, plan_b, off_b[k], r0)
            cb_ref[rows, :] = cb
            mu = _mean_lanes(cb)
            cen = cb - mu
            rstd = lax.rsqrt(_mean_lanes(cen * cen) + LN_EPS)
            ln = cen * rstd * lng[...] + lnb[...]
            sbo_ref[rows, :] = (ln * jax.nn.sigmoid(ln)).astype(BF16)

        y0[...] = jnp.dot(pa_ref[...], wa_v[...], preferred_element_type=F32)
        y1[...] = jnp.dot(sbo_ref[...], wb_v[...], preferred_element_type=F32)

        def merge(rows):
            ya, yb = y0[rows, :], y1[rows, :]
            ya_ref[rows, :] = ya.astype(BF16)
            yb_ref[rows, :] = yb.astype(BF16)
            mg_ref[rows, :] = (sa_ref[rows, :].astype(F32) * ya + sb_ref[rows, :].astype(F32) * yb).astype(BF16)
        _for_chunks(tm, ROW_CHUNK, merge)

        mix_ref[...] = jnp.dot(mg_ref[...], wo_v[...], preferred_element_type=F32)

        def resid(rows):
            mix = mix_ref[rows, :]
            r = lax.rsqrt(_mean_lanes(mix * mix) + RMS_EPS)
            x1_ref[rows, :] = x_ref[rows, :] + mix * r * g_ref[...]
        _for_chunks(tm, ROW_CHUNK, resid)

    bf = jax.ShapeDtypeStruct((t, d), BF16)
    f32 = jax.ShapeDtypeStruct((t, d), F32)
    return _call(
        body, name="mixer_fwd", grid=(n_steps,),
        in_specs=[_halo_prev(tm, HALO_A, d), _rows(tm, d), _halo_next(tm, HALO_A, d, t),
                  _halo_prev(tm, HALO_B, d), _rows(tm, d), _halo_next(tm, HALO_B, d, t),
                  _rows(tm, d), _rows(tm, d), _rows(tm, d), _rows(tm, d),
                  _const((ka, d)), _const((1, d)), _const((kb, d)), _const((1, d)),
                  _const((1, d)), _const((1, d)), ANY, ANY, ANY, _const((1, d))],
        out_specs=[_rows(tm, d)] * 9,
        out_shape=[f32, bf, bf, f32, bf, bf, bf, bf, f32],
        scratch_shapes=[pltpu.VMEM((tm + 2 * HALO_A, d), F32), pltpu.VMEM((tm + 2 * HALO_B, d), F32),
                        pltpu.VMEM((len(plan_a), _shifted_rows(tm, off_a), d), F32),
                        pltpu.VMEM((len(plan_b), _shifted_rows(tm, off_b), d), F32),
                        pltpu.VMEM((d, d), BF16), pltpu.VMEM((d, d), BF16), pltpu.VMEM((d, d), BF16),
                        pltpu.VMEM((tm, d), F32), pltpu.VMEM((tm, d), F32),
                        pltpu.SemaphoreType.DMA((3,))],
        args=(ua, ua, ua, ub, ub, ub, bg, sa, sb, x, conv_a_w, conv_a_b, conv_b_w, conv_b_b,
              ln_g, ln_b, w_a, w_b, w_o, g1post), ex=ex)


def _mlp_call(x1, target, g2pre, g2post, w1_g, w2, tm, ex=None):
    t, d = x1.shape
    nb, _, fq = w1_g.shape
    f = nb * fq
    n_steps = t // tm
    inv_d = 1.0 / d

    def body(x1_ref, t_ref, gpre, gpost, w1_hbm, w2_hbm,
             dx1_ref, f_ref, df2_ref, h2_ref, df1_ref, dgpost_ref, dgpre_ref, loss_ref,
             w1_v, w2_v, f1_s, blk_s, f2_s, acc_post, acc_pre, acc_loss, sem):
        _load_blocks_once(w1_hbm, w1_v, sem)
        _load_once(w2_hbm, w2_v, sem.at[nb])

        @pl.when(pl.program_id(0) == 0)
        def _():
            acc_post[...] = jnp.zeros_like(acc_post)
            acc_pre[...] = jnp.zeros_like(acc_pre)
            acc_loss[...] = jnp.zeros_like(acc_loss)

        def norm(rows):
            xv = x1_ref[rows, :]
            r = lax.rsqrt(_mean_lanes(xv * xv) + RMS_EPS)
            h2_ref[rows, :] = (xv * r * gpre[...]).astype(BF16)
        _for_chunks(tm, ROW_CHUNK, norm)

        for j in range(nb):
            cols = pl.ds(j * fq, fq)
            f1_s[:, cols] = jnp.dot(h2_ref[...], w1_v[:, cols], preferred_element_type=F32)

        def act(rows):
            relu = jnp.maximum(f1_s[rows, :], 0.0)
            f_ref[rows, :] = (relu * relu).astype(BF16)
        _for_chunks(tm, ROW_CHUNK, act)

        f2_s[...] = jnp.dot(f_ref[...], w2_v[...], preferred_element_type=F32)

        def head(rows):
            f2 = f2_s[rows, :]
            rf = lax.rsqrt(_mean_lanes(f2 * f2) + RMS_EPS)
            y = x1_ref[rows, :] + f2 * rf * gpost[...]
            err = y - t_ref[rows, :]
            acc_loss[...] += _fold8(err * err)
            dy = err * inv_d
            gdy = dy * gpost[...]
            df2 = rf * gdy - f2 * (rf * rf * rf * _mean_lanes(gdy * f2))
            df2_ref[rows, :] = df2.astype(BF16)
            acc_post[...] += _fold8(dy * f2 * rf)
            dx1_ref[rows, :] = dy
        _for_chunks(tm, ROW_CHUNK, head)

        for j in range(nb):
            cols = pl.ds(j * fq, fq)
            blk_s[...] = lax.dot_general(df2_ref[...], w2_v[cols, :], NT_DIMS, preferred_element_type=F32)

            def dact(rows):
                relu = jnp.maximum(f1_s[rows, cols], 0.0)
                df1_ref[rows, cols] = (blk_s[rows, :] * (2.0 * relu)).astype(BF16)
            _for_chunks(tm, ROW_CHUNK, dact)

        f2_s[...] = lax.dot_general(df1_ref[...], w1_v[...], NT_DIMS, preferred_element_type=F32)

        def dnorm(rows):
            dh2 = f2_s[rows, :]
            xv = x1_ref[rows, :]
            r = lax.rsqrt(_mean_lanes(xv * xv) + RMS_EPS)
            gd = dh2 * gpre[...]
            dx1_ref[rows, :] = dx1_ref[rows, :] + r * gd - xv * (r * r * r * _mean_lanes(gd * xv))
            acc_pre[...] += _fold8(dh2 * xv * r)
        _for_chunks(tm, ROW_CHUNK, dnorm)

        _write_row_sums(acc_post, dgpost_ref, n_steps)
        _write_row_sums(acc_pre, dgpre_ref, n_steps)
        _write_row_sums(acc_loss, loss_ref, n_steps)

    row = jax.ShapeDtypeStruct((1, d), F32)
    return _call(
        body, name="mlp_fwd_bwd", grid=(n_steps,),
        in_specs=[_rows(tm, d), _rows(tm, d), _const((1, d)), _const((1, d)), ANY, ANY],
        out_specs=[_rows(tm, d), _rows(tm, f), _rows(tm, d), _rows(tm, d), _rows(tm, f),
                   _const((1, d)), _const((1, d)), _const((1, d))],
        out_shape=[jax.ShapeDtypeStruct((t, d), F32), jax.ShapeDtypeStruct((t, f), BF16),
                   jax.ShapeDtypeStruct((t, d), BF16), jax.ShapeDtypeStruct((t, d), BF16),
                   jax.ShapeDtypeStruct((t, f), BF16), row, row, row],
        scratch_shapes=[pltpu.VMEM((d, f), BF16), pltpu.VMEM((f, d), BF16),
                        pltpu.VMEM((tm, f), F32), pltpu.VMEM((tm, fq), F32), pltpu.VMEM((tm, d), F32),
                        pltpu.VMEM((SUBLANES, d), F32), pltpu.VMEM((SUBLANES, d), F32),
                        pltpu.VMEM((SUBLANES, d), F32), pltpu.SemaphoreType.DMA((nb + 1,))],
        args=(x1, target, g2pre, g2post, w1_g, w2), ex=ex)


def _mixer_bwd_call(dx1, mix, sa, sb, ya, yb, bg, va, cb, g1post, ln_g, ln_b, w_a, w_b, w_o, tm, ex=None):
    t, d = dx1.shape
    n_steps = t // tm

    def body(dx1_ref, mix_ref, sa_ref, sb_ref, ya_ref, yb_ref, bg_ref, va_ref, cb_ref, g_ref, lng, lnb,
             wa_hbm, wb_hbm, wo_hbm,
             dmix_ref, dya_ref, dyb_ref, dva_ref, dcb_ref, dbg_ref, dza_ref, dzb_ref,
             dg_ref, dlng_ref, dlnb_ref, dba_ref, dbb_ref,
             wa_v, wb_v, wo_v, s0, s1, acc_g, acc_lng, acc_lnb, acc_ba, acc_bb, sem):
        _load_once(wa_hbm, wa_v, sem.at[0])
        _load_once(wb_hbm, wb_v, sem.at[1])
        _load_once(wo_hbm, wo_v, sem.at[2])
        accs = (acc_g, acc_lng, acc_lnb, acc_ba, acc_bb)

        @pl.when(pl.program_id(0) == 0)
        def _():
            for acc in accs:
                acc[...] = jnp.zeros_like(acc)

        def dnorm(rows):
            mix = mix_ref[rows, :]
            dxv = dx1_ref[rows, :]
            r = lax.rsqrt(_mean_lanes(mix * mix) + RMS_EPS)
            gd = dxv * g_ref[...]
            dmix_ref[rows, :] = (r * gd - mix * (r * r * r * _mean_lanes(gd * mix))).astype(BF16)
            acc_g[...] += _fold8(dxv * mix * r)
        _for_chunks(tm, ROW_CHUNK, dnorm)

        s0[...] = lax.dot_general(dmix_ref[...], wo_v[...], NT_DIMS, preferred_element_type=F32)

        def dmerge(rows):
            dm = s0[rows, :]
            sav, sbv = sa_ref[rows, :].astype(F32), sb_ref[rows, :].astype(F32)
            dya_ref[rows, :] = (dm * sav).astype(BF16)
            dyb_ref[rows, :] = (dm * sbv).astype(BF16)
            dza_ref[rows, :] = (dm * ya_ref[rows, :].astype(F32) * sav * (1.0 - sav)).astype(BF16)
            dzb_ref[rows, :] = (dm * yb_ref[rows, :].astype(F32) * sbv * (1.0 - sbv)).astype(BF16)
        _for_chunks(tm, ROW_CHUNK, dmerge)

        s0[...] = lax.dot_general(dya_ref[...], wa_v[...], NT_DIMS, preferred_element_type=F32)
        s1[...] = lax.dot_general(dyb_ref[...], wb_v[...], NT_DIMS, preferred_element_type=F32)

        def dbranches(rows):
            dpa = s0[rows, :]
            dbg_ref[rows, :] = (dpa * va_ref[rows, :].astype(F32)).astype(BF16)
            dva = dpa * bg_ref[rows, :].astype(F32)
            dva_ref[rows, :] = dva
            acc_ba[...] += _fold8(dva)
            cbv = cb_ref[rows, :]
            mu = _mean_lanes(cbv)
            cen = cbv - mu
            rstd = lax.rsqrt(_mean_lanes(cen * cen) + LN_EPS)
            xhat = cen * rstd
            ln = xhat * lng[...] + lnb[...]
            sig = jax.nn.sigmoid(ln)
            dln = s1[rows, :] * (sig * (1.0 + ln * (1.0 - sig)))
            acc_lng[...] += _fold8(dln * xhat)
            acc_lnb[...] += _fold8(dln)
            dxh = dln * lng[...]
            dcb = rstd * (dxh - _mean_lanes(dxh) - xhat * _mean_lanes(dxh * xhat))
            dcb_ref[rows, :] = dcb
            acc_bb[...] += _fold8(dcb)
        _for_chunks(tm, ROW_CHUNK, dbranches)

        _write_row_sums(acc_g, dg_ref, n_steps)
        _write_row_sums(acc_lng, dlng_ref, n_steps)
        _write_row_sums(acc_lnb, dlnb_ref, n_steps)
        _write_row_sums(acc_ba, dba_ref, n_steps)
        _write_row_sums(acc_bb, dbb_ref, n_steps)

    bf = jax.ShapeDtypeStruct((t, d), BF16)
    f32 = jax.ShapeDtypeStruct((t, d), F32)
    row = jax.ShapeDtypeStruct((1, d), F32)
    return _call(
        body, name="mixer_bwd", grid=(n_steps,),
        in_specs=[_rows(tm, d)] * 9 + [_const((1, d))] * 3 + [ANY, ANY, ANY],
        out_specs=[_rows(tm, d)] * 8 + [_const((1, d))] * 5,
        out_shape=[bf, bf, bf, f32, f32, bf, bf, bf, row, row, row, row, row],
        scratch_shapes=[pltpu.VMEM((d, d), BF16), pltpu.VMEM((d, d), BF16), pltpu.VMEM((d, d), BF16),
                        pltpu.VMEM((tm, d), F32), pltpu.VMEM((tm, d), F32)]
        + [pltpu.VMEM((SUBLANES, d), F32)] * 5 + [pltpu.SemaphoreType.DMA((3,))],
        args=(dx1, mix, sa, sb, ya, yb, bg, va, cb, g1post, ln_g, ln_b, w_a, w_b, w_o), ex=ex)


def _conv_bwd_call(dva, dcb, ua, ub, cg, ha, a, sg, dbg, dza, dzb, conv_a_w, conv_b_w, tm, ex=None):
    t, d = dva.shape
    n_steps = t // tm
    ka, kb = conv_a_w.shape[0], conv_b_w.shape[0]
    off_a = [HALO_A + (ka - 1) // 2 - k for k in range(ka)]
    off_b = [HALO_B + (kb - 1) // 2 - k for k in range(kb)]
    plan_a, plan_b = _shift_plan(off_a), _shift_plan(off_b)

    def body(dvap, dvac, dvan, dcbp, dcbc, dcbn, ua_ref, ub_ref,
             cg_ref, ha_ref, a_ref, sg_ref, dbg_ref, dza_ref, dzb_ref, wa_c, wb_c,
             dproj_ref, dwa_ref, dwb_ref, dbin_ref,
             e_dva, e_dcb, sh_a, sh_b, acc_wa, acc_wb, acc_bin):
        i = pl.program_id(0)

        @pl.when(i == 0)
        def _():
            acc_wa[...] = jnp.zeros_like(acc_wa)
            acc_wb[...] = jnp.zeros_like(acc_wb)
            acc_bin[...] = jnp.zeros_like(acc_bin)

        _fill_ext(e_dva, dvap, dvac, dvan, HALO_A, tm, i, n_steps)
        _fill_ext(e_dcb, dcbp, dcbc, dcbn, HALO_B, tm, i, n_steps)
        _fill_shifted(e_dva, sh_a, plan_a)
        _fill_shifted(e_dcb, sh_b, plan_b)

        def put(col, rows, val_f32):
            dproj_ref[rows, pl.ds(col * d, d)] = val_f32.astype(BF16)
            acc_bin[:, pl.ds(col * d, d)] += _fold8(val_f32)

        for r0 in range(0, tm, CONV_ROWS):
            rows = pl.ds(r0, CONV_ROWS)
            ua_c, ub_c = ua_ref[rows, :], ub_ref[rows, :]
            dua = jnp.zeros((CONV_ROWS, d), F32)
            for k in range(ka):
                xk = _window(e_dva, sh_a, plan_a, off_a[k], r0)
                dua = dua + wa_c[k:k + 1, :] * xk
                acc_wa[pl.ds(k * SUBLANES, SUBLANES), :] += _fold8(ua_c * xk)
            dub = jnp.zeros((CONV_ROWS, d), F32)
            for k in range(kb):
                xk = _window(e_dcb, sh_b, plan_b, off_b[k], r0)
                dub = dub + wb_c[k:k + 1, :] * xk
                acc_wb[pl.ds(k * SUBLANES, SUBLANES), :] += _fold8(ub_c * xk)
            cgv, hav = cg_ref[rows, :].astype(F32), ha_ref[rows, :].astype(F32)
            av, sgv = a_ref[rows, :].astype(F32), sg_ref[rows, :].astype(F32)
            put(0, rows, dbg_ref[rows, :].astype(F32))
            put(1, rows, dua * hav)
            put(2, rows, dua * cgv)
            put(3, rows, dub * sgv)
            put(4, rows, dub * av * sgv * (1.0 - sgv))
            put(5, rows, dza_ref[rows, :].astype(F32))
            put(6, rows, dzb_ref[rows, :].astype(F32))

        @pl.when(i == n_steps - 1)
        def _():
            for k in range(ka):
                dwa_ref[k:k + 1, :] = jnp.sum(acc_wa[pl.ds(k * SUBLANES, SUBLANES), :], axis=0, keepdims=True)
            for k in range(kb):
                dwb_ref[k:k + 1, :] = jnp.sum(acc_wb[pl.ds(k * SUBLANES, SUBLANES), :], axis=0, keepdims=True)
            dbin_ref[...] = jnp.sum(acc_bin[...], axis=0, keepdims=True)

    halo_a = [_halo_prev(tm, HALO_A, d), _rows(tm, d), _halo_next(tm, HALO_A, d, t)]
    halo_b = [_halo_prev(tm, HALO_B, d), _rows(tm, d), _halo_next(tm, HALO_B, d, t)]
    return _call(
        body, name="conv_bwd", grid=(n_steps,),
        in_specs=halo_a + halo_b + [_rows(tm, d)] * 9 + [_const((ka, d)), _const((kb, d))],
        out_specs=[_rows(tm, 7 * d), _const((ka, d)), _const((kb, d)), _const((1, 7 * d))],
        out_shape=[jax.ShapeDtypeStruct((t, 7 * d), BF16), jax.ShapeDtypeStruct((ka, d), F32),
                   jax.ShapeDtypeStruct((kb, d), F32), jax.ShapeDtypeStruct((1, 7 * d), F32)],
        scratch_shapes=[pltpu.VMEM((tm + 2 * HALO_A, d), F32), pltpu.VMEM((tm + 2 * HALO_B, d), F32),
                        pltpu.VMEM((len(plan_a), _shifted_rows(tm, off_a), d), F32),
                        pltpu.VMEM((len(plan_b), _shifted_rows(tm, off_b), d), F32),
                        pltpu.VMEM((ka * SUBLANES, d), F32), pltpu.VMEM((kb * SUBLANES, d), F32),
                        pltpu.VMEM((SUBLANES, 7 * d), F32)],
        args=(dva, dva, dva, dcb, dcb, dcb, ua, ub, cg, ha, a, sg, dbg, dza, dzb,
              conv_a_w, conv_b_w), ex=ex)


def _dx_call(dproj, x, dx1, g1pre, w_in_g, tm, ex=None):
    t, d = x.shape
    nb, _, n4 = w_in_g.shape
    ni = nb * n4
    n_steps = t // tm

    def body(dp_ref, x_ref, dx1_ref, g_ref, w_hbm, dx_ref, dg_ref, w_v, dh_s, acc_g, sem):
        _load_blocks_once(w_hbm, w_v, sem)

        @pl.when(pl.program_id(0) == 0)
        def _():
            acc_g[...] = jnp.zeros_like(acc_g)

        dh_s[...] = lax.dot_general(dp_ref[...], w_v[...], NT_DIMS, preferred_element_type=F32)

        def dnorm(rows):
            dh = dh_s[rows, :]
            xv = x_ref[rows, :]
            r = lax.rsqrt(_mean_lanes(xv * xv) + RMS_EPS)
            gd = dh * g_ref[...]
            dx_ref[rows, :] = dx1_ref[rows, :] + r * gd - xv * (r * r * r * _mean_lanes(gd * xv))
            acc_g[...] += _fold8(dh * xv * r)
        _for_chunks(tm, ROW_CHUNK, dnorm)
        _write_row_sums(acc_g, dg_ref, n_steps)

    return _call(
        body, name="dx_bwd", grid=(n_steps,),
        in_specs=[_rows(tm, ni), _rows(tm, d), _rows(tm, d), _const((1, d)), ANY],
        out_specs=[_rows(tm, d), _const((1, d))],
        out_shape=[jax.ShapeDtypeStruct((t, d), F32), jax.ShapeDtypeStruct((1, d), F32)],
        scratch_shapes=[pltpu.VMEM((d, ni), BF16), pltpu.VMEM((tm, d), F32),
                        pltpu.VMEM((SUBLANES, d), F32), pltpu.SemaphoreType.DMA((nb,))],
        args=(dproj, x, dx1, g1pre, w_in_g), ex=ex)


def _tn_matmul(a, g, nblk, a_cols, g_cols, a_blocked, g_blocked, tt, name, ex=None):
    t = a.shape[0]

    def body(a_ref, g_ref, o_ref):
        @pl.when(pl.program_id(1) == 0)
        def _():
            o_ref[...] = jnp.zeros_like(o_ref)
        o_ref[0] += lax.dot_general(a_ref[...], g_ref[...], TN_DIMS, preferred_element_type=F32)

    (out,), xouts = _call(
        body, name=name, grid=(nblk, t // tt),
        in_specs=[pl.BlockSpec((tt, a_cols), (lambda b, s: (s, b)) if a_blocked else (lambda b, s: (s, 0))),
                  pl.BlockSpec((tt, g_cols), (lambda b, s: (s, b)) if g_blocked else (lambda b, s: (s, 0)))],
        out_specs=[pl.BlockSpec((1, a_cols, g_cols), lambda b, s: (b, 0, 0))],
        out_shape=[jax.ShapeDtypeStruct((nblk, a_cols, g_cols), F32)],
        scratch_shapes=[], args=(a, g), ex=ex)
    return out, xouts


def _pair_sum_call(g_full, from_sibling, core, name):
    nblk, r, c = g_full.shape
    hr = r // 2
    tr = min(hr, 256)
    n = hr // tr

    def body(core_ref, g_ref, p_ref, o_ref):
        o_ref[...] = (g_ref[...] + p_ref[...]).astype(BF16)

    return pl.pallas_call(
        body, name=name,
        grid_spec=pltpu.PrefetchScalarGridSpec(
            num_scalar_prefetch=1, grid=(nblk, n),
            in_specs=[pl.BlockSpec((1, tr, c), lambda j, i, cr: (j, cr[0] * n + i, 0)),
                      pl.BlockSpec((1, tr, c), lambda j, i, cr: (j, i, 0))],
            out_specs=pl.BlockSpec((1, tr, c), lambda j, i, cr: (j, i, 0))),
        out_shape=jax.ShapeDtypeStruct((nblk, hr, c), BF16),
        compiler_params=_params("parallel", "parallel"))(core, g_full, from_sibling)


def _chip_sum_call(pair, received, chip_core, name):
    _, hr, c = pair.shape
    tr = min(hr, 256)
    n = hr // tr

    def body(cc_ref, own_ref, r_ref, o_ref):
        o_ref[...] = ((own_ref[0].astype(F32) + r_ref[0].astype(F32)) + r_ref[1].astype(F32)) + r_ref[2].astype(F32)

    return pl.pallas_call(
        body, name=name,
        grid_spec=pltpu.PrefetchScalarGridSpec(
            num_scalar_prefetch=1, grid=(n,),
            in_specs=[pl.BlockSpec((1, tr, c), lambda i, cc: (cc[0], i, 0)),
                      pl.BlockSpec((N_CHIPS - 1, tr, c), lambda i, cc: (0, i, 0))],
            out_specs=pl.BlockSpec((tr, c), lambda i, cc: (cc[1] * n + i, 0))),
        out_shape=jax.ShapeDtypeStruct((2 * hr, c), F32),
        compiler_params=_params("parallel"))(chip_core, pair, received)


def _adamw(w, g, m, v):
    m = ADAM_B1 * m + (1.0 - ADAM_B1) * g
    v = ADAM_B2 * v + (1.0 - ADAM_B2) * (g * g)
    m_hat = m / (1.0 - ADAM_B1 ** ADAM_STEP)
    v_hat = v / (1.0 - ADAM_B2 ** ADAM_STEP)
    delta = -ADAM_LR * (m_hat / (jnp.sqrt(v_hat) + ADAM_EPS) + ADAM_WD * w)
    return delta, m, v


def _adam_call(w, g, m, v, name):
    r, c = w.shape
    tr = min(r, 256)

    def body(w_ref, g_ref, m_ref, v_ref, d_ref, mo_ref, vo_ref):
        d_ref[...], mo_ref[...], vo_ref[...] = _adamw(w_ref[...], g_ref[...], m_ref[...], v_ref[...])

    shape = jax.ShapeDtypeStruct((r, c), F32)
    return pl.pallas_call(
        body, name=name, grid=(r // tr,), in_specs=[_rows(tr, c)] * 4, out_specs=[_rows(tr, c)] * 3,
        out_shape=[shape] * 3, compiler_params=_params("parallel"))(w, g, m, v)


def _place():
    return lax.axis_index("x"), lax.axis_index("y"), lax.axis_index("c")


def _other_chips(x, y):
    rel = [(x, 1 - y), (1 - x, y), (1 - x, 1 - y)]
    return [(px, py, 2 * px + py) for px, py in rel]


def _gather_weights_call(shards):
    n = len(shards)

    def body(*refs):
        ins, outs = refs[:n], refs[n:2 * n]
        ici_send, ici_recv, fwd_send, fwd_recv, own_sem = refs[2 * n:]
        x, y, c = _place()
        me = 2 * x + y
        chips = _other_chips(x, y)
        started = []
        own = [pltpu.make_async_copy(ins[a], outs[a].at[me], own_sem.at[a]) for a in range(n)]
        for cp in own:
            cp.start()

        def half(ref, chip, core):
            hr = ref.shape[1] // 2
            return ref.at[chip, pl.ds(core * hr, hr), :]

        for a in range(n):
            hr = ins[a].shape[0] // 2
            for j, (px, py, _) in enumerate(chips):
                cp = pltpu.make_async_remote_copy(
                    src_ref=ins[a].at[pl.ds(c * hr, hr), :], dst_ref=half(outs[a], me, c),
                    send_sem=ici_send.at[a, j], recv_sem=ici_recv.at[a, j],
                    device_id=(px, py, c), device_id_type=MESH)
                cp.start()
                started.append(cp)
        for a in range(n):
            for j, (px, py, pk) in enumerate(chips):
                landed = half(outs[a], pk, c)
                pltpu.make_async_remote_copy(
                    src_ref=landed, dst_ref=landed, send_sem=ici_send.at[a, j], recv_sem=ici_recv.at[a, j],
                    device_id=(px, py, c), device_id_type=MESH).wait_recv()
                cp = pltpu.make_async_remote_copy(
                    src_ref=landed, dst_ref=landed, send_sem=fwd_send.at[a, j], recv_sem=fwd_recv.at[a, j],
                    device_id=(x, y, 1 - c), device_id_type=MESH)
                cp.start()
                started.append(cp)
        for a in range(n):
            for j, (px, py, pk) in enumerate(chips):
                passed = half(outs[a], pk, 1 - c)
                pltpu.make_async_remote_copy(
                    src_ref=passed, dst_ref=passed, send_sem=fwd_send.at[a, j], recv_sem=fwd_recv.at[a, j],
                    device_id=(x, y, 1 - c), device_id_type=MESH).wait_recv()
        for cp in started:
            cp.wait_send()
        for cp in own:
            cp.wait()

    return pl.pallas_call(
        body, name="gather_weights", in_specs=[ANY] * n, out_specs=[ANY] * n,
        out_shape=[jax.ShapeDtypeStruct((N_CHIPS,) + s.shape, s.dtype) for s in shards],
        scratch_shapes=[pltpu.SemaphoreType.DMA((n, N_CHIPS - 1)), pltpu.SemaphoreType.DMA((n, N_CHIPS - 1)),
                        pltpu.SemaphoreType.DMA((n, N_CHIPS - 1)), pltpu.SemaphoreType.DMA((n, N_CHIPS - 1)),
                        pltpu.SemaphoreType.DMA((n,))],
        compiler_params=pltpu.CompilerParams(has_side_effects=True))(*shards)


def _gather_conv_weights(conv_a_w, conv_b_w, d):
    ka, dq = conv_a_w.shape
    kb = conv_b_w.shape[0]
    ra = -(-ka // SUBLANES) * SUBLANES
    rb = -(-kb // SUBLANES) * SUBLANES
    a_pad = jnp.pad(conv_a_w, ((0, ra - ka), (0, 0)))
    b_pad = jnp.pad(conv_b_w, ((0, rb - kb), (0, 0)))

    def body(a_ref, b_ref, oa_ref, ob_ref, pack, slots, send_sem, recv_sem):
        x, y, c = _place()
        me = 2 * x + y
        chips = _other_chips(x, y)
        pack[pl.ds(0, ra), :] = a_ref[...]
        pack[pl.ds(ra, rb), :] = b_ref[...]
        copies = []
        for j, (px, py, _) in enumerate(chips):
            cp = pltpu.make_async_remote_copy(
                src_ref=pack, dst_ref=slots.at[me], send_sem=send_sem.at[j], recv_sem=recv_sem.at[j],
                device_id=(px, py, c), device_id_type=MESH)
            cp.start()
            copies.append(cp)
        for j, (px, py, pk) in enumerate(chips):
            pltpu.make_async_remote_copy(
                src_ref=pack, dst_ref=slots.at[pk], send_sem=send_sem.at[j], recv_sem=recv_sem.at[j],
                device_id=(px, py, c), device_id_type=MESH).wait_recv()
        for cp in copies:
            cp.wait_send()
        slots[me] = pack[...]
        for k in range(N_CHIPS):
            oa_ref[:, pl.ds(k * dq, dq)] = slots[k, pl.ds(0, ra), :]
            ob_ref[:, pl.ds(k * dq, dq)] = slots[k, pl.ds(ra, rb), :]

    oa, ob = pl.pallas_call(
        body, name="gather_conv_weights", in_specs=[VMEM_FULL] * 2, out_specs=[VMEM_FULL] * 2,
        out_shape=[jax.ShapeDtypeStruct((ra, d), F32), jax.ShapeDtypeStruct((rb, d), F32)],
        scratch_shapes=[pltpu.VMEM((ra + rb, dq), F32), pltpu.VMEM((N_CHIPS, ra + rb, dq), F32),
                        pltpu.SemaphoreType.DMA((N_CHIPS - 1,)), pltpu.SemaphoreType.DMA((N_CHIPS - 1,))],
        compiler_params=pltpu.CompilerParams(has_side_effects=True))(a_pad, b_pad)
    return oa[:ka], ob[:kb]


def _sibling_halves_call(grads):
    n = len(grads)

    def body(*refs):
        ins, outs = refs[:n], refs[n:2 * n]
        send_sem, recv_sem = refs[2 * n:]
        x, y, c = _place()
        copies = []
        for a in range(n):
            hr = ins[a].shape[1] // 2
            cp = pltpu.make_async_remote_copy(
                src_ref=ins[a].at[:, pl.ds((1 - c) * hr, hr), :], dst_ref=outs[a],
                send_sem=send_sem.at[a], recv_sem=recv_sem.at[a],
                device_id=(x, y, 1 - c), device_id_type=MESH)
            cp.start()
            copies.append(cp)
        for cp in copies:
            cp.wait()

    return pl.pallas_call(
        body, name="grads_to_sibling", in_specs=[ANY] * n, out_specs=[ANY] * n,
        out_shape=[jax.ShapeDtypeStruct((g.shape[0], g.shape[1] // 2, g.shape[2]), g.dtype) for g in grads],
        scratch_shapes=[pltpu.SemaphoreType.DMA((n,)), pltpu.SemaphoreType.DMA((n,))],
        compiler_params=pltpu.CompilerParams(has_side_effects=True))(*grads)


def _scatter_to_owner_call(pairs):
    n = len(pairs)

    def body(*refs):
        ins, outs = refs[:n], refs[n:2 * n]
        send_sem, recv_sem = refs[2 * n:]
        x, y, c = _place()
        chips = _other_chips(x, y)
        copies = []
        for a in range(n):
            for j, (px, py, pk) in enumerate(chips):
                cp = pltpu.make_async_remote_copy(
                    src_ref=ins[a].at[pk], dst_ref=outs[a].at[j],
                    send_sem=send_sem.at[a, j], recv_sem=recv_sem.at[a, j],
                    device_id=(px, py, c), device_id_type=MESH)
                cp.start()
                copies.append(cp)
        for cp in copies:
            cp.wait()

    return pl.pallas_call(
        body, name="grads_to_owner", in_specs=[ANY] * n, out_specs=[ANY] * n,
        out_shape=[jax.ShapeDtypeStruct((N_CHIPS - 1,) + p.shape[1:], p.dtype) for p in pairs],
        scratch_shapes=[pltpu.SemaphoreType.DMA((n, N_CHIPS - 1)), pltpu.SemaphoreType.DMA((n, N_CHIPS - 1))],
        compiler_params=pltpu.CompilerParams(has_side_effects=True))(*pairs)


def _share_halves_call(halves):
    n = len(halves)

    def body(*refs):
        ins, outs = refs[:n], refs[n:2 * n]
        send_sem, recv_sem, own_sem = refs[2 * n:]
        x, y, c = _place()
        copies, own = [], []
        for a in range(n):
            hr = ins[a].shape[0]
            mine = outs[a].at[pl.ds(c * hr, hr), :]
            cp = pltpu.make_async_copy(ins[a], mine, own_sem.at[a])
            cp.start()
            own.append(cp)
            cp = pltpu.make_async_remote_copy(
                src_ref=ins[a], dst_ref=mine, send_sem=send_sem.at[a], recv_sem=recv_sem.at[a],
                device_id=(x, y, 1 - c), device_id_type=MESH)
            cp.start()
            copies.append(cp)
        for a in range(n):
            hr = ins[a].shape[0]
            theirs = outs[a].at[pl.ds((1 - c) * hr, hr), :]
            pltpu.make_async_remote_copy(
                src_ref=ins[a], dst_ref=theirs, send_sem=send_sem.at[a], recv_sem=recv_sem.at[a],
                device_id=(x, y, 1 - c), device_id_type=MESH).wait_recv()
        for cp in copies:
            cp.wait_send()
        for cp in own:
            cp.wait()

    return pl.pallas_call(
        body, name="halves_to_sibling", in_specs=[ANY] * n, out_specs=[ANY] * n,
        out_shape=[jax.ShapeDtypeStruct((2 * h.shape[0], h.shape[1]), h.dtype) for h in halves],
        scratch_shapes=[pltpu.SemaphoreType.DMA((n,)), pltpu.SemaphoreType.DMA((n,)),
                        pltpu.SemaphoreType.DMA((n,))],
        compiler_params=pltpu.CompilerParams(has_side_effects=True))(*halves)


def _small_step_call(partials, loss_rows, weights, m_s, v_s, sharded, d):
    n = len(partials)
    row_counts = [p.shape[0] for p in partials]
    starts = [sum(row_counts[:i]) for i in range(n)]
    loss_row = sum(row_counts)
    pack_rows = -(-(loss_row + 1) // SUBLANES) * SUBLANES
    dq = d // N_CHIPS

    def body(*refs):
        p_refs = refs[:n]
        loss_in = refs[n]
        w_refs = refs[n + 1:2 * n + 1]
        m_refs = refs[2 * n + 1:3 * n + 1]
        v_refs = refs[3 * n + 1:4 * n + 1]
        o = 4 * n + 1
        g_out = refs[o:o + n]
        d_out = refs[o + n:o + 2 * n]
        m_out = refs[o + 2 * n:o + 3 * n]
        v_out = refs[o + 3 * n:o + 4 * n]
        loss_out = refs[o + 4 * n]
        pack, slots, send_sem, recv_sem = refs[o + 4 * n + 1:]
        x, y, c = _place()
        me = 4 * x + 2 * y + c

        pack[...] = jnp.zeros_like(pack)
        for i in range(n):
            pack[pl.ds(starts[i], row_counts[i]), :] = p_refs[i][...]
        pack[pl.ds(loss_row, 1), :] = loss_in[...]

        copies = []
        for rel in range(1, N_DEV):
            peer = (x ^ (rel >> 2), y ^ ((rel >> 1) & 1), c ^ (rel & 1))
            cp = pltpu.make_async_remote_copy(
                src_ref=pack, dst_ref=slots.at[me], send_sem=send_sem.at[rel - 1], recv_sem=recv_sem.at[rel - 1],
                device_id=peer, device_id_type=MESH)
            cp.start()
            copies.append(cp)
        for rel in range(1, N_DEV):
            peer_idx = me ^ rel
            pltpu.make_async_remote_copy(
                src_ref=pack, dst_ref=slots.at[peer_idx], send_sem=send_sem.at[rel - 1],
                recv_sem=recv_sem.at[rel - 1], device_id=(x, y, c), device_id_type=MESH).wait_recv()
        for cp in copies:
            cp.wait_send()

        slots[me] = pack[...]
        total = slots[0]
        for dev in range(1, N_DEV):
            total = total + slots[dev]
        pack[...] = total

        loss_out[...] = jnp.broadcast_to(
            (0.5 / d) * jnp.sum(pack[pl.ds(loss_row, 1), :], axis=-1, keepdims=True), loss_out.shape)
        chip = 2 * x + y
        for i in range(n):
            rows = pl.ds(starts[i], row_counts[i])
            if sharded[i]:
                for k in range(N_CHIPS):
                    @pl.when(chip == k)
                    def _():
                        g_out[i][...] = pack[rows, pl.ds(k * dq, dq)]
            else:
                g_out[i][...] = pack[rows, :]
            d_out[i][...], m_out[i][...], v_out[i][...] = _adamw(
                w_refs[i][...], g_out[i][...], m_refs[i][...], v_refs[i][...])

    w_shapes = [jax.ShapeDtypeStruct(w.shape, F32) for w in weights]
    n_in = 4 * n + 1
    return pl.pallas_call(
        body, name="small_grads_allreduce_adamw",
        in_specs=[VMEM_FULL] * n_in, out_specs=[VMEM_FULL] * (4 * n + 1),
        out_shape=w_shapes * 4 + [jax.ShapeDtypeStruct((SUBLANES, 128), F32)],
        scratch_shapes=[pltpu.VMEM((pack_rows, d), F32), pltpu.VMEM((N_DEV, pack_rows, d), F32),
                        pltpu.SemaphoreType.DMA((N_DEV - 1,)), pltpu.SemaphoreType.DMA((N_DEV - 1,))],
        compiler_params=pltpu.CompilerParams(has_side_effects=True, vmem_limit_bytes=VMEM_LIMIT))(
            *partials, loss_rows, *weights, *m_s, *v_s)


def _tile(t, want):
    return min(t, want)


def kernel(x, norm1_pre_g, w_in, b_in, conv_a_w, conv_a_b, w_a_out, conv_b_w, conv_b_b, ln_b_g, ln_b_b, w_b_out, w_o, norm1_post_g, norm2_pre_g, w_mlp_in, w_mlp_out, norm2_post_g, loss_target, m_norm1_pre_g, m_w_in, m_b_in, m_conv_a_w, m_conv_a_b, m_w_a_out, m_conv_b_w, m_conv_b_b, m_ln_b_g, m_ln_b_b, m_w_b_out, m_w_o, m_norm1_post_g, m_norm2_pre_g, m_w_mlp_in, m_w_mlp_out, m_norm2_post_g, v_norm1_pre_g, v_w_in, v_b_in, v_conv_a_w, v_conv_a_b, v_w_a_out, v_conv_b_w, v_conv_b_b, v_ln_b_g, v_ln_b_b, v_w_b_out, v_w_o, v_norm1_post_g, v_norm2_pre_g, v_w_mlp_in, v_w_mlp_out, v_norm2_post_g):
    _, t, d = x.shape
    xt = x.reshape(t, d)
    tgt = loss_target.reshape(t, d)
    row = lambda vec: vec.reshape(1, -1)
    cx, cy, cc = _place()
    core = cc.astype(jnp.int32).reshape(1)
    chip = (2 * cx + cy).astype(jnp.int32).reshape(1)

    big = dict(w_in=w_in, w_a_out=w_a_out, w_b_out=w_b_out, w_o=w_o, w_mlp_in=w_mlp_in, w_mlp_out=w_mlp_out)
    names = list(big)
    chip_core = jnp.concatenate([chip, core])
    slot = {k: _cast_to_slot(big[k], chip, "cast_" + k) for k in names}
    mixer_w, mlp_w = ["w_a_out", "w_b_out", "w_o"], ["w_mlp_in", "w_mlp_out"]
    rows_of = lambda buf: buf.reshape(-1, buf.shape[-1])

    def pair_sums(keys, full, from_sibling):
        return [_pair_sum_call(g, p, core, "pair_sum_" + k) for k, g, p in zip(keys, full, from_sibling)]

    def chip_sums(keys, pairs, received):
        return [_chip_sum_call(p, r, chip_core, "chip_sum_" + k) for k, p, r in zip(keys, pairs, received)]

    (w_in_g,) = _exchange_call("gather_w_in", [_ex_gather_ici([slot["w_in"]]), _ex_gather_forward([slot["w_in"]])])

    g1pre, g1post, g2pre, g2post = row(norm1_pre_g), row(norm1_post_g), row(norm2_pre_g), row(norm2_post_g)
    lng, lnb, ba, bb = row(ln_b_g), row(ln_b_b), row(conv_a_b), row(conv_b_b)
    conv_a_full, conv_b_full = _gather_conv_weights(conv_a_w, conv_b_w, d)

    (h, ua, ub, bg, cg, ha, a, sg, sa, sb), landed = _proj_call(
        xt, g1pre, w_in_g, row(b_in), _tile(t, 512), ex=_ex_gather_ici([slot[k] for k in mixer_w + mlp_w]))
    w_a_g, w_b_g, w_o_g = _exchange_call("forward_mixer_weights", [_ex_gather_forward(landed[:3])])
    w_a_full, w_b_full, w_o_full = rows_of(w_a_g), rows_of(w_b_g), rows_of(w_o_g)
    (x1, va, pa, cb, sbo, ya, yb, mg, mix), (w1_g, w2_g) = _mixer_fwd_call(
        ua, ub, bg, sa, sb, xt, conv_a_full, ba, conv_b_full, bb, lng, lnb,
        w_a_full, w_b_full, w_o_full, g1post, _tile(t, 256), ex=_ex_gather_forward(landed[3:]))
    (dx1, f, df2, h2, df1, dg2post, dg2pre, loss_rows), _ = _mlp_call(
        x1, tgt, g2pre, g2post, w1_g, rows_of(w2_g), _tile(t, 256))

    tt = _tile(t, 1024)
    n4, fq, dq = w_in.shape[1], w_mlp_in.shape[1], d // N_CHIPS
    g_mlp = [_tn_matmul(h2, df1, N_CHIPS, d, fq, False, True, tt, "dw_mlp_in")[0],
             _tn_matmul(f, df2, N_CHIPS, fq, d, True, False, tt, "dw_mlp_out")[0]]
    (dmix, dya, dyb, dva, dcb, dbg, dza, dzb, dg1post, dlng, dlnb, dba, dbb), sib_mlp = _mixer_bwd_call(
        dx1, mix, sa, sb, ya, yb, bg, va, cb, g1post, lng, lnb, w_a_full, w_b_full, w_o_full, _tile(t, 256),
        ex=_ex_sibling_halves(g_mlp))
    p_mlp = pair_sums(mlp_w, g_mlp, sib_mlp)
    g_mix = [_tn_matmul(pa, dya, 1, d, d, False, False, tt, "dw_a_out")[0].reshape(N_CHIPS, dq, d),
             _tn_matmul(sbo, dyb, 1, d, d, False, False, tt, "dw_b_out")[0].reshape(N_CHIPS, dq, d),
             _tn_matmul(mg, dmix, 1, d, d, False, False, tt, "dw_o")[0].reshape(N_CHIPS, dq, d)]
    ex_a, ex_b = _ex_scatter_to_owner(p_mlp), _ex_sibling_halves(g_mix)
    (dproj, dwa_conv, dwb_conv, dbin), xo = _conv_bwd_call(
        dva, dcb, ua, ub, cg, ha, a, sg, dbg, dza, dzb, conv_a_full, conv_b_full, _tile(t, 256),
        ex=_merge(ex_a, ex_b))
    recv_mlp, sib_mix = _split(xo, ex_a, ex_b)
    r_mlp = chip_sums(mlp_w, p_mlp, recv_mlp)
    p_mix = pair_sums(mixer_w, g_mix, sib_mix)
    ex_a, ex_b = _ex_share_halves(r_mlp), _ex_scatter_to_owner(p_mix)
    g_in, xo = _tn_matmul(h, dproj, N_CHIPS, d, n4, False, True, tt, "dw_in", ex=_merge(ex_a, ex_b))
    red_mlp, recv_mix = _split(xo, ex_a, ex_b)
    r_mix = chip_sums(mixer_w, p_mix, recv_mix)
    ex_a, ex_b = _ex_sibling_halves([g_in]), _ex_share_halves(r_mix)
    (grad_x, dg1pre), xo = _dx_call(dproj, xt, dx1, g1pre, w_in_g, _tile(t, 512), ex=_merge(ex_a, ex_b))
    sib_in, red_mix = _split(xo, ex_a, ex_b)
    p_in = pair_sums(["w_in"], [g_in], sib_in)
    recv_in = _exchange_call("w_in_grad_to_owner", [_ex_scatter_to_owner(p_in)])
    r_in = chip_sums(["w_in"], p_in, recv_in)
    red_in = _exchange_call("w_in_grad_to_sibling", [_ex_share_halves(r_in)])
    reduced = dict(zip(mlp_w + mixer_w + ["w_in"], red_mlp + red_mix + red_in))

    moments = dict(w_in=(m_w_in, v_w_in), w_a_out=(m_w_a_out, v_w_a_out), w_b_out=(m_w_b_out, v_w_b_out),
                   w_o=(m_w_o, v_w_o), w_mlp_in=(m_w_mlp_in, v_w_mlp_in), w_mlp_out=(m_w_mlp_out, v_w_mlp_out))
    out = {}
    for k in names:
        delta, new_m, new_v = _adam_call(big[k], reduced[k], *moments[k], "adamw_" + k)
        out[k] = (reduced[k], delta, new_m, new_v)

    small = [
        ("conv_b_w", dwb_conv, conv_b_w, m_conv_b_w, v_conv_b_w, True),
        ("conv_b_b", dbb, bb, row(m_conv_b_b), row(v_conv_b_b), False),
        ("b_in", dbin.reshape(7, d), b_in.reshape(7, d), m_b_in.reshape(7, d), v_b_in.reshape(7, d), False),
        ("norm1_pre_g", dg1pre, row(norm1_pre_g), row(m_norm1_pre_g), row(v_norm1_pre_g), False),
        ("conv_a_w", dwa_conv, conv_a_w, m_conv_a_w, v_conv_a_w, True),
        ("conv_a_b", dba, ba, row(m_conv_a_b), row(v_conv_a_b), False),
        ("ln_b_g", dlng, lng, row(m_ln_b_g), row(v_ln_b_g), False),
        ("ln_b_b", dlnb, lnb, row(m_ln_b_b), row(v_ln_b_b), False),
        ("norm1_post_g", dg1post, g1post, row(m_norm1_post_g), row(v_norm1_post_g), False),
        ("norm2_pre_g", dg2pre, g2pre, row(m_norm2_pre_g), row(v_norm2_pre_g), False),
        ("norm2_post_g", dg2post, g2post, row(m_norm2_post_g), row(v_norm2_post_g), False),
    ]
    res = _small_step_call([s[1] for s in small], loss_rows, [s[2] for s in small], [s[3] for s in small],
                           [s[4] for s in small], [s[5] for s in small], d)
    ns = len(small)
    loss = res[4 * ns][0, 0]
    shapes = dict(norm1_pre_g=norm1_pre_g.shape, b_in=b_in.shape, conv_a_w=conv_a_w.shape,
                  conv_a_b=conv_a_b.shape, conv_b_w=conv_b_w.shape, conv_b_b=conv_b_b.shape,
                  ln_b_g=ln_b_g.shape, ln_b_b=ln_b_b.shape, norm1_post_g=norm1_post_g.shape,
                  norm2_pre_g=norm2_pre_g.shape, norm2_post_g=norm2_post_g.shape)
    for i, s in enumerate(small):
        out[s[0]] = tuple(res[q * ns + i].reshape(shapes[s[0]]) for q in range(4))

    order = ["norm1_pre_g", "w_in", "b_in", "conv_a_w", "conv_a_b", "w_a_out", "conv_b_w", "conv_b_b",
             "ln_b_g", "ln_b_b", "w_b_out", "w_o", "norm1_post_g", "norm2_pre_g", "w_mlp_in", "w_mlp_out",
             "norm2_post_g"]
    return (loss, grad_x.reshape(x.shape), *[out[k][0] for k in order], *[out[k][1] for k in order],
            *[out[k][2] for k in order], *[out[k][3] for k in order])
```

```python
import functools

import jax
import jax.numpy as jnp
from jax import lax
from jax.experimental import pallas as pl
from jax.experimental.pallas import tpu as pltpu

RMS_EPS = 1e-6
LN_EPS = 1e-5
ADAM_LR = 0.001
ADAM_B1 = 0.9
ADAM_B2 = 0.999
ADAM_EPS = 1e-08
ADAM_WD = 0.01
ADAM_STEP = 10

F32 = jnp.float32
BF16 = jnp.bfloat16
MESH = pl.DeviceIdType.MESH
ANY = pl.BlockSpec(memory_space=pl.ANY)
VMEM_FULL = pl.BlockSpec(memory_space=pltpu.VMEM)

V7X_VMEM_BYTES = 64 * 1024 * 1024
VMEM_LIMIT = V7X_VMEM_BYTES - 8 * 1024 * 1024
SUBLANES = 8
N_CHIPS = 4
N_DEV = 8
HALO_A = 8
HALO_B = 16
CONV_ROWS = 16
ROW_CHUNK = 32

NT_DIMS = (((1,), (1,)), ((), ()))
TN_DIMS = (((0,), (0,)), ((), ()))


def _params(*sem):
    return pltpu.CompilerParams(dimension_semantics=sem, vmem_limit_bytes=VMEM_LIMIT)


def _rows(tm, d):
    return pl.BlockSpec((tm, d), lambda i: (i, 0))


def _const(shape):
    return pl.BlockSpec(shape, lambda i: (0,) * len(shape))


def _halo_prev(tm, hb, d):
    return pl.BlockSpec((hb, d), lambda i: (jnp.maximum(i * (tm // hb) - 1, 0), 0))


def _halo_next(tm, hb, d, t):
    return pl.BlockSpec((hb, d), lambda i: (jnp.minimum((i + 1) * (tm // hb), t // hb - 1), 0))


def _for_chunks(n_rows, rc, fn):
    for r0 in range(0, n_rows, rc):
        fn(pl.ds(r0, rc))


def _fold8(v):
    return v.reshape(v.shape[0] // SUBLANES, SUBLANES, v.shape[1]).sum(axis=0)


def _mean_lanes(v):
    return jnp.mean(v, axis=-1, keepdims=True)


def _load_blocks_once(w_hbm, w_vmem, sem):
    nb, _, n = w_hbm.shape

    @pl.when(pl.program_id(0) == 0)
    def _():
        copies = [pltpu.make_async_copy(w_hbm.at[j], w_vmem.at[:, pl.ds(j * n, n)], sem.at[j])
                  for j in range(nb)]
        for cp in copies:
            cp.start()
        for cp in copies:
            cp.wait()


def _load_once(w_hbm, w_vmem, sem):
    @pl.when(pl.program_id(0) == 0)
    def _():
        cp = pltpu.make_async_copy(w_hbm, w_vmem, sem)
        cp.start()
        cp.wait()


def _write_row_sums(acc_ref, out_ref, n_steps):
    @pl.when(pl.program_id(0) == n_steps - 1)
    def _():
        out_ref[...] = jnp.sum(acc_ref[...], axis=0, keepdims=True)


def _place():
    return lax.axis_index("x"), lax.axis_index("y"), lax.axis_index("c")


def _other_chips(x, y):
    rel = [(x, 1 - y), (1 - x, y), (1 - x, 1 - y)]
    return [(px, py, 2 * px + py) for px, py in rel]


class _Exchange:
    def __init__(self, inputs, out_shapes, aliases, n_sems, copies):
        self.inputs = list(inputs)
        self.out_shapes = list(out_shapes)
        self.aliases = dict(aliases)
        self.n_sems = n_sems
        self.copies = copies


def _remote(src, dst, send, recv, device):
    return pltpu.make_async_remote_copy(src_ref=src, dst_ref=dst, send_sem=send, recv_sem=recv,
                                        device_id=device, device_id_type=MESH)


def _sds(a):
    return jax.ShapeDtypeStruct(a.shape, a.dtype)


def _ex_gather_ici(bufs):
    n = len(bufs)

    def copies(xin, xout, send, recv):
        x, y, c = _place()
        me = 2 * x + y
        out = []
        for a in range(n):
            hr = xin[a].shape[1] // 2
            rows = pl.ds(c * hr, hr)
            for j, (px, py, _) in enumerate(_other_chips(x, y)):
                k = a * (N_CHIPS - 1) + j
                out.append(_remote(xin[a].at[me, rows, :], xout[a].at[me, rows, :], send(k), recv(k), (px, py, c)))
        return out

    return _Exchange(bufs, [_sds(b) for b in bufs], {a: a for a in range(n)}, n * (N_CHIPS - 1), copies)


def _ex_gather_forward(bufs):
    n = len(bufs)

    def copies(xin, xout, send, recv):
        x, y, c = _place()
        out = []
        for a in range(n):
            hr = xin[a].shape[1] // 2
            rows = pl.ds(c * hr, hr)
            for j, (_, _, pk) in enumerate(_other_chips(x, y)):
                k = a * (N_CHIPS - 1) + j
                out.append(_remote(xin[a].at[pk, rows, :], xout[a].at[pk, rows, :], send(k), recv(k), (x, y, 1 - c)))
        return out

    return _Exchange(bufs, [_sds(b) for b in bufs], {a: a for a in range(n)}, n * (N_CHIPS - 1), copies)


def _ex_sibling_halves(grads):
    n = len(grads)

    def copies(xin, xout, send, recv):
        x, y, c = _place()
        out = []
        for a in range(n):
            hr = xin[a].shape[1] // 2
            out.append(_remote(xin[a].at[:, pl.ds((1 - c) * hr, hr), :], xout[a], send(a), recv(a), (x, y, 1 - c)))
        return out

    shapes = [jax.ShapeDtypeStruct((g.shape[0], g.shape[1] // 2, g.shape[2]), g.dtype) for g in grads]
    return _Exchange(grads, shapes, {}, n, copies)


def _ex_scatter_to_owner(pairs):
    n = len(pairs)

    def copies(xin, xout, send, recv):
        x, y, c = _place()
        out = []
        for a in range(n):
            for j, (px, py, pk) in enumerate(_other_chips(x, y)):
                k = a * (N_CHIPS - 1) + j
                out.append(_remote(xin[a].at[pk], xout[a].at[j], send(k), recv(k), (px, py, c)))
        return out

    shapes = [jax.ShapeDtypeStruct((N_CHIPS - 1,) + p.shape[1:], p.dtype) for p in pairs]
    return _Exchange(pairs, shapes, {}, n * (N_CHIPS - 1), copies)


def _ex_share_halves(reduced):
    n = len(reduced)

    def copies(xin, xout, send, recv):
        x, y, c = _place()
        out = []
        for a in range(n):
            hr = xin[a].shape[0] // 2
            rows = pl.ds(c * hr, hr)
            out.append(_remote(xin[a].at[rows, :], xout[a].at[rows, :], send(a), recv(a), (x, y, 1 - c)))
        return out

    return _Exchange(reduced, [_sds(r) for r in reduced], {a: a for a in range(n)}, n, copies)


def _merge(*exs):
    exs = [e for e in exs if e is not None]
    if not exs:
        return None
    inputs, shapes, aliases = [], [], {}
    in_off, out_off, sem_off = [], [], []
    n_sems = 0
    for e in exs:
        in_off.append(len(inputs))
        out_off.append(len(shapes))
        sem_off.append(n_sems)
        aliases.update({len(inputs) + i: len(shapes) + o for i, o in e.aliases.items()})
        inputs += e.inputs
        shapes += e.out_shapes
        n_sems += e.n_sems

    def copies(xin, xout, send, recv):
        out = []
        for e, io, oo, so in zip(exs, in_off, out_off, sem_off):
            out += e.copies(xin[io:io + len(e.inputs)], xout[oo:oo + len(e.out_shapes)],
                            lambda i, so=so: send(so + i), lambda i, so=so: recv(so + i))
        return out

    return _Exchange(inputs, shapes, aliases, n_sems, copies)


def _split(ex_outs, *exs):
    parts, o = [], 0
    for e in exs:
        parts.append(list(ex_outs[o:o + len(e.out_shapes)]))
        o += len(e.out_shapes)
    return parts


def _call(body, *, name, grid, in_specs, out_specs, out_shape, scratch_shapes, args, ex=None, aliases=None):
    n_in, n_out, n_scr = len(in_specs), len(out_specs), len(scratch_shapes)
    seq = ("arbitrary",) * len(grid)
    aliases = dict(aliases or {})
    if ex is None:
        outs = pl.pallas_call(
            body, name=name, grid=grid, in_specs=list(in_specs), out_specs=list(out_specs),
            out_shape=list(out_shape), scratch_shapes=list(scratch_shapes), input_output_aliases=aliases,
            compiler_params=_params(*seq))(*args)
        return list(outs), []
    n_xi, n_xo = len(ex.inputs), len(ex.out_shapes)

    def full(*refs):
        ins, xin = refs[:n_in], refs[n_in:n_in + n_xi]
        o = n_in + n_xi
        outs, xout = refs[o:o + n_out], refs[o + n_out:o + n_out + n_xo]
        s = o + n_out + n_xo
        scr = refs[s:s + n_scr]
        send_sems, recv_sems = refs[s + n_scr:]
        send = lambda i: send_sems.at[i]
        recv = lambda i: recv_sems.at[i]
        first = functools.reduce(jnp.logical_and, [pl.program_id(a) == 0 for a in range(len(grid))])
        last = functools.reduce(jnp.logical_and, [pl.program_id(a) == grid[a] - 1 for a in range(len(grid))])

        @pl.when(first)
        def _():
            for cp in ex.copies(xin, xout, send, recv):
                cp.start()

        body(*ins, *outs, *scr)

        @pl.when(last)
        def _():
            for cp in ex.copies(xin, xout, send, recv):
                cp.wait()

    res = pl.pallas_call(
        full, name=name, grid=grid, in_specs=list(in_specs) + [ANY] * n_xi,
        out_specs=list(out_specs) + [ANY] * n_xo, out_shape=list(out_shape) + ex.out_shapes,
        scratch_shapes=list(scratch_shapes) + [pltpu.SemaphoreType.DMA((ex.n_sems,)),
                                               pltpu.SemaphoreType.DMA((ex.n_sems,))],
        input_output_aliases={**aliases, **{n_in + i: n_out + o for i, o in ex.aliases.items()}},
        compiler_params=pltpu.CompilerParams(dimension_semantics=seq, vmem_limit_bytes=VMEM_LIMIT,
                                             has_side_effects=True))(*args, *ex.inputs)
    return list(res[:n_out]), list(res[n_out:])


def _exchange_call(name, phases):
    first = phases[0]
    n_xi, n_xo = len(first.inputs), len(first.out_shapes)

    def body(*refs):
        xin, xout = refs[:n_xi], refs[n_xi:n_xi + n_xo]
        sems = refs[n_xi + n_xo:]
        for p, ex in enumerate(phases):
            send_sems, recv_sems = sems[2 * p], sems[2 * p + 1]
            cps = ex.copies(xin, xout, lambda i: send_sems.at[i], lambda i: recv_sems.at[i])
            for cp in cps:
                cp.start()
            for cp in cps:
                cp.wait()

    sems = []
    for ex in phases:
        sems += [pltpu.SemaphoreType.DMA((ex.n_sems,)), pltpu.SemaphoreType.DMA((ex.n_sems,))]
    return list(pl.pallas_call(
        body, name=name, in_specs=[ANY] * n_xi, out_specs=[ANY] * n_xo, out_shape=first.out_shapes,
        scratch_shapes=sems, input_output_aliases=dict(first.aliases),
        compiler_params=pltpu.CompilerParams(has_side_effects=True))(*first.inputs))


def _cast_to_slot(w, chip, name):
    r, c = w.shape
    tr = min(r, 256)

    def body(chip_ref, w_ref, o_ref):
        o_ref[0] = w_ref[...].astype(BF16)

    return pl.pallas_call(
        body, name=name,
        grid_spec=pltpu.PrefetchScalarGridSpec(
            num_scalar_prefetch=1, grid=(r // tr,),
            in_specs=[pl.BlockSpec((tr, c), lambda i, k: (i, 0))],
            out_specs=pl.BlockSpec((1, tr, c), lambda i, k: (k[0], i, 0))),
        out_shape=jax.ShapeDtypeStruct((N_CHIPS, r, c), BF16),
        compiler_params=_params("parallel"))(chip, w)


def _proj_call(x, g1pre, w_in_g, b_in, tm, ex=None):
    t, d = x.shape
    nb, _, n4 = w_in_g.shape
    ni = nb * n4
    assert ni == 7 * d

    def body(x_ref, g_ref, b_ref, w_hbm, h_ref, ua_ref, ub_ref, bg_ref, cg_ref, ha_ref, a_ref,
             sg_ref, sa_ref, sb_ref, w_v, p0, p1, sem):
        _load_blocks_once(w_hbm, w_v, sem)

        def norm(rows):
            xv = x_ref[rows, :]
            r = lax.rsqrt(_mean_lanes(xv * xv) + RMS_EPS)
            h_ref[rows, :] = (xv * r * g_ref[...]).astype(BF16)
        _for_chunks(tm, ROW_CHUNK, norm)

        def group(i, dst):
            cols = pl.ds(i * d, d)
            dst[...] = jnp.dot(h_ref[...], w_v[:, cols], preferred_element_type=F32) + b_ref[:, cols]

        group(0, p0)

        def bgate(rows):
            bg_ref[rows, :] = p0[rows, :].astype(BF16)
        _for_chunks(tm, ROW_CHUNK, bgate)

        group(1, p0)
        group(2, p1)

        def branch_a(rows):
            cg, ha = p0[rows, :], p1[rows, :]
            ua_ref[rows, :] = cg * ha
            cg_ref[rows, :] = cg.astype(BF16)
            ha_ref[rows, :] = ha.astype(BF16)
        _for_chunks(tm, ROW_CHUNK, branch_a)

        group(3, p0)
        group(4, p1)

        def branch_b(rows):
            a, sg = p0[rows, :], jax.nn.sigmoid(p1[rows, :])
            ub_ref[rows, :] = a * sg
            a_ref[rows, :] = a.astype(BF16)
            sg_ref[rows, :] = sg.astype(BF16)
        _for_chunks(tm, ROW_CHUNK, branch_b)

        group(5, p0)
        group(6, p1)

        def gates(rows):
            sa_ref[rows, :] = jax.nn.sigmoid(p0[rows, :]).astype(BF16)
            sb_ref[rows, :] = jax.nn.sigmoid(p1[rows, :]).astype(BF16)
        _for_chunks(tm, ROW_CHUNK, gates)

    bf = jax.ShapeDtypeStruct((t, d), BF16)
    f32 = jax.ShapeDtypeStruct((t, d), F32)
    return _call(
        body, name="proj_fwd", grid=(t // tm,),
        in_specs=[_rows(tm, d), _const((1, d)), _const((1, ni)), ANY],
        out_specs=[_rows(tm, d)] * 10,
        out_shape=[bf, f32, f32, bf, bf, bf, bf, bf, bf, bf],
        scratch_shapes=[pltpu.VMEM((d, ni), BF16), pltpu.VMEM((tm, d), F32), pltpu.VMEM((tm, d), F32),
                        pltpu.SemaphoreType.DMA((nb,))],
        args=(x, g1pre, b_in, w_in_g), ex=ex)


def _fill_ext(ext, prev_ref, cur_ref, next_ref, hb, tm, i, n_steps):
    ext[pl.ds(0, hb), :] = jnp.where(i > 0, prev_ref[...], 0.0)
    ext[pl.ds(hb, tm), :] = cur_ref[...]
    ext[pl.ds(hb + tm, hb), :] = jnp.where(i < n_steps - 1, next_ref[...], 0.0)


def _shift_plan(offsets):
    shifts = sorted({o % SUBLANES for o in offsets if o % SUBLANES})
    return {s: i for i, s in enumerate(shifts)}


def _shifted_rows(tm, offsets):
    return tm + SUBLANES * max(o // SUBLANES for o in offsets)


def _fill_shifted(ext, sh, plan):
    n = sh.shape[1]
    for s, i in plan.items():
        sh[i, :, :] = ext[pl.ds(s, n), :]


def _window(ext, sh, plan, offset, r0):
    q, s = divmod(offset, SUBLANES)
    if s == 0:
        return ext[pl.ds(offset + r0, CONV_ROWS), :]
    return sh[plan[s], pl.ds(SUBLANES * q + r0, CONV_ROWS), :]


def _mixer_fwd_call(ua, ub, bg, sa, sb, x, conv_a_w, conv_a_b, conv_b_w, conv_b_b, ln_g, ln_b,
                    w_a, w_b, w_o, g1post, tm, ex=None):
    t, d = x.shape
    n_steps = t // tm
    ka, kb = conv_a_w.shape[0], conv_b_w.shape[0]
    off_a = [HALO_A - (ka - 1) // 2 + k for k in range(ka)]
    off_b = [HALO_B - (kb - 1) // 2 + k for k in range(kb)]
    plan_a, plan_b = _shift_plan(off_a), _shift_plan(off_b)

    def body(uap, uac, uan, ubp, ubc, ubn, bg_ref, sa_ref, sb_ref, x_ref, wa_c, ba_c, wb_c, bb_c,
             lng, lnb, wa_hbm, wb_hbm, wo_hbm, g_ref,
             x1_ref, va_ref, pa_ref, cb_ref, sbo_ref, ya_ref, yb_ref, mg_ref, mix_ref,
             ext_a, ext_b, sh_a, sh_b, wa_v, wb_v, wo_v, y0, y1, sem):
        i = pl.program_id(0)
        _load_once(wa_hbm, wa_v, sem.at[0])
        _load_once(wb_hbm, wb_v, sem.at[1])
        _load_once(wo_hbm, wo_v, sem.at[2])
        _fill_ext(ext_a, uap, uac, uan, HALO_A, tm, i, n_steps)
        _fill_ext(ext_b, ubp, ubc, ubn, HALO_B, tm, i, n_steps)
        _fill_shifted(ext_a, sh_a, plan_a)
        _fill_shifted(ext_b, sh_b, plan_b)

        for r0 in range(0, tm, CONV_ROWS):
            rows = pl.ds(r0, CONV_ROWS)
            va = jnp.broadcast_to(ba_c[...], (CONV_ROWS, d))
            for k in range(ka):
                va = va + wa_c[k:k + 1, :] * _window(ext_a, sh_a, plan_a, off_a[k], r0)
            va_ref[rows, :] = va.astype(BF16)
            pa_ref[rows, :] = (bg_ref[rows, :].astype(F32) * va).astype(BF16)
            cb = jnp.broadcast_to(bb_c[...], (CONV_ROWS, d))
            for k in range(kb):
                cb = cb + wb_c[k:k + 1, :] * _window(ext_b, sh_b, plan_b, off_b[k], r0)
            cb_ref[rows, :] = cb
            mu = _mean_lanes(cb)
            cen = cb - mu
            rstd = lax.rsqrt(_mean_lanes(cen * cen) + LN_EPS)
            ln = cen * rstd * lng[...] + lnb[...]
            sbo_ref[rows, :] = (ln * jax.nn.sigmoid(ln)).astype(BF16)

        y0[...] = jnp.dot(pa_ref[...], wa_v[...], preferred_element_type=F32)
        y1[...] = jnp.dot(sbo_ref[...], wb_v[...], preferred_element_type=F32)

        def merge(rows):
            ya, yb = y0[rows, :], y1[rows, :]
            ya_ref[rows, :] = ya.astype(BF16)
            yb_ref[rows, :] = yb.astype(BF16)
            mg_ref[rows, :] = (sa_ref[rows, :].astype(F32) * ya + sb_ref[rows, :].astype(F32) * yb).astype(BF16)
        _for_chunks(tm, ROW_CHUNK, merge)

        mix_ref[...] = jnp.dot(mg_ref[...], wo_v[...], preferred_element_type=F32)

        def resid(rows):
            mix = mix_ref[rows, :]
            r = lax.rsqrt(_mean_lanes(mix * mix) + RMS_EPS)
            x1_ref[rows, :] = x_ref[rows, :] + mix * r * g_ref[...]
        _for_chunks(tm, ROW_CHUNK, resid)

    bf = jax.ShapeDtypeStruct((t, d), BF16)
    f32 = jax.ShapeDtypeStruct((t, d), F32)
    return _call(
        body, name="mixer_fwd", grid=(n_steps,),
        in_specs=[_halo_prev(tm, HALO_A, d), _rows(tm, d), _halo_next(tm, HALO_A, d, t),
                  _halo_prev(tm, HALO_B, d), _rows(tm, d), _halo_next(tm, HALO_B, d, t),
                  _rows(tm, d), _rows(tm, d), _rows(tm, d), _rows(tm, d),
                  _const((ka, d)), _const((1, d)), _const((kb, d)), _const((1, d)),
                  _const((1, d)), _const((1, d)), ANY, ANY, ANY, _const((1, d))],
        out_specs=[_rows(tm, d)] * 9,
        out_shape=[f32, bf, bf, f32, bf, bf, bf, bf, f32],
        scratch_shapes=[pltpu.VMEM((tm + 2 * HALO_A, d), F32), pltpu.VMEM((tm + 2 * HALO_B, d), F32),
                        pltpu.VMEM((len(plan_a), _shifted_rows(tm, off_a), d), F32),
                        pltpu.VMEM((len(plan_b), _shifted_rows(tm, off_b), d), F32),
                        pltpu.VMEM((d, d), BF16), pltpu.VMEM((d, d), BF16), pltpu.VMEM((d, d), BF16),
                        pltpu.VMEM((tm, d), F32), pltpu.VMEM((tm, d), F32),
                        pltpu.SemaphoreType.DMA((3,))],
        args=(ua, ua, ua, ub, ub, ub, bg, sa, sb, x, conv_a_w, conv_a_b, conv_b_w, conv_b_b,
              ln_g, ln_b, w_a, w_b, w_o, g1post), ex=ex)


def _mlp_call(x1, target, g2pre, g2post, w1_g, w2, tm, ex=None):
    t, d = x1.shape
    nb, _, fq = w1_g.shape
    f = nb * fq
    n_steps = t // tm
    inv_d = 1.0 / d

    def body(x1_ref, t_ref, gpre, gpost, w1_hbm, w2_hbm,
             dx1_ref, f_ref, df2_ref, h2_ref, df1_ref, dgpost_ref, dgpre_ref, loss_ref,
             w1_v, w2_v, f1_s, blk_s, f2_s, acc_post, acc_pre, acc_loss, sem):
        _load_blocks_once(w1_hbm, w1_v, sem)
        _load_once(w2_hbm, w2_v, sem.at[nb])

        @pl.when(pl.program_id(0) == 0)
        def _():
            acc_post[...] = jnp.zeros_like(acc_post)
            acc_pre[...] = jnp.zeros_like(acc_pre)
            acc_loss[...] = jnp.zeros_like(acc_loss)

        def norm(rows):
            xv = x1_ref[rows, :]
            r = lax.rsqrt(_mean_lanes(xv * xv) + RMS_EPS)
            h2_ref[rows, :] = (xv * r * gpre[...]).astype(BF16)
        _for_chunks(tm, ROW_CHUNK, norm)

        for j in range(nb):
            cols = pl.ds(j * fq, fq)
            f1_s[:, cols] = jnp.dot(h2_ref[...], w1_v[:, cols], preferred_element_type=F32)

        def act(rows):
            relu = jnp.maximum(f1_s[rows, :], 0.0)
            f_ref[rows, :] = (relu * relu).astype(BF16)
        _for_chunks(tm, ROW_CHUNK, act)

        f2_s[...] = jnp.dot(f_ref[...], w2_v[...], preferred_element_type=F32)

        def head(rows):
            f2 = f2_s[rows, :]
            rf = lax.rsqrt(_mean_lanes(f2 * f2) + RMS_EPS)
            y = x1_ref[rows, :] + f2 * rf * gpost[...]
            err = y - t_ref[rows, :]
            acc_loss[...] += _fold8(err * err)
            dy = err * inv_d
            gdy = dy * gpost[...]
            df2 = rf * gdy - f2 * (rf * rf * rf * _mean_lanes(gdy * f2))
            df2_ref[rows, :] = df2.astype(BF16)
            acc_post[...] += _fold8(dy * f2 * rf)
            dx1_ref[rows, :] = dy
        _for_chunks(tm, ROW_CHUNK, head)

        for j in range(nb):
            cols = pl.ds(j * fq, fq)
            blk_s[...] = lax.dot_general(df2_ref[...], w2_v[cols, :], NT_DIMS, preferred_element_type=F32)

            def dact(rows):
                relu = jnp.maximum(f1_s[rows, cols], 0.0)
                df1_ref[rows, cols] = (blk_s[rows, :] * (2.0 * relu)).astype(BF16)
            _for_chunks(tm, ROW_CHUNK, dact)

        f2_s[...] = lax.dot_general(df1_ref[...], w1_v[...], NT_DIMS, preferred_element_type=F32)

        def dnorm(rows):
            dh2 = f2_s[rows, :]
            xv = x1_ref[rows, :]
            r = lax.rsqrt(_mean_lanes(xv * xv) + RMS_EPS)
            gd = dh2 * gpre[...]
            dx1_ref[rows, :] = dx1_ref[rows, :] + r * gd - xv * (r * r * r * _mean_lanes(gd * xv))
            acc_pre[...] += _fold8(dh2 * xv * r)
        _for_chunks(tm, ROW_CHUNK, dnorm)

        _write_row_sums(acc_post, dgpost_ref, n_steps)
        _write_row_sums(acc_pre, dgpre_ref, n_steps)
        _write_row_sums(acc_loss, loss_ref, n_steps)

    row = jax.ShapeDtypeStruct((1, d), F32)
    return _call(
        body, name="mlp_fwd_bwd", grid=(n_steps,),
        in_specs=[_rows(tm, d), _rows(tm, d), _const((1, d)), _const((1, d)), ANY, ANY],
        out_specs=[_rows(tm, d), _rows(tm, f), _rows(tm, d), _rows(tm, d), _rows(tm, f),
                   _const((1, d)), _const((1, d)), _const((1, d))],
        out_shape=[jax.ShapeDtypeStruct((t, d), F32), jax.ShapeDtypeStruct((t, f), BF16),
                   jax.ShapeDtypeStruct((t, d), BF16), jax.ShapeDtypeStruct((t, d), BF16),
                   jax.ShapeDtypeStruct((t, f), BF16), row, row, row],
        scratch_shapes=[pltpu.VMEM((d, f), BF16), pltpu.VMEM((f, d), BF16),
                        pltpu.VMEM((tm, f), F32), pltpu.VMEM((tm, fq), F32), pltpu.VMEM((tm, d), F32),
                        pltpu.VMEM((SUBLANES, d), F32), pltpu.VMEM((SUBLANES, d), F32),
                        pltpu.VMEM((SUBLANES, d), F32), pltpu.SemaphoreType.DMA((nb + 1,))],
        args=(x1, target, g2pre, g2post, w1_g, w2), ex=ex)


def _mixer_bwd_call(dx1, mix, sa, sb, ya, yb, bg, va, cb, g1post, ln_g, ln_b, w_a, w_b, w_o, tm, ex=None):
    t, d = dx1.shape
    n_steps = t // tm

    def body(dx1_ref, mix_ref, sa_ref, sb_ref, ya_ref, yb_ref, bg_ref, va_ref, cb_ref, g_ref, lng, lnb,
             wa_hbm, wb_hbm, wo_hbm,
             dmix_ref, dya_ref, dyb_ref, dva_ref, dcb_ref, dbg_ref, dza_ref, dzb_ref,
             dg_ref, dlng_ref, dlnb_ref, dba_ref, dbb_ref,
             wa_v, wb_v, wo_v, s0, s1, acc_g, acc_lng, acc_lnb, acc_ba, acc_bb, sem):
        _load_once(wa_hbm, wa_v, sem.at[0])
        _load_once(wb_hbm, wb_v, sem.at[1])
        _load_once(wo_hbm, wo_v, sem.at[2])
        accs = (acc_g, acc_lng, acc_lnb, acc_ba, acc_bb)

        @pl.when(pl.program_id(0) == 0)
        def _():
            for acc in accs:
                acc[...] = jnp.zeros_like(acc)

        def dnorm(rows):
            mix = mix_ref[rows, :]
            dxv = dx1_ref[rows, :]
            r = lax.rsqrt(_mean_lanes(mix * mix) + RMS_EPS)
            gd = dxv * g_ref[...]
            dmix_ref[rows, :] = (r * gd - mix * (r * r * r * _mean_lanes(gd * mix))).astype(BF16)
            acc_g[...] += _fold8(dxv * mix * r)
        _for_chunks(tm, ROW_CHUNK, dnorm)

        s0[...] = lax.dot_general(dmix_ref[...], wo_v[...], NT_DIMS, preferred_element_type=F32)

        def dmerge(rows):
            dm = s0[rows, :]
            sav, sbv = sa_ref[rows, :].astype(F32), sb_ref[rows, :].astype(F32)
            dya_ref[rows, :] = (dm * sav).astype(BF16)
            dyb_ref[rows, :] = (dm * sbv).astype(BF16)
            dza_ref[rows, :] = (dm * ya_ref[rows, :].astype(F32) * sav * (1.0 - sav)).astype(BF16)
            dzb_ref[rows, :] = (dm * yb_ref[rows, :].astype(F32) * sbv * (1.0 - sbv)).astype(BF16)
        _for_chunks(tm, ROW_CHUNK, dmerge)

        s0[...] = lax.dot_general(dya_ref[...], wa_v[...], NT_DIMS, preferred_element_type=F32)
        s1[...] = lax.dot_general(dyb_ref[...], wb_v[...], NT_DIMS, preferred_element_type=F32)

        def dbranches(rows):
            dpa = s0[rows, :]
            dbg_ref[rows, :] = (dpa * va_ref[rows, :].astype(F32)).astype(BF16)
            dva = dpa * bg_ref[rows, :].astype(F32)
            dva_ref[rows, :] = dva
            acc_ba[...] += _fold8(dva)
            cbv = cb_ref[rows, :]
            mu = _mean_lanes(cbv)
            cen = cbv - mu
            rstd = lax.rsqrt(_mean_lanes(cen * cen) + LN_EPS)
            xhat = cen * rstd
            ln = xhat * lng[...] + lnb[...]
            sig = jax.nn.sigmoid(ln)
            dln = s1[rows, :] * (sig * (1.0 + ln * (1.0 - sig)))
            acc_lng[...] += _fold8(dln * xhat)
            acc_lnb[...] += _fold8(dln)
            dxh = dln * lng[...]
            dcb = rstd * (dxh - _mean_lanes(dxh) - xhat * _mean_lanes(dxh * xhat))
            dcb_ref[rows, :] = dcb
            acc_bb[...] += _fold8(dcb)
        _for_chunks(tm, ROW_CHUNK, dbranches)

        _write_row_sums(acc_g, dg_ref, n_steps)
        _write_row_sums(acc_lng, dlng_ref, n_steps)
        _write_row_sums(acc_lnb, dlnb_ref, n_steps)
        _write_row_sums(acc_ba, dba_ref, n_steps)
        _write_row_sums(acc_bb, dbb_ref, n_steps)

    bf = jax.ShapeDtypeStruct((t, d), BF16)
    f32 = jax.ShapeDtypeStruct((t, d), F32)
    row = jax.ShapeDtypeStruct((1, d), F32)
    return _call(
        body, name="mixer_bwd", grid=(n_steps,),
        in_specs=[_rows(tm, d)] * 9 + [_const((1, d))] * 3 + [ANY, ANY, ANY],
        out_specs=[_rows(tm, d)] * 8 + [_const((1, d))] * 5,
        out_shape=[bf, bf, bf, f32, f32, bf, bf, bf, row, row, row, row, row],
        scratch_shapes=[pltpu.VMEM((d, d), BF16), pltpu.VMEM((d, d), BF16), pltpu.VMEM((d, d), BF16),
                        pltpu.VMEM((tm, d), F32), pltpu.VMEM((tm, d), F32)]
        + [pltpu.VMEM((SUBLANES, d), F32)] * 5 + [pltpu.SemaphoreType.DMA((3,))],
        args=(dx1, mix, sa, sb, ya, yb, bg, va, cb, g1post, ln_g, ln_b, w_a, w_b, w_o), ex=ex)


def _conv_bwd_call(dva, dcb, ua, ub, cg, ha, a, sg, dbg, dza, dzb, conv_a_w, conv_b_w, tm, ex=None):
    t, d = dva.shape
    n_steps = t // tm
    ka, kb = conv_a_w.shape[0], conv_b_w.shape[0]
    off_a = [HALO_A + (ka - 1) // 2 - k for k in range(ka)]
    off_b = [HALO_B + (kb - 1) // 2 - k for k in range(kb)]
    plan_a, plan_b = _shift_plan(off_a), _shift_plan(off_b)

    def body(dvap, dvac, dvan, dcbp, dcbc, dcbn, ua_ref, ub_ref,
             cg_ref, ha_ref, a_ref, sg_ref, dbg_ref, dza_ref, dzb_ref, wa_c, wb_c,
             dproj_ref, dwa_ref, dwb_ref, dbin_ref,
             e_dva, e_dcb, sh_a, sh_b, acc_wa, acc_wb, acc_bin):
        i = pl.program_id(0)

        @pl.when(i == 0)
        def _():
            acc_wa[...] = jnp.zeros_like(acc_wa)
            acc_wb[...] = jnp.zeros_like(acc_wb)
            acc_bin[...] = jnp.zeros_like(acc_bin)

        _fill_ext(e_dva, dvap, dvac, dvan, HALO_A, tm, i, n_steps)
        _fill_ext(e_dcb, dcbp, dcbc, dcbn, HALO_B, tm, i, n_steps)
        _fill_shifted(e_dva, sh_a, plan_a)
        _fill_shifted(e_dcb, sh_b, plan_b)

        def put(col, rows, val_f32):
            dproj_ref[rows, pl.ds(col * d, d)] = val_f32.astype(BF16)
            acc_bin[:, pl.ds(col * d, d)] += _fold8(val_f32)

        for r0 in range(0, tm, CONV_ROWS):
            rows = pl.ds(r0, CONV_ROWS)
            ua_c, ub_c = ua_ref[rows, :], ub_ref[rows, :]
            dua = jnp.zeros((CONV_ROWS, d), F32)
            for k in range(ka):
                xk = _window(e_dva, sh_a, plan_a, off_a[k], r0)
                dua = dua + wa_c[k:k + 1, :] * xk
                acc_wa[pl.ds(k * SUBLANES, SUBLANES), :] += _fold8(ua_c * xk)
            dub = jnp.zeros((CONV_ROWS, d), F32)
            for k in range(kb):
                xk = _window(e_dcb, sh_b, plan_b, off_b[k], r0)
                dub = dub + wb_c[k:k + 1, :] * xk
                acc_wb[pl.ds(k * SUBLANES, SUBLANES), :] += _fold8(ub_c * xk)
            cgv, hav = cg_ref[rows, :].astype(F32), ha_ref[rows, :].astype(F32)
            av, sgv = a_ref[rows, :].astype(F32), sg_ref[rows, :].astype(F32)
            put(0, rows, dbg_ref[rows, :].astype(F32))
            put(1, rows, dua * hav)
            put(2, rows, dua * cgv)
            put(3, rows, dub * sgv)
            put(4, rows, dub * av * sgv * (1.0 - sgv))
            put(5, rows, dza_ref[rows, :].astype(F32))
            put(6, rows, dzb_ref[rows, :].astype(F32))

        @pl.when(i == n_steps - 1)
        def _():
            for k in range(ka):
                dwa_ref[k:k + 1, :] = jnp.sum(acc_wa[pl.ds(k * SUBLANES, SUBLANES), :], axis=0, keepdims=True)
            for k in range(kb):
                dwb_ref[k:k + 1, :] = jnp.sum(acc_wb[pl.ds(k * SUBLANES, SUBLANES), :], axis=0, keepdims=True)
            dbin_ref[...] = jnp.sum(acc_bin[...], axis=0, keepdims=True)

    halo_a = [_halo_prev(tm, HALO_A, d), _rows(tm, d), _halo_next(tm, HALO_A, d, t)]
    halo_b = [_halo_prev(tm, HALO_B, d), _rows(tm, d), _halo_next(tm, HALO_B, d, t)]
    return _call(
        body, name="conv_bwd", grid=(n_steps,),
        in_specs=halo_a + halo_b + [_rows(tm, d)] * 9 + [_const((ka, d)), _const((kb, d))],
        out_specs=[_rows(tm, 7 * d), _const((ka, d)), _const((kb, d)), _const((1, 7 * d))],
        out_shape=[jax.ShapeDtypeStruct((t, 7 * d), BF16), jax.ShapeDtypeStruct((ka, d), F32),
                   jax.ShapeDtypeStruct((kb, d), F32), jax.ShapeDtypeStruct((1, 7 * d), F32)],
        scratch_shapes=[pltpu.VMEM((tm + 2 * HALO_A, d), F32), pltpu.VMEM((tm + 2 * HALO_B, d), F32),
                        pltpu.VMEM((len(plan_a), _shifted_rows(tm, off_a), d), F32),
                        pltpu.VMEM((len(plan_b), _shifted_rows(tm, off_b), d), F32),
                        pltpu.VMEM((ka * SUBLANES, d), F32), pltpu.VMEM((kb * SUBLANES, d), F32),
                        pltpu.VMEM((SUBLANES, 7 * d), F32)],
        args=(dva, dva, dva, dcb, dcb, dcb, ua, ub, cg, ha, a, sg, dbg, dza, dzb,
              conv_a_w, conv_b_w), ex=ex)


def _dx_call(dproj, x, dx1, g1pre, w_in_g, tm, part, n_parts, prev, ex=None):
    t, d = x.shape
    nb, _, n4 = w_in_g.shape
    ni = nb * n4
    n_steps = t // tm // n_parts
    first = part * n_steps
    rows = lambda width: pl.BlockSpec((tm, width), lambda i: (i + first, 0))
    if prev is None:
        prev = (jnp.zeros((SUBLANES, 128), F32), jnp.zeros((1, d), F32))
    prev_dx, prev_dg = prev

    def body(dp_ref, x_ref, dx1_ref, g_ref, w_hbm, prev_dx_hbm, prev_dg_ref, dx_ref, dg_ref, w_v, dh_s, acc_g, sem):
        _load_blocks_once(w_hbm, w_v, sem)

        @pl.when(pl.program_id(0) == 0)
        def _():
            acc_g[...] = jnp.zeros_like(acc_g)
            acc_g[0:1, :] = prev_dg_ref[...]

        dh_s[...] = lax.dot_general(dp_ref[...], w_v[...], NT_DIMS, preferred_element_type=F32)

        def dnorm(rows):
            dh = dh_s[rows, :]
            xv = x_ref[rows, :]
            r = lax.rsqrt(_mean_lanes(xv * xv) + RMS_EPS)
            gd = dh * g_ref[...]
            dx_ref[rows, :] = dx1_ref[rows, :] + r * gd - xv * (r * r * r * _mean_lanes(gd * xv))
            acc_g[...] += _fold8(dh * xv * r)
        _for_chunks(tm, ROW_CHUNK, dnorm)
        _write_row_sums(acc_g, dg_ref, n_steps)

    return _call(
        body, name="dx_bwd_%d" % part, grid=(n_steps,),
        in_specs=[rows(ni), rows(d), rows(d), _const((1, d)), ANY, ANY, _const((1, d))],
        out_specs=[rows(d), _const((1, d))],
        out_shape=[jax.ShapeDtypeStruct((t, d), F32), jax.ShapeDtypeStruct((1, d), F32)],
        scratch_shapes=[pltpu.VMEM((d, ni), BF16), pltpu.VMEM((tm, d), F32),
                        pltpu.VMEM((SUBLANES, d), F32), pltpu.SemaphoreType.DMA((nb,))],
        args=(dproj, x, dx1, g1pre, w_in_g, prev_dx, prev_dg), ex=ex,
        aliases={5: 0} if part > 0 else None)


def _tn_matmul(a, g, nblk, a_cols, g_cols, a_blocked, g_blocked, tt, name, ex=None):
    t = a.shape[0]

    def body(a_ref, g_ref, o_ref):
        @pl.when(pl.program_id(1) == 0)
        def _():
            o_ref[...] = jnp.zeros_like(o_ref)
        o_ref[0] += lax.dot_general(a_ref[...], g_ref[...], TN_DIMS, preferred_element_type=F32)

    (out,), xouts = _call(
        body, name=name, grid=(nblk, t // tt),
        in_specs=[pl.BlockSpec((tt, a_cols), (lambda b, s: (s, b)) if a_blocked else (lambda b, s: (s, 0))),
                  pl.BlockSpec((tt, g_cols), (lambda b, s: (s, b)) if g_blocked else (lambda b, s: (s, 0)))],
        out_specs=[pl.BlockSpec((1, a_cols, g_cols), lambda b, s: (b, 0, 0))],
        out_shape=[jax.ShapeDtypeStruct((nblk, a_cols, g_cols), F32)],
        scratch_shapes=[], args=(a, g), ex=ex)
    return out, xouts


def _pair_sum_call(g_full, from_sibling, core, name):
    nblk, r, c = g_full.shape
    hr = r // 2
    tr = min(hr, 256)
    n = hr // tr

    def body(core_ref, g_ref, p_ref, o_ref):
        o_ref[...] = (g_ref[...] + p_ref[...]).astype(BF16)

    return pl.pallas_call(
        body, name=name,
        grid_spec=pltpu.PrefetchScalarGridSpec(
            num_scalar_prefetch=1, grid=(nblk, n),
            in_specs=[pl.BlockSpec((1, tr, c), lambda j, i, cr: (j, cr[0] * n + i, 0)),
                      pl.BlockSpec((1, tr, c), lambda j, i, cr: (j, i, 0))],
            out_specs=pl.BlockSpec((1, tr, c), lambda j, i, cr: (j, i, 0))),
        out_shape=jax.ShapeDtypeStruct((nblk, hr, c), BF16),
        compiler_params=_params("parallel", "parallel"))(core, g_full, from_sibling)


def _chip_sum_call(pair, received, chip_core, name):
    _, hr, c = pair.shape
    tr = min(hr, 256)
    n = hr // tr

    def body(cc_ref, own_ref, r_ref, o_ref):
        o_ref[...] = ((own_ref[0].astype(F32) + r_ref[0].astype(F32)) + r_ref[1].astype(F32)) + r_ref[2].astype(F32)

    return pl.pallas_call(
        body, name=name,
        grid_spec=pltpu.PrefetchScalarGridSpec(
            num_scalar_prefetch=1, grid=(n,),
            in_specs=[pl.BlockSpec((1, tr, c), lambda i, cc: (cc[0], i, 0)),
                      pl.BlockSpec((N_CHIPS - 1, tr, c), lambda i, cc: (0, i, 0))],
            out_specs=pl.BlockSpec((tr, c), lambda i, cc: (cc[1] * n + i, 0))),
        out_shape=jax.ShapeDtypeStruct((2 * hr, c), F32),
        compiler_params=_params("parallel"))(chip_core, pair, received)


def _adamw(w, g, m, v):
    m = ADAM_B1 * m + (1.0 - ADAM_B1) * g
    v = ADAM_B2 * v + (1.0 - ADAM_B2) * (g * g)
    m_hat = m / (1.0 - ADAM_B1 ** ADAM_STEP)
    v_hat = v / (1.0 - ADAM_B2 ** ADAM_STEP)
    delta = -ADAM_LR * (m_hat / (jnp.sqrt(v_hat) + ADAM_EPS) + ADAM_WD * w)
    return delta, m, v


def _adam_call(w, g, m, v, name):
    r, c = w.shape
    tr = min(r, 256)

    def body(w_ref, g_ref, m_ref, v_ref, go_ref, d_ref, mo_ref, vo_ref):
        go_ref[...] = g_ref[...]
        d_ref[...], mo_ref[...], vo_ref[...] = _adamw(w_ref[...], g_ref[...], m_ref[...], v_ref[...])

    shape = jax.ShapeDtypeStruct((r, c), F32)
    return pl.pallas_call(
        body, name=name, grid=(r // tr,), in_specs=[_rows(tr, c)] * 4, out_specs=[_rows(tr, c)] * 4,
        out_shape=[shape] * 4, compiler_params=_params("parallel"))(w, g, m, v)


def _place():
    return lax.axis_index("x"), lax.axis_index("y"), lax.axis_index("c")


def _other_chips(x, y):
    rel = [(x, 1 - y), (1 - x, y), (1 - x, 1 - y)]
    return [(px, py, 2 * px + py) for px, py in rel]


def _gather_weights_call(shards):
    n = len(shards)

    def body(*refs):
        ins, outs = refs[:n], refs[n:2 * n]
        ici_send, ici_recv, fwd_send, fwd_recv, own_sem = refs[2 * n:]
        x, y, c = _place()
        me = 2 * x + y
        chips = _other_chips(x, y)
        started = []
        own = [pltpu.make_async_copy(ins[a], outs[a].at[me], own_sem.at[a]) for a in range(n)]
        for cp in own:
            cp.start()

        def half(ref, chip, core):
            hr = ref.shape[1] // 2
            return ref.at[chip, pl.ds(core * hr, hr), :]

        for a in range(n):
            hr = ins[a].shape[0] // 2
            for j, (px, py, _) in enumerate(chips):
                cp = pltpu.make_async_remote_copy(
                    src_ref=ins[a].at[pl.ds(c * hr, hr), :], dst_ref=half(outs[a], me, c),
                    send_sem=ici_send.at[a, j], recv_sem=ici_recv.at[a, j],
                    device_id=(px, py, c), device_id_type=MESH)
                cp.start()
                started.append(cp)
        for a in range(n):
            for j, (px, py, pk) in enumerate(chips):
                landed = half(outs[a], pk, c)
                pltpu.make_async_remote_copy(
                    src_ref=landed, dst_ref=landed, send_sem=ici_send.at[a, j], recv_sem=ici_recv.at[a, j],
                    device_id=(px, py, c), device_id_type=MESH).wait_recv()
                cp = pltpu.make_async_remote_copy(
                    src_ref=landed, dst_ref=landed, send_sem=fwd_send.at[a, j], recv_sem=fwd_recv.at[a, j],
                    device_id=(x, y, 1 - c), device_id_type=MESH)
                cp.start()
                started.append(cp)
        for a in range(n):
            for j, (px, py, pk) in enumerate(chips):
                passed = half(outs[a], pk, 1 - c)
                pltpu.make_async_remote_copy(
                    src_ref=passed, dst_ref=passed, send_sem=fwd_send.at[a, j], recv_sem=fwd_recv.at[a, j],
                    device_id=(x, y, 1 - c), device_id_type=MESH).wait_recv()
        for cp in started:
            cp.wait_send()
        for cp in own:
            cp.wait()

    return pl.pallas_call(
        body, name="gather_weights", in_specs=[ANY] * n, out_specs=[ANY] * n,
        out_shape=[jax.ShapeDtypeStruct((N_CHIPS,) + s.shape, s.dtype) for s in shards],
        scratch_shapes=[pltpu.SemaphoreType.DMA((n, N_CHIPS - 1)), pltpu.SemaphoreType.DMA((n, N_CHIPS - 1)),
                        pltpu.SemaphoreType.DMA((n, N_CHIPS - 1)), pltpu.SemaphoreType.DMA((n, N_CHIPS - 1)),
                        pltpu.SemaphoreType.DMA((n,))],
        compiler_params=pltpu.CompilerParams(has_side_effects=True))(*shards)


def _gather_conv_weights(conv_a_w, conv_b_w, d):
    ka, dq = conv_a_w.shape
    kb = conv_b_w.shape[0]
    ra = -(-ka // SUBLANES) * SUBLANES
    rb = -(-kb // SUBLANES) * SUBLANES
    a_pad = jnp.pad(conv_a_w, ((0, ra - ka), (0, 0)))
    b_pad = jnp.pad(conv_b_w, ((0, rb - kb), (0, 0)))

    def body(a_ref, b_ref, oa_ref, ob_ref, pack, slots, send_sem, recv_sem):
        x, y, c = _place()
        me = 2 * x + y
        chips = _other_chips(x, y)
        pack[pl.ds(0, ra), :] = a_ref[...]
        pack[pl.ds(ra, rb), :] = b_ref[...]
        copies = []
        for j, (px, py, _) in enumerate(chips):
            cp = pltpu.make_async_remote_copy(
                src_ref=pack, dst_ref=slots.at[me], send_sem=send_sem.at[j], recv_sem=recv_sem.at[j],
                device_id=(px, py, c), device_id_type=MESH)
            cp.start()
            copies.append(cp)
        for j, (px, py, pk) in enumerate(chips):
            pltpu.make_async_remote_copy(
                src_ref=pack, dst_ref=slots.at[pk], send_sem=send_sem.at[j], recv_sem=recv_sem.at[j],
                device_id=(px, py, c), device_id_type=MESH).wait_recv()
        for cp in copies:
            cp.wait_send()
        slots[me] = pack[...]
        for k in range(N_CHIPS):
            oa_ref[:, pl.ds(k * dq, dq)] = slots[k, pl.ds(0, ra), :]
            ob_ref[:, pl.ds(k * dq, dq)] = slots[k, pl.ds(ra, rb), :]

    oa, ob = pl.pallas_call(
        body, name="gather_conv_weights", in_specs=[VMEM_FULL] * 2, out_specs=[VMEM_FULL] * 2,
        out_shape=[jax.ShapeDtypeStruct((ra, d), F32), jax.ShapeDtypeStruct((rb, d), F32)],
        scratch_shapes=[pltpu.VMEM((ra + rb, dq), F32), pltpu.VMEM((N_CHIPS, ra + rb, dq), F32),
                        pltpu.SemaphoreType.DMA((N_CHIPS - 1,)), pltpu.SemaphoreType.DMA((N_CHIPS - 1,))],
        compiler_params=pltpu.CompilerParams(has_side_effects=True))(a_pad, b_pad)
    return oa[:ka], ob[:kb]


def _sibling_halves_call(grads):
    n = len(grads)

    def body(*refs):
        ins, outs = refs[:n], refs[n:2 * n]
        send_sem, recv_sem = refs[2 * n:]
        x, y, c = _place()
        copies = []
        for a in range(n):
            hr = ins[a].shape[1] // 2
            cp = pltpu.make_async_remote_copy(
                src_ref=ins[a].at[:, pl.ds((1 - c) * hr, hr), :], dst_ref=outs[a],
                send_sem=send_sem.at[a], recv_sem=recv_sem.at[a],
                device_id=(x, y, 1 - c), device_id_type=MESH)
            cp.start()
            copies.append(cp)
        for cp in copies:
            cp.wait()

    return pl.pallas_call(
        body, name="grads_to_sibling", in_specs=[ANY] * n, out_specs=[ANY] * n,
        out_shape=[jax.ShapeDtypeStruct((g.shape[0], g.shape[1] // 2, g.shape[2]), g.dtype) for g in grads],
        scratch_shapes=[pltpu.SemaphoreType.DMA((n,)), pltpu.SemaphoreType.DMA((n,))],
        compiler_params=pltpu.CompilerParams(has_side_effects=True))(*grads)


def _scatter_to_owner_call(pairs):
    n = len(pairs)

    def body(*refs):
        ins, outs = refs[:n], refs[n:2 * n]
        send_sem, recv_sem = refs[2 * n:]
        x, y, c = _place()
        chips = _other_chips(x, y)
        copies = []
        for a in range(n):
            for j, (px, py, pk) in enumerate(chips):
                cp = pltpu.make_async_remote_copy(
                    src_ref=ins[a].at[pk], dst_ref=outs[a].at[j],
                    send_sem=send_sem.at[a, j], recv_sem=recv_sem.at[a, j],
                    device_id=(px, py, c), device_id_type=MESH)
                cp.start()
                copies.append(cp)
        for cp in copies:
            cp.wait()

    return pl.pallas_call(
        body, name="grads_to_owner", in_specs=[ANY] * n, out_specs=[ANY] * n,
        out_shape=[jax.ShapeDtypeStruct((N_CHIPS - 1,) + p.shape[1:], p.dtype) for p in pairs],
        scratch_shapes=[pltpu.SemaphoreType.DMA((n, N_CHIPS - 1)), pltpu.SemaphoreType.DMA((n, N_CHIPS - 1))],
        compiler_params=pltpu.CompilerParams(has_side_effects=True))(*pairs)


def _share_halves_call(halves):
    n = len(halves)

    def body(*refs):
        ins, outs = refs[:n], refs[n:2 * n]
        send_sem, recv_sem, own_sem = refs[2 * n:]
        x, y, c = _place()
        copies, own = [], []
        for a in range(n):
            hr = ins[a].shape[0]
            mine = outs[a].at[pl.ds(c * hr, hr), :]
            cp = pltpu.make_async_copy(ins[a], mine, own_sem.at[a])
            cp.start()
            own.append(cp)
            cp = pltpu.make_async_remote_copy(
                src_ref=ins[a], dst_ref=mine, send_sem=send_sem.at[a], recv_sem=recv_sem.at[a],
                device_id=(x, y, 1 - c), device_id_type=MESH)
            cp.start()
            copies.append(cp)
        for a in range(n):
            hr = ins[a].shape[0]
            theirs = outs[a].at[pl.ds((1 - c) * hr, hr), :]
            pltpu.make_async_remote_copy(
                src_ref=ins[a], dst_ref=theirs, send_sem=send_sem.at[a], recv_sem=recv_sem.at[a],
                device_id=(x, y, 1 - c), device_id_type=MESH).wait_recv()
        for cp in copies:
            cp.wait_send()
        for cp in own:
            cp.wait()

    return pl.pallas_call(
        body, name="halves_to_sibling", in_specs=[ANY] * n, out_specs=[ANY] * n,
        out_shape=[jax.ShapeDtypeStruct((2 * h.shape[0], h.shape[1]), h.dtype) for h in halves],
        scratch_shapes=[pltpu.SemaphoreType.DMA((n,)), pltpu.SemaphoreType.DMA((n,)),
                        pltpu.SemaphoreType.DMA((n,))],
        compiler_params=pltpu.CompilerParams(has_side_effects=True))(*halves)


def _small_step_call(partials, loss_rows, weights, m_s, v_s, sharded, d):
    n = len(partials)
    row_counts = [p.shape[0] for p in partials]
    starts = [sum(row_counts[:i]) for i in range(n)]
    loss_row = sum(row_counts)
    pack_rows = -(-(loss_row + 1) // SUBLANES) * SUBLANES
    dq = d // N_CHIPS

    def body(*refs):
        p_refs = refs[:n]
        loss_in = refs[n]
        w_refs = refs[n + 1:2 * n + 1]
        m_refs = refs[2 * n + 1:3 * n + 1]
        v_refs = refs[3 * n + 1:4 * n + 1]
        o = 4 * n + 1
        g_out = refs[o:o + n]
        d_out = refs[o + n:o + 2 * n]
        m_out = refs[o + 2 * n:o + 3 * n]
        v_out = refs[o + 3 * n:o + 4 * n]
        loss_out = refs[o + 4 * n]
        pack, slots, send_sem, recv_sem = refs[o + 4 * n + 1:]
        x, y, c = _place()
        me = 4 * x + 2 * y + c

        pack[...] = jnp.zeros_like(pack)
        for i in range(n):
            pack[pl.ds(starts[i], row_counts[i]), :] = p_refs[i][...]
        pack[pl.ds(loss_row, 1), :] = loss_in[...]

        copies = []
        for rel in range(1, N_DEV):
            peer = (x ^ (rel >> 2), y ^ ((rel >> 1) & 1), c ^ (rel & 1))
            cp = pltpu.make_async_remote_copy(
                src_ref=pack, dst_ref=slots.at[me], send_sem=send_sem.at[rel - 1], recv_sem=recv_sem.at[rel - 1],
                device_id=peer, device_id_type=MESH)
            cp.start()
            copies.append(cp)
        for rel in range(1, N_DEV):
            peer_idx = me ^ rel
            pltpu.make_async_remote_copy(
                src_ref=pack, dst_ref=slots.at[peer_idx], send_sem=send_sem.at[rel - 1],
                recv_sem=recv_sem.at[rel - 1], device_id=(x, y, c), device_id_type=MESH).wait_recv()
        for cp in copies:
            cp.wait_send()

        slots[me] = pack[...]
        total = slots[0]
        for dev in range(1, N_DEV):
            total = total + slots[dev]
        pack[...] = total

        loss_out[...] = jnp.broadcast_to(
            (0.5 / d) * jnp.sum(pack[pl.ds(loss_row, 1), :], axis=-1, keepdims=True), loss_out.shape)
        chip = 2 * x + y
        for i in range(n):
            rows = pl.ds(starts[i], row_counts[i])
            if sharded[i]:
                for k in range(N_CHIPS):
                    @pl.when(chip == k)
                    def _():
                        g_out[i][...] = pack[rows, pl.ds(k * dq, dq)]
            else:
                g_out[i][...] = pack[rows, :]
            d_out[i][...], m_out[i][...], v_out[i][...] = _adamw(
                w_refs[i][...], g_out[i][...], m_refs[i][...], v_refs[i][...])

    w_shapes = [jax.ShapeDtypeStruct(w.shape, F32) for w in weights]
    n_in = 4 * n + 1
    return pl.pallas_call(
        body, name="small_grads_allreduce_adamw",
        in_specs=[VMEM_FULL] * n_in, out_specs=[VMEM_FULL] * (4 * n + 1),
        out_shape=w_shapes * 4 + [jax.ShapeDtypeStruct((SUBLANES, 128), F32)],
        scratch_shapes=[pltpu.VMEM((pack_rows, d), F32), pltpu.VMEM((N_DEV, pack_rows, d), F32),
                        pltpu.SemaphoreType.DMA((N_DEV - 1,)), pltpu.SemaphoreType.DMA((N_DEV - 1,))],
        compiler_params=pltpu.CompilerParams(has_side_effects=True, vmem_limit_bytes=VMEM_LIMIT))(
            *partials, loss_rows, *weights, *m_s, *v_s)


def _tile(t, want):
    return min(t, want)


def kernel(x, norm1_pre_g, w_in, b_in, conv_a_w, conv_a_b, w_a_out, conv_b_w, conv_b_b, ln_b_g, ln_b_b, w_b_out, w_o, norm1_post_g, norm2_pre_g, w_mlp_in, w_mlp_out, norm2_post_g, loss_target, m_norm1_pre_g, m_w_in, m_b_in, m_conv_a_w, m_conv_a_b, m_w_a_out, m_conv_b_w, m_conv_b_b, m_ln_b_g, m_ln_b_b, m_w_b_out, m_w_o, m_norm1_post_g, m_norm2_pre_g, m_w_mlp_in, m_w_mlp_out, m_norm2_post_g, v_norm1_pre_g, v_w_in, v_b_in, v_conv_a_w, v_conv_a_b, v_w_a_out, v_conv_b_w, v_conv_b_b, v_ln_b_g, v_ln_b_b, v_w_b_out, v_w_o, v_norm1_post_g, v_norm2_pre_g, v_w_mlp_in, v_w_mlp_out, v_norm2_post_g):
    _, t, d = x.shape
    xt = x.reshape(t, d)
    tgt = loss_target.reshape(t, d)
    row = lambda vec: vec.reshape(1, -1)
    cx, cy, cc = _place()
    core = cc.astype(jnp.int32).reshape(1)
    chip = (2 * cx + cy).astype(jnp.int32).reshape(1)

    big = dict(w_in=w_in, w_a_out=w_a_out, w_b_out=w_b_out, w_o=w_o, w_mlp_in=w_mlp_in, w_mlp_out=w_mlp_out)
    names = list(big)
    chip_core = jnp.concatenate([chip, core])
    slot = {k: _cast_to_slot(big[k], chip, "cast_" + k) for k in names}
    mixer_w, mlp_w = ["w_a_out", "w_b_out", "w_o"], ["w_mlp_in", "w_mlp_out"]
    rows_of = lambda buf: buf.reshape(-1, buf.shape[-1])

    def pair_sums(keys, full, from_sibling):
        return [_pair_sum_call(g, p, core, "pair_sum_" + k) for k, g, p in zip(keys, full, from_sibling)]

    def chip_sums(keys, pairs, received):
        return [_chip_sum_call(p, r, chip_core, "chip_sum_" + k) for k, p, r in zip(keys, pairs, received)]

    (w_in_g,) = _exchange_call("gather_w_in", [_ex_gather_ici([slot["w_in"]]), _ex_gather_forward([slot["w_in"]])])

    g1pre, g1post, g2pre, g2post = row(norm1_pre_g), row(norm1_post_g), row(norm2_pre_g), row(norm2_post_g)
    lng, lnb, ba, bb = row(ln_b_g), row(ln_b_b), row(conv_a_b), row(conv_b_b)
    conv_a_full, conv_b_full = _gather_conv_weights(conv_a_w, conv_b_w, d)

    (h, ua, ub, bg, cg, ha, a, sg, sa, sb), landed = _proj_call(
        xt, g1pre, w_in_g, row(b_in), _tile(t, 512), ex=_ex_gather_ici([slot[k] for k in mixer_w + mlp_w]))
    w_a_g, w_b_g, w_o_g = _exchange_call("forward_mixer_weights", [_ex_gather_forward(landed[:3])])
    w_a_full, w_b_full, w_o_full = rows_of(w_a_g), rows_of(w_b_g), rows_of(w_o_g)
    (x1, va, pa, cb, sbo, ya, yb, mg, mix), (w1_g, w2_g) = _mixer_fwd_call(
        ua, ub, bg, sa, sb, xt, conv_a_full, ba, conv_b_full, bb, lng, lnb,
        w_a_full, w_b_full, w_o_full, g1post, _tile(t, 256), ex=_ex_gather_forward(landed[3:]))
    (dx1, f, df2, h2, df1, dg2post, dg2pre, loss_rows), _ = _mlp_call(
        x1, tgt, g2pre, g2post, w1_g, rows_of(w2_g), _tile(t, 256))

    tt = _tile(t, 1024)
    n4, fq, dq = w_in.shape[1], w_mlp_in.shape[1], d // N_CHIPS
    g_mlp = [_tn_matmul(h2, df1, N_CHIPS, d, fq, False, True, tt, "dw_mlp_in")[0],
             _tn_matmul(f, df2, N_CHIPS, fq, d, True, False, tt, "dw_mlp_out")[0]]
    (dmix, dya, dyb, dva, dcb, dbg, dza, dzb, dg1post, dlng, dlnb, dba, dbb), sib_mlp = _mixer_bwd_call(
        dx1, mix, sa, sb, ya, yb, bg, va, cb, g1post, lng, lnb, w_a_full, w_b_full, w_o_full, _tile(t, 256),
        ex=_ex_sibling_halves(g_mlp))
    p_mlp = pair_sums(mlp_w, g_mlp, sib_mlp)
    g_mix = [_tn_matmul(pa, dya, 1, d, d, False, False, tt, "dw_a_out")[0].reshape(N_CHIPS, dq, d),
             _tn_matmul(sbo, dyb, 1, d, d, False, False, tt, "dw_b_out")[0].reshape(N_CHIPS, dq, d),
             _tn_matmul(mg, dmix, 1, d, d, False, False, tt, "dw_o")[0].reshape(N_CHIPS, dq, d)]
    ex_a, ex_b = _ex_scatter_to_owner(p_mlp), _ex_sibling_halves(g_mix)
    (dproj, dwa_conv, dwb_conv, dbin), xo = _conv_bwd_call(
        dva, dcb, ua, ub, cg, ha, a, sg, dbg, dza, dzb, conv_a_full, conv_b_full, _tile(t, 256),
        ex=_merge(ex_a, ex_b))
    recv_mlp, sib_mix = _split(xo, ex_a, ex_b)
    r_mlp = chip_sums(mlp_w, p_mlp, recv_mlp)
    p_mix = pair_sums(mixer_w, g_mix, sib_mix)
    ex_a, ex_b = _ex_share_halves(r_mlp), _ex_scatter_to_owner(p_mix)
    g_in, xo = _tn_matmul(h, dproj, N_CHIPS, d, n4, False, True, tt, "dw_in", ex=_merge(ex_a, ex_b))
    red_mlp, recv_mix = _split(xo, ex_a, ex_b)
    r_mix = chip_sums(mixer_w, p_mix, recv_mix)
    ex_a, ex_b = _ex_sibling_halves([g_in]), _ex_share_halves(r_mix)
    dx_first, xo = _dx_call(dproj, xt, dx1, g1pre, w_in_g, _tile(t, 512), 0, 2, None, ex=_merge(ex_a, ex_b))
    sib_in, red_mix = _split(xo, ex_a, ex_b)
    p_in = pair_sums(["w_in"], [g_in], sib_in)
    (grad_x, dg1pre), recv_in = _dx_call(dproj, xt, dx1, g1pre, w_in_g, _tile(t, 512), 1, 2, dx_first,
                                         ex=_ex_scatter_to_owner(p_in))
    r_in = chip_sums(["w_in"], p_in, recv_in)
    red_in = _exchange_call("w_in_grad_to_sibling", [_ex_share_halves(r_in)])
    reduced = dict(zip(mlp_w + mixer_w + ["w_in"], red_mlp + red_mix + red_in))

    moments = dict(w_in=(m_w_in, v_w_in), w_a_out=(m_w_a_out, v_w_a_out), w_b_out=(m_w_b_out, v_w_b_out),
                   w_o=(m_w_o, v_w_o), w_mlp_in=(m_w_mlp_in, v_w_mlp_in), w_mlp_out=(m_w_mlp_out, v_w_mlp_out))
    out = {}
    for k in names:
        out[k] = tuple(_adam_call(big[k], reduced[k], *moments[k], "adamw_" + k))

    small = [
        ("conv_b_w", dwb_conv, conv_b_w, m_conv_b_w, v_conv_b_w, True),
        ("conv_b_b", dbb, bb, row(m_conv_b_b), row(v_conv_b_b), False),
        ("b_in", dbin.reshape(7, d), b_in.reshape(7, d), m_b_in.reshape(7, d), v_b_in.reshape(7, d), False),
        ("norm1_pre_g", dg1pre, row(norm1_pre_g), row(m_norm1_pre_g), row(v_norm1_pre_g), False),
        ("conv_a_w", dwa_conv, conv_a_w, m_conv_a_w, v_conv_a_w, True),
        ("conv_a_b", dba, ba, row(m_conv_a_b), row(v_conv_a_b), False),
        ("ln_b_g", dlng, lng, row(m_ln_b_g), row(v_ln_b_g), False),
        ("ln_b_b", dlnb, lnb, row(m_ln_b_b), row(v_ln_b_b), False),
        ("norm1_post_g", dg1post, g1post, row(m_norm1_post_g), row(v_norm1_post_g), False),
        ("norm2_pre_g", dg2pre, g2pre, row(m_norm2_pre_g), row(v_norm2_pre_g), False),
        ("norm2_post_g", dg2post, g2post, row(m_norm2_post_g), row(v_norm2_post_g), False),
    ]
    res = _small_step_call([s[1] for s in small], loss_rows, [s[2] for s in small], [s[3] for s in small],
                           [s[4] for s in small], [s[5] for s in small], d)
    ns = len(small)
    loss = res[4 * ns][0, 0]
    shapes = dict(norm1_pre_g=norm1_pre_g.shape, b_in=b_in.shape, conv_a_w=conv_a_w.shape,
                  conv_a_b=conv_a_b.shape, conv_b_w=conv_b_w.shape, conv_b_b=conv_b_b.shape,
                  ln_b_g=ln_b_g.shape, ln_b_b=ln_b_b.shape, norm1_post_g=norm1_post_g.shape,
                  norm2_pre_g=norm2_pre_g.shape, norm2_post_g=norm2_post_g.shape)
    for i, s in enumerate(small):
        out[s[0]] = tuple(res[q * ns + i].reshape(shapes[s[0]]) for q in range(4))

    order = ["norm1_pre_g", "w_in", "b_in", "conv_a_w", "conv_a_b", "w_a_out", "conv_b_w", "conv_b_b",
             "ln_b_g", "ln_b_b", "w_b_out", "w_o", "norm1_post_g", "norm2_pre_g", "w_mlp_in", "w_mlp_out",
             "norm2_post_g"]
    return (loss, grad_x.reshape(x.shape), *[out[k][0] for k in order], *[out[k][1] for k in order],
            *[out[k][2] for k in order], *[out[k][3] for k in order])
```

```python
import functools

import jax
import jax.numpy as jnp
from jax import lax
from jax.experimental import pallas as pl
from jax.experimental.pallas import tpu as pltpu

RMS_EPS = 1e-6
LN_EPS = 1e-5
ADAM_LR = 0.001
ADAM_B1 = 0.9
ADAM_B2 = 0.999
ADAM_EPS = 1e-08
ADAM_WD = 0.01
ADAM_STEP = 10

F32 = jnp.float32
BF16 = jnp.bfloat16
MESH = pl.DeviceIdType.MESH
ANY = pl.BlockSpec(memory_space=pl.ANY)
VMEM_FULL = pl.BlockSpec(memory_space=pltpu.VMEM)

V7X_VMEM_BYTES = 64 * 1024 * 1024
VMEM_LIMIT = V7X_VMEM_BYTES - 8 * 1024 * 1024
SUBLANES = 8
N_CHIPS = 4
N_DEV = 8
HALO_A = 8
HALO_B = 16
CONV_ROWS = 16
ROW_CHUNK = 32

NT_DIMS = (((1,), (1,)), ((), ()))
TN_DIMS = (((0,), (0,)), ((), ()))


def _params(*sem):
    return pltpu.CompilerParams(dimension_semantics=sem, vmem_limit_bytes=VMEM_LIMIT)


def _rows(tm, d):
    return pl.BlockSpec((tm, d), lambda i: (i, 0))


def _const(shape):
    return pl.BlockSpec(shape, lambda i: (0,) * len(shape))


def _halo_prev(tm, hb, d):
    return pl.BlockSpec((hb, d), lambda i: (jnp.maximum(i * (tm // hb) - 1, 0), 0))


def _halo_next(tm, hb, d, t):
    return pl.BlockSpec((hb, d), lambda i: (jnp.minimum((i + 1) * (tm // hb), t // hb - 1), 0))


def _for_chunks(n_rows, rc, fn):
    for r0 in range(0, n_rows, rc):
        fn(pl.ds(r0, rc))


def _fold8(v):
    return v.reshape(v.shape[0] // SUBLANES, SUBLANES, v.shape[1]).sum(axis=0)


def _mean_lanes(v):
    return jnp.mean(v, axis=-1, keepdims=True)


def _load_blocks_once(w_hbm, w_vmem, sem):
    nb, _, n = w_hbm.shape

    @pl.when(pl.program_id(0) == 0)
    def _():
        copies = [pltpu.make_async_copy(w_hbm.at[j], w_vmem.at[:, pl.ds(j * n, n)], sem.at[j])
                  for j in range(nb)]
        for cp in copies:
            cp.start()
        for cp in copies:
            cp.wait()


def _load_once(w_hbm, w_vmem, sem):
    @pl.when(pl.program_id(0) == 0)
    def _():
        cp = pltpu.make_async_copy(w_hbm, w_vmem, sem)
        cp.start()
        cp.wait()


def _write_row_sums(acc_ref, out_ref, n_steps):
    @pl.when(pl.program_id(0) == n_steps - 1)
    def _():
        out_ref[...] = jnp.sum(acc_ref[...], axis=0, keepdims=True)


def _place():
    return lax.axis_index("x"), lax.axis_index("y"), lax.axis_index("c")


def _other_chips(x, y):
    rel = [(x, 1 - y), (1 - x, y), (1 - x, 1 - y)]
    return [(px, py, 2 * px + py) for px, py in rel]


class _Exchange:
    def __init__(self, inputs, out_shapes, aliases, n_sems, copies):
        self.inputs = list(inputs)
        self.out_shapes = list(out_shapes)
        self.aliases = dict(aliases)
        self.n_sems = n_sems
        self.copies = copies


def _remote(src, dst, send, recv, device):
    return pltpu.make_async_remote_copy(src_ref=src, dst_ref=dst, send_sem=send, recv_sem=recv,
                                        device_id=device, device_id_type=MESH)


def _sds(a):
    return jax.ShapeDtypeStruct(a.shape, a.dtype)


def _ex_gather_ici(bufs):
    n = len(bufs)

    def copies(xin, xout, send, recv):
        x, y, c = _place()
        me = 2 * x + y
        out = []
        for a in range(n):
            hr = xin[a].shape[1] // 2
            rows = pl.ds(c * hr, hr)
            for j, (px, py, _) in enumerate(_other_chips(x, y)):
                k = a * (N_CHIPS - 1) + j
                out.append(_remote(xin[a].at[me, rows, :], xout[a].at[me, rows, :], send(k), recv(k), (px, py, c)))
        return out

    return _Exchange(bufs, [_sds(b) for b in bufs], {a: a for a in range(n)}, n * (N_CHIPS - 1), copies)


def _ex_gather_forward(bufs):
    n = len(bufs)

    def copies(xin, xout, send, recv):
        x, y, c = _place()
        out = []
        for a in range(n):
            hr = xin[a].shape[1] // 2
            rows = pl.ds(c * hr, hr)
            for j, (_, _, pk) in enumerate(_other_chips(x, y)):
                k = a * (N_CHIPS - 1) + j
                out.append(_remote(xin[a].at[pk, rows, :], xout[a].at[pk, rows, :], send(k), recv(k), (x, y, 1 - c)))
        return out

    return _Exchange(bufs, [_sds(b) for b in bufs], {a: a for a in range(n)}, n * (N_CHIPS - 1), copies)


def _ex_sibling_halves(grads):
    n = len(grads)

    def copies(xin, xout, send, recv):
        x, y, c = _place()
        out = []
        for a in range(n):
            hr = xin[a].shape[1] // 2
            out.append(_remote(xin[a].at[:, pl.ds((1 - c) * hr, hr), :], xout[a], send(a), recv(a), (x, y, 1 - c)))
        return out

    shapes = [jax.ShapeDtypeStruct((g.shape[0], g.shape[1] // 2, g.shape[2]), g.dtype) for g in grads]
    return _Exchange(grads, shapes, {}, n, copies)


def _ex_scatter_to_owner(pairs):
    n = len(pairs)

    def copies(xin, xout, send, recv):
        x, y, c = _place()
        out = []
        for a in range(n):
            for j, (px, py, pk) in enumerate(_other_chips(x, y)):
                k = a * (N_CHIPS - 1) + j
                out.append(_remote(xin[a].at[pk], xout[a].at[j], send(k), recv(k), (px, py, c)))
        return out

    shapes = [jax.ShapeDtypeStruct((N_CHIPS - 1,) + p.shape[1:], p.dtype) for p in pairs]
    return _Exchange(pairs, shapes, {}, n * (N_CHIPS - 1), copies)


def _ex_share_halves(reduced):
    n = len(reduced)

    def copies(xin, xout, send, recv):
        x, y, c = _place()
        out = []
        for a in range(n):
            hr = xin[a].shape[0] // 2
            rows = pl.ds(c * hr, hr)
            out.append(_remote(xin[a].at[rows, :], xout[a].at[rows, :], send(a), recv(a), (x, y, 1 - c)))
        return out

    return _Exchange(reduced, [_sds(r) for r in reduced], {a: a for a in range(n)}, n, copies)


def _merge(*exs):
    exs = [e for e in exs if e is not None]
    if not exs:
        return None
    inputs, shapes, aliases = [], [], {}
    in_off, out_off, sem_off = [], [], []
    n_sems = 0
    for e in exs:
        in_off.append(len(inputs))
        out_off.append(len(shapes))
        sem_off.append(n_sems)
        aliases.update({len(inputs) + i: len(shapes) + o for i, o in e.aliases.items()})
        inputs += e.inputs
        shapes += e.out_shapes
        n_sems += e.n_sems

    def copies(xin, xout, send, recv):
        out = []
        for e, io, oo, so in zip(exs, in_off, out_off, sem_off):
            out += e.copies(xin[io:io + len(e.inputs)], xout[oo:oo + len(e.out_shapes)],
                            lambda i, so=so: send(so + i), lambda i, so=so: recv(so + i))
        return out

    return _Exchange(inputs, shapes, aliases, n_sems, copies)


def _split(ex_outs, *exs):
    parts, o = [], 0
    for e in exs:
        parts.append(list(ex_outs[o:o + len(e.out_shapes)]))
        o += len(e.out_shapes)
    return parts


def _call(body, *, name, grid, in_specs, out_specs, out_shape, scratch_shapes, args, ex=None, aliases=None):
    n_in, n_out, n_scr = len(in_specs), len(out_specs), len(scratch_shapes)
    seq = ("arbitrary",) * len(grid)
    aliases = dict(aliases or {})
    if ex is None:
        outs = pl.pallas_call(
            body, name=name, grid=grid, in_specs=list(in_specs), out_specs=list(out_specs),
            out_shape=list(out_shape), scratch_shapes=list(scratch_shapes), input_output_aliases=aliases,
            compiler_params=_params(*seq))(*args)
        return list(outs), []
    n_xi, n_xo = len(ex.inputs), len(ex.out_shapes)

    def full(*refs):
        ins, xin = refs[:n_in], refs[n_in:n_in + n_xi]
        o = n_in + n_xi
        outs, xout = refs[o:o + n_out], refs[o + n_out:o + n_out + n_xo]
        s = o + n_out + n_xo
        scr = refs[s:s + n_scr]
        send_sems, recv_sems = refs[s + n_scr:]
        send = lambda i: send_sems.at[i]
        recv = lambda i: recv_sems.at[i]
        first = functools.reduce(jnp.logical_and, [pl.program_id(a) == 0 for a in range(len(grid))])
        last = functools.reduce(jnp.logical_and, [pl.program_id(a) == grid[a] - 1 for a in range(len(grid))])

        @pl.when(first)
        def _():
            for cp in ex.copies(xin, xout, send, recv):
                cp.start()

        body(*ins, *outs, *scr)

        @pl.when(last)
        def _():
            for cp in ex.copies(xin, xout, send, recv):
                cp.wait()

    res = pl.pallas_call(
        full, name=name, grid=grid, in_specs=list(in_specs) + [ANY] * n_xi,
        out_specs=list(out_specs) + [ANY] * n_xo, out_shape=list(out_shape) + ex.out_shapes,
        scratch_shapes=list(scratch_shapes) + [pltpu.SemaphoreType.DMA((ex.n_sems,)),
                                               pltpu.SemaphoreType.DMA((ex.n_sems,))],
        input_output_aliases={**aliases, **{n_in + i: n_out + o for i, o in ex.aliases.items()}},
        compiler_params=pltpu.CompilerParams(dimension_semantics=seq, vmem_limit_bytes=VMEM_LIMIT,
                                             has_side_effects=True))(*args, *ex.inputs)
    return list(res[:n_out]), list(res[n_out:])


def _exchange_call(name, phases):
    first = phases[0]
    n_xi, n_xo = len(first.inputs), len(first.out_shapes)

    def body(*refs):
        xin, xout = refs[:n_xi], refs[n_xi:n_xi + n_xo]
        sems = refs[n_xi + n_xo:]
        for p, ex in enumerate(phases):
            send_sems, recv_sems = sems[2 * p], sems[2 * p + 1]
            cps = ex.copies(xin, xout, lambda i: send_sems.at[i], lambda i: recv_sems.at[i])
            for cp in cps:
                cp.start()
            for cp in cps:
                cp.wait()

    sems = []
    for ex in phases:
        sems += [pltpu.SemaphoreType.DMA((ex.n_sems,)), pltpu.SemaphoreType.DMA((ex.n_sems,))]
    return list(pl.pallas_call(
        body, name=name, in_specs=[ANY] * n_xi, out_specs=[ANY] * n_xo, out_shape=first.out_shapes,
        scratch_shapes=sems, input_output_aliases=dict(first.aliases),
        compiler_params=pltpu.CompilerParams(has_side_effects=True))(*first.inputs))


def _cast_to_slot(w, chip, name):
    r, c = w.shape
    tr = min(r, 256)

    def body(chip_ref, w_ref, o_ref):
        o_ref[0] = w_ref[...].astype(BF16)

    return pl.pallas_call(
        body, name=name,
        grid_spec=pltpu.PrefetchScalarGridSpec(
            num_scalar_prefetch=1, grid=(r // tr,),
            in_specs=[pl.BlockSpec((tr, c), lambda i, k: (i, 0))],
            out_specs=pl.BlockSpec((1, tr, c), lambda i, k: (k[0], i, 0))),
        out_shape=jax.ShapeDtypeStruct((N_CHIPS, r, c), BF16),
        compiler_params=_params("parallel"))(chip, w)


def _proj_call(x, g1pre, w_in_g, b_in, tm, ex=None):
    t, d = x.shape
    nb, _, n4 = w_in_g.shape
    ni = nb * n4
    assert ni == 7 * d

    def body(x_ref, g_ref, b_ref, w_hbm, h_ref, ua_ref, ub_ref, bg_ref, cg_ref, ha_ref, a_ref,
             sg_ref, sa_ref, sb_ref, w_v, p0, p1, sem):
        _load_blocks_once(w_hbm, w_v, sem)

        def norm(rows):
            xv = x_ref[rows, :]
            r = lax.rsqrt(_mean_lanes(xv * xv) + RMS_EPS)
            h_ref[rows, :] = (xv * r * g_ref[...]).astype(BF16)
        _for_chunks(tm, ROW_CHUNK, norm)

        def group(i, dst):
            cols = pl.ds(i * d, d)
            dst[...] = jnp.dot(h_ref[...], w_v[:, cols], preferred_element_type=F32) + b_ref[:, cols]

        group(0, p0)

        def bgate(rows):
            bg_ref[rows, :] = p0[rows, :].astype(BF16)
        _for_chunks(tm, ROW_CHUNK, bgate)

        group(1, p0)
        group(2, p1)

        def branch_a(rows):
            cg, ha = p0[rows, :], p1[rows, :]
            ua_ref[rows, :] = cg * ha
            cg_ref[rows, :] = cg.astype(BF16)
            ha_ref[rows, :] = ha.astype(BF16)
        _for_chunks(tm, ROW_CHUNK, branch_a)

        group(3, p0)
        group(4, p1)

        def branch_b(rows):
            a, sg = p0[rows, :], jax.nn.sigmoid(p1[rows, :])
            ub_ref[rows, :] = a * sg
            a_ref[rows, :] = a.astype(BF16)
            sg_ref[rows, :] = sg.astype(BF16)
        _for_chunks(tm, ROW_CHUNK, branch_b)

        group(5, p0)
        group(6, p1)

        def gates(rows):
            sa_ref[rows, :] = jax.nn.sigmoid(p0[rows, :]).astype(BF16)
            sb_ref[rows, :] = jax.nn.sigmoid(p1[rows, :]).astype(BF16)
        _for_chunks(tm, ROW_CHUNK, gates)

    bf = jax.ShapeDtypeStruct((t, d), BF16)
    f32 = jax.ShapeDtypeStruct((t, d), F32)
    return _call(
        body, name="proj_fwd", grid=(t // tm,),
        in_specs=[_rows(tm, d), _const((1, d)), _const((1, ni)), ANY],
        out_specs=[_rows(tm, d)] * 10,
        out_shape=[bf, f32, f32, bf, bf, bf, bf, bf, bf, bf],
        scratch_shapes=[pltpu.VMEM((d, ni), BF16), pltpu.VMEM((tm, d), F32), pltpu.VMEM((tm, d), F32),
                        pltpu.SemaphoreType.DMA((nb,))],
        args=(x, g1pre, b_in, w_in_g), ex=ex)


def _fill_ext(ext, prev_ref, cur_ref, next_ref, hb, tm, i, n_steps):
    ext[pl.ds(0, hb), :] = jnp.where(i > 0, prev_ref[...], 0.0)
    ext[pl.ds(hb, tm), :] = cur_ref[...]
    ext[pl.ds(hb + tm, hb), :] = jnp.where(i < n_steps - 1, next_ref[...], 0.0)


def _shift_plan(offsets):
    shifts = sorted({o % SUBLANES for o in offsets if o % SUBLANES})
    return {s: i for i, s in enumerate(shifts)}


def _shifted_rows(tm, offsets):
    return tm + SUBLANES * max(o // SUBLANES for o in offsets)


def _fill_shifted(ext, sh, plan):
    n = sh.shape[1]
    for s, i in plan.items():
        sh[i, :, :] = ext[pl.ds(s, n), :]


def _fill_tap_rows(w_ref, rows8):
    @pl.when(pl.program_id(0) == 0)
    def _():
        for k in range(w_ref.shape[0]):
            rows8[pl.ds(k * SUBLANES, SUBLANES), :] = jnp.broadcast_to(w_ref[k:k + 1, :], (SUBLANES, w_ref.shape[1]))


def _tap(rows8, k):
    w8 = rows8[pl.ds(k * SUBLANES, SUBLANES), :]
    return jnp.concatenate([w8] * (CONV_ROWS // SUBLANES), axis=0)


def _window(ext, sh, plan, offset, r0):
    q, s = divmod(offset, SUBLANES)
    if s == 0:
        return ext[pl.ds(offset + r0, CONV_ROWS), :]
    return sh[plan[s], pl.ds(SUBLANES * q + r0, CONV_ROWS), :]


def _mixer_fwd_call(ua, ub, bg, sa, sb, x, conv_a_w, conv_a_b, conv_b_w, conv_b_b, ln_g, ln_b,
                    w_a, w_b, w_o, g1post, tm, ex=None):
    t, d = x.shape
    n_steps = t // tm
    ka, kb = conv_a_w.shape[0], conv_b_w.shape[0]
    off_a = [HALO_A - (ka - 1) // 2 + k for k in range(ka)]
    off_b = [HALO_B - (kb - 1) // 2 + k for k in range(kb)]
    plan_a, plan_b = _shift_plan(off_a), _shift_plan(off_b)

    def body(uap, uac, uan, ubp, ubc, ubn, bg_ref, sa_ref, sb_ref, x_ref, wa_c, ba_c, wb_c, bb_c,
             lng, lnb, wa_hbm, wb_hbm, wo_hbm, g_ref,
             x1_ref, va_ref, pa_ref, cb_ref, sbo_ref, ya_ref, yb_ref, mg_ref, mix_ref,
             ext_a, ext_b, sh_a, sh_b, wa8, wb8, wa_v, wb_v, wo_v, y0, y1, sem):
        i = pl.program_id(0)
        _fill_tap_rows(wa_c, wa8)
        _fill_tap_rows(wb_c, wb8)
        _load_once(wa_hbm, wa_v, sem.at[0])
        _load_once(wb_hbm, wb_v, sem.at[1])
        _load_once(wo_hbm, wo_v, sem.at[2])
        _fill_ext(ext_a, uap, uac, uan, HALO_A, tm, i, n_steps)
        _fill_ext(ext_b, ubp, ubc, ubn, HALO_B, tm, i, n_steps)
        _fill_shifted(ext_a, sh_a, plan_a)
        _fill_shifted(ext_b, sh_b, plan_b)

        for r0 in range(0, tm, CONV_ROWS):
            rows = pl.ds(r0, CONV_ROWS)
            va = jnp.broadcast_to(ba_c[...], (CONV_ROWS, d))
            for k in range(ka):
                va = va + _tap(wa8, k) * _window(ext_a, sh_a, plan_a, off_a[k], r0)
            va_ref[rows, :] = va.astype(BF16)
            pa_ref[rows, :] = (bg_ref[rows, :].astype(F32) * va).astype(BF16)
            cb = jnp.broadcast_to(bb_c[...], (CONV_ROWS, d))
            for k in range(kb):
                cb = cb + _tap(wb8, k) * _window(ext_b, sh_b, plan_b, off_b[k], r0)
            cb_ref[rows, :] = cb
            mu = _mean_lanes(cb)
            cen = cb - mu
            rstd = lax.rsqrt(_mean_lanes(cen * cen) + LN_EPS)
            ln = cen * rstd * lng[...] + lnb[...]
            sbo_ref[rows, :] = (ln * jax.nn.sigmoid(ln)).astype(BF16)

        y0[...] = jnp.dot(pa_ref[...], wa_v[...], preferred_element_type=F32)
        y1[...] = jnp.dot(sbo_ref[...], wb_v[...], preferred_element_type=F32)

        def merge(rows):
            ya, yb = y0[rows, :], y1[rows, :]
            ya_ref[rows, :] = ya.astype(BF16)
            yb_ref[rows, :] = yb.astype(BF16)
            mg_ref[rows, :] = (sa_ref[rows, :].astype(F32) * ya + sb_ref[rows, :].astype(F32) * yb).astype(BF16)
        _for_chunks(tm, ROW_CHUNK, merge)

        mix_ref[...] = jnp.dot(mg_ref[...], wo_v[...], preferred_element_type=F32)

        def resid(rows):
            mix = mix_ref[rows, :]
            r = lax.rsqrt(_mean_lanes(mix * mix) + RMS_EPS)
            x1_ref[rows, :] = x_ref[rows, :] + mix * r * g_ref[...]
        _for_chunks(tm, ROW_CHUNK, resid)

    bf = jax.ShapeDtypeStruct((t, d), BF16)
    f32 = jax.ShapeDtypeStruct((t, d), F32)
    return _call(
        body, name="mixer_fwd", grid=(n_steps,),
        in_specs=[_halo_prev(tm, HALO_A, d), _rows(tm, d), _halo_next(tm, HALO_A, d, t),
                  _halo_prev(tm, HALO_B, d), _rows(tm, d), _halo_next(tm, HALO_B, d, t),
                  _rows(tm, d), _rows(tm, d), _rows(tm, d), _rows(tm, d),
                  _const((ka, d)), _const((1, d)), _const((kb, d)), _const((1, d)),
                  _const((1, d)), _const((1, d)), ANY, ANY, ANY, _const((1, d))],
        out_specs=[_rows(tm, d)] * 9,
        out_shape=[f32, bf, bf, f32, bf, bf, bf, bf, f32],
        scratch_shapes=[pltpu.VMEM((tm + 2 * HALO_A, d), F32), pltpu.VMEM((tm + 2 * HALO_B, d), F32),
                        pltpu.VMEM((len(plan_a), _shifted_rows(tm, off_a), d), F32),
                        pltpu.VMEM((len(plan_b), _shifted_rows(tm, off_b), d), F32),
                        pltpu.VMEM((ka * SUBLANES, d), F32), pltpu.VMEM((kb * SUBLANES, d), F32),
                        pltpu.VMEM((d, d), BF16), pltpu.VMEM((d, d), BF16), pltpu.VMEM((d, d), BF16),
                        pltpu.VMEM((tm, d), F32), pltpu.VMEM((tm, d), F32),
                        pltpu.SemaphoreType.DMA((3,))],
        args=(ua, ua, ua, ub, ub, ub, bg, sa, sb, x, conv_a_w, conv_a_b, conv_b_w, conv_b_b,
              ln_g, ln_b, w_a, w_b, w_o, g1post), ex=ex)


def _mlp_call(x1, target, g2pre, g2post, w1_g, w2, tm, ex=None):
    t, d = x1.shape
    nb, _, fq = w1_g.shape
    f = nb * fq
    n_steps = t // tm
    inv_d = 1.0 / d

    def body(x1_ref, t_ref, gpre, gpost, w1_hbm, w2_hbm,
             dx1_ref, f_ref, df2_ref, h2_ref, df1_ref, dgpost_ref, dgpre_ref, loss_ref,
             w1_v, w2_v, f1_s, blk_s, f2_s, acc_post, acc_pre, acc_loss, sem):
        _load_blocks_once(w1_hbm, w1_v, sem)
        _load_once(w2_hbm, w2_v, sem.at[nb])

        @pl.when(pl.program_id(0) == 0)
        def _():
            acc_post[...] = jnp.zeros_like(acc_post)
            acc_pre[...] = jnp.zeros_like(acc_pre)
            acc_loss[...] = jnp.zeros_like(acc_loss)

        def norm(rows):
            xv = x1_ref[rows, :]
            r = lax.rsqrt(_mean_lanes(xv * xv) + RMS_EPS)
            h2_ref[rows, :] = (xv * r * gpre[...]).astype(BF16)
        _for_chunks(tm, ROW_CHUNK, norm)

        for j in range(nb):
            cols = pl.ds(j * fq, fq)
            f1_s[:, cols] = jnp.dot(h2_ref[...], w1_v[:, cols], preferred_element_type=F32)

        def act(rows):
            relu = jnp.maximum(f1_s[rows, :], 0.0)
            f_ref[rows, :] = (relu * relu).astype(BF16)
        _for_chunks(tm, ROW_CHUNK, act)

        f2_s[...] = jnp.dot(f_ref[...], w2_v[...], preferred_element_type=F32)

        def head(rows):
            f2 = f2_s[rows, :]
            rf = lax.rsqrt(_mean_lanes(f2 * f2) + RMS_EPS)
            y = x1_ref[rows, :] + f2 * rf * gpost[...]
            err = y - t_ref[rows, :]
            acc_loss[...] += _fold8(err * err)
            dy = err * inv_d
            gdy = dy * gpost[...]
            df2 = rf * gdy - f2 * (rf * rf * rf * _mean_lanes(gdy * f2))
            df2_ref[rows, :] = df2.astype(BF16)
            acc_post[...] += _fold8(dy * f2 * rf)
            dx1_ref[rows, :] = dy
        _for_chunks(tm, ROW_CHUNK, head)

        for j in range(nb):
            cols = pl.ds(j * fq, fq)
            blk_s[...] = lax.dot_general(df2_ref[...], w2_v[cols, :], NT_DIMS, preferred_element_type=F32)

            def dact(rows):
                relu = jnp.maximum(f1_s[rows, cols], 0.0)
                df1_ref[rows, cols] = (blk_s[rows, :] * (2.0 * relu)).astype(BF16)
            _for_chunks(tm, ROW_CHUNK, dact)

        f2_s[...] = lax.dot_general(df1_ref[...], w1_v[...], NT_DIMS, preferred_element_type=F32)

        def dnorm(rows):
            dh2 = f2_s[rows, :]
            xv = x1_ref[rows, :]
            r = lax.rsqrt(_mean_lanes(xv * xv) + RMS_EPS)
            gd = dh2 * gpre[...]
            dx1_ref[rows, :] = dx1_ref[rows, :] + r * gd - xv * (r * r * r * _mean_lanes(gd * xv))
            acc_pre[...] += _fold8(dh2 * xv * r)
        _for_chunks(tm, ROW_CHUNK, dnorm)

        _write_row_sums(acc_post, dgpost_ref, n_steps)
        _write_row_sums(acc_pre, dgpre_ref, n_steps)
        _write_row_sums(acc_loss, loss_ref, n_steps)

    row = jax.ShapeDtypeStruct((1, d), F32)
    return _call(
        body, name="mlp_fwd_bwd", grid=(n_steps,),
        in_specs=[_rows(tm, d), _rows(tm, d), _const((1, d)), _const((1, d)), ANY, ANY],
        out_specs=[_rows(tm, d), _rows(tm, f), _rows(tm, d), _rows(tm, d), _rows(tm, f),
                   _const((1, d)), _const((1, d)), _const((1, d))],
        out_shape=[jax.ShapeDtypeStruct((t, d), F32), jax.ShapeDtypeStruct((t, f), BF16),
                   jax.ShapeDtypeStruct((t, d), BF16), jax.ShapeDtypeStruct((t, d), BF16),
                   jax.ShapeDtypeStruct((t, f), BF16), row, row, row],
        scratch_shapes=[pltpu.VMEM((d, f), BF16), pltpu.VMEM((f, d), BF16),
                        pltpu.VMEM((tm, f), F32), pltpu.VMEM((tm, fq), F32), pltpu.VMEM((tm, d), F32),
                        pltpu.VMEM((SUBLANES, d), F32), pltpu.VMEM((SUBLANES, d), F32),
                        pltpu.VMEM((SUBLANES, d), F32), pltpu.SemaphoreType.DMA((nb + 1,))],
        args=(x1, target, g2pre, g2post, w1_g, w2), ex=ex)


def _mixer_bwd_call(dx1, mix, sa, sb, ya, yb, bg, va, cb, g1post, ln_g, ln_b, w_a, w_b, w_o, tm, ex=None):
    t, d = dx1.shape
    n_steps = t // tm

    def body(dx1_ref, mix_ref, sa_ref, sb_ref, ya_ref, yb_ref, bg_ref, va_ref, cb_ref, g_ref, lng, lnb,
             wa_hbm, wb_hbm, wo_hbm,
             dmix_ref, dya_ref, dyb_ref, dva_ref, dcb_ref, dbg_ref, dza_ref, dzb_ref,
             dg_ref, dlng_ref, dlnb_ref, dba_ref, dbb_ref,
             wa_v, wb_v, wo_v, s0, s1, acc_g, acc_lng, acc_lnb, acc_ba, acc_bb, sem):
        _load_once(wa_hbm, wa_v, sem.at[0])
        _load_once(wb_hbm, wb_v, sem.at[1])
        _load_once(wo_hbm, wo_v, sem.at[2])
        accs = (acc_g, acc_lng, acc_lnb, acc_ba, acc_bb)

        @pl.when(pl.program_id(0) == 0)
        def _():
            for acc in accs:
                acc[...] = jnp.zeros_like(acc)

        def dnorm(rows):
            mix = mix_ref[rows, :]
            dxv = dx1_ref[rows, :]
            r = lax.rsqrt(_mean_lanes(mix * mix) + RMS_EPS)
            gd = dxv * g_ref[...]
            dmix_ref[rows, :] = (r * gd - mix * (r * r * r * _mean_lanes(gd * mix))).astype(BF16)
            acc_g[...] += _fold8(dxv * mix * r)
        _for_chunks(tm, ROW_CHUNK, dnorm)

        s0[...] = lax.dot_general(dmix_ref[...], wo_v[...], NT_DIMS, preferred_element_type=F32)

        def dmerge(rows):
            dm = s0[rows, :]
            sav, sbv = sa_ref[rows, :].astype(F32), sb_ref[rows, :].astype(F32)
            dya_ref[rows, :] = (dm * sav).astype(BF16)
            dyb_ref[rows, :] = (dm * sbv).astype(BF16)
            dza_ref[rows, :] = (dm * ya_ref[rows, :].astype(F32) * sav * (1.0 - sav)).astype(BF16)
            dzb_ref[rows, :] = (dm * yb_ref[rows, :].astype(F32) * sbv * (1.0 - sbv)).astype(BF16)
        _for_chunks(tm, ROW_CHUNK, dmerge)

        s0[...] = lax.dot_general(dya_ref[...], wa_v[...], NT_DIMS, preferred_element_type=F32)
        s1[...] = lax.dot_general(dyb_ref[...], wb_v[...], NT_DIMS, preferred_element_type=F32)

        def dbranches(rows):
            dpa = s0[rows, :]
            dbg_ref[rows, :] = (dpa * va_ref[rows, :].astype(F32)).astype(BF16)
            dva = dpa * bg_ref[rows, :].astype(F32)
            dva_ref[rows, :] = dva
            acc_ba[...] += _fold8(dva)
            cbv = cb_ref[rows, :]
            mu = _mean_lanes(cbv)
            cen = cbv - mu
            rstd = lax.rsqrt(_mean_lanes(cen * cen) + LN_EPS)
            xhat = cen * rstd
            ln = xhat * lng[...] + lnb[...]
            sig = jax.nn.sigmoid(ln)
            dln = s1[rows, :] * (sig * (1.0 + ln * (1.0 - sig)))
            acc_lng[...] += _fold8(dln * xhat)
            acc_lnb[...] += _fold8(dln)
            dxh = dln * lng[...]
            dcb = rstd * (dxh - _mean_lanes(dxh) - xhat * _mean_lanes(dxh * xhat))
            dcb_ref[rows, :] = dcb
            acc_bb[...] += _fold8(dcb)
        _for_chunks(tm, ROW_CHUNK, dbranches)

        _write_row_sums(acc_g, dg_ref, n_steps)
        _write_row_sums(acc_lng, dlng_ref, n_steps)
        _write_row_sums(acc_lnb, dlnb_ref, n_steps)
        _write_row_sums(acc_ba, dba_ref, n_steps)
        _write_row_sums(acc_bb, dbb_ref, n_steps)

    bf = jax.ShapeDtypeStruct((t, d), BF16)
    f32 = jax.ShapeDtypeStruct((t, d), F32)
    row = jax.ShapeDtypeStruct((1, d), F32)
    return _call(
        body, name="mixer_bwd", grid=(n_steps,),
        in_specs=[_rows(tm, d)] * 9 + [_const((1, d))] * 3 + [ANY, ANY, ANY],
        out_specs=[_rows(tm, d)] * 8 + [_const((1, d))] * 5,
        out_shape=[bf, bf, bf, f32, f32, bf, bf, bf, row, row, row, row, row],
        scratch_shapes=[pltpu.VMEM((d, d), BF16), pltpu.VMEM((d, d), BF16), pltpu.VMEM((d, d), BF16),
                        pltpu.VMEM((tm, d), F32), pltpu.VMEM((tm, d), F32)]
        + [pltpu.VMEM((SUBLANES, d), F32)] * 5 + [pltpu.SemaphoreType.DMA((3,))],
        args=(dx1, mix, sa, sb, ya, yb, bg, va, cb, g1post, ln_g, ln_b, w_a, w_b, w_o), ex=ex)


def _conv_bwd_call(dva, dcb, ua, ub, cg, ha, a, sg, dbg, dza, dzb, conv_a_w, conv_b_w, tm, ex=None):
    t, d = dva.shape
    n_steps = t // tm
    ka, kb = conv_a_w.shape[0], conv_b_w.shape[0]
    off_a = [HALO_A + (ka - 1) // 2 - k for k in range(ka)]
    off_b = [HALO_B + (kb - 1) // 2 - k for k in range(kb)]
    plan_a, plan_b = _shift_plan(off_a), _shift_plan(off_b)

    def body(dvap, dvac, dvan, dcbp, dcbc, dcbn, ua_ref, ub_ref,
             cg_ref, ha_ref, a_ref, sg_ref, dbg_ref, dza_ref, dzb_ref, wa_c, wb_c,
             dproj_ref, dwa_ref, dwb_ref, dbin_ref,
             e_dva, e_dcb, sh_a, sh_b, wa8, wb8, acc_wa, acc_wb, acc_bin):
        i = pl.program_id(0)
        _fill_tap_rows(wa_c, wa8)
        _fill_tap_rows(wb_c, wb8)

        @pl.when(i == 0)
        def _():
            acc_wa[...] = jnp.zeros_like(acc_wa)
            acc_wb[...] = jnp.zeros_like(acc_wb)
            acc_bin[...] = jnp.zeros_like(acc_bin)

        _fill_ext(e_dva, dvap, dvac, dvan, HALO_A, tm, i, n_steps)
        _fill_ext(e_dcb, dcbp, dcbc, dcbn, HALO_B, tm, i, n_steps)
        _fill_shifted(e_dva, sh_a, plan_a)
        _fill_shifted(e_dcb, sh_b, plan_b)

        def put(col, rows, val_f32):
            dproj_ref[rows, pl.ds(col * d, d)] = val_f32.astype(BF16)
            acc_bin[:, pl.ds(col * d, d)] += _fold8(val_f32)

        for r0 in range(0, tm, CONV_ROWS):
            rows = pl.ds(r0, CONV_ROWS)
            ua_c, ub_c = ua_ref[rows, :], ub_ref[rows, :]
            dua = jnp.zeros((CONV_ROWS, d), F32)
            for k in range(ka):
                xk = _window(e_dva, sh_a, plan_a, off_a[k], r0)
                dua = dua + _tap(wa8, k) * xk
                acc_wa[pl.ds(k * SUBLANES, SUBLANES), :] += _fold8(ua_c * xk)
            dub = jnp.zeros((CONV_ROWS, d), F32)
            for k in range(kb):
                xk = _window(e_dcb, sh_b, plan_b, off_b[k], r0)
                dub = dub + _tap(wb8, k) * xk
                acc_wb[pl.ds(k * SUBLANES, SUBLANES), :] += _fold8(ub_c * xk)
            cgv, hav = cg_ref[rows, :].astype(F32), ha_ref[rows, :].astype(F32)
            av, sgv = a_ref[rows, :].astype(F32), sg_ref[rows, :].astype(F32)
            put(0, rows, dbg_ref[rows, :].astype(F32))
            put(1, rows, dua * hav)
            put(2, rows, dua * cgv)
            put(3, rows, dub * sgv)
            put(4, rows, dub * av * sgv * (1.0 - sgv))
            put(5, rows, dza_ref[rows, :].astype(F32))
            put(6, rows, dzb_ref[rows, :].astype(F32))

        @pl.when(i == n_steps - 1)
        def _():
            for k in range(ka):
                dwa_ref[k:k + 1, :] = jnp.sum(acc_wa[pl.ds(k * SUBLANES, SUBLANES), :], axis=0, keepdims=True)
            for k in range(kb):
                dwb_ref[k:k + 1, :] = jnp.sum(acc_wb[pl.ds(k * SUBLANES, SUBLANES), :], axis=0, keepdims=True)
            dbin_ref[...] = jnp.sum(acc_bin[...], axis=0, keepdims=True)

    halo_a = [_halo_prev(tm, HALO_A, d), _rows(tm, d), _halo_next(tm, HALO_A, d, t)]
    halo_b = [_halo_prev(tm, HALO_B, d), _rows(tm, d), _halo_next(tm, HALO_B, d, t)]
    return _call(
        body, name="conv_bwd", grid=(n_steps,),
        in_specs=halo_a + halo_b + [_rows(tm, d)] * 9 + [_const((ka, d)), _const((kb, d))],
        out_specs=[_rows(tm, 7 * d), _const((ka, d)), _const((kb, d)), _const((1, 7 * d))],
        out_shape=[jax.ShapeDtypeStruct((t, 7 * d), BF16), jax.ShapeDtypeStruct((ka, d), F32),
                   jax.ShapeDtypeStruct((kb, d), F32), jax.ShapeDtypeStruct((1, 7 * d), F32)],
        scratch_shapes=[pltpu.VMEM((tm + 2 * HALO_A, d), F32), pltpu.VMEM((tm + 2 * HALO_B, d), F32),
                        pltpu.VMEM((len(plan_a), _shifted_rows(tm, off_a), d), F32),
                        pltpu.VMEM((len(plan_b), _shifted_rows(tm, off_b), d), F32),
                        pltpu.VMEM((ka * SUBLANES, d), F32), pltpu.VMEM((kb * SUBLANES, d), F32),
                        pltpu.VMEM((ka * SUBLANES, d), F32), pltpu.VMEM((kb * SUBLANES, d), F32),
                        pltpu.VMEM((SUBLANES, 7 * d), F32)],
        args=(dva, dva, dva, dcb, dcb, dcb, ua, ub, cg, ha, a, sg, dbg, dza, dzb,
              conv_a_w, conv_b_w), ex=ex)


def _dx_call(dproj, x, dx1, g1pre, w_in_g, tm, part, n_parts, prev, ex=None):
    t, d = x.shape
    nb, _, n4 = w_in_g.shape
    ni = nb * n4
    n_steps = t // tm // n_parts
    first = part * n_steps
    rows = lambda width: pl.BlockSpec((tm, width), lambda i: (i + first, 0))
    if prev is None:
        prev = (jnp.zeros((SUBLANES, 128), F32), jnp.zeros((1, d), F32))
    prev_dx, prev_dg = prev

    def body(dp_ref, x_ref, dx1_ref, g_ref, w_hbm, prev_dx_hbm, prev_dg_ref, dx_ref, dg_ref, w_v, dh_s, acc_g, sem):
        _load_blocks_once(w_hbm, w_v, sem)

        @pl.when(pl.program_id(0) == 0)
        def _():
            acc_g[...] = jnp.zeros_like(acc_g)
            acc_g[0:1, :] = prev_dg_ref[...]

        dh_s[...] = lax.dot_general(dp_ref[...], w_v[...], NT_DIMS, preferred_element_type=F32)

        def dnorm(rows):
            dh = dh_s[rows, :]
            xv = x_ref[rows, :]
            r = lax.rsqrt(_mean_lanes(xv * xv) + RMS_EPS)
            gd = dh * g_ref[...]
            dx_ref[rows, :] = dx1_ref[rows, :] + r * gd - xv * (r * r * r * _mean_lanes(gd * xv))
            acc_g[...] += _fold8(dh * xv * r)
        _for_chunks(tm, ROW_CHUNK, dnorm)
        _write_row_sums(acc_g, dg_ref, n_steps)

    return _call(
        body, name="dx_bwd_%d" % part, grid=(n_steps,),
        in_specs=[rows(ni), rows(d), rows(d), _const((1, d)), ANY, ANY, _const((1, d))],
        out_specs=[rows(d), _const((1, d))],
        out_shape=[jax.ShapeDtypeStruct((t, d), F32), jax.ShapeDtypeStruct((1, d), F32)],
        scratch_shapes=[pltpu.VMEM((d, ni), BF16), pltpu.VMEM((tm, d), F32),
                        pltpu.VMEM((SUBLANES, d), F32), pltpu.SemaphoreType.DMA((nb,))],
        args=(dproj, x, dx1, g1pre, w_in_g, prev_dx, prev_dg), ex=ex,
        aliases={5: 0} if part > 0 else None)


def _tn_matmul(a, g, nblk, a_cols, g_cols, a_blocked, g_blocked, tt, name, ex=None):
    t = a.shape[0]

    def body(a_ref, g_ref, o_ref):
        @pl.when(pl.program_id(1) == 0)
        def _():
            o_ref[...] = jnp.zeros_like(o_ref)
        o_ref[0] += lax.dot_general(a_ref[...], g_ref[...], TN_DIMS, preferred_element_type=F32)

    (out,), xouts = _call(
        body, name=name, grid=(nblk, t // tt),
        in_specs=[pl.BlockSpec((tt, a_cols), (lambda b, s: (s, b)) if a_blocked else (lambda b, s: (s, 0))),
                  pl.BlockSpec((tt, g_cols), (lambda b, s: (s, b)) if g_blocked else (lambda b, s: (s, 0)))],
        out_specs=[pl.BlockSpec((1, a_cols, g_cols), lambda b, s: (b, 0, 0))],
        out_shape=[jax.ShapeDtypeStruct((nblk, a_cols, g_cols), F32)],
        scratch_shapes=[], args=(a, g), ex=ex)
    return out, xouts


def _pair_sum_call(g_full, from_sibling, core, name):
    nblk, r, c = g_full.shape
    hr = r // 2
    tr = min(hr, 256)
    n = hr // tr

    def body(core_ref, g_ref, p_ref, o_ref):
        o_ref[...] = (g_ref[...] + p_ref[...]).astype(BF16)

    return pl.pallas_call(
        body, name=name,
        grid_spec=pltpu.PrefetchScalarGridSpec(
            num_scalar_prefetch=1, grid=(nblk, n),
            in_specs=[pl.BlockSpec((1, tr, c), lambda j, i, cr: (j, cr[0] * n + i, 0)),
                      pl.BlockSpec((1, tr, c), lambda j, i, cr: (j, i, 0))],
            out_specs=pl.BlockSpec((1, tr, c), lambda j, i, cr: (j, i, 0))),
        out_shape=jax.ShapeDtypeStruct((nblk, hr, c), BF16),
        compiler_params=_params("parallel", "parallel"))(core, g_full, from_sibling)


def _chip_sum_call(pair, received, chip_core, name):
    _, hr, c = pair.shape
    tr = min(hr, 256)
    n = hr // tr

    def body(cc_ref, own_ref, r_ref, o_ref):
        o_ref[...] = ((own_ref[0].astype(F32) + r_ref[0].astype(F32)) + r_ref[1].astype(F32)) + r_ref[2].astype(F32)

    return pl.pallas_call(
        body, name=name,
        grid_spec=pltpu.PrefetchScalarGridSpec(
            num_scalar_prefetch=1, grid=(n,),
            in_specs=[pl.BlockSpec((1, tr, c), lambda i, cc: (cc[0], i, 0)),
                      pl.BlockSpec((N_CHIPS - 1, tr, c), lambda i, cc: (0, i, 0))],
            out_specs=pl.BlockSpec((tr, c), lambda i, cc: (cc[1] * n + i, 0))),
        out_shape=jax.ShapeDtypeStruct((2 * hr, c), F32),
        compiler_params=_params("parallel"))(chip_core, pair, received)


def _adamw(w, g, m, v):
    m = ADAM_B1 * m + (1.0 - ADAM_B1) * g
    v = ADAM_B2 * v + (1.0 - ADAM_B2) * (g * g)
    m_hat = m / (1.0 - ADAM_B1 ** ADAM_STEP)
    v_hat = v / (1.0 - ADAM_B2 ** ADAM_STEP)
    delta = -ADAM_LR * (m_hat / (jnp.sqrt(v_hat) + ADAM_EPS) + ADAM_WD * w)
    return delta, m, v


def _adam_call(w, g, m, v, name):
    r, c = w.shape
    tr = min(r, 256)

    def body(w_ref, g_ref, m_ref, v_ref, go_ref, d_ref, mo_ref, vo_ref):
        go_ref[...] = g_ref[...]
        d_ref[...], mo_ref[...], vo_ref[...] = _adamw(w_ref[...], g_ref[...], m_ref[...], v_ref[...])

    shape = jax.ShapeDtypeStruct((r, c), F32)
    return pl.pallas_call(
        body, name=name, grid=(r // tr,), in_specs=[_rows(tr, c)] * 4, out_specs=[_rows(tr, c)] * 4,
        out_shape=[shape] * 4, compiler_params=_params("parallel"))(w, g, m, v)


def _gather_conv_weights(conv_a_w, conv_b_w, d):
    ka, dq = conv_a_w.shape
    kb = conv_b_w.shape[0]
    ra = -(-ka // SUBLANES) * SUBLANES
    rb = -(-kb // SUBLANES) * SUBLANES
    a_pad = jnp.pad(conv_a_w, ((0, ra - ka), (0, 0)))
    b_pad = jnp.pad(conv_b_w, ((0, rb - kb), (0, 0)))

    def body(a_ref, b_ref, oa_ref, ob_ref, pack, slots, send_sem, recv_sem):
        x, y, c = _place()
        me = 2 * x + y
        chips = _other_chips(x, y)
        pack[pl.ds(0, ra), :] = a_ref[...]
        pack[pl.ds(ra, rb), :] = b_ref[...]
        copies = []
        for j, (px, py, _) in enumerate(chips):
            cp = pltpu.make_async_remote_copy(
                src_ref=pack, dst_ref=slots.at[me], send_sem=send_sem.at[j], recv_sem=recv_sem.at[j],
                device_id=(px, py, c), device_id_type=MESH)
            cp.start()
            copies.append(cp)
        for j, (px, py, pk) in enumerate(chips):
            pltpu.make_async_remote_copy(
                src_ref=pack, dst_ref=slots.at[pk], send_sem=send_sem.at[j], recv_sem=recv_sem.at[j],
                device_id=(px, py, c), device_id_type=MESH).wait_recv()
        for cp in copies:
            cp.wait_send()
        slots[me] = pack[...]
        for k in range(N_CHIPS):
            oa_ref[:, pl.ds(k * dq, dq)] = slots[k, pl.ds(0, ra), :]
            ob_ref[:, pl.ds(k * dq, dq)] = slots[k, pl.ds(ra, rb), :]

    oa, ob = pl.pallas_call(
        body, name="gather_conv_weights", in_specs=[VMEM_FULL] * 2, out_specs=[VMEM_FULL] * 2,
        out_shape=[jax.ShapeDtypeStruct((ra, d), F32), jax.ShapeDtypeStruct((rb, d), F32)],
        scratch_shapes=[pltpu.VMEM((ra + rb, dq), F32), pltpu.VMEM((N_CHIPS, ra + rb, dq), F32),
                        pltpu.SemaphoreType.DMA((N_CHIPS - 1,)), pltpu.SemaphoreType.DMA((N_CHIPS - 1,))],
        compiler_params=pltpu.CompilerParams(has_side_effects=True))(a_pad, b_pad)
    return oa[:ka], ob[:kb]


def _small_step_call(partials, loss_rows, weights, m_s, v_s, sharded, d):
    n = len(partials)
    row_counts = [p.shape[0] for p in partials]
    starts = [sum(row_counts[:i]) for i in range(n)]
    loss_row = sum(row_counts)
    pack_rows = -(-(loss_row + 1) // SUBLANES) * SUBLANES
    dq = d // N_CHIPS

    def body(*refs):
        p_refs = refs[:n]
        loss_in = refs[n]
        w_refs = refs[n + 1:2 * n + 1]
        m_refs = refs[2 * n + 1:3 * n + 1]
        v_refs = refs[3 * n + 1:4 * n + 1]
        o = 4 * n + 1
        g_out = refs[o:o + n]
        d_out = refs[o + n:o + 2 * n]
        m_out = refs[o + 2 * n:o + 3 * n]
        v_out = refs[o + 3 * n:o + 4 * n]
        loss_out = refs[o + 4 * n]
        pack, from_sibling, slots, send_sem, recv_sem = refs[o + 4 * n + 1:]
        x, y, c = _place()
        me = 2 * x + y

        pack[...] = jnp.zeros_like(pack)
        for i in range(n):
            pack[pl.ds(starts[i], row_counts[i]), :] = p_refs[i][...]
        pack[pl.ds(loss_row, 1), :] = loss_in[...]

        pair = _remote(pack, from_sibling, send_sem.at[0], recv_sem.at[0], (x, y, 1 - c))
        pair.start()
        pair.wait()
        pack[...] = pack[...] + from_sibling[...]
        chips = _other_chips(x, y)
        copies = [_remote(pack, slots.at[me], send_sem.at[1 + j], recv_sem.at[1 + j], (px, py, c))
                  for j, (px, py, _) in enumerate(chips)]
        for cp in copies:
            cp.start()
        for j, (px, py, pk) in enumerate(chips):
            _remote(pack, slots.at[pk], send_sem.at[1 + j], recv_sem.at[1 + j], (px, py, c)).wait_recv()
        for cp in copies:
            cp.wait_send()

        slots[me] = pack[...]
        total = slots[0]
        for k in range(1, N_CHIPS):
            total = total + slots[k]
        pack[...] = total

        loss_out[...] = jnp.broadcast_to(
            (0.5 / d) * jnp.sum(pack[pl.ds(loss_row, 1), :], axis=-1, keepdims=True), loss_out.shape)
        chip = 2 * x + y
        for i in range(n):
            rows = pl.ds(starts[i], row_counts[i])
            if sharded[i]:
                for k in range(N_CHIPS):
                    @pl.when(chip == k)
                    def _():
                        g_out[i][...] = pack[rows, pl.ds(k * dq, dq)]
            else:
                g_out[i][...] = pack[rows, :]
            d_out[i][...], m_out[i][...], v_out[i][...] = _adamw(
                w_refs[i][...], g_out[i][...], m_refs[i][...], v_refs[i][...])

    w_shapes = [jax.ShapeDtypeStruct(w.shape, F32) for w in weights]
    n_in = 4 * n + 1
    return pl.pallas_call(
        body, name="small_grads_allreduce_adamw",
        in_specs=[VMEM_FULL] * n_in, out_specs=[VMEM_FULL] * (4 * n + 1),
        out_shape=w_shapes * 4 + [jax.ShapeDtypeStruct((SUBLANES, 128), F32)],
        scratch_shapes=[pltpu.VMEM((pack_rows, d), F32), pltpu.VMEM((pack_rows, d), F32),
                        pltpu.VMEM((N_CHIPS, pack_rows, d), F32),
                        pltpu.SemaphoreType.DMA((N_CHIPS,)), pltpu.SemaphoreType.DMA((N_CHIPS,))],
        compiler_params=pltpu.CompilerParams(has_side_effects=True, vmem_limit_bytes=VMEM_LIMIT))(
            *partials, loss_rows, *weights, *m_s, *v_s)


def _tile(t, want):
    return min(t, want)


def kernel(x, norm1_pre_g, w_in, b_in, conv_a_w, conv_a_b, w_a_out, conv_b_w, conv_b_b, ln_b_g, ln_b_b, w_b_out, w_o, norm1_post_g, norm2_pre_g, w_mlp_in, w_mlp_out, norm2_post_g, loss_target, m_norm1_pre_g, m_w_in, m_b_in, m_conv_a_w, m_conv_a_b, m_w_a_out, m_conv_b_w, m_conv_b_b, m_ln_b_g, m_ln_b_b, m_w_b_out, m_w_o, m_norm1_post_g, m_norm2_pre_g, m_w_mlp_in, m_w_mlp_out, m_norm2_post_g, v_norm1_pre_g, v_w_in, v_b_in, v_conv_a_w, v_conv_a_b, v_w_a_out, v_conv_b_w, v_conv_b_b, v_ln_b_g, v_ln_b_b, v_w_b_out, v_w_o, v_norm1_post_g, v_norm2_pre_g, v_w_mlp_in, v_w_mlp_out, v_norm2_post_g):
    _, t, d = x.shape
    xt = x.reshape(t, d)
    tgt = loss_target.reshape(t, d)
    row = lambda vec: vec.reshape(1, -1)
    cx, cy, cc = _place()
    core = cc.astype(jnp.int32).reshape(1)
    chip = (2 * cx + cy).astype(jnp.int32).reshape(1)

    big = dict(w_in=w_in, w_a_out=w_a_out, w_b_out=w_b_out, w_o=w_o, w_mlp_in=w_mlp_in, w_mlp_out=w_mlp_out)
    names = list(big)
    chip_core = jnp.concatenate([chip, core])
    slot = {k: _cast_to_slot(big[k], chip, "cast_" + k) for k in names}
    mixer_w, mlp_w = ["w_a_out", "w_b_out", "w_o"], ["w_mlp_in", "w_mlp_out"]
    rows_of = lambda buf: buf.reshape(-1, buf.shape[-1])

    def pair_sums(keys, full, from_sibling):
        return [_pair_sum_call(g, p, core, "pair_sum_" + k) for k, g, p in zip(keys, full, from_sibling)]

    def chip_sums(keys, pairs, received):
        return [_chip_sum_call(p, r, chip_core, "chip_sum_" + k) for k, p, r in zip(keys, pairs, received)]

    (w_in_g,) = _exchange_call("gather_w_in", [_ex_gather_ici([slot["w_in"]]), _ex_gather_forward([slot["w_in"]])])

    g1pre, g1post, g2pre, g2post = row(norm1_pre_g), row(norm1_post_g), row(norm2_pre_g), row(norm2_post_g)
    lng, lnb, ba, bb = row(ln_b_g), row(ln_b_b), row(conv_a_b), row(conv_b_b)
    conv_a_full, conv_b_full = _gather_conv_weights(conv_a_w, conv_b_w, d)

    (h, ua, ub, bg, cg, ha, a, sg, sa, sb), landed = _proj_call(
        xt, g1pre, w_in_g, row(b_in), _tile(t, 512), ex=_ex_gather_ici([slot[k] for k in mixer_w + mlp_w]))
    w_a_g, w_b_g, w_o_g = _exchange_call("forward_mixer_weights", [_ex_gather_forward(landed[:3])])
    w_a_full, w_b_full, w_o_full = rows_of(w_a_g), rows_of(w_b_g), rows_of(w_o_g)
    (x1, va, pa, cb, sbo, ya, yb, mg, mix), (w1_g, w2_g) = _mixer_fwd_call(
        ua, ub, bg, sa, sb, xt, conv_a_full, ba, conv_b_full, bb, lng, lnb,
        w_a_full, w_b_full, w_o_full, g1post, _tile(t, 256), ex=_ex_gather_forward(landed[3:]))
    (dx1, f, df2, h2, df1, dg2post, dg2pre, loss_rows), _ = _mlp_call(
        x1, tgt, g2pre, g2post, w1_g, rows_of(w2_g), _tile(t, 256))

    tt = _tile(t, 1024)
    n4, fq, dq = w_in.shape[1], w_mlp_in.shape[1], d // N_CHIPS
    g_mlp = [_tn_matmul(h2, df1, N_CHIPS, d, fq, False, True, tt, "dw_mlp_in")[0],
             _tn_matmul(f, df2, N_CHIPS, fq, d, True, False, tt, "dw_mlp_out")[0]]
    (dmix, dya, dyb, dva, dcb, dbg, dza, dzb, dg1post, dlng, dlnb, dba, dbb), sib_mlp = _mixer_bwd_call(
        dx1, mix, sa, sb, ya, yb, bg, va, cb, g1post, lng, lnb, w_a_full, w_b_full, w_o_full, _tile(t, 256),
        ex=_ex_sibling_halves(g_mlp))
    p_mlp = pair_sums(mlp_w, g_mlp, sib_mlp)
    g_mix = [_tn_matmul(pa, dya, 1, d, d, False, False, tt, "dw_a_out")[0].reshape(N_CHIPS, dq, d),
             _tn_matmul(sbo, dyb, 1, d, d, False, False, tt, "dw_b_out")[0].reshape(N_CHIPS, dq, d),
             _tn_matmul(mg, dmix, 1, d, d, False, False, tt, "dw_o")[0].reshape(N_CHIPS, dq, d)]
    ex_a, ex_b = _ex_scatter_to_owner(p_mlp), _ex_sibling_halves(g_mix)
    (dproj, dwa_conv, dwb_conv, dbin), xo = _conv_bwd_call(
        dva, dcb, ua, ub, cg, ha, a, sg, dbg, dza, dzb, conv_a_full, conv_b_full, _tile(t, 256),
        ex=_merge(ex_a, ex_b))
    recv_mlp, sib_mix = _split(xo, ex_a, ex_b)
    r_mlp = chip_sums(mlp_w, p_mlp, recv_mlp)
    p_mix = pair_sums(mixer_w, g_mix, sib_mix)
    ex_a, ex_b = _ex_share_halves(r_mlp), _ex_scatter_to_owner(p_mix)
    g_in, xo = _tn_matmul(h, dproj, N_CHIPS, d, n4, False, True, tt, "dw_in", ex=_merge(ex_a, ex_b))
    red_mlp, recv_mix = _split(xo, ex_a, ex_b)
    r_mix = chip_sums(mixer_w, p_mix, recv_mix)
    ex_a, ex_b = _ex_sibling_halves([g_in]), _ex_share_halves(r_mix)
    dx_first, xo = _dx_call(dproj, xt, dx1, g1pre, w_in_g, _tile(t, 512), 0, 2, None, ex=_merge(ex_a, ex_b))
    sib_in, red_mix = _split(xo, ex_a, ex_b)
    p_in = pair_sums(["w_in"], [g_in], sib_in)
    (grad_x, dg1pre), recv_in = _dx_call(dproj, xt, dx1, g1pre, w_in_g, _tile(t, 512), 1, 2, dx_first,
                                         ex=_ex_scatter_to_owner(p_in))
    r_in = chip_sums(["w_in"], p_in, recv_in)
    red_in = _exchange_call("w_in_grad_to_sibling", [_ex_share_halves(r_in)])
    reduced = dict(zip(mlp_w + mixer_w + ["w_in"], red_mlp + red_mix + red_in))

    moments = dict(w_in=(m_w_in, v_w_in), w_a_out=(m_w_a_out, v_w_a_out), w_b_out=(m_w_b_out, v_w_b_out),
                   w_o=(m_w_o, v_w_o), w_mlp_in=(m_w_mlp_in, v_w_mlp_in), w_mlp_out=(m_w_mlp_out, v_w_mlp_out))
    out = {}
    for k in names:
        out[k] = tuple(_adam_call(big[k], reduced[k], *moments[k], "adamw_" + k))

    small = [
        ("conv_b_w", dwb_conv, conv_b_w, m_conv_b_w, v_conv_b_w, True),
        ("conv_b_b", dbb, bb, row(m_conv_b_b), row(v_conv_b_b), False),
        ("b_in", dbin.reshape(7, d), b_in.reshape(7, d), m_b_in.reshape(7, d), v_b_in.reshape(7, d), False),
        ("norm1_pre_g", dg1pre, row(norm1_pre_g), row(m_norm1_pre_g), row(v_norm1_pre_g), False),
        ("conv_a_w", dwa_conv, conv_a_w, m_conv_a_w, v_conv_a_w, True),
        ("conv_a_b", dba, ba, row(m_conv_a_b), row(v_conv_a_b), False),
        ("ln_b_g", dlng, lng, row(m_ln_b_g), row(v_ln_b_g), False),
        ("ln_b_b", dlnb, lnb, row(m_ln_b_b), row(v_ln_b_b), False),
        ("norm1_post_g", dg1post, g1post, row(m_norm1_post_g), row(v_norm1_post_g), False),
        ("norm2_pre_g", dg2pre, g2pre, row(m_norm2_pre_g), row(v_norm2_pre_g), False),
        ("norm2_post_g", dg2post, g2post, row(m_norm2_post_g), row(v_norm2_post_g), False),
    ]
    res = _small_step_call([s[1] for s in small], loss_rows, [s[2] for s in small], [s[3] for s in small],
                           [s[4] for s in small], [s[5] for s in small], d)
    ns = len(small)
    loss = res[4 * ns][0, 0]
    shapes = dict(norm1_pre_g=norm1_pre_g.shape, b_in=b_in.shape, conv_a_w=conv_a_w.shape,
                  conv_a_b=conv_a_b.shape, conv_b_w=conv_b_w.shape, conv_b_b=conv_b_b.shape,
                  ln_b_g=ln_b_g.shape, ln_b_b=ln_b_b.shape, norm1_post_g=norm1_post_g.shape,
                  norm2_pre_g=norm2_pre_g.shape, norm2_post_g=norm2_post_g.shape)
    for i, s in enumerate(small):
        out[s[0]] = tuple(res[q * ns + i].reshape(shapes[s[0]]) for q in range(4))

    order = ["norm1_pre_g", "w_in", "b_in", "conv_a_w", "conv_a_b", "w_a_out", "conv_b_w", "conv_b_b",
             "ln_b_g", "ln_b_b", "w_b_out", "w_o", "norm1_post_g", "norm2_pre_g", "w_mlp_in", "w_mlp_out",
             "norm2_post_g"]
    return (loss, grad_x.reshape(x.shape), *[out[k][0] for k in order], *[out[k][1] for k in order],
            *[out[k][2] for k in order], *[out[k][3] for k in order])
```

```python
import functools

import jax
import jax.numpy as jnp
from jax import lax
from jax.experimental import pallas as pl
from jax.experimental.pallas import tpu as pltpu

RMS_EPS = 1e-6
LN_EPS = 1e-5
ADAM_LR = 0.001
ADAM_B1 = 0.9
ADAM_B2 = 0.999
ADAM_EPS = 1e-08
ADAM_WD = 0.01
ADAM_STEP = 10

F32 = jnp.float32
BF16 = jnp.bfloat16
MESH = pl.DeviceIdType.MESH
ANY = pl.BlockSpec(memory_space=pl.ANY)
VMEM_FULL = pl.BlockSpec(memory_space=pltpu.VMEM)

V7X_VMEM_BYTES = 64 * 1024 * 1024
VMEM_LIMIT = V7X_VMEM_BYTES - 8 * 1024 * 1024
SUBLANES = 8
N_CHIPS = 4
N_DEV = 8
HALO_A = 8
HALO_B = 16
CONV_ROWS = 16
ROW_CHUNK = 32

NT_DIMS = (((1,), (1,)), ((), ()))
TN_DIMS = (((0,), (0,)), ((), ()))


def _params(*sem):
    return pltpu.CompilerParams(dimension_semantics=sem, vmem_limit_bytes=VMEM_LIMIT)


def _rows(tm, d):
    return pl.BlockSpec((tm, d), lambda i: (i, 0))


def _const(shape):
    return pl.BlockSpec(shape, lambda i: (0,) * len(shape))


def _halo_prev(tm, hb, d):
    return pl.BlockSpec((hb, d), lambda i: (jnp.maximum(i * (tm // hb) - 1, 0), 0))


def _halo_next(tm, hb, d, t):
    return pl.BlockSpec((hb, d), lambda i: (jnp.minimum((i + 1) * (tm // hb), t // hb - 1), 0))


def _for_chunks(n_rows, rc, fn):
    for r0 in range(0, n_rows, rc):
        fn(pl.ds(r0, rc))


def _fold8(v):
    return v.reshape(v.shape[0] // SUBLANES, SUBLANES, v.shape[1]).sum(axis=0)


def _mean_lanes(v):
    return jnp.mean(v, axis=-1, keepdims=True)


def _load_blocks_once(w_hbm, w_vmem, sem):
    nb, _, n = w_hbm.shape

    @pl.when(pl.program_id(0) == 0)
    def _():
        copies = [pltpu.make_async_copy(w_hbm.at[j], w_vmem.at[:, pl.ds(j * n, n)], sem.at[j])
                  for j in range(nb)]
        for cp in copies:
            cp.start()
        for cp in copies:
            cp.wait()


def _load_once(w_hbm, w_vmem, sem):
    @pl.when(pl.program_id(0) == 0)
    def _():
        cp = pltpu.make_async_copy(w_hbm, w_vmem, sem)
        cp.start()
        cp.wait()


def _write_row_sums(acc_ref, out_ref, n_steps):
    @pl.when(pl.program_id(0) == n_steps - 1)
    def _():
        out_ref[...] = jnp.sum(acc_ref[...], axis=0, keepdims=True)


def _place():
    return lax.axis_index("x"), lax.axis_index("y"), lax.axis_index("c")


def _other_chips(x, y):
    rel = [(x, 1 - y), (1 - x, y), (1 - x, 1 - y)]
    return [(px, py, 2 * px + py) for px, py in rel]


class _Exchange:
    def __init__(self, inputs, out_shapes, aliases, n_sems, copies):
        self.inputs = list(inputs)
        self.out_shapes = list(out_shapes)
        self.aliases = dict(aliases)
        self.n_sems = n_sems
        self.copies = copies


def _remote(src, dst, send, recv, device):
    return pltpu.make_async_remote_copy(src_ref=src, dst_ref=dst, send_sem=send, recv_sem=recv,
                                        device_id=device, device_id_type=MESH)


def _sds(a):
    return jax.ShapeDtypeStruct(a.shape, a.dtype)


def _ex_gather_ici(bufs):
    n = len(bufs)

    def copies(xin, xout, send, recv):
        x, y, c = _place()
        me = 2 * x + y
        out = []
        for a in range(n):
            hr = xin[a].shape[1] // 2
            rows = pl.ds(c * hr, hr)
            for j, (px, py, _) in enumerate(_other_chips(x, y)):
                k = a * (N_CHIPS - 1) + j
                out.append(_remote(xin[a].at[me, rows, :], xout[a].at[me, rows, :], send(k), recv(k), (px, py, c)))
        return out

    return _Exchange(bufs, [_sds(b) for b in bufs], {a: a for a in range(n)}, n * (N_CHIPS - 1), copies)


def _ex_gather_forward(bufs):
    n = len(bufs)

    def copies(xin, xout, send, recv):
        x, y, c = _place()
        out = []
        for a in range(n):
            hr = xin[a].shape[1] // 2
            rows = pl.ds(c * hr, hr)
            for j, (_, _, pk) in enumerate(_other_chips(x, y)):
                k = a * (N_CHIPS - 1) + j
                out.append(_remote(xin[a].at[pk, rows, :], xout[a].at[pk, rows, :], send(k), recv(k), (x, y, 1 - c)))
        return out

    return _Exchange(bufs, [_sds(b) for b in bufs], {a: a for a in range(n)}, n * (N_CHIPS - 1), copies)


def _ex_sibling_halves(grads):
    n = len(grads)

    def copies(xin, xout, send, recv):
        x, y, c = _place()
        out = []
        for a in range(n):
            hr = xin[a].shape[1] // 2
            out.append(_remote(xin[a].at[:, pl.ds((1 - c) * hr, hr), :], xout[a], send(a), recv(a), (x, y, 1 - c)))
        return out

    shapes = [jax.ShapeDtypeStruct((g.shape[0], g.shape[1] // 2, g.shape[2]), g.dtype) for g in grads]
    return _Exchange(grads, shapes, {}, n, copies)


def _ex_scatter_to_owner(pairs):
    n = len(pairs)

    def copies(xin, xout, send, recv):
        x, y, c = _place()
        out = []
        for a in range(n):
            for j, (px, py, pk) in enumerate(_other_chips(x, y)):
                k = a * (N_CHIPS - 1) + j
                out.append(_remote(xin[a].at[pk], xout[a].at[j], send(k), recv(k), (px, py, c)))
        return out

    shapes = [jax.ShapeDtypeStruct((N_CHIPS - 1,) + p.shape[1:], p.dtype) for p in pairs]
    return _Exchange(pairs, shapes, {}, n * (N_CHIPS - 1), copies)


def _ex_share_halves(reduced):
    n = len(reduced)

    def copies(xin, xout, send, recv):
        x, y, c = _place()
        out = []
        for a in range(n):
            hr = xin[a].shape[0] // 2
            rows = pl.ds(c * hr, hr)
            out.append(_remote(xin[a].at[rows, :], xout[a].at[rows, :], send(a), recv(a), (x, y, 1 - c)))
        return out

    return _Exchange(reduced, [_sds(r) for r in reduced], {a: a for a in range(n)}, n, copies)


def _merge(*exs):
    exs = [e for e in exs if e is not None]
    if not exs:
        return None
    inputs, shapes, aliases = [], [], {}
    in_off, out_off, sem_off = [], [], []
    n_sems = 0
    for e in exs:
        in_off.append(len(inputs))
        out_off.append(len(shapes))
        sem_off.append(n_sems)
        aliases.update({len(inputs) + i: len(shapes) + o for i, o in e.aliases.items()})
        inputs += e.inputs
        shapes += e.out_shapes
        n_sems += e.n_sems

    def copies(xin, xout, send, recv):
        out = []
        for e, io, oo, so in zip(exs, in_off, out_off, sem_off):
            out += e.copies(xin[io:io + len(e.inputs)], xout[oo:oo + len(e.out_shapes)],
                            lambda i, so=so: send(so + i), lambda i, so=so: recv(so + i))
        return out

    return _Exchange(inputs, shapes, aliases, n_sems, copies)


def _split(ex_outs, *exs):
    parts, o = [], 0
    for e in exs:
        parts.append(list(ex_outs[o:o + len(e.out_shapes)]))
        o += len(e.out_shapes)
    return parts


def _call(body, *, name, grid, in_specs, out_specs, out_shape, scratch_shapes, args, ex=None, aliases=None):
    n_in, n_out, n_scr = len(in_specs), len(out_specs), len(scratch_shapes)
    seq = ("arbitrary",) * len(grid)
    aliases = dict(aliases or {})
    if ex is None:
        outs = pl.pallas_call(
            body, name=name, grid=grid, in_specs=list(in_specs), out_specs=list(out_specs),
            out_shape=list(out_shape), scratch_shapes=list(scratch_shapes), input_output_aliases=aliases,
            compiler_params=_params(*seq))(*args)
        return list(outs), []
    n_xi, n_xo = len(ex.inputs), len(ex.out_shapes)

    def full(*refs):
        ins, xin = refs[:n_in], refs[n_in:n_in + n_xi]
        o = n_in + n_xi
        outs, xout = refs[o:o + n_out], refs[o + n_out:o + n_out + n_xo]
        s = o + n_out + n_xo
        scr = refs[s:s + n_scr]
        send_sems, recv_sems = refs[s + n_scr:]
        send = lambda i: send_sems.at[i]
        recv = lambda i: recv_sems.at[i]
        first = functools.reduce(jnp.logical_and, [pl.program_id(a) == 0 for a in range(len(grid))])
        last = functools.reduce(jnp.logical_and, [pl.program_id(a) == grid[a] - 1 for a in range(len(grid))])

        @pl.when(first)
        def _():
            for cp in ex.copies(xin, xout, send, recv):
                cp.start()

        body(*ins, *outs, *scr)

        @pl.when(last)
        def _():
            for cp in ex.copies(xin, xout, send, recv):
                cp.wait()

    res = pl.pallas_call(
        full, name=name, grid=grid, in_specs=list(in_specs) + [ANY] * n_xi,
        out_specs=list(out_specs) + [ANY] * n_xo, out_shape=list(out_shape) + ex.out_shapes,
        scratch_shapes=list(scratch_shapes) + [pltpu.SemaphoreType.DMA((ex.n_sems,)),
                                               pltpu.SemaphoreType.DMA((ex.n_sems,))],
        input_output_aliases={**aliases, **{n_in + i: n_out + o for i, o in ex.aliases.items()}},
        compiler_params=pltpu.CompilerParams(dimension_semantics=seq, vmem_limit_bytes=VMEM_LIMIT,
                                             has_side_effects=True))(*args, *ex.inputs)
    return list(res[:n_out]), list(res[n_out:])


def _exchange_call(name, phases):
    first = phases[0]
    n_xi, n_xo = len(first.inputs), len(first.out_shapes)

    def body(*refs):
        xin, xout = refs[:n_xi], refs[n_xi:n_xi + n_xo]
        sems = refs[n_xi + n_xo:]
        for p, ex in enumerate(phases):
            send_sems, recv_sems = sems[2 * p], sems[2 * p + 1]
            cps = ex.copies(xin, xout, lambda i: send_sems.at[i], lambda i: recv_sems.at[i])
            for cp in cps:
                cp.start()
            for cp in cps:
                cp.wait()

    sems = []
    for ex in phases:
        sems += [pltpu.SemaphoreType.DMA((ex.n_sems,)), pltpu.SemaphoreType.DMA((ex.n_sems,))]
    return list(pl.pallas_call(
        body, name=name, in_specs=[ANY] * n_xi, out_specs=[ANY] * n_xo, out_shape=first.out_shapes,
        scratch_shapes=sems, input_output_aliases=dict(first.aliases),
        compiler_params=pltpu.CompilerParams(has_side_effects=True))(*first.inputs))


def _cast_to_slot(w, chip, name):
    r, c = w.shape
    tr = min(r, 256)

    def body(chip_ref, w_ref, o_ref):
        o_ref[0] = w_ref[...].astype(BF16)

    return pl.pallas_call(
        body, name=name,
        grid_spec=pltpu.PrefetchScalarGridSpec(
            num_scalar_prefetch=1, grid=(r // tr,),
            in_specs=[pl.BlockSpec((tr, c), lambda i, k: (i, 0))],
            out_specs=pl.BlockSpec((1, tr, c), lambda i, k: (k[0], i, 0))),
        out_shape=jax.ShapeDtypeStruct((N_CHIPS, r, c), BF16),
        compiler_params=_params("parallel"))(chip, w)


def _proj_call(x, g1pre, w_in_g, b_in, tm, ex=None):
    t, d = x.shape
    nb, _, n4 = w_in_g.shape
    ni = nb * n4
    assert ni == 7 * d

    def body(x_ref, g_ref, b_ref, w_hbm, h_ref, ua_ref, ub_ref, bg_ref, cg_ref, ha_ref, a_ref,
             sg_ref, sa_ref, sb_ref, w_v, p0, p1, sem):
        _load_blocks_once(w_hbm, w_v, sem)

        def norm(rows):
            xv = x_ref[rows, :]
            r = lax.rsqrt(_mean_lanes(xv * xv) + RMS_EPS)
            h_ref[rows, :] = (xv * r * g_ref[...]).astype(BF16)
        _for_chunks(tm, ROW_CHUNK, norm)

        def group(i, dst):
            cols = pl.ds(i * d, d)
            dst[...] = jnp.dot(h_ref[...], w_v[:, cols], preferred_element_type=F32) + b_ref[:, cols]

        group(0, p0)

        def bgate(rows):
            bg_ref[rows, :] = p0[rows, :].astype(BF16)
        _for_chunks(tm, ROW_CHUNK, bgate)

        group(1, p0)
        group(2, p1)

        def branch_a(rows):
            cg, ha = p0[rows, :], p1[rows, :]
            ua_ref[rows, :] = cg * ha
            cg_ref[rows, :] = cg.astype(BF16)
            ha_ref[rows, :] = ha.astype(BF16)
        _for_chunks(tm, ROW_CHUNK, branch_a)

        group(3, p0)
        group(4, p1)

        def branch_b(rows):
            a, sg = p0[rows, :], jax.nn.sigmoid(p1[rows, :])
            ub_ref[rows, :] = a * sg
            a_ref[rows, :] = a.astype(BF16)
            sg_ref[rows, :] = sg.astype(BF16)
        _for_chunks(tm, ROW_CHUNK, branch_b)

        group(5, p0)
        group(6, p1)

        def gates(rows):
            sa_ref[rows, :] = jax.nn.sigmoid(p0[rows, :]).astype(BF16)
            sb_ref[rows, :] = jax.nn.sigmoid(p1[rows, :]).astype(BF16)
        _for_chunks(tm, ROW_CHUNK, gates)

    bf = jax.ShapeDtypeStruct((t, d), BF16)
    f32 = jax.ShapeDtypeStruct((t, d), F32)
    return _call(
        body, name="proj_fwd", grid=(t // tm,),
        in_specs=[_rows(tm, d), _const((1, d)), _const((1, ni)), ANY],
        out_specs=[_rows(tm, d)] * 10,
        out_shape=[bf, f32, f32, bf, bf, bf, bf, bf, bf, bf],
        scratch_shapes=[pltpu.VMEM((d, ni), BF16), pltpu.VMEM((tm, d), F32), pltpu.VMEM((tm, d), F32),
                        pltpu.SemaphoreType.DMA((nb,))],
        args=(x, g1pre, b_in, w_in_g), ex=ex)


def _fill_ext(ext, prev_ref, cur_ref, next_ref, hb, tm, i, n_steps):
    ext[pl.ds(0, hb), :] = jnp.where(i > 0, prev_ref[...], 0.0)
    ext[pl.ds(hb, tm), :] = cur_ref[...]
    ext[pl.ds(hb + tm, hb), :] = jnp.where(i < n_steps - 1, next_ref[...], 0.0)


def _shift_plan(offsets):
    shifts = sorted({o % SUBLANES for o in offsets if o % SUBLANES})
    return {s: i for i, s in enumerate(shifts)}


def _shifted_rows(tm, offsets):
    return tm + SUBLANES * max(o // SUBLANES for o in offsets)


def _fill_shifted(ext, sh, plan):
    n = sh.shape[1]
    for s, i in plan.items():
        sh[i, :, :] = ext[pl.ds(s, n), :]


def _fill_tap_rows(w_ref, rows8):
    @pl.when(pl.program_id(0) == 0)
    def _():
        for k in range(w_ref.shape[0]):
            rows8[pl.ds(k * SUBLANES, SUBLANES), :] = jnp.broadcast_to(w_ref[k:k + 1, :], (SUBLANES, w_ref.shape[1]))


def _tap(rows8, k):
    w8 = rows8[pl.ds(k * SUBLANES, SUBLANES), :]
    return jnp.concatenate([w8] * (CONV_ROWS // SUBLANES), axis=0)


def _window(ext, sh, plan, offset, r0):
    q, s = divmod(offset, SUBLANES)
    if s == 0:
        return ext[pl.ds(offset + r0, CONV_ROWS), :]
    return sh[plan[s], pl.ds(SUBLANES * q + r0, CONV_ROWS), :]


def _mixer_fwd_call(ua, ub, bg, sa, sb, x, conv_a_w, conv_a_b, conv_b_w, conv_b_b, ln_g, ln_b,
                    w_a, w_b, w_o, g1post, tm, ex=None):
    t, d = x.shape
    n_steps = t // tm
    ka, kb = conv_a_w.shape[0], conv_b_w.shape[0]
    off_a = [HALO_A - (ka - 1) // 2 + k for k in range(ka)]
    off_b = [HALO_B - (kb - 1) // 2 + k for k in range(kb)]
    plan_a, plan_b = _shift_plan(off_a), _shift_plan(off_b)

    def body(uap, uac, uan, ubp, ubc, ubn, bg_ref, sa_ref, sb_ref, x_ref, wa_c, ba_c, wb_c, bb_c,
             lng, lnb, wa_hbm, wb_hbm, wo_hbm, g_ref,
             x1_ref, va_ref, pa_ref, cb_ref, sbo_ref, ya_ref, yb_ref, mg_ref, mix_ref,
             ext_a, ext_b, sh_a, sh_b, wa8, wb8, wa_v, wb_v, wo_v, y0, y1, sem):
        i = pl.program_id(0)
        _fill_tap_rows(wa_c, wa8)
        _fill_tap_rows(wb_c, wb8)
        _load_once(wa_hbm, wa_v, sem.at[0])
        _load_once(wb_hbm, wb_v, sem.at[1])
        _load_once(wo_hbm, wo_v, sem.at[2])
        _fill_ext(ext_a, uap, uac, uan, HALO_A, tm, i, n_steps)
        _fill_ext(ext_b, ubp, ubc, ubn, HALO_B, tm, i, n_steps)
        _fill_shifted(ext_a, sh_a, plan_a)
        _fill_shifted(ext_b, sh_b, plan_b)

        for r0 in range(0, tm, CONV_ROWS):
            rows = pl.ds(r0, CONV_ROWS)
            va = jnp.broadcast_to(ba_c[...], (CONV_ROWS, d))
            for k in range(ka):
                va = va + _tap(wa8, k) * _window(ext_a, sh_a, plan_a, off_a[k], r0)
            va_ref[rows, :] = va.astype(BF16)
            pa_ref[rows, :] = (bg_ref[rows, :].astype(F32) * va).astype(BF16)
            cb = jnp.broadcast_to(bb_c[...], (CONV_ROWS, d))
            for k in range(kb):
                cb = cb + _tap(wb8, k) * _window(ext_b, sh_b, plan_b, off_b[k], r0)
            cb_ref[rows, :] = cb
            mu = _mean_lanes(cb)
            cen = cb - mu
            rstd = lax.rsqrt(_mean_lanes(cen * cen) + LN_EPS)
            ln = cen * rstd * lng[...] + lnb[...]
            sbo_ref[rows, :] = (ln * jax.nn.sigmoid(ln)).astype(BF16)

        y0[...] = jnp.dot(pa_ref[...], wa_v[...], preferred_element_type=F32)
        y1[...] = jnp.dot(sbo_ref[...], wb_v[...], preferred_element_type=F32)

        def merge(rows):
            ya, yb = y0[rows, :], y1[rows, :]
            ya_ref[rows, :] = ya.astype(BF16)
            yb_ref[rows, :] = yb.astype(BF16)
            mg_ref[rows, :] = (sa_ref[rows, :].astype(F32) * ya + sb_ref[rows, :].astype(F32) * yb).astype(BF16)
        _for_chunks(tm, ROW_CHUNK, merge)

        mix_ref[...] = jnp.dot(mg_ref[...], wo_v[...], preferred_element_type=F32)

        def resid(rows):
            mix = mix_ref[rows, :]
            r = lax.rsqrt(_mean_lanes(mix * mix) + RMS_EPS)
            x1_ref[rows, :] = x_ref[rows, :] + mix * r * g_ref[...]
        _for_chunks(tm, ROW_CHUNK, resid)

    bf = jax.ShapeDtypeStruct((t, d), BF16)
    f32 = jax.ShapeDtypeStruct((t, d), F32)
    return _call(
        body, name="mixer_fwd", grid=(n_steps,),
        in_specs=[_halo_prev(tm, HALO_A, d), _rows(tm, d), _halo_next(tm, HALO_A, d, t),
                  _halo_prev(tm, HALO_B, d), _rows(tm, d), _halo_next(tm, HALO_B, d, t),
                  _rows(tm, d), _rows(tm, d), _rows(tm, d), _rows(tm, d),
                  _const((ka, d)), _const((1, d)), _const((kb, d)), _const((1, d)),
                  _const((1, d)), _const((1, d)), ANY, ANY, ANY, _const((1, d))],
        out_specs=[_rows(tm, d)] * 9,
        out_shape=[f32, bf, bf, f32, bf, bf, bf, bf, f32],
        scratch_shapes=[pltpu.VMEM((tm + 2 * HALO_A, d), F32), pltpu.VMEM((tm + 2 * HALO_B, d), F32),
                        pltpu.VMEM((len(plan_a), _shifted_rows(tm, off_a), d), F32),
                        pltpu.VMEM((len(plan_b), _shifted_rows(tm, off_b), d), F32),
                        pltpu.VMEM((ka * SUBLANES, d), F32), pltpu.VMEM((kb * SUBLANES, d), F32),
                        pltpu.VMEM((d, d), BF16), pltpu.VMEM((d, d), BF16), pltpu.VMEM((d, d), BF16),
                        pltpu.VMEM((tm, d), F32), pltpu.VMEM((tm, d), F32),
                        pltpu.SemaphoreType.DMA((3,))],
        args=(ua, ua, ua, ub, ub, ub, bg, sa, sb, x, conv_a_w, conv_a_b, conv_b_w, conv_b_b,
              ln_g, ln_b, w_a, w_b, w_o, g1post), ex=ex)


def _mlp_call(x1, target, mix, g2pre, g2post, g1post, w1_g, w2, tm, ex=None):
    t, d = x1.shape
    nb, _, fq = w1_g.shape
    f = nb * fq
    n_steps = t // tm
    inv_d = 1.0 / d

    def body(x1_ref, t_ref, mix_ref, gpre, gpost, gmix, w1_hbm, w2_hbm,
             dx1_ref, f_ref, df2_ref, h2_ref, df1_ref, dmix_ref, dgpost_ref, dgpre_ref, dgmix_ref, loss_ref,
             w1_v, w2_v, f1_s, blk_s, f2_s, acc_post, acc_pre, acc_mix, acc_loss, sem):
        _load_blocks_once(w1_hbm, w1_v, sem)
        _load_once(w2_hbm, w2_v, sem.at[nb])

        @pl.when(pl.program_id(0) == 0)
        def _():
            acc_post[...] = jnp.zeros_like(acc_post)
            acc_pre[...] = jnp.zeros_like(acc_pre)
            acc_mix[...] = jnp.zeros_like(acc_mix)
            acc_loss[...] = jnp.zeros_like(acc_loss)

        def norm(rows):
            xv = x1_ref[rows, :]
            r = lax.rsqrt(_mean_lanes(xv * xv) + RMS_EPS)
            h2_ref[rows, :] = (xv * r * gpre[...]).astype(BF16)
        _for_chunks(tm, ROW_CHUNK, norm)

        for j in range(nb):
            cols = pl.ds(j * fq, fq)
            f1_s[:, cols] = jnp.dot(h2_ref[...], w1_v[:, cols], preferred_element_type=F32)

        def act(rows):
            relu = jnp.maximum(f1_s[rows, :], 0.0)
            f_ref[rows, :] = (relu * relu).astype(BF16)
        _for_chunks(tm, ROW_CHUNK, act)

        f2_s[...] = jnp.dot(f_ref[...], w2_v[...], preferred_element_type=F32)

        def head(rows):
            f2 = f2_s[rows, :]
            rf = lax.rsqrt(_mean_lanes(f2 * f2) + RMS_EPS)
            y = x1_ref[rows, :] + f2 * rf * gpost[...]
            err = y - t_ref[rows, :]
            acc_loss[...] += _fold8(err * err)
            dy = err * inv_d
            gdy = dy * gpost[...]
            df2 = rf * gdy - f2 * (rf * rf * rf * _mean_lanes(gdy * f2))
            df2_ref[rows, :] = df2.astype(BF16)
            acc_post[...] += _fold8(dy * f2 * rf)
            dx1_ref[rows, :] = dy
        _for_chunks(tm, ROW_CHUNK, head)

        for j in range(nb):
            cols = pl.ds(j * fq, fq)
            blk_s[...] = lax.dot_general(df2_ref[...], w2_v[cols, :], NT_DIMS, preferred_element_type=F32)

            def dact(rows):
                relu = jnp.maximum(f1_s[rows, cols], 0.0)
                df1_ref[rows, cols] = (blk_s[rows, :] * (2.0 * relu)).astype(BF16)
            _for_chunks(tm, ROW_CHUNK, dact)

        f2_s[...] = lax.dot_general(df1_ref[...], w1_v[...], NT_DIMS, preferred_element_type=F32)

        def dnorm(rows):
            dh2 = f2_s[rows, :]
            xv = x1_ref[rows, :]
            r = lax.rsqrt(_mean_lanes(xv * xv) + RMS_EPS)
            gd = dh2 * gpre[...]
            dxv = dx1_ref[rows, :] + r * gd - xv * (r * r * r * _mean_lanes(gd * xv))
            dx1_ref[rows, :] = dxv
            acc_pre[...] += _fold8(dh2 * xv * r)
            mix = mix_ref[rows, :]
            rm = lax.rsqrt(_mean_lanes(mix * mix) + RMS_EPS)
            gm = dxv * gmix[...]
            dmix_ref[rows, :] = (rm * gm - mix * (rm * rm * rm * _mean_lanes(gm * mix))).astype(BF16)
            acc_mix[...] += _fold8(dxv * mix * rm)
        _for_chunks(tm, ROW_CHUNK, dnorm)

        _write_row_sums(acc_post, dgpost_ref, n_steps)
        _write_row_sums(acc_pre, dgpre_ref, n_steps)
        _write_row_sums(acc_mix, dgmix_ref, n_steps)
        _write_row_sums(acc_loss, loss_ref, n_steps)

    row = jax.ShapeDtypeStruct((1, d), F32)
    return _call(
        body, name="mlp_fwd_bwd", grid=(n_steps,),
        in_specs=[_rows(tm, d), _rows(tm, d), _rows(tm, d), _const((1, d)), _const((1, d)), _const((1, d)), ANY, ANY],
        out_specs=[_rows(tm, d), _rows(tm, f), _rows(tm, d), _rows(tm, d), _rows(tm, f), _rows(tm, d),
                   _const((1, d)), _const((1, d)), _const((1, d)), _const((1, d))],
        out_shape=[jax.ShapeDtypeStruct((t, d), F32), jax.ShapeDtypeStruct((t, f), BF16),
                   jax.ShapeDtypeStruct((t, d), BF16), jax.ShapeDtypeStruct((t, d), BF16),
                   jax.ShapeDtypeStruct((t, f), BF16), jax.ShapeDtypeStruct((t, d), BF16), row, row, row, row],
        scratch_shapes=[pltpu.VMEM((d, f), BF16), pltpu.VMEM((f, d), BF16),
                        pltpu.VMEM((tm, f), F32), pltpu.VMEM((tm, fq), F32), pltpu.VMEM((tm, d), F32),
                        pltpu.VMEM((SUBLANES, d), F32), pltpu.VMEM((SUBLANES, d), F32),
                        pltpu.VMEM((SUBLANES, d), F32), pltpu.VMEM((SUBLANES, d), F32),
                        pltpu.SemaphoreType.DMA((nb + 1,))],
        args=(x1, target, mix, g2pre, g2post, g1post, w1_g, w2), ex=ex)


def _mixer_bwd_call(dmix, sa, sb, ya, yb, bg, va, cb, ln_g, ln_b, w_a, w_b, w_o, tm, ex=None):
    t, d = dmix.shape
    n_steps = t // tm

    def body(dmix_ref, sa_ref, sb_ref, ya_ref, yb_ref, bg_ref, va_ref, cb_ref, lng, lnb,
             wa_hbm, wb_hbm, wo_hbm,
             dya_ref, dyb_ref, dva_ref, dcb_ref, dbg_ref, dza_ref, dzb_ref,
             dlng_ref, dlnb_ref, dba_ref, dbb_ref,
             wa_v, wb_v, wo_v, s0, s1, acc_lng, acc_lnb, acc_ba, acc_bb, sem):
        _load_once(wa_hbm, wa_v, sem.at[0])
        _load_once(wb_hbm, wb_v, sem.at[1])
        _load_once(wo_hbm, wo_v, sem.at[2])
        accs = (acc_lng, acc_lnb, acc_ba, acc_bb)

        @pl.when(pl.program_id(0) == 0)
        def _():
            for acc in accs:
                acc[...] = jnp.zeros_like(acc)

        s0[...] = lax.dot_general(dmix_ref[...], wo_v[...], NT_DIMS, preferred_element_type=F32)

        def dmerge(rows):
            dm = s0[rows, :]
            sav, sbv = sa_ref[rows, :].astype(F32), sb_ref[rows, :].astype(F32)
            dya_ref[rows, :] = (dm * sav).astype(BF16)
            dyb_ref[rows, :] = (dm * sbv).astype(BF16)
            dza_ref[rows, :] = (dm * ya_ref[rows, :].astype(F32) * sav * (1.0 - sav)).astype(BF16)
            dzb_ref[rows, :] = (dm * yb_ref[rows, :].astype(F32) * sbv * (1.0 - sbv)).astype(BF16)
        _for_chunks(tm, ROW_CHUNK, dmerge)

        s0[...] = lax.dot_general(dya_ref[...], wa_v[...], NT_DIMS, preferred_element_type=F32)
        s1[...] = lax.dot_general(dyb_ref[...], wb_v[...], NT_DIMS, preferred_element_type=F32)

        def dbranches(rows):
            dpa = s0[rows, :]
            dbg_ref[rows, :] = (dpa * va_ref[rows, :].astype(F32)).astype(BF16)
            dva = dpa * bg_ref[rows, :].astype(F32)
            dva_ref[rows, :] = dva
            acc_ba[...] += _fold8(dva)
            cbv = cb_ref[rows, :]
            mu = _mean_lanes(cbv)
            cen = cbv - mu
            rstd = lax.rsqrt(_mean_lanes(cen * cen) + LN_EPS)
            xhat = cen * rstd
            ln = xhat * lng[...] + lnb[...]
            sig = jax.nn.sigmoid(ln)
            dln = s1[rows, :] * (sig * (1.0 + ln * (1.0 - sig)))
            acc_lng[...] += _fold8(dln * xhat)
            acc_lnb[...] += _fold8(dln)
            dxh = dln * lng[...]
            dcb = rstd * (dxh - _mean_lanes(dxh) - xhat * _mean_lanes(dxh * xhat))
            dcb_ref[rows, :] = dcb
            acc_bb[...] += _fold8(dcb)
        _for_chunks(tm, ROW_CHUNK, dbranches)

        _write_row_sums(acc_lng, dlng_ref, n_steps)
        _write_row_sums(acc_lnb, dlnb_ref, n_steps)
        _write_row_sums(acc_ba, dba_ref, n_steps)
        _write_row_sums(acc_bb, dbb_ref, n_steps)

    bf = jax.ShapeDtypeStruct((t, d), BF16)
    f32 = jax.ShapeDtypeStruct((t, d), F32)
    row = jax.ShapeDtypeStruct((1, d), F32)
    return _call(
        body, name="mixer_bwd", grid=(n_steps,),
        in_specs=[_rows(tm, d)] * 8 + [_const((1, d))] * 2 + [ANY, ANY, ANY],
        out_specs=[_rows(tm, d)] * 7 + [_const((1, d))] * 4,
        out_shape=[bf, bf, f32, f32, bf, bf, bf, row, row, row, row],
        scratch_shapes=[pltpu.VMEM((d, d), BF16), pltpu.VMEM((d, d), BF16), pltpu.VMEM((d, d), BF16),
                        pltpu.VMEM((tm, d), F32), pltpu.VMEM((tm, d), F32)]
        + [pltpu.VMEM((SUBLANES, d), F32)] * 4 + [pltpu.SemaphoreType.DMA((3,))],
        args=(dmix, sa, sb, ya, yb, bg, va, cb, ln_g, ln_b, w_a, w_b, w_o), ex=ex)


def _conv_bwd_call(dva, dcb, ua, ub, cg, ha, a, sg, dbg, dza, dzb, conv_a_w, conv_b_w, tm, ex=None):
    t, d = dva.shape
    n_steps = t // tm
    ka, kb = conv_a_w.shape[0], conv_b_w.shape[0]
    off_a = [HALO_A + (ka - 1) // 2 - k for k in range(ka)]
    off_b = [HALO_B + (kb - 1) // 2 - k for k in range(kb)]
    plan_a, plan_b = _shift_plan(off_a), _shift_plan(off_b)

    def body(dvap, dvac, dvan, dcbp, dcbc, dcbn, ua_ref, ub_ref,
             cg_ref, ha_ref, a_ref, sg_ref, dbg_ref, dza_ref, dzb_ref, wa_c, wb_c,
             dproj_ref, dwa_ref, dwb_ref, dbin_ref,
             e_dva, e_dcb, sh_a, sh_b, wa8, wb8, acc_wa, acc_wb, acc_bin):
        i = pl.program_id(0)
        _fill_tap_rows(wa_c, wa8)
        _fill_tap_rows(wb_c, wb8)

        @pl.when(i == 0)
        def _():
            acc_wa[...] = jnp.zeros_like(acc_wa)
            acc_wb[...] = jnp.zeros_like(acc_wb)
            acc_bin[...] = jnp.zeros_like(acc_bin)

        _fill_ext(e_dva, dvap, dvac, dvan, HALO_A, tm, i, n_steps)
        _fill_ext(e_dcb, dcbp, dcbc, dcbn, HALO_B, tm, i, n_steps)
        _fill_shifted(e_dva, sh_a, plan_a)
        _fill_shifted(e_dcb, sh_b, plan_b)

        def put(col, rows, val_f32):
            dproj_ref[rows, pl.ds(col * d, d)] = val_f32.astype(BF16)
            acc_bin[:, pl.ds(col * d, d)] += _fold8(val_f32)

        for r0 in range(0, tm, CONV_ROWS):
            rows = pl.ds(r0, CONV_ROWS)
            ua_c, ub_c = ua_ref[rows, :], ub_ref[rows, :]
            dua = jnp.zeros((CONV_ROWS, d), F32)
            for k in range(ka):
                xk = _window(e_dva, sh_a, plan_a, off_a[k], r0)
                dua = dua + _tap(wa8, k) * xk
                acc_wa[pl.ds(k * SUBLANES, SUBLANES), :] += _fold8(ua_c * xk)
            dub = jnp.zeros((CONV_ROWS, d), F32)
            for k in range(kb):
                xk = _window(e_dcb, sh_b, plan_b, off_b[k], r0)
                dub = dub + _tap(wb8, k) * xk
                acc_wb[pl.ds(k * SUBLANES, SUBLANES), :] += _fold8(ub_c * xk)
            cgv, hav = cg_ref[rows, :].astype(F32), ha_ref[rows, :].astype(F32)
            av, sgv = a_ref[rows, :].astype(F32), sg_ref[rows, :].astype(F32)
            put(0, rows, dbg_ref[rows, :].astype(F32))
            put(1, rows, dua * hav)
            put(2, rows, dua * cgv)
            put(3, rows, dub * sgv)
            put(4, rows, dub * av * sgv * (1.0 - sgv))
            put(5, rows, dza_ref[rows, :].astype(F32))
            put(6, rows, dzb_ref[rows, :].astype(F32))

        @pl.when(i == n_steps - 1)
        def _():
            for k in range(ka):
                dwa_ref[k:k + 1, :] = jnp.sum(acc_wa[pl.ds(k * SUBLANES, SUBLANES), :], axis=0, keepdims=True)
            for k in range(kb):
                dwb_ref[k:k + 1, :] = jnp.sum(acc_wb[pl.ds(k * SUBLANES, SUBLANES), :], axis=0, keepdims=True)
            dbin_ref[...] = jnp.sum(acc_bin[...], axis=0, keepdims=True)

    halo_a = [_halo_prev(tm, HALO_A, d), _rows(tm, d), _halo_next(tm, HALO_A, d, t)]
    halo_b = [_halo_prev(tm, HALO_B, d), _rows(tm, d), _halo_next(tm, HALO_B, d, t)]
    return _call(
        body, name="conv_bwd", grid=(n_steps,),
        in_specs=halo_a + halo_b + [_rows(tm, d)] * 9 + [_const((ka, d)), _const((kb, d))],
        out_specs=[_rows(tm, 7 * d), _const((ka, d)), _const((kb, d)), _const((1, 7 * d))],
        out_shape=[jax.ShapeDtypeStruct((t, 7 * d), BF16), jax.ShapeDtypeStruct((ka, d), F32),
                   jax.ShapeDtypeStruct((kb, d), F32), jax.ShapeDtypeStruct((1, 7 * d), F32)],
        scratch_shapes=[pltpu.VMEM((tm + 2 * HALO_A, d), F32), pltpu.VMEM((tm + 2 * HALO_B, d), F32),
                        pltpu.VMEM((len(plan_a), _shifted_rows(tm, off_a), d), F32),
                        pltpu.VMEM((len(plan_b), _shifted_rows(tm, off_b), d), F32),
                        pltpu.VMEM((ka * SUBLANES, d), F32), pltpu.VMEM((kb * SUBLANES, d), F32),
                        pltpu.VMEM((ka * SUBLANES, d), F32), pltpu.VMEM((kb * SUBLANES, d), F32),
                        pltpu.VMEM((SUBLANES, 7 * d), F32)],
        args=(dva, dva, dva, dcb, dcb, dcb, ua, ub, cg, ha, a, sg, dbg, dza, dzb,
              conv_a_w, conv_b_w), ex=ex)


def _dx_call(dproj, x, dx1, g1pre, w_in_g, tm, part, n_parts, prev, ex=None):
    t, d = x.shape
    nb, _, n4 = w_in_g.shape
    ni = nb * n4
    n_steps = t // tm // n_parts
    first = part * n_steps
    rows = lambda width: pl.BlockSpec((tm, width), lambda i: (i + first, 0))
    if prev is None:
        prev = (jnp.zeros((SUBLANES, 128), F32), jnp.zeros((1, d), F32))
    prev_dx, prev_dg = prev

    def body(dp_ref, x_ref, dx1_ref, g_ref, w_hbm, prev_dx_hbm, prev_dg_ref, dx_ref, dg_ref, w_v, dh_s, acc_g, sem):
        _load_blocks_once(w_hbm, w_v, sem)

        @pl.when(pl.program_id(0) == 0)
        def _():
            acc_g[...] = jnp.zeros_like(acc_g)
            acc_g[0:1, :] = prev_dg_ref[...]

        dh_s[...] = lax.dot_general(dp_ref[...], w_v[...], NT_DIMS, preferred_element_type=F32)

        def dnorm(rows):
            dh = dh_s[rows, :]
            xv = x_ref[rows, :]
            r = lax.rsqrt(_mean_lanes(xv * xv) + RMS_EPS)
            gd = dh * g_ref[...]
            dx_ref[rows, :] = dx1_ref[rows, :] + r * gd - xv * (r * r * r * _mean_lanes(gd * xv))
            acc_g[...] += _fold8(dh * xv * r)
        _for_chunks(tm, ROW_CHUNK, dnorm)
        _write_row_sums(acc_g, dg_ref, n_steps)

    return _call(
        body, name="dx_bwd_%d" % part, grid=(n_steps,),
        in_specs=[rows(ni), rows(d), rows(d), _const((1, d)), ANY, ANY, _const((1, d))],
        out_specs=[rows(d), _const((1, d))],
        out_shape=[jax.ShapeDtypeStruct((t, d), F32), jax.ShapeDtypeStruct((1, d), F32)],
        scratch_shapes=[pltpu.VMEM((d, ni), BF16), pltpu.VMEM((tm, d), F32),
                        pltpu.VMEM((SUBLANES, d), F32), pltpu.SemaphoreType.DMA((nb,))],
        args=(dproj, x, dx1, g1pre, w_in_g, prev_dx, prev_dg), ex=ex,
        aliases={5: 0} if part > 0 else None)


def _tn_matmul(a, g, nblk, a_cols, g_cols, a_blocked, g_blocked, tt, name, ex=None):
    t = a.shape[0]

    def body(a_ref, g_ref, o_ref):
        @pl.when(pl.program_id(1) == 0)
        def _():
            o_ref[...] = jnp.zeros_like(o_ref)
        o_ref[0] += lax.dot_general(a_ref[...], g_ref[...], TN_DIMS, preferred_element_type=F32)

    (out,), xouts = _call(
        body, name=name, grid=(nblk, t // tt),
        in_specs=[pl.BlockSpec((tt, a_cols), (lambda b, s: (s, b)) if a_blocked else (lambda b, s: (s, 0))),
                  pl.BlockSpec((tt, g_cols), (lambda b, s: (s, b)) if g_blocked else (lambda b, s: (s, 0)))],
        out_specs=[pl.BlockSpec((1, a_cols, g_cols), lambda b, s: (b, 0, 0))],
        out_shape=[jax.ShapeDtypeStruct((nblk, a_cols, g_cols), F32)],
        scratch_shapes=[], args=(a, g), ex=ex)
    return out, xouts


def _pair_sum_call(g_full, from_sibling, core, name):
    nblk, r, c = g_full.shape
    hr = r // 2
    tr = min(hr, 256)
    n = hr // tr

    def body(core_ref, g_ref, p_ref, o_ref):
        o_ref[...] = (g_ref[...] + p_ref[...]).astype(BF16)

    return pl.pallas_call(
        body, name=name,
        grid_spec=pltpu.PrefetchScalarGridSpec(
            num_scalar_prefetch=1, grid=(nblk, n),
            in_specs=[pl.BlockSpec((1, tr, c), lambda j, i, cr: (j, cr[0] * n + i, 0)),
                      pl.BlockSpec((1, tr, c), lambda j, i, cr: (j, i, 0))],
            out_specs=pl.BlockSpec((1, tr, c), lambda j, i, cr: (j, i, 0))),
        out_shape=jax.ShapeDtypeStruct((nblk, hr, c), BF16),
        compiler_params=_params("parallel", "parallel"))(core, g_full, from_sibling)


def _chip_sum_call(pair, received, chip_core, name):
    _, hr, c = pair.shape
    tr = min(hr, 256)
    n = hr // tr

    def body(cc_ref, own_ref, r_ref, o_ref):
        o_ref[...] = ((own_ref[0].astype(F32) + r_ref[0].astype(F32)) + r_ref[1].astype(F32)) + r_ref[2].astype(F32)

    return pl.pallas_call(
        body, name=name,
        grid_spec=pltpu.PrefetchScalarGridSpec(
            num_scalar_prefetch=1, grid=(n,),
            in_specs=[pl.BlockSpec((1, tr, c), lambda i, cc: (cc[0], i, 0)),
                      pl.BlockSpec((N_CHIPS - 1, tr, c), lambda i, cc: (0, i, 0))],
            out_specs=pl.BlockSpec((tr, c), lambda i, cc: (cc[1] * n + i, 0))),
        out_shape=jax.ShapeDtypeStruct((2 * hr, c), F32),
        compiler_params=_params("parallel"))(chip_core, pair, received)


def _adamw(w, g, m, v):
    m = ADAM_B1 * m + (1.0 - ADAM_B1) * g
    v = ADAM_B2 * v + (1.0 - ADAM_B2) * (g * g)
    m_hat = m / (1.0 - ADAM_B1 ** ADAM_STEP)
    v_hat = v / (1.0 - ADAM_B2 ** ADAM_STEP)
    delta = -ADAM_LR * (m_hat / (jnp.sqrt(v_hat) + ADAM_EPS) + ADAM_WD * w)
    return delta, m, v


def _adam_call(w, g, m, v, name):
    r, c = w.shape
    tr = min(r, 256)

    def body(w_ref, g_ref, m_ref, v_ref, go_ref, d_ref, mo_ref, vo_ref):
        go_ref[...] = g_ref[...]
        d_ref[...], mo_ref[...], vo_ref[...] = _adamw(w_ref[...], g_ref[...], m_ref[...], v_ref[...])

    shape = jax.ShapeDtypeStruct((r, c), F32)
    return pl.pallas_call(
        body, name=name, grid=(r // tr,), in_specs=[_rows(tr, c)] * 4, out_specs=[_rows(tr, c)] * 4,
        out_shape=[shape] * 4, compiler_params=_params("parallel"))(w, g, m, v)


def _gather_conv_weights(conv_a_w, conv_b_w, d):
    ka, dq = conv_a_w.shape
    kb = conv_b_w.shape[0]
    ra = -(-ka // SUBLANES) * SUBLANES
    rb = -(-kb // SUBLANES) * SUBLANES
    a_pad = jnp.pad(conv_a_w, ((0, ra - ka), (0, 0)))
    b_pad = jnp.pad(conv_b_w, ((0, rb - kb), (0, 0)))

    def body(a_ref, b_ref, oa_ref, ob_ref, pack, slots, send_sem, recv_sem):
        x, y, c = _place()
        me = 2 * x + y
        chips = _other_chips(x, y)
        pack[pl.ds(0, ra), :] = a_ref[...]
        pack[pl.ds(ra, rb), :] = b_ref[...]
        copies = []
        for j, (px, py, _) in enumerate(chips):
            cp = pltpu.make_async_remote_copy(
                src_ref=pack, dst_ref=slots.at[me], send_sem=send_sem.at[j], recv_sem=recv_sem.at[j],
                device_id=(px, py, c), device_id_type=MESH)
            cp.start()
            copies.append(cp)
        for j, (px, py, pk) in enumerate(chips):
            pltpu.make_async_remote_copy(
                src_ref=pack, dst_ref=slots.at[pk], send_sem=send_sem.at[j], recv_sem=recv_sem.at[j],
                device_id=(px, py, c), device_id_type=MESH).wait_recv()
        for cp in copies:
            cp.wait_send()
        slots[me] = pack[...]
        for k in range(N_CHIPS):
            oa_ref[:, pl.ds(k * dq, dq)] = slots[k, pl.ds(0, ra), :]
            ob_ref[:, pl.ds(k * dq, dq)] = slots[k, pl.ds(ra, rb), :]

    oa, ob = pl.pallas_call(
        body, name="gather_conv_weights", in_specs=[VMEM_FULL] * 2, out_specs=[VMEM_FULL] * 2,
        out_shape=[jax.ShapeDtypeStruct((ra, d), F32), jax.ShapeDtypeStruct((rb, d), F32)],
        scratch_shapes=[pltpu.VMEM((ra + rb, dq), F32), pltpu.VMEM((N_CHIPS, ra + rb, dq), F32),
                        pltpu.SemaphoreType.DMA((N_CHIPS - 1,)), pltpu.SemaphoreType.DMA((N_CHIPS - 1,))],
        compiler_params=pltpu.CompilerParams(has_side_effects=True))(a_pad, b_pad)
    return oa[:ka], ob[:kb]


def _small_step_call(partials, loss_rows, weights, m_s, v_s, sharded, d):
    n = len(partials)
    row_counts = [p.shape[0] for p in partials]
    starts = [sum(row_counts[:i]) for i in range(n)]
    loss_row = sum(row_counts)
    pack_rows = -(-(loss_row + 1) // SUBLANES) * SUBLANES
    dq = d // N_CHIPS

    def body(*refs):
        p_refs = refs[:n]
        loss_in = refs[n]
        w_refs = refs[n + 1:2 * n + 1]
        m_refs = refs[2 * n + 1:3 * n + 1]
        v_refs = refs[3 * n + 1:4 * n + 1]
        o = 4 * n + 1
        g_out = refs[o:o + n]
        d_out = refs[o + n:o + 2 * n]
        m_out = refs[o + 2 * n:o + 3 * n]
        v_out = refs[o + 3 * n:o + 4 * n]
        loss_out = refs[o + 4 * n]
        pack, from_sibling, slots, send_sem, recv_sem = refs[o + 4 * n + 1:]
        x, y, c = _place()
        me = 2 * x + y

        pack[...] = jnp.zeros_like(pack)
        for i in range(n):
            pack[pl.ds(starts[i], row_counts[i]), :] = p_refs[i][...]
        pack[pl.ds(loss_row, 1), :] = loss_in[...]

        pair = _remote(pack, from_sibling, send_sem.at[0], recv_sem.at[0], (x, y, 1 - c))
        pair.start()
        pair.wait()
        pack[...] = pack[...] + from_sibling[...]
        chips = _other_chips(x, y)
        copies = [_remote(pack, slots.at[me], send_sem.at[1 + j], recv_sem.at[1 + j], (px, py, c))
                  for j, (px, py, _) in enumerate(chips)]
        for cp in copies:
            cp.start()
        for j, (px, py, pk) in enumerate(chips):
            _remote(pack, slots.at[pk], send_sem.at[1 + j], recv_sem.at[1 + j], (px, py, c)).wait_recv()
        for cp in copies:
            cp.wait_send()

        slots[me] = pack[...]
        total = slots[0]
        for k in range(1, N_CHIPS):
            total = total + slots[k]
        pack[...] = total

        loss_out[...] = jnp.broadcast_to(
            (0.5 / d) * jnp.sum(pack[pl.ds(loss_row, 1), :], axis=-1, keepdims=True), loss_out.shape)
        chip = 2 * x + y
        for i in range(n):
            rows = pl.ds(starts[i], row_counts[i])
            if sharded[i]:
                for k in range(N_CHIPS):
                    @pl.when(chip == k)
                    def _():
                        g_out[i][...] = pack[rows, pl.ds(k * dq, dq)]
            else:
                g_out[i][...] = pack[rows, :]
            d_out[i][...], m_out[i][...], v_out[i][...] = _adamw(
                w_refs[i][...], g_out[i][...], m_refs[i][...], v_refs[i][...])

    w_shapes = [jax.ShapeDtypeStruct(w.shape, F32) for w in weights]
    n_in = 4 * n + 1
    return pl.pallas_call(
        body, name="small_grads_allreduce_adamw",
        in_specs=[VMEM_FULL] * n_in, out_specs=[VMEM_FULL] * (4 * n + 1),
        out_shape=w_shapes * 4 + [jax.ShapeDtypeStruct((SUBLANES, 128), F32)],
        scratch_shapes=[pltpu.VMEM((pack_rows, d), F32), pltpu.VMEM((pack_rows, d), F32),
                        pltpu.VMEM((N_CHIPS, pack_rows, d), F32),
                        pltpu.SemaphoreType.DMA((N_CHIPS,)), pltpu.SemaphoreType.DMA((N_CHIPS,))],
        compiler_params=pltpu.CompilerParams(has_side_effects=True, vmem_limit_bytes=VMEM_LIMIT))(
            *partials, loss_rows, *weights, *m_s, *v_s)


def _tile(t, want):
    return min(t, want)


def kernel(x, norm1_pre_g, w_in, b_in, conv_a_w, conv_a_b, w_a_out, conv_b_w, conv_b_b, ln_b_g, ln_b_b, w_b_out, w_o, norm1_post_g, norm2_pre_g, w_mlp_in, w_mlp_out, norm2_post_g, loss_target, m_norm1_pre_g, m_w_in, m_b_in, m_conv_a_w, m_conv_a_b, m_w_a_out, m_conv_b_w, m_conv_b_b, m_ln_b_g, m_ln_b_b, m_w_b_out, m_w_o, m_norm1_post_g, m_norm2_pre_g, m_w_mlp_in, m_w_mlp_out, m_norm2_post_g, v_norm1_pre_g, v_w_in, v_b_in, v_conv_a_w, v_conv_a_b, v_w_a_out, v_conv_b_w, v_conv_b_b, v_ln_b_g, v_ln_b_b, v_w_b_out, v_w_o, v_norm1_post_g, v_norm2_pre_g, v_w_mlp_in, v_w_mlp_out, v_norm2_post_g):
    _, t, d = x.shape
    xt = x.reshape(t, d)
    tgt = loss_target.reshape(t, d)
    row = lambda vec: vec.reshape(1, -1)
    cx, cy, cc = _place()
    core = cc.astype(jnp.int32).reshape(1)
    chip = (2 * cx + cy).astype(jnp.int32).reshape(1)

    big = dict(w_in=w_in, w_a_out=w_a_out, w_b_out=w_b_out, w_o=w_o, w_mlp_in=w_mlp_in, w_mlp_out=w_mlp_out)
    names = list(big)
    chip_core = jnp.concatenate([chip, core])
    slot = {k: _cast_to_slot(big[k], chip, "cast_" + k) for k in names}
    mixer_w, mlp_w = ["w_a_out", "w_b_out", "w_o"], ["w_mlp_in", "w_mlp_out"]
    rows_of = lambda buf: buf.reshape(-1, buf.shape[-1])

    def pair_sums(keys, full, from_sibling):
        return [_pair_sum_call(g, p, core, "pair_sum_" + k) for k, g, p in zip(keys, full, from_sibling)]

    def chip_sums(keys, pairs, received):
        return [_chip_sum_call(p, r, chip_core, "chip_sum_" + k) for k, p, r in zip(keys, pairs, received)]

    (w_in_g,) = _exchange_call("gather_w_in", [_ex_gather_ici([slot["w_in"]]), _ex_gather_forward([slot["w_in"]])])

    g1pre, g1post, g2pre, g2post = row(norm1_pre_g), row(norm1_post_g), row(norm2_pre_g), row(norm2_post_g)
    lng, lnb, ba, bb = row(ln_b_g), row(ln_b_b), row(conv_a_b), row(conv_b_b)
    conv_a_full, conv_b_full = _gather_conv_weights(conv_a_w, conv_b_w, d)

    (h, ua, ub, bg, cg, ha, a, sg, sa, sb), landed = _proj_call(
        xt, g1pre, w_in_g, row(b_in), _tile(t, 512), ex=_ex_gather_ici([slot[k] for k in mixer_w + mlp_w]))
    w_a_g, w_b_g, w_o_g = _exchange_call("forward_mixer_weights", [_ex_gather_forward(landed[:3])])
    w_a_full, w_b_full, w_o_full = rows_of(w_a_g), rows_of(w_b_g), rows_of(w_o_g)
    (x1, va, pa, cb, sbo, ya, yb, mg, mix), (w1_g, w2_g) = _mixer_fwd_call(
        ua, ub, bg, sa, sb, xt, conv_a_full, ba, conv_b_full, bb, lng, lnb,
        w_a_full, w_b_full, w_o_full, g1post, _tile(t, 256), ex=_ex_gather_forward(landed[3:]))
    (dx1, f, df2, h2, df1, dmix, dg2post, dg2pre, dg1post, loss_rows), _ = _mlp_call(
        x1, tgt, mix, g2pre, g2post, g1post, w1_g, rows_of(w2_g), _tile(t, 256))

    tt = _tile(t, 2048)
    n4, fq, dq = w_in.shape[1], w_mlp_in.shape[1], d // N_CHIPS
    g_mlp = [_tn_matmul(h2, df1, N_CHIPS, d, fq, False, True, tt, "dw_mlp_in")[0],
             _tn_matmul(f, df2, N_CHIPS, fq, d, True, False, tt, "dw_mlp_out")[0]]
    (dya, dyb, dva, dcb, dbg, dza, dzb, dlng, dlnb, dba, dbb), sib_mlp = _mixer_bwd_call(
        dmix, sa, sb, ya, yb, bg, va, cb, lng, lnb, w_a_full, w_b_full, w_o_full, _tile(t, 256),
        ex=_ex_sibling_halves(g_mlp))
    p_mlp = pair_sums(mlp_w, g_mlp, sib_mlp)
    g_mix = [_tn_matmul(pa, dya, 1, d, d, False, False, tt, "dw_a_out")[0].reshape(N_CHIPS, dq, d),
             _tn_matmul(sbo, dyb, 1, d, d, False, False, tt, "dw_b_out")[0].reshape(N_CHIPS, dq, d),
             _tn_matmul(mg, dmix, 1, d, d, False, False, tt, "dw_o")[0].reshape(N_CHIPS, dq, d)]
    ex_a, ex_b = _ex_scatter_to_owner(p_mlp), _ex_sibling_halves(g_mix)
    (dproj, dwa_conv, dwb_conv, dbin), xo = _conv_bwd_call(
        dva, dcb, ua, ub, cg, ha, a, sg, dbg, dza, dzb, conv_a_full, conv_b_full, _tile(t, 256),
        ex=_merge(ex_a, ex_b))
    recv_mlp, sib_mix = _split(xo, ex_a, ex_b)
    r_mlp = chip_sums(mlp_w, p_mlp, recv_mlp)
    p_mix = pair_sums(mixer_w, g_mix, sib_mix)
    ex_a, ex_b = _ex_share_halves(r_mlp), _ex_scatter_to_owner(p_mix)
    g_in, xo = _tn_matmul(h, dproj, N_CHIPS, d, n4, False, True, tt, "dw_in", ex=_merge(ex_a, ex_b))
    red_mlp, recv_mix = _split(xo, ex_a, ex_b)
    r_mix = chip_sums(mixer_w, p_mix, recv_mix)
    ex_a, ex_b = _ex_sibling_halves([g_in]), _ex_share_halves(r_mix)
    dx_first, xo = _dx_call(dproj, xt, dx1, g1pre, w_in_g, _tile(t, 512), 0, 2, None, ex=_merge(ex_a, ex_b))
    sib_in, red_mix = _split(xo, ex_a, ex_b)
    p_in = pair_sums(["w_in"], [g_in], sib_in)
    (grad_x, dg1pre), recv_in = _dx_call(dproj, xt, dx1, g1pre, w_in_g, _tile(t, 512), 1, 2, dx_first,
                                         ex=_ex_scatter_to_owner(p_in))
    r_in = chip_sums(["w_in"], p_in, recv_in)
    red_in = _exchange_call("w_in_grad_to_sibling", [_ex_share_halves(r_in)])
    reduced = dict(zip(mlp_w + mixer_w + ["w_in"], red_mlp + red_mix + red_in))

    moments = dict(w_in=(m_w_in, v_w_in), w_a_out=(m_w_a_out, v_w_a_out), w_b_out=(m_w_b_out, v_w_b_out),
                   w_o=(m_w_o, v_w_o), w_mlp_in=(m_w_mlp_in, v_w_mlp_in), w_mlp_out=(m_w_mlp_out, v_w_mlp_out))
    out = {}
    for k in names:
        out[k] = tuple(_adam_call(big[k], reduced[k], *moments[k], "adamw_" + k))

    small = [
        ("conv_b_w", dwb_conv, conv_b_w, m_conv_b_w, v_conv_b_w, True),
        ("conv_b_b", dbb, bb, row(m_conv_b_b), row(v_conv_b_b), False),
        ("b_in", dbin.reshape(7, d), b_in.reshape(7, d), m_b_in.reshape(7, d), v_b_in.reshape(7, d), False),
        ("norm1_pre_g", dg1pre, row(norm1_pre_g), row(m_norm1_pre_g), row(v_norm1_pre_g), False),
        ("conv_a_w", dwa_conv, conv_a_w, m_conv_a_w, v_conv_a_w, True),
        ("conv_a_b", dba, ba, row(m_conv_a_b), row(v_conv_a_b), False),
        ("ln_b_g", dlng, lng, row(m_ln_b_g), row(v_ln_b_g), False),
        ("ln_b_b", dlnb, lnb, row(m_ln_b_b), row(v_ln_b_b), False),
        ("norm1_post_g", dg1post, g1post, row(m_norm1_post_g), row(v_norm1_post_g), False),
        ("norm2_pre_g", dg2pre, g2pre, row(m_norm2_pre_g), row(v_norm2_pre_g), False),
        ("norm2_post_g", dg2post, g2post, row(m_norm2_post_g), row(v_norm2_post_g), False),
    ]
    res = _small_step_call([s[1] for s in small], loss_rows, [s[2] for s in small], [s[3] for s in small],
                           [s[4] for s in small], [s[5] for s in small], d)
    ns = len(small)
    loss = res[4 * ns][0, 0]
    shapes = dict(norm1_pre_g=norm1_pre_g.shape, b_in=b_in.shape, conv_a_w=conv_a_w.shape,
                  conv_a_b=conv_a_b.shape, conv_b_w=conv_b_w.shape, conv_b_b=conv_b_b.shape,
                  ln_b_g=ln_b_g.shape, ln_b_b=ln_b_b.shape, norm1_post_g=norm1_post_g.shape,
                  norm2_pre_g=norm2_pre_g.shape, norm2_post_g=norm2_post_g.shape)
    for i, s in enumerate(small):
        out[s[0]] = tuple(res[q * ns + i].reshape(shapes[s[0]]) for q in range(4))

    order = ["norm1_pre_g", "w_in", "b_in", "conv_a_w", "conv_a_b", "w_a_out", "conv_b_w", "conv_b_b",
             "ln_b_g", "ln_b_b", "w_b_out", "w_o", "norm1_post_g", "norm2_pre_g", "w_mlp_in", "w_mlp_out",
             "norm2_post_g"]
    return (loss, grad_x.reshape(x.shape), *[out[k][0] for k in order], *[out[k][1] for k in order],
            *[out[k][2] for k in order], *[out[k][3] for k in order])
```

```python
import functools

import jax
import jax.numpy as jnp
from jax import lax
from jax.experimental import pallas as pl
from jax.experimental.pallas import tpu as pltpu

RMS_EPS = 1e-6
LN_EPS = 1e-5
ADAM_LR = 0.001
ADAM_B1 = 0.9
ADAM_B2 = 0.999
ADAM_EPS = 1e-08
ADAM_WD = 0.01
ADAM_STEP = 10

F32 = jnp.float32
BF16 = jnp.bfloat16
MESH = pl.DeviceIdType.MESH
ANY = pl.BlockSpec(memory_space=pl.ANY)
VMEM_FULL = pl.BlockSpec(memory_space=pltpu.VMEM)

V7X_VMEM_BYTES = 64 * 1024 * 1024
VMEM_LIMIT = V7X_VMEM_BYTES - 8 * 1024 * 1024
SUBLANES = 8
N_CHIPS = 4
N_DEV = 8
HALO_A = 8
HALO_B = 16
CONV_ROWS = 16
ROW_CHUNK = 32

NT_DIMS = (((1,), (1,)), ((), ()))
TN_DIMS = (((0,), (0,)), ((), ()))


def _params(*sem):
    return pltpu.CompilerParams(dimension_semantics=sem, vmem_limit_bytes=VMEM_LIMIT)


def _rows(tm, d):
    return pl.BlockSpec((tm, d), lambda i: (i, 0))


def _const(shape):
    return pl.BlockSpec(shape, lambda i: (0,) * len(shape))


def _halo_prev(tm, hb, d):
    return pl.BlockSpec((hb, d), lambda i: (jnp.maximum(i * (tm // hb) - 1, 0), 0))


def _halo_next(tm, hb, d, t):
    return pl.BlockSpec((hb, d), lambda i: (jnp.minimum((i + 1) * (tm // hb), t // hb - 1), 0))


def _for_chunks(n_rows, rc, fn):
    for r0 in range(0, n_rows, rc):
        fn(pl.ds(r0, rc))


def _fold8(v):
    return v.reshape(v.shape[0] // SUBLANES, SUBLANES, v.shape[1]).sum(axis=0)


def _mean_lanes(v):
    return jnp.mean(v, axis=-1, keepdims=True)


def _load_blocks_once(w_hbm, w_vmem, sem):
    nb, _, n = w_hbm.shape

    @pl.when(pl.program_id(0) == 0)
    def _():
        copies = [pltpu.make_async_copy(w_hbm.at[j], w_vmem.at[:, pl.ds(j * n, n)], sem.at[j])
                  for j in range(nb)]
        for cp in copies:
            cp.start()
        for cp in copies:
            cp.wait()


def _load_once(w_hbm, w_vmem, sem):
    @pl.when(pl.program_id(0) == 0)
    def _():
        cp = pltpu.make_async_copy(w_hbm, w_vmem, sem)
        cp.start()
        cp.wait()


def _write_row_sums(acc_ref, out_ref, n_steps):
    @pl.when(pl.program_id(0) == n_steps - 1)
    def _():
        out_ref[...] = jnp.sum(acc_ref[...], axis=0, keepdims=True)


def _place():
    return lax.axis_index("x"), lax.axis_index("y"), lax.axis_index("c")


def _other_chips(x, y):
    rel = [(x, 1 - y), (1 - x, y), (1 - x, 1 - y)]
    return [(px, py, 2 * px + py) for px, py in rel]


class _Exchange:
    def __init__(self, inputs, out_shapes, aliases, n_sems, copies):
        self.inputs = list(inputs)
        self.out_shapes = list(out_shapes)
        self.aliases = dict(aliases)
        self.n_sems = n_sems
        self.copies = copies


def _remote(src, dst, send, recv, device):
    return pltpu.make_async_remote_copy(src_ref=src, dst_ref=dst, send_sem=send, recv_sem=recv,
                                        device_id=device, device_id_type=MESH)


def _sds(a):
    return jax.ShapeDtypeStruct(a.shape, a.dtype)


def _ex_gather_ici(bufs):
    n = len(bufs)

    def copies(xin, xout, send, recv):
        x, y, c = _place()
        me = 2 * x + y
        out = []
        for a in range(n):
            hr = xin[a].shape[1] // 2
            rows = pl.ds(c * hr, hr)
            for j, (px, py, _) in enumerate(_other_chips(x, y)):
                k = a * (N_CHIPS - 1) + j
                out.append(_remote(xin[a].at[me, rows, :], xout[a].at[me, rows, :], send(k), recv(k), (px, py, c)))
        return out

    return _Exchange(bufs, [_sds(b) for b in bufs], {a: a for a in range(n)}, n * (N_CHIPS - 1), copies)


def _ex_gather_forward(bufs):
    n = len(bufs)

    def copies(xin, xout, send, recv):
        x, y, c = _place()
        out = []
        for a in range(n):
            hr = xin[a].shape[1] // 2
            rows = pl.ds(c * hr, hr)
            for j, (_, _, pk) in enumerate(_other_chips(x, y)):
                k = a * (N_CHIPS - 1) + j
                out.append(_remote(xin[a].at[pk, rows, :], xout[a].at[pk, rows, :], send(k), recv(k), (x, y, 1 - c)))
        return out

    return _Exchange(bufs, [_sds(b) for b in bufs], {a: a for a in range(n)}, n * (N_CHIPS - 1), copies)


def _ex_sibling_halves(grads):
    n = len(grads)

    def copies(xin, xout, send, recv):
        x, y, c = _place()
        out = []
        for a in range(n):
            hr = xin[a].shape[1] // 2
            out.append(_remote(xin[a].at[:, pl.ds((1 - c) * hr, hr), :], xout[a], send(a), recv(a), (x, y, 1 - c)))
        return out

    shapes = [jax.ShapeDtypeStruct((g.shape[0], g.shape[1] // 2, g.shape[2]), g.dtype) for g in grads]
    return _Exchange(grads, shapes, {}, n, copies)


def _ex_scatter_to_owner(pairs):
    n = len(pairs)

    def copies(xin, xout, send, recv):
        x, y, c = _place()
        out = []
        for a in range(n):
            for j, (px, py, pk) in enumerate(_other_chips(x, y)):
                k = a * (N_CHIPS - 1) + j
                out.append(_remote(xin[a].at[pk], xout[a].at[j], send(k), recv(k), (px, py, c)))
        return out

    shapes = [jax.ShapeDtypeStruct((N_CHIPS - 1,) + p.shape[1:], p.dtype) for p in pairs]
    return _Exchange(pairs, shapes, {}, n * (N_CHIPS - 1), copies)


def _ex_share_halves(reduced):
    n = len(reduced)

    def copies(xin, xout, send, recv):
        x, y, c = _place()
        out = []
        for a in range(n):
            hr = xin[a].shape[0] // 2
            rows = pl.ds(c * hr, hr)
            out.append(_remote(xin[a].at[rows, :], xout[a].at[rows, :], send(a), recv(a), (x, y, 1 - c)))
        return out

    return _Exchange(reduced, [_sds(r) for r in reduced], {a: a for a in range(n)}, n, copies)


def _merge(*exs):
    exs = [e for e in exs if e is not None]
    if not exs:
        return None
    inputs, shapes, aliases = [], [], {}
    in_off, out_off, sem_off = [], [], []
    n_sems = 0
    for e in exs:
        in_off.append(len(inputs))
        out_off.append(len(shapes))
        sem_off.append(n_sems)
        aliases.update({len(inputs) + i: len(shapes) + o for i, o in e.aliases.items()})
        inputs += e.inputs
        shapes += e.out_shapes
        n_sems += e.n_sems

    def copies(xin, xout, send, recv):
        out = []
        for e, io, oo, so in zip(exs, in_off, out_off, sem_off):
            out += e.copies(xin[io:io + len(e.inputs)], xout[oo:oo + len(e.out_shapes)],
                            lambda i, so=so: send(so + i), lambda i, so=so: recv(so + i))
        return out

    return _Exchange(inputs, shapes, aliases, n_sems, copies)


def _split(ex_outs, *exs):
    parts, o = [], 0
    for e in exs:
        parts.append(list(ex_outs[o:o + len(e.out_shapes)]))
        o += len(e.out_shapes)
    return parts


def _call(body, *, name, grid, in_specs, out_specs, out_shape, scratch_shapes, args, ex=None, aliases=None):
    n_in, n_out, n_scr = len(in_specs), len(out_specs), len(scratch_shapes)
    seq = ("arbitrary",) * len(grid)
    aliases = dict(aliases or {})
    if ex is None:
        outs = pl.pallas_call(
            body, name=name, grid=grid, in_specs=list(in_specs), out_specs=list(out_specs),
            out_shape=list(out_shape), scratch_shapes=list(scratch_shapes), input_output_aliases=aliases,
            compiler_params=_params(*seq))(*args)
        return list(outs), []
    n_xi, n_xo = len(ex.inputs), len(ex.out_shapes)

    def full(*refs):
        ins, xin = refs[:n_in], refs[n_in:n_in + n_xi]
        o = n_in + n_xi
        outs, xout = refs[o:o + n_out], refs[o + n_out:o + n_out + n_xo]
        s = o + n_out + n_xo
        scr = refs[s:s + n_scr]
        send_sems, recv_sems = refs[s + n_scr:]
        send = lambda i: send_sems.at[i]
        recv = lambda i: recv_sems.at[i]
        first = functools.reduce(jnp.logical_and, [pl.program_id(a) == 0 for a in range(len(grid))])
        last = functools.reduce(jnp.logical_and, [pl.program_id(a) == grid[a] - 1 for a in range(len(grid))])

        @pl.when(first)
        def _():
            for cp in ex.copies(xin, xout, send, recv):
                cp.start()

        body(*ins, *outs, *scr)

        @pl.when(last)
        def _():
            for cp in ex.copies(xin, xout, send, recv):
                cp.wait()

    res = pl.pallas_call(
        full, name=name, grid=grid, in_specs=list(in_specs) + [ANY] * n_xi,
        out_specs=list(out_specs) + [ANY] * n_xo, out_shape=list(out_shape) + ex.out_shapes,
        scratch_shapes=list(scratch_shapes) + [pltpu.SemaphoreType.DMA((ex.n_sems,)),
                                               pltpu.SemaphoreType.DMA((ex.n_sems,))],
        input_output_aliases={**aliases, **{n_in + i: n_out + o for i, o in ex.aliases.items()}},
        compiler_params=pltpu.CompilerParams(dimension_semantics=seq, vmem_limit_bytes=VMEM_LIMIT,
                                             has_side_effects=True))(*args, *ex.inputs)
    return list(res[:n_out]), list(res[n_out:])


def _exchange_call(name, phases):
    first = phases[0]
    n_xi, n_xo = len(first.inputs), len(first.out_shapes)

    def body(*refs):
        xin, xout = refs[:n_xi], refs[n_xi:n_xi + n_xo]
        sems = refs[n_xi + n_xo:]
        for p, ex in enumerate(phases):
            send_sems, recv_sems = sems[2 * p], sems[2 * p + 1]
            cps = ex.copies(xin, xout, lambda i: send_sems.at[i], lambda i: recv_sems.at[i])
            for cp in cps:
                cp.start()
            for cp in cps:
                cp.wait()

    sems = []
    for ex in phases:
        sems += [pltpu.SemaphoreType.DMA((ex.n_sems,)), pltpu.SemaphoreType.DMA((ex.n_sems,))]
    return list(pl.pallas_call(
        body, name=name, in_specs=[ANY] * n_xi, out_specs=[ANY] * n_xo, out_shape=first.out_shapes,
        scratch_shapes=sems, input_output_aliases=dict(first.aliases),
        compiler_params=pltpu.CompilerParams(has_side_effects=True))(*first.inputs))


def _cast_to_slot(w, chip, name):
    r, c = w.shape
    tr = min(r, 256)

    def body(chip_ref, w_ref, o_ref):
        o_ref[0] = w_ref[...].astype(BF16)

    return pl.pallas_call(
        body, name=name,
        grid_spec=pltpu.PrefetchScalarGridSpec(
            num_scalar_prefetch=1, grid=(r // tr,),
            in_specs=[pl.BlockSpec((tr, c), lambda i, k: (i, 0))],
            out_specs=pl.BlockSpec((1, tr, c), lambda i, k: (k[0], i, 0))),
        out_shape=jax.ShapeDtypeStruct((N_CHIPS, r, c), BF16),
        compiler_params=_params("parallel"))(chip, w)


def _proj_call(x, g1pre, w_in_g, b_in, tm, ex=None):
    t, d = x.shape
    nb, _, n4 = w_in_g.shape
    ni = nb * n4
    assert ni == 7 * d

    def body(x_ref, g_ref, b_ref, w_hbm, h_ref, ua_ref, ub_ref, bg_ref, cg_ref, ha_ref, a_ref,
             sg_ref, sa_ref, sb_ref, w_v, p0, p1, sem):
        _load_blocks_once(w_hbm, w_v, sem)

        def norm(rows):
            xv = x_ref[rows, :]
            r = lax.rsqrt(_mean_lanes(xv * xv) + RMS_EPS)
            h_ref[rows, :] = (xv * r * g_ref[...]).astype(BF16)
        _for_chunks(tm, ROW_CHUNK, norm)

        def group(i, dst):
            cols = pl.ds(i * d, d)
            dst[...] = jnp.dot(h_ref[...], w_v[:, cols], preferred_element_type=F32) + b_ref[:, cols]

        group(0, p0)

        def bgate(rows):
            bg_ref[rows, :] = p0[rows, :].astype(BF16)
        _for_chunks(tm, ROW_CHUNK, bgate)

        group(1, p0)
        group(2, p1)

        def branch_a(rows):
            cg, ha = p0[rows, :], p1[rows, :]
            ua_ref[rows, :] = cg * ha
            cg_ref[rows, :] = cg.astype(BF16)
            ha_ref[rows, :] = ha.astype(BF16)
        _for_chunks(tm, ROW_CHUNK, branch_a)

        group(3, p0)
        group(4, p1)

        def branch_b(rows):
            a, sg = p0[rows, :], jax.nn.sigmoid(p1[rows, :])
            ub_ref[rows, :] = a * sg
            a_ref[rows, :] = a.astype(BF16)
            sg_ref[rows, :] = sg.astype(BF16)
        _for_chunks(tm, ROW_CHUNK, branch_b)

        group(5, p0)
        group(6, p1)

        def gates(rows):
            sa_ref[rows, :] = jax.nn.sigmoid(p0[rows, :]).astype(BF16)
            sb_ref[rows, :] = jax.nn.sigmoid(p1[rows, :]).astype(BF16)
        _for_chunks(tm, ROW_CHUNK, gates)

    bf = jax.ShapeDtypeStruct((t, d), BF16)
    f32 = jax.ShapeDtypeStruct((t, d), F32)
    return _call(
        body, name="proj_fwd", grid=(t // tm,),
        in_specs=[_rows(tm, d), _const((1, d)), _const((1, ni)), ANY],
        out_specs=[_rows(tm, d)] * 10,
        out_shape=[bf, f32, f32, bf, bf, bf, bf, bf, bf, bf],
        scratch_shapes=[pltpu.VMEM((d, ni), BF16), pltpu.VMEM((tm, d), F32), pltpu.VMEM((tm, d), F32),
                        pltpu.SemaphoreType.DMA((nb,))],
        args=(x, g1pre, b_in, w_in_g), ex=ex)


def _fill_ext(ext, prev_ref, cur_ref, next_ref, hb, tm, i, n_steps):
    ext[pl.ds(0, hb), :] = jnp.where(i > 0, prev_ref[...], 0.0)
    ext[pl.ds(hb, tm), :] = cur_ref[...]
    ext[pl.ds(hb + tm, hb), :] = jnp.where(i < n_steps - 1, next_ref[...], 0.0)


def _shift_plan(offsets):
    shifts = sorted({o % SUBLANES for o in offsets if o % SUBLANES})
    return {s: i for i, s in enumerate(shifts)}


def _shifted_rows(tm, offsets):
    return tm + SUBLANES * max(o // SUBLANES for o in offsets)


def _fill_shifted(ext, sh, plan):
    n = sh.shape[1]
    for s, i in plan.items():
        sh[i, :, :] = ext[pl.ds(s, n), :]


def _fill_tap_rows(w_ref, rows8):
    @pl.when(pl.program_id(0) == 0)
    def _():
        for k in range(w_ref.shape[0]):
            rows8[pl.ds(k * SUBLANES, SUBLANES), :] = jnp.broadcast_to(w_ref[k:k + 1, :], (SUBLANES, w_ref.shape[1]))


def _tap(rows8, k):
    w8 = rows8[pl.ds(k * SUBLANES, SUBLANES), :]
    return jnp.concatenate([w8] * (CONV_ROWS // SUBLANES), axis=0)


def _window(ext, sh, plan, offset, r0):
    q, s = divmod(offset, SUBLANES)
    if s == 0:
        return ext[pl.ds(offset + r0, CONV_ROWS), :]
    return sh[plan[s], pl.ds(SUBLANES * q + r0, CONV_ROWS), :]


def _mixer_fwd_call(ua, ub, bg, sa, sb, x, conv_a_w, conv_a_b, conv_b_w, conv_b_b, ln_g, ln_b,
                    w_a, w_b, w_o, g1post, tm, ex=None):
    t, d = x.shape
    n_steps = t // tm
    ka, kb = conv_a_w.shape[0], conv_b_w.shape[0]
    off_a = [HALO_A - (ka - 1) // 2 + k for k in range(ka)]
    off_b = [HALO_B - (kb - 1) // 2 + k for k in range(kb)]
    plan_a, plan_b = _shift_plan(off_a), _shift_plan(off_b)

    def body(uap, uac, uan, ubp, ubc, ubn, bg_ref, sa_ref, sb_ref, x_ref, wa_c, ba_c, wb_c, bb_c,
             lng, lnb, wa_hbm, wb_hbm, wo_hbm, g_ref,
             x1_ref, va_ref, pa_ref, cb_ref, sbo_ref, ya_ref, yb_ref, mg_ref, mix_ref,
             ext_a, ext_b, sh_a, sh_b, wa8, wb8, wa_v, wb_v, wo_v, y0, y1, sem):
        i = pl.program_id(0)
        _fill_tap_rows(wa_c, wa8)
        _fill_tap_rows(wb_c, wb8)
        _load_once(wa_hbm, wa_v, sem.at[0])
        _load_once(wb_hbm, wb_v, sem.at[1])
        _load_once(wo_hbm, wo_v, sem.at[2])
        _fill_ext(ext_a, uap, uac, uan, HALO_A, tm, i, n_steps)
        _fill_ext(ext_b, ubp, ubc, ubn, HALO_B, tm, i, n_steps)
        _fill_shifted(ext_a, sh_a, plan_a)
        _fill_shifted(ext_b, sh_b, plan_b)

        for r0 in range(0, tm, CONV_ROWS):
            rows = pl.ds(r0, CONV_ROWS)
            va = jnp.broadcast_to(ba_c[...], (CONV_ROWS, d))
            for k in range(ka):
                va = va + _tap(wa8, k) * _window(ext_a, sh_a, plan_a, off_a[k], r0)
            va_ref[rows, :] = va.astype(BF16)
            pa_ref[rows, :] = (bg_ref[rows, :].astype(F32) * va).astype(BF16)
            cb = jnp.broadcast_to(bb_c[...], (CONV_ROWS, d))
            for k in range(kb):
                cb = cb + _tap(wb8, k) * _window(ext_b, sh_b, plan_b, off_b[k], r0)
            cb_ref[rows, :] = cb
            mu = _mean_lanes(cb)
            cen = cb - mu
            rstd = lax.rsqrt(_mean_lanes(cen * cen) + LN_EPS)
            ln = cen * rstd * lng[...] + lnb[...]
            sbo_ref[rows, :] = (ln * jax.nn.sigmoid(ln)).astype(BF16)

        y0[...] = jnp.dot(pa_ref[...], wa_v[...], preferred_element_type=F32)
        y1[...] = jnp.dot(sbo_ref[...], wb_v[...], preferred_element_type=F32)

        def merge(rows):
            ya, yb = y0[rows, :], y1[rows, :]
            ya_ref[rows, :] = ya.astype(BF16)
            yb_ref[rows, :] = yb.astype(BF16)
            mg_ref[rows, :] = (sa_ref[rows, :].astype(F32) * ya + sb_ref[rows, :].astype(F32) * yb).astype(BF16)
        _for_chunks(tm, ROW_CHUNK, merge)

        mix_ref[...] = jnp.dot(mg_ref[...], wo_v[...], preferred_element_type=F32)

        def resid(rows):
            mix = mix_ref[rows, :]
            r = lax.rsqrt(_mean_lanes(mix * mix) + RMS_EPS)
            x1_ref[rows, :] = x_ref[rows, :] + mix * r * g_ref[...]
        _for_chunks(tm, ROW_CHUNK, resid)

    bf = jax.ShapeDtypeStruct((t, d), BF16)
    f32 = jax.ShapeDtypeStruct((t, d), F32)
    return _call(
        body, name="mixer_fwd", grid=(n_steps,),
        in_specs=[_halo_prev(tm, HALO_A, d), _rows(tm, d), _halo_next(tm, HALO_A, d, t),
                  _halo_prev(tm, HALO_B, d), _rows(tm, d), _halo_next(tm, HALO_B, d, t),
                  _rows(tm, d), _rows(tm, d), _rows(tm, d), _rows(tm, d),
                  _const((ka, d)), _const((1, d)), _const((kb, d)), _const((1, d)),
                  _const((1, d)), _const((1, d)), ANY, ANY, ANY, _const((1, d))],
        out_specs=[_rows(tm, d)] * 9,
        out_shape=[f32, bf, bf, f32, bf, bf, bf, bf, f32],
        scratch_shapes=[pltpu.VMEM((tm + 2 * HALO_A, d), F32), pltpu.VMEM((tm + 2 * HALO_B, d), F32),
                        pltpu.VMEM((len(plan_a), _shifted_rows(tm, off_a), d), F32),
                        pltpu.VMEM((len(plan_b), _shifted_rows(tm, off_b), d), F32),
                        pltpu.VMEM((ka * SUBLANES, d), F32), pltpu.VMEM((kb * SUBLANES, d), F32),
                        pltpu.VMEM((d, d), BF16), pltpu.VMEM((d, d), BF16), pltpu.VMEM((d, d), BF16),
                        pltpu.VMEM((tm, d), F32), pltpu.VMEM((tm, d), F32),
                        pltpu.SemaphoreType.DMA((3,))],
        args=(ua, ua, ua, ub, ub, ub, bg, sa, sb, x, conv_a_w, conv_a_b, conv_b_w, conv_b_b,
              ln_g, ln_b, w_a, w_b, w_o, g1post), ex=ex)


def _mlp_call(x1, target, mix, g2pre, g2post, g1post, w1_g, w2, tm, ex=None):
    t, d = x1.shape
    nb, _, fq = w1_g.shape
    f = nb * fq
    n_steps = t // tm
    inv_d = 1.0 / d

    def body(x1_ref, t_ref, mix_ref, gpre, gpost, gmix, w1_hbm, w2_hbm,
             dx1_ref, f_ref, df2_ref, h2_ref, df1_ref, dmix_ref, dgpost_ref, dgpre_ref, dgmix_ref, loss_ref,
             w1_v, w2_v, f1_s, blk_s, f2_s, acc_post, acc_pre, acc_mix, acc_loss, sem):
        _load_blocks_once(w1_hbm, w1_v, sem)
        _load_once(w2_hbm, w2_v, sem.at[nb])

        @pl.when(pl.program_id(0) == 0)
        def _():
            acc_post[...] = jnp.zeros_like(acc_post)
            acc_pre[...] = jnp.zeros_like(acc_pre)
            acc_mix[...] = jnp.zeros_like(acc_mix)
            acc_loss[...] = jnp.zeros_like(acc_loss)

        def norm(rows):
            xv = x1_ref[rows, :]
            r = lax.rsqrt(_mean_lanes(xv * xv) + RMS_EPS)
            h2_ref[rows, :] = (xv * r * gpre[...]).astype(BF16)
        _for_chunks(tm, ROW_CHUNK, norm)

        for j in range(nb):
            cols = pl.ds(j * fq, fq)
            f1_s[:, cols] = jnp.dot(h2_ref[...], w1_v[:, cols], preferred_element_type=F32)

        def act(rows):
            relu = jnp.maximum(f1_s[rows, :], 0.0)
            f_ref[rows, :] = (relu * relu).astype(BF16)
        _for_chunks(tm, ROW_CHUNK, act)

        f2_s[...] = jnp.dot(f_ref[...], w2_v[...], preferred_element_type=F32)

        def head(rows):
            f2 = f2_s[rows, :]
            rf = lax.rsqrt(_mean_lanes(f2 * f2) + RMS_EPS)
            y = x1_ref[rows, :] + f2 * rf * gpost[...]
            err = y - t_ref[rows, :]
            acc_loss[...] += _fold8(err * err)
            dy = err * inv_d
            gdy = dy * gpost[...]
            df2 = rf * gdy - f2 * (rf * rf * rf * _mean_lanes(gdy * f2))
            df2_ref[rows, :] = df2.astype(BF16)
            acc_post[...] += _fold8(dy * f2 * rf)
            dx1_ref[rows, :] = dy
        _for_chunks(tm, ROW_CHUNK, head)

        for j in range(nb):
            cols = pl.ds(j * fq, fq)
            blk_s[...] = lax.dot_general(df2_ref[...], w2_v[cols, :], NT_DIMS, preferred_element_type=F32)

            def dact(rows):
                relu = jnp.maximum(f1_s[rows, cols], 0.0)
                df1_ref[rows, cols] = (blk_s[rows, :] * (2.0 * relu)).astype(BF16)
            _for_chunks(tm, ROW_CHUNK, dact)

        f2_s[...] = lax.dot_general(df1_ref[...], w1_v[...], NT_DIMS, preferred_element_type=F32)

        def dnorm(rows):
            dh2 = f2_s[rows, :]
            xv = x1_ref[rows, :]
            r = lax.rsqrt(_mean_lanes(xv * xv) + RMS_EPS)
            gd = dh2 * gpre[...]
            dxv = dx1_ref[rows, :] + r * gd - xv * (r * r * r * _mean_lanes(gd * xv))
            dx1_ref[rows, :] = dxv
            acc_pre[...] += _fold8(dh2 * xv * r)
            mix = mix_ref[rows, :]
            rm = lax.rsqrt(_mean_lanes(mix * mix) + RMS_EPS)
            gm = dxv * gmix[...]
            dmix_ref[rows, :] = (rm * gm - mix * (rm * rm * rm * _mean_lanes(gm * mix))).astype(BF16)
            acc_mix[...] += _fold8(dxv * mix * rm)
        _for_chunks(tm, ROW_CHUNK, dnorm)

        _write_row_sums(acc_post, dgpost_ref, n_steps)
        _write_row_sums(acc_pre, dgpre_ref, n_steps)
        _write_row_sums(acc_mix, dgmix_ref, n_steps)
        _write_row_sums(acc_loss, loss_ref, n_steps)

    row = jax.ShapeDtypeStruct((1, d), F32)
    return _call(
        body, name="mlp_fwd_bwd", grid=(n_steps,),
        in_specs=[_rows(tm, d), _rows(tm, d), _rows(tm, d), _const((1, d)), _const((1, d)), _const((1, d)), ANY, ANY],
        out_specs=[_rows(tm, d), _rows(tm, f), _rows(tm, d), _rows(tm, d), _rows(tm, f), _rows(tm, d),
                   _const((1, d)), _const((1, d)), _const((1, d)), _const((1, d))],
        out_shape=[jax.ShapeDtypeStruct((t, d), F32), jax.ShapeDtypeStruct((t, f), BF16),
                   jax.ShapeDtypeStruct((t, d), BF16), jax.ShapeDtypeStruct((t, d), BF16),
                   jax.ShapeDtypeStruct((t, f), BF16), jax.ShapeDtypeStruct((t, d), BF16), row, row, row, row],
        scratch_shapes=[pltpu.VMEM((d, f), BF16), pltpu.VMEM((f, d), BF16),
                        pltpu.VMEM((tm, f), F32), pltpu.VMEM((tm, fq), F32), pltpu.VMEM((tm, d), F32),
                        pltpu.VMEM((SUBLANES, d), F32), pltpu.VMEM((SUBLANES, d), F32),
                        pltpu.VMEM((SUBLANES, d), F32), pltpu.VMEM((SUBLANES, d), F32),
                        pltpu.SemaphoreType.DMA((nb + 1,))],
        args=(x1, target, mix, g2pre, g2post, g1post, w1_g, w2), ex=ex)


def _mixer_bwd_call(dmix, sa, sb, ya, yb, bg, va, cb, ln_g, ln_b, w_a, w_b, w_o, tm, ex=None):
    t, d = dmix.shape
    n_steps = t // tm

    def body(dmix_ref, sa_ref, sb_ref, ya_ref, yb_ref, bg_ref, va_ref, cb_ref, lng, lnb,
             wa_hbm, wb_hbm, wo_hbm,
             dya_ref, dyb_ref, dva_ref, dcb_ref, dbg_ref, dza_ref, dzb_ref,
             dlng_ref, dlnb_ref, dba_ref, dbb_ref,
             wa_v, wb_v, wo_v, s0, s1, acc_lng, acc_lnb, acc_ba, acc_bb, sem):
        _load_once(wa_hbm, wa_v, sem.at[0])
        _load_once(wb_hbm, wb_v, sem.at[1])
        _load_once(wo_hbm, wo_v, sem.at[2])
        accs = (acc_lng, acc_lnb, acc_ba, acc_bb)

        @pl.when(pl.program_id(0) == 0)
        def _():
            for acc in accs:
                acc[...] = jnp.zeros_like(acc)

        s0[...] = lax.dot_general(dmix_ref[...], wo_v[...], NT_DIMS, preferred_element_type=F32)

        def dmerge(rows):
            dm = s0[rows, :]
            sav, sbv = sa_ref[rows, :].astype(F32), sb_ref[rows, :].astype(F32)
            dya_ref[rows, :] = (dm * sav).astype(BF16)
            dyb_ref[rows, :] = (dm * sbv).astype(BF16)
            dza_ref[rows, :] = (dm * ya_ref[rows, :].astype(F32) * sav * (1.0 - sav)).astype(BF16)
            dzb_ref[rows, :] = (dm * yb_ref[rows, :].astype(F32) * sbv * (1.0 - sbv)).astype(BF16)
        _for_chunks(tm, ROW_CHUNK, dmerge)

        s0[...] = lax.dot_general(dya_ref[...], wa_v[...], NT_DIMS, preferred_element_type=F32)
        s1[...] = lax.dot_general(dyb_ref[...], wb_v[...], NT_DIMS, preferred_element_type=F32)

        def dbranches(rows):
            dpa = s0[rows, :]
            dbg_ref[rows, :] = (dpa * va_ref[rows, :].astype(F32)).astype(BF16)
            dva = dpa * bg_ref[rows, :].astype(F32)
            dva_ref[rows, :] = dva
            acc_ba[...] += _fold8(dva)
            cbv = cb_ref[rows, :]
            mu = _mean_lanes(cbv)
            cen = cbv - mu
            rstd = lax.rsqrt(_mean_lanes(cen * cen) + LN_EPS)
            xhat = cen * rstd
            ln = xhat * lng[...] + lnb[...]
            sig = jax.nn.sigmoid(ln)
            dln = s1[rows, :] * (sig * (1.0 + ln * (1.0 - sig)))
            acc_lng[...] += _fold8(dln * xhat)
            acc_lnb[...] += _fold8(dln)
            dxh = dln * lng[...]
            dcb = rstd * (dxh - _mean_lanes(dxh) - xhat * _mean_lanes(dxh * xhat))
            dcb_ref[rows, :] = dcb
            acc_bb[...] += _fold8(dcb)
        _for_chunks(tm, ROW_CHUNK, dbranches)

        _write_row_sums(acc_lng, dlng_ref, n_steps)
        _write_row_sums(acc_lnb, dlnb_ref, n_steps)
        _write_row_sums(acc_ba, dba_ref, n_steps)
        _write_row_sums(acc_bb, dbb_ref, n_steps)

    bf = jax.ShapeDtypeStruct((t, d), BF16)
    f32 = jax.ShapeDtypeStruct((t, d), F32)
    row = jax.ShapeDtypeStruct((1, d), F32)
    return _call(
        body, name="mixer_bwd", grid=(n_steps,),
        in_specs=[_rows(tm, d)] * 8 + [_const((1, d))] * 2 + [ANY, ANY, ANY],
        out_specs=[_rows(tm, d)] * 7 + [_const((1, d))] * 4,
        out_shape=[bf, bf, f32, f32, bf, bf, bf, row, row, row, row],
        scratch_shapes=[pltpu.VMEM((d, d), BF16), pltpu.VMEM((d, d), BF16), pltpu.VMEM((d, d), BF16),
                        pltpu.VMEM((tm, d), F32), pltpu.VMEM((tm, d), F32)]
        + [pltpu.VMEM((SUBLANES, d), F32)] * 4 + [pltpu.SemaphoreType.DMA((3,))],
        args=(dmix, sa, sb, ya, yb, bg, va, cb, ln_g, ln_b, w_a, w_b, w_o), ex=ex)


def _conv_bwd_call(dva, dcb, ua, ub, cg, ha, a, sg, dbg, dza, dzb, conv_a_w, conv_b_w, tm, ex=None):
    t, d = dva.shape
    n_steps = t // tm
    ka, kb = conv_a_w.shape[0], conv_b_w.shape[0]
    off_a = [HALO_A + (ka - 1) // 2 - k for k in range(ka)]
    off_b = [HALO_B + (kb - 1) // 2 - k for k in range(kb)]
    plan_a, plan_b = _shift_plan(off_a), _shift_plan(off_b)

    def body(dvap, dvac, dvan, dcbp, dcbc, dcbn, ua_ref, ub_ref,
             cg_ref, ha_ref, a_ref, sg_ref, dbg_ref, dza_ref, dzb_ref, wa_c, wb_c,
             dproj_ref, dwa_ref, dwb_ref, dbin_ref,
             e_dva, e_dcb, sh_a, sh_b, wa8, wb8, acc_wa, acc_wb, acc_bin):
        i = pl.program_id(0)
        _fill_tap_rows(wa_c, wa8)
        _fill_tap_rows(wb_c, wb8)

        @pl.when(i == 0)
        def _():
            acc_wa[...] = jnp.zeros_like(acc_wa)
            acc_wb[...] = jnp.zeros_like(acc_wb)
            acc_bin[...] = jnp.zeros_like(acc_bin)

        _fill_ext(e_dva, dvap, dvac, dvan, HALO_A, tm, i, n_steps)
        _fill_ext(e_dcb, dcbp, dcbc, dcbn, HALO_B, tm, i, n_steps)
        _fill_shifted(e_dva, sh_a, plan_a)
        _fill_shifted(e_dcb, sh_b, plan_b)

        def put(col, rows, val_f32):
            dproj_ref[rows, pl.ds(col * d, d)] = val_f32.astype(BF16)
            acc_bin[:, pl.ds(col * d, d)] += _fold8(val_f32)

        for r0 in range(0, tm, CONV_ROWS):
            rows = pl.ds(r0, CONV_ROWS)
            ua_c, ub_c = ua_ref[rows, :], ub_ref[rows, :]
            dua = jnp.zeros((CONV_ROWS, d), F32)
            for k in range(ka):
                xk = _window(e_dva, sh_a, plan_a, off_a[k], r0)
                dua = dua + _tap(wa8, k) * xk
                acc_wa[pl.ds(k * SUBLANES, SUBLANES), :] += _fold8(ua_c * xk)
            dub = jnp.zeros((CONV_ROWS, d), F32)
            for k in range(kb):
                xk = _window(e_dcb, sh_b, plan_b, off_b[k], r0)
                dub = dub + _tap(wb8, k) * xk
                acc_wb[pl.ds(k * SUBLANES, SUBLANES), :] += _fold8(ub_c * xk)
            cgv, hav = cg_ref[rows, :].astype(F32), ha_ref[rows, :].astype(F32)
            av, sgv = a_ref[rows, :].astype(F32), sg_ref[rows, :].astype(F32)
            put(0, rows, dbg_ref[rows, :].astype(F32))
            put(1, rows, dua * hav)
            put(2, rows, dua * cgv)
            put(3, rows, dub * sgv)
            put(4, rows, dub * av * sgv * (1.0 - sgv))
            put(5, rows, dza_ref[rows, :].astype(F32))
            put(6, rows, dzb_ref[rows, :].astype(F32))

        @pl.when(i == n_steps - 1)
        def _():
            for k in range(ka):
                dwa_ref[k:k + 1, :] = jnp.sum(acc_wa[pl.ds(k * SUBLANES, SUBLANES), :], axis=0, keepdims=True)
            for k in range(kb):
                dwb_ref[k:k + 1, :] = jnp.sum(acc_wb[pl.ds(k * SUBLANES, SUBLANES), :], axis=0, keepdims=True)
            dbin_ref[...] = jnp.sum(acc_bin[...], axis=0, keepdims=True)

    halo_a = [_halo_prev(tm, HALO_A, d), _rows(tm, d), _halo_next(tm, HALO_A, d, t)]
    halo_b = [_halo_prev(tm, HALO_B, d), _rows(tm, d), _halo_next(tm, HALO_B, d, t)]
    return _call(
        body, name="conv_bwd", grid=(n_steps,),
        in_specs=halo_a + halo_b + [_rows(tm, d)] * 9 + [_const((ka, d)), _const((kb, d))],
        out_specs=[_rows(tm, 7 * d), _const((ka, d)), _const((kb, d)), _const((1, 7 * d))],
        out_shape=[jax.ShapeDtypeStruct((t, 7 * d), BF16), jax.ShapeDtypeStruct((ka, d), F32),
                   jax.ShapeDtypeStruct((kb, d), F32), jax.ShapeDtypeStruct((1, 7 * d), F32)],
        scratch_shapes=[pltpu.VMEM((tm + 2 * HALO_A, d), F32), pltpu.VMEM((tm + 2 * HALO_B, d), F32),
                        pltpu.VMEM((len(plan_a), _shifted_rows(tm, off_a), d), F32),
                        pltpu.VMEM((len(plan_b), _shifted_rows(tm, off_b), d), F32),
                        pltpu.VMEM((ka * SUBLANES, d), F32), pltpu.VMEM((kb * SUBLANES, d), F32),
                        pltpu.VMEM((ka * SUBLANES, d), F32), pltpu.VMEM((kb * SUBLANES, d), F32),
                        pltpu.VMEM((SUBLANES, 7 * d), F32)],
        args=(dva, dva, dva, dcb, dcb, dcb, ua, ub, cg, ha, a, sg, dbg, dza, dzb,
              conv_a_w, conv_b_w), ex=ex)


def _dx_call(dproj, x, dx1, g1pre, w_in_g, tm, first, n_steps, prev, ex=None):
    t, d = x.shape
    nb, _, n4 = w_in_g.shape
    ni = nb * n4
    rows = lambda width: pl.BlockSpec((tm, width), lambda i: (i + first, 0))
    if prev is None:
        prev = (jnp.zeros((SUBLANES, 128), F32), jnp.zeros((1, d), F32))
    prev_dx, prev_dg = prev

    def body(dp_ref, x_ref, dx1_ref, g_ref, w_hbm, prev_dx_hbm, prev_dg_ref, dx_ref, dg_ref, w_v, dh_s, acc_g, sem):
        _load_blocks_once(w_hbm, w_v, sem)

        @pl.when(pl.program_id(0) == 0)
        def _():
            acc_g[...] = jnp.zeros_like(acc_g)
            acc_g[0:1, :] = prev_dg_ref[...]

        dh_s[...] = lax.dot_general(dp_ref[...], w_v[...], NT_DIMS, preferred_element_type=F32)

        def dnorm(rows):
            dh = dh_s[rows, :]
            xv = x_ref[rows, :]
            r = lax.rsqrt(_mean_lanes(xv * xv) + RMS_EPS)
            gd = dh * g_ref[...]
            dx_ref[rows, :] = dx1_ref[rows, :] + r * gd - xv * (r * r * r * _mean_lanes(gd * xv))
            acc_g[...] += _fold8(dh * xv * r)
        _for_chunks(tm, ROW_CHUNK, dnorm)
        _write_row_sums(acc_g, dg_ref, n_steps)

    return _call(
        body, name="dx_bwd_from_%d" % first, grid=(n_steps,),
        in_specs=[rows(ni), rows(d), rows(d), _const((1, d)), ANY, ANY, _const((1, d))],
        out_specs=[rows(d), _const((1, d))],
        out_shape=[jax.ShapeDtypeStruct((t, d), F32), jax.ShapeDtypeStruct((1, d), F32)],
        scratch_shapes=[pltpu.VMEM((d, ni), BF16), pltpu.VMEM((tm, d), F32),
                        pltpu.VMEM((SUBLANES, d), F32), pltpu.SemaphoreType.DMA((nb,))],
        args=(dproj, x, dx1, g1pre, w_in_g, prev_dx, prev_dg), ex=ex,
        aliases={5: 0} if first > 0 else None)


def _tn_matmul(a, g, nblk, a_cols, g_cols, a_blocked, g_blocked, tt, name, ex=None):
    t = a.shape[0]

    def body(a_ref, g_ref, o_ref):
        @pl.when(pl.program_id(1) == 0)
        def _():
            o_ref[...] = jnp.zeros_like(o_ref)
        o_ref[0] += lax.dot_general(a_ref[...], g_ref[...], TN_DIMS, preferred_element_type=F32)

    (out,), xouts = _call(
        body, name=name, grid=(nblk, t // tt),
        in_specs=[pl.BlockSpec((tt, a_cols), (lambda b, s: (s, b)) if a_blocked else (lambda b, s: (s, 0))),
                  pl.BlockSpec((tt, g_cols), (lambda b, s: (s, b)) if g_blocked else (lambda b, s: (s, 0)))],
        out_specs=[pl.BlockSpec((1, a_cols, g_cols), lambda b, s: (b, 0, 0))],
        out_shape=[jax.ShapeDtypeStruct((nblk, a_cols, g_cols), F32)],
        scratch_shapes=[], args=(a, g), ex=ex)
    return out, xouts


def _tn_matmuls(pairs, tt, name):
    t, d = pairs[0][0].shape
    k = len(pairs)

    def body(*refs):
        ins, outs = refs[:2 * k], refs[2 * k:]

        @pl.when(pl.program_id(0) == 0)
        def _():
            for o_ref in outs:
                o_ref[...] = jnp.zeros_like(o_ref)
        for j in range(k):
            outs[j][...] += lax.dot_general(ins[2 * j][...], ins[2 * j + 1][...], TN_DIMS, preferred_element_type=F32)

    outs, _ = _call(
        body, name=name, grid=(t // tt,), in_specs=[_rows(tt, d)] * (2 * k), out_specs=[_const((d, d))] * k,
        out_shape=[jax.ShapeDtypeStruct((d, d), F32)] * k, scratch_shapes=[],
        args=[m for pair in pairs for m in pair])
    return outs


def _pair_sum_call(g_full, from_sibling, core, name):
    nblk, r, c = g_full.shape
    hr = r // 2
    tr = min(hr, 256)
    n = hr // tr

    def body(core_ref, g_ref, p_ref, o_ref):
        o_ref[...] = (g_ref[...] + p_ref[...]).astype(BF16)

    return pl.pallas_call(
        body, name=name,
        grid_spec=pltpu.PrefetchScalarGridSpec(
            num_scalar_prefetch=1, grid=(nblk, n),
            in_specs=[pl.BlockSpec((1, tr, c), lambda j, i, cr: (j, cr[0] * n + i, 0)),
                      pl.BlockSpec((1, tr, c), lambda j, i, cr: (j, i, 0))],
            out_specs=pl.BlockSpec((1, tr, c), lambda j, i, cr: (j, i, 0))),
        out_shape=jax.ShapeDtypeStruct((nblk, hr, c), BF16),
        compiler_params=_params("parallel", "parallel"))(core, g_full, from_sibling)


def _chip_sum_call(pair, received, chip_core, name):
    _, hr, c = pair.shape
    tr = min(hr, 256)
    n = hr // tr

    def body(cc_ref, own_ref, r_ref, o_ref):
        o_ref[...] = ((own_ref[0].astype(F32) + r_ref[0].astype(F32)) + r_ref[1].astype(F32)) + r_ref[2].astype(F32)

    return pl.pallas_call(
        body, name=name,
        grid_spec=pltpu.PrefetchScalarGridSpec(
            num_scalar_prefetch=1, grid=(n,),
            in_specs=[pl.BlockSpec((1, tr, c), lambda i, cc: (cc[0], i, 0)),
                      pl.BlockSpec((N_CHIPS - 1, tr, c), lambda i, cc: (0, i, 0))],
            out_specs=pl.BlockSpec((tr, c), lambda i, cc: (cc[1] * n + i, 0))),
        out_shape=jax.ShapeDtypeStruct((2 * hr, c), F32),
        compiler_params=_params("parallel"))(chip_core, pair, received)


def _adamw(w, g, m, v):
    m = ADAM_B1 * m + (1.0 - ADAM_B1) * g
    v = ADAM_B2 * v + (1.0 - ADAM_B2) * (g * g)
    m_hat = m / (1.0 - ADAM_B1 ** ADAM_STEP)
    v_hat = v / (1.0 - ADAM_B2 ** ADAM_STEP)
    delta = -ADAM_LR * (m_hat / (jnp.sqrt(v_hat) + ADAM_EPS) + ADAM_WD * w)
    return delta, m, v


def _adam_call(w, g, m, v, name):
    r, c = w.shape
    tr = min(r, 256)

    def body(w_ref, g_ref, m_ref, v_ref, go_ref, d_ref, mo_ref, vo_ref):
        go_ref[...] = g_ref[...]
        d_ref[...], mo_ref[...], vo_ref[...] = _adamw(w_ref[...], g_ref[...], m_ref[...], v_ref[...])

    shape = jax.ShapeDtypeStruct((r, c), F32)
    return pl.pallas_call(
        body, name=name, grid=(r // tr,), in_specs=[_rows(tr, c)] * 4, out_specs=[_rows(tr, c)] * 4,
        out_shape=[shape] * 4, compiler_params=_params("parallel"))(w, g, m, v)


def _gather_conv_weights(conv_a_w, conv_b_w, d):
    ka, dq = conv_a_w.shape
    kb = conv_b_w.shape[0]
    ra = -(-ka // SUBLANES) * SUBLANES
    rb = -(-kb // SUBLANES) * SUBLANES
    a_pad = jnp.pad(conv_a_w, ((0, ra - ka), (0, 0)))
    b_pad = jnp.pad(conv_b_w, ((0, rb - kb), (0, 0)))

    def body(a_ref, b_ref, oa_ref, ob_ref, pack, slots, send_sem, recv_sem):
        x, y, c = _place()
        me = 2 * x + y
        chips = _other_chips(x, y)
        pack[pl.ds(0, ra), :] = a_ref[...]
        pack[pl.ds(ra, rb), :] = b_ref[...]
        copies = []
        for j, (px, py, _) in enumerate(chips):
            cp = pltpu.make_async_remote_copy(
                src_ref=pack, dst_ref=slots.at[me], send_sem=send_sem.at[j], recv_sem=recv_sem.at[j],
                device_id=(px, py, c), device_id_type=MESH)
            cp.start()
            copies.append(cp)
        for j, (px, py, pk) in enumerate(chips):
            pltpu.make_async_remote_copy(
                src_ref=pack, dst_ref=slots.at[pk], send_sem=send_sem.at[j], recv_sem=recv_sem.at[j],
                device_id=(px, py, c), device_id_type=MESH).wait_recv()
        for cp in copies:
            cp.wait_send()
        slots[me] = pack[...]
        for k in range(N_CHIPS):
            oa_ref[:, pl.ds(k * dq, dq)] = slots[k, pl.ds(0, ra), :]
            ob_ref[:, pl.ds(k * dq, dq)] = slots[k, pl.ds(ra, rb), :]

    oa, ob = pl.pallas_call(
        body, name="gather_conv_weights", in_specs=[VMEM_FULL] * 2, out_specs=[VMEM_FULL] * 2,
        out_shape=[jax.ShapeDtypeStruct((ra, d), F32), jax.ShapeDtypeStruct((rb, d), F32)],
        scratch_shapes=[pltpu.VMEM((ra + rb, dq), F32), pltpu.VMEM((N_CHIPS, ra + rb, dq), F32),
                        pltpu.SemaphoreType.DMA((N_CHIPS - 1,)), pltpu.SemaphoreType.DMA((N_CHIPS - 1,))],
        compiler_params=pltpu.CompilerParams(has_side_effects=True))(a_pad, b_pad)
    return oa[:ka], ob[:kb]


def _small_step_call(partials, loss_rows, weights, m_s, v_s, sharded, d):
    n = len(partials)
    row_counts = [p.shape[0] for p in partials]
    starts = [sum(row_counts[:i]) for i in range(n)]
    loss_row = sum(row_counts)
    pack_rows = -(-(loss_row + 1) // SUBLANES) * SUBLANES
    dq = d // N_CHIPS

    def body(*refs):
        p_refs = refs[:n]
        loss_in = refs[n]
        w_refs = refs[n + 1:2 * n + 1]
        m_refs = refs[2 * n + 1:3 * n + 1]
        v_refs = refs[3 * n + 1:4 * n + 1]
        o = 4 * n + 1
        g_out = refs[o:o + n]
        d_out = refs[o + n:o + 2 * n]
        m_out = refs[o + 2 * n:o + 3 * n]
        v_out = refs[o + 3 * n:o + 4 * n]
        loss_out = refs[o + 4 * n]
        pack, from_sibling, slots, send_sem, recv_sem = refs[o + 4 * n + 1:]
        x, y, c = _place()
        me = 2 * x + y

        pack[...] = jnp.zeros_like(pack)
        for i in range(n):
            pack[pl.ds(starts[i], row_counts[i]), :] = p_refs[i][...]
        pack[pl.ds(loss_row, 1), :] = loss_in[...]

        pair = _remote(pack, from_sibling, send_sem.at[0], recv_sem.at[0], (x, y, 1 - c))
        pair.start()
        pair.wait()
        pack[...] = pack[...] + from_sibling[...]
        chips = _other_chips(x, y)
        copies = [_remote(pack, slots.at[me], send_sem.at[1 + j], recv_sem.at[1 + j], (px, py, c))
                  for j, (px, py, _) in enumerate(chips)]
        for cp in copies:
            cp.start()
        for j, (px, py, pk) in enumerate(chips):
            _remote(pack, slots.at[pk], send_sem.at[1 + j], recv_sem.at[1 + j], (px, py, c)).wait_recv()
        for cp in copies:
            cp.wait_send()

        slots[me] = pack[...]
        total = slots[0]
        for k in range(1, N_CHIPS):
            total = total + slots[k]
        pack[...] = total

        loss_out[...] = jnp.broadcast_to(
            (0.5 / d) * jnp.sum(pack[pl.ds(loss_row, 1), :], axis=-1, keepdims=True), loss_out.shape)
        chip = 2 * x + y
        for i in range(n):
            rows = pl.ds(starts[i], row_counts[i])
            if sharded[i]:
                for k in range(N_CHIPS):
                    @pl.when(chip == k)
                    def _():
                        g_out[i][...] = pack[rows, pl.ds(k * dq, dq)]
            else:
                g_out[i][...] = pack[rows, :]
            d_out[i][...], m_out[i][...], v_out[i][...] = _adamw(
                w_refs[i][...], g_out[i][...], m_refs[i][...], v_refs[i][...])

    w_shapes = [jax.ShapeDtypeStruct(w.shape, F32) for w in weights]
    n_in = 4 * n + 1
    return pl.pallas_call(
        body, name="small_grads_allreduce_adamw",
        in_specs=[VMEM_FULL] * n_in, out_specs=[VMEM_FULL] * (4 * n + 1),
        out_shape=w_shapes * 4 + [jax.ShapeDtypeStruct((SUBLANES, 128), F32)],
        scratch_shapes=[pltpu.VMEM((pack_rows, d), F32), pltpu.VMEM((pack_rows, d), F32),
                        pltpu.VMEM((N_CHIPS, pack_rows, d), F32),
                        pltpu.SemaphoreType.DMA((N_CHIPS,)), pltpu.SemaphoreType.DMA((N_CHIPS,))],
        compiler_params=pltpu.CompilerParams(has_side_effects=True, vmem_limit_bytes=VMEM_LIMIT))(
            *partials, loss_rows, *weights, *m_s, *v_s)


def _tile(t, want):
    return min(t, want)


def kernel(x, norm1_pre_g, w_in, b_in, conv_a_w, conv_a_b, w_a_out, conv_b_w, conv_b_b, ln_b_g, ln_b_b, w_b_out, w_o, norm1_post_g, norm2_pre_g, w_mlp_in, w_mlp_out, norm2_post_g, loss_target, m_norm1_pre_g, m_w_in, m_b_in, m_conv_a_w, m_conv_a_b, m_w_a_out, m_conv_b_w, m_conv_b_b, m_ln_b_g, m_ln_b_b, m_w_b_out, m_w_o, m_norm1_post_g, m_norm2_pre_g, m_w_mlp_in, m_w_mlp_out, m_norm2_post_g, v_norm1_pre_g, v_w_in, v_b_in, v_conv_a_w, v_conv_a_b, v_w_a_out, v_conv_b_w, v_conv_b_b, v_ln_b_g, v_ln_b_b, v_w_b_out, v_w_o, v_norm1_post_g, v_norm2_pre_g, v_w_mlp_in, v_w_mlp_out, v_norm2_post_g):
    _, t, d = x.shape
    xt = x.reshape(t, d)
    tgt = loss_target.reshape(t, d)
    row = lambda vec: vec.reshape(1, -1)
    cx, cy, cc = _place()
    core = cc.astype(jnp.int32).reshape(1)
    chip = (2 * cx + cy).astype(jnp.int32).reshape(1)

    big = dict(w_in=w_in, w_a_out=w_a_out, w_b_out=w_b_out, w_o=w_o, w_mlp_in=w_mlp_in, w_mlp_out=w_mlp_out)
    names = list(big)
    chip_core = jnp.concatenate([chip, core])
    slot = {k: _cast_to_slot(big[k], chip, "cast_" + k) for k in names}
    mixer_w, mlp_w = ["w_a_out", "w_b_out", "w_o"], ["w_mlp_in", "w_mlp_out"]
    rows_of = lambda buf: buf.reshape(-1, buf.shape[-1])

    def pair_sums(keys, full, from_sibling):
        return [_pair_sum_call(g, p, core, "pair_sum_" + k) for k, g, p in zip(keys, full, from_sibling)]

    def chip_sums(keys, pairs, received):
        return [_chip_sum_call(p, r, chip_core, "chip_sum_" + k) for k, p, r in zip(keys, pairs, received)]

    (w_in_g,) = _exchange_call("gather_w_in", [_ex_gather_ici([slot["w_in"]]), _ex_gather_forward([slot["w_in"]])])

    g1pre, g1post, g2pre, g2post = row(norm1_pre_g), row(norm1_post_g), row(norm2_pre_g), row(norm2_post_g)
    lng, lnb, ba, bb = row(ln_b_g), row(ln_b_b), row(conv_a_b), row(conv_b_b)
    conv_a_full, conv_b_full = _gather_conv_weights(conv_a_w, conv_b_w, d)

    (h, ua, ub, bg, cg, ha, a, sg, sa, sb), landed = _proj_call(
        xt, g1pre, w_in_g, row(b_in), _tile(t, 512), ex=_ex_gather_ici([slot[k] for k in mixer_w + mlp_w]))
    w_a_g, w_b_g, w_o_g = _exchange_call("forward_mixer_weights", [_ex_gather_forward(landed[:3])])
    w_a_full, w_b_full, w_o_full = rows_of(w_a_g), rows_of(w_b_g), rows_of(w_o_g)
    (x1, va, pa, cb, sbo, ya, yb, mg, mix), (w1_g, w2_g) = _mixer_fwd_call(
        ua, ub, bg, sa, sb, xt, conv_a_full, ba, conv_b_full, bb, lng, lnb,
        w_a_full, w_b_full, w_o_full, g1post, _tile(t, 256), ex=_ex_gather_forward(landed[3:]))
    (dx1, f, df2, h2, df1, dmix, dg2post, dg2pre, dg1post, loss_rows), _ = _mlp_call(
        x1, tgt, mix, g2pre, g2post, g1post, w1_g, rows_of(w2_g), _tile(t, 256))

    tt = _tile(t, 2048)
    n4, fq, dq = w_in.shape[1], w_mlp_in.shape[1], d // N_CHIPS
    g_mlp = [_tn_matmul(h2, df1, N_CHIPS, d, fq, False, True, tt, "dw_mlp_in")[0],
             _tn_matmul(f, df2, N_CHIPS, fq, d, True, False, tt, "dw_mlp_out")[0]]
    (dya, dyb, dva, dcb, dbg, dza, dzb, dlng, dlnb, dba, dbb), sib_mlp = _mixer_bwd_call(
        dmix, sa, sb, ya, yb, bg, va, cb, lng, lnb, w_a_full, w_b_full, w_o_full, _tile(t, 512),
        ex=_ex_sibling_halves(g_mlp))
    p_mlp = pair_sums(mlp_w, g_mlp, sib_mlp)
    g_mix = [g.reshape(N_CHIPS, dq, d)
             for g in _tn_matmuls([(pa, dya), (sbo, dyb), (mg, dmix)], _tile(t, 1024), "dw_mixer")]
    ex_a, ex_b = _ex_scatter_to_owner(p_mlp), _ex_sibling_halves(g_mix)
    (dproj, dwa_conv, dwb_conv, dbin), xo = _conv_bwd_call(
        dva, dcb, ua, ub, cg, ha, a, sg, dbg, dza, dzb, conv_a_full, conv_b_full, _tile(t, 256),
        ex=_merge(ex_a, ex_b))
    recv_mlp, sib_mix = _split(xo, ex_a, ex_b)
    r_mlp = chip_sums(mlp_w, p_mlp, recv_mlp)
    p_mix = pair_sums(mixer_w, g_mix, sib_mix)
    ex_a, ex_b = _ex_share_halves(r_mlp), _ex_scatter_to_owner(p_mix)
    g_in, xo = _tn_matmul(h, dproj, N_CHIPS, d, n4, False, True, tt, "dw_in", ex=_merge(ex_a, ex_b))
    red_mlp, recv_mix = _split(xo, ex_a, ex_b)
    r_mix = chip_sums(mixer_w, p_mix, recv_mix)
    ex_a, ex_b = _ex_sibling_halves([g_in]), _ex_share_halves(r_mix)
    tm_dx = _tile(t, 512)
    n_dx = t // tm_dx
    n_a = max(1, (3 * n_dx) // 8)
    dx_done, xo = _dx_call(dproj, xt, dx1, g1pre, w_in_g, tm_dx, 0, n_a, None, ex=_merge(ex_a, ex_b))
    sib_in, red_mix = _split(xo, ex_a, ex_b)
    p_in = pair_sums(["w_in"], [g_in], sib_in)
    (grad_x, dg1pre), recv_in = _dx_call(dproj, xt, dx1, g1pre, w_in_g, tm_dx, n_a, n_dx - n_a, dx_done,
                                         ex=_ex_scatter_to_owner(p_in))
    r_in = chip_sums(["w_in"], p_in, recv_in)
    red_in = _exchange_call("w_in_grad_to_sibling", [_ex_share_halves(r_in)])
    reduced = dict(zip(mlp_w + mixer_w + ["w_in"], red_mlp + red_mix + red_in))

    moments = dict(w_in=(m_w_in, v_w_in), w_a_out=(m_w_a_out, v_w_a_out), w_b_out=(m_w_b_out, v_w_b_out),
                   w_o=(m_w_o, v_w_o), w_mlp_in=(m_w_mlp_in, v_w_mlp_in), w_mlp_out=(m_w_mlp_out, v_w_mlp_out))
    out = {}
    for k in names:
        out[k] = tuple(_adam_call(big[k], reduced[k], *moments[k], "adamw_" + k))

    small = [
        ("conv_b_w", dwb_conv, conv_b_w, m_conv_b_w, v_conv_b_w, True),
        ("conv_b_b", dbb, bb, row(m_conv_b_b), row(v_conv_b_b), False),
        ("b_in", dbin.reshape(7, d), b_in.reshape(7, d), m_b_in.reshape(7, d), v_b_in.reshape(7, d), False),
        ("norm1_pre_g", dg1pre, row(norm1_pre_g), row(m_norm1_pre_g), row(v_norm1_pre_g), False),
        ("conv_a_w", dwa_conv, conv_a_w, m_conv_a_w, v_conv_a_w, True),
        ("conv_a_b", dba, ba, row(m_conv_a_b), row(v_conv_a_b), False),
        ("ln_b_g", dlng, lng, row(m_ln_b_g), row(v_ln_b_g), False),
        ("ln_b_b", dlnb, lnb, row(m_ln_b_b), row(v_ln_b_b), False),
        ("norm1_post_g", dg1post, g1post, row(m_norm1_post_g), row(v_norm1_post_g), False),
        ("norm2_pre_g", dg2pre, g2pre, row(m_norm2_pre_g), row(v_norm2_pre_g), False),
        ("norm2_post_g", dg2post, g2post, row(m_norm2_post_g), row(v_norm2_post_g), False),
    ]
    res = _small_step_call([s[1] for s in small], loss_rows, [s[2] for s in small], [s[3] for s in small],
                           [s[4] for s in small], [s[5] for s in small], d)
    ns = len(small)
    loss = res[4 * ns][0, 0]
    shapes = dict(norm1_pre_g=norm1_pre_g.shape, b_in=b_in.shape, conv_a_w=conv_a_w.shape,
                  conv_a_b=conv_a_b.shape, conv_b_w=conv_b_w.shape, conv_b_b=conv_b_b.shape,
                  ln_b_g=ln_b_g.shape, ln_b_b=ln_b_b.shape, norm1_post_g=norm1_post_g.shape,
                  norm2_pre_g=norm2_pre_g.shape, norm2_post_g=norm2_post_g.shape)
    for i, s in enumerate(small):
        out[s[0]] = tuple(res[q * ns + i].reshape(shapes[s[0]]) for q in range(4))

    order = ["norm1_pre_g", "w_in", "b_in", "conv_a_w", "conv_a_b", "w_a_out", "conv_b_w", "conv_b_b",
             "ln_b_g", "ln_b_b", "w_b_out", "w_o", "norm1_post_g", "norm2_pre_g", "w_mlp_in", "w_mlp_out",
             "norm2_post_g"]
    return (loss, grad_x.reshape(x.shape), *[out[k][0] for k in order], *[out[k][1] for k in order],
            *[out[k][2] for k in order], *[out[k][3] for k in order])
```

```python
import functools

import jax
import jax.numpy as jnp
from jax import lax
from jax.experimental import pallas as pl
from jax.experimental.pallas import tpu as pltpu
from jax.experimental.pallas import tpu_sc as plsc

RMS_EPS = 1e-6
LN_EPS = 1e-5
ADAM_LR = 0.001
ADAM_B1 = 0.9
ADAM_B2 = 0.999
ADAM_EPS = 1e-08
ADAM_WD = 0.01
ADAM_STEP = 10

F32 = jnp.float32
BF16 = jnp.bfloat16
MESH = pl.DeviceIdType.MESH
ANY = pl.BlockSpec(memory_space=pl.ANY)
VMEM_FULL = pl.BlockSpec(memory_space=pltpu.VMEM)

V7X_VMEM_BYTES = 64 * 1024 * 1024
VMEM_LIMIT = V7X_VMEM_BYTES - 8 * 1024 * 1024
SUBLANES = 8
N_CHIPS = 4
N_DEV = 8
SC_TILES = 32
SC_LANES = 16
SC_ROWS = 16
HALO_A = 8
HALO_B = 16
CONV_ROWS = 16
ROW_CHUNK = 32

NT_DIMS = (((1,), (1,)), ((), ()))
TN_DIMS = (((0,), (0,)), ((), ()))


def _params(*sem):
    return pltpu.CompilerParams(dimension_semantics=sem, vmem_limit_bytes=VMEM_LIMIT)


def _rows(tm, d):
    return pl.BlockSpec((tm, d), lambda i: (i, 0))


def _const(shape):
    return pl.BlockSpec(shape, lambda i: (0,) * len(shape))


def _halo_prev(tm, hb, d):
    return pl.BlockSpec((hb, d), lambda i: (jnp.maximum(i * (tm // hb) - 1, 0), 0))


def _halo_next(tm, hb, d, t):
    return pl.BlockSpec((hb, d), lambda i: (jnp.minimum((i + 1) * (tm // hb), t // hb - 1), 0))


def _for_chunks(n_rows, rc, fn):
    for r0 in range(0, n_rows, rc):
        fn(pl.ds(r0, rc))


def _fold8(v):
    return v.reshape(v.shape[0] // SUBLANES, SUBLANES, v.shape[1]).sum(axis=0)


def _mean_lanes(v):
    return jnp.mean(v, axis=-1, keepdims=True)


def _load_blocks_once(w_hbm, w_vmem, sem):
    nb, _, n = w_hbm.shape

    @pl.when(pl.program_id(0) == 0)
    def _():
        copies = [pltpu.make_async_copy(w_hbm.at[j], w_vmem.at[:, pl.ds(j * n, n)], sem.at[j])
                  for j in range(nb)]
        for cp in copies:
            cp.start()
        for cp in copies:
            cp.wait()


def _load_once(w_hbm, w_vmem, sem):
    @pl.when(pl.program_id(0) == 0)
    def _():
        cp = pltpu.make_async_copy(w_hbm, w_vmem, sem)
        cp.start()
        cp.wait()


def _write_row_sums(acc_ref, out_ref, n_steps):
    @pl.when(pl.program_id(0) == n_steps - 1)
    def _():
        out_ref[...] = jnp.sum(acc_ref[...], axis=0, keepdims=True)


def _place():
    return lax.axis_index("x"), lax.axis_index("y"), lax.axis_index("c")


def _other_chips(x, y):
    rel = [(x, 1 - y), (1 - x, y), (1 - x, 1 - y)]
    return [(px, py, 2 * px + py) for px, py in rel]


class _Exchange:
    def __init__(self, inputs, out_shapes, aliases, n_sems, copies):
        self.inputs = list(inputs)
        self.out_shapes = list(out_shapes)
        self.aliases = dict(aliases)
        self.n_sems = n_sems
        self.copies = copies


def _remote(src, dst, send, recv, device):
    return pltpu.make_async_remote_copy(src_ref=src, dst_ref=dst, send_sem=send, recv_sem=recv,
                                        device_id=device, device_id_type=MESH)


def _sds(a):
    return jax.ShapeDtypeStruct(a.shape, a.dtype)


def _ex_gather_ici(bufs):
    n = len(bufs)

    def copies(xin, xout, send, recv):
        x, y, c = _place()
        me = 2 * x + y
        out = []
        for a in range(n):
            hr = xin[a].shape[1] // 2
            rows = pl.ds(c * hr, hr)
            for j, (px, py, _) in enumerate(_other_chips(x, y)):
                k = a * (N_CHIPS - 1) + j
                out.append(_remote(xin[a].at[me, rows, :], xout[a].at[me, rows, :], send(k), recv(k), (px, py, c)))
        return out

    return _Exchange(bufs, [_sds(b) for b in bufs], {a: a for a in range(n)}, n * (N_CHIPS - 1), copies)


def _ex_gather_forward(bufs):
    n = len(bufs)

    def copies(xin, xout, send, recv):
        x, y, c = _place()
        out = []
        for a in range(n):
            hr = xin[a].shape[1] // 2
            rows = pl.ds(c * hr, hr)
            for j, (_, _, pk) in enumerate(_other_chips(x, y)):
                k = a * (N_CHIPS - 1) + j
                out.append(_remote(xin[a].at[pk, rows, :], xout[a].at[pk, rows, :], send(k), recv(k), (x, y, 1 - c)))
        return out

    return _Exchange(bufs, [_sds(b) for b in bufs], {a: a for a in range(n)}, n * (N_CHIPS - 1), copies)


def _ex_sibling_halves(grads):
    n = len(grads)

    def copies(xin, xout, send, recv):
        x, y, c = _place()
        out = []
        for a in range(n):
            hr = xin[a].shape[1] // 2
            out.append(_remote(xin[a].at[:, pl.ds((1 - c) * hr, hr), :], xout[a], send(a), recv(a), (x, y, 1 - c)))
        return out

    shapes = [jax.ShapeDtypeStruct((g.shape[0], g.shape[1] // 2, g.shape[2]), g.dtype) for g in grads]
    return _Exchange(grads, shapes, {}, n, copies)


def _ex_scatter_to_owner(pairs):
    n = len(pairs)

    def copies(xin, xout, send, recv):
        x, y, c = _place()
        out = []
        for a in range(n):
            for j, (px, py, pk) in enumerate(_other_chips(x, y)):
                k = a * (N_CHIPS - 1) + j
                out.append(_remote(xin[a].at[pk], xout[a].at[j], send(k), recv(k), (px, py, c)))
        return out

    shapes = [jax.ShapeDtypeStruct((N_CHIPS - 1,) + p.shape[1:], p.dtype) for p in pairs]
    return _Exchange(pairs, shapes, {}, n * (N_CHIPS - 1), copies)


def _ex_share_halves(reduced):
    n = len(reduced)

    def copies(xin, xout, send, recv):
        x, y, c = _place()
        out = []
        for a in range(n):
            hr = xin[a].shape[0] // 2
            rows = pl.ds(c * hr, hr)
            out.append(_remote(xin[a].at[rows, :], xout[a].at[rows, :], send(a), recv(a), (x, y, 1 - c)))
        return out

    return _Exchange(reduced, [_sds(r) for r in reduced], {a: a for a in range(n)}, n, copies)


def _merge(*exs):
    exs = [e for e in exs if e is not None]
    if not exs:
        return None
    inputs, shapes, aliases = [], [], {}
    in_off, out_off, sem_off = [], [], []
    n_sems = 0
    for e in exs:
        in_off.append(len(inputs))
        out_off.append(len(shapes))
        sem_off.append(n_sems)
        aliases.update({len(inputs) + i: len(shapes) + o for i, o in e.aliases.items()})
        inputs += e.inputs
        shapes += e.out_shapes
        n_sems += e.n_sems

    def copies(xin, xout, send, recv):
        out = []
        for e, io, oo, so in zip(exs, in_off, out_off, sem_off):
            out += e.copies(xin[io:io + len(e.inputs)], xout[oo:oo + len(e.out_shapes)],
                            lambda i, so=so: send(so + i), lambda i, so=so: recv(so + i))
        return out

    return _Exchange(inputs, shapes, aliases, n_sems, copies)


def _split(ex_outs, *exs):
    parts, o = [], 0
    for e in exs:
        parts.append(list(ex_outs[o:o + len(e.out_shapes)]))
        o += len(e.out_shapes)
    return parts


def _call(body, *, name, grid, in_specs, out_specs, out_shape, scratch_shapes, args, ex=None, aliases=None):
    n_in, n_out, n_scr = len(in_specs), len(out_specs), len(scratch_shapes)
    seq = ("arbitrary",) * len(grid)
    aliases = dict(aliases or {})
    if ex is None:
        outs = pl.pallas_call(
            body, name=name, grid=grid, in_specs=list(in_specs), out_specs=list(out_specs),
            out_shape=list(out_shape), scratch_shapes=list(scratch_shapes), input_output_aliases=aliases,
            compiler_params=_params(*seq))(*args)
        return list(outs), []
    n_xi, n_xo = len(ex.inputs), len(ex.out_shapes)

    def full(*refs):
        ins, xin = refs[:n_in], refs[n_in:n_in + n_xi]
        o = n_in + n_xi
        outs, xout = refs[o:o + n_out], refs[o + n_out:o + n_out + n_xo]
        s = o + n_out + n_xo
        scr = refs[s:s + n_scr]
        send_sems, recv_sems = refs[s + n_scr:]
        send = lambda i: send_sems.at[i]
        recv = lambda i: recv_sems.at[i]
        first = functools.reduce(jnp.logical_and, [pl.program_id(a) == 0 for a in range(len(grid))])
        last = functools.reduce(jnp.logical_and, [pl.program_id(a) == grid[a] - 1 for a in range(len(grid))])

        @pl.when(first)
        def _():
            for cp in ex.copies(xin, xout, send, recv):
                cp.start()

        body(*ins, *outs, *scr)

        @pl.when(last)
        def _():
            for cp in ex.copies(xin, xout, send, recv):
                cp.wait()

    res = pl.pallas_call(
        full, name=name, grid=grid, in_specs=list(in_specs) + [ANY] * n_xi,
        out_specs=list(out_specs) + [ANY] * n_xo, out_shape=list(out_shape) + ex.out_shapes,
        scratch_shapes=list(scratch_shapes) + [pltpu.SemaphoreType.DMA((ex.n_sems,)),
                                               pltpu.SemaphoreType.DMA((ex.n_sems,))],
        input_output_aliases={**aliases, **{n_in + i: n_out + o for i, o in ex.aliases.items()}},
        compiler_params=pltpu.CompilerParams(dimension_semantics=seq, vmem_limit_bytes=VMEM_LIMIT,
                                             has_side_effects=True))(*args, *ex.inputs)
    return list(res[:n_out]), list(res[n_out:])


def _exchange_call(name, phases):
    first = phases[0]
    n_xi, n_xo = len(first.inputs), len(first.out_shapes)

    def body(*refs):
        xin, xout = refs[:n_xi], refs[n_xi:n_xi + n_xo]
        sems = refs[n_xi + n_xo:]
        for p, ex in enumerate(phases):
            send_sems, recv_sems = sems[2 * p], sems[2 * p + 1]
            cps = ex.copies(xin, xout, lambda i: send_sems.at[i], lambda i: recv_sems.at[i])
            for cp in cps:
                cp.start()
            for cp in cps:
                cp.wait()

    sems = []
    for ex in phases:
        sems += [pltpu.SemaphoreType.DMA((ex.n_sems,)), pltpu.SemaphoreType.DMA((ex.n_sems,))]
    return list(pl.pallas_call(
        body, name=name, in_specs=[ANY] * n_xi, out_specs=[ANY] * n_xo, out_shape=first.out_shapes,
        scratch_shapes=sems, input_output_aliases=dict(first.aliases),
        compiler_params=pltpu.CompilerParams(has_side_effects=True))(*first.inputs))


def _cast_to_slot(w, chip, name):
    r, c = w.shape
    tr = min(r, 256)

    def body(chip_ref, w_ref, o_ref):
        o_ref[0] = w_ref[...].astype(BF16)

    return pl.pallas_call(
        body, name=name,
        grid_spec=pltpu.PrefetchScalarGridSpec(
            num_scalar_prefetch=1, grid=(r // tr,),
            in_specs=[pl.BlockSpec((tr, c), lambda i, k: (i, 0))],
            out_specs=pl.BlockSpec((1, tr, c), lambda i, k: (k[0], i, 0))),
        out_shape=jax.ShapeDtypeStruct((N_CHIPS, r, c), BF16),
        compiler_params=_params("parallel"))(chip, w)


def _proj_call(x, g1pre, w_in_g, b_in, tm, ex=None):
    t, d = x.shape
    nb, _, n4 = w_in_g.shape
    ni = nb * n4
    assert ni == 7 * d

    def body(x_ref, g_ref, b_ref, w_hbm, h_ref, ua_ref, ub_ref, bg_ref, cg_ref, ha_ref, a_ref,
             sg_ref, sa_ref, sb_ref, w_v, p0, p1, sem):
        _load_blocks_once(w_hbm, w_v, sem)

        def norm(rows):
            xv = x_ref[rows, :]
            r = lax.rsqrt(_mean_lanes(xv * xv) + RMS_EPS)
            h_ref[rows, :] = (xv * r * g_ref[...]).astype(BF16)
        _for_chunks(tm, ROW_CHUNK, norm)

        def group(i, dst):
            cols = pl.ds(i * d, d)
            dst[...] = jnp.dot(h_ref[...], w_v[:, cols], preferred_element_type=F32) + b_ref[:, cols]

        group(0, p0)

        def bgate(rows):
            bg_ref[rows, :] = p0[rows, :].astype(BF16)
        _for_chunks(tm, ROW_CHUNK, bgate)

        group(1, p0)
        group(2, p1)

        def branch_a(rows):
            cg, ha = p0[rows, :], p1[rows, :]
            ua_ref[rows, :] = cg * ha
            cg_ref[rows, :] = cg.astype(BF16)
            ha_ref[rows, :] = ha.astype(BF16)
        _for_chunks(tm, ROW_CHUNK, branch_a)

        group(3, p0)
        group(4, p1)

        def branch_b(rows):
            a, sg = p0[rows, :], jax.nn.sigmoid(p1[rows, :])
            ub_ref[rows, :] = a * sg
            a_ref[rows, :] = a.astype(BF16)
            sg_ref[rows, :] = sg.astype(BF16)
        _for_chunks(tm, ROW_CHUNK, branch_b)

        group(5, p0)
        group(6, p1)

        def gates(rows):
            sa_ref[rows, :] = jax.nn.sigmoid(p0[rows, :]).astype(BF16)
            sb_ref[rows, :] = jax.nn.sigmoid(p1[rows, :]).astype(BF16)
        _for_chunks(tm, ROW_CHUNK, gates)

    bf = jax.ShapeDtypeStruct((t, d), BF16)
    f32 = jax.ShapeDtypeStruct((t, d), F32)
    return _call(
        body, name="proj_fwd", grid=(t // tm,),
        in_specs=[_rows(tm, d), _const((1, d)), _const((1, ni)), ANY],
        out_specs=[_rows(tm, d)] * 10,
        out_shape=[bf, f32, f32, bf, bf, bf, bf, bf, bf, bf],
        scratch_shapes=[pltpu.VMEM((d, ni), BF16), pltpu.VMEM((tm, d), F32), pltpu.VMEM((tm, d), F32),
                        pltpu.SemaphoreType.DMA((nb,))],
        args=(x, g1pre, b_in, w_in_g), ex=ex)


def _fill_ext(ext, prev_ref, cur_ref, next_ref, hb, tm, i, n_steps):
    ext[pl.ds(0, hb), :] = jnp.where(i > 0, prev_ref[...], 0.0)
    ext[pl.ds(hb, tm), :] = cur_ref[...]
    ext[pl.ds(hb + tm, hb), :] = jnp.where(i < n_steps - 1, next_ref[...], 0.0)


def _shift_plan(offsets):
    shifts = sorted({o % SUBLANES for o in offsets if o % SUBLANES})
    return {s: i for i, s in enumerate(shifts)}


def _shifted_rows(tm, offsets):
    return tm + SUBLANES * max(o // SUBLANES for o in offsets)


def _fill_shifted(ext, sh, plan):
    n = sh.shape[1]
    for s, i in plan.items():
        sh[i, :, :] = ext[pl.ds(s, n), :]


def _fill_tap_rows(w_ref, rows8):
    @pl.when(pl.program_id(0) == 0)
    def _():
        for k in range(w_ref.shape[0]):
            rows8[pl.ds(k * SUBLANES, SUBLANES), :] = jnp.broadcast_to(w_ref[k:k + 1, :], (SUBLANES, w_ref.shape[1]))


def _tap(rows8, k):
    w8 = rows8[pl.ds(k * SUBLANES, SUBLANES), :]
    return jnp.concatenate([w8] * (CONV_ROWS // SUBLANES), axis=0)


def _window(ext, sh, plan, offset, r0):
    q, s = divmod(offset, SUBLANES)
    if s == 0:
        return ext[pl.ds(offset + r0, CONV_ROWS), :]
    return sh[plan[s], pl.ds(SUBLANES * q + r0, CONV_ROWS), :]


def _mixer_fwd_call(ua, ub, bg, sa, sb, x, conv_a_w, conv_a_b, conv_b_w, conv_b_b, ln_g, ln_b,
                    w_a, w_b, w_o, g1post, tm, ex=None):
    t, d = x.shape
    n_steps = t // tm
    ka, kb = conv_a_w.shape[0], conv_b_w.shape[0]
    off_a = [HALO_A - (ka - 1) // 2 + k for k in range(ka)]
    off_b = [HALO_B - (kb - 1) // 2 + k for k in range(kb)]
    plan_a, plan_b = _shift_plan(off_a), _shift_plan(off_b)

    def body(uap, uac, uan, ubp, ubc, ubn, bg_ref, sa_ref, sb_ref, x_ref, wa_c, ba_c, wb_c, bb_c,
             lng, lnb, wa_hbm, wb_hbm, wo_hbm, g_ref,
             x1_ref, va_ref, pa_ref, cb_ref, sbo_ref, ya_ref, yb_ref, mg_ref, mix_ref,
             ext_a, ext_b, sh_a, sh_b, wa8, wb8, wa_v, wb_v, wo_v, y0, y1, sem):
        i = pl.program_id(0)
        _fill_tap_rows(wa_c, wa8)
        _fill_tap_rows(wb_c, wb8)
        _load_once(wa_hbm, wa_v, sem.at[0])
        _load_once(wb_hbm, wb_v, sem.at[1])
        _load_once(wo_hbm, wo_v, sem.at[2])
        _fill_ext(ext_a, uap, uac, uan, HALO_A, tm, i, n_steps)
        _fill_ext(ext_b, ubp, ubc, ubn, HALO_B, tm, i, n_steps)
        _fill_shifted(ext_a, sh_a, plan_a)
        _fill_shifted(ext_b, sh_b, plan_b)

        for r0 in range(0, tm, CONV_ROWS):
            rows = pl.ds(r0, CONV_ROWS)
            va = jnp.broadcast_to(ba_c[...], (CONV_ROWS, d))
            for k in range(ka):
                va = va + _tap(wa8, k) * _window(ext_a, sh_a, plan_a, off_a[k], r0)
            va_ref[rows, :] = va.astype(BF16)
            pa_ref[rows, :] = (bg_ref[rows, :].astype(F32) * va).astype(BF16)
            cb = jnp.broadcast_to(bb_c[...], (CONV_ROWS, d))
            for k in range(kb):
                cb = cb + _tap(wb8, k) * _window(ext_b, sh_b, plan_b, off_b[k], r0)
            cb_ref[rows, :] = cb
            mu = _mean_lanes(cb)
            cen = cb - mu
            rstd = lax.rsqrt(_mean_lanes(cen * cen) + LN_EPS)
            ln = cen * rstd * lng[...] + lnb[...]
            sbo_ref[rows, :] = (ln * jax.nn.sigmoid(ln)).astype(BF16)

        y0[...] = jnp.dot(pa_ref[...], wa_v[...], preferred_element_type=F32)
        y1[...] = jnp.dot(sbo_ref[...], wb_v[...], preferred_element_type=F32)

        def merge(rows):
            ya, yb = y0[rows, :], y1[rows, :]
            ya_ref[rows, :] = ya.astype(BF16)
            yb_ref[rows, :] = yb.astype(BF16)
            mg_ref[rows, :] = (sa_ref[rows, :].astype(F32) * ya + sb_ref[rows, :].astype(F32) * yb).astype(BF16)
        _for_chunks(tm, ROW_CHUNK, merge)

        mix_ref[...] = jnp.dot(mg_ref[...], wo_v[...], preferred_element_type=F32)

        def resid(rows):
            mix = mix_ref[rows, :]
            r = lax.rsqrt(_mean_lanes(mix * mix) + RMS_EPS)
            x1_ref[rows, :] = x_ref[rows, :] + mix * r * g_ref[...]
        _for_chunks(tm, ROW_CHUNK, resid)

    bf = jax.ShapeDtypeStruct((t, d), BF16)
    f32 = jax.ShapeDtypeStruct((t, d), F32)
    return _call(
        body, name="mixer_fwd", grid=(n_steps,),
        in_specs=[_halo_prev(tm, HALO_A, d), _rows(tm, d), _halo_next(tm, HALO_A, d, t),
                  _halo_prev(tm, HALO_B, d), _rows(tm, d), _halo_next(tm, HALO_B, d, t),
                  _rows(tm, d), _rows(tm, d), _rows(tm, d), _rows(tm, d),
                  _const((ka, d)), _const((1, d)), _const((kb, d)), _const((1, d)),
                  _const((1, d)), _const((1, d)), ANY, ANY, ANY, _const((1, d))],
        out_specs=[_rows(tm, d)] * 9,
        out_shape=[f32, bf, bf, f32, bf, bf, bf, bf, f32],
        scratch_shapes=[pltpu.VMEM((tm + 2 * HALO_A, d), F32), pltpu.VMEM((tm + 2 * HALO_B, d), F32),
                        pltpu.VMEM((len(plan_a), _shifted_rows(tm, off_a), d), F32),
                        pltpu.VMEM((len(plan_b), _shifted_rows(tm, off_b), d), F32),
                        pltpu.VMEM((ka * SUBLANES, d), F32), pltpu.VMEM((kb * SUBLANES, d), F32),
                        pltpu.VMEM((d, d), BF16), pltpu.VMEM((d, d), BF16), pltpu.VMEM((d, d), BF16),
                        pltpu.VMEM((tm, d), F32), pltpu.VMEM((tm, d), F32),
                        pltpu.SemaphoreType.DMA((3,))],
        args=(ua, ua, ua, ub, ub, ub, bg, sa, sb, x, conv_a_w, conv_a_b, conv_b_w, conv_b_b,
              ln_g, ln_b, w_a, w_b, w_o, g1post), ex=ex)


def _mlp_call(x1, target, mix, g2pre, g2post, g1post, w1_g, w2, tm, ex=None):
    t, d = x1.shape
    nb, _, fq = w1_g.shape
    f = nb * fq
    n_steps = t // tm
    inv_d = 1.0 / d

    def body(x1_ref, t_ref, mix_ref, gpre, gpost, gmix, w1_hbm, w2_hbm,
             dx1_ref, f_ref, df2_ref, h2_ref, df1_ref, dmix_ref, dgpost_ref, dgpre_ref, dgmix_ref, loss_ref,
             w1_v, w2_v, f1_s, blk_s, f2_s, acc_post, acc_pre, acc_mix, acc_loss, sem):
        _load_blocks_once(w1_hbm, w1_v, sem)
        _load_once(w2_hbm, w2_v, sem.at[nb])

        @pl.when(pl.program_id(0) == 0)
        def _():
            acc_post[...] = jnp.zeros_like(acc_post)
            acc_pre[...] = jnp.zeros_like(acc_pre)
            acc_mix[...] = jnp.zeros_like(acc_mix)
            acc_loss[...] = jnp.zeros_like(acc_loss)

        def norm(rows):
            xv = x1_ref[rows, :]
            r = lax.rsqrt(_mean_lanes(xv * xv) + RMS_EPS)
            h2_ref[rows, :] = (xv * r * gpre[...]).astype(BF16)
        _for_chunks(tm, ROW_CHUNK, norm)

        for j in range(nb):
            cols = pl.ds(j * fq, fq)
            f1_s[:, cols] = jnp.dot(h2_ref[...], w1_v[:, cols], preferred_element_type=F32)

        def act(rows):
            relu = jnp.maximum(f1_s[rows, :], 0.0)
            f_ref[rows, :] = (relu * relu).astype(BF16)
        _for_chunks(tm, ROW_CHUNK, act)

        f2_s[...] = jnp.dot(f_ref[...], w2_v[...], preferred_element_type=F32)

        def head(rows):
            f2 = f2_s[rows, :]
            rf = lax.rsqrt(_mean_lanes(f2 * f2) + RMS_EPS)
            y = x1_ref[rows, :] + f2 * rf * gpost[...]
            err = y - t_ref[rows, :]
            acc_loss[...] += _fold8(err * err)
            dy = err * inv_d
            gdy = dy * gpost[...]
            df2 = rf * gdy - f2 * (rf * rf * rf * _mean_lanes(gdy * f2))
            df2_ref[rows, :] = df2.astype(BF16)
            acc_post[...] += _fold8(dy * f2 * rf)
            dx1_ref[rows, :] = dy
        _for_chunks(tm, ROW_CHUNK, head)

        for j in range(nb):
            cols = pl.ds(j * fq, fq)
            blk_s[...] = lax.dot_general(df2_ref[...], w2_v[cols, :], NT_DIMS, preferred_element_type=F32)

            def dact(rows):
                relu = jnp.maximum(f1_s[rows, cols], 0.0)
                df1_ref[rows, cols] = (blk_s[rows, :] * (2.0 * relu)).astype(BF16)
            _for_chunks(tm, ROW_CHUNK, dact)

        f2_s[...] = lax.dot_general(df1_ref[...], w1_v[...], NT_DIMS, preferred_element_type=F32)

        def dnorm(rows):
            dh2 = f2_s[rows, :]
            xv = x1_ref[rows, :]
            r = lax.rsqrt(_mean_lanes(xv * xv) + RMS_EPS)
            gd = dh2 * gpre[...]
            dxv = dx1_ref[rows, :] + r * gd - xv * (r * r * r * _mean_lanes(gd * xv))
            dx1_ref[rows, :] = dxv
            acc_pre[...] += _fold8(dh2 * xv * r)
            mix = mix_ref[rows, :]
            rm = lax.rsqrt(_mean_lanes(mix * mix) + RMS_EPS)
            gm = dxv * gmix[...]
            dmix_ref[rows, :] = (rm * gm - mix * (rm * rm * rm * _mean_lanes(gm * mix))).astype(BF16)
            acc_mix[...] += _fold8(dxv * mix * rm)
        _for_chunks(tm, ROW_CHUNK, dnorm)

        _write_row_sums(acc_post, dgpost_ref, n_steps)
        _write_row_sums(acc_pre, dgpre_ref, n_steps)
        _write_row_sums(acc_mix, dgmix_ref, n_steps)
        _write_row_sums(acc_loss, loss_ref, n_steps)

    row = jax.ShapeDtypeStruct((1, d), F32)
    return _call(
        body, name="mlp_fwd_bwd", grid=(n_steps,),
        in_specs=[_rows(tm, d), _rows(tm, d), _rows(tm, d), _const((1, d)), _const((1, d)), _const((1, d)), ANY, ANY],
        out_specs=[_rows(tm, d), _rows(tm, f), _rows(tm, d), _rows(tm, d), _rows(tm, f), _rows(tm, d),
                   _const((1, d)), _const((1, d)), _const((1, d)), _const((1, d))],
        out_shape=[jax.ShapeDtypeStruct((t, d), F32), jax.ShapeDtypeStruct((t, f), BF16),
                   jax.ShapeDtypeStruct((t, d), BF16), jax.ShapeDtypeStruct((t, d), BF16),
                   jax.ShapeDtypeStruct((t, f), BF16), jax.ShapeDtypeStruct((t, d), BF16), row, row, row, row],
        scratch_shapes=[pltpu.VMEM((d, f), BF16), pltpu.VMEM((f, d), BF16),
                        pltpu.VMEM((tm, f), F32), pltpu.VMEM((tm, fq), F32), pltpu.VMEM((tm, d), F32),
                        pltpu.VMEM((SUBLANES, d), F32), pltpu.VMEM((SUBLANES, d), F32),
                        pltpu.VMEM((SUBLANES, d), F32), pltpu.VMEM((SUBLANES, d), F32),
                        pltpu.SemaphoreType.DMA((nb + 1,))],
        args=(x1, target, mix, g2pre, g2post, g1post, w1_g, w2), ex=ex)


def _mixer_bwd_call(dmix, sa, sb, ya, yb, bg, va, cb, ln_g, ln_b, w_a, w_b, w_o, tm, ex=None):
    t, d = dmix.shape
    n_steps = t // tm

    def body(dmix_ref, sa_ref, sb_ref, ya_ref, yb_ref, bg_ref, va_ref, cb_ref, lng, lnb,
             wa_hbm, wb_hbm, wo_hbm,
             dya_ref, dyb_ref, dva_ref, dcb_ref, dbg_ref, dza_ref, dzb_ref,
             dlng_ref, dlnb_ref, dba_ref, dbb_ref,
             wa_v, wb_v, wo_v, s0, s1, acc_lng, acc_lnb, acc_ba, acc_bb, sem):
        _load_once(wa_hbm, wa_v, sem.at[0])
        _load_once(wb_hbm, wb_v, sem.at[1])
        _load_once(wo_hbm, wo_v, sem.at[2])
        accs = (acc_lng, acc_lnb, acc_ba, acc_bb)

        @pl.when(pl.program_id(0) == 0)
        def _():
            for acc in accs:
                acc[...] = jnp.zeros_like(acc)

        s0[...] = lax.dot_general(dmix_ref[...], wo_v[...], NT_DIMS, preferred_element_type=F32)

        def dmerge(rows):
            dm = s0[rows, :]
            sav, sbv = sa_ref[rows, :].astype(F32), sb_ref[rows, :].astype(F32)
            dya_ref[rows, :] = (dm * sav).astype(BF16)
            dyb_ref[rows, :] = (dm * sbv).astype(BF16)
            dza_ref[rows, :] = (dm * ya_ref[rows, :].astype(F32) * sav * (1.0 - sav)).astype(BF16)
            dzb_ref[rows, :] = (dm * yb_ref[rows, :].astype(F32) * sbv * (1.0 - sbv)).astype(BF16)
        _for_chunks(tm, ROW_CHUNK, dmerge)

        s0[...] = lax.dot_general(dya_ref[...], wa_v[...], NT_DIMS, preferred_element_type=F32)
        s1[...] = lax.dot_general(dyb_ref[...], wb_v[...], NT_DIMS, preferred_element_type=F32)

        def dbranches(rows):
            dpa = s0[rows, :]
            dbg_ref[rows, :] = (dpa * va_ref[rows, :].astype(F32)).astype(BF16)
            dva = dpa * bg_ref[rows, :].astype(F32)
            dva_ref[rows, :] = dva
            acc_ba[...] += _fold8(dva)
            cbv = cb_ref[rows, :]
            mu = _mean_lanes(cbv)
            cen = cbv - mu
            rstd = lax.rsqrt(_mean_lanes(cen * cen) + LN_EPS)
            xhat = cen * rstd
            ln = xhat * lng[...] + lnb[...]
            sig = jax.nn.sigmoid(ln)
            dln = s1[rows, :] * (sig * (1.0 + ln * (1.0 - sig)))
            acc_lng[...] += _fold8(dln * xhat)
            acc_lnb[...] += _fold8(dln)
            dxh = dln * lng[...]
            dcb = rstd * (dxh - _mean_lanes(dxh) - xhat * _mean_lanes(dxh * xhat))
            dcb_ref[rows, :] = dcb
            acc_bb[...] += _fold8(dcb)
        _for_chunks(tm, ROW_CHUNK, dbranches)

        _write_row_sums(acc_lng, dlng_ref, n_steps)
        _write_row_sums(acc_lnb, dlnb_ref, n_steps)
        _write_row_sums(acc_ba, dba_ref, n_steps)
        _write_row_sums(acc_bb, dbb_ref, n_steps)

    bf = jax.ShapeDtypeStruct((t, d), BF16)
    f32 = jax.ShapeDtypeStruct((t, d), F32)
    row = jax.ShapeDtypeStruct((1, d), F32)
    return _call(
        body, name="mixer_bwd", grid=(n_steps,),
        in_specs=[_rows(tm, d)] * 8 + [_const((1, d))] * 2 + [ANY, ANY, ANY],
        out_specs=[_rows(tm, d)] * 7 + [_const((1, d))] * 4,
        out_shape=[bf, bf, f32, f32, bf, bf, bf, row, row, row, row],
        scratch_shapes=[pltpu.VMEM((d, d), BF16), pltpu.VMEM((d, d), BF16), pltpu.VMEM((d, d), BF16),
                        pltpu.VMEM((tm, d), F32), pltpu.VMEM((tm, d), F32)]
        + [pltpu.VMEM((SUBLANES, d), F32)] * 4 + [pltpu.SemaphoreType.DMA((3,))],
        args=(dmix, sa, sb, ya, yb, bg, va, cb, ln_g, ln_b, w_a, w_b, w_o), ex=ex)


def _conv_bwd_call(dva, dcb, ua, ub, cg, ha, a, sg, dbg, dza, dzb, conv_a_w, conv_b_w, tm, ex=None):
    t, d = dva.shape
    n_steps = t // tm
    ka, kb = conv_a_w.shape[0], conv_b_w.shape[0]
    off_a = [HALO_A + (ka - 1) // 2 - k for k in range(ka)]
    off_b = [HALO_B + (kb - 1) // 2 - k for k in range(kb)]
    plan_a, plan_b = _shift_plan(off_a), _shift_plan(off_b)

    def body(dvap, dvac, dvan, dcbp, dcbc, dcbn, ua_ref, ub_ref,
             cg_ref, ha_ref, a_ref, sg_ref, dbg_ref, dza_ref, dzb_ref, wa_c, wb_c,
             dproj_ref, dwa_ref, dwb_ref, dbin_ref,
             e_dva, e_dcb, sh_a, sh_b, wa8, wb8, acc_wa, acc_wb, acc_bin):
        i = pl.program_id(0)
        _fill_tap_rows(wa_c, wa8)
        _fill_tap_rows(wb_c, wb8)

        @pl.when(i == 0)
        def _():
            acc_wa[...] = jnp.zeros_like(acc_wa)
            acc_wb[...] = jnp.zeros_like(acc_wb)
            acc_bin[...] = jnp.zeros_like(acc_bin)

        _fill_ext(e_dva, dvap, dvac, dvan, HALO_A, tm, i, n_steps)
        _fill_ext(e_dcb, dcbp, dcbc, dcbn, HALO_B, tm, i, n_steps)
        _fill_shifted(e_dva, sh_a, plan_a)
        _fill_shifted(e_dcb, sh_b, plan_b)

        def put(col, rows, val_f32):
            dproj_ref[rows, pl.ds(col * d, d)] = val_f32.astype(BF16)
            acc_bin[:, pl.ds(col * d, d)] += _fold8(val_f32)

        for r0 in range(0, tm, CONV_ROWS):
            rows = pl.ds(r0, CONV_ROWS)
            ua_c, ub_c = ua_ref[rows, :], ub_ref[rows, :]
            dua = jnp.zeros((CONV_ROWS, d), F32)
            for k in range(ka):
                xk = _window(e_dva, sh_a, plan_a, off_a[k], r0)
                dua = dua + _tap(wa8, k) * xk
                acc_wa[pl.ds(k * SUBLANES, SUBLANES), :] += _fold8(ua_c * xk)
            dub = jnp.zeros((CONV_ROWS, d), F32)
            for k in range(kb):
                xk = _window(e_dcb, sh_b, plan_b, off_b[k], r0)
                dub = dub + _tap(wb8, k) * xk
                acc_wb[pl.ds(k * SUBLANES, SUBLANES), :] += _fold8(ub_c * xk)
            cgv, hav = cg_ref[rows, :].astype(F32), ha_ref[rows, :].astype(F32)
            av, sgv = a_ref[rows, :].astype(F32), sg_ref[rows, :].astype(F32)
            put(0, rows, dbg_ref[rows, :].astype(F32))
            put(1, rows, dua * hav)
            put(2, rows, dua * cgv)
            put(3, rows, dub * sgv)
            put(4, rows, dub * av * sgv * (1.0 - sgv))
            put(5, rows, dza_ref[rows, :].astype(F32))
            put(6, rows, dzb_ref[rows, :].astype(F32))

        @pl.when(i == n_steps - 1)
        def _():
            for k in range(ka):
                dwa_ref[k:k + 1, :] = jnp.sum(acc_wa[pl.ds(k * SUBLANES, SUBLANES), :], axis=0, keepdims=True)
            for k in range(kb):
                dwb_ref[k:k + 1, :] = jnp.sum(acc_wb[pl.ds(k * SUBLANES, SUBLANES), :], axis=0, keepdims=True)
            dbin_ref[...] = jnp.sum(acc_bin[...], axis=0, keepdims=True)

    halo_a = [_halo_prev(tm, HALO_A, d), _rows(tm, d), _halo_next(tm, HALO_A, d, t)]
    halo_b = [_halo_prev(tm, HALO_B, d), _rows(tm, d), _halo_next(tm, HALO_B, d, t)]
    return _call(
        body, name="conv_bwd", grid=(n_steps,),
        in_specs=halo_a + halo_b + [_rows(tm, d)] * 9 + [_const((ka, d)), _const((kb, d))],
        out_specs=[_rows(tm, 7 * d), _const((ka, d)), _const((kb, d)), _const((1, 7 * d))],
        out_shape=[jax.ShapeDtypeStruct((t, 7 * d), BF16), jax.ShapeDtypeStruct((ka, d), F32),
                   jax.ShapeDtypeStruct((kb, d), F32), jax.ShapeDtypeStruct((1, 7 * d), F32)],
        scratch_shapes=[pltpu.VMEM((tm + 2 * HALO_A, d), F32), pltpu.VMEM((tm + 2 * HALO_B, d), F32),
                        pltpu.VMEM((len(plan_a), _shifted_rows(tm, off_a), d), F32),
                        pltpu.VMEM((len(plan_b), _shifted_rows(tm, off_b), d), F32),
                        pltpu.VMEM((ka * SUBLANES, d), F32), pltpu.VMEM((kb * SUBLANES, d), F32),
                        pltpu.VMEM((ka * SUBLANES, d), F32), pltpu.VMEM((kb * SUBLANES, d), F32),
                        pltpu.VMEM((SUBLANES, 7 * d), F32)],
        args=(dva, dva, dva, dcb, dcb, dcb, ua, ub, cg, ha, a, sg, dbg, dza, dzb,
              conv_a_w, conv_b_w), ex=ex)


def _dx_call(dproj, x, dx1, g1pre, w_in_g, tm, first, n_steps, prev, ex=None):
    t, d = x.shape
    nb, _, n4 = w_in_g.shape
    ni = nb * n4
    rows = lambda width: pl.BlockSpec((tm, width), lambda i: (i + first, 0))
    if prev is None:
        prev = (jnp.zeros((SUBLANES, 128), F32), jnp.zeros((1, d), F32))
    prev_dx, prev_dg = prev

    def body(dp_ref, x_ref, dx1_ref, g_ref, w_hbm, prev_dx_hbm, prev_dg_ref, dx_ref, dg_ref, w_v, dh_s, acc_g, sem):
        _load_blocks_once(w_hbm, w_v, sem)

        @pl.when(pl.program_id(0) == 0)
        def _():
            acc_g[...] = jnp.zeros_like(acc_g)
            acc_g[0:1, :] = prev_dg_ref[...]

        dh_s[...] = lax.dot_general(dp_ref[...], w_v[...], NT_DIMS, preferred_element_type=F32)

        def dnorm(rows):
            dh = dh_s[rows, :]
            xv = x_ref[rows, :]
            r = lax.rsqrt(_mean_lanes(xv * xv) + RMS_EPS)
            gd = dh * g_ref[...]
            dx_ref[rows, :] = dx1_ref[rows, :] + r * gd - xv * (r * r * r * _mean_lanes(gd * xv))
            acc_g[...] += _fold8(dh * xv * r)
        _for_chunks(tm, ROW_CHUNK, dnorm)
        _write_row_sums(acc_g, dg_ref, n_steps)

    return _call(
        body, name="dx_bwd_from_%d" % first, grid=(n_steps,),
        in_specs=[rows(ni), rows(d), rows(d), _const((1, d)), ANY, ANY, _const((1, d))],
        out_specs=[rows(d), _const((1, d))],
        out_shape=[jax.ShapeDtypeStruct((t, d), F32), jax.ShapeDtypeStruct((1, d), F32)],
        scratch_shapes=[pltpu.VMEM((d, ni), BF16), pltpu.VMEM((tm, d), F32),
                        pltpu.VMEM((SUBLANES, d), F32), pltpu.SemaphoreType.DMA((nb,))],
        args=(dproj, x, dx1, g1pre, w_in_g, prev_dx, prev_dg), ex=ex,
        aliases={5: 0} if first > 0 else None)


def _tn_matmul(a, g, nblk, a_cols, g_cols, a_blocked, g_blocked, tt, name, ex=None):
    t = a.shape[0]

    def body(a_ref, g_ref, o_ref):
        @pl.when(pl.program_id(1) == 0)
        def _():
            o_ref[...] = jnp.zeros_like(o_ref)
        o_ref[0] += lax.dot_general(a_ref[...], g_ref[...], TN_DIMS, preferred_element_type=F32)

    (out,), xouts = _call(
        body, name=name, grid=(nblk, t // tt),
        in_specs=[pl.BlockSpec((tt, a_cols), (lambda b, s: (s, b)) if a_blocked else (lambda b, s: (s, 0))),
                  pl.BlockSpec((tt, g_cols), (lambda b, s: (s, b)) if g_blocked else (lambda b, s: (s, 0)))],
        out_specs=[pl.BlockSpec((1, a_cols, g_cols), lambda b, s: (b, 0, 0))],
        out_shape=[jax.ShapeDtypeStruct((nblk, a_cols, g_cols), F32)],
        scratch_shapes=[], args=(a, g), ex=ex)
    return out, xouts


def _tn_matmuls(pairs, tt, name):
    t, d = pairs[0][0].shape
    k = len(pairs)

    def body(*refs):
        ins, outs = refs[:2 * k], refs[2 * k:]

        @pl.when(pl.program_id(0) == 0)
        def _():
            for o_ref in outs:
                o_ref[...] = jnp.zeros_like(o_ref)
        for j in range(k):
            outs[j][...] += lax.dot_general(ins[2 * j][...], ins[2 * j + 1][...], TN_DIMS, preferred_element_type=F32)

    outs, _ = _call(
        body, name=name, grid=(t // tt,), in_specs=[_rows(tt, d)] * (2 * k), out_specs=[_const((d, d))] * k,
        out_shape=[jax.ShapeDtypeStruct((d, d), F32)] * k, scratch_shapes=[],
        args=[m for pair in pairs for m in pair])
    return outs


def _pair_sum_call(g_full, from_sibling, core, name):
    nblk, r, c = g_full.shape
    hr = r // 2
    tr = min(hr, 256)
    n = hr // tr

    def body(core_ref, g_ref, p_ref, o_ref):
        o_ref[...] = (g_ref[...] + p_ref[...]).astype(BF16)

    return pl.pallas_call(
        body, name=name,
        grid_spec=pltpu.PrefetchScalarGridSpec(
            num_scalar_prefetch=1, grid=(nblk, n),
            in_specs=[pl.BlockSpec((1, tr, c), lambda j, i, cr: (j, cr[0] * n + i, 0)),
                      pl.BlockSpec((1, tr, c), lambda j, i, cr: (j, i, 0))],
            out_specs=pl.BlockSpec((1, tr, c), lambda j, i, cr: (j, i, 0))),
        out_shape=jax.ShapeDtypeStruct((nblk, hr, c), BF16),
        compiler_params=_params("parallel", "parallel"))(core, g_full, from_sibling)


def _chip_sum_call(pair, received, chip_core, name):
    _, hr, c = pair.shape
    tr = min(hr, 256)
    n = hr // tr

    def body(cc_ref, own_ref, r_ref, o_ref):
        o_ref[...] = ((own_ref[0].astype(F32) + r_ref[0].astype(F32)) + r_ref[1].astype(F32)) + r_ref[2].astype(F32)

    return pl.pallas_call(
        body, name=name,
        grid_spec=pltpu.PrefetchScalarGridSpec(
            num_scalar_prefetch=1, grid=(n,),
            in_specs=[pl.BlockSpec((1, tr, c), lambda i, cc: (cc[0], i, 0)),
                      pl.BlockSpec((N_CHIPS - 1, tr, c), lambda i, cc: (0, i, 0))],
            out_specs=pl.BlockSpec((tr, c), lambda i, cc: (cc[1] * n + i, 0))),
        out_shape=jax.ShapeDtypeStruct((2 * hr, c), F32),
        compiler_params=_params("parallel"))(chip_core, pair, received)


def _adamw(w, g, m, v):
    m = ADAM_B1 * m + (1.0 - ADAM_B1) * g
    v = ADAM_B2 * v + (1.0 - ADAM_B2) * (g * g)
    m_hat = m / (1.0 - ADAM_B1 ** ADAM_STEP)
    v_hat = v / (1.0 - ADAM_B2 ** ADAM_STEP)
    delta = -ADAM_LR * (m_hat / (jnp.sqrt(v_hat) + ADAM_EPS) + ADAM_WD * w)
    return delta, m, v


def _adam_call(w, g, m, v, name):
    r, c = w.shape
    tr = min(r, 256)

    def body(w_ref, g_ref, m_ref, v_ref, go_ref, d_ref, mo_ref, vo_ref):
        go_ref[...] = g_ref[...]
        d_ref[...], mo_ref[...], vo_ref[...] = _adamw(w_ref[...], g_ref[...], m_ref[...], v_ref[...])

    shape = jax.ShapeDtypeStruct((r, c), F32)
    return pl.pallas_call(
        body, name=name, grid=(r // tr,), in_specs=[_rows(tr, c)] * 4, out_specs=[_rows(tr, c)] * 4,
        out_shape=[shape] * 4, compiler_params=_params("parallel"))(w, g, m, v)


def _adam_sc_call(w, g, m, v, name):
    r, c = w.shape
    rows_tile = r // SC_TILES
    rr = min(rows_tile, SC_ROWS)

    def body(w_hbm, g_hbm, m_hbm, v_hbm, go_hbm, d_hbm, mo_hbm, vo_hbm, wb, gb, mb, vb):
        tile = lax.axis_index("sc_tile") * 2 + lax.axis_index("sc_core")

        @pl.loop(0, rows_tile, step=rr)
        def _(p):
            rows = pl.ds(tile * rows_tile + p, rr)
            pltpu.sync_copy(w_hbm.at[rows, :], wb)
            pltpu.sync_copy(g_hbm.at[rows, :], gb)
            pltpu.sync_copy(m_hbm.at[rows, :], mb)
            pltpu.sync_copy(v_hbm.at[rows, :], vb)

            @pl.loop(0, rr)
            def _(i):
                @pl.loop(0, c, step=SC_LANES)
                def _(j):
                    at = (i, pl.ds(j, SC_LANES))
                    delta, m2, v2 = _adamw(wb[at], gb[at], mb[at], vb[at])
                    wb[at] = delta
                    mb[at] = m2
                    vb[at] = v2

            pltpu.sync_copy(gb, go_hbm.at[rows, :])
            pltpu.sync_copy(wb, d_hbm.at[rows, :])
            pltpu.sync_copy(mb, mo_hbm.at[rows, :])
            pltpu.sync_copy(vb, vo_hbm.at[rows, :])

    shape = jax.ShapeDtypeStruct((r, c), F32)
    return pl.kernel(
        body, name=name, out_type=[shape] * 4,
        mesh=plsc.VectorSubcoreMesh(core_axis_name="sc_core", subcore_axis_name="sc_tile"),
        scratch_types=[pltpu.VMEM((rr, c), F32)] * 4)(w, g, m, v)


def _gather_conv_weights(conv_a_w, conv_b_w, d):
    ka, dq = conv_a_w.shape
    kb = conv_b_w.shape[0]
    ra = -(-ka // SUBLANES) * SUBLANES
    rb = -(-kb // SUBLANES) * SUBLANES
    a_pad = jnp.pad(conv_a_w, ((0, ra - ka), (0, 0)))
    b_pad = jnp.pad(conv_b_w, ((0, rb - kb), (0, 0)))

    def body(a_ref, b_ref, oa_ref, ob_ref, pack, slots, send_sem, recv_sem):
        x, y, c = _place()
        me = 2 * x + y
        chips = _other_chips(x, y)
        pack[pl.ds(0, ra), :] = a_ref[...]
        pack[pl.ds(ra, rb), :] = b_ref[...]
        copies = []
        for j, (px, py, _) in enumerate(chips):
            cp = pltpu.make_async_remote_copy(
                src_ref=pack, dst_ref=slots.at[me], send_sem=send_sem.at[j], recv_sem=recv_sem.at[j],
                device_id=(px, py, c), device_id_type=MESH)
            cp.start()
            copies.append(cp)
        for j, (px, py, pk) in enumerate(chips):
            pltpu.make_async_remote_copy(
                src_ref=pack, dst_ref=slots.at[pk], send_sem=send_sem.at[j], recv_sem=recv_sem.at[j],
                device_id=(px, py, c), device_id_type=MESH).wait_recv()
        for cp in copies:
            cp.wait_send()
        slots[me] = pack[...]
        for k in range(N_CHIPS):
            oa_ref[:, pl.ds(k * dq, dq)] = slots[k, pl.ds(0, ra), :]
            ob_ref[:, pl.ds(k * dq, dq)] = slots[k, pl.ds(ra, rb), :]

    oa, ob = pl.pallas_call(
        body, name="gather_conv_weights", in_specs=[VMEM_FULL] * 2, out_specs=[VMEM_FULL] * 2,
        out_shape=[jax.ShapeDtypeStruct((ra, d), F32), jax.ShapeDtypeStruct((rb, d), F32)],
        scratch_shapes=[pltpu.VMEM((ra + rb, dq), F32), pltpu.VMEM((N_CHIPS, ra + rb, dq), F32),
                        pltpu.SemaphoreType.DMA((N_CHIPS - 1,)), pltpu.SemaphoreType.DMA((N_CHIPS - 1,))],
        compiler_params=pltpu.CompilerParams(has_side_effects=True))(a_pad, b_pad)
    return oa[:ka], ob[:kb]


def _small_step_call(partials, loss_rows, weights, m_s, v_s, sharded, d):
    n = len(partials)
    row_counts = [p.shape[0] for p in partials]
    starts = [sum(row_counts[:i]) for i in range(n)]
    loss_row = sum(row_counts)
    pack_rows = -(-(loss_row + 1) // SUBLANES) * SUBLANES
    dq = d // N_CHIPS

    def body(*refs):
        p_refs = refs[:n]
        loss_in = refs[n]
        w_refs = refs[n + 1:2 * n + 1]
        m_refs = refs[2 * n + 1:3 * n + 1]
        v_refs = refs[3 * n + 1:4 * n + 1]
        o = 4 * n + 1
        g_out = refs[o:o + n]
        d_out = refs[o + n:o + 2 * n]
        m_out = refs[o + 2 * n:o + 3 * n]
        v_out = refs[o + 3 * n:o + 4 * n]
        loss_out = refs[o + 4 * n]
        pack, from_sibling, slots, send_sem, recv_sem = refs[o + 4 * n + 1:]
        x, y, c = _place()
        me = 2 * x + y

        pack[...] = jnp.zeros_like(pack)
        for i in range(n):
            pack[pl.ds(starts[i], row_counts[i]), :] = p_refs[i][...]
        pack[pl.ds(loss_row, 1), :] = loss_in[...]

        pair = _remote(pack, from_sibling, send_sem.at[0], recv_sem.at[0], (x, y, 1 - c))
        pair.start()
        pair.wait()
        pack[...] = pack[...] + from_sibling[...]
        chips = _other_chips(x, y)
        copies = [_remote(pack, slots.at[me], send_sem.at[1 + j], recv_sem.at[1 + j], (px, py, c))
                  for j, (px, py, _) in enumerate(chips)]
        for cp in copies:
            cp.start()
        for j, (px, py, pk) in enumerate(chips):
            _remote(pack, slots.at[pk], send_sem.at[1 + j], recv_sem.at[1 + j], (px, py, c)).wait_recv()
        for cp in copies:
            cp.wait_send()

        slots[me] = pack[...]
        total = slots[0]
        for k in range(1, N_CHIPS):
            total = total + slots[k]
        pack[...] = total

        loss_out[...] = jnp.broadcast_to(
            (0.5 / d) * jnp.sum(pack[pl.ds(loss_row, 1), :], axis=-1, keepdims=True), loss_out.shape)
        chip = 2 * x + y
        for i in range(n):
            rows = pl.ds(starts[i], row_counts[i])
            if sharded[i]:
                for k in range(N_CHIPS):
                    @pl.when(chip == k)
                    def _():
                        g_out[i][...] = pack[rows, pl.ds(k * dq, dq)]
            else:
                g_out[i][...] = pack[rows, :]
            d_out[i][...], m_out[i][...], v_out[i][...] = _adamw(
                w_refs[i][...], g_out[i][...], m_refs[i][...], v_refs[i][...])

    w_shapes = [jax.ShapeDtypeStruct(w.shape, F32) for w in weights]
    n_in = 4 * n + 1
    return pl.pallas_call(
        body, name="small_grads_allreduce_adamw",
        in_specs=[VMEM_FULL] * n_in, out_specs=[VMEM_FULL] * (4 * n + 1),
        out_shape=w_shapes * 4 + [jax.ShapeDtypeStruct((SUBLANES, 128), F32)],
        scratch_shapes=[pltpu.VMEM((pack_rows, d), F32), pltpu.VMEM((pack_rows, d), F32),
                        pltpu.VMEM((N_CHIPS, pack_rows, d), F32),
                        pltpu.SemaphoreType.DMA((N_CHIPS,)), pltpu.SemaphoreType.DMA((N_CHIPS,))],
        compiler_params=pltpu.CompilerParams(has_side_effects=True, vmem_limit_bytes=VMEM_LIMIT))(
            *partials, loss_rows, *weights, *m_s, *v_s)


def _tile(t, want):
    return min(t, want)


def kernel(x, norm1_pre_g, w_in, b_in, conv_a_w, conv_a_b, w_a_out, conv_b_w, conv_b_b, ln_b_g, ln_b_b, w_b_out, w_o, norm1_post_g, norm2_pre_g, w_mlp_in, w_mlp_out, norm2_post_g, loss_target, m_norm1_pre_g, m_w_in, m_b_in, m_conv_a_w, m_conv_a_b, m_w_a_out, m_conv_b_w, m_conv_b_b, m_ln_b_g, m_ln_b_b, m_w_b_out, m_w_o, m_norm1_post_g, m_norm2_pre_g, m_w_mlp_in, m_w_mlp_out, m_norm2_post_g, v_norm1_pre_g, v_w_in, v_b_in, v_conv_a_w, v_conv_a_b, v_w_a_out, v_conv_b_w, v_conv_b_b, v_ln_b_g, v_ln_b_b, v_w_b_out, v_w_o, v_norm1_post_g, v_norm2_pre_g, v_w_mlp_in, v_w_mlp_out, v_norm2_post_g):
    _, t, d = x.shape
    xt = x.reshape(t, d)
    tgt = loss_target.reshape(t, d)
    row = lambda vec: vec.reshape(1, -1)
    cx, cy, cc = _place()
    core = cc.astype(jnp.int32).reshape(1)
    chip = (2 * cx + cy).astype(jnp.int32).reshape(1)

    big = dict(w_in=w_in, w_a_out=w_a_out, w_b_out=w_b_out, w_o=w_o, w_mlp_in=w_mlp_in, w_mlp_out=w_mlp_out)
    names = list(big)
    chip_core = jnp.concatenate([chip, core])
    slot = {k: _cast_to_slot(big[k], chip, "cast_" + k) for k in names}
    mixer_w, mlp_w = ["w_a_out", "w_b_out", "w_o"], ["w_mlp_in", "w_mlp_out"]
    rows_of = lambda buf: buf.reshape(-1, buf.shape[-1])

    def pair_sums(keys, full, from_sibling):
        return [_pair_sum_call(g, p, core, "pair_sum_" + k) for k, g, p in zip(keys, full, from_sibling)]

    def chip_sums(keys, pairs, received):
        return [_chip_sum_call(p, r, chip_core, "chip_sum_" + k) for k, p, r in zip(keys, pairs, received)]

    (w_in_g,) = _exchange_call("gather_w_in", [_ex_gather_ici([slot["w_in"]]), _ex_gather_forward([slot["w_in"]])])

    g1pre, g1post, g2pre, g2post = row(norm1_pre_g), row(norm1_post_g), row(norm2_pre_g), row(norm2_post_g)
    lng, lnb, ba, bb = row(ln_b_g), row(ln_b_b), row(conv_a_b), row(conv_b_b)
    conv_a_full, conv_b_full = _gather_conv_weights(conv_a_w, conv_b_w, d)

    (h, ua, ub, bg, cg, ha, a, sg, sa, sb), landed = _proj_call(
        xt, g1pre, w_in_g, row(b_in), _tile(t, 512), ex=_ex_gather_ici([slot[k] for k in mixer_w + mlp_w]))
    w_a_g, w_b_g, w_o_g = _exchange_call("forward_mixer_weights", [_ex_gather_forward(landed[:3])])
    w_a_full, w_b_full, w_o_full = rows_of(w_a_g), rows_of(w_b_g), rows_of(w_o_g)
    (x1, va, pa, cb, sbo, ya, yb, mg, mix), (w1_g, w2_g) = _mixer_fwd_call(
        ua, ub, bg, sa, sb, xt, conv_a_full, ba, conv_b_full, bb, lng, lnb,
        w_a_full, w_b_full, w_o_full, g1post, _tile(t, 256), ex=_ex_gather_forward(landed[3:]))
    (dx1, f, df2, h2, df1, dmix, dg2post, dg2pre, dg1post, loss_rows), _ = _mlp_call(
        x1, tgt, mix, g2pre, g2post, g1post, w1_g, rows_of(w2_g), _tile(t, 256))

    tt = _tile(t, 2048)
    n4, fq, dq = w_in.shape[1], w_mlp_in.shape[1], d // N_CHIPS
    g_mlp = [_tn_matmul(h2, df1, N_CHIPS, d, fq, False, True, tt, "dw_mlp_in")[0],
             _tn_matmul(f, df2, N_CHIPS, fq, d, True, False, tt, "dw_mlp_out")[0]]
    (dya, dyb, dva, dcb, dbg, dza, dzb, dlng, dlnb, dba, dbb), sib_mlp = _mixer_bwd_call(
        dmix, sa, sb, ya, yb, bg, va, cb, lng, lnb, w_a_full, w_b_full, w_o_full, _tile(t, 512),
        ex=_ex_sibling_halves(g_mlp))
    p_mlp = pair_sums(mlp_w, g_mlp, sib_mlp)
    g_mix = [g.reshape(N_CHIPS, dq, d)
             for g in _tn_matmuls([(pa, dya), (sbo, dyb), (mg, dmix)], _tile(t, 1024), "dw_mixer")]
    ex_a, ex_b = _ex_scatter_to_owner(p_mlp), _ex_sibling_halves(g_mix)
    (dproj, dwa_conv, dwb_conv, dbin), xo = _conv_bwd_call(
        dva, dcb, ua, ub, cg, ha, a, sg, dbg, dza, dzb, conv_a_full, conv_b_full, _tile(t, 256),
        ex=_merge(ex_a, ex_b))
    recv_mlp, sib_mix = _split(xo, ex_a, ex_b)
    r_mlp = chip_sums(mlp_w, p_mlp, recv_mlp)
    p_mix = pair_sums(mixer_w, g_mix, sib_mix)
    ex_a, ex_b = _ex_share_halves(r_mlp), _ex_scatter_to_owner(p_mix)
    g_in, xo = _tn_matmul(h, dproj, N_CHIPS, d, n4, False, True, tt, "dw_in", ex=_merge(ex_a, ex_b))
    red_mlp, recv_mix = _split(xo, ex_a, ex_b)
    r_mix = chip_sums(mixer_w, p_mix, recv_mix)
    ex_a, ex_b = _ex_sibling_halves([g_in]), _ex_share_halves(r_mix)
    tm_dx = _tile(t, 512)
    n_dx = t // tm_dx
    n_a = max(1, (3 * n_dx) // 8)
    dx_done, xo = _dx_call(dproj, xt, dx1, g1pre, w_in_g, tm_dx, 0, n_a, None, ex=_merge(ex_a, ex_b))
    sib_in, red_mix = _split(xo, ex_a, ex_b)
    p_in = pair_sums(["w_in"], [g_in], sib_in)
    (grad_x, dg1pre), recv_in = _dx_call(dproj, xt, dx1, g1pre, w_in_g, tm_dx, n_a, n_dx - n_a, dx_done,
                                         ex=_ex_scatter_to_owner(p_in))
    r_in = chip_sums(["w_in"], p_in, recv_in)
    red_in = _exchange_call("w_in_grad_to_sibling", [_ex_share_halves(r_in)])
    reduced = dict(zip(mlp_w + mixer_w + ["w_in"], red_mlp + red_mix + red_in))

    moments = dict(w_in=(m_w_in, v_w_in), w_a_out=(m_w_a_out, v_w_a_out), w_b_out=(m_w_b_out, v_w_b_out),
                   w_o=(m_w_o, v_w_o), w_mlp_in=(m_w_mlp_in, v_w_mlp_in), w_mlp_out=(m_w_mlp_out, v_w_mlp_out))
    out = {}
    for k in names:
        adam = _adam_sc_call if k != "w_in" and big[k].shape[0] % (SC_TILES * SUBLANES) == 0 else _adam_call
        out[k] = tuple(adam(big[k], reduced[k], *moments[k], "adamw_" + k))

    small = [
        ("conv_b_w", dwb_conv, conv_b_w, m_conv_b_w, v_conv_b_w, True),
        ("conv_b_b", dbb, bb, row(m_conv_b_b), row(v_conv_b_b), False),
        ("b_in", dbin.reshape(7, d), b_in.reshape(7, d), m_b_in.reshape(7, d), v_b_in.reshape(7, d), False),
        ("norm1_pre_g", dg1pre, row(norm1_pre_g), row(m_norm1_pre_g), row(v_norm1_pre_g), False),
        ("conv_a_w", dwa_conv, conv_a_w, m_conv_a_w, v_conv_a_w, True),
        ("conv_a_b", dba, ba, row(m_conv_a_b), row(v_conv_a_b), False),
        ("ln_b_g", dlng, lng, row(m_ln_b_g), row(v_ln_b_g), False),
        ("ln_b_b", dlnb, lnb, row(m_ln_b_b), row(v_ln_b_b), False),
        ("norm1_post_g", dg1post, g1post, row(m_norm1_post_g), row(v_norm1_post_g), False),
        ("norm2_pre_g", dg2pre, g2pre, row(m_norm2_pre_g), row(v_norm2_pre_g), False),
        ("norm2_post_g", dg2post, g2post, row(m_norm2_post_g), row(v_norm2_post_g), False),
    ]
    res = _small_step_call([s[1] for s in small], loss_rows, [s[2] for s in small], [s[3] for s in small],
                           [s[4] for s in small], [s[5] for s in small], d)
    ns = len(small)
    loss = res[4 * ns][0, 0]
    shapes = dict(norm1_pre_g=norm1_pre_g.shape, b_in=b_in.shape, conv_a_w=conv_a_w.shape,
                  conv_a_b=conv_a_b.shape, conv_b_w=conv_b_w.shape, conv_b_b=conv_b_b.shape,
                  ln_b_g=ln_b_g.shape, ln_b_b=ln_b_b.shape, norm1_post_g=norm1_post_g.shape,
                  norm2_pre_g=norm2_pre_g.shape, norm2_post_g=norm2_post_g.shape)
    for i, s in enumerate(small):
        out[s[0]] = tuple(res[q * ns + i].reshape(shapes[s[0]]) for q in range(4))

    order = ["norm1_pre_g", "w_in", "b_in", "conv_a_w", "conv_a_b", "w_a_out", "conv_b_w", "conv_b_b",
             "ln_b_g", "ln_b_b", "w_b_out", "w_o", "norm1_post_g", "norm2_pre_g", "w_mlp_in", "w_mlp_out",
             "norm2_post_g"]
    return (loss, grad_x.reshape(x.shape), *[out[k][0] for k in order], *[out[k][1] for k in order],
            *[out[k][2] for k in order], *[out[k][3] for k in order])
```

```python
import functools

import jax
import jax.numpy as jnp
from jax import lax
from jax.experimental import pallas as pl
from jax.experimental.pallas import tpu as pltpu
from jax.experimental.pallas import tpu_sc as plsc

RMS_EPS = 1e-6
LN_EPS = 1e-5
ADAM_LR = 0.001
ADAM_B1 = 0.9
ADAM_B2 = 0.999
ADAM_EPS = 1e-08
ADAM_WD = 0.01
ADAM_STEP = 10

F32 = jnp.float32
BF16 = jnp.bfloat16
MESH = pl.DeviceIdType.MESH
ANY = pl.BlockSpec(memory_space=pl.ANY)
VMEM_FULL = pl.BlockSpec(memory_space=pltpu.VMEM)

V7X_VMEM_BYTES = 64 * 1024 * 1024
VMEM_LIMIT = V7X_VMEM_BYTES - 8 * 1024 * 1024
SUBLANES = 8
N_CHIPS = 4
N_DEV = 8
SC_TILES = 32
SC_LANES = 16
SC_ROWS = 16
HALO_A = 8
HALO_B = 16
CONV_ROWS = 16
ROW_CHUNK = 32

NT_DIMS = (((1,), (1,)), ((), ()))
TN_DIMS = (((0,), (0,)), ((), ()))


def _params(*sem):
    return pltpu.CompilerParams(dimension_semantics=sem, vmem_limit_bytes=VMEM_LIMIT)


def _rows(tm, d):
    return pl.BlockSpec((tm, d), lambda i: (i, 0))


def _const(shape):
    return pl.BlockSpec(shape, lambda i: (0,) * len(shape))


def _halo_prev(tm, hb, d):
    return pl.BlockSpec((hb, d), lambda i: (jnp.maximum(i * (tm // hb) - 1, 0), 0))


def _halo_next(tm, hb, d, t):
    return pl.BlockSpec((hb, d), lambda i: (jnp.minimum((i + 1) * (tm // hb), t // hb - 1), 0))


def _for_chunks(n_rows, rc, fn):
    for r0 in range(0, n_rows, rc):
        fn(pl.ds(r0, rc))


def _fold8(v):
    return v.reshape(v.shape[0] // SUBLANES, SUBLANES, v.shape[1]).sum(axis=0)


def _mean_lanes(v):
    return jnp.mean(v, axis=-1, keepdims=True)


def _load_blocks_once(w_hbm, w_vmem, sem):
    nb, _, n = w_hbm.shape

    @pl.when(pl.program_id(0) == 0)
    def _():
        copies = [pltpu.make_async_copy(w_hbm.at[j], w_vmem.at[:, pl.ds(j * n, n)], sem.at[j])
                  for j in range(nb)]
        for cp in copies:
            cp.start()
        for cp in copies:
            cp.wait()


def _load_once(w_hbm, w_vmem, sem):
    @pl.when(pl.program_id(0) == 0)
    def _():
        cp = pltpu.make_async_copy(w_hbm, w_vmem, sem)
        cp.start()
        cp.wait()


def _write_row_sums(acc_ref, out_ref, n_steps):
    @pl.when(pl.program_id(0) == n_steps - 1)
    def _():
        out_ref[...] = jnp.sum(acc_ref[...], axis=0, keepdims=True)


def _place():
    return lax.axis_index("x"), lax.axis_index("y"), lax.axis_index("c")


def _other_chips(x, y):
    rel = [(x, 1 - y), (1 - x, y), (1 - x, 1 - y)]
    return [(px, py, 2 * px + py) for px, py in rel]


class _Exchange:
    def __init__(self, inputs, out_shapes, aliases, n_sems, copies):
        self.inputs = list(inputs)
        self.out_shapes = list(out_shapes)
        self.aliases = dict(aliases)
        self.n_sems = n_sems
        self.copies = copies


def _remote(src, dst, send, recv, device):
    return pltpu.make_async_remote_copy(src_ref=src, dst_ref=dst, send_sem=send, recv_sem=recv,
                                        device_id=device, device_id_type=MESH)


def _sds(a):
    return jax.ShapeDtypeStruct(a.shape, a.dtype)


def _ex_gather_ici(bufs):
    n = len(bufs)

    def copies(xin, xout, send, recv):
        x, y, c = _place()
        me = 2 * x + y
        out = []
        for a in range(n):
            hr = xin[a].shape[1] // 2
            rows = pl.ds(c * hr, hr)
            for j, (px, py, _) in enumerate(_other_chips(x, y)):
                k = a * (N_CHIPS - 1) + j
                out.append(_remote(xin[a].at[me, rows, :], xout[a].at[me, rows, :], send(k), recv(k), (px, py, c)))
        return out

    return _Exchange(bufs, [_sds(b) for b in bufs], {a: a for a in range(n)}, n * (N_CHIPS - 1), copies)


def _ex_gather_neighbours(bufs):
    n = len(bufs)

    def copies(xin, xout, send, recv):
        x, y, c = _place()
        me = 2 * x + y
        out = []
        for a in range(n):
            hr = xin[a].shape[1] // 2
            rows = pl.ds(c * hr, hr)
            for j, peer in enumerate([(1 - x, y, c), (x, 1 - y, c)]):
                out.append(_remote(xin[a].at[me, rows, :], xout[a].at[me, rows, :], send(2 * a + j), recv(2 * a + j), peer))
        return out

    return _Exchange(bufs, [_sds(b) for b in bufs], {a: a for a in range(n)}, 2 * n, copies)


def _ex_gather_relay(bufs):
    n = len(bufs)

    def copies(xin, xout, send, recv):
        x, y, c = _place()
        x_chip, y_chip = 2 * (1 - x) + y, 2 * x + (1 - y)
        out = []
        for a in range(n):
            hr = xin[a].shape[1] // 2
            qr = hr // 2
            first, second = pl.ds(c * hr, qr), pl.ds(c * hr + qr, qr)
            out.append(_remote(xin[a].at[x_chip, first, :], xout[a].at[x_chip, first, :],
                               send(2 * a), recv(2 * a), (x, 1 - y, c)))
            out.append(_remote(xin[a].at[y_chip, second, :], xout[a].at[y_chip, second, :],
                               send(2 * a + 1), recv(2 * a + 1), (1 - x, y, c)))
        return out

    return _Exchange(bufs, [_sds(b) for b in bufs], {a: a for a in range(n)}, 2 * n, copies)


def _ex_gather_forward(bufs):
    n = len(bufs)

    def copies(xin, xout, send, recv):
        x, y, c = _place()
        out = []
        for a in range(n):
            hr = xin[a].shape[1] // 2
            rows = pl.ds(c * hr, hr)
            for j, (_, _, pk) in enumerate(_other_chips(x, y)):
                k = a * (N_CHIPS - 1) + j
                out.append(_remote(xin[a].at[pk, rows, :], xout[a].at[pk, rows, :], send(k), recv(k), (x, y, 1 - c)))
        return out

    return _Exchange(bufs, [_sds(b) for b in bufs], {a: a for a in range(n)}, n * (N_CHIPS - 1), copies)


def _ex_sibling_halves(grads):
    n = len(grads)

    def copies(xin, xout, send, recv):
        x, y, c = _place()
        out = []
        for a in range(n):
            hr = xin[a].shape[1] // 2
            out.append(_remote(xin[a].at[:, pl.ds((1 - c) * hr, hr), :], xout[a], send(a), recv(a), (x, y, 1 - c)))
        return out

    shapes = [jax.ShapeDtypeStruct((g.shape[0], g.shape[1] // 2, g.shape[2]), g.dtype) for g in grads]
    return _Exchange(grads, shapes, {}, n, copies)


def _ex_scatter_to_owner(pairs):
    n = len(pairs)

    def copies(xin, xout, send, recv):
        x, y, c = _place()
        out = []
        for a in range(n):
            for j, (px, py, pk) in enumerate(_other_chips(x, y)):
                k = a * (N_CHIPS - 1) + j
                out.append(_remote(xin[a].at[pk], xout[a].at[j], send(k), recv(k), (px, py, c)))
        return out

    shapes = [jax.ShapeDtypeStruct((N_CHIPS - 1,) + p.shape[1:], p.dtype) for p in pairs]
    return _Exchange(pairs, shapes, {}, n * (N_CHIPS - 1), copies)


def _ex_share_halves(reduced):
    n = len(reduced)

    def copies(xin, xout, send, recv):
        x, y, c = _place()
        out = []
        for a in range(n):
            hr = xin[a].shape[0] // 2
            rows = pl.ds(c * hr, hr)
            out.append(_remote(xin[a].at[rows, :], xout[a].at[rows, :], send(a), recv(a), (x, y, 1 - c)))
        return out

    return _Exchange(reduced, [_sds(r) for r in reduced], {a: a for a in range(n)}, n, copies)


def _merge(*exs):
    exs = [e for e in exs if e is not None]
    if not exs:
        return None
    inputs, shapes, aliases = [], [], {}
    in_off, out_off, sem_off = [], [], []
    n_sems = 0
    for e in exs:
        in_off.append(len(inputs))
        out_off.append(len(shapes))
        sem_off.append(n_sems)
        aliases.update({len(inputs) + i: len(shapes) + o for i, o in e.aliases.items()})
        inputs += e.inputs
        shapes += e.out_shapes
        n_sems += e.n_sems

    def copies(xin, xout, send, recv):
        out = []
        for e, io, oo, so in zip(exs, in_off, out_off, sem_off):
            out += e.copies(xin[io:io + len(e.inputs)], xout[oo:oo + len(e.out_shapes)],
                            lambda i, so=so: send(so + i), lambda i, so=so: recv(so + i))
        return out

    return _Exchange(inputs, shapes, aliases, n_sems, copies)


def _split(ex_outs, *exs):
    parts, o = [], 0
    for e in exs:
        parts.append(list(ex_outs[o:o + len(e.out_shapes)]))
        o += len(e.out_shapes)
    return parts


def _call(body, *, name, grid, in_specs, out_specs, out_shape, scratch_shapes, args, ex=None, aliases=None):
    n_in, n_out, n_scr = len(in_specs), len(out_specs), len(scratch_shapes)
    seq = ("arbitrary",) * len(grid)
    aliases = dict(aliases or {})
    if ex is None:
        outs = pl.pallas_call(
            body, name=name, grid=grid, in_specs=list(in_specs), out_specs=list(out_specs),
            out_shape=list(out_shape), scratch_shapes=list(scratch_shapes), input_output_aliases=aliases,
            compiler_params=_params(*seq))(*args)
        return list(outs), []
    n_xi, n_xo = len(ex.inputs), len(ex.out_shapes)

    def full(*refs):
        ins, xin = refs[:n_in], refs[n_in:n_in + n_xi]
        o = n_in + n_xi
        outs, xout = refs[o:o + n_out], refs[o + n_out:o + n_out + n_xo]
        s = o + n_out + n_xo
        scr = refs[s:s + n_scr]
        send_sems, recv_sems = refs[s + n_scr:]
        send = lambda i: send_sems.at[i]
        recv = lambda i: recv_sems.at[i]
        first = functools.reduce(jnp.logical_and, [pl.program_id(a) == 0 for a in range(len(grid))])
        last = functools.reduce(jnp.logical_and, [pl.program_id(a) == grid[a] - 1 for a in range(len(grid))])

        @pl.when(first)
        def _():
            for cp in ex.copies(xin, xout, send, recv):
                cp.start()

        body(*ins, *outs, *scr)

        @pl.when(last)
        def _():
            for cp in ex.copies(xin, xout, send, recv):
                cp.wait()

    res = pl.pallas_call(
        full, name=name, grid=grid, in_specs=list(in_specs) + [ANY] * n_xi,
        out_specs=list(out_specs) + [ANY] * n_xo, out_shape=list(out_shape) + ex.out_shapes,
        scratch_shapes=list(scratch_shapes) + [pltpu.SemaphoreType.DMA((ex.n_sems,)),
                                               pltpu.SemaphoreType.DMA((ex.n_sems,))],
        input_output_aliases={**aliases, **{n_in + i: n_out + o for i, o in ex.aliases.items()}},
        compiler_params=pltpu.CompilerParams(dimension_semantics=seq, vmem_limit_bytes=VMEM_LIMIT,
                                             has_side_effects=True))(*args, *ex.inputs)
    return list(res[:n_out]), list(res[n_out:])


def _exchange_call(name, phases):
    first = phases[0]
    n_xi, n_xo = len(first.inputs), len(first.out_shapes)

    def body(*refs):
        xin, xout = refs[:n_xi], refs[n_xi:n_xi + n_xo]
        sems = refs[n_xi + n_xo:]
        for p, ex in enumerate(phases):
            send_sems, recv_sems = sems[2 * p], sems[2 * p + 1]
            cps = ex.copies(xin, xout, lambda i: send_sems.at[i], lambda i: recv_sems.at[i])
            for cp in cps:
                cp.start()
            for cp in cps:
                cp.wait()

    sems = []
    for ex in phases:
        sems += [pltpu.SemaphoreType.DMA((ex.n_sems,)), pltpu.SemaphoreType.DMA((ex.n_sems,))]
    return list(pl.pallas_call(
        body, name=name, in_specs=[ANY] * n_xi, out_specs=[ANY] * n_xo, out_shape=first.out_shapes,
        scratch_shapes=sems, input_output_aliases=dict(first.aliases),
        compiler_params=pltpu.CompilerParams(has_side_effects=True))(*first.inputs))


def _cast_to_slot(w, chip, name):
    r, c = w.shape
    tr = min(r, 256)

    def body(chip_ref, w_ref, o_ref):
        o_ref[0] = w_ref[...].astype(BF16)

    return pl.pallas_call(
        body, name=name,
        grid_spec=pltpu.PrefetchScalarGridSpec(
            num_scalar_prefetch=1, grid=(r // tr,),
            in_specs=[pl.BlockSpec((tr, c), lambda i, k: (i, 0))],
            out_specs=pl.BlockSpec((1, tr, c), lambda i, k: (k[0], i, 0))),
        out_shape=jax.ShapeDtypeStruct((N_CHIPS, r, c), BF16),
        compiler_params=_params("parallel"))(chip, w)


def _proj_call(x, g1pre, w_in_g, b_in, tm, ex=None):
    t, d = x.shape
    nb, _, n4 = w_in_g.shape
    ni = nb * n4
    assert ni == 7 * d

    def body(x_ref, g_ref, b_ref, w_hbm, h_ref, ua_ref, ub_ref, bg_ref, cg_ref, ha_ref, a_ref,
             sg_ref, sa_ref, sb_ref, w_v, p0, p1, sem):
        _load_blocks_once(w_hbm, w_v, sem)

        def norm(rows):
            xv = x_ref[rows, :]
            r = lax.rsqrt(_mean_lanes(xv * xv) + RMS_EPS)
            h_ref[rows, :] = (xv * r * g_ref[...]).astype(BF16)
        _for_chunks(tm, ROW_CHUNK, norm)

        def group(i, dst):
            cols = pl.ds(i * d, d)
            dst[...] = jnp.dot(h_ref[...], w_v[:, cols], preferred_element_type=F32) + b_ref[:, cols]

        group(0, p0)

        def bgate(rows):
            bg_ref[rows, :] = p0[rows, :].astype(BF16)
        _for_chunks(tm, ROW_CHUNK, bgate)

        group(1, p0)
        group(2, p1)

        def branch_a(rows):
            cg, ha = p0[rows, :], p1[rows, :]
            ua_ref[rows, :] = cg * ha
            cg_ref[rows, :] = cg.astype(BF16)
            ha_ref[rows, :] = ha.astype(BF16)
        _for_chunks(tm, ROW_CHUNK, branch_a)

        group(3, p0)
        group(4, p1)

        def branch_b(rows):
            a, sg = p0[rows, :], jax.nn.sigmoid(p1[rows, :])
            ub_ref[rows, :] = a * sg
            a_ref[rows, :] = a.astype(BF16)
            sg_ref[rows, :] = sg.astype(BF16)
        _for_chunks(tm, ROW_CHUNK, branch_b)

        group(5, p0)
        group(6, p1)

        def gates(rows):
            sa_ref[rows, :] = jax.nn.sigmoid(p0[rows, :]).astype(BF16)
            sb_ref[rows, :] = jax.nn.sigmoid(p1[rows, :]).astype(BF16)
        _for_chunks(tm, ROW_CHUNK, gates)

    bf = jax.ShapeDtypeStruct((t, d), BF16)
    f32 = jax.ShapeDtypeStruct((t, d), F32)
    return _call(
        body, name="proj_fwd", grid=(t // tm,),
        in_specs=[_rows(tm, d), _const((1, d)), _const((1, ni)), ANY],
        out_specs=[_rows(tm, d)] * 10,
        out_shape=[bf, f32, f32, bf, bf, bf, bf, bf, bf, bf],
        scratch_shapes=[pltpu.VMEM((d, ni), BF16), pltpu.VMEM((tm, d), F32), pltpu.VMEM((tm, d), F32),
                        pltpu.SemaphoreType.DMA((nb,))],
        args=(x, g1pre, b_in, w_in_g), ex=ex)


def _fill_ext(ext, prev_ref, cur_ref, next_ref, hb, tm, i, n_steps):
    ext[pl.ds(0, hb), :] = jnp.where(i > 0, prev_ref[...], 0.0)
    ext[pl.ds(hb, tm), :] = cur_ref[...]
    ext[pl.ds(hb + tm, hb), :] = jnp.where(i < n_steps - 1, next_ref[...], 0.0)


def _shift_plan(offsets):
    shifts = sorted({o % SUBLANES for o in offsets if o % SUBLANES})
    return {s: i for i, s in enumerate(shifts)}


def _shifted_rows(tm, offsets):
    return tm + SUBLANES * max(o // SUBLANES for o in offsets)


def _fill_shifted(ext, sh, plan):
    n = sh.shape[1]
    for s, i in plan.items():
        sh[i, :, :] = ext[pl.ds(s, n), :]


def _fill_tap_rows(w_ref, rows8):
    @pl.when(pl.program_id(0) == 0)
    def _():
        for k in range(w_ref.shape[0]):
            rows8[pl.ds(k * SUBLANES, SUBLANES), :] = jnp.broadcast_to(w_ref[k:k + 1, :], (SUBLANES, w_ref.shape[1]))


def _tap(rows8, k):
    w8 = rows8[pl.ds(k * SUBLANES, SUBLANES), :]
    return jnp.concatenate([w8] * (CONV_ROWS // SUBLANES), axis=0)


def _window(ext, sh, plan, offset, r0):
    q, s = divmod(offset, SUBLANES)
    if s == 0:
        return ext[pl.ds(offset + r0, CONV_ROWS), :]
    return sh[plan[s], pl.ds(SUBLANES * q + r0, CONV_ROWS), :]


def _mixer_fwd_call(ua, ub, bg, sa, sb, x, conv_a_w, conv_a_b, conv_b_w, conv_b_b, ln_g, ln_b,
                    w_a, w_b, w_o, g1post, tm, ex=None):
    t, d = x.shape
    n_steps = t // tm
    ka, kb = conv_a_w.shape[0], conv_b_w.shape[0]
    off_a = [HALO_A - (ka - 1) // 2 + k for k in range(ka)]
    off_b = [HALO_B - (kb - 1) // 2 + k for k in range(kb)]
    plan_a, plan_b = _shift_plan(off_a), _shift_plan(off_b)

    def body(uap, uac, uan, ubp, ubc, ubn, bg_ref, sa_ref, sb_ref, x_ref, wa_c, ba_c, wb_c, bb_c,
             lng, lnb, wa_hbm, wb_hbm, wo_hbm, g_ref,
             x1_ref, va_ref, pa_ref, cb_ref, sbo_ref, ya_ref, yb_ref, mg_ref, mix_ref,
             ext_a, ext_b, sh_a, sh_b, wa8, wb8, wa_v, wb_v, wo_v, y0, y1, sem):
        i = pl.program_id(0)
        _fill_tap_rows(wa_c, wa8)
        _fill_tap_rows(wb_c, wb8)
        _load_once(wa_hbm, wa_v, sem.at[0])
        _load_once(wb_hbm, wb_v, sem.at[1])
        _load_once(wo_hbm, wo_v, sem.at[2])
        _fill_ext(ext_a, uap, uac, uan, HALO_A, tm, i, n_steps)
        _fill_ext(ext_b, ubp, ubc, ubn, HALO_B, tm, i, n_steps)
        _fill_shifted(ext_a, sh_a, plan_a)
        _fill_shifted(ext_b, sh_b, plan_b)

        for r0 in range(0, tm, CONV_ROWS):
            rows = pl.ds(r0, CONV_ROWS)
            va = jnp.broadcast_to(ba_c[...], (CONV_ROWS, d))
            for k in range(ka):
                va = va + _tap(wa8, k) * _window(ext_a, sh_a, plan_a, off_a[k], r0)
            va_ref[rows, :] = va.astype(BF16)
            pa_ref[rows, :] = (bg_ref[rows, :].astype(F32) * va).astype(BF16)
            cb = jnp.broadcast_to(bb_c[...], (CONV_ROWS, d))
            for k in range(kb):
                cb = cb + _tap(wb8, k) * _window(ext_b, sh_b, plan_b, off_b[k], r0)
            cb_ref[rows, :] = cb
            mu = _mean_lanes(cb)
            cen = cb - mu
            rstd = lax.rsqrt(_mean_lanes(cen * cen) + LN_EPS)
            ln = cen * rstd * lng[...] + lnb[...]
            sbo_ref[rows, :] = (ln * jax.nn.sigmoid(ln)).astype(BF16)

        y0[...] = jnp.dot(pa_ref[...], wa_v[...], preferred_element_type=F32)
        y1[...] = jnp.dot(sbo_ref[...], wb_v[...], preferred_element_type=F32)

        def merge(rows):
            ya, yb = y0[rows, :], y1[rows, :]
            ya_ref[rows, :] = ya.astype(BF16)
            yb_ref[rows, :] = yb.astype(BF16)
            mg_ref[rows, :] = (sa_ref[rows, :].astype(F32) * ya + sb_ref[rows, :].astype(F32) * yb).astype(BF16)
        _for_chunks(tm, ROW_CHUNK, merge)

        mix_ref[...] = jnp.dot(mg_ref[...], wo_v[...], preferred_element_type=F32)

        def resid(rows):
            mix = mix_ref[rows, :]
            r = lax.rsqrt(_mean_lanes(mix * mix) + RMS_EPS)
            x1_ref[rows, :] = x_ref[rows, :] + mix * r * g_ref[...]
        _for_chunks(tm, ROW_CHUNK, resid)

    bf = jax.ShapeDtypeStruct((t, d), BF16)
    f32 = jax.ShapeDtypeStruct((t, d), F32)
    return _call(
        body, name="mixer_fwd", grid=(n_steps,),
        in_specs=[_halo_prev(tm, HALO_A, d), _rows(tm, d), _halo_next(tm, HALO_A, d, t),
                  _halo_prev(tm, HALO_B, d), _rows(tm, d), _halo_next(tm, HALO_B, d, t),
                  _rows(tm, d), _rows(tm, d), _rows(tm, d), _rows(tm, d),
                  _const((ka, d)), _const((1, d)), _const((kb, d)), _const((1, d)),
                  _const((1, d)), _const((1, d)), ANY, ANY, ANY, _const((1, d))],
        out_specs=[_rows(tm, d)] * 9,
        out_shape=[f32, bf, bf, f32, bf, bf, bf, bf, f32],
        scratch_shapes=[pltpu.VMEM((tm + 2 * HALO_A, d), F32), pltpu.VMEM((tm + 2 * HALO_B, d), F32),
                        pltpu.VMEM((len(plan_a), _shifted_rows(tm, off_a), d), F32),
                        pltpu.VMEM((len(plan_b), _shifted_rows(tm, off_b), d), F32),
                        pltpu.VMEM((ka * SUBLANES, d), F32), pltpu.VMEM((kb * SUBLANES, d), F32),
                        pltpu.VMEM((d, d), BF16), pltpu.VMEM((d, d), BF16), pltpu.VMEM((d, d), BF16),
                        pltpu.VMEM((tm, d), F32), pltpu.VMEM((tm, d), F32),
                        pltpu.SemaphoreType.DMA((3,))],
        args=(ua, ua, ua, ub, ub, ub, bg, sa, sb, x, conv_a_w, conv_a_b, conv_b_w, conv_b_b,
              ln_g, ln_b, w_a, w_b, w_o, g1post), ex=ex)


def _mlp_call(x1, target, mix, g2pre, g2post, g1post, w1_g, w2, tm, ex=None):
    t, d = x1.shape
    nb, _, fq = w1_g.shape
    f = nb * fq
    n_steps = t // tm
    inv_d = 1.0 / d

    def body(x1_ref, t_ref, mix_ref, gpre, gpost, gmix, w1_hbm, w2_hbm,
             dx1_ref, f_ref, df2_ref, h2_ref, df1_ref, dmix_ref, dgpost_ref, dgpre_ref, dgmix_ref, loss_ref,
             w1_v, w2_v, f1_s, blk_s, f2_s, acc_post, acc_pre, acc_mix, acc_loss, sem):
        _load_blocks_once(w1_hbm, w1_v, sem)
        _load_once(w2_hbm, w2_v, sem.at[nb])

        @pl.when(pl.program_id(0) == 0)
        def _():
            acc_post[...] = jnp.zeros_like(acc_post)
            acc_pre[...] = jnp.zeros_like(acc_pre)
            acc_mix[...] = jnp.zeros_like(acc_mix)
            acc_loss[...] = jnp.zeros_like(acc_loss)

        def norm(rows):
            xv = x1_ref[rows, :]
            r = lax.rsqrt(_mean_lanes(xv * xv) + RMS_EPS)
            h2_ref[rows, :] = (xv * r * gpre[...]).astype(BF16)
        _for_chunks(tm, ROW_CHUNK, norm)

        for j in range(nb):
            cols = pl.ds(j * fq, fq)
            f1_s[:, cols] = jnp.dot(h2_ref[...], w1_v[:, cols], preferred_element_type=F32)

        def act(rows):
            relu = jnp.maximum(f1_s[rows, :], 0.0)
            f_ref[rows, :] = (relu * relu).astype(BF16)
        _for_chunks(tm, ROW_CHUNK, act)

        f2_s[...] = jnp.dot(f_ref[...], w2_v[...], preferred_element_type=F32)

        def head(rows):
            f2 = f2_s[rows, :]
            rf = lax.rsqrt(_mean_lanes(f2 * f2) + RMS_EPS)
            y = x1_ref[rows, :] + f2 * rf * gpost[...]
            err = y - t_ref[rows, :]
            acc_loss[...] += _fold8(err * err)
            dy = err * inv_d
            gdy = dy * gpost[...]
            df2 = rf * gdy - f2 * (rf * rf * rf * _mean_lanes(gdy * f2))
            df2_ref[rows, :] = df2.astype(BF16)
            acc_post[...] += _fold8(dy * f2 * rf)
            dx1_ref[rows, :] = dy
        _for_chunks(tm, ROW_CHUNK, head)

        for j in range(nb):
            cols = pl.ds(j * fq, fq)
            blk_s[...] = lax.dot_general(df2_ref[...], w2_v[cols, :], NT_DIMS, preferred_element_type=F32)

            def dact(rows):
                relu = jnp.maximum(f1_s[rows, cols], 0.0)
                df1_ref[rows, cols] = (blk_s[rows, :] * (2.0 * relu)).astype(BF16)
            _for_chunks(tm, ROW_CHUNK, dact)

        f2_s[...] = lax.dot_general(df1_ref[...], w1_v[...], NT_DIMS, preferred_element_type=F32)

        def dnorm(rows):
            dh2 = f2_s[rows, :]
            xv = x1_ref[rows, :]
            r = lax.rsqrt(_mean_lanes(xv * xv) + RMS_EPS)
            gd = dh2 * gpre[...]
            dxv = dx1_ref[rows, :] + r * gd - xv * (r * r * r * _mean_lanes(gd * xv))
            dx1_ref[rows, :] = dxv
            acc_pre[...] += _fold8(dh2 * xv * r)
            mix = mix_ref[rows, :]
            rm = lax.rsqrt(_mean_lanes(mix * mix) + RMS_EPS)
            gm = dxv * gmix[...]
            dmix_ref[rows, :] = (rm * gm - mix * (rm * rm * rm * _mean_lanes(gm * mix))).astype(BF16)
            acc_mix[...] += _fold8(dxv * mix * rm)
        _for_chunks(tm, ROW_CHUNK, dnorm)

        _write_row_sums(acc_post, dgpost_ref, n_steps)
        _write_row_sums(acc_pre, dgpre_ref, n_steps)
        _write_row_sums(acc_mix, dgmix_ref, n_steps)
        _write_row_sums(acc_loss, loss_ref, n_steps)

    row = jax.ShapeDtypeStruct((1, d), F32)
    return _call(
        body, name="mlp_fwd_bwd", grid=(n_steps,),
        in_specs=[_rows(tm, d), _rows(tm, d), _rows(tm, d), _const((1, d)), _const((1, d)), _const((1, d)), ANY, ANY],
        out_specs=[_rows(tm, d), _rows(tm, f), _rows(tm, d), _rows(tm, d), _rows(tm, f), _rows(tm, d),
                   _const((1, d)), _const((1, d)), _const((1, d)), _const((1, d))],
        out_shape=[jax.ShapeDtypeStruct((t, d), F32), jax.ShapeDtypeStruct((t, f), BF16),
                   jax.ShapeDtypeStruct((t, d), BF16), jax.ShapeDtypeStruct((t, d), BF16),
                   jax.ShapeDtypeStruct((t, f), BF16), jax.ShapeDtypeStruct((t, d), BF16), row, row, row, row],
        scratch_shapes=[pltpu.VMEM((d, f), BF16), pltpu.VMEM((f, d), BF16),
                        pltpu.VMEM((tm, f), F32), pltpu.VMEM((tm, fq), F32), pltpu.VMEM((tm, d), F32),
                        pltpu.VMEM((SUBLANES, d), F32), pltpu.VMEM((SUBLANES, d), F32),
                        pltpu.VMEM((SUBLANES, d), F32), pltpu.VMEM((SUBLANES, d), F32),
                        pltpu.SemaphoreType.DMA((nb + 1,))],
        args=(x1, target, mix, g2pre, g2post, g1post, w1_g, w2), ex=ex)


def _mixer_bwd_call(dmix, sa, sb, ya, yb, bg, va, cb, ln_g, ln_b, w_a, w_b, w_o, tm, ex=None):
    t, d = dmix.shape
    n_steps = t // tm

    def body(dmix_ref, sa_ref, sb_ref, ya_ref, yb_ref, bg_ref, va_ref, cb_ref, lng, lnb,
             wa_hbm, wb_hbm, wo_hbm,
             dya_ref, dyb_ref, dva_ref, dcb_ref, dbg_ref, dza_ref, dzb_ref,
             dlng_ref, dlnb_ref, dba_ref, dbb_ref, sbg_ref, sza_ref, szb_ref,
             wa_v, wb_v, wo_v, s0, s1, acc_lng, acc_lnb, acc_ba, acc_bb, acc_bg, acc_za, acc_zb, sem):
        _load_once(wa_hbm, wa_v, sem.at[0])
        _load_once(wb_hbm, wb_v, sem.at[1])
        _load_once(wo_hbm, wo_v, sem.at[2])
        accs = (acc_lng, acc_lnb, acc_ba, acc_bb, acc_bg, acc_za, acc_zb)

        @pl.when(pl.program_id(0) == 0)
        def _():
            for acc in accs:
                acc[...] = jnp.zeros_like(acc)

        s0[...] = lax.dot_general(dmix_ref[...], wo_v[...], NT_DIMS, preferred_element_type=F32)

        def dmerge(rows):
            dm = s0[rows, :]
            sav, sbv = sa_ref[rows, :].astype(F32), sb_ref[rows, :].astype(F32)
            dya_ref[rows, :] = (dm * sav).astype(BF16)
            dyb_ref[rows, :] = (dm * sbv).astype(BF16)
            dza = dm * ya_ref[rows, :].astype(F32) * sav * (1.0 - sav)
            dzb = dm * yb_ref[rows, :].astype(F32) * sbv * (1.0 - sbv)
            dza_ref[rows, :] = dza.astype(BF16)
            dzb_ref[rows, :] = dzb.astype(BF16)
            acc_za[...] += _fold8(dza)
            acc_zb[...] += _fold8(dzb)
        _for_chunks(tm, ROW_CHUNK, dmerge)

        s0[...] = lax.dot_general(dya_ref[...], wa_v[...], NT_DIMS, preferred_element_type=F32)
        s1[...] = lax.dot_general(dyb_ref[...], wb_v[...], NT_DIMS, preferred_element_type=F32)

        def dbranches(rows):
            dpa = s0[rows, :]
            dbg = dpa * va_ref[rows, :].astype(F32)
            dbg_ref[rows, :] = dbg.astype(BF16)
            acc_bg[...] += _fold8(dbg)
            dva = dpa * bg_ref[rows, :].astype(F32)
            dva_ref[rows, :] = dva
            acc_ba[...] += _fold8(dva)
            cbv = cb_ref[rows, :]
            mu = _mean_lanes(cbv)
            cen = cbv - mu
            rstd = lax.rsqrt(_mean_lanes(cen * cen) + LN_EPS)
            xhat = cen * rstd
            ln = xhat * lng[...] + lnb[...]
            sig = jax.nn.sigmoid(ln)
            dln = s1[rows, :] * (sig * (1.0 + ln * (1.0 - sig)))
            acc_lng[...] += _fold8(dln * xhat)
            acc_lnb[...] += _fold8(dln)
            dxh = dln * lng[...]
            dcb = rstd * (dxh - _mean_lanes(dxh) - xhat * _mean_lanes(dxh * xhat))
            dcb_ref[rows, :] = dcb
            acc_bb[...] += _fold8(dcb)
        _for_chunks(tm, ROW_CHUNK, dbranches)

        _write_row_sums(acc_lng, dlng_ref, n_steps)
        _write_row_sums(acc_lnb, dlnb_ref, n_steps)
        _write_row_sums(acc_ba, dba_ref, n_steps)
        _write_row_sums(acc_bb, dbb_ref, n_steps)
        _write_row_sums(acc_bg, sbg_ref, n_steps)
        _write_row_sums(acc_za, sza_ref, n_steps)
        _write_row_sums(acc_zb, szb_ref, n_steps)

    bf = jax.ShapeDtypeStruct((t, d), BF16)
    f32 = jax.ShapeDtypeStruct((t, d), F32)
    row = jax.ShapeDtypeStruct((1, d), F32)
    return _call(
        body, name="mixer_bwd", grid=(n_steps,),
        in_specs=[_rows(tm, d)] * 8 + [_const((1, d))] * 2 + [ANY, ANY, ANY],
        out_specs=[_rows(tm, d)] * 7 + [_const((1, d))] * 7,
        out_shape=[bf, bf, f32, f32, bf, bf, bf] + [row] * 7,
        scratch_shapes=[pltpu.VMEM((d, d), BF16), pltpu.VMEM((d, d), BF16), pltpu.VMEM((d, d), BF16),
                        pltpu.VMEM((tm, d), F32), pltpu.VMEM((tm, d), F32)]
        + [pltpu.VMEM((SUBLANES, d), F32)] * 7 + [pltpu.SemaphoreType.DMA((3,))],
        args=(dmix, sa, sb, ya, yb, bg, va, cb, ln_g, ln_b, w_a, w_b, w_o), ex=ex)


def _conv_bwd_call(dva, dcb, ua, ub, cg, ha, a, sg, dbg, dza, dzb, through_sums, conv_a_w, conv_b_w, tm, ex=None):
    t, d = dva.shape
    n_steps = t // tm
    ka, kb = conv_a_w.shape[0], conv_b_w.shape[0]
    off_a = [HALO_A + (ka - 1) // 2 - k for k in range(ka)]
    off_b = [HALO_B + (kb - 1) // 2 - k for k in range(kb)]
    plan_a, plan_b = _shift_plan(off_a), _shift_plan(off_b)

    def body(dvap, dvac, dvan, dcbp, dcbc, dcbn, ua_ref, ub_ref,
             cg_ref, ha_ref, a_ref, sg_ref, dbg_ref, dza_ref, dzb_ref, wa_c, wb_c, sbg_ref, sza_ref, szb_ref,
             dproj_ref, dwa_ref, dwb_ref, dbin_ref,
             e_dva, e_dcb, sh_a, sh_b, wa8, wb8, acc_wa, acc_wb, acc_bin):
        i = pl.program_id(0)
        _fill_tap_rows(wa_c, wa8)
        _fill_tap_rows(wb_c, wb8)

        @pl.when(i == 0)
        def _():
            acc_wa[...] = jnp.zeros_like(acc_wa)
            acc_wb[...] = jnp.zeros_like(acc_wb)
            acc_bin[...] = jnp.zeros_like(acc_bin)
            for col, s_ref in ((0, sbg_ref), (5, sza_ref), (6, szb_ref)):
                acc_bin[0:1, pl.ds(col * d, d)] = s_ref[...]

        _fill_ext(e_dva, dvap, dvac, dvan, HALO_A, tm, i, n_steps)
        _fill_ext(e_dcb, dcbp, dcbc, dcbn, HALO_B, tm, i, n_steps)
        _fill_shifted(e_dva, sh_a, plan_a)
        _fill_shifted(e_dcb, sh_b, plan_b)

        def put(col, rows, val_f32):
            dproj_ref[rows, pl.ds(col * d, d)] = val_f32.astype(BF16)
            acc_bin[:, pl.ds(col * d, d)] += _fold8(val_f32)

        for r0 in range(0, tm, CONV_ROWS):
            rows = pl.ds(r0, CONV_ROWS)
            ua_c, ub_c = ua_ref[rows, :], ub_ref[rows, :]
            dua = jnp.zeros((CONV_ROWS, d), F32)
            for k in range(ka):
                xk = _window(e_dva, sh_a, plan_a, off_a[k], r0)
                dua = dua + _tap(wa8, k) * xk
                acc_wa[pl.ds(k * SUBLANES, SUBLANES), :] += _fold8(ua_c * xk)
            dub = jnp.zeros((CONV_ROWS, d), F32)
            for k in range(kb):
                xk = _window(e_dcb, sh_b, plan_b, off_b[k], r0)
                dub = dub + _tap(wb8, k) * xk
                acc_wb[pl.ds(k * SUBLANES, SUBLANES), :] += _fold8(ub_c * xk)
            cgv, hav = cg_ref[rows, :].astype(F32), ha_ref[rows, :].astype(F32)
            av, sgv = a_ref[rows, :].astype(F32), sg_ref[rows, :].astype(F32)
            put(1, rows, dua * hav)
            put(2, rows, dua * cgv)
            put(3, rows, dub * sgv)
            put(4, rows, dub * av * sgv * (1.0 - sgv))
            for col, through in ((0, dbg_ref), (5, dza_ref), (6, dzb_ref)):
                dproj_ref[rows, pl.ds(col * d, d)] = through[rows, :]

        @pl.when(i == n_steps - 1)
        def _():
            for k in range(ka):
                dwa_ref[k:k + 1, :] = jnp.sum(acc_wa[pl.ds(k * SUBLANES, SUBLANES), :], axis=0, keepdims=True)
            for k in range(kb):
                dwb_ref[k:k + 1, :] = jnp.sum(acc_wb[pl.ds(k * SUBLANES, SUBLANES), :], axis=0, keepdims=True)
            dbin_ref[...] = jnp.sum(acc_bin[...], axis=0, keepdims=True)

    halo_a = [_halo_prev(tm, HALO_A, d), _rows(tm, d), _halo_next(tm, HALO_A, d, t)]
    halo_b = [_halo_prev(tm, HALO_B, d), _rows(tm, d), _halo_next(tm, HALO_B, d, t)]
    return _call(
        body, name="conv_bwd", grid=(n_steps,),
        in_specs=halo_a + halo_b + [_rows(tm, d)] * 9 + [_const((ka, d)), _const((kb, d))] + [_const((1, d))] * 3,
        out_specs=[_rows(tm, 7 * d), _const((ka, d)), _const((kb, d)), _const((1, 7 * d))],
        out_shape=[jax.ShapeDtypeStruct((t, 7 * d), BF16), jax.ShapeDtypeStruct((ka, d), F32),
                   jax.ShapeDtypeStruct((kb, d), F32), jax.ShapeDtypeStruct((1, 7 * d), F32)],
        scratch_shapes=[pltpu.VMEM((tm + 2 * HALO_A, d), F32), pltpu.VMEM((tm + 2 * HALO_B, d), F32),
                        pltpu.VMEM((len(plan_a), _shifted_rows(tm, off_a), d), F32),
                        pltpu.VMEM((len(plan_b), _shifted_rows(tm, off_b), d), F32),
                        pltpu.VMEM((ka * SUBLANES, d), F32), pltpu.VMEM((kb * SUBLANES, d), F32),
                        pltpu.VMEM((ka * SUBLANES, d), F32), pltpu.VMEM((kb * SUBLANES, d), F32),
                        pltpu.VMEM((SUBLANES, 7 * d), F32)],
        args=(dva, dva, dva, dcb, dcb, dcb, ua, ub, cg, ha, a, sg, dbg, dza, dzb,
              conv_a_w, conv_b_w, *through_sums), ex=ex)


def _dx_call(dproj, x, dx1, g1pre, w_in_g, tm, first, n_steps, prev, ex=None):
    t, d = x.shape
    nb, _, n4 = w_in_g.shape
    ni = nb * n4
    rows = lambda width: pl.BlockSpec((tm, width), lambda i: (i + first, 0))
    if prev is None:
        prev = (jnp.zeros((SUBLANES, 128), F32), jnp.zeros((1, d), F32))
    prev_dx, prev_dg = prev

    def body(dp_ref, x_ref, dx1_ref, g_ref, w_hbm, prev_dx_hbm, prev_dg_ref, dx_ref, dg_ref, w_v, dh_s, acc_g, sem):
        _load_blocks_once(w_hbm, w_v, sem)

        @pl.when(pl.program_id(0) == 0)
        def _():
            acc_g[...] = jnp.zeros_like(acc_g)
            acc_g[0:1, :] = prev_dg_ref[...]

        dh_s[...] = lax.dot_general(dp_ref[...], w_v[...], NT_DIMS, preferred_element_type=F32)

        def dnorm(rows):
            dh = dh_s[rows, :]
            xv = x_ref[rows, :]
            r = lax.rsqrt(_mean_lanes(xv * xv) + RMS_EPS)
            gd = dh * g_ref[...]
            dx_ref[rows, :] = dx1_ref[rows, :] + r * gd - xv * (r * r * r * _mean_lanes(gd * xv))
            acc_g[...] += _fold8(dh * xv * r)
        _for_chunks(tm, ROW_CHUNK, dnorm)
        _write_row_sums(acc_g, dg_ref, n_steps)

    return _call(
        body, name="dx_bwd_from_%d" % first, grid=(n_steps,),
        in_specs=[rows(ni), rows(d), rows(d), _const((1, d)), ANY, ANY, _const((1, d))],
        out_specs=[rows(d), _const((1, d))],
        out_shape=[jax.ShapeDtypeStruct((t, d), F32), jax.ShapeDtypeStruct((1, d), F32)],
        scratch_shapes=[pltpu.VMEM((d, ni), BF16), pltpu.VMEM((tm, d), F32),
                        pltpu.VMEM((SUBLANES, d), F32), pltpu.SemaphoreType.DMA((nb,))],
        args=(dproj, x, dx1, g1pre, w_in_g, prev_dx, prev_dg), ex=ex,
        aliases={5: 0} if first > 0 else None)


def _tn_matmul(a, g, nblk, a_cols, g_cols, a_blocked, g_blocked, tt, name, ex=None):
    t = a.shape[0]

    def body(a_ref, g_ref, o_ref):
        @pl.when(pl.program_id(1) == 0)
        def _():
            o_ref[...] = jnp.zeros_like(o_ref)
        o_ref[0] += lax.dot_general(a_ref[...], g_ref[...], TN_DIMS, preferred_element_type=F32)

    (out,), xouts = _call(
        body, name=name, grid=(nblk, t // tt),
        in_specs=[pl.BlockSpec((tt, a_cols), (lambda b, s: (s, b)) if a_blocked else (lambda b, s: (s, 0))),
                  pl.BlockSpec((tt, g_cols), (lambda b, s: (s, b)) if g_blocked else (lambda b, s: (s, 0)))],
        out_specs=[pl.BlockSpec((1, a_cols, g_cols), lambda b, s: (b, 0, 0))],
        out_shape=[jax.ShapeDtypeStruct((nblk, a_cols, g_cols), F32)],
        scratch_shapes=[], args=(a, g), ex=ex)
    return out, xouts


def _tn_matmuls(pairs, tt, name):
    t, d = pairs[0][0].shape
    k = len(pairs)

    def body(*refs):
        ins, outs = refs[:2 * k], refs[2 * k:]

        @pl.when(pl.program_id(0) == 0)
        def _():
            for o_ref in outs:
                o_ref[...] = jnp.zeros_like(o_ref)
        for j in range(k):
            outs[j][...] += lax.dot_general(ins[2 * j][...], ins[2 * j + 1][...], TN_DIMS, preferred_element_type=F32)

    outs, _ = _call(
        body, name=name, grid=(t // tt,), in_specs=[_rows(tt, d)] * (2 * k), out_specs=[_const((d, d))] * k,
        out_shape=[jax.ShapeDtypeStruct((d, d), F32)] * k, scratch_shapes=[],
        args=[m for pair in pairs for m in pair])
    return outs


def _pair_sum_call(g_full, from_sibling, core, name):
    nblk, r, c = g_full.shape
    hr = r // 2
    tr = min(hr, 256)
    n = hr // tr

    def body(core_ref, g_ref, p_ref, o_ref):
        o_ref[...] = (g_ref[...] + p_ref[...]).astype(BF16)

    return pl.pallas_call(
        body, name=name,
        grid_spec=pltpu.PrefetchScalarGridSpec(
            num_scalar_prefetch=1, grid=(nblk, n),
            in_specs=[pl.BlockSpec((1, tr, c), lambda j, i, cr: (j, cr[0] * n + i, 0)),
                      pl.BlockSpec((1, tr, c), lambda j, i, cr: (j, i, 0))],
            out_specs=pl.BlockSpec((1, tr, c), lambda j, i, cr: (j, i, 0))),
        out_shape=jax.ShapeDtypeStruct((nblk, hr, c), BF16),
        compiler_params=_params("parallel", "parallel"))(core, g_full, from_sibling)


def _chip_sum_call(pair, received, chip_core, name):
    _, hr, c = pair.shape
    tr = min(hr, 256)
    n = hr // tr

    def body(cc_ref, own_ref, r_ref, o_ref):
        o_ref[...] = ((own_ref[0].astype(F32) + r_ref[0].astype(F32)) + r_ref[1].astype(F32)) + r_ref[2].astype(F32)

    return pl.pallas_call(
        body, name=name,
        grid_spec=pltpu.PrefetchScalarGridSpec(
            num_scalar_prefetch=1, grid=(n,),
            in_specs=[pl.BlockSpec((1, tr, c), lambda i, cc: (cc[0], i, 0)),
                      pl.BlockSpec((N_CHIPS - 1, tr, c), lambda i, cc: (0, i, 0))],
            out_specs=pl.BlockSpec((tr, c), lambda i, cc: (cc[1] * n + i, 0))),
        out_shape=jax.ShapeDtypeStruct((2 * hr, c), F32),
        compiler_params=_params("parallel"))(chip_core, pair, received)


def _adamw(w, g, m, v):
    m = ADAM_B1 * m + (1.0 - ADAM_B1) * g
    v = ADAM_B2 * v + (1.0 - ADAM_B2) * (g * g)
    m_hat = m / (1.0 - ADAM_B1 ** ADAM_STEP)
    v_hat = v / (1.0 - ADAM_B2 ** ADAM_STEP)
    delta = -ADAM_LR * (m_hat / (jnp.sqrt(v_hat) + ADAM_EPS) + ADAM_WD * w)
    return delta, m, v


def _adam_call(w, g, m, v, name):
    r, c = w.shape
    tr = min(r, 256)

    def body(w_ref, g_ref, m_ref, v_ref, go_ref, d_ref, mo_ref, vo_ref):
        go_ref[...] = g_ref[...]
        d_ref[...], mo_ref[...], vo_ref[...] = _adamw(w_ref[...], g_ref[...], m_ref[...], v_ref[...])

    shape = jax.ShapeDtypeStruct((r, c), F32)
    return pl.pallas_call(
        body, name=name, grid=(r // tr,), in_specs=[_rows(tr, c)] * 4, out_specs=[_rows(tr, c)] * 4,
        out_shape=[shape] * 4, compiler_params=_params("parallel"))(w, g, m, v)


def _adam_sc_call(w, g, m, v, name):
    r, c = w.shape
    rows_tile = r // SC_TILES
    rr = min(rows_tile, SC_ROWS)

    def body(w_hbm, g_hbm, m_hbm, v_hbm, go_hbm, d_hbm, mo_hbm, vo_hbm, wb, gb, mb, vb):
        tile = lax.axis_index("sc_tile") * 2 + lax.axis_index("sc_core")

        @pl.loop(0, rows_tile, step=rr)
        def _(p):
            rows = pl.ds(tile * rows_tile + p, rr)
            pltpu.sync_copy(w_hbm.at[rows, :], wb)
            pltpu.sync_copy(g_hbm.at[rows, :], gb)
            pltpu.sync_copy(m_hbm.at[rows, :], mb)
            pltpu.sync_copy(v_hbm.at[rows, :], vb)

            @pl.loop(0, rr)
            def _(i):
                @pl.loop(0, c, step=SC_LANES)
                def _(j):
                    at = (i, pl.ds(j, SC_LANES))
                    delta, m2, v2 = _adamw(wb[at], gb[at], mb[at], vb[at])
                    wb[at] = delta
                    mb[at] = m2
                    vb[at] = v2

            pltpu.sync_copy(gb, go_hbm.at[rows, :])
            pltpu.sync_copy(wb, d_hbm.at[rows, :])
            pltpu.sync_copy(mb, mo_hbm.at[rows, :])
            pltpu.sync_copy(vb, vo_hbm.at[rows, :])

    shape = jax.ShapeDtypeStruct((r, c), F32)
    return pl.kernel(
        body, name=name, out_type=[shape] * 4,
        mesh=plsc.VectorSubcoreMesh(core_axis_name="sc_core", subcore_axis_name="sc_tile"),
        scratch_types=[pltpu.VMEM((rr, c), F32)] * 4)(w, g, m, v)


def _gather_conv_weights(conv_a_w, conv_b_w, d):
    ka, dq = conv_a_w.shape
    kb = conv_b_w.shape[0]
    ra = -(-ka // SUBLANES) * SUBLANES
    rb = -(-kb // SUBLANES) * SUBLANES
    a_pad = jnp.pad(conv_a_w, ((0, ra - ka), (0, 0)))
    b_pad = jnp.pad(conv_b_w, ((0, rb - kb), (0, 0)))

    def body(a_ref, b_ref, oa_ref, ob_ref, pack, slots, send_sem, recv_sem):
        x, y, c = _place()
        me = 2 * x + y
        chips = _other_chips(x, y)
        pack[pl.ds(0, ra), :] = a_ref[...]
        pack[pl.ds(ra, rb), :] = b_ref[...]
        copies = []
        for j, (px, py, _) in enumerate(chips):
            cp = pltpu.make_async_remote_copy(
                src_ref=pack, dst_ref=slots.at[me], send_sem=send_sem.at[j], recv_sem=recv_sem.at[j],
                device_id=(px, py, c), device_id_type=MESH)
            cp.start()
            copies.append(cp)
        for j, (px, py, pk) in enumerate(chips):
            pltpu.make_async_remote_copy(
                src_ref=pack, dst_ref=slots.at[pk], send_sem=send_sem.at[j], recv_sem=recv_sem.at[j],
                device_id=(px, py, c), device_id_type=MESH).wait_recv()
        for cp in copies:
            cp.wait_send()
        slots[me] = pack[...]
        for k in range(N_CHIPS):
            oa_ref[:, pl.ds(k * dq, dq)] = slots[k, pl.ds(0, ra), :]
            ob_ref[:, pl.ds(k * dq, dq)] = slots[k, pl.ds(ra, rb), :]

    oa, ob = pl.pallas_call(
        body, name="gather_conv_weights", in_specs=[VMEM_FULL] * 2, out_specs=[VMEM_FULL] * 2,
        out_shape=[jax.ShapeDtypeStruct((ra, d), F32), jax.ShapeDtypeStruct((rb, d), F32)],
        scratch_shapes=[pltpu.VMEM((ra + rb, dq), F32), pltpu.VMEM((N_CHIPS, ra + rb, dq), F32),
                        pltpu.SemaphoreType.DMA((N_CHIPS - 1,)), pltpu.SemaphoreType.DMA((N_CHIPS - 1,))],
        compiler_params=pltpu.CompilerParams(has_side_effects=True))(a_pad, b_pad)
    return oa[:ka], ob[:kb]


def _small_step_call(partials, loss_rows, weights, m_s, v_s, sharded, d):
    n = len(partials)
    row_counts = [p.shape[0] for p in partials]
    starts = [sum(row_counts[:i]) for i in range(n)]
    loss_row = sum(row_counts)
    pack_rows = -(-(loss_row + 1) // SUBLANES) * SUBLANES
    dq = d // N_CHIPS

    def body(*refs):
        p_refs = refs[:n]
        loss_in = refs[n]
        w_refs = refs[n + 1:2 * n + 1]
        m_refs = refs[2 * n + 1:3 * n + 1]
        v_refs = refs[3 * n + 1:4 * n + 1]
        o = 4 * n + 1
        g_out = refs[o:o + n]
        d_out = refs[o + n:o + 2 * n]
        m_out = refs[o + 2 * n:o + 3 * n]
        v_out = refs[o + 3 * n:o + 4 * n]
        loss_out = refs[o + 4 * n]
        pack, from_sibling, slots, send_sem, recv_sem = refs[o + 4 * n + 1:]
        x, y, c = _place()
        me = 2 * x + y

        pack[...] = jnp.zeros_like(pack)
        for i in range(n):
            pack[pl.ds(starts[i], row_counts[i]), :] = p_refs[i][...]
        pack[pl.ds(loss_row, 1), :] = loss_in[...]

        pair = _remote(pack, from_sibling, send_sem.at[0], recv_sem.at[0], (x, y, 1 - c))
        pair.start()
        pair.wait()
        pack[...] = pack[...] + from_sibling[...]
        chips = _other_chips(x, y)
        copies = [_remote(pack, slots.at[me], send_sem.at[1 + j], recv_sem.at[1 + j], (px, py, c))
                  for j, (px, py, _) in enumerate(chips)]
        for cp in copies:
            cp.start()
        for j, (px, py, pk) in enumerate(chips):
            _remote(pack, slots.at[pk], send_sem.at[1 + j], recv_sem.at[1 + j], (px, py, c)).wait_recv()
        for cp in copies:
            cp.wait_send()

        slots[me] = pack[...]
        total = slots[0]
        for k in range(1, N_CHIPS):
            total = total + slots[k]
        pack[...] = total

        loss_out[...] = jnp.broadcast_to(
            (0.5 / d) * jnp.sum(pack[pl.ds(loss_row, 1), :], axis=-1, keepdims=True), loss_out.shape)
        chip = 2 * x + y
        for i in range(n):
            rows = pl.ds(starts[i], row_counts[i])
            if sharded[i]:
                for k in range(N_CHIPS):
                    @pl.when(chip == k)
                    def _():
                        g_out[i][...] = pack[rows, pl.ds(k * dq, dq)]
            else:
                g_out[i][...] = pack[rows, :]
            d_out[i][...], m_out[i][...], v_out[i][...] = _adamw(
                w_refs[i][...], g_out[i][...], m_refs[i][...], v_refs[i][...])

    w_shapes = [jax.ShapeDtypeStruct(w.shape, F32) for w in weights]
    n_in = 4 * n + 1
    return pl.pallas_call(
        body, name="small_grads_allreduce_adamw",
        in_specs=[VMEM_FULL] * n_in, out_specs=[VMEM_FULL] * (4 * n + 1),
        out_shape=w_shapes * 4 + [jax.ShapeDtypeStruct((SUBLANES, 128), F32)],
        scratch_shapes=[pltpu.VMEM((pack_rows, d), F32), pltpu.VMEM((pack_rows, d), F32),
                        pltpu.VMEM((N_CHIPS, pack_rows, d), F32),
                        pltpu.SemaphoreType.DMA((N_CHIPS,)), pltpu.SemaphoreType.DMA((N_CHIPS,))],
        compiler_params=pltpu.CompilerParams(has_side_effects=True, vmem_limit_bytes=VMEM_LIMIT))(
            *partials, loss_rows, *weights, *m_s, *v_s)


def _tile(t, want):
    return min(t, want)


def kernel(x, norm1_pre_g, w_in, b_in, conv_a_w, conv_a_b, w_a_out, conv_b_w, conv_b_b, ln_b_g, ln_b_b, w_b_out, w_o, norm1_post_g, norm2_pre_g, w_mlp_in, w_mlp_out, norm2_post_g, loss_target, m_norm1_pre_g, m_w_in, m_b_in, m_conv_a_w, m_conv_a_b, m_w_a_out, m_conv_b_w, m_conv_b_b, m_ln_b_g, m_ln_b_b, m_w_b_out, m_w_o, m_norm1_post_g, m_norm2_pre_g, m_w_mlp_in, m_w_mlp_out, m_norm2_post_g, v_norm1_pre_g, v_w_in, v_b_in, v_conv_a_w, v_conv_a_b, v_w_a_out, v_conv_b_w, v_conv_b_b, v_ln_b_g, v_ln_b_b, v_w_b_out, v_w_o, v_norm1_post_g, v_norm2_pre_g, v_w_mlp_in, v_w_mlp_out, v_norm2_post_g):
    _, t, d = x.shape
    xt = x.reshape(t, d)
    tgt = loss_target.reshape(t, d)
    row = lambda vec: vec.reshape(1, -1)
    cx, cy, cc = _place()
    core = cc.astype(jnp.int32).reshape(1)
    chip = (2 * cx + cy).astype(jnp.int32).reshape(1)

    big = dict(w_in=w_in, w_a_out=w_a_out, w_b_out=w_b_out, w_o=w_o, w_mlp_in=w_mlp_in, w_mlp_out=w_mlp_out)
    names = list(big)
    chip_core = jnp.concatenate([chip, core])
    slot = {k: _cast_to_slot(big[k], chip, "cast_" + k) for k in names}
    mixer_w, mlp_w = ["w_a_out", "w_b_out", "w_o"], ["w_mlp_in", "w_mlp_out"]
    rows_of = lambda buf: buf.reshape(-1, buf.shape[-1])

    def pair_sums(keys, full, from_sibling):
        return [_pair_sum_call(g, p, core, "pair_sum_" + k) for k, g, p in zip(keys, full, from_sibling)]

    def chip_sums(keys, pairs, received):
        return [_chip_sum_call(p, r, chip_core, "chip_sum_" + k) for k, p, r in zip(keys, pairs, received)]

    w_in_slot = [slot["w_in"]]
    (w_in_g,) = _exchange_call("gather_w_in", [_ex_gather_neighbours(w_in_slot), _ex_gather_relay(w_in_slot),
                                               _ex_gather_forward(w_in_slot)])

    g1pre, g1post, g2pre, g2post = row(norm1_pre_g), row(norm1_post_g), row(norm2_pre_g), row(norm2_post_g)
    lng, lnb, ba, bb = row(ln_b_g), row(ln_b_b), row(conv_a_b), row(conv_b_b)
    conv_a_full, conv_b_full = _gather_conv_weights(conv_a_w, conv_b_w, d)

    (h, ua, ub, bg, cg, ha, a, sg, sa, sb), landed = _proj_call(
        xt, g1pre, w_in_g, row(b_in), _tile(t, 512), ex=_ex_gather_ici([slot[k] for k in mixer_w + mlp_w]))
    w_a_g, w_b_g, w_o_g = _exchange_call("forward_mixer_weights", [_ex_gather_forward(landed[:3])])
    w_a_full, w_b_full, w_o_full = rows_of(w_a_g), rows_of(w_b_g), rows_of(w_o_g)
    (x1, va, pa, cb, sbo, ya, yb, mg, mix), (w1_g, w2_g) = _mixer_fwd_call(
        ua, ub, bg, sa, sb, xt, conv_a_full, ba, conv_b_full, bb, lng, lnb,
        w_a_full, w_b_full, w_o_full, g1post, _tile(t, 256), ex=_ex_gather_forward(landed[3:]))
    (dx1, f, df2, h2, df1, dmix, dg2post, dg2pre, dg1post, loss_rows), _ = _mlp_call(
        x1, tgt, mix, g2pre, g2post, g1post, w1_g, rows_of(w2_g), _tile(t, 256))

    tt = _tile(t, 2048)
    n4, fq, dq = w_in.shape[1], w_mlp_in.shape[1], d // N_CHIPS
    g_mlp = [_tn_matmul(h2, df1, N_CHIPS, d, fq, False, True, tt, "dw_mlp_in")[0],
             _tn_matmul(f, df2, N_CHIPS, fq, d, True, False, tt, "dw_mlp_out")[0]]
    (dya, dyb, dva, dcb, dbg, dza, dzb, dlng, dlnb, dba, dbb, sbg, sza, szb), sib_mlp = _mixer_bwd_call(
        dmix, sa, sb, ya, yb, bg, va, cb, lng, lnb, w_a_full, w_b_full, w_o_full, _tile(t, 512),
        ex=_ex_sibling_halves(g_mlp))
    p_mlp = pair_sums(mlp_w, g_mlp, sib_mlp)
    g_mix = [g.reshape(N_CHIPS, dq, d)
             for g in _tn_matmuls([(pa, dya), (sbo, dyb), (mg, dmix)], _tile(t, 1024), "dw_mixer")]
    ex_a, ex_b = _ex_scatter_to_owner(p_mlp), _ex_sibling_halves(g_mix)
    (dproj, dwa_conv, dwb_conv, dbin), xo = _conv_bwd_call(
        dva, dcb, ua, ub, cg, ha, a, sg, dbg, dza, dzb, (sbg, sza, szb), conv_a_full, conv_b_full, _tile(t, 256),
        ex=_merge(ex_a, ex_b))
    recv_mlp, sib_mix = _split(xo, ex_a, ex_b)
    r_mlp = chip_sums(mlp_w, p_mlp, recv_mlp)
    p_mix = pair_sums(mixer_w, g_mix, sib_mix)
    ex_a, ex_b = _ex_share_halves(r_mlp), _ex_scatter_to_owner(p_mix)
    g_in, xo = _tn_matmul(h, dproj, N_CHIPS, d, n4, False, True, tt, "dw_in", ex=_merge(ex_a, ex_b))
    red_mlp, recv_mix = _split(xo, ex_a, ex_b)
    r_mix = chip_sums(mixer_w, p_mix, recv_mix)
    ex_a, ex_b = _ex_sibling_halves([g_in]), _ex_share_halves(r_mix)
    tm_dx = _tile(t, 512)
    n_dx = t // tm_dx
    n_a = max(1, (3 * n_dx) // 8)
    dx_done, xo = _dx_call(dproj, xt, dx1, g1pre, w_in_g, tm_dx, 0, n_a, None, ex=_merge(ex_a, ex_b))
    sib_in, red_mix = _split(xo, ex_a, ex_b)
    p_in = pair_sums(["w_in"], [g_in], sib_in)
    (grad_x, dg1pre), recv_in = _dx_call(dproj, xt, dx1, g1pre, w_in_g, tm_dx, n_a, n_dx - n_a, dx_done,
                                         ex=_ex_scatter_to_owner(p_in))
    r_in = chip_sums(["w_in"], p_in, recv_in)
    red_in = _exchange_call("w_in_grad_to_sibling", [_ex_share_halves(r_in)])
    reduced = dict(zip(mlp_w + mixer_w + ["w_in"], red_mlp + red_mix + red_in))

    moments = dict(w_in=(m_w_in, v_w_in), w_a_out=(m_w_a_out, v_w_a_out), w_b_out=(m_w_b_out, v_w_b_out),
                   w_o=(m_w_o, v_w_o), w_mlp_in=(m_w_mlp_in, v_w_mlp_in), w_mlp_out=(m_w_mlp_out, v_w_mlp_out))
    out = {}
    for k in names:
        adam = _adam_sc_call if k != "w_in" and big[k].shape[0] % (SC_TILES * SUBLANES) == 0 else _adam_call
        out[k] = tuple(adam(big[k], reduced[k], *moments[k], "adamw_" + k))

    small = [
        ("conv_b_w", dwb_conv, conv_b_w, m_conv_b_w, v_conv_b_w, True),
        ("conv_b_b", dbb, bb, row(m_conv_b_b), row(v_conv_b_b), False),
        ("b_in", dbin.reshape(7, d), b_in.reshape(7, d), m_b_in.reshape(7, d), v_b_in.reshape(7, d), False),
        ("norm1_pre_g", dg1pre, row(norm1_pre_g), row(m_norm1_pre_g), row(v_norm1_pre_g), False),
        ("conv_a_w", dwa_conv, conv_a_w, m_conv_a_w, v_conv_a_w, True),
        ("conv_a_b", dba, ba, row(m_conv_a_b), row(v_conv_a_b), False),
        ("ln_b_g", dlng, lng, row(m_ln_b_g), row(v_ln_b_g), False),
        ("ln_b_b", dlnb, lnb, row(m_ln_b_b), row(v_ln_b_b), False),
        ("norm1_post_g", dg1post, g1post, row(m_norm1_post_g), row(v_norm1_post_g), False),
        ("norm2_pre_g", dg2pre, g2pre, row(m_norm2_pre_g), row(v_norm2_pre_g), False),
        ("norm2_post_g", dg2post, g2post, row(m_norm2_post_g), row(v_norm2_post_g), False),
    ]
    res = _small_step_call([s[1] for s in small], loss_rows, [s[2] for s in small], [s[3] for s in small],
                           [s[4] for s in small], [s[5] for s in small], d)
    ns = len(small)
    loss = res[4 * ns][0, 0]
    shapes = dict(norm1_pre_g=norm1_pre_g.shape, b_in=b_in.shape, conv_a_w=conv_a_w.shape,
                  conv_a_b=conv_a_b.shape, conv_b_w=conv_b_w.shape, conv_b_b=conv_b_b.shape,
                  ln_b_g=ln_b_g.shape, ln_b_b=ln_b_b.shape, norm1_post_g=norm1_post_g.shape,
                  norm2_pre_g=norm2_pre_g.shape, norm2_post_g=norm2_post_g.shape)
    for i, s in enumerate(small):
        out[s[0]] = tuple(res[q * ns + i].reshape(shapes[s[0]]) for q in range(4))

    order = ["norm1_pre_g", "w_in", "b_in", "conv_a_w", "conv_a_b", "w_a_out", "conv_b_w", "conv_b_b",
             "ln_b_g", "ln_b_b", "w_b_out", "w_o", "norm1_post_g", "norm2_pre_g", "w_mlp_in", "w_mlp_out",
             "norm2_post_g"]
    return (loss, grad_x.reshape(x.shape), *[out[k][0] for k in order], *[out[k][1] for k in order],
            *[out[k][2] for k in order], *[out[k][3] for k in order])
```

```python
import functools

import jax
import jax.numpy as jnp
from jax import lax
from jax.experimental import pallas as pl
from jax.experimental.pallas import tpu as pltpu
from jax.experimental.pallas import tpu_sc as plsc

RMS_EPS = 1e-6
LN_EPS = 1e-5
ADAM_LR = 0.001
ADAM_B1 = 0.9
ADAM_B2 = 0.999
ADAM_EPS = 1e-08
ADAM_WD = 0.01
ADAM_STEP = 10

F32 = jnp.float32
BF16 = jnp.bfloat16
MESH = pl.DeviceIdType.MESH
ANY = pl.BlockSpec(memory_space=pl.ANY)
VMEM_FULL = pl.BlockSpec(memory_space=pltpu.VMEM)

V7X_VMEM_BYTES = 64 * 1024 * 1024
VMEM_LIMIT = V7X_VMEM_BYTES - 8 * 1024 * 1024
SUBLANES = 8
N_CHIPS = 4
N_DEV = 8
SC_TILES = 32
SC_LANES = 16
SC_ROWS = 16
HALO_A = 8
HALO_B = 16
CONV_ROWS = 16
ROW_CHUNK = 32

NT_DIMS = (((1,), (1,)), ((), ()))
TN_DIMS = (((0,), (0,)), ((), ()))


def _params(*sem):
    return pltpu.CompilerParams(dimension_semantics=sem, vmem_limit_bytes=VMEM_LIMIT)


def _rows(tm, d):
    return pl.BlockSpec((tm, d), lambda i: (i, 0))


def _const(shape):
    return pl.BlockSpec(shape, lambda i: (0,) * len(shape))


def _halo_prev(tm, hb, d):
    return pl.BlockSpec((hb, d), lambda i: (jnp.maximum(i * (tm // hb) - 1, 0), 0))


def _halo_next(tm, hb, d, t):
    return pl.BlockSpec((hb, d), lambda i: (jnp.minimum((i + 1) * (tm // hb), t // hb - 1), 0))


def _for_chunks(n_rows, rc, fn):
    for r0 in range(0, n_rows, rc):
        fn(pl.ds(r0, rc))


def _fold8(v):
    return v.reshape(v.shape[0] // SUBLANES, SUBLANES, v.shape[1]).sum(axis=0)


def _mean_lanes(v):
    return jnp.mean(v, axis=-1, keepdims=True)


def _load_blocks_once(w_hbm, w_vmem, sem):
    nb, _, n = w_hbm.shape

    @pl.when(pl.program_id(0) == 0)
    def _():
        copies = [pltpu.make_async_copy(w_hbm.at[j], w_vmem.at[:, pl.ds(j * n, n)], sem.at[j])
                  for j in range(nb)]
        for cp in copies:
            cp.start()
        for cp in copies:
            cp.wait()


def _load_once(w_hbm, w_vmem, sem):
    @pl.when(pl.program_id(0) == 0)
    def _():
        cp = pltpu.make_async_copy(w_hbm, w_vmem, sem)
        cp.start()
        cp.wait()


def _write_row_sums(acc_ref, out_ref, n_steps):
    @pl.when(pl.program_id(0) == n_steps - 1)
    def _():
        out_ref[...] = jnp.sum(acc_ref[...], axis=0, keepdims=True)


def _place():
    return lax.axis_index("x"), lax.axis_index("y"), lax.axis_index("c")


def _other_chips(x, y):
    rel = [(x, 1 - y), (1 - x, y), (1 - x, 1 - y)]
    return [(px, py, 2 * px + py) for px, py in rel]


class _Exchange:
    def __init__(self, inputs, out_shapes, aliases, n_sems, copies):
        self.inputs = list(inputs)
        self.out_shapes = list(out_shapes)
        self.aliases = dict(aliases)
        self.n_sems = n_sems
        self.copies = copies


def _remote(src, dst, send, recv, device):
    return pltpu.make_async_remote_copy(src_ref=src, dst_ref=dst, send_sem=send, recv_sem=recv,
                                        device_id=device, device_id_type=MESH)


def _sds(a):
    return jax.ShapeDtypeStruct(a.shape, a.dtype)


def _ex_gather_ici(bufs):
    n = len(bufs)

    def copies(xin, xout, send, recv):
        x, y, c = _place()
        me = 2 * x + y
        out = []
        for a in range(n):
            hr = xin[a].shape[1] // 2
            rows = pl.ds(c * hr, hr)
            for j, (px, py, _) in enumerate(_other_chips(x, y)):
                k = a * (N_CHIPS - 1) + j
                out.append(_remote(xin[a].at[me, rows, :], xout[a].at[me, rows, :], send(k), recv(k), (px, py, c)))
        return out

    return _Exchange(bufs, [_sds(b) for b in bufs], {a: a for a in range(n)}, n * (N_CHIPS - 1), copies)


def _ex_gather_forward(bufs):
    n = len(bufs)

    def copies(xin, xout, send, recv):
        x, y, c = _place()
        out = []
        for a in range(n):
            hr = xin[a].shape[1] // 2
            rows = pl.ds(c * hr, hr)
            for j, (_, _, pk) in enumerate(_other_chips(x, y)):
                k = a * (N_CHIPS - 1) + j
                out.append(_remote(xin[a].at[pk, rows, :], xout[a].at[pk, rows, :], send(k), recv(k), (x, y, 1 - c)))
        return out

    return _Exchange(bufs, [_sds(b) for b in bufs], {a: a for a in range(n)}, n * (N_CHIPS - 1), copies)


def _ex_sibling_halves(grads):
    n = len(grads)

    def copies(xin, xout, send, recv):
        x, y, c = _place()
        out = []
        for a in range(n):
            hr = xin[a].shape[1] // 2
            out.append(_remote(xin[a].at[:, pl.ds((1 - c) * hr, hr), :], xout[a], send(a), recv(a), (x, y, 1 - c)))
        return out

    shapes = [jax.ShapeDtypeStruct((g.shape[0], g.shape[1] // 2, g.shape[2]), g.dtype) for g in grads]
    return _Exchange(grads, shapes, {}, n, copies)


def _ex_scatter_to_owner(pairs):
    n = len(pairs)

    def copies(xin, xout, send, recv):
        x, y, c = _place()
        out = []
        for a in range(n):
            for j, (px, py, pk) in enumerate(_other_chips(x, y)):
                k = a * (N_CHIPS - 1) + j
                out.append(_remote(xin[a].at[pk], xout[a].at[j], send(k), recv(k), (px, py, c)))
        return out

    shapes = [jax.ShapeDtypeStruct((N_CHIPS - 1,) + p.shape[1:], p.dtype) for p in pairs]
    return _Exchange(pairs, shapes, {}, n * (N_CHIPS - 1), copies)


def _ex_share_halves(reduced):
    n = len(reduced)

    def copies(xin, xout, send, recv):
        x, y, c = _place()
        out = []
        for a in range(n):
            hr = xin[a].shape[0] // 2
            rows = pl.ds(c * hr, hr)
            out.append(_remote(xin[a].at[rows, :], xout[a].at[rows, :], send(a), recv(a), (x, y, 1 - c)))
        return out

    return _Exchange(reduced, [_sds(r) for r in reduced], {a: a for a in range(n)}, n, copies)


def _merge(*exs):
    exs = [e for e in exs if e is not None]
    if not exs:
        return None
    inputs, shapes, aliases = [], [], {}
    in_off, out_off, sem_off = [], [], []
    n_sems = 0
    for e in exs:
        in_off.append(len(inputs))
        out_off.append(len(shapes))
        sem_off.append(n_sems)
        aliases.update({len(inputs) + i: len(shapes) + o for i, o in e.aliases.items()})
        inputs += e.inputs
        shapes += e.out_shapes
        n_sems += e.n_sems

    def copies(xin, xout, send, recv):
        out = []
        for e, io, oo, so in zip(exs, in_off, out_off, sem_off):
            out += e.copies(xin[io:io + len(e.inputs)], xout[oo:oo + len(e.out_shapes)],
                            lambda i, so=so: send(so + i), lambda i, so=so: recv(so + i))
        return out

    return _Exchange(inputs, shapes, aliases, n_sems, copies)


def _split(ex_outs, *exs):
    parts, o = [], 0
    for e in exs:
        parts.append(list(ex_outs[o:o + len(e.out_shapes)]))
        o += len(e.out_shapes)
    return parts


def _call(body, *, name, grid, in_specs, out_specs, out_shape, scratch_shapes, args, ex=None, aliases=None):
    n_in, n_out, n_scr = len(in_specs), len(out_specs), len(scratch_shapes)
    seq = ("arbitrary",) * len(grid)
    aliases = dict(aliases or {})
    if ex is None:
        outs = pl.pallas_call(
            body, name=name, grid=grid, in_specs=list(in_specs), out_specs=list(out_specs),
            out_shape=list(out_shape), scratch_shapes=list(scratch_shapes), input_output_aliases=aliases,
            compiler_params=_params(*seq))(*args)
        return list(outs), []
    n_xi, n_xo = len(ex.inputs), len(ex.out_shapes)

    def full(*refs):
        ins, xin = refs[:n_in], refs[n_in:n_in + n_xi]
        o = n_in + n_xi
        outs, xout = refs[o:o + n_out], refs[o + n_out:o + n_out + n_xo]
        s = o + n_out + n_xo
        scr = refs[s:s + n_scr]
        send_sems, recv_sems = refs[s + n_scr:]
        send = lambda i: send_sems.at[i]
        recv = lambda i: recv_sems.at[i]
        first = functools.reduce(jnp.logical_and, [pl.program_id(a) == 0 for a in range(len(grid))])
        last = functools.reduce(jnp.logical_and, [pl.program_id(a) == grid[a] - 1 for a in range(len(grid))])

        @pl.when(first)
        def _():
            for cp in ex.copies(xin, xout, send, recv):
                cp.start()

        body(*ins, *outs, *scr)

        @pl.when(last)
        def _():
            for cp in ex.copies(xin, xout, send, recv):
                cp.wait()

    res = pl.pallas_call(
        full, name=name, grid=grid, in_specs=list(in_specs) + [ANY] * n_xi,
        out_specs=list(out_specs) + [ANY] * n_xo, out_shape=list(out_shape) + ex.out_shapes,
        scratch_shapes=list(scratch_shapes) + [pltpu.SemaphoreType.DMA((ex.n_sems,)),
                                               pltpu.SemaphoreType.DMA((ex.n_sems,))],
        input_output_aliases={**aliases, **{n_in + i: n_out + o for i, o in ex.aliases.items()}},
        compiler_params=pltpu.CompilerParams(dimension_semantics=seq, vmem_limit_bytes=VMEM_LIMIT,
                                             has_side_effects=True))(*args, *ex.inputs)
    return list(res[:n_out]), list(res[n_out:])


def _exchange_call(name, phases):
    first = phases[0]
    n_xi, n_xo = len(first.inputs), len(first.out_shapes)

    def body(*refs):
        xin, xout = refs[:n_xi], refs[n_xi:n_xi + n_xo]
        sems = refs[n_xi + n_xo:]
        for p, ex in enumerate(phases):
            send_sems, recv_sems = sems[2 * p], sems[2 * p + 1]
            cps = ex.copies(xin, xout, lambda i: send_sems.at[i], lambda i: recv_sems.at[i])
            for cp in cps:
                cp.start()
            for cp in cps:
                cp.wait()

    sems = []
    for ex in phases:
        sems += [pltpu.SemaphoreType.DMA((ex.n_sems,)), pltpu.SemaphoreType.DMA((ex.n_sems,))]
    return list(pl.pallas_call(
        body, name=name, in_specs=[ANY] * n_xi, out_specs=[ANY] * n_xo, out_shape=first.out_shapes,
        scratch_shapes=sems, input_output_aliases=dict(first.aliases),
        compiler_params=pltpu.CompilerParams(has_side_effects=True))(*first.inputs))


def _cast_to_slot(w, chip, name):
    r, c = w.shape
    tr = min(r, 256)

    def body(chip_ref, w_ref, o_ref):
        o_ref[0] = w_ref[...].astype(BF16)

    return pl.pallas_call(
        body, name=name,
        grid_spec=pltpu.PrefetchScalarGridSpec(
            num_scalar_prefetch=1, grid=(r // tr,),
            in_specs=[pl.BlockSpec((tr, c), lambda i, k: (i, 0))],
            out_specs=pl.BlockSpec((1, tr, c), lambda i, k: (k[0], i, 0))),
        out_shape=jax.ShapeDtypeStruct((N_CHIPS, r, c), BF16),
        compiler_params=_params("parallel"))(chip, w)


def _proj_call(x, g1pre, w_in_g, b_in, tm, ex=None):
    t, d = x.shape
    nb, _, n4 = w_in_g.shape
    ni = nb * n4
    assert ni == 7 * d

    def body(x_ref, g_ref, b_ref, w_hbm, h_ref, ua_ref, ub_ref, bg_ref, cg_ref, ha_ref, a_ref,
             sg_ref, sa_ref, sb_ref, w_v, p0, p1, sem):
        _load_blocks_once(w_hbm, w_v, sem)

        def norm(rows):
            xv = x_ref[rows, :]
            r = lax.rsqrt(_mean_lanes(xv * xv) + RMS_EPS)
            h_ref[rows, :] = (xv * r * g_ref[...]).astype(BF16)
        _for_chunks(tm, ROW_CHUNK, norm)

        def group(i, dst):
            cols = pl.ds(i * d, d)
            dst[...] = jnp.dot(h_ref[...], w_v[:, cols], preferred_element_type=F32) + b_ref[:, cols]

        group(0, p0)

        def bgate(rows):
            bg_ref[rows, :] = p0[rows, :].astype(BF16)
        _for_chunks(tm, ROW_CHUNK, bgate)

        group(1, p0)
        group(2, p1)

        def branch_a(rows):
            cg, ha = p0[rows, :], p1[rows, :]
            ua_ref[rows, :] = cg * ha
            cg_ref[rows, :] = cg.astype(BF16)
            ha_ref[rows, :] = ha.astype(BF16)
        _for_chunks(tm, ROW_CHUNK, branch_a)

        group(3, p0)
        group(4, p1)

        def branch_b(rows):
            a, sg = p0[rows, :], jax.nn.sigmoid(p1[rows, :])
            ub_ref[rows, :] = a * sg
            a_ref[rows, :] = a.astype(BF16)
            sg_ref[rows, :] = sg.astype(BF16)
        _for_chunks(tm, ROW_CHUNK, branch_b)

        group(5, p0)
        group(6, p1)

        def gates(rows):
            sa_ref[rows, :] = jax.nn.sigmoid(p0[rows, :]).astype(BF16)
            sb_ref[rows, :] = jax.nn.sigmoid(p1[rows, :]).astype(BF16)
        _for_chunks(tm, ROW_CHUNK, gates)

    bf = jax.ShapeDtypeStruct((t, d), BF16)
    f32 = jax.ShapeDtypeStruct((t, d), F32)
    return _call(
        body, name="proj_fwd", grid=(t // tm,),
        in_specs=[_rows(tm, d), _const((1, d)), _const((1, ni)), ANY],
        out_specs=[_rows(tm, d)] * 10,
        out_shape=[bf, f32, f32, bf, bf, bf, bf, bf, bf, bf],
        scratch_shapes=[pltpu.VMEM((d, ni), BF16), pltpu.VMEM((tm, d), F32), pltpu.VMEM((tm, d), F32),
                        pltpu.SemaphoreType.DMA((nb,))],
        args=(x, g1pre, b_in, w_in_g), ex=ex)


def _fill_ext(ext, prev_ref, cur_ref, next_ref, hb, tm, i, n_steps):
    ext[pl.ds(0, hb), :] = jnp.where(i > 0, prev_ref[...], 0.0)
    ext[pl.ds(hb, tm), :] = cur_ref[...]
    ext[pl.ds(hb + tm, hb), :] = jnp.where(i < n_steps - 1, next_ref[...], 0.0)


def _shift_plan(offsets):
    shifts = sorted({o % SUBLANES for o in offsets if o % SUBLANES})
    return {s: i for i, s in enumerate(shifts)}


def _shifted_rows(tm, offsets):
    return tm + SUBLANES * max(o // SUBLANES for o in offsets)


def _fill_shifted(ext, sh, plan):
    n = sh.shape[1]
    for s, i in plan.items():
        sh[i, :, :] = ext[pl.ds(s, n), :]


def _fill_tap_rows(w_ref, rows8):
    @pl.when(pl.program_id(0) == 0)
    def _():
        for k in range(w_ref.shape[0]):
            rows8[pl.ds(k * SUBLANES, SUBLANES), :] = jnp.broadcast_to(w_ref[k:k + 1, :], (SUBLANES, w_ref.shape[1]))


def _tap(rows8, k):
    w8 = rows8[pl.ds(k * SUBLANES, SUBLANES), :]
    return jnp.concatenate([w8] * (CONV_ROWS // SUBLANES), axis=0)


def _window(ext, sh, plan, offset, r0):
    q, s = divmod(offset, SUBLANES)
    if s == 0:
        return ext[pl.ds(offset + r0, CONV_ROWS), :]
    return sh[plan[s], pl.ds(SUBLANES * q + r0, CONV_ROWS), :]


def _mixer_fwd_call(ua, ub, bg, sa, sb, x, conv_a_w, conv_a_b, conv_b_w, conv_b_b, ln_g, ln_b,
                    w_a, w_b, w_o, g1post, tm, ex=None):
    t, d = x.shape
    n_steps = t // tm
    ka, kb = conv_a_w.shape[0], conv_b_w.shape[0]
    off_a = [HALO_A - (ka - 1) // 2 + k for k in range(ka)]
    off_b = [HALO_B - (kb - 1) // 2 + k for k in range(kb)]
    plan_a, plan_b = _shift_plan(off_a), _shift_plan(off_b)

    def body(uap, uac, uan, ubp, ubc, ubn, bg_ref, sa_ref, sb_ref, x_ref, wa_c, ba_c, wb_c, bb_c,
             lng, lnb, wa_hbm, wb_hbm, wo_hbm, g_ref,
             x1_ref, va_ref, pa_ref, cb_ref, sbo_ref, ya_ref, yb_ref, mg_ref, mix_ref,
             ext_a, ext_b, sh_a, sh_b, wa8, wb8, wa_v, wb_v, wo_v, y0, y1, sem):
        i = pl.program_id(0)
        _fill_tap_rows(wa_c, wa8)
        _fill_tap_rows(wb_c, wb8)
        _load_once(wa_hbm, wa_v, sem.at[0])
        _load_once(wb_hbm, wb_v, sem.at[1])
        _load_once(wo_hbm, wo_v, sem.at[2])
        _fill_ext(ext_a, uap, uac, uan, HALO_A, tm, i, n_steps)
        _fill_ext(ext_b, ubp, ubc, ubn, HALO_B, tm, i, n_steps)
        _fill_shifted(ext_a, sh_a, plan_a)
        _fill_shifted(ext_b, sh_b, plan_b)

        for r0 in range(0, tm, CONV_ROWS):
            rows = pl.ds(r0, CONV_ROWS)
            va = jnp.broadcast_to(ba_c[...], (CONV_ROWS, d))
            for k in range(ka):
                va = va + _tap(wa8, k) * _window(ext_a, sh_a, plan_a, off_a[k], r0)
            va_ref[rows, :] = va.astype(BF16)
            pa_ref[rows, :] = (bg_ref[rows, :].astype(F32) * va).astype(BF16)
            cb = jnp.broadcast_to(bb_c[...], (CONV_ROWS, d))
            for k in range(kb):
                cb = cb + _tap(wb8, k) * _window(ext_b, sh_b, plan_b, off_b[k], r0)
            cb_ref[rows, :] = cb
            mu = _mean_lanes(cb)
            cen = cb - mu
            rstd = lax.rsqrt(_mean_lanes(cen * cen) + LN_EPS)
            ln = cen * rstd * lng[...] + lnb[...]
            sbo_ref[rows, :] = (ln * jax.nn.sigmoid(ln)).astype(BF16)

        y0[...] = jnp.dot(pa_ref[...], wa_v[...], preferred_element_type=F32)
        y1[...] = jnp.dot(sbo_ref[...], wb_v[...], preferred_element_type=F32)

        def merge(rows):
            ya, yb = y0[rows, :], y1[rows, :]
            ya_ref[rows, :] = ya.astype(BF16)
            yb_ref[rows, :] = yb.astype(BF16)
            mg_ref[rows, :] = (sa_ref[rows, :].astype(F32) * ya + sb_ref[rows, :].astype(F32) * yb).astype(BF16)
        _for_chunks(tm, ROW_CHUNK, merge)

        mix_ref[...] = jnp.dot(mg_ref[...], wo_v[...], preferred_element_type=F32)

        def resid(rows):
            mix = mix_ref[rows, :]
            r = lax.rsqrt(_mean_lanes(mix * mix) + RMS_EPS)
            x1_ref[rows, :] = x_ref[rows, :] + mix * r * g_ref[...]
        _for_chunks(tm, ROW_CHUNK, resid)

    bf = jax.ShapeDtypeStruct((t, d), BF16)
    f32 = jax.ShapeDtypeStruct((t, d), F32)
    return _call(
        body, name="mixer_fwd", grid=(n_steps,),
        in_specs=[_halo_prev(tm, HALO_A, d), _rows(tm, d), _halo_next(tm, HALO_A, d, t),
                  _halo_prev(tm, HALO_B, d), _rows(tm, d), _halo_next(tm, HALO_B, d, t),
                  _rows(tm, d), _rows(tm, d), _rows(tm, d), _rows(tm, d),
                  _const((ka, d)), _const((1, d)), _const((kb, d)), _const((1, d)),
                  _const((1, d)), _const((1, d)), ANY, ANY, ANY, _const((1, d))],
        out_specs=[_rows(tm, d)] * 9,
        out_shape=[f32, bf, bf, f32, bf, bf, bf, bf, f32],
        scratch_shapes=[pltpu.VMEM((tm + 2 * HALO_A, d), F32), pltpu.VMEM((tm + 2 * HALO_B, d), F32),
                        pltpu.VMEM((len(plan_a), _shifted_rows(tm, off_a), d), F32),
                        pltpu.VMEM((len(plan_b), _shifted_rows(tm, off_b), d), F32),
                        pltpu.VMEM((ka * SUBLANES, d), F32), pltpu.VMEM((kb * SUBLANES, d), F32),
                        pltpu.VMEM((d, d), BF16), pltpu.VMEM((d, d), BF16), pltpu.VMEM((d, d), BF16),
                        pltpu.VMEM((tm, d), F32), pltpu.VMEM((tm, d), F32),
                        pltpu.SemaphoreType.DMA((3,))],
        args=(ua, ua, ua, ub, ub, ub, bg, sa, sb, x, conv_a_w, conv_a_b, conv_b_w, conv_b_b,
              ln_g, ln_b, w_a, w_b, w_o, g1post), ex=ex)


def _mlp_call(x1, target, mix, g2pre, g2post, g1post, w1_g, w2, tm, ex=None):
    t, d = x1.shape
    nb, _, fq = w1_g.shape
    f = nb * fq
    n_steps = t // tm
    inv_d = 1.0 / d

    def body(x1_ref, t_ref, mix_ref, gpre, gpost, gmix, w1_hbm, w2_hbm,
             dx1_ref, f_ref, df2_ref, h2_ref, df1_ref, dmix_ref, dgpost_ref, dgpre_ref, dgmix_ref, loss_ref,
             w1_v, w2_v, f1_s, blk_s, f2_s, acc_post, acc_pre, acc_mix, acc_loss, sem):
        _load_blocks_once(w1_hbm, w1_v, sem)
        _load_once(w2_hbm, w2_v, sem.at[nb])

        @pl.when(pl.program_id(0) == 0)
        def _():
            acc_post[...] = jnp.zeros_like(acc_post)
            acc_pre[...] = jnp.zeros_like(acc_pre)
            acc_mix[...] = jnp.zeros_like(acc_mix)
            acc_loss[...] = jnp.zeros_like(acc_loss)

        def norm(rows):
            xv = x1_ref[rows, :]
            r = lax.rsqrt(_mean_lanes(xv * xv) + RMS_EPS)
            h2_ref[rows, :] = (xv * r * gpre[...]).astype(BF16)
        _for_chunks(tm, ROW_CHUNK, norm)

        for j in range(nb):
            cols = pl.ds(j * fq, fq)
            f1_s[:, cols] = jnp.dot(h2_ref[...], w1_v[:, cols], preferred_element_type=F32)

        def act(rows):
            relu = jnp.maximum(f1_s[rows, :], 0.0)
            f_ref[rows, :] = (relu * relu).astype(BF16)
        _for_chunks(tm, ROW_CHUNK, act)

        f2_s[...] = jnp.dot(f_ref[...], w2_v[...], preferred_element_type=F32)

        def head(rows):
            f2 = f2_s[rows, :]
            rf = lax.rsqrt(_mean_lanes(f2 * f2) + RMS_EPS)
            y = x1_ref[rows, :] + f2 * rf * gpost[...]
            err = y - t_ref[rows, :]
            acc_loss[...] += _fold8(err * err)
            dy = err * inv_d
            gdy = dy * gpost[...]
            df2 = rf * gdy - f2 * (rf * rf * rf * _mean_lanes(gdy * f2))
            df2_ref[rows, :] = df2.astype(BF16)
            acc_post[...] += _fold8(dy * f2 * rf)
            dx1_ref[rows, :] = dy
        _for_chunks(tm, ROW_CHUNK, head)

        for j in range(nb):
            cols = pl.ds(j * fq, fq)
            blk_s[...] = lax.dot_general(df2_ref[...], w2_v[cols, :], NT_DIMS, preferred_element_type=F32)

            def dact(rows):
                relu = jnp.maximum(f1_s[rows, cols], 0.0)
                df1_ref[rows, cols] = (blk_s[rows, :] * (2.0 * relu)).astype(BF16)
            _for_chunks(tm, ROW_CHUNK, dact)

        f2_s[...] = lax.dot_general(df1_ref[...], w1_v[...], NT_DIMS, preferred_element_type=F32)

        def dnorm(rows):
            dh2 = f2_s[rows, :]
            xv = x1_ref[rows, :]
            r = lax.rsqrt(_mean_lanes(xv * xv) + RMS_EPS)
            gd = dh2 * gpre[...]
            dxv = dx1_ref[rows, :] + r * gd - xv * (r * r * r * _mean_lanes(gd * xv))
            dx1_ref[rows, :] = dxv
            acc_pre[...] += _fold8(dh2 * xv * r)
            mix = mix_ref[rows, :]
            rm = lax.rsqrt(_mean_lanes(mix * mix) + RMS_EPS)
            gm = dxv * gmix[...]
            dmix_ref[rows, :] = (rm * gm - mix * (rm * rm * rm * _mean_lanes(gm * mix))).astype(BF16)
            acc_mix[...] += _fold8(dxv * mix * rm)
        _for_chunks(tm, ROW_CHUNK, dnorm)

        _write_row_sums(acc_post, dgpost_ref, n_steps)
        _write_row_sums(acc_pre, dgpre_ref, n_steps)
        _write_row_sums(acc_mix, dgmix_ref, n_steps)
        _write_row_sums(acc_loss, loss_ref, n_steps)

    row = jax.ShapeDtypeStruct((1, d), F32)
    return _call(
        body, name="mlp_fwd_bwd", grid=(n_steps,),
        in_specs=[_rows(tm, d), _rows(tm, d), _rows(tm, d), _const((1, d)), _const((1, d)), _const((1, d)), ANY, ANY],
        out_specs=[_rows(tm, d), _rows(tm, f), _rows(tm, d), _rows(tm, d), _rows(tm, f), _rows(tm, d),
                   _const((1, d)), _const((1, d)), _const((1, d)), _const((1, d))],
        out_shape=[jax.ShapeDtypeStruct((t, d), F32), jax.ShapeDtypeStruct((t, f), BF16),
                   jax.ShapeDtypeStruct((t, d), BF16), jax.ShapeDtypeStruct((t, d), BF16),
                   jax.ShapeDtypeStruct((t, f), BF16), jax.ShapeDtypeStruct((t, d), BF16), row, row, row, row],
        scratch_shapes=[pltpu.VMEM((d, f), BF16), pltpu.VMEM((f, d), BF16),
                        pltpu.VMEM((tm, f), F32), pltpu.VMEM((tm, fq), F32), pltpu.VMEM((tm, d), F32),
                        pltpu.VMEM((SUBLANES, d), F32), pltpu.VMEM((SUBLANES, d), F32),
                        pltpu.VMEM((SUBLANES, d), F32), pltpu.VMEM((SUBLANES, d), F32),
                        pltpu.SemaphoreType.DMA((nb + 1,))],
        args=(x1, target, mix, g2pre, g2post, g1post, w1_g, w2), ex=ex)


def _mixer_bwd_call(dmix, sa, sb, ya, yb, bg, va, cb, ln_g, ln_b, w_a, w_b, w_o, tm, ex=None):
    t, d = dmix.shape
    n_steps = t // tm

    def body(dmix_ref, sa_ref, sb_ref, ya_ref, yb_ref, bg_ref, va_ref, cb_ref, lng, lnb,
             wa_hbm, wb_hbm, wo_hbm,
             dya_ref, dyb_ref, dva_ref, dcb_ref, dbg_ref, dza_ref, dzb_ref,
             dlng_ref, dlnb_ref, dba_ref, dbb_ref, sbg_ref, sza_ref, szb_ref,
             wa_v, wb_v, wo_v, s0, s1, acc_lng, acc_lnb, acc_ba, acc_bb, acc_bg, acc_za, acc_zb, sem):
        _load_once(wa_hbm, wa_v, sem.at[0])
        _load_once(wb_hbm, wb_v, sem.at[1])
        _load_once(wo_hbm, wo_v, sem.at[2])
        accs = (acc_lng, acc_lnb, acc_ba, acc_bb, acc_bg, acc_za, acc_zb)

        @pl.when(pl.program_id(0) == 0)
        def _():
            for acc in accs:
                acc[...] = jnp.zeros_like(acc)

        s0[...] = lax.dot_general(dmix_ref[...], wo_v[...], NT_DIMS, preferred_element_type=F32)

        def dmerge(rows):
            dm = s0[rows, :]
            sav, sbv = sa_ref[rows, :].astype(F32), sb_ref[rows, :].astype(F32)
            dya_ref[rows, :] = (dm * sav).astype(BF16)
            dyb_ref[rows, :] = (dm * sbv).astype(BF16)
            dza = dm * ya_ref[rows, :].astype(F32) * sav * (1.0 - sav)
            dzb = dm * yb_ref[rows, :].astype(F32) * sbv * (1.0 - sbv)
            dza_ref[rows, :] = dza.astype(BF16)
            dzb_ref[rows, :] = dzb.astype(BF16)
            acc_za[...] += _fold8(dza)
            acc_zb[...] += _fold8(dzb)
        _for_chunks(tm, ROW_CHUNK, dmerge)

        s0[...] = lax.dot_general(dya_ref[...], wa_v[...], NT_DIMS, preferred_element_type=F32)
        s1[...] = lax.dot_general(dyb_ref[...], wb_v[...], NT_DIMS, preferred_element_type=F32)

        def dbranches(rows):
            dpa = s0[rows, :]
            dbg = dpa * va_ref[rows, :].astype(F32)
            dbg_ref[rows, :] = dbg.astype(BF16)
            acc_bg[...] += _fold8(dbg)
            dva = dpa * bg_ref[rows, :].astype(F32)
            dva_ref[rows, :] = dva
            acc_ba[...] += _fold8(dva)
            cbv = cb_ref[rows, :]
            mu = _mean_lanes(cbv)
            cen = cbv - mu
            rstd = lax.rsqrt(_mean_lanes(cen * cen) + LN_EPS)
            xhat = cen * rstd
            ln = xhat * lng[...] + lnb[...]
            sig = jax.nn.sigmoid(ln)
            dln = s1[rows, :] * (sig * (1.0 + ln * (1.0 - sig)))
            acc_lng[...] += _fold8(dln * xhat)
            acc_lnb[...] += _fold8(dln)
            dxh = dln * lng[...]
            dcb = rstd * (dxh - _mean_lanes(dxh) - xhat * _mean_lanes(dxh * xhat))
            dcb_ref[rows, :] = dcb
            acc_bb[...] += _fold8(dcb)
        _for_chunks(tm, ROW_CHUNK, dbranches)

        _write_row_sums(acc_lng, dlng_ref, n_steps)
        _write_row_sums(acc_lnb, dlnb_ref, n_steps)
        _write_row_sums(acc_ba, dba_ref, n_steps)
        _write_row_sums(acc_bb, dbb_ref, n_steps)
        _write_row_sums(acc_bg, sbg_ref, n_steps)
        _write_row_sums(acc_za, sza_ref, n_steps)
        _write_row_sums(acc_zb, szb_ref, n_steps)

    bf = jax.ShapeDtypeStruct((t, d), BF16)
    f32 = jax.ShapeDtypeStruct((t, d), F32)
    row = jax.ShapeDtypeStruct((1, d), F32)
    return _call(
        body, name="mixer_bwd", grid=(n_steps,),
        in_specs=[_rows(tm, d)] * 8 + [_const((1, d))] * 2 + [ANY, ANY, ANY],
        out_specs=[_rows(tm, d)] * 7 + [_const((1, d))] * 7,
        out_shape=[bf, bf, f32, f32, bf, bf, bf] + [row] * 7,
        scratch_shapes=[pltpu.VMEM((d, d), BF16), pltpu.VMEM((d, d), BF16), pltpu.VMEM((d, d), BF16),
                        pltpu.VMEM((tm, d), F32), pltpu.VMEM((tm, d), F32)]
        + [pltpu.VMEM((SUBLANES, d), F32)] * 7 + [pltpu.SemaphoreType.DMA((3,))],
        args=(dmix, sa, sb, ya, yb, bg, va, cb, ln_g, ln_b, w_a, w_b, w_o), ex=ex)


def _conv_bwd_call(dva, dcb, ua, ub, cg, ha, a, sg, dbg, dza, dzb, through_sums, conv_a_w, conv_b_w, tm, ex=None):
    t, d = dva.shape
    n_steps = t // tm
    ka, kb = conv_a_w.shape[0], conv_b_w.shape[0]
    off_a = [HALO_A + (ka - 1) // 2 - k for k in range(ka)]
    off_b = [HALO_B + (kb - 1) // 2 - k for k in range(kb)]
    plan_a, plan_b = _shift_plan(off_a), _shift_plan(off_b)

    def body(dvap, dvac, dvan, dcbp, dcbc, dcbn, ua_ref, ub_ref,
             cg_ref, ha_ref, a_ref, sg_ref, dbg_ref, dza_ref, dzb_ref, wa_c, wb_c, sbg_ref, sza_ref, szb_ref,
             dproj_ref, dwa_ref, dwb_ref, dbin_ref,
             e_dva, e_dcb, sh_a, sh_b, wa8, wb8, acc_wa, acc_wb, acc_bin):
        i = pl.program_id(0)
        _fill_tap_rows(wa_c, wa8)
        _fill_tap_rows(wb_c, wb8)

        @pl.when(i == 0)
        def _():
            acc_wa[...] = jnp.zeros_like(acc_wa)
            acc_wb[...] = jnp.zeros_like(acc_wb)
            acc_bin[...] = jnp.zeros_like(acc_bin)
            for col, s_ref in ((0, sbg_ref), (5, sza_ref), (6, szb_ref)):
                acc_bin[0:1, pl.ds(col * d, d)] = s_ref[...]

        _fill_ext(e_dva, dvap, dvac, dvan, HALO_A, tm, i, n_steps)
        _fill_ext(e_dcb, dcbp, dcbc, dcbn, HALO_B, tm, i, n_steps)
        _fill_shifted(e_dva, sh_a, plan_a)
        _fill_shifted(e_dcb, sh_b, plan_b)

        def put(col, rows, val_f32):
            dproj_ref[rows, pl.ds(col * d, d)] = val_f32.astype(BF16)
            acc_bin[:, pl.ds(col * d, d)] += _fold8(val_f32)

        for r0 in range(0, tm, CONV_ROWS):
            rows = pl.ds(r0, CONV_ROWS)
            ua_c, ub_c = ua_ref[rows, :], ub_ref[rows, :]
            dua = jnp.zeros((CONV_ROWS, d), F32)
            for k in range(ka):
                xk = _window(e_dva, sh_a, plan_a, off_a[k], r0)
                dua = dua + _tap(wa8, k) * xk
                acc_wa[pl.ds(k * SUBLANES, SUBLANES), :] += _fold8(ua_c * xk)
            dub = jnp.zeros((CONV_ROWS, d), F32)
            for k in range(kb):
                xk = _window(e_dcb, sh_b, plan_b, off_b[k], r0)
                dub = dub + _tap(wb8, k) * xk
                acc_wb[pl.ds(k * SUBLANES, SUBLANES), :] += _fold8(ub_c * xk)
            cgv, hav = cg_ref[rows, :].astype(F32), ha_ref[rows, :].astype(F32)
            av, sgv = a_ref[rows, :].astype(F32), sg_ref[rows, :].astype(F32)
            put(1, rows, dua * hav)
            put(2, rows, dua * cgv)
            put(3, rows, dub * sgv)
            put(4, rows, dub * av * sgv * (1.0 - sgv))
            for col, through in ((0, dbg_ref), (5, dza_ref), (6, dzb_ref)):
                dproj_ref[rows, pl.ds(col * d, d)] = through[rows, :]

        @pl.when(i == n_steps - 1)
        def _():
            for k in range(ka):
                dwa_ref[k:k + 1, :] = jnp.sum(acc_wa[pl.ds(k * SUBLANES, SUBLANES), :], axis=0, keepdims=True)
            for k in range(kb):
                dwb_ref[k:k + 1, :] = jnp.sum(acc_wb[pl.ds(k * SUBLANES, SUBLANES), :], axis=0, keepdims=True)
            dbin_ref[...] = jnp.sum(acc_bin[...], axis=0, keepdims=True)

    halo_a = [_halo_prev(tm, HALO_A, d), _rows(tm, d), _halo_next(tm, HALO_A, d, t)]
    halo_b = [_halo_prev(tm, HALO_B, d), _rows(tm, d), _halo_next(tm, HALO_B, d, t)]
    return _call(
        body, name="conv_bwd", grid=(n_steps,),
        in_specs=halo_a + halo_b + [_rows(tm, d)] * 9 + [_const((ka, d)), _const((kb, d))] + [_const((1, d))] * 3,
        out_specs=[_rows(tm, 7 * d), _const((ka, d)), _const((kb, d)), _const((1, 7 * d))],
        out_shape=[jax.ShapeDtypeStruct((t, 7 * d), BF16), jax.ShapeDtypeStruct((ka, d), F32),
                   jax.ShapeDtypeStruct((kb, d), F32), jax.ShapeDtypeStruct((1, 7 * d), F32)],
        scratch_shapes=[pltpu.VMEM((tm + 2 * HALO_A, d), F32), pltpu.VMEM((tm + 2 * HALO_B, d), F32),
                        pltpu.VMEM((len(plan_a), _shifted_rows(tm, off_a), d), F32),
                        pltpu.VMEM((len(plan_b), _shifted_rows(tm, off_b), d), F32),
                        pltpu.VMEM((ka * SUBLANES, d), F32), pltpu.VMEM((kb * SUBLANES, d), F32),
                        pltpu.VMEM((ka * SUBLANES, d), F32), pltpu.VMEM((kb * SUBLANES, d), F32),
                        pltpu.VMEM((SUBLANES, 7 * d), F32)],
        args=(dva, dva, dva, dcb, dcb, dcb, ua, ub, cg, ha, a, sg, dbg, dza, dzb,
              conv_a_w, conv_b_w, *through_sums), ex=ex)


def _dx_call(dproj, x, dx1, g1pre, w_in_g, tm, first, n_steps, prev, ex=None):
    t, d = x.shape
    nb, _, n4 = w_in_g.shape
    ni = nb * n4
    rows = lambda width: pl.BlockSpec((tm, width), lambda i: (i + first, 0))
    if prev is None:
        prev = (jnp.zeros((SUBLANES, 128), F32), jnp.zeros((1, d), F32))
    prev_dx, prev_dg = prev

    def body(dp_ref, x_ref, dx1_ref, g_ref, w_hbm, prev_dx_hbm, prev_dg_ref, dx_ref, dg_ref, w_v, dh_s, acc_g, sem):
        _load_blocks_once(w_hbm, w_v, sem)

        @pl.when(pl.program_id(0) == 0)
        def _():
            acc_g[...] = jnp.zeros_like(acc_g)
            acc_g[0:1, :] = prev_dg_ref[...]

        dh_s[...] = lax.dot_general(dp_ref[...], w_v[...], NT_DIMS, preferred_element_type=F32)

        def dnorm(rows):
            dh = dh_s[rows, :]
            xv = x_ref[rows, :]
            r = lax.rsqrt(_mean_lanes(xv * xv) + RMS_EPS)
            gd = dh * g_ref[...]
            dx_ref[rows, :] = dx1_ref[rows, :] + r * gd - xv * (r * r * r * _mean_lanes(gd * xv))
            acc_g[...] += _fold8(dh * xv * r)
        _for_chunks(tm, ROW_CHUNK, dnorm)
        _write_row_sums(acc_g, dg_ref, n_steps)

    return _call(
        body, name="dx_bwd_from_%d" % first, grid=(n_steps,),
        in_specs=[rows(ni), rows(d), rows(d), _const((1, d)), ANY, ANY, _const((1, d))],
        out_specs=[rows(d), _const((1, d))],
        out_shape=[jax.ShapeDtypeStruct((t, d), F32), jax.ShapeDtypeStruct((1, d), F32)],
        scratch_shapes=[pltpu.VMEM((d, ni), BF16), pltpu.VMEM((tm, d), F32),
                        pltpu.VMEM((SUBLANES, d), F32), pltpu.SemaphoreType.DMA((nb,))],
        args=(dproj, x, dx1, g1pre, w_in_g, prev_dx, prev_dg), ex=ex,
        aliases={5: 0} if first > 0 else None)


def _tn_matmul(a, g, nblk, a_cols, g_cols, a_blocked, g_blocked, tt, name, ex=None):
    t = a.shape[0]

    def body(a_ref, g_ref, o_ref):
        @pl.when(pl.program_id(1) == 0)
        def _():
            o_ref[...] = jnp.zeros_like(o_ref)
        o_ref[0] += lax.dot_general(a_ref[...], g_ref[...], TN_DIMS, preferred_element_type=F32)

    (out,), xouts = _call(
        body, name=name, grid=(nblk, t // tt),
        in_specs=[pl.BlockSpec((tt, a_cols), (lambda b, s: (s, b)) if a_blocked else (lambda b, s: (s, 0))),
                  pl.BlockSpec((tt, g_cols), (lambda b, s: (s, b)) if g_blocked else (lambda b, s: (s, 0)))],
        out_specs=[pl.BlockSpec((1, a_cols, g_cols), lambda b, s: (b, 0, 0))],
        out_shape=[jax.ShapeDtypeStruct((nblk, a_cols, g_cols), F32)],
        scratch_shapes=[], args=(a, g), ex=ex)
    return out, xouts


def _tn_matmuls(pairs, tt, name):
    t, d = pairs[0][0].shape
    k = len(pairs)

    def body(*refs):
        ins, outs = refs[:2 * k], refs[2 * k:]

        @pl.when(pl.program_id(0) == 0)
        def _():
            for o_ref in outs:
                o_ref[...] = jnp.zeros_like(o_ref)
        for j in range(k):
            outs[j][...] += lax.dot_general(ins[2 * j][...], ins[2 * j + 1][...], TN_DIMS, preferred_element_type=F32)

    outs, _ = _call(
        body, name=name, grid=(t // tt,), in_specs=[_rows(tt, d)] * (2 * k), out_specs=[_const((d, d))] * k,
        out_shape=[jax.ShapeDtypeStruct((d, d), F32)] * k, scratch_shapes=[],
        args=[m for pair in pairs for m in pair])
    return outs


def _pair_sum_call(g_full, from_sibling, core, name):
    nblk, r, c = g_full.shape
    hr = r // 2
    tr = min(hr, 256)
    n = hr // tr

    def body(core_ref, g_ref, p_ref, o_ref):
        o_ref[...] = (g_ref[...] + p_ref[...]).astype(BF16)

    return pl.pallas_call(
        body, name=name,
        grid_spec=pltpu.PrefetchScalarGridSpec(
            num_scalar_prefetch=1, grid=(nblk, n),
            in_specs=[pl.BlockSpec((1, tr, c), lambda j, i, cr: (j, cr[0] * n + i, 0)),
                      pl.BlockSpec((1, tr, c), lambda j, i, cr: (j, i, 0))],
            out_specs=pl.BlockSpec((1, tr, c), lambda j, i, cr: (j, i, 0))),
        out_shape=jax.ShapeDtypeStruct((nblk, hr, c), BF16),
        compiler_params=_params("parallel", "parallel"))(core, g_full, from_sibling)


def _chip_sum_call(pair, received, chip_core, name):
    _, hr, c = pair.shape
    tr = min(hr, 256)
    n = hr // tr

    def body(cc_ref, own_ref, r_ref, o_ref):
        o_ref[...] = ((own_ref[0].astype(F32) + r_ref[0].astype(F32)) + r_ref[1].astype(F32)) + r_ref[2].astype(F32)

    return pl.pallas_call(
        body, name=name,
        grid_spec=pltpu.PrefetchScalarGridSpec(
            num_scalar_prefetch=1, grid=(n,),
            in_specs=[pl.BlockSpec((1, tr, c), lambda i, cc: (cc[0], i, 0)),
                      pl.BlockSpec((N_CHIPS - 1, tr, c), lambda i, cc: (0, i, 0))],
            out_specs=pl.BlockSpec((tr, c), lambda i, cc: (cc[1] * n + i, 0))),
        out_shape=jax.ShapeDtypeStruct((2 * hr, c), F32),
        compiler_params=_params("parallel"))(chip_core, pair, received)


def _adamw(w, g, m, v):
    m = ADAM_B1 * m + (1.0 - ADAM_B1) * g
    v = ADAM_B2 * v + (1.0 - ADAM_B2) * (g * g)
    m_hat = m / (1.0 - ADAM_B1 ** ADAM_STEP)
    v_hat = v / (1.0 - ADAM_B2 ** ADAM_STEP)
    delta = -ADAM_LR * (m_hat / (jnp.sqrt(v_hat) + ADAM_EPS) + ADAM_WD * w)
    return delta, m, v


def _adam_call(w, g, m, v, name):
    r, c = w.shape
    tr = min(r, 256)

    def body(w_ref, g_ref, m_ref, v_ref, go_ref, d_ref, mo_ref, vo_ref):
        go_ref[...] = g_ref[...]
        d_ref[...], mo_ref[...], vo_ref[...] = _adamw(w_ref[...], g_ref[...], m_ref[...], v_ref[...])

    shape = jax.ShapeDtypeStruct((r, c), F32)
    return pl.pallas_call(
        body, name=name, grid=(r // tr,), in_specs=[_rows(tr, c)] * 4, out_specs=[_rows(tr, c)] * 4,
        out_shape=[shape] * 4, compiler_params=_params("parallel"))(w, g, m, v)


def _adam_sc_call(w, g, m, v, name):
    r, c = w.shape
    rows_tile = r // SC_TILES
    rr = min(rows_tile, SC_ROWS)

    def body(w_hbm, g_hbm, m_hbm, v_hbm, go_hbm, d_hbm, mo_hbm, vo_hbm, wb, gb, mb, vb):
        tile = lax.axis_index("sc_tile") * 2 + lax.axis_index("sc_core")

        @pl.loop(0, rows_tile, step=rr)
        def _(p):
            rows = pl.ds(tile * rows_tile + p, rr)
            pltpu.sync_copy(w_hbm.at[rows, :], wb)
            pltpu.sync_copy(g_hbm.at[rows, :], gb)
            pltpu.sync_copy(m_hbm.at[rows, :], mb)
            pltpu.sync_copy(v_hbm.at[rows, :], vb)

            @pl.loop(0, rr)
            def _(i):
                @pl.loop(0, c, step=SC_LANES)
                def _(j):
                    at = (i, pl.ds(j, SC_LANES))
                    delta, m2, v2 = _adamw(wb[at], gb[at], mb[at], vb[at])
                    wb[at] = delta
                    mb[at] = m2
                    vb[at] = v2

            pltpu.sync_copy(gb, go_hbm.at[rows, :])
            pltpu.sync_copy(wb, d_hbm.at[rows, :])
            pltpu.sync_copy(mb, mo_hbm.at[rows, :])
            pltpu.sync_copy(vb, vo_hbm.at[rows, :])

    shape = jax.ShapeDtypeStruct((r, c), F32)
    return pl.kernel(
        body, name=name, out_type=[shape] * 4,
        mesh=plsc.VectorSubcoreMesh(core_axis_name="sc_core", subcore_axis_name="sc_tile"),
        scratch_types=[pltpu.VMEM((rr, c), F32)] * 4)(w, g, m, v)


def _gather_first_call(buf, conv_a_w, conv_b_w, d):
    ka, dq = conv_a_w.shape
    kb = conv_b_w.shape[0]
    ra = -(-ka // SUBLANES) * SUBLANES
    rb = -(-kb // SUBLANES) * SUBLANES
    a_pad = jnp.pad(conv_a_w, ((0, ra - ka), (0, 0)))
    b_pad = jnp.pad(conv_b_w, ((0, rb - kb), (0, 0)))
    hr = buf.shape[1] // 2
    qr = hr // 2

    def body(w_in, a_ref, b_ref, w_out, oa_ref, ob_ref, pack, slots, send, recv, csend, crecv):
        x, y, c = _place()
        me, x_chip, y_chip, d_chip = 2 * x + y, 2 * (1 - x) + y, 2 * x + (1 - y), 2 * (1 - x) + (1 - y)
        x_nbr, y_nbr, sibling = (1 - x, y, c), (x, 1 - y, c), (x, y, 1 - c)
        mine, theirs = pl.ds(c * hr, hr), pl.ds((1 - c) * hr, hr)
        first, second = pl.ds(c * hr, qr), pl.ds(c * hr + qr, qr)

        def rows_of(chip, rows):
            return w_out.at[chip, rows, :]

        def copy(k, src, dst, peer):
            return _remote(src, dst, send.at[k], recv.at[k], peer)

        def landed(k, dst):
            copy(k, dst, dst, sibling).wait_recv()

        pack[pl.ds(0, ra), :] = a_ref[...]
        pack[pl.ds(ra, rb), :] = b_ref[...]
        chips = _other_chips(x, y)
        conv = [_remote(pack, slots.at[me], csend.at[j], crecv.at[j], (px, py, c)) for j, (px, py, _) in enumerate(chips)]
        started = [copy(0, w_in.at[me, mine, :], rows_of(me, mine), x_nbr),
                   copy(1, w_in.at[me, mine, :], rows_of(me, mine), y_nbr)]
        for cp in conv + started:
            cp.start()

        def go(cp):
            cp.start()
            started.append(cp)

        landed(0, rows_of(x_chip, mine))
        go(copy(2, rows_of(x_chip, first), rows_of(x_chip, first), y_nbr))
        go(copy(4, rows_of(x_chip, mine), rows_of(x_chip, mine), sibling))
        landed(1, rows_of(y_chip, mine))
        go(copy(3, rows_of(y_chip, second), rows_of(y_chip, second), x_nbr))
        go(copy(5, rows_of(y_chip, mine), rows_of(y_chip, mine), sibling))
        landed(2, rows_of(d_chip, first))
        landed(3, rows_of(d_chip, second))
        go(copy(6, rows_of(d_chip, mine), rows_of(d_chip, mine), sibling))
        for k, chip in ((4, x_chip), (5, y_chip), (6, d_chip)):
            landed(k, rows_of(chip, theirs))
        for cp in started:
            cp.wait_send()

        for j, (px, py, pk) in enumerate(chips):
            _remote(pack, slots.at[pk], csend.at[j], crecv.at[j], (px, py, c)).wait_recv()
        for cp in conv:
            cp.wait_send()
        slots[me] = pack[...]
        for k in range(N_CHIPS):
            oa_ref[:, pl.ds(k * dq, dq)] = slots[k, pl.ds(0, ra), :]
            ob_ref[:, pl.ds(k * dq, dq)] = slots[k, pl.ds(ra, rb), :]

    n_w = 7
    w_g, oa, ob = pl.pallas_call(
        body, name="gather_first", in_specs=[ANY, VMEM_FULL, VMEM_FULL], out_specs=[ANY, VMEM_FULL, VMEM_FULL],
        out_shape=[_sds(buf), jax.ShapeDtypeStruct((ra, d), F32), jax.ShapeDtypeStruct((rb, d), F32)],
        scratch_shapes=[pltpu.VMEM((ra + rb, dq), F32), pltpu.VMEM((N_CHIPS, ra + rb, dq), F32),
                        pltpu.SemaphoreType.DMA((n_w,)), pltpu.SemaphoreType.DMA((n_w,)),
                        pltpu.SemaphoreType.DMA((N_CHIPS - 1,)), pltpu.SemaphoreType.DMA((N_CHIPS - 1,))],
        input_output_aliases={0: 0},
        compiler_params=pltpu.CompilerParams(has_side_effects=True))(buf, a_pad, b_pad)
    return w_g, oa[:ka], ob[:kb]


def _small_step_call(partials, loss_rows, weights, m_s, v_s, sharded, d):
    n = len(partials)
    row_counts = [p.shape[0] for p in partials]
    starts = [sum(row_counts[:i]) for i in range(n)]
    loss_row = sum(row_counts)
    pack_rows = -(-(loss_row + 1) // SUBLANES) * SUBLANES
    dq = d // N_CHIPS

    def body(*refs):
        p_refs = refs[:n]
        loss_in = refs[n]
        w_refs = refs[n + 1:2 * n + 1]
        m_refs = refs[2 * n + 1:3 * n + 1]
        v_refs = refs[3 * n + 1:4 * n + 1]
        o = 4 * n + 1
        g_out = refs[o:o + n]
        d_out = refs[o + n:o + 2 * n]
        m_out = refs[o + 2 * n:o + 3 * n]
        v_out = refs[o + 3 * n:o + 4 * n]
        loss_out = refs[o + 4 * n]
        pack, from_sibling, slots, send_sem, recv_sem = refs[o + 4 * n + 1:]
        x, y, c = _place()
        me = 2 * x + y

        pack[...] = jnp.zeros_like(pack)
        for i in range(n):
            pack[pl.ds(starts[i], row_counts[i]), :] = p_refs[i][...]
        pack[pl.ds(loss_row, 1), :] = loss_in[...]

        pair = _remote(pack, from_sibling, send_sem.at[0], recv_sem.at[0], (x, y, 1 - c))
        pair.start()
        pair.wait()
        pack[...] = pack[...] + from_sibling[...]
        chips = _other_chips(x, y)
        copies = [_remote(pack, slots.at[me], send_sem.at[1 + j], recv_sem.at[1 + j], (px, py, c))
                  for j, (px, py, _) in enumerate(chips)]
        for cp in copies:
            cp.start()
        for j, (px, py, pk) in enumerate(chips):
            _remote(pack, slots.at[pk], send_sem.at[1 + j], recv_sem.at[1 + j], (px, py, c)).wait_recv()
        for cp in copies:
            cp.wait_send()

        slots[me] = pack[...]
        total = slots[0]
        for k in range(1, N_CHIPS):
            total = total + slots[k]
        pack[...] = total

        loss_out[...] = jnp.broadcast_to(
            (0.5 / d) * jnp.sum(pack[pl.ds(loss_row, 1), :], axis=-1, keepdims=True), loss_out.shape)
        chip = 2 * x + y
        for i in range(n):
            rows = pl.ds(starts[i], row_counts[i])
            if sharded[i]:
                for k in range(N_CHIPS):
                    @pl.when(chip == k)
                    def _():
                        g_out[i][...] = pack[rows, pl.ds(k * dq, dq)]
            else:
                g_out[i][...] = pack[rows, :]
            d_out[i][...], m_out[i][...], v_out[i][...] = _adamw(
                w_refs[i][...], g_out[i][...], m_refs[i][...], v_refs[i][...])

    w_shapes = [jax.ShapeDtypeStruct(w.shape, F32) for w in weights]
    n_in = 4 * n + 1
    return pl.pallas_call(
        body, name="small_grads_allreduce_adamw",
        in_specs=[VMEM_FULL] * n_in, out_specs=[VMEM_FULL] * (4 * n + 1),
        out_shape=w_shapes * 4 + [jax.ShapeDtypeStruct((SUBLANES, 128), F32)],
        scratch_shapes=[pltpu.VMEM((pack_rows, d), F32), pltpu.VMEM((pack_rows, d), F32),
                        pltpu.VMEM((N_CHIPS, pack_rows, d), F32),
                        pltpu.SemaphoreType.DMA((N_CHIPS,)), pltpu.SemaphoreType.DMA((N_CHIPS,))],
        compiler_params=pltpu.CompilerParams(has_side_effects=True, vmem_limit_bytes=VMEM_LIMIT))(
            *partials, loss_rows, *weights, *m_s, *v_s)


def _tile(t, want):
    return min(t, want)


def kernel(x, norm1_pre_g, w_in, b_in, conv_a_w, conv_a_b, w_a_out, conv_b_w, conv_b_b, ln_b_g, ln_b_b, w_b_out, w_o, norm1_post_g, norm2_pre_g, w_mlp_in, w_mlp_out, norm2_post_g, loss_target, m_norm1_pre_g, m_w_in, m_b_in, m_conv_a_w, m_conv_a_b, m_w_a_out, m_conv_b_w, m_conv_b_b, m_ln_b_g, m_ln_b_b, m_w_b_out, m_w_o, m_norm1_post_g, m_norm2_pre_g, m_w_mlp_in, m_w_mlp_out, m_norm2_post_g, v_norm1_pre_g, v_w_in, v_b_in, v_conv_a_w, v_conv_a_b, v_w_a_out, v_conv_b_w, v_conv_b_b, v_ln_b_g, v_ln_b_b, v_w_b_out, v_w_o, v_norm1_post_g, v_norm2_pre_g, v_w_mlp_in, v_w_mlp_out, v_norm2_post_g):
    _, t, d = x.shape
    xt = x.reshape(t, d)
    tgt = loss_target.reshape(t, d)
    row = lambda vec: vec.reshape(1, -1)
    cx, cy, cc = _place()
    core = cc.astype(jnp.int32).reshape(1)
    chip = (2 * cx + cy).astype(jnp.int32).reshape(1)

    big = dict(w_in=w_in, w_a_out=w_a_out, w_b_out=w_b_out, w_o=w_o, w_mlp_in=w_mlp_in, w_mlp_out=w_mlp_out)
    names = list(big)
    chip_core = jnp.concatenate([chip, core])
    slot = {k: _cast_to_slot(big[k], chip, "cast_" + k) for k in names}
    mixer_w, mlp_w = ["w_a_out", "w_b_out", "w_o"], ["w_mlp_in", "w_mlp_out"]
    rows_of = lambda buf: buf.reshape(-1, buf.shape[-1])

    def pair_sums(keys, full, from_sibling):
        return [_pair_sum_call(g, p, core, "pair_sum_" + k) for k, g, p in zip(keys, full, from_sibling)]

    def chip_sums(keys, pairs, received):
        return [_chip_sum_call(p, r, chip_core, "chip_sum_" + k) for k, p, r in zip(keys, pairs, received)]

    w_in_g, conv_a_full, conv_b_full = _gather_first_call(slot["w_in"], conv_a_w, conv_b_w, d)

    g1pre, g1post, g2pre, g2post = row(norm1_pre_g), row(norm1_post_g), row(norm2_pre_g), row(norm2_post_g)
    lng, lnb, ba, bb = row(ln_b_g), row(ln_b_b), row(conv_a_b), row(conv_b_b)

    (h, ua, ub, bg, cg, ha, a, sg, sa, sb), landed = _proj_call(
        xt, g1pre, w_in_g, row(b_in), _tile(t, 512), ex=_ex_gather_ici([slot[k] for k in mixer_w + mlp_w]))
    w_a_g, w_b_g, w_o_g = _exchange_call("forward_mixer_weights", [_ex_gather_forward(landed[:3])])
    w_a_full, w_b_full, w_o_full = rows_of(w_a_g), rows_of(w_b_g), rows_of(w_o_g)
    (x1, va, pa, cb, sbo, ya, yb, mg, mix), (w1_g, w2_g) = _mixer_fwd_call(
        ua, ub, bg, sa, sb, xt, conv_a_full, ba, conv_b_full, bb, lng, lnb,
        w_a_full, w_b_full, w_o_full, g1post, _tile(t, 256), ex=_ex_gather_forward(landed[3:]))
    (dx1, f, df2, h2, df1, dmix, dg2post, dg2pre, dg1post, loss_rows), _ = _mlp_call(
        x1, tgt, mix, g2pre, g2post, g1post, w1_g, rows_of(w2_g), _tile(t, 256))

    tt = _tile(t, 2048)
    n4, fq, dq = w_in.shape[1], w_mlp_in.shape[1], d // N_CHIPS
    g_mlp = [_tn_matmul(h2, df1, N_CHIPS, d, fq, False, True, tt, "dw_mlp_in")[0],
             _tn_matmul(f, df2, N_CHIPS, fq, d, True, False, tt, "dw_mlp_out")[0]]
    (dya, dyb, dva, dcb, dbg, dza, dzb, dlng, dlnb, dba, dbb, sbg, sza, szb), sib_mlp = _mixer_bwd_call(
        dmix, sa, sb, ya, yb, bg, va, cb, lng, lnb, w_a_full, w_b_full, w_o_full, _tile(t, 512),
        ex=_ex_sibling_halves(g_mlp))
    p_mlp = pair_sums(mlp_w, g_mlp, sib_mlp)
    g_mix = [g.reshape(N_CHIPS, dq, d)
             for g in _tn_matmuls([(pa, dya), (sbo, dyb), (mg, dmix)], _tile(t, 1024), "dw_mixer")]
    ex_a, ex_b = _ex_scatter_to_owner(p_mlp), _ex_sibling_halves(g_mix)
    (dproj, dwa_conv, dwb_conv, dbin), xo = _conv_bwd_call(
        dva, dcb, ua, ub, cg, ha, a, sg, dbg, dza, dzb, (sbg, sza, szb), conv_a_full, conv_b_full, _tile(t, 256),
        ex=_merge(ex_a, ex_b))
    recv_mlp, sib_mix = _split(xo, ex_a, ex_b)
    r_mlp = chip_sums(mlp_w, p_mlp, recv_mlp)
    p_mix = pair_sums(mixer_w, g_mix, sib_mix)
    ex_a, ex_b = _ex_share_halves(r_mlp), _ex_scatter_to_owner(p_mix)
    g_in, xo = _tn_matmul(h, dproj, N_CHIPS, d, n4, False, True, tt, "dw_in", ex=_merge(ex_a, ex_b))
    red_mlp, recv_mix = _split(xo, ex_a, ex_b)
    r_mix = chip_sums(mixer_w, p_mix, recv_mix)
    ex_a, ex_b = _ex_sibling_halves([g_in]), _ex_share_halves(r_mix)
    tm_dx = _tile(t, 512)
    n_dx = t // tm_dx
    n_a = max(1, (3 * n_dx) // 8)
    dx_done, xo = _dx_call(dproj, xt, dx1, g1pre, w_in_g, tm_dx, 0, n_a, None, ex=_merge(ex_a, ex_b))
    sib_in, red_mix = _split(xo, ex_a, ex_b)
    p_in = pair_sums(["w_in"], [g_in], sib_in)
    (grad_x, dg1pre), recv_in = _dx_call(dproj, xt, dx1, g1pre, w_in_g, tm_dx, n_a, n_dx - n_a, dx_done,
                                         ex=_ex_scatter_to_owner(p_in))
    r_in = chip_sums(["w_in"], p_in, recv_in)
    red_in = _exchange_call("w_in_grad_to_sibling", [_ex_share_halves(r_in)])
    reduced = dict(zip(mlp_w + mixer_w + ["w_in"], red_mlp + red_mix + red_in))

    moments = dict(w_in=(m_w_in, v_w_in), w_a_out=(m_w_a_out, v_w_a_out), w_b_out=(m_w_b_out, v_w_b_out),
                   w_o=(m_w_o, v_w_o), w_mlp_in=(m_w_mlp_in, v_w_mlp_in), w_mlp_out=(m_w_mlp_out, v_w_mlp_out))
    out = {}
    for k in names:
        adam = _adam_sc_call if k != "w_in" and big[k].shape[0] % (SC_TILES * SUBLANES) == 0 else _adam_call
        out[k] = tuple(adam(big[k], reduced[k], *moments[k], "adamw_" + k))

    small = [
        ("conv_b_w", dwb_conv, conv_b_w, m_conv_b_w, v_conv_b_w, True),
        ("conv_b_b", dbb, bb, row(m_conv_b_b), row(v_conv_b_b), False),
        ("b_in", dbin.reshape(7, d), b_in.reshape(7, d), m_b_in.reshape(7, d), v_b_in.reshape(7, d), False),
        ("norm1_pre_g", dg1pre, row(norm1_pre_g), row(m_norm1_pre_g), row(v_norm1_pre_g), False),
        ("conv_a_w", dwa_conv, conv_a_w, m_conv_a_w, v_conv_a_w, True),
        ("conv_a_b", dba, ba, row(m_conv_a_b), row(v_conv_a_b), False),
        ("ln_b_g", dlng, lng, row(m_ln_b_g), row(v_ln_b_g), False),
        ("ln_b_b", dlnb, lnb, row(m_ln_b_b), row(v_ln_b_b), False),
        ("norm1_post_g", dg1post, g1post, row(m_norm1_post_g), row(v_norm1_post_g), False),
        ("norm2_pre_g", dg2pre, g2pre, row(m_norm2_pre_g), row(v_norm2_pre_g), False),
        ("norm2_post_g", dg2post, g2post, row(m_norm2_post_g), row(v_norm2_post_g), False),
    ]
    res = _small_step_call([s[1] for s in small], loss_rows, [s[2] for s in small], [s[3] for s in small],
                           [s[4] for s in small], [s[5] for s in small], d)
    ns = len(small)
    loss = res[4 * ns][0, 0]
    shapes = dict(norm1_pre_g=norm1_pre_g.shape, b_in=b_in.shape, conv_a_w=conv_a_w.shape,
                  conv_a_b=conv_a_b.shape, conv_b_w=conv_b_w.shape, conv_b_b=conv_b_b.shape,
                  ln_b_g=ln_b_g.shape, ln_b_b=ln_b_b.shape, norm1_post_g=norm1_post_g.shape,
                  norm2_pre_g=norm2_pre_g.shape, norm2_post_g=norm2_post_g.shape)
    for i, s in enumerate(small):
        out[s[0]] = tuple(res[q * ns + i].reshape(shapes[s[0]]) for q in range(4))

    order = ["norm1_pre_g", "w_in", "b_in", "conv_a_w", "conv_a_b", "w_a_out", "conv_b_w", "conv_b_b",
             "ln_b_g", "ln_b_b", "w_b_out", "w_o", "norm1_post_g", "norm2_pre_g", "w_mlp_in", "w_mlp_out",
             "norm2_post_g"]
    return (loss, grad_x.reshape(x.shape), *[out[k][0] for k in order], *[out[k][1] for k in order],
            *[out[k][2] for k in order], *[out[k][3] for k in order])
```

```python
import functools

import jax
import jax.numpy as jnp
from jax import lax
from jax.experimental import pallas as pl
from jax.experimental.pallas import tpu as pltpu
from jax.experimental.pallas import tpu_sc as plsc

RMS_EPS = 1e-6
LN_EPS = 1e-5
ADAM_LR = 0.001
ADAM_B1 = 0.9
ADAM_B2 = 0.999
ADAM_EPS = 1e-08
ADAM_WD = 0.01
ADAM_STEP = 10

F32 = jnp.float32
BF16 = jnp.bfloat16
MESH = pl.DeviceIdType.MESH
ANY = pl.BlockSpec(memory_space=pl.ANY)
VMEM_FULL = pl.BlockSpec(memory_space=pltpu.VMEM)

V7X_VMEM_BYTES = 64 * 1024 * 1024
VMEM_LIMIT = V7X_VMEM_BYTES - 8 * 1024 * 1024
SUBLANES = 8
N_CHIPS = 4
N_DEV = 8
SC_TILES = 32
SC_LANES = 16
SC_ROWS = 16
HALO_A = 8
HALO_B = 16
CONV_ROWS = 16
ROW_CHUNK = 32

NT_DIMS = (((1,), (1,)), ((), ()))
TN_DIMS = (((0,), (0,)), ((), ()))


def _params(*sem):
    return pltpu.CompilerParams(dimension_semantics=sem, vmem_limit_bytes=VMEM_LIMIT)


def _rows(tm, d):
    return pl.BlockSpec((tm, d), lambda i: (i, 0))


def _const(shape):
    return pl.BlockSpec(shape, lambda i: (0,) * len(shape))


def _halo_prev(tm, hb, d):
    return pl.BlockSpec((hb, d), lambda i: (jnp.maximum(i * (tm // hb) - 1, 0), 0))


def _halo_next(tm, hb, d, t):
    return pl.BlockSpec((hb, d), lambda i: (jnp.minimum((i + 1) * (tm // hb), t // hb - 1), 0))


def _for_chunks(n_rows, rc, fn):
    for r0 in range(0, n_rows, rc):
        fn(pl.ds(r0, rc))


def _fold8(v):
    return v.reshape(v.shape[0] // SUBLANES, SUBLANES, v.shape[1]).sum(axis=0)


def _mean_lanes(v):
    return jnp.mean(v, axis=-1, keepdims=True)


def _load_blocks_once(w_hbm, w_vmem, sem):
    nb, _, n = w_hbm.shape

    @pl.when(pl.program_id(0) == 0)
    def _():
        copies = [pltpu.make_async_copy(w_hbm.at[j], w_vmem.at[:, pl.ds(j * n, n)], sem.at[j])
                  for j in range(nb)]
        for cp in copies:
            cp.start()
        for cp in copies:
            cp.wait()


def _load_once(w_hbm, w_vmem, sem):
    @pl.when(pl.program_id(0) == 0)
    def _():
        cp = pltpu.make_async_copy(w_hbm, w_vmem, sem)
        cp.start()
        cp.wait()


def _write_row_sums(acc_ref, out_ref, n_steps):
    @pl.when(pl.program_id(0) == n_steps - 1)
    def _():
        out_ref[...] = jnp.sum(acc_ref[...], axis=0, keepdims=True)


def _place():
    return lax.axis_index("x"), lax.axis_index("y"), lax.axis_index("c")


def _other_chips(x, y):
    rel = [(x, 1 - y), (1 - x, y), (1 - x, 1 - y)]
    return [(px, py, 2 * px + py) for px, py in rel]


class _Exchange:
    def __init__(self, inputs, out_shapes, aliases, n_sems, copies):
        self.inputs = list(inputs)
        self.out_shapes = list(out_shapes)
        self.aliases = dict(aliases)
        self.n_sems = n_sems
        self.copies = copies


def _remote(src, dst, send, recv, device):
    return pltpu.make_async_remote_copy(src_ref=src, dst_ref=dst, send_sem=send, recv_sem=recv,
                                        device_id=device, device_id_type=MESH)


def _sds(a):
    return jax.ShapeDtypeStruct(a.shape, a.dtype)


def _ex_gather_ici(bufs):
    n = len(bufs)

    def copies(xin, xout, send, recv):
        x, y, c = _place()
        me = 2 * x + y
        out = []
        for a in range(n):
            hr = xin[a].shape[1] // 2
            rows = pl.ds(c * hr, hr)
            for j, (px, py, _) in enumerate(_other_chips(x, y)):
                k = a * (N_CHIPS - 1) + j
                out.append(_remote(xin[a].at[me, rows, :], xout[a].at[me, rows, :], send(k), recv(k), (px, py, c)))
        return out

    return _Exchange(bufs, [_sds(b) for b in bufs], {a: a for a in range(n)}, n * (N_CHIPS - 1), copies)


def _ex_gather_forward(bufs):
    n = len(bufs)

    def copies(xin, xout, send, recv):
        x, y, c = _place()
        out = []
        for a in range(n):
            hr = xin[a].shape[1] // 2
            rows = pl.ds(c * hr, hr)
            for j, (_, _, pk) in enumerate(_other_chips(x, y)):
                k = a * (N_CHIPS - 1) + j
                out.append(_remote(xin[a].at[pk, rows, :], xout[a].at[pk, rows, :], send(k), recv(k), (x, y, 1 - c)))
        return out

    return _Exchange(bufs, [_sds(b) for b in bufs], {a: a for a in range(n)}, n * (N_CHIPS - 1), copies)


def _ex_sibling_halves(grads):
    n = len(grads)

    def copies(xin, xout, send, recv):
        x, y, c = _place()
        out = []
        for a in range(n):
            hr = xin[a].shape[1] // 2
            out.append(_remote(xin[a].at[:, pl.ds((1 - c) * hr, hr), :], xout[a], send(a), recv(a), (x, y, 1 - c)))
        return out

    shapes = [jax.ShapeDtypeStruct((g.shape[0], g.shape[1] // 2, g.shape[2]), g.dtype) for g in grads]
    return _Exchange(grads, shapes, {}, n, copies)


def _ex_scatter_to_owner(pairs):
    n = len(pairs)

    def copies(xin, xout, send, recv):
        x, y, c = _place()
        out = []
        for a in range(n):
            for j, (px, py, pk) in enumerate(_other_chips(x, y)):
                k = a * (N_CHIPS - 1) + j
                out.append(_remote(xin[a].at[pk], xout[a].at[j], send(k), recv(k), (px, py, c)))
        return out

    shapes = [jax.ShapeDtypeStruct((N_CHIPS - 1,) + p.shape[1:], p.dtype) for p in pairs]
    return _Exchange(pairs, shapes, {}, n * (N_CHIPS - 1), copies)


def _ex_share_halves(reduced):
    n = len(reduced)

    def copies(xin, xout, send, recv):
        x, y, c = _place()
        out = []
        for a in range(n):
            hr = xin[a].shape[0] // 2
            rows = pl.ds(c * hr, hr)
            out.append(_remote(xin[a].at[rows, :], xout[a].at[rows, :], send(a), recv(a), (x, y, 1 - c)))
        return out

    return _Exchange(reduced, [_sds(r) for r in reduced], {a: a for a in range(n)}, n, copies)


def _merge(*exs):
    exs = [e for e in exs if e is not None]
    if not exs:
        return None
    inputs, shapes, aliases = [], [], {}
    in_off, out_off, sem_off = [], [], []
    n_sems = 0
    for e in exs:
        in_off.append(len(inputs))
        out_off.append(len(shapes))
        sem_off.append(n_sems)
        aliases.update({len(inputs) + i: len(shapes) + o for i, o in e.aliases.items()})
        inputs += e.inputs
        shapes += e.out_shapes
        n_sems += e.n_sems

    def copies(xin, xout, send, recv):
        out = []
        for e, io, oo, so in zip(exs, in_off, out_off, sem_off):
            out += e.copies(xin[io:io + len(e.inputs)], xout[oo:oo + len(e.out_shapes)],
                            lambda i, so=so: send(so + i), lambda i, so=so: recv(so + i))
        return out

    return _Exchange(inputs, shapes, aliases, n_sems, copies)


def _split(ex_outs, *exs):
    parts, o = [], 0
    for e in exs:
        parts.append(list(ex_outs[o:o + len(e.out_shapes)]))
        o += len(e.out_shapes)
    return parts


def _call(body, *, name, grid, in_specs, out_specs, out_shape, scratch_shapes, args, ex=None, aliases=None,
          then=None):
    n_in, n_out, n_scr = len(in_specs), len(out_specs), len(scratch_shapes)
    seq = ("arbitrary",) * len(grid)
    aliases = dict(aliases or {})
    if ex is None:
        outs = pl.pallas_call(
            body, name=name, grid=grid, in_specs=list(in_specs), out_specs=list(out_specs),
            out_shape=list(out_shape), scratch_shapes=list(scratch_shapes), input_output_aliases=aliases,
            compiler_params=_params(*seq))(*args)
        return list(outs), []
    n_xi, n_xo = len(ex.inputs), len(ex.out_shapes)

    def full(*refs):
        ins, xin = refs[:n_in], refs[n_in:n_in + n_xi]
        o = n_in + n_xi
        outs, xout = refs[o:o + n_out], refs[o + n_out:o + n_out + n_xo]
        s = o + n_out + n_xo
        scr = refs[s:s + n_scr]
        send_sems, recv_sems = refs[s + n_scr], refs[s + n_scr + 1]
        send = lambda i: send_sems.at[i]
        recv = lambda i: recv_sems.at[i]
        first = functools.reduce(jnp.logical_and, [pl.program_id(a) == 0 for a in range(len(grid))])
        last = functools.reduce(jnp.logical_and, [pl.program_id(a) == grid[a] - 1 for a in range(len(grid))])

        @pl.when(first)
        def _():
            for cp in ex.copies(xin, xout, send, recv):
                cp.start()

        body(*ins, *outs, *scr)

        if then is None:
            @pl.when(last)
            def _():
                for cp in ex.copies(xin, xout, send, recv):
                    cp.wait()
        else:
            step, ex2 = then
            send2_sems, recv2_sems = refs[s + n_scr + 2], refs[s + n_scr + 3]
            send2 = lambda i: send2_sems.at[i]
            recv2 = lambda i: recv2_sems.at[i]

            @pl.when(pl.program_id(0) == step)
            def _():
                for cp in ex.copies(xin, xout, send, recv):
                    cp.wait()
                for cp in ex2.copies(xin, xout, send2, recv2):
                    cp.start()

            @pl.when(last)
            def _():
                for cp in ex2.copies(xin, xout, send2, recv2):
                    cp.wait()

    sems = [pltpu.SemaphoreType.DMA((ex.n_sems,)), pltpu.SemaphoreType.DMA((ex.n_sems,))]
    if then is not None:
        assert len(grid) == 1 and 0 < then[0] < grid[0] - 1
        sems += [pltpu.SemaphoreType.DMA((then[1].n_sems,)), pltpu.SemaphoreType.DMA((then[1].n_sems,))]
    res = pl.pallas_call(
        full, name=name, grid=grid, in_specs=list(in_specs) + [ANY] * n_xi,
        out_specs=list(out_specs) + [ANY] * n_xo, out_shape=list(out_shape) + ex.out_shapes,
        scratch_shapes=list(scratch_shapes) + sems,
        input_output_aliases={**aliases, **{n_in + i: n_out + o for i, o in ex.aliases.items()}},
        compiler_params=pltpu.CompilerParams(dimension_semantics=seq, vmem_limit_bytes=VMEM_LIMIT,
                                             has_side_effects=True))(*args, *ex.inputs)
    return list(res[:n_out]), list(res[n_out:])


def _exchange_call(name, phases):
    first = phases[0]
    n_xi, n_xo = len(first.inputs), len(first.out_shapes)

    def body(*refs):
        xin, xout = refs[:n_xi], refs[n_xi:n_xi + n_xo]
        sems = refs[n_xi + n_xo:]
        for p, ex in enumerate(phases):
            send_sems, recv_sems = sems[2 * p], sems[2 * p + 1]
            cps = ex.copies(xin, xout, lambda i: send_sems.at[i], lambda i: recv_sems.at[i])
            for cp in cps:
                cp.start()
            for cp in cps:
                cp.wait()

    sems = []
    for ex in phases:
        sems += [pltpu.SemaphoreType.DMA((ex.n_sems,)), pltpu.SemaphoreType.DMA((ex.n_sems,))]
    return list(pl.pallas_call(
        body, name=name, in_specs=[ANY] * n_xi, out_specs=[ANY] * n_xo, out_shape=first.out_shapes,
        scratch_shapes=sems, input_output_aliases=dict(first.aliases),
        compiler_params=pltpu.CompilerParams(has_side_effects=True))(*first.inputs))


def _cast_to_slot(w, chip, name):
    r, c = w.shape
    tr = min(r, 256)

    def body(chip_ref, w_ref, o_ref):
        o_ref[0] = w_ref[...].astype(BF16)

    return pl.pallas_call(
        body, name=name,
        grid_spec=pltpu.PrefetchScalarGridSpec(
            num_scalar_prefetch=1, grid=(r // tr,),
            in_specs=[pl.BlockSpec((tr, c), lambda i, k: (i, 0))],
            out_specs=pl.BlockSpec((1, tr, c), lambda i, k: (k[0], i, 0))),
        out_shape=jax.ShapeDtypeStruct((N_CHIPS, r, c), BF16),
        compiler_params=_params("parallel"))(chip, w)


def _proj_call(x, g1pre, w_in_g, b_in, tm, ex=None, then=None):
    t, d = x.shape
    nb, _, n4 = w_in_g.shape
    ni = nb * n4
    assert ni == 7 * d

    def body(x_ref, g_ref, b_ref, w_hbm, h_ref, ua_ref, ub_ref, bg_ref, cg_ref, ha_ref, a_ref,
             sg_ref, sa_ref, sb_ref, w_v, p0, p1, sem):
        _load_blocks_once(w_hbm, w_v, sem)

        def norm(rows):
            xv = x_ref[rows, :]
            r = lax.rsqrt(_mean_lanes(xv * xv) + RMS_EPS)
            h_ref[rows, :] = (xv * r * g_ref[...]).astype(BF16)
        _for_chunks(tm, ROW_CHUNK, norm)

        def group(i, dst):
            cols = pl.ds(i * d, d)
            dst[...] = jnp.dot(h_ref[...], w_v[:, cols], preferred_element_type=F32) + b_ref[:, cols]

        group(0, p0)

        def bgate(rows):
            bg_ref[rows, :] = p0[rows, :].astype(BF16)
        _for_chunks(tm, ROW_CHUNK, bgate)

        group(1, p0)
        group(2, p1)

        def branch_a(rows):
            cg, ha = p0[rows, :], p1[rows, :]
            ua_ref[rows, :] = cg * ha
            cg_ref[rows, :] = cg.astype(BF16)
            ha_ref[rows, :] = ha.astype(BF16)
        _for_chunks(tm, ROW_CHUNK, branch_a)

        group(3, p0)
        group(4, p1)

        def branch_b(rows):
            a, sg = p0[rows, :], jax.nn.sigmoid(p1[rows, :])
            ub_ref[rows, :] = a * sg
            a_ref[rows, :] = a.astype(BF16)
            sg_ref[rows, :] = sg.astype(BF16)
        _for_chunks(tm, ROW_CHUNK, branch_b)

        group(5, p0)
        group(6, p1)

        def gates(rows):
            sa_ref[rows, :] = jax.nn.sigmoid(p0[rows, :]).astype(BF16)
            sb_ref[rows, :] = jax.nn.sigmoid(p1[rows, :]).astype(BF16)
        _for_chunks(tm, ROW_CHUNK, gates)

    bf = jax.ShapeDtypeStruct((t, d), BF16)
    f32 = jax.ShapeDtypeStruct((t, d), F32)
    return _call(
        body, name="proj_fwd", grid=(t // tm,),
        in_specs=[_rows(tm, d), _const((1, d)), _const((1, ni)), ANY],
        out_specs=[_rows(tm, d)] * 10,
        out_shape=[bf, f32, f32, bf, bf, bf, bf, bf, bf, bf],
        scratch_shapes=[pltpu.VMEM((d, ni), BF16), pltpu.VMEM((tm, d), F32), pltpu.VMEM((tm, d), F32),
                        pltpu.SemaphoreType.DMA((nb,))],
        args=(x, g1pre, b_in, w_in_g), ex=ex, then=then)


def _fill_ext(ext, prev_ref, cur_ref, next_ref, hb, tm, i, n_steps):
    ext[pl.ds(0, hb), :] = jnp.where(i > 0, prev_ref[...], 0.0)
    ext[pl.ds(hb, tm), :] = cur_ref[...]
    ext[pl.ds(hb + tm, hb), :] = jnp.where(i < n_steps - 1, next_ref[...], 0.0)


def _shift_plan(offsets):
    shifts = sorted({o % SUBLANES for o in offsets if o % SUBLANES})
    return {s: i for i, s in enumerate(shifts)}


def _shifted_rows(tm, offsets):
    return tm + SUBLANES * max(o // SUBLANES for o in offsets)


def _fill_shifted(ext, sh, plan):
    n = sh.shape[1]
    for s, i in plan.items():
        sh[i, :, :] = ext[pl.ds(s, n), :]


def _fill_tap_rows(w_ref, rows8):
    @pl.when(pl.program_id(0) == 0)
    def _():
        for k in range(w_ref.shape[0]):
            rows8[pl.ds(k * SUBLANES, SUBLANES), :] = jnp.broadcast_to(w_ref[k:k + 1, :], (SUBLANES, w_ref.shape[1]))


def _tap(rows8, k):
    w8 = rows8[pl.ds(k * SUBLANES, SUBLANES), :]
    return jnp.concatenate([w8] * (CONV_ROWS // SUBLANES), axis=0)


def _window(ext, sh, plan, offset, r0):
    q, s = divmod(offset, SUBLANES)
    if s == 0:
        return ext[pl.ds(offset + r0, CONV_ROWS), :]
    return sh[plan[s], pl.ds(SUBLANES * q + r0, CONV_ROWS), :]


def _mixer_fwd_call(ua, ub, bg, sa, sb, x, conv_a_w, conv_a_b, conv_b_w, conv_b_b, ln_g, ln_b,
                    w_a, w_b, w_o, g1post, tm, ex=None):
    t, d = x.shape
    n_steps = t // tm
    ka, kb = conv_a_w.shape[0], conv_b_w.shape[0]
    off_a = [HALO_A - (ka - 1) // 2 + k for k in range(ka)]
    off_b = [HALO_B - (kb - 1) // 2 + k for k in range(kb)]
    plan_a, plan_b = _shift_plan(off_a), _shift_plan(off_b)

    def body(uap, uac, uan, ubp, ubc, ubn, bg_ref, sa_ref, sb_ref, x_ref, wa_c, ba_c, wb_c, bb_c,
             lng, lnb, wa_hbm, wb_hbm, wo_hbm, g_ref,
             x1_ref, va_ref, pa_ref, cb_ref, sbo_ref, ya_ref, yb_ref, mg_ref, mix_ref,
             ext_a, ext_b, sh_a, sh_b, wa8, wb8, wa_v, wb_v, wo_v, y0, y1, sem):
        i = pl.program_id(0)
        _fill_tap_rows(wa_c, wa8)
        _fill_tap_rows(wb_c, wb8)
        _load_once(wa_hbm, wa_v, sem.at[0])
        _load_once(wb_hbm, wb_v, sem.at[1])
        _load_once(wo_hbm, wo_v, sem.at[2])
        _fill_ext(ext_a, uap, uac, uan, HALO_A, tm, i, n_steps)
        _fill_ext(ext_b, ubp, ubc, ubn, HALO_B, tm, i, n_steps)
        _fill_shifted(ext_a, sh_a, plan_a)
        _fill_shifted(ext_b, sh_b, plan_b)

        for r0 in range(0, tm, CONV_ROWS):
            rows = pl.ds(r0, CONV_ROWS)
            va = jnp.broadcast_to(ba_c[...], (CONV_ROWS, d))
            for k in range(ka):
                va = va + _tap(wa8, k) * _window(ext_a, sh_a, plan_a, off_a[k], r0)
            va_ref[rows, :] = va.astype(BF16)
            pa_ref[rows, :] = (bg_ref[rows, :].astype(F32) * va).astype(BF16)
            cb = jnp.broadcast_to(bb_c[...], (CONV_ROWS, d))
            for k in range(kb):
                cb = cb + _tap(wb8, k) * _window(ext_b, sh_b, plan_b, off_b[k], r0)
            cb_ref[rows, :] = cb
            mu = _mean_lanes(cb)
            cen = cb - mu
            rstd = lax.rsqrt(_mean_lanes(cen * cen) + LN_EPS)
            ln = cen * rstd * lng[...] + lnb[...]
            sbo_ref[rows, :] = (ln * jax.nn.sigmoid(ln)).astype(BF16)

        y0[...] = jnp.dot(pa_ref[...], wa_v[...], preferred_element_type=F32)
        y1[...] = jnp.dot(sbo_ref[...], wb_v[...], preferred_element_type=F32)

        def merge(rows):
            ya, yb = y0[rows, :], y1[rows, :]
            ya_ref[rows, :] = ya.astype(BF16)
            yb_ref[rows, :] = yb.astype(BF16)
            mg_ref[rows, :] = (sa_ref[rows, :].astype(F32) * ya + sb_ref[rows, :].astype(F32) * yb).astype(BF16)
        _for_chunks(tm, ROW_CHUNK, merge)

        mix_ref[...] = jnp.dot(mg_ref[...], wo_v[...], preferred_element_type=F32)

        def resid(rows):
            mix = mix_ref[rows, :]
            r = lax.rsqrt(_mean_lanes(mix * mix) + RMS_EPS)
            x1_ref[rows, :] = x_ref[rows, :] + mix * r * g_ref[...]
        _for_chunks(tm, ROW_CHUNK, resid)

    bf = jax.ShapeDtypeStruct((t, d), BF16)
    f32 = jax.ShapeDtypeStruct((t, d), F32)
    return _call(
        body, name="mixer_fwd", grid=(n_steps,),
        in_specs=[_halo_prev(tm, HALO_A, d), _rows(tm, d), _halo_next(tm, HALO_A, d, t),
                  _halo_prev(tm, HALO_B, d), _rows(tm, d), _halo_next(tm, HALO_B, d, t),
                  _rows(tm, d), _rows(tm, d), _rows(tm, d), _rows(tm, d),
                  _const((ka, d)), _const((1, d)), _const((kb, d)), _const((1, d)),
                  _const((1, d)), _const((1, d)), ANY, ANY, ANY, _const((1, d))],
        out_specs=[_rows(tm, d)] * 9,
        out_shape=[f32, bf, bf, f32, bf, bf, bf, bf, f32],
        scratch_shapes=[pltpu.VMEM((tm + 2 * HALO_A, d), F32), pltpu.VMEM((tm + 2 * HALO_B, d), F32),
                        pltpu.VMEM((len(plan_a), _shifted_rows(tm, off_a), d), F32),
                        pltpu.VMEM((len(plan_b), _shifted_rows(tm, off_b), d), F32),
                        pltpu.VMEM((ka * SUBLANES, d), F32), pltpu.VMEM((kb * SUBLANES, d), F32),
                        pltpu.VMEM((d, d), BF16), pltpu.VMEM((d, d), BF16), pltpu.VMEM((d, d), BF16),
                        pltpu.VMEM((tm, d), F32), pltpu.VMEM((tm, d), F32),
                        pltpu.SemaphoreType.DMA((3,))],
        args=(ua, ua, ua, ub, ub, ub, bg, sa, sb, x, conv_a_w, conv_a_b, conv_b_w, conv_b_b,
              ln_g, ln_b, w_a, w_b, w_o, g1post), ex=ex)


def _mlp_call(x1, target, mix, g2pre, g2post, g1post, w1_g, w2, tm, ex=None):
    t, d = x1.shape
    nb, _, fq = w1_g.shape
    f = nb * fq
    n_steps = t // tm
    inv_d = 1.0 / d

    def body(x1_ref, t_ref, mix_ref, gpre, gpost, gmix, w1_hbm, w2_hbm,
             dx1_ref, f_ref, df2_ref, h2_ref, df1_ref, dmix_ref, dgpost_ref, dgpre_ref, dgmix_ref, loss_ref,
             w1_v, w2_v, f1_s, blk_s, f2_s, acc_post, acc_pre, acc_mix, acc_loss, sem):
        _load_blocks_once(w1_hbm, w1_v, sem)
        _load_once(w2_hbm, w2_v, sem.at[nb])

        @pl.when(pl.program_id(0) == 0)
        def _():
            acc_post[...] = jnp.zeros_like(acc_post)
            acc_pre[...] = jnp.zeros_like(acc_pre)
            acc_mix[...] = jnp.zeros_like(acc_mix)
            acc_loss[...] = jnp.zeros_like(acc_loss)

        def norm(rows):
            xv = x1_ref[rows, :]
            r = lax.rsqrt(_mean_lanes(xv * xv) + RMS_EPS)
            h2_ref[rows, :] = (xv * r * gpre[...]).astype(BF16)
        _for_chunks(tm, ROW_CHUNK, norm)

        for j in range(nb):
            cols = pl.ds(j * fq, fq)
            f1_s[:, cols] = jnp.dot(h2_ref[...], w1_v[:, cols], preferred_element_type=F32)

        def act(rows):
            relu = jnp.maximum(f1_s[rows, :], 0.0)
            f_ref[rows, :] = (relu * relu).astype(BF16)
        _for_chunks(tm, ROW_CHUNK, act)

        f2_s[...] = jnp.dot(f_ref[...], w2_v[...], preferred_element_type=F32)

        def head(rows):
            f2 = f2_s[rows, :]
            rf = lax.rsqrt(_mean_lanes(f2 * f2) + RMS_EPS)
            y = x1_ref[rows, :] + f2 * rf * gpost[...]
            err = y - t_ref[rows, :]
            acc_loss[...] += _fold8(err * err)
            dy = err * inv_d
            gdy = dy * gpost[...]
            df2 = rf * gdy - f2 * (rf * rf * rf * _mean_lanes(gdy * f2))
            df2_ref[rows, :] = df2.astype(BF16)
            acc_post[...] += _fold8(dy * f2 * rf)
            dx1_ref[rows, :] = dy
        _for_chunks(tm, ROW_CHUNK, head)

        for j in range(nb):
            cols = pl.ds(j * fq, fq)
            blk_s[...] = lax.dot_general(df2_ref[...], w2_v[cols, :], NT_DIMS, preferred_element_type=F32)

            def dact(rows):
                relu = jnp.maximum(f1_s[rows, cols], 0.0)
                df1_ref[rows, cols] = (blk_s[rows, :] * (2.0 * relu)).astype(BF16)
            _for_chunks(tm, ROW_CHUNK, dact)

        f2_s[...] = lax.dot_general(df1_ref[...], w1_v[...], NT_DIMS, preferred_element_type=F32)

        def dnorm(rows):
            dh2 = f2_s[rows, :]
            xv = x1_ref[rows, :]
            r = lax.rsqrt(_mean_lanes(xv * xv) + RMS_EPS)
            gd = dh2 * gpre[...]
            dxv = dx1_ref[rows, :] + r * gd - xv * (r * r * r * _mean_lanes(gd * xv))
            dx1_ref[rows, :] = dxv
            acc_pre[...] += _fold8(dh2 * xv * r)
            mix = mix_ref[rows, :]
            rm = lax.rsqrt(_mean_lanes(mix * mix) + RMS_EPS)
            gm = dxv * gmix[...]
            dmix_ref[rows, :] = (rm * gm - mix * (rm * rm * rm * _mean_lanes(gm * mix))).astype(BF16)
            acc_mix[...] += _fold8(dxv * mix * rm)
        _for_chunks(tm, ROW_CHUNK, dnorm)

        _write_row_sums(acc_post, dgpost_ref, n_steps)
        _write_row_sums(acc_pre, dgpre_ref, n_steps)
        _write_row_sums(acc_mix, dgmix_ref, n_steps)
        _write_row_sums(acc_loss, loss_ref, n_steps)

    row = jax.ShapeDtypeStruct((1, d), F32)
    return _call(
        body, name="mlp_fwd_bwd", grid=(n_steps,),
        in_specs=[_rows(tm, d), _rows(tm, d), _rows(tm, d), _const((1, d)), _const((1, d)), _const((1, d)), ANY, ANY],
        out_specs=[_rows(tm, d), _rows(tm, f), _rows(tm, d), _rows(tm, d), _rows(tm, f), _rows(tm, d),
                   _const((1, d)), _const((1, d)), _const((1, d)), _const((1, d))],
        out_shape=[jax.ShapeDtypeStruct((t, d), F32), jax.ShapeDtypeStruct((t, f), BF16),
                   jax.ShapeDtypeStruct((t, d), BF16), jax.ShapeDtypeStruct((t, d), BF16),
                   jax.ShapeDtypeStruct((t, f), BF16), jax.ShapeDtypeStruct((t, d), BF16), row, row, row, row],
        scratch_shapes=[pltpu.VMEM((d, f), BF16), pltpu.VMEM((f, d), BF16),
                        pltpu.VMEM((tm, f), F32), pltpu.VMEM((tm, fq), F32), pltpu.VMEM((tm, d), F32),
                        pltpu.VMEM((SUBLANES, d), F32), pltpu.VMEM((SUBLANES, d), F32),
                        pltpu.VMEM((SUBLANES, d), F32), pltpu.VMEM((SUBLANES, d), F32),
                        pltpu.SemaphoreType.DMA((nb + 1,))],
        args=(x1, target, mix, g2pre, g2post, g1post, w1_g, w2), ex=ex)


def _mixer_bwd_call(dmix, sa, sb, ya, yb, bg, va, cb, ln_g, ln_b, w_a, w_b, w_o, tm, ex=None):
    t, d = dmix.shape
    n_steps = t // tm

    def body(dmix_ref, sa_ref, sb_ref, ya_ref, yb_ref, bg_ref, va_ref, cb_ref, lng, lnb,
             wa_hbm, wb_hbm, wo_hbm,
             dya_ref, dyb_ref, dva_ref, dcb_ref, dbg_ref, dza_ref, dzb_ref,
             dlng_ref, dlnb_ref, dba_ref, dbb_ref, sbg_ref, sza_ref, szb_ref,
             wa_v, wb_v, wo_v, s0, s1, acc_lng, acc_lnb, acc_ba, acc_bb, acc_bg, acc_za, acc_zb, sem):
        _load_once(wa_hbm, wa_v, sem.at[0])
        _load_once(wb_hbm, wb_v, sem.at[1])
        _load_once(wo_hbm, wo_v, sem.at[2])
        accs = (acc_lng, acc_lnb, acc_ba, acc_bb, acc_bg, acc_za, acc_zb)

        @pl.when(pl.program_id(0) == 0)
        def _():
            for acc in accs:
                acc[...] = jnp.zeros_like(acc)

        s0[...] = lax.dot_general(dmix_ref[...], wo_v[...], NT_DIMS, preferred_element_type=F32)

        def dmerge(rows):
            dm = s0[rows, :]
            sav, sbv = sa_ref[rows, :].astype(F32), sb_ref[rows, :].astype(F32)
            dya_ref[rows, :] = (dm * sav).astype(BF16)
            dyb_ref[rows, :] = (dm * sbv).astype(BF16)
            dza = dm * ya_ref[rows, :].astype(F32) * sav * (1.0 - sav)
            dzb = dm * yb_ref[rows, :].astype(F32) * sbv * (1.0 - sbv)
            dza_ref[rows, :] = dza.astype(BF16)
            dzb_ref[rows, :] = dzb.astype(BF16)
            acc_za[...] += _fold8(dza)
            acc_zb[...] += _fold8(dzb)
        _for_chunks(tm, ROW_CHUNK, dmerge)

        s0[...] = lax.dot_general(dya_ref[...], wa_v[...], NT_DIMS, preferred_element_type=F32)
        s1[...] = lax.dot_general(dyb_ref[...], wb_v[...], NT_DIMS, preferred_element_type=F32)

        def dbranches(rows):
            dpa = s0[rows, :]
            dbg = dpa * va_ref[rows, :].astype(F32)
            dbg_ref[rows, :] = dbg.astype(BF16)
            acc_bg[...] += _fold8(dbg)
            dva = dpa * bg_ref[rows, :].astype(F32)
            dva_ref[rows, :] = dva
            acc_ba[...] += _fold8(dva)
            cbv = cb_ref[rows, :]
            mu = _mean_lanes(cbv)
            cen = cbv - mu
            rstd = lax.rsqrt(_mean_lanes(cen * cen) + LN_EPS)
            xhat = cen * rstd
            ln = xhat * lng[...] + lnb[...]
            sig = jax.nn.sigmoid(ln)
            dln = s1[rows, :] * (sig * (1.0 + ln * (1.0 - sig)))
            acc_lng[...] += _fold8(dln * xhat)
            acc_lnb[...] += _fold8(dln)
            dxh = dln * lng[...]
            dcb = rstd * (dxh - _mean_lanes(dxh) - xhat * _mean_lanes(dxh * xhat))
            dcb_ref[rows, :] = dcb
            acc_bb[...] += _fold8(dcb)
        _for_chunks(tm, ROW_CHUNK, dbranches)

        _write_row_sums(acc_lng, dlng_ref, n_steps)
        _write_row_sums(acc_lnb, dlnb_ref, n_steps)
        _write_row_sums(acc_ba, dba_ref, n_steps)
        _write_row_sums(acc_bb, dbb_ref, n_steps)
        _write_row_sums(acc_bg, sbg_ref, n_steps)
        _write_row_sums(acc_za, sza_ref, n_steps)
        _write_row_sums(acc_zb, szb_ref, n_steps)

    bf = jax.ShapeDtypeStruct((t, d), BF16)
    f32 = jax.ShapeDtypeStruct((t, d), F32)
    row = jax.ShapeDtypeStruct((1, d), F32)
    return _call(
        body, name="mixer_bwd", grid=(n_steps,),
        in_specs=[_rows(tm, d)] * 8 + [_const((1, d))] * 2 + [ANY, ANY, ANY],
        out_specs=[_rows(tm, d)] * 7 + [_const((1, d))] * 7,
        out_shape=[bf, bf, f32, f32, bf, bf, bf] + [row] * 7,
        scratch_shapes=[pltpu.VMEM((d, d), BF16), pltpu.VMEM((d, d), BF16), pltpu.VMEM((d, d), BF16),
                        pltpu.VMEM((tm, d), F32), pltpu.VMEM((tm, d), F32)]
        + [pltpu.VMEM((SUBLANES, d), F32)] * 7 + [pltpu.SemaphoreType.DMA((3,))],
        args=(dmix, sa, sb, ya, yb, bg, va, cb, ln_g, ln_b, w_a, w_b, w_o), ex=ex)


def _conv_bwd_call(dva, dcb, ua, ub, cg, ha, a, sg, dbg, dza, dzb, through_sums, conv_a_w, conv_b_w, tm, ex=None):
    t, d = dva.shape
    n_steps = t // tm
    ka, kb = conv_a_w.shape[0], conv_b_w.shape[0]
    off_a = [HALO_A + (ka - 1) // 2 - k for k in range(ka)]
    off_b = [HALO_B + (kb - 1) // 2 - k for k in range(kb)]
    plan_a, plan_b = _shift_plan(off_a), _shift_plan(off_b)

    def body(dvap, dvac, dvan, dcbp, dcbc, dcbn, ua_ref, ub_ref,
             cg_ref, ha_ref, a_ref, sg_ref, dbg_ref, dza_ref, dzb_ref, wa_c, wb_c, sbg_ref, sza_ref, szb_ref,
             dproj_ref, dwa_ref, dwb_ref, dbin_ref,
             e_dva, e_dcb, sh_a, sh_b, wa8, wb8, acc_wa, acc_wb, acc_bin):
        i = pl.program_id(0)
        _fill_tap_rows(wa_c, wa8)
        _fill_tap_rows(wb_c, wb8)

        @pl.when(i == 0)
        def _():
            acc_wa[...] = jnp.zeros_like(acc_wa)
            acc_wb[...] = jnp.zeros_like(acc_wb)
            acc_bin[...] = jnp.zeros_like(acc_bin)
            for col, s_ref in ((0, sbg_ref), (5, sza_ref), (6, szb_ref)):
                acc_bin[0:1, pl.ds(col * d, d)] = s_ref[...]

        _fill_ext(e_dva, dvap, dvac, dvan, HALO_A, tm, i, n_steps)
        _fill_ext(e_dcb, dcbp, dcbc, dcbn, HALO_B, tm, i, n_steps)
        _fill_shifted(e_dva, sh_a, plan_a)
        _fill_shifted(e_dcb, sh_b, plan_b)

        def put(col, rows, val_f32):
            dproj_ref[rows, pl.ds(col * d, d)] = val_f32.astype(BF16)
            acc_bin[:, pl.ds(col * d, d)] += _fold8(val_f32)

        for r0 in range(0, tm, CONV_ROWS):
            rows = pl.ds(r0, CONV_ROWS)
            ua_c, ub_c = ua_ref[rows, :], ub_ref[rows, :]
            dua = jnp.zeros((CONV_ROWS, d), F32)
            for k in range(ka):
                xk = _window(e_dva, sh_a, plan_a, off_a[k], r0)
                dua = dua + _tap(wa8, k) * xk
                acc_wa[pl.ds(k * SUBLANES, SUBLANES), :] += _fold8(ua_c * xk)
            dub = jnp.zeros((CONV_ROWS, d), F32)
            for k in range(kb):
                xk = _window(e_dcb, sh_b, plan_b, off_b[k], r0)
                dub = dub + _tap(wb8, k) * xk
                acc_wb[pl.ds(k * SUBLANES, SUBLANES), :] += _fold8(ub_c * xk)
            cgv, hav = cg_ref[rows, :].astype(F32), ha_ref[rows, :].astype(F32)
            av, sgv = a_ref[rows, :].astype(F32), sg_ref[rows, :].astype(F32)
            put(1, rows, dua * hav)
            put(2, rows, dua * cgv)
            put(3, rows, dub * sgv)
            put(4, rows, dub * av * sgv * (1.0 - sgv))
            for col, through in ((0, dbg_ref), (5, dza_ref), (6, dzb_ref)):
                dproj_ref[rows, pl.ds(col * d, d)] = through[rows, :]

        @pl.when(i == n_steps - 1)
        def _():
            for k in range(ka):
                dwa_ref[k:k + 1, :] = jnp.sum(acc_wa[pl.ds(k * SUBLANES, SUBLANES), :], axis=0, keepdims=True)
            for k in range(kb):
                dwb_ref[k:k + 1, :] = jnp.sum(acc_wb[pl.ds(k * SUBLANES, SUBLANES), :], axis=0, keepdims=True)
            dbin_ref[...] = jnp.sum(acc_bin[...], axis=0, keepdims=True)

    halo_a = [_halo_prev(tm, HALO_A, d), _rows(tm, d), _halo_next(tm, HALO_A, d, t)]
    halo_b = [_halo_prev(tm, HALO_B, d), _rows(tm, d), _halo_next(tm, HALO_B, d, t)]
    return _call(
        body, name="conv_bwd", grid=(n_steps,),
        in_specs=halo_a + halo_b + [_rows(tm, d)] * 9 + [_const((ka, d)), _const((kb, d))] + [_const((1, d))] * 3,
        out_specs=[_rows(tm, 7 * d), _const((ka, d)), _const((kb, d)), _const((1, 7 * d))],
        out_shape=[jax.ShapeDtypeStruct((t, 7 * d), BF16), jax.ShapeDtypeStruct((ka, d), F32),
                   jax.ShapeDtypeStruct((kb, d), F32), jax.ShapeDtypeStruct((1, 7 * d), F32)],
        scratch_shapes=[pltpu.VMEM((tm + 2 * HALO_A, d), F32), pltpu.VMEM((tm + 2 * HALO_B, d), F32),
                        pltpu.VMEM((len(plan_a), _shifted_rows(tm, off_a), d), F32),
                        pltpu.VMEM((len(plan_b), _shifted_rows(tm, off_b), d), F32),
                        pltpu.VMEM((ka * SUBLANES, d), F32), pltpu.VMEM((kb * SUBLANES, d), F32),
                        pltpu.VMEM((ka * SUBLANES, d), F32), pltpu.VMEM((kb * SUBLANES, d), F32),
                        pltpu.VMEM((SUBLANES, 7 * d), F32)],
        args=(dva, dva, dva, dcb, dcb, dcb, ua, ub, cg, ha, a, sg, dbg, dza, dzb,
              conv_a_w, conv_b_w, *through_sums), ex=ex)


def _dx_call(dproj, x, dx1, g1pre, w_in_g, tm, first, n_steps, prev, ex=None):
    t, d = x.shape
    nb, _, n4 = w_in_g.shape
    ni = nb * n4
    rows = lambda width: pl.BlockSpec((tm, width), lambda i: (i + first, 0))
    if prev is None:
        prev = (jnp.zeros((SUBLANES, 128), F32), jnp.zeros((1, d), F32))
    prev_dx, prev_dg = prev

    def body(dp_ref, x_ref, dx1_ref, g_ref, w_hbm, prev_dx_hbm, prev_dg_ref, dx_ref, dg_ref, w_v, dh_s, acc_g, sem):
        _load_blocks_once(w_hbm, w_v, sem)

        @pl.when(pl.program_id(0) == 0)
        def _():
            acc_g[...] = jnp.zeros_like(acc_g)
            acc_g[0:1, :] = prev_dg_ref[...]

        dh_s[...] = lax.dot_general(dp_ref[...], w_v[...], NT_DIMS, preferred_element_type=F32)

        def dnorm(rows):
            dh = dh_s[rows, :]
            xv = x_ref[rows, :]
            r = lax.rsqrt(_mean_lanes(xv * xv) + RMS_EPS)
            gd = dh * g_ref[...]
            dx_ref[rows, :] = dx1_ref[rows, :] + r * gd - xv * (r * r * r * _mean_lanes(gd * xv))
            acc_g[...] += _fold8(dh * xv * r)
        _for_chunks(tm, ROW_CHUNK, dnorm)
        _write_row_sums(acc_g, dg_ref, n_steps)

    return _call(
        body, name="dx_bwd_from_%d" % first, grid=(n_steps,),
        in_specs=[rows(ni), rows(d), rows(d), _const((1, d)), ANY, ANY, _const((1, d))],
        out_specs=[rows(d), _const((1, d))],
        out_shape=[jax.ShapeDtypeStruct((t, d), F32), jax.ShapeDtypeStruct((1, d), F32)],
        scratch_shapes=[pltpu.VMEM((d, ni), BF16), pltpu.VMEM((tm, d), F32),
                        pltpu.VMEM((SUBLANES, d), F32), pltpu.SemaphoreType.DMA((nb,))],
        args=(dproj, x, dx1, g1pre, w_in_g, prev_dx, prev_dg), ex=ex,
        aliases={5: 0} if first > 0 else None)


def _tn_matmul(a, g, nblk, a_cols, g_cols, a_blocked, g_blocked, tt, name, ex=None):
    t = a.shape[0]

    def body(a_ref, g_ref, o_ref):
        @pl.when(pl.program_id(1) == 0)
        def _():
            o_ref[...] = jnp.zeros_like(o_ref)
        o_ref[0] += lax.dot_general(a_ref[...], g_ref[...], TN_DIMS, preferred_element_type=F32)

    (out,), xouts = _call(
        body, name=name, grid=(nblk, t // tt),
        in_specs=[pl.BlockSpec((tt, a_cols), (lambda b, s: (s, b)) if a_blocked else (lambda b, s: (s, 0))),
                  pl.BlockSpec((tt, g_cols), (lambda b, s: (s, b)) if g_blocked else (lambda b, s: (s, 0)))],
        out_specs=[pl.BlockSpec((1, a_cols, g_cols), lambda b, s: (b, 0, 0))],
        out_shape=[jax.ShapeDtypeStruct((nblk, a_cols, g_cols), F32)],
        scratch_shapes=[], args=(a, g), ex=ex)
    return out, xouts


def _tn_matmuls(pairs, tt, name):
    t, d = pairs[0][0].shape
    k = len(pairs)

    def body(*refs):
        ins, outs = refs[:2 * k], refs[2 * k:]

        @pl.when(pl.program_id(0) == 0)
        def _():
            for o_ref in outs:
                o_ref[...] = jnp.zeros_like(o_ref)
        for j in range(k):
            outs[j][...] += lax.dot_general(ins[2 * j][...], ins[2 * j + 1][...], TN_DIMS, preferred_element_type=F32)

    outs, _ = _call(
        body, name=name, grid=(t // tt,), in_specs=[_rows(tt, d)] * (2 * k), out_specs=[_const((d, d))] * k,
        out_shape=[jax.ShapeDtypeStruct((d, d), F32)] * k, scratch_shapes=[],
        args=[m for pair in pairs for m in pair])
    return outs


def _pair_sum_call(g_full, from_sibling, core, name):
    nblk, r, c = g_full.shape
    hr = r // 2
    tr = min(hr, 256)
    n = hr // tr

    def body(core_ref, g_ref, p_ref, o_ref):
        o_ref[...] = (g_ref[...] + p_ref[...]).astype(BF16)

    return pl.pallas_call(
        body, name=name,
        grid_spec=pltpu.PrefetchScalarGridSpec(
            num_scalar_prefetch=1, grid=(nblk, n),
            in_specs=[pl.BlockSpec((1, tr, c), lambda j, i, cr: (j, cr[0] * n + i, 0)),
                      pl.BlockSpec((1, tr, c), lambda j, i, cr: (j, i, 0))],
            out_specs=pl.BlockSpec((1, tr, c), lambda j, i, cr: (j, i, 0))),
        out_shape=jax.ShapeDtypeStruct((nblk, hr, c), BF16),
        compiler_params=_params("parallel", "parallel"))(core, g_full, from_sibling)


def _chip_sum_call(pair, received, chip_core, name):
    _, hr, c = pair.shape
    tr = min(hr, 256)
    n = hr // tr

    def body(cc_ref, own_ref, r_ref, o_ref):
        o_ref[...] = ((own_ref[0].astype(F32) + r_ref[0].astype(F32)) + r_ref[1].astype(F32)) + r_ref[2].astype(F32)

    return pl.pallas_call(
        body, name=name,
        grid_spec=pltpu.PrefetchScalarGridSpec(
            num_scalar_prefetch=1, grid=(n,),
            in_specs=[pl.BlockSpec((1, tr, c), lambda i, cc: (cc[0], i, 0)),
                      pl.BlockSpec((N_CHIPS - 1, tr, c), lambda i, cc: (0, i, 0))],
            out_specs=pl.BlockSpec((tr, c), lambda i, cc: (cc[1] * n + i, 0))),
        out_shape=jax.ShapeDtypeStruct((2 * hr, c), F32),
        compiler_params=_params("parallel"))(chip_core, pair, received)


def _adamw(w, g, m, v):
    m = ADAM_B1 * m + (1.0 - ADAM_B1) * g
    v = ADAM_B2 * v + (1.0 - ADAM_B2) * (g * g)
    m_hat = m / (1.0 - ADAM_B1 ** ADAM_STEP)
    v_hat = v / (1.0 - ADAM_B2 ** ADAM_STEP)
    delta = -ADAM_LR * (m_hat / (jnp.sqrt(v_hat) + ADAM_EPS) + ADAM_WD * w)
    return delta, m, v


def _adam_call(w, g, m, v, name):
    r, c = w.shape
    tr = min(r, 256)

    def body(w_ref, g_ref, m_ref, v_ref, go_ref, d_ref, mo_ref, vo_ref):
        go_ref[...] = g_ref[...]
        d_ref[...], mo_ref[...], vo_ref[...] = _adamw(w_ref[...], g_ref[...], m_ref[...], v_ref[...])

    shape = jax.ShapeDtypeStruct((r, c), F32)
    return pl.pallas_call(
        body, name=name, grid=(r // tr,), in_specs=[_rows(tr, c)] * 4, out_specs=[_rows(tr, c)] * 4,
        out_shape=[shape] * 4, compiler_params=_params("parallel"))(w, g, m, v)


def _adam_sc_call(w, g, m, v, name):
    r, c = w.shape
    rows_tile = r // SC_TILES
    rr = min(rows_tile, SC_ROWS)

    def body(w_hbm, g_hbm, m_hbm, v_hbm, go_hbm, d_hbm, mo_hbm, vo_hbm, wb, gb, mb, vb):
        tile = lax.axis_index("sc_tile") * 2 + lax.axis_index("sc_core")

        @pl.loop(0, rows_tile, step=rr)
        def _(p):
            rows = pl.ds(tile * rows_tile + p, rr)
            pltpu.sync_copy(w_hbm.at[rows, :], wb)
            pltpu.sync_copy(g_hbm.at[rows, :], gb)
            pltpu.sync_copy(m_hbm.at[rows, :], mb)
            pltpu.sync_copy(v_hbm.at[rows, :], vb)

            @pl.loop(0, rr)
            def _(i):
                @pl.loop(0, c, step=SC_LANES)
                def _(j):
                    at = (i, pl.ds(j, SC_LANES))
                    delta, m2, v2 = _adamw(wb[at], gb[at], mb[at], vb[at])
                    wb[at] = delta
                    mb[at] = m2
                    vb[at] = v2

            pltpu.sync_copy(gb, go_hbm.at[rows, :])
            pltpu.sync_copy(wb, d_hbm.at[rows, :])
            pltpu.sync_copy(mb, mo_hbm.at[rows, :])
            pltpu.sync_copy(vb, vo_hbm.at[rows, :])

    shape = jax.ShapeDtypeStruct((r, c), F32)
    return pl.kernel(
        body, name=name, out_type=[shape] * 4,
        mesh=plsc.VectorSubcoreMesh(core_axis_name="sc_core", subcore_axis_name="sc_tile"),
        scratch_types=[pltpu.VMEM((rr, c), F32)] * 4)(w, g, m, v)


def _gather_first_call(buf, others, conv_a_w, conv_b_w, d):
    ka, dq = conv_a_w.shape
    kb = conv_b_w.shape[0]
    ra = -(-ka // SUBLANES) * SUBLANES
    rb = -(-kb // SUBLANES) * SUBLANES
    a_pad = jnp.pad(conv_a_w, ((0, ra - ka), (0, 0)))
    b_pad = jnp.pad(conv_b_w, ((0, rb - kb), (0, 0)))
    hr = buf.shape[1] // 2
    qr = hr // 2
    n_o = len(others)
    stage_rows = max(o.shape[0] for o in others)
    stage_cols = others[0].shape[1]
    assert all(o.shape[1] == stage_cols for o in others)

    def body(w_in, a_ref, b_ref, *rest):
        others_in, w_out, oa_ref, ob_ref = rest[:n_o], rest[n_o], rest[n_o + 1], rest[n_o + 2]
        others_out = rest[n_o + 3:2 * n_o + 3]
        pack, slots, stage_f, stage_b, send, recv, csend, crecv, osem = rest[2 * n_o + 3:]
        x, y, c = _place()
        me, x_chip, y_chip, d_chip = 2 * x + y, 2 * (1 - x) + y, 2 * x + (1 - y), 2 * (1 - x) + (1 - y)
        x_nbr, y_nbr, sibling = (1 - x, y, c), (x, 1 - y, c), (x, y, 1 - c)
        mine, theirs = pl.ds(c * hr, hr), pl.ds((1 - c) * hr, hr)
        first, second = pl.ds(c * hr, qr), pl.ds(c * hr + qr, qr)

        def rows_of(chip, rows):
            return w_out.at[chip, rows, :]

        def copy(k, src, dst, peer):
            return _remote(src, dst, send.at[k], recv.at[k], peer)

        def landed(k, dst):
            copy(k, dst, dst, sibling).wait_recv()

        pack[pl.ds(0, ra), :] = a_ref[...]
        pack[pl.ds(ra, rb), :] = b_ref[...]
        chips = _other_chips(x, y)
        conv = [_remote(pack, slots.at[me], csend.at[j], crecv.at[j], (px, py, c)) for j, (px, py, _) in enumerate(chips)]
        started = [copy(0, w_in.at[me, mine, :], rows_of(me, mine), x_nbr),
                   copy(1, w_in.at[me, mine, :], rows_of(me, mine), y_nbr)]
        for cp in conv + started:
            cp.start()

        def go(cp):
            cp.start()
            started.append(cp)

        for src, dst in zip(others_in, others_out):
            r = src.shape[0]
            f_view, b_view = stage_f.at[pl.ds(0, r), :], stage_b.at[pl.ds(0, r), :]
            fetch = pltpu.make_async_copy(src, f_view, osem.at[0])
            fetch.start()
            fetch.wait()
            b_view[...] = f_view[...].astype(BF16)
            place = pltpu.make_async_copy(b_view, dst.at[me], osem.at[1])
            place.start()
            place.wait()

        landed(0, rows_of(x_chip, mine))
        go(copy(2, rows_of(x_chip, first), rows_of(x_chip, first), y_nbr))
        go(copy(4, rows_of(x_chip, mine), rows_of(x_chip, mine), sibling))
        landed(1, rows_of(y_chip, mine))
        go(copy(3, rows_of(y_chip, second), rows_of(y_chip, second), x_nbr))
        go(copy(5, rows_of(y_chip, mine), rows_of(y_chip, mine), sibling))
        landed(2, rows_of(d_chip, first))
        landed(3, rows_of(d_chip, second))
        go(copy(6, rows_of(d_chip, mine), rows_of(d_chip, mine), sibling))
        for k, chip in ((4, x_chip), (5, y_chip), (6, d_chip)):
            landed(k, rows_of(chip, theirs))
        for cp in started:
            cp.wait_send()

        for j, (px, py, pk) in enumerate(chips):
            _remote(pack, slots.at[pk], csend.at[j], crecv.at[j], (px, py, c)).wait_recv()
        for cp in conv:
            cp.wait_send()
        slots[me] = pack[...]
        for k in range(N_CHIPS):
            oa_ref[:, pl.ds(k * dq, dq)] = slots[k, pl.ds(0, ra), :]
            ob_ref[:, pl.ds(k * dq, dq)] = slots[k, pl.ds(ra, rb), :]

    n_w = 7
    res = pl.pallas_call(
        body, name="gather_first", in_specs=[ANY, VMEM_FULL, VMEM_FULL] + [ANY] * n_o,
        out_specs=[ANY, VMEM_FULL, VMEM_FULL] + [ANY] * n_o,
        out_shape=[_sds(buf), jax.ShapeDtypeStruct((ra, d), F32), jax.ShapeDtypeStruct((rb, d), F32)]
        + [jax.ShapeDtypeStruct((N_CHIPS,) + o.shape, BF16) for o in others],
        scratch_shapes=[pltpu.VMEM((ra + rb, dq), F32), pltpu.VMEM((N_CHIPS, ra + rb, dq), F32),
                        pltpu.VMEM((stage_rows, stage_cols), F32), pltpu.VMEM((stage_rows, stage_cols), BF16),
                        pltpu.SemaphoreType.DMA((n_w,)), pltpu.SemaphoreType.DMA((n_w,)),
                        pltpu.SemaphoreType.DMA((N_CHIPS - 1,)), pltpu.SemaphoreType.DMA((N_CHIPS - 1,)),
                        pltpu.SemaphoreType.DMA((2,))],
        input_output_aliases={0: 0},
        compiler_params=pltpu.CompilerParams(has_side_effects=True, vmem_limit_bytes=VMEM_LIMIT))(
            buf, a_pad, b_pad, *others)
    return res[0], res[1][:ka], res[2][:kb], list(res[3:])


def _small_step_call(partials, loss_rows, weights, m_s, v_s, sharded, d):
    n = len(partials)
    row_counts = [p.shape[0] for p in partials]
    starts = [sum(row_counts[:i]) for i in range(n)]
    loss_row = sum(row_counts)
    pack_rows = -(-(loss_row + 1) // SUBLANES) * SUBLANES
    dq = d // N_CHIPS

    def body(*refs):
        p_refs = refs[:n]
        loss_in = refs[n]
        w_refs = refs[n + 1:2 * n + 1]
        m_refs = refs[2 * n + 1:3 * n + 1]
        v_refs = refs[3 * n + 1:4 * n + 1]
        o = 4 * n + 1
        g_out = refs[o:o + n]
        d_out = refs[o + n:o + 2 * n]
        m_out = refs[o + 2 * n:o + 3 * n]
        v_out = refs[o + 3 * n:o + 4 * n]
        loss_out = refs[o + 4 * n]
        pack, from_sibling, slots, send_sem, recv_sem = refs[o + 4 * n + 1:]
        x, y, c = _place()
        me = 2 * x + y

        pack[...] = jnp.zeros_like(pack)
        for i in range(n):
            pack[pl.ds(starts[i], row_counts[i]), :] = p_refs[i][...]
        pack[pl.ds(loss_row, 1), :] = loss_in[...]

        pair = _remote(pack, from_sibling, send_sem.at[0], recv_sem.at[0], (x, y, 1 - c))
        pair.start()
        pair.wait()
        pack[...] = pack[...] + from_sibling[...]
        chips = _other_chips(x, y)
        copies = [_remote(pack, slots.at[me], send_sem.at[1 + j], recv_sem.at[1 + j], (px, py, c))
                  for j, (px, py, _) in enumerate(chips)]
        for cp in copies:
            cp.start()
        for j, (px, py, pk) in enumerate(chips):
            _remote(pack, slots.at[pk], send_sem.at[1 + j], recv_sem.at[1 + j], (px, py, c)).wait_recv()
        for cp in copies:
            cp.wait_send()

        slots[me] = pack[...]
        total = slots[0]
        for k in range(1, N_CHIPS):
            total = total + slots[k]
        pack[...] = total

        loss_out[...] = jnp.broadcast_to(
            (0.5 / d) * jnp.sum(pack[pl.ds(loss_row, 1), :], axis=-1, keepdims=True), loss_out.shape)
        chip = 2 * x + y
        for i in range(n):
            rows = pl.ds(starts[i], row_counts[i])
            if sharded[i]:
                for k in range(N_CHIPS):
                    @pl.when(chip == k)
                    def _():
                        g_out[i][...] = pack[rows, pl.ds(k * dq, dq)]
            else:
                g_out[i][...] = pack[rows, :]
            d_out[i][...], m_out[i][...], v_out[i][...] = _adamw(
                w_refs[i][...], g_out[i][...], m_refs[i][...], v_refs[i][...])

    w_shapes = [jax.ShapeDtypeStruct(w.shape, F32) for w in weights]
    n_in = 4 * n + 1
    return pl.pallas_call(
        body, name="small_grads_allreduce_adamw",
        in_specs=[VMEM_FULL] * n_in, out_specs=[VMEM_FULL] * (4 * n + 1),
        out_shape=w_shapes * 4 + [jax.ShapeDtypeStruct((SUBLANES, 128), F32)],
        scratch_shapes=[pltpu.VMEM((pack_rows, d), F32), pltpu.VMEM((pack_rows, d), F32),
                        pltpu.VMEM((N_CHIPS, pack_rows, d), F32),
                        pltpu.SemaphoreType.DMA((N_CHIPS,)), pltpu.SemaphoreType.DMA((N_CHIPS,))],
        compiler_params=pltpu.CompilerParams(has_side_effects=True, vmem_limit_bytes=VMEM_LIMIT))(
            *partials, loss_rows, *weights, *m_s, *v_s)


def _tile(t, want):
    return min(t, want)


def kernel(x, norm1_pre_g, w_in, b_in, conv_a_w, conv_a_b, w_a_out, conv_b_w, conv_b_b, ln_b_g, ln_b_b, w_b_out, w_o, norm1_post_g, norm2_pre_g, w_mlp_in, w_mlp_out, norm2_post_g, loss_target, m_norm1_pre_g, m_w_in, m_b_in, m_conv_a_w, m_conv_a_b, m_w_a_out, m_conv_b_w, m_conv_b_b, m_ln_b_g, m_ln_b_b, m_w_b_out, m_w_o, m_norm1_post_g, m_norm2_pre_g, m_w_mlp_in, m_w_mlp_out, m_norm2_post_g, v_norm1_pre_g, v_w_in, v_b_in, v_conv_a_w, v_conv_a_b, v_w_a_out, v_conv_b_w, v_conv_b_b, v_ln_b_g, v_ln_b_b, v_w_b_out, v_w_o, v_norm1_post_g, v_norm2_pre_g, v_w_mlp_in, v_w_mlp_out, v_norm2_post_g):
    _, t, d = x.shape
    xt = x.reshape(t, d)
    tgt = loss_target.reshape(t, d)
    row = lambda vec: vec.reshape(1, -1)
    cx, cy, cc = _place()
    core = cc.astype(jnp.int32).reshape(1)
    chip = (2 * cx + cy).astype(jnp.int32).reshape(1)

    big = dict(w_in=w_in, w_a_out=w_a_out, w_b_out=w_b_out, w_o=w_o, w_mlp_in=w_mlp_in, w_mlp_out=w_mlp_out)
    names = list(big)
    chip_core = jnp.concatenate([chip, core])
    mixer_w, mlp_w = ["w_a_out", "w_b_out", "w_o"], ["w_mlp_in", "w_mlp_out"]
    rows_of = lambda buf: buf.reshape(-1, buf.shape[-1])

    def pair_sums(keys, full, from_sibling):
        return [_pair_sum_call(g, p, core, "pair_sum_" + k) for k, g, p in zip(keys, full, from_sibling)]

    def chip_sums(keys, pairs, received):
        return [_chip_sum_call(p, r, chip_core, "chip_sum_" + k) for k, p, r in zip(keys, pairs, received)]

    w_in_g, conv_a_full, conv_b_full, slots = _gather_first_call(
        _cast_to_slot(w_in, chip, "cast_w_in"), [big[k] for k in mixer_w + mlp_w], conv_a_w, conv_b_w, d)

    g1pre, g1post, g2pre, g2post = row(norm1_pre_g), row(norm1_post_g), row(norm2_pre_g), row(norm2_post_g)
    lng, lnb, ba, bb = row(ln_b_g), row(ln_b_b), row(conv_a_b), row(conv_b_b)

    tm_proj = _tile(t, 512)
    n_proj = t // tm_proj
    if n_proj >= 3:
        (h, ua, ub, bg, cg, ha, a, sg, sa, sb), landed = _proj_call(
            xt, g1pre, w_in_g, row(b_in), tm_proj, ex=_ex_gather_ici(slots),
            then=(min(n_proj - 2, (7 * n_proj) // 8), _ex_gather_forward(slots[:3])))
        w_a_g, w_b_g, w_o_g = landed[:3]
    else:
        (h, ua, ub, bg, cg, ha, a, sg, sa, sb), landed = _proj_call(
            xt, g1pre, w_in_g, row(b_in), tm_proj, ex=_ex_gather_ici(slots))
        w_a_g, w_b_g, w_o_g = _exchange_call("forward_mixer_weights", [_ex_gather_forward(landed[:3])])
    w_a_full, w_b_full, w_o_full = rows_of(w_a_g), rows_of(w_b_g), rows_of(w_o_g)
    (x1, va, pa, cb, sbo, ya, yb, mg, mix), (w1_g, w2_g) = _mixer_fwd_call(
        ua, ub, bg, sa, sb, xt, conv_a_full, ba, conv_b_full, bb, lng, lnb,
        w_a_full, w_b_full, w_o_full, g1post, _tile(t, 256), ex=_ex_gather_forward(landed[3:]))
    (dx1, f, df2, h2, df1, dmix, dg2post, dg2pre, dg1post, loss_rows), _ = _mlp_call(
        x1, tgt, mix, g2pre, g2post, g1post, w1_g, rows_of(w2_g), _tile(t, 256))

    tt = _tile(t, 2048)
    n4, fq, dq = w_in.shape[1], w_mlp_in.shape[1], d // N_CHIPS
    g_mlp = [_tn_matmul(h2, df1, N_CHIPS, d, fq, False, True, tt, "dw_mlp_in")[0],
             _tn_matmul(f, df2, N_CHIPS, fq, d, True, False, tt, "dw_mlp_out")[0]]
    (dya, dyb, dva, dcb, dbg, dza, dzb, dlng, dlnb, dba, dbb, sbg, sza, szb), sib_mlp = _mixer_bwd_call(
        dmix, sa, sb, ya, yb, bg, va, cb, lng, lnb, w_a_full, w_b_full, w_o_full, _tile(t, 512),
        ex=_ex_sibling_halves(g_mlp))
    p_mlp = pair_sums(mlp_w, g_mlp, sib_mlp)
    g_mix = [g.reshape(N_CHIPS, dq, d)
             for g in _tn_matmuls([(pa, dya), (sbo, dyb), (mg, dmix)], _tile(t, 1024), "dw_mixer")]
    ex_a, ex_b = _ex_scatter_to_owner(p_mlp), _ex_sibling_halves(g_mix)
    (dproj, dwa_conv, dwb_conv, dbin), xo = _conv_bwd_call(
        dva, dcb, ua, ub, cg, ha, a, sg, dbg, dza, dzb, (sbg, sza, szb), conv_a_full, conv_b_full, _tile(t, 256),
        ex=_merge(ex_a, ex_b))
    recv_mlp, sib_mix = _split(xo, ex_a, ex_b)
    r_mlp = chip_sums(mlp_w, p_mlp, recv_mlp)
    p_mix = pair_sums(mixer_w, g_mix, sib_mix)
    ex_a, ex_b = _ex_share_halves(r_mlp), _ex_scatter_to_owner(p_mix)
    g_in, xo = _tn_matmul(h, dproj, N_CHIPS, d, n4, False, True, tt, "dw_in", ex=_merge(ex_a, ex_b))
    red_mlp, recv_mix = _split(xo, ex_a, ex_b)
    r_mix = chip_sums(mixer_w, p_mix, recv_mix)
    ex_a, ex_b = _ex_sibling_halves([g_in]), _ex_share_halves(r_mix)
    tm_dx = _tile(t, 512)
    n_dx = t // tm_dx
    n_a = max(1, (3 * n_dx) // 8)
    dx_done, xo = _dx_call(dproj, xt, dx1, g1pre, w_in_g, tm_dx, 0, n_a, None, ex=_merge(ex_a, ex_b))
    sib_in, red_mix = _split(xo, ex_a, ex_b)
    p_in = pair_sums(["w_in"], [g_in], sib_in)
    (grad_x, dg1pre), recv_in = _dx_call(dproj, xt, dx1, g1pre, w_in_g, tm_dx, n_a, n_dx - n_a, dx_done,
                                         ex=_ex_scatter_to_owner(p_in))
    r_in = chip_sums(["w_in"], p_in, recv_in)
    red_in = _exchange_call("w_in_grad_to_sibling", [_ex_share_halves(r_in)])
    reduced = dict(zip(mlp_w + mixer_w + ["w_in"], red_mlp + red_mix + red_in))

    moments = dict(w_in=(m_w_in, v_w_in), w_a_out=(m_w_a_out, v_w_a_out), w_b_out=(m_w_b_out, v_w_b_out),
                   w_o=(m_w_o, v_w_o), w_mlp_in=(m_w_mlp_in, v_w_mlp_in), w_mlp_out=(m_w_mlp_out, v_w_mlp_out))
    out = {}
    for k in names:
        adam = _adam_sc_call if k != "w_in" and big[k].shape[0] % (SC_TILES * SUBLANES) == 0 else _adam_call
        out[k] = tuple(adam(big[k], reduced[k], *moments[k], "adamw_" + k))

    small = [
        ("conv_b_w", dwb_conv, conv_b_w, m_conv_b_w, v_conv_b_w, True),
        ("conv_b_b", dbb, bb, row(m_conv_b_b), row(v_conv_b_b), False),
        ("b_in", dbin.reshape(7, d), b_in.reshape(7, d), m_b_in.reshape(7, d), v_b_in.reshape(7, d), False),
        ("norm1_pre_g", dg1pre, row(norm1_pre_g), row(m_norm1_pre_g), row(v_norm1_pre_g), False),
        ("conv_a_w", dwa_conv, conv_a_w, m_conv_a_w, v_conv_a_w, True),
        ("conv_a_b", dba, ba, row(m_conv_a_b), row(v_conv_a_b), False),
        ("ln_b_g", dlng, lng, row(m_ln_b_g), row(v_ln_b_g), False),
        ("ln_b_b", dlnb, lnb, row(m_ln_b_b), row(v_ln_b_b), False),
        ("norm1_post_g", dg1post, g1post, row(m_norm1_post_g), row(v_norm1_post_g), False),
        ("norm2_pre_g", dg2pre, g2pre, row(m_norm2_pre_g), row(v_norm2_pre_g), False),
        ("norm2_post_g", dg2post, g2post, row(m_norm2_post_g), row(v_norm2_post_g), False),
    ]
    res = _small_step_call([s[1] for s in small], loss_rows, [s[2] for s in small], [s[3] for s in small],
                           [s[4] for s in small], [s[5] for s in small], d)
    ns = len(small)
    loss = res[4 * ns][0, 0]
    shapes = dict(norm1_pre_g=norm1_pre_g.shape, b_in=b_in.shape, conv_a_w=conv_a_w.shape,
                  conv_a_b=conv_a_b.shape, conv_b_w=conv_b_w.shape, conv_b_b=conv_b_b.shape,
                  ln_b_g=ln_b_g.shape, ln_b_b=ln_b_b.shape, norm1_post_g=norm1_post_g.shape,
                  norm2_pre_g=norm2_pre_g.shape, norm2_post_g=norm2_post_g.shape)
    for i, s in enumerate(small):
        out[s[0]] = tuple(res[q * ns + i].reshape(shapes[s[0]]) for q in range(4))

    order = ["norm1_pre_g", "w_in", "b_in", "conv_a_w", "conv_a_b", "w_a_out", "conv_b_w", "conv_b_b",
             "ln_b_g", "ln_b_b", "w_b_out", "w_o", "norm1_post_g", "norm2_pre_g", "w_mlp_in", "w_mlp_out",
             "norm2_post_g"]
    return (loss, grad_x.reshape(x.shape), *[out[k][0] for k in order], *[out[k][1] for k in order],
            *[out[k][2] for k in order], *[out[k][3] for k in order])
```

```python
import functools

import jax
import jax.numpy as jnp
from jax import lax
from jax.experimental import pallas as pl
from jax.experimental.pallas import tpu as pltpu
from jax.experimental.pallas import tpu_sc as plsc

RMS_EPS = 1e-6
LN_EPS = 1e-5
ADAM_LR = 0.001
ADAM_B1 = 0.9
ADAM_B2 = 0.999
ADAM_EPS = 1e-08
ADAM_WD = 0.01
ADAM_STEP = 10

F32 = jnp.float32
BF16 = jnp.bfloat16
MESH = pl.DeviceIdType.MESH
ANY = pl.BlockSpec(memory_space=pl.ANY)
VMEM_FULL = pl.BlockSpec(memory_space=pltpu.VMEM)

V7X_VMEM_BYTES = 64 * 1024 * 1024
VMEM_LIMIT = V7X_VMEM_BYTES - 8 * 1024 * 1024
SUBLANES = 8
N_CHIPS = 4
N_DEV = 8
SC_TILES = 32
SC_LANES = 16
SC_ROWS = 16
HALO_A = 8
HALO_B = 16
CONV_ROWS = 16
ROW_CHUNK = 32

NT_DIMS = (((1,), (1,)), ((), ()))
TN_DIMS = (((0,), (0,)), ((), ()))


def _params(*sem):
    return pltpu.CompilerParams(dimension_semantics=sem, vmem_limit_bytes=VMEM_LIMIT)


def _rows(tm, d):
    return pl.BlockSpec((tm, d), lambda i: (i, 0))


def _const(shape):
    return pl.BlockSpec(shape, lambda i: (0,) * len(shape))


def _halo_prev(tm, hb, d):
    return pl.BlockSpec((hb, d), lambda i: (jnp.maximum(i * (tm // hb) - 1, 0), 0))


def _halo_next(tm, hb, d, t):
    return pl.BlockSpec((hb, d), lambda i: (jnp.minimum((i + 1) * (tm // hb), t // hb - 1), 0))


def _for_chunks(n_rows, rc, fn):
    for r0 in range(0, n_rows, rc):
        fn(pl.ds(r0, rc))


def _fold8(v):
    return v.reshape(v.shape[0] // SUBLANES, SUBLANES, v.shape[1]).sum(axis=0)


def _mean_lanes(v):
    return jnp.mean(v, axis=-1, keepdims=True)


def _load_blocks_once(w_hbm, w_vmem, sem):
    nb, _, n = w_hbm.shape

    @pl.when(pl.program_id(0) == 0)
    def _():
        copies = [pltpu.make_async_copy(w_hbm.at[j], w_vmem.at[:, pl.ds(j * n, n)], sem.at[j])
                  for j in range(nb)]
        for cp in copies:
            cp.start()
        for cp in copies:
            cp.wait()


def _load_once(w_hbm, w_vmem, sem):
    @pl.when(pl.program_id(0) == 0)
    def _():
        cp = pltpu.make_async_copy(w_hbm, w_vmem, sem)
        cp.start()
        cp.wait()


def _write_row_sums(acc_ref, out_ref, n_steps):
    @pl.when(pl.program_id(0) == n_steps - 1)
    def _():
        out_ref[...] = jnp.sum(acc_ref[...], axis=0, keepdims=True)


def _place():
    return lax.axis_index("x"), lax.axis_index("y"), lax.axis_index("c")


def _other_chips(x, y):
    rel = [(x, 1 - y), (1 - x, y), (1 - x, 1 - y)]
    return [(px, py, 2 * px + py) for px, py in rel]


class _Exchange:
    def __init__(self, inputs, out_shapes, aliases, n_sems, copies):
        self.inputs = list(inputs)
        self.out_shapes = list(out_shapes)
        self.aliases = dict(aliases)
        self.n_sems = n_sems
        self.copies = copies


def _remote(src, dst, send, recv, device):
    return pltpu.make_async_remote_copy(src_ref=src, dst_ref=dst, send_sem=send, recv_sem=recv,
                                        device_id=device, device_id_type=MESH)


def _sds(a):
    return jax.ShapeDtypeStruct(a.shape, a.dtype)


def _ex_gather_ici(bufs):
    n = len(bufs)

    def copies(xin, xout, send, recv):
        x, y, c = _place()
        me = 2 * x + y
        out = []
        for a in range(n):
            hr = xin[a].shape[1] // 2
            rows = pl.ds(c * hr, hr)
            for j, (px, py, _) in enumerate(_other_chips(x, y)):
                k = a * (N_CHIPS - 1) + j
                out.append(_remote(xin[a].at[me, rows, :], xout[a].at[me, rows, :], send(k), recv(k), (px, py, c)))
        return out

    return _Exchange(bufs, [_sds(b) for b in bufs], {a: a for a in range(n)}, n * (N_CHIPS - 1), copies)


def _ex_gather_forward(bufs):
    n = len(bufs)

    def copies(xin, xout, send, recv):
        x, y, c = _place()
        out = []
        for a in range(n):
            hr = xin[a].shape[1] // 2
            rows = pl.ds(c * hr, hr)
            for j, (_, _, pk) in enumerate(_other_chips(x, y)):
                k = a * (N_CHIPS - 1) + j
                out.append(_remote(xin[a].at[pk, rows, :], xout[a].at[pk, rows, :], send(k), recv(k), (x, y, 1 - c)))
        return out

    return _Exchange(bufs, [_sds(b) for b in bufs], {a: a for a in range(n)}, n * (N_CHIPS - 1), copies)


def _ex_sibling_halves(grads):
    n = len(grads)

    def copies(xin, xout, send, recv):
        x, y, c = _place()
        out = []
        for a in range(n):
            hr = xin[a].shape[1] // 2
            out.append(_remote(xin[a].at[:, pl.ds((1 - c) * hr, hr), :], xout[a], send(a), recv(a), (x, y, 1 - c)))
        return out

    shapes = [jax.ShapeDtypeStruct((g.shape[0], g.shape[1] // 2, g.shape[2]), g.dtype) for g in grads]
    return _Exchange(grads, shapes, {}, n, copies)


def _ex_scatter_to_owner(pairs):
    n = len(pairs)

    def copies(xin, xout, send, recv):
        x, y, c = _place()
        out = []
        for a in range(n):
            for j, (px, py, pk) in enumerate(_other_chips(x, y)):
                k = a * (N_CHIPS - 1) + j
                out.append(_remote(xin[a].at[pk], xout[a].at[j], send(k), recv(k), (px, py, c)))
        return out

    shapes = [jax.ShapeDtypeStruct((N_CHIPS - 1,) + p.shape[1:], p.dtype) for p in pairs]
    return _Exchange(pairs, shapes, {}, n * (N_CHIPS - 1), copies)


def _ex_share_halves(reduced):
    n = len(reduced)

    def copies(xin, xout, send, recv):
        x, y, c = _place()
        out = []
        for a in range(n):
            hr = xin[a].shape[0] // 2
            rows = pl.ds(c * hr, hr)
            out.append(_remote(xin[a].at[rows, :], xout[a].at[rows, :], send(a), recv(a), (x, y, 1 - c)))
        return out

    return _Exchange(reduced, [_sds(r) for r in reduced], {a: a for a in range(n)}, n, copies)


def _merge(*exs):
    exs = [e for e in exs if e is not None]
    if not exs:
        return None
    inputs, shapes, aliases = [], [], {}
    in_off, out_off, sem_off = [], [], []
    n_sems = 0
    for e in exs:
        in_off.append(len(inputs))
        out_off.append(len(shapes))
        sem_off.append(n_sems)
        aliases.update({len(inputs) + i: len(shapes) + o for i, o in e.aliases.items()})
        inputs += e.inputs
        shapes += e.out_shapes
        n_sems += e.n_sems

    def copies(xin, xout, send, recv):
        out = []
        for e, io, oo, so in zip(exs, in_off, out_off, sem_off):
            out += e.copies(xin[io:io + len(e.inputs)], xout[oo:oo + len(e.out_shapes)],
                            lambda i, so=so: send(so + i), lambda i, so=so: recv(so + i))
        return out

    return _Exchange(inputs, shapes, aliases, n_sems, copies)


def _split(ex_outs, *exs):
    parts, o = [], 0
    for e in exs:
        parts.append(list(ex_outs[o:o + len(e.out_shapes)]))
        o += len(e.out_shapes)
    return parts


def _call(body, *, name, grid, in_specs, out_specs, out_shape, scratch_shapes, args, ex=None, aliases=None,
          then=None):
    n_in, n_out, n_scr = len(in_specs), len(out_specs), len(scratch_shapes)
    seq = ("arbitrary",) * len(grid)
    aliases = dict(aliases or {})
    if ex is None:
        outs = pl.pallas_call(
            body, name=name, grid=grid, in_specs=list(in_specs), out_specs=list(out_specs),
            out_shape=list(out_shape), scratch_shapes=list(scratch_shapes), input_output_aliases=aliases,
            compiler_params=_params(*seq))(*args)
        return list(outs), []
    n_xi, n_xo = len(ex.inputs), len(ex.out_shapes)

    def full(*refs):
        ins, xin = refs[:n_in], refs[n_in:n_in + n_xi]
        o = n_in + n_xi
        outs, xout = refs[o:o + n_out], refs[o + n_out:o + n_out + n_xo]
        s = o + n_out + n_xo
        scr = refs[s:s + n_scr]
        send_sems, recv_sems = refs[s + n_scr], refs[s + n_scr + 1]
        send = lambda i: send_sems.at[i]
        recv = lambda i: recv_sems.at[i]
        first = functools.reduce(jnp.logical_and, [pl.program_id(a) == 0 for a in range(len(grid))])
        last = functools.reduce(jnp.logical_and, [pl.program_id(a) == grid[a] - 1 for a in range(len(grid))])

        @pl.when(first)
        def _():
            for cp in ex.copies(xin, xout, send, recv):
                cp.start()

        body(*ins, *outs, *scr)

        if then is None:
            @pl.when(last)
            def _():
                for cp in ex.copies(xin, xout, send, recv):
                    cp.wait()
        else:
            step, ex2 = then
            send2_sems, recv2_sems = refs[s + n_scr + 2], refs[s + n_scr + 3]
            send2 = lambda i: send2_sems.at[i]
            recv2 = lambda i: recv2_sems.at[i]

            @pl.when(pl.program_id(0) == step)
            def _():
                for cp in ex.copies(xin, xout, send, recv):
                    cp.wait()
                for cp in ex2.copies(xin, xout, send2, recv2):
                    cp.start()

            @pl.when(last)
            def _():
                for cp in ex2.copies(xin, xout, send2, recv2):
                    cp.wait()

    sems = [pltpu.SemaphoreType.DMA((ex.n_sems,)), pltpu.SemaphoreType.DMA((ex.n_sems,))]
    if then is not None:
        assert len(grid) == 1 and 0 < then[0] < grid[0] - 1
        sems += [pltpu.SemaphoreType.DMA((then[1].n_sems,)), pltpu.SemaphoreType.DMA((then[1].n_sems,))]
    res = pl.pallas_call(
        full, name=name, grid=grid, in_specs=list(in_specs) + [ANY] * n_xi,
        out_specs=list(out_specs) + [ANY] * n_xo, out_shape=list(out_shape) + ex.out_shapes,
        scratch_shapes=list(scratch_shapes) + sems,
        input_output_aliases={**aliases, **{n_in + i: n_out + o for i, o in ex.aliases.items()}},
        compiler_params=pltpu.CompilerParams(dimension_semantics=seq, vmem_limit_bytes=VMEM_LIMIT,
                                             has_side_effects=True))(*args, *ex.inputs)
    return list(res[:n_out]), list(res[n_out:])


def _exchange_call(name, phases):
    first = phases[0]
    n_xi, n_xo = len(first.inputs), len(first.out_shapes)

    def body(*refs):
        xin, xout = refs[:n_xi], refs[n_xi:n_xi + n_xo]
        sems = refs[n_xi + n_xo:]
        for p, ex in enumerate(phases):
            send_sems, recv_sems = sems[2 * p], sems[2 * p + 1]
            cps = ex.copies(xin, xout, lambda i: send_sems.at[i], lambda i: recv_sems.at[i])
            for cp in cps:
                cp.start()
            for cp in cps:
                cp.wait()

    sems = []
    for ex in phases:
        sems += [pltpu.SemaphoreType.DMA((ex.n_sems,)), pltpu.SemaphoreType.DMA((ex.n_sems,))]
    return list(pl.pallas_call(
        body, name=name, in_specs=[ANY] * n_xi, out_specs=[ANY] * n_xo, out_shape=first.out_shapes,
        scratch_shapes=sems, input_output_aliases=dict(first.aliases),
        compiler_params=pltpu.CompilerParams(has_side_effects=True))(*first.inputs))


def _cast_to_slot(w, chip, name):
    r, c = w.shape
    tr = min(r, 256)

    def body(chip_ref, w_ref, o_ref):
        o_ref[0] = w_ref[...].astype(BF16)

    return pl.pallas_call(
        body, name=name,
        grid_spec=pltpu.PrefetchScalarGridSpec(
            num_scalar_prefetch=1, grid=(r // tr,),
            in_specs=[pl.BlockSpec((tr, c), lambda i, k: (i, 0))],
            out_specs=pl.BlockSpec((1, tr, c), lambda i, k: (k[0], i, 0))),
        out_shape=jax.ShapeDtypeStruct((N_CHIPS, r, c), BF16),
        compiler_params=_params("parallel"))(chip, w)


def _proj_call(x, g1pre, w_in_g, b_in, tm, ex=None, then=None):
    t, d = x.shape
    nb, _, n4 = w_in_g.shape
    ni = nb * n4
    assert ni == 7 * d

    def body(x_ref, g_ref, b_ref, w_hbm, h_ref, ua_ref, ub_ref, bg_ref, cg_ref, ha_ref, a_ref,
             sg_ref, sa_ref, sb_ref, w_v, p0, p1, sem):
        _load_blocks_once(w_hbm, w_v, sem)

        def norm(rows):
            xv = x_ref[rows, :]
            r = lax.rsqrt(_mean_lanes(xv * xv) + RMS_EPS)
            h_ref[rows, :] = (xv * r * g_ref[...]).astype(BF16)
        _for_chunks(tm, ROW_CHUNK, norm)

        def group(i, dst):
            cols = pl.ds(i * d, d)
            dst[...] = jnp.dot(h_ref[...], w_v[:, cols], preferred_element_type=F32) + b_ref[:, cols]

        group(0, p0)

        def bgate(rows):
            bg_ref[rows, :] = p0[rows, :].astype(BF16)
        _for_chunks(tm, ROW_CHUNK, bgate)

        group(1, p0)
        group(2, p1)

        def branch_a(rows):
            cg, ha = p0[rows, :], p1[rows, :]
            ua_ref[rows, :] = cg * ha
            cg_ref[rows, :] = cg.astype(BF16)
            ha_ref[rows, :] = ha.astype(BF16)
        _for_chunks(tm, ROW_CHUNK, branch_a)

        group(3, p0)
        group(4, p1)

        def branch_b(rows):
            a, sg = p0[rows, :], jax.nn.sigmoid(p1[rows, :])
            ub_ref[rows, :] = a * sg
            a_ref[rows, :] = a.astype(BF16)
            sg_ref[rows, :] = sg.astype(BF16)
        _for_chunks(tm, ROW_CHUNK, branch_b)

        group(5, p0)
        group(6, p1)

        def gates(rows):
            sa_ref[rows, :] = jax.nn.sigmoid(p0[rows, :]).astype(BF16)
            sb_ref[rows, :] = jax.nn.sigmoid(p1[rows, :]).astype(BF16)
        _for_chunks(tm, ROW_CHUNK, gates)

    bf = jax.ShapeDtypeStruct((t, d), BF16)
    f32 = jax.ShapeDtypeStruct((t, d), F32)
    return _call(
        body, name="proj_fwd", grid=(t // tm,),
        in_specs=[_rows(tm, d), _const((1, d)), _const((1, ni)), ANY],
        out_specs=[_rows(tm, d)] * 10,
        out_shape=[bf, f32, f32, bf, bf, bf, bf, bf, bf, bf],
        scratch_shapes=[pltpu.VMEM((d, ni), BF16), pltpu.VMEM((tm, d), F32), pltpu.VMEM((tm, d), F32),
                        pltpu.SemaphoreType.DMA((nb,))],
        args=(x, g1pre, b_in, w_in_g), ex=ex, then=then)


def _fill_ext(ext, prev_ref, cur_ref, next_ref, hb, tm, i, n_steps):
    ext[pl.ds(0, hb), :] = jnp.where(i > 0, prev_ref[...], 0.0)
    ext[pl.ds(hb, tm), :] = cur_ref[...]
    ext[pl.ds(hb + tm, hb), :] = jnp.where(i < n_steps - 1, next_ref[...], 0.0)


def _shift_plan(offsets):
    shifts = sorted({o % SUBLANES for o in offsets if o % SUBLANES})
    return {s: i for i, s in enumerate(shifts)}


def _shifted_rows(tm, offsets):
    return tm + SUBLANES * max(o // SUBLANES for o in offsets)


def _fill_shifted(ext, sh, plan):
    n = sh.shape[1]
    for s, i in plan.items():
        sh[i, :, :] = ext[pl.ds(s, n), :]


def _fill_tap_rows(w_ref, rows8):
    @pl.when(pl.program_id(0) == 0)
    def _():
        for k in range(w_ref.shape[0]):
            rows8[pl.ds(k * SUBLANES, SUBLANES), :] = jnp.broadcast_to(w_ref[k:k + 1, :], (SUBLANES, w_ref.shape[1]))


def _tap(rows8, k):
    w8 = rows8[pl.ds(k * SUBLANES, SUBLANES), :]
    return jnp.concatenate([w8] * (CONV_ROWS // SUBLANES), axis=0)


def _window(ext, sh, plan, offset, r0):
    q, s = divmod(offset, SUBLANES)
    if s == 0:
        return ext[pl.ds(offset + r0, CONV_ROWS), :]
    return sh[plan[s], pl.ds(SUBLANES * q + r0, CONV_ROWS), :]


def _mixer_fwd_call(ua, ub, bg, sa, sb, x, conv_a_w, conv_a_b, conv_b_w, conv_b_b, ln_g, ln_b,
                    w_a, w_b, w_o, g1post, tm, ex=None):
    t, d = x.shape
    n_steps = t // tm
    ka, kb = conv_a_w.shape[0], conv_b_w.shape[0]
    off_a = [HALO_A - (ka - 1) // 2 + k for k in range(ka)]
    off_b = [HALO_B - (kb - 1) // 2 + k for k in range(kb)]
    plan_a, plan_b = _shift_plan(off_a), _shift_plan(off_b)

    def body(uap, uac, uan, ubp, ubc, ubn, bg_ref, sa_ref, sb_ref, x_ref, wa_c, ba_c, wb_c, bb_c,
             lng, lnb, wa_hbm, wb_hbm, wo_hbm, g_ref,
             x1_ref, va_ref, pa_ref, cb_ref, sbo_ref, ya_ref, yb_ref, mg_ref, mix_ref,
             ext_a, ext_b, sh_a, sh_b, wa8, wb8, wa_v, wb_v, wo_v, y0, y1, sem):
        i = pl.program_id(0)
        _fill_tap_rows(wa_c, wa8)
        _fill_tap_rows(wb_c, wb8)
        _load_once(wa_hbm, wa_v, sem.at[0])
        _load_once(wb_hbm, wb_v, sem.at[1])
        _load_once(wo_hbm, wo_v, sem.at[2])
        _fill_ext(ext_a, uap, uac, uan, HALO_A, tm, i, n_steps)
        _fill_ext(ext_b, ubp, ubc, ubn, HALO_B, tm, i, n_steps)
        _fill_shifted(ext_a, sh_a, plan_a)
        _fill_shifted(ext_b, sh_b, plan_b)

        for r0 in range(0, tm, CONV_ROWS):
            rows = pl.ds(r0, CONV_ROWS)
            va = jnp.broadcast_to(ba_c[...], (CONV_ROWS, d))
            for k in range(ka):
                va = va + _tap(wa8, k) * _window(ext_a, sh_a, plan_a, off_a[k], r0)
            va_ref[rows, :] = va.astype(BF16)
            pa_ref[rows, :] = (bg_ref[rows, :].astype(F32) * va).astype(BF16)
            cb = jnp.broadcast_to(bb_c[...], (CONV_ROWS, d))
            for k in range(kb):
                cb = cb + _tap(wb8, k) * _window(ext_b, sh_b, plan_b, off_b[k], r0)
            cb_ref[rows, :] = cb
            mu = _mean_lanes(cb)
            cen = cb - mu
            rstd = lax.rsqrt(_mean_lanes(cen * cen) + LN_EPS)
            ln = cen * rstd * lng[...] + lnb[...]
            sbo_ref[rows, :] = (ln * jax.nn.sigmoid(ln)).astype(BF16)

        y0[...] = jnp.dot(pa_ref[...], wa_v[...], preferred_element_type=F32)
        y1[...] = jnp.dot(sbo_ref[...], wb_v[...], preferred_element_type=F32)

        def merge(rows):
            ya, yb = y0[rows, :], y1[rows, :]
            ya_ref[rows, :] = ya.astype(BF16)
            yb_ref[rows, :] = yb.astype(BF16)
            mg_ref[rows, :] = (sa_ref[rows, :].astype(F32) * ya + sb_ref[rows, :].astype(F32) * yb).astype(BF16)
        _for_chunks(tm, ROW_CHUNK, merge)

        mix_ref[...] = jnp.dot(mg_ref[...], wo_v[...], preferred_element_type=F32)

        def resid(rows):
            mix = mix_ref[rows, :]
            r = lax.rsqrt(_mean_lanes(mix * mix) + RMS_EPS)
            x1_ref[rows, :] = x_ref[rows, :] + mix * r * g_ref[...]
        _for_chunks(tm, ROW_CHUNK, resid)

    bf = jax.ShapeDtypeStruct((t, d), BF16)
    f32 = jax.ShapeDtypeStruct((t, d), F32)
    return _call(
        body, name="mixer_fwd", grid=(n_steps,),
        in_specs=[_halo_prev(tm, HALO_A, d), _rows(tm, d), _halo_next(tm, HALO_A, d, t),
                  _halo_prev(tm, HALO_B, d), _rows(tm, d), _halo_next(tm, HALO_B, d, t),
                  _rows(tm, d), _rows(tm, d), _rows(tm, d), _rows(tm, d),
                  _const((ka, d)), _const((1, d)), _const((kb, d)), _const((1, d)),
                  _const((1, d)), _const((1, d)), ANY, ANY, ANY, _const((1, d))],
        out_specs=[_rows(tm, d)] * 9,
        out_shape=[f32, bf, bf, f32, bf, bf, bf, bf, f32],
        scratch_shapes=[pltpu.VMEM((tm + 2 * HALO_A, d), F32), pltpu.VMEM((tm + 2 * HALO_B, d), F32),
                        pltpu.VMEM((len(plan_a), _shifted_rows(tm, off_a), d), F32),
                        pltpu.VMEM((len(plan_b), _shifted_rows(tm, off_b), d), F32),
                        pltpu.VMEM((ka * SUBLANES, d), F32), pltpu.VMEM((kb * SUBLANES, d), F32),
                        pltpu.VMEM((d, d), BF16), pltpu.VMEM((d, d), BF16), pltpu.VMEM((d, d), BF16),
                        pltpu.VMEM((tm, d), F32), pltpu.VMEM((tm, d), F32),
                        pltpu.SemaphoreType.DMA((3,))],
        args=(ua, ua, ua, ub, ub, ub, bg, sa, sb, x, conv_a_w, conv_a_b, conv_b_w, conv_b_b,
              ln_g, ln_b, w_a, w_b, w_o, g1post), ex=ex)


def _mlp_call(x1, target, mix, g2pre, g2post, g1post, w1_g, w2, tm, ex=None):
    t, d = x1.shape
    nb, _, fq = w1_g.shape
    f = nb * fq
    n_steps = t // tm
    inv_d = 1.0 / d

    def body(x1_ref, t_ref, mix_ref, gpre, gpost, gmix, w1_hbm, w2_hbm,
             dx1_ref, f_ref, df2_ref, h2_ref, df1_ref, dmix_ref, dgpost_ref, dgpre_ref, dgmix_ref, loss_ref,
             w1_v, w2_v, f1_s, blk_s, f2_s, acc_post, acc_pre, acc_mix, acc_loss, sem):
        _load_blocks_once(w1_hbm, w1_v, sem)
        _load_once(w2_hbm, w2_v, sem.at[nb])

        @pl.when(pl.program_id(0) == 0)
        def _():
            acc_post[...] = jnp.zeros_like(acc_post)
            acc_pre[...] = jnp.zeros_like(acc_pre)
            acc_mix[...] = jnp.zeros_like(acc_mix)
            acc_loss[...] = jnp.zeros_like(acc_loss)

        def norm(rows):
            xv = x1_ref[rows, :]
            r = lax.rsqrt(_mean_lanes(xv * xv) + RMS_EPS)
            h2_ref[rows, :] = (xv * r * gpre[...]).astype(BF16)
        _for_chunks(tm, ROW_CHUNK, norm)

        for j in range(nb):
            cols = pl.ds(j * fq, fq)
            f1_s[:, cols] = jnp.dot(h2_ref[...], w1_v[:, cols], preferred_element_type=F32)

        def act(rows):
            relu = jnp.maximum(f1_s[rows, :], 0.0)
            f_ref[rows, :] = (relu * relu).astype(BF16)
        _for_chunks(tm, ROW_CHUNK, act)

        f2_s[...] = jnp.dot(f_ref[...], w2_v[...], preferred_element_type=F32)

        def head(rows):
            f2 = f2_s[rows, :]
            rf = lax.rsqrt(_mean_lanes(f2 * f2) + RMS_EPS)
            y = x1_ref[rows, :] + f2 * rf * gpost[...]
            err = y - t_ref[rows, :]
            acc_loss[...] += _fold8(err * err)
            dy = err * inv_d
            gdy = dy * gpost[...]
            df2 = rf * gdy - f2 * (rf * rf * rf * _mean_lanes(gdy * f2))
            df2_ref[rows, :] = df2.astype(BF16)
            acc_post[...] += _fold8(dy * f2 * rf)
            dx1_ref[rows, :] = dy
        _for_chunks(tm, ROW_CHUNK, head)

        for j in range(nb):
            cols = pl.ds(j * fq, fq)
            blk_s[...] = lax.dot_general(df2_ref[...], w2_v[cols, :], NT_DIMS, preferred_element_type=F32)

            def dact(rows):
                relu = jnp.maximum(f1_s[rows, cols], 0.0)
                df1_ref[rows, cols] = (blk_s[rows, :] * (2.0 * relu)).astype(BF16)
            _for_chunks(tm, ROW_CHUNK, dact)

        f2_s[...] = lax.dot_general(df1_ref[...], w1_v[...], NT_DIMS, preferred_element_type=F32)

        def dnorm(rows):
            dh2 = f2_s[rows, :]
            xv = x1_ref[rows, :]
            r = lax.rsqrt(_mean_lanes(xv * xv) + RMS_EPS)
            gd = dh2 * gpre[...]
            dxv = dx1_ref[rows, :] + r * gd - xv * (r * r * r * _mean_lanes(gd * xv))
            dx1_ref[rows, :] = dxv
            acc_pre[...] += _fold8(dh2 * xv * r)
            mix = mix_ref[rows, :]
            rm = lax.rsqrt(_mean_lanes(mix * mix) + RMS_EPS)
            gm = dxv * gmix[...]
            dmix_ref[rows, :] = (rm * gm - mix * (rm * rm * rm * _mean_lanes(gm * mix))).astype(BF16)
            acc_mix[...] += _fold8(dxv * mix * rm)
        _for_chunks(tm, ROW_CHUNK, dnorm)

        _write_row_sums(acc_post, dgpost_ref, n_steps)
        _write_row_sums(acc_pre, dgpre_ref, n_steps)
        _write_row_sums(acc_mix, dgmix_ref, n_steps)
        _write_row_sums(acc_loss, loss_ref, n_steps)

    row = jax.ShapeDtypeStruct((1, d), F32)
    return _call(
        body, name="mlp_fwd_bwd", grid=(n_steps,),
        in_specs=[_rows(tm, d), _rows(tm, d), _rows(tm, d), _const((1, d)), _const((1, d)), _const((1, d)), ANY, ANY],
        out_specs=[_rows(tm, d), _rows(tm, f), _rows(tm, d), _rows(tm, d), _rows(tm, f), _rows(tm, d),
                   _const((1, d)), _const((1, d)), _const((1, d)), _const((1, d))],
        out_shape=[jax.ShapeDtypeStruct((t, d), F32), jax.ShapeDtypeStruct((t, f), BF16),
                   jax.ShapeDtypeStruct((t, d), BF16), jax.ShapeDtypeStruct((t, d), BF16),
                   jax.ShapeDtypeStruct((t, f), BF16), jax.ShapeDtypeStruct((t, d), BF16), row, row, row, row],
        scratch_shapes=[pltpu.VMEM((d, f), BF16), pltpu.VMEM((f, d), BF16),
                        pltpu.VMEM((tm, f), F32), pltpu.VMEM((tm, fq), F32), pltpu.VMEM((tm, d), F32),
                        pltpu.VMEM((SUBLANES, d), F32), pltpu.VMEM((SUBLANES, d), F32),
                        pltpu.VMEM((SUBLANES, d), F32), pltpu.VMEM((SUBLANES, d), F32),
                        pltpu.SemaphoreType.DMA((nb + 1,))],
        args=(x1, target, mix, g2pre, g2post, g1post, w1_g, w2), ex=ex)


def _mixer_bwd_call(dmix, sa, sb, ya, yb, bg, va, cb, ln_g, ln_b, w_a, w_b, w_o, tm, ex=None):
    t, d = dmix.shape
    n_steps = t // tm

    def body(dmix_ref, sa_ref, sb_ref, ya_ref, yb_ref, bg_ref, va_ref, cb_ref, lng, lnb,
             wa_hbm, wb_hbm, wo_hbm,
             dya_ref, dyb_ref, dva_ref, dcb_ref, dbg_ref, dza_ref, dzb_ref,
             dlng_ref, dlnb_ref, dba_ref, dbb_ref, sbg_ref, sza_ref, szb_ref,
             wa_v, wb_v, wo_v, s0, s1, acc_lng, acc_lnb, acc_ba, acc_bb, acc_bg, acc_za, acc_zb, sem):
        _load_once(wa_hbm, wa_v, sem.at[0])
        _load_once(wb_hbm, wb_v, sem.at[1])
        _load_once(wo_hbm, wo_v, sem.at[2])
        accs = (acc_lng, acc_lnb, acc_ba, acc_bb, acc_bg, acc_za, acc_zb)

        @pl.when(pl.program_id(0) == 0)
        def _():
            for acc in accs:
                acc[...] = jnp.zeros_like(acc)

        s0[...] = lax.dot_general(dmix_ref[...], wo_v[...], NT_DIMS, preferred_element_type=F32)

        def dmerge(rows):
            dm = s0[rows, :]
            sav, sbv = sa_ref[rows, :].astype(F32), sb_ref[rows, :].astype(F32)
            dya_ref[rows, :] = (dm * sav).astype(BF16)
            dyb_ref[rows, :] = (dm * sbv).astype(BF16)
            dza = dm * ya_ref[rows, :].astype(F32) * sav * (1.0 - sav)
            dzb = dm * yb_ref[rows, :].astype(F32) * sbv * (1.0 - sbv)
            dza_ref[rows, :] = dza.astype(BF16)
            dzb_ref[rows, :] = dzb.astype(BF16)
            acc_za[...] += _fold8(dza)
            acc_zb[...] += _fold8(dzb)
        _for_chunks(tm, ROW_CHUNK, dmerge)

        s0[...] = lax.dot_general(dya_ref[...], wa_v[...], NT_DIMS, preferred_element_type=F32)
        s1[...] = lax.dot_general(dyb_ref[...], wb_v[...], NT_DIMS, preferred_element_type=F32)

        def dbranches(rows):
            dpa = s0[rows, :]
            dbg = dpa * va_ref[rows, :].astype(F32)
            dbg_ref[rows, :] = dbg.astype(BF16)
            acc_bg[...] += _fold8(dbg)
            dva = dpa * bg_ref[rows, :].astype(F32)
            dva_ref[rows, :] = dva
            acc_ba[...] += _fold8(dva)
            cbv = cb_ref[rows, :]
            mu = _mean_lanes(cbv)
            cen = cbv - mu
            rstd = lax.rsqrt(_mean_lanes(cen * cen) + LN_EPS)
            xhat = cen * rstd
            ln = xhat * lng[...] + lnb[...]
            sig = jax.nn.sigmoid(ln)
            dln = s1[rows, :] * (sig * (1.0 + ln * (1.0 - sig)))
            acc_lng[...] += _fold8(dln * xhat)
            acc_lnb[...] += _fold8(dln)
            dxh = dln * lng[...]
            dcb = rstd * (dxh - _mean_lanes(dxh) - xhat * _mean_lanes(dxh * xhat))
            dcb_ref[rows, :] = dcb
            acc_bb[...] += _fold8(dcb)
        _for_chunks(tm, ROW_CHUNK, dbranches)

        _write_row_sums(acc_lng, dlng_ref, n_steps)
        _write_row_sums(acc_lnb, dlnb_ref, n_steps)
        _write_row_sums(acc_ba, dba_ref, n_steps)
        _write_row_sums(acc_bb, dbb_ref, n_steps)
        _write_row_sums(acc_bg, sbg_ref, n_steps)
        _write_row_sums(acc_za, sza_ref, n_steps)
        _write_row_sums(acc_zb, szb_ref, n_steps)

    bf = jax.ShapeDtypeStruct((t, d), BF16)
    f32 = jax.ShapeDtypeStruct((t, d), F32)
    row = jax.ShapeDtypeStruct((1, d), F32)
    return _call(
        body, name="mixer_bwd", grid=(n_steps,),
        in_specs=[_rows(tm, d)] * 8 + [_const((1, d))] * 2 + [ANY, ANY, ANY],
        out_specs=[_rows(tm, d)] * 7 + [_const((1, d))] * 7,
        out_shape=[bf, bf, f32, f32, bf, bf, bf] + [row] * 7,
        scratch_shapes=[pltpu.VMEM((d, d), BF16), pltpu.VMEM((d, d), BF16), pltpu.VMEM((d, d), BF16),
                        pltpu.VMEM((tm, d), F32), pltpu.VMEM((tm, d), F32)]
        + [pltpu.VMEM((SUBLANES, d), F32)] * 7 + [pltpu.SemaphoreType.DMA((3,))],
        args=(dmix, sa, sb, ya, yb, bg, va, cb, ln_g, ln_b, w_a, w_b, w_o), ex=ex)


def _conv_bwd_call(dva, dcb, ua, ub, cg, ha, a, sg, dbg, dza, dzb, through_sums, conv_a_w, conv_b_w, tm, ex=None):
    t, d = dva.shape
    n_steps = t // tm
    ka, kb = conv_a_w.shape[0], conv_b_w.shape[0]
    off_a = [HALO_A + (ka - 1) // 2 - k for k in range(ka)]
    off_b = [HALO_B + (kb - 1) // 2 - k for k in range(kb)]
    plan_a, plan_b = _shift_plan(off_a), _shift_plan(off_b)

    def body(dvap, dvac, dvan, dcbp, dcbc, dcbn, ua_ref, ub_ref,
             cg_ref, ha_ref, a_ref, sg_ref, dbg_ref, dza_ref, dzb_ref, wa_c, wb_c, sbg_ref, sza_ref, szb_ref,
             dproj_ref, dwa_ref, dwb_ref, dbin_ref,
             e_dva, e_dcb, sh_a, sh_b, wa8, wb8, acc_wa, acc_wb, acc_bin):
        i = pl.program_id(0)
        _fill_tap_rows(wa_c, wa8)
        _fill_tap_rows(wb_c, wb8)

        @pl.when(i == 0)
        def _():
            acc_wa[...] = jnp.zeros_like(acc_wa)
            acc_wb[...] = jnp.zeros_like(acc_wb)
            acc_bin[...] = jnp.zeros_like(acc_bin)
            for col, s_ref in ((0, sbg_ref), (5, sza_ref), (6, szb_ref)):
                acc_bin[0:1, pl.ds(col * d, d)] = s_ref[...]

        _fill_ext(e_dva, dvap, dvac, dvan, HALO_A, tm, i, n_steps)
        _fill_ext(e_dcb, dcbp, dcbc, dcbn, HALO_B, tm, i, n_steps)
        _fill_shifted(e_dva, sh_a, plan_a)
        _fill_shifted(e_dcb, sh_b, plan_b)

        def put(col, rows, val_f32):
            dproj_ref[rows, pl.ds(col * d, d)] = val_f32.astype(BF16)
            acc_bin[:, pl.ds(col * d, d)] += _fold8(val_f32)

        for r0 in range(0, tm, CONV_ROWS):
            rows = pl.ds(r0, CONV_ROWS)
            ua_c, ub_c = ua_ref[rows, :], ub_ref[rows, :]
            dua = jnp.zeros((CONV_ROWS, d), F32)
            for k in range(ka):
                xk = _window(e_dva, sh_a, plan_a, off_a[k], r0)
                dua = dua + _tap(wa8, k) * xk
                acc_wa[pl.ds(k * SUBLANES, SUBLANES), :] += _fold8(ua_c * xk)
            dub = jnp.zeros((CONV_ROWS, d), F32)
            for k in range(kb):
                xk = _window(e_dcb, sh_b, plan_b, off_b[k], r0)
                dub = dub + _tap(wb8, k) * xk
                acc_wb[pl.ds(k * SUBLANES, SUBLANES), :] += _fold8(ub_c * xk)
            cgv, hav = cg_ref[rows, :].astype(F32), ha_ref[rows, :].astype(F32)
            av, sgv = a_ref[rows, :].astype(F32), sg_ref[rows, :].astype(F32)
            put(1, rows, dua * hav)
            put(2, rows, dua * cgv)
            put(3, rows, dub * sgv)
            put(4, rows, dub * av * sgv * (1.0 - sgv))
            for col, through in ((0, dbg_ref), (5, dza_ref), (6, dzb_ref)):
                dproj_ref[rows, pl.ds(col * d, d)] = through[rows, :]

        @pl.when(i == n_steps - 1)
        def _():
            for k in range(ka):
                dwa_ref[k:k + 1, :] = jnp.sum(acc_wa[pl.ds(k * SUBLANES, SUBLANES), :], axis=0, keepdims=True)
            for k in range(kb):
                dwb_ref[k:k + 1, :] = jnp.sum(acc_wb[pl.ds(k * SUBLANES, SUBLANES), :], axis=0, keepdims=True)
            dbin_ref[...] = jnp.sum(acc_bin[...], axis=0, keepdims=True)

    halo_a = [_halo_prev(tm, HALO_A, d), _rows(tm, d), _halo_next(tm, HALO_A, d, t)]
    halo_b = [_halo_prev(tm, HALO_B, d), _rows(tm, d), _halo_next(tm, HALO_B, d, t)]
    return _call(
        body, name="conv_bwd", grid=(n_steps,),
        in_specs=halo_a + halo_b + [_rows(tm, d)] * 9 + [_const((ka, d)), _const((kb, d))] + [_const((1, d))] * 3,
        out_specs=[_rows(tm, 7 * d), _const((ka, d)), _const((kb, d)), _const((1, 7 * d))],
        out_shape=[jax.ShapeDtypeStruct((t, 7 * d), BF16), jax.ShapeDtypeStruct((ka, d), F32),
                   jax.ShapeDtypeStruct((kb, d), F32), jax.ShapeDtypeStruct((1, 7 * d), F32)],
        scratch_shapes=[pltpu.VMEM((tm + 2 * HALO_A, d), F32), pltpu.VMEM((tm + 2 * HALO_B, d), F32),
                        pltpu.VMEM((len(plan_a), _shifted_rows(tm, off_a), d), F32),
                        pltpu.VMEM((len(plan_b), _shifted_rows(tm, off_b), d), F32),
                        pltpu.VMEM((ka * SUBLANES, d), F32), pltpu.VMEM((kb * SUBLANES, d), F32),
                        pltpu.VMEM((ka * SUBLANES, d), F32), pltpu.VMEM((kb * SUBLANES, d), F32),
                        pltpu.VMEM((SUBLANES, 7 * d), F32)],
        args=(dva, dva, dva, dcb, dcb, dcb, ua, ub, cg, ha, a, sg, dbg, dza, dzb,
              conv_a_w, conv_b_w, *through_sums), ex=ex)


def _dx_call(dproj, x, dx1, g1pre, w_in_g, tm, first, n_steps, prev, ex=None):
    t, d = x.shape
    nb, _, n4 = w_in_g.shape
    ni = nb * n4
    rows = lambda width: pl.BlockSpec((tm, width), lambda i: (i + first, 0))
    if prev is None:
        prev = (jnp.zeros((SUBLANES, 128), F32), jnp.zeros((1, d), F32))
    prev_dx, prev_dg = prev

    def body(dp_ref, x_ref, dx1_ref, g_ref, w_hbm, prev_dx_hbm, prev_dg_ref, dx_ref, dg_ref, w_v, dh_s, acc_g, sem):
        _load_blocks_once(w_hbm, w_v, sem)

        @pl.when(pl.program_id(0) == 0)
        def _():
            acc_g[...] = jnp.zeros_like(acc_g)
            acc_g[0:1, :] = prev_dg_ref[...]

        dh_s[...] = lax.dot_general(dp_ref[...], w_v[...], NT_DIMS, preferred_element_type=F32)

        def dnorm(rows):
            dh = dh_s[rows, :]
            xv = x_ref[rows, :]
            r = lax.rsqrt(_mean_lanes(xv * xv) + RMS_EPS)
            gd = dh * g_ref[...]
            dx_ref[rows, :] = dx1_ref[rows, :] + r * gd - xv * (r * r * r * _mean_lanes(gd * xv))
            acc_g[...] += _fold8(dh * xv * r)
        _for_chunks(tm, ROW_CHUNK, dnorm)
        _write_row_sums(acc_g, dg_ref, n_steps)

    return _call(
        body, name="dx_bwd_from_%d" % first, grid=(n_steps,),
        in_specs=[rows(ni), rows(d), rows(d), _const((1, d)), ANY, ANY, _const((1, d))],
        out_specs=[rows(d), _const((1, d))],
        out_shape=[jax.ShapeDtypeStruct((t, d), F32), jax.ShapeDtypeStruct((1, d), F32)],
        scratch_shapes=[pltpu.VMEM((d, ni), BF16), pltpu.VMEM((tm, d), F32),
                        pltpu.VMEM((SUBLANES, d), F32), pltpu.SemaphoreType.DMA((nb,))],
        args=(dproj, x, dx1, g1pre, w_in_g, prev_dx, prev_dg), ex=ex,
        aliases={5: 0} if first > 0 else None)


def _tn_matmul(a, g, nblk, a_cols, g_cols, a_blocked, g_blocked, tt, name, ex=None):
    t = a.shape[0]

    def body(a_ref, g_ref, o_ref):
        @pl.when(pl.program_id(1) == 0)
        def _():
            o_ref[...] = jnp.zeros_like(o_ref)
        o_ref[0] += lax.dot_general(a_ref[...], g_ref[...], TN_DIMS, preferred_element_type=F32)

    (out,), xouts = _call(
        body, name=name, grid=(nblk, t // tt),
        in_specs=[pl.BlockSpec((tt, a_cols), (lambda b, s: (s, b)) if a_blocked else (lambda b, s: (s, 0))),
                  pl.BlockSpec((tt, g_cols), (lambda b, s: (s, b)) if g_blocked else (lambda b, s: (s, 0)))],
        out_specs=[pl.BlockSpec((1, a_cols, g_cols), lambda b, s: (b, 0, 0))],
        out_shape=[jax.ShapeDtypeStruct((nblk, a_cols, g_cols), F32)],
        scratch_shapes=[], args=(a, g), ex=ex)
    return out, xouts


def _tn_matmuls(pairs, tt, name):
    t, d = pairs[0][0].shape
    k = len(pairs)

    def body(*refs):
        ins, outs = refs[:2 * k], refs[2 * k:]

        @pl.when(pl.program_id(0) == 0)
        def _():
            for o_ref in outs:
                o_ref[...] = jnp.zeros_like(o_ref)
        for j in range(k):
            outs[j][...] += lax.dot_general(ins[2 * j][...], ins[2 * j + 1][...], TN_DIMS, preferred_element_type=F32)

    outs, _ = _call(
        body, name=name, grid=(t // tt,), in_specs=[_rows(tt, d)] * (2 * k), out_specs=[_const((d, d))] * k,
        out_shape=[jax.ShapeDtypeStruct((d, d), F32)] * k, scratch_shapes=[],
        args=[m for pair in pairs for m in pair])
    return outs


def _pair_sum_call(g_full, from_sibling, core, name):
    nblk, r, c = g_full.shape
    hr = r // 2
    tr = min(hr, 256)
    n = hr // tr

    def body(core_ref, g_ref, p_ref, o_ref):
        o_ref[...] = (g_ref[...] + p_ref[...]).astype(BF16)

    return pl.pallas_call(
        body, name=name,
        grid_spec=pltpu.PrefetchScalarGridSpec(
            num_scalar_prefetch=1, grid=(nblk, n),
            in_specs=[pl.BlockSpec((1, tr, c), lambda j, i, cr: (j, cr[0] * n + i, 0)),
                      pl.BlockSpec((1, tr, c), lambda j, i, cr: (j, i, 0))],
            out_specs=pl.BlockSpec((1, tr, c), lambda j, i, cr: (j, i, 0))),
        out_shape=jax.ShapeDtypeStruct((nblk, hr, c), BF16),
        compiler_params=_params("parallel", "parallel"))(core, g_full, from_sibling)


def _chip_sum_call(pair, received, chip_core, name):
    _, hr, c = pair.shape
    tr = min(hr, 256)
    n = hr // tr

    def body(cc_ref, own_ref, r_ref, o_ref):
        o_ref[...] = ((own_ref[0].astype(F32) + r_ref[0].astype(F32)) + r_ref[1].astype(F32)) + r_ref[2].astype(F32)

    return pl.pallas_call(
        body, name=name,
        grid_spec=pltpu.PrefetchScalarGridSpec(
            num_scalar_prefetch=1, grid=(n,),
            in_specs=[pl.BlockSpec((1, tr, c), lambda i, cc: (cc[0], i, 0)),
                      pl.BlockSpec((N_CHIPS - 1, tr, c), lambda i, cc: (0, i, 0))],
            out_specs=pl.BlockSpec((tr, c), lambda i, cc: (cc[1] * n + i, 0))),
        out_shape=jax.ShapeDtypeStruct((2 * hr, c), F32),
        compiler_params=_params("parallel"))(chip_core, pair, received)


def _adamw(w, g, m, v):
    m = ADAM_B1 * m + (1.0 - ADAM_B1) * g
    v = ADAM_B2 * v + (1.0 - ADAM_B2) * (g * g)
    m_hat = m / (1.0 - ADAM_B1 ** ADAM_STEP)
    v_hat = v / (1.0 - ADAM_B2 ** ADAM_STEP)
    delta = -ADAM_LR * (m_hat / (jnp.sqrt(v_hat) + ADAM_EPS) + ADAM_WD * w)
    return delta, m, v


def _adam_call(w, g, m, v, name):
    r, c = w.shape
    tr = min(r, 256)

    def body(w_ref, g_ref, m_ref, v_ref, go_ref, d_ref, mo_ref, vo_ref):
        go_ref[...] = g_ref[...]
        d_ref[...], mo_ref[...], vo_ref[...] = _adamw(w_ref[...], g_ref[...], m_ref[...], v_ref[...])

    shape = jax.ShapeDtypeStruct((r, c), F32)
    return pl.pallas_call(
        body, name=name, grid=(r // tr,), in_specs=[_rows(tr, c)] * 4, out_specs=[_rows(tr, c)] * 4,
        out_shape=[shape] * 4, compiler_params=_params("parallel"))(w, g, m, v)


def _adam_sc_call(w, g, m, v, name):
    r, c = w.shape
    rows_tile = r // SC_TILES
    rr = min(rows_tile, SC_ROWS)

    def body(w_hbm, g_hbm, m_hbm, v_hbm, go_hbm, d_hbm, mo_hbm, vo_hbm, wb, gb, mb, vb):
        tile = lax.axis_index("sc_tile") * 2 + lax.axis_index("sc_core")

        @pl.loop(0, rows_tile, step=rr)
        def _(p):
            rows = pl.ds(tile * rows_tile + p, rr)
            pltpu.sync_copy(w_hbm.at[rows, :], wb)
            pltpu.sync_copy(g_hbm.at[rows, :], gb)
            pltpu.sync_copy(m_hbm.at[rows, :], mb)
            pltpu.sync_copy(v_hbm.at[rows, :], vb)

            @pl.loop(0, rr)
            def _(i):
                @pl.loop(0, c, step=SC_LANES)
                def _(j):
                    at = (i, pl.ds(j, SC_LANES))
                    delta, m2, v2 = _adamw(wb[at], gb[at], mb[at], vb[at])
                    wb[at] = delta
                    mb[at] = m2
                    vb[at] = v2

            pltpu.sync_copy(gb, go_hbm.at[rows, :])
            pltpu.sync_copy(wb, d_hbm.at[rows, :])
            pltpu.sync_copy(mb, mo_hbm.at[rows, :])
            pltpu.sync_copy(vb, vo_hbm.at[rows, :])

    shape = jax.ShapeDtypeStruct((r, c), F32)
    return pl.kernel(
        body, name=name, out_type=[shape] * 4,
        mesh=plsc.VectorSubcoreMesh(core_axis_name="sc_core", subcore_axis_name="sc_tile"),
        scratch_types=[pltpu.VMEM((rr, c), F32)] * 4)(w, g, m, v)


def _gather_first_call(buf, others, conv_a_w, conv_b_w, d):
    ka, dq = conv_a_w.shape
    kb = conv_b_w.shape[0]
    ra = -(-ka // SUBLANES) * SUBLANES
    rb = -(-kb // SUBLANES) * SUBLANES
    a_pad = jnp.pad(conv_a_w, ((0, ra - ka), (0, 0)))
    b_pad = jnp.pad(conv_b_w, ((0, rb - kb), (0, 0)))
    hr = buf.shape[1] // 2
    qr = hr // 2
    n_o = len(others)
    stage_rows = max(o.shape[0] for o in others)
    stage_cols = others[0].shape[1]
    assert all(o.shape[1] == stage_cols for o in others)

    def body(w_in, a_ref, b_ref, *rest):
        others_in, w_out, oa_ref, ob_ref = rest[:n_o], rest[n_o], rest[n_o + 1], rest[n_o + 2]
        others_out = rest[n_o + 3:2 * n_o + 3]
        pack, slots, stage_f, stage_b, send, recv, csend, crecv, osem = rest[2 * n_o + 3:]
        x, y, c = _place()
        me, x_chip, y_chip, d_chip = 2 * x + y, 2 * (1 - x) + y, 2 * x + (1 - y), 2 * (1 - x) + (1 - y)
        x_nbr, y_nbr, sibling = (1 - x, y, c), (x, 1 - y, c), (x, y, 1 - c)
        mine, theirs = pl.ds(c * hr, hr), pl.ds((1 - c) * hr, hr)
        first, second = pl.ds(c * hr, qr), pl.ds(c * hr + qr, qr)

        def rows_of(chip, rows):
            return w_out.at[chip, rows, :]

        def copy(k, src, dst, peer):
            return _remote(src, dst, send.at[k], recv.at[k], peer)

        def landed(k, dst):
            copy(k, dst, dst, sibling).wait_recv()

        pack[pl.ds(0, ra), :] = a_ref[...]
        pack[pl.ds(ra, rb), :] = b_ref[...]
        chips = _other_chips(x, y)
        conv = [_remote(pack, slots.at[me], csend.at[j], crecv.at[j], (px, py, c)) for j, (px, py, _) in enumerate(chips)]
        started = [copy(0, w_in.at[me, mine, :], rows_of(me, mine), x_nbr),
                   copy(1, w_in.at[me, mine, :], rows_of(me, mine), y_nbr)]
        for cp in conv + started:
            cp.start()

        def go(cp):
            cp.start()
            started.append(cp)

        for src, dst in zip(others_in, others_out):
            r = src.shape[0]
            f_view, b_view = stage_f.at[pl.ds(0, r), :], stage_b.at[pl.ds(0, r), :]
            fetch = pltpu.make_async_copy(src, f_view, osem.at[0])
            fetch.start()
            fetch.wait()
            b_view[...] = f_view[...].astype(BF16)
            place = pltpu.make_async_copy(b_view, dst.at[me], osem.at[1])
            place.start()
            place.wait()

        landed(0, rows_of(x_chip, mine))
        go(copy(2, rows_of(x_chip, first), rows_of(x_chip, first), y_nbr))
        go(copy(4, rows_of(x_chip, mine), rows_of(x_chip, mine), sibling))
        landed(1, rows_of(y_chip, mine))
        go(copy(3, rows_of(y_chip, second), rows_of(y_chip, second), x_nbr))
        go(copy(5, rows_of(y_chip, mine), rows_of(y_chip, mine), sibling))
        landed(2, rows_of(d_chip, first))
        landed(3, rows_of(d_chip, second))
        go(copy(6, rows_of(d_chip, mine), rows_of(d_chip, mine), sibling))
        for k, chip in ((4, x_chip), (5, y_chip), (6, d_chip)):
            landed(k, rows_of(chip, theirs))
        for cp in started:
            cp.wait_send()

        for j, (px, py, pk) in enumerate(chips):
            _remote(pack, slots.at[pk], csend.at[j], crecv.at[j], (px, py, c)).wait_recv()
        for cp in conv:
            cp.wait_send()
        slots[me] = pack[...]
        for k in range(N_CHIPS):
            oa_ref[:, pl.ds(k * dq, dq)] = slots[k, pl.ds(0, ra), :]
            ob_ref[:, pl.ds(k * dq, dq)] = slots[k, pl.ds(ra, rb), :]

    n_w = 7
    res = pl.pallas_call(
        body, name="gather_first", in_specs=[ANY, VMEM_FULL, VMEM_FULL] + [ANY] * n_o,
        out_specs=[ANY, VMEM_FULL, VMEM_FULL] + [ANY] * n_o,
        out_shape=[_sds(buf), jax.ShapeDtypeStruct((ra, d), F32), jax.ShapeDtypeStruct((rb, d), F32)]
        + [jax.ShapeDtypeStruct((N_CHIPS,) + o.shape, BF16) for o in others],
        scratch_shapes=[pltpu.VMEM((ra + rb, dq), F32), pltpu.VMEM((N_CHIPS, ra + rb, dq), F32),
                        pltpu.VMEM((stage_rows, stage_cols), F32), pltpu.VMEM((stage_rows, stage_cols), BF16),
                        pltpu.SemaphoreType.DMA((n_w,)), pltpu.SemaphoreType.DMA((n_w,)),
                        pltpu.SemaphoreType.DMA((N_CHIPS - 1,)), pltpu.SemaphoreType.DMA((N_CHIPS - 1,)),
                        pltpu.SemaphoreType.DMA((2,))],
        input_output_aliases={0: 0},
        compiler_params=pltpu.CompilerParams(has_side_effects=True, vmem_limit_bytes=VMEM_LIMIT))(
            buf, a_pad, b_pad, *others)
    return res[0], res[1][:ka], res[2][:kb], list(res[3:])


def _small_step_call(partials, loss_rows, weights, m_s, v_s, sharded, d, ex):
    n = len(partials)
    n_xi, n_xo = len(ex.inputs), len(ex.out_shapes)
    row_counts = [p.shape[0] for p in partials]
    starts = [sum(row_counts[:i]) for i in range(n)]
    loss_row = sum(row_counts)
    pack_rows = -(-(loss_row + 1) // SUBLANES) * SUBLANES
    dq = d // N_CHIPS

    def body(*refs):
        p_refs = refs[:n]
        loss_in = refs[n]
        w_refs = refs[n + 1:2 * n + 1]
        m_refs = refs[2 * n + 1:3 * n + 1]
        v_refs = refs[3 * n + 1:4 * n + 1]
        xin = refs[4 * n + 1:4 * n + 1 + n_xi]
        o = 4 * n + 1 + n_xi
        g_out = refs[o:o + n]
        d_out = refs[o + n:o + 2 * n]
        m_out = refs[o + 2 * n:o + 3 * n]
        v_out = refs[o + 3 * n:o + 4 * n]
        loss_out = refs[o + 4 * n]
        xout = refs[o + 4 * n + 1:o + 4 * n + 1 + n_xo]
        pack, from_sibling, slots, send_sem, recv_sem, xsend, xrecv = refs[o + 4 * n + 1 + n_xo:]
        x, y, c = _place()
        me = 2 * x + y
        riding = ex.copies(xin, xout, lambda i: xsend.at[i], lambda i: xrecv.at[i])
        for cp in riding:
            cp.start()

        pack[...] = jnp.zeros_like(pack)
        for i in range(n):
            pack[pl.ds(starts[i], row_counts[i]), :] = p_refs[i][...]
        pack[pl.ds(loss_row, 1), :] = loss_in[...]

        pair = _remote(pack, from_sibling, send_sem.at[0], recv_sem.at[0], (x, y, 1 - c))
        pair.start()
        pair.wait()
        pack[...] = pack[...] + from_sibling[...]
        chips = _other_chips(x, y)
        copies = [_remote(pack, slots.at[me], send_sem.at[1 + j], recv_sem.at[1 + j], (px, py, c))
                  for j, (px, py, _) in enumerate(chips)]
        for cp in copies:
            cp.start()
        for j, (px, py, pk) in enumerate(chips):
            _remote(pack, slots.at[pk], send_sem.at[1 + j], recv_sem.at[1 + j], (px, py, c)).wait_recv()
        for cp in copies:
            cp.wait_send()

        slots[me] = pack[...]
        total = slots[0]
        for k in range(1, N_CHIPS):
            total = total + slots[k]
        pack[...] = total

        loss_out[...] = jnp.broadcast_to(
            (0.5 / d) * jnp.sum(pack[pl.ds(loss_row, 1), :], axis=-1, keepdims=True), loss_out.shape)
        chip = 2 * x + y
        for i in range(n):
            rows = pl.ds(starts[i], row_counts[i])
            if sharded[i]:
                for k in range(N_CHIPS):
                    @pl.when(chip == k)
                    def _():
                        g_out[i][...] = pack[rows, pl.ds(k * dq, dq)]
            else:
                g_out[i][...] = pack[rows, :]
            d_out[i][...], m_out[i][...], v_out[i][...] = _adamw(
                w_refs[i][...], g_out[i][...], m_refs[i][...], v_refs[i][...])
        for cp in riding:
            cp.wait()

    w_shapes = [jax.ShapeDtypeStruct(w.shape, F32) for w in weights]
    n_in, n_out = 4 * n + 1, 4 * n + 1
    res = pl.pallas_call(
        body, name="small_grads_allreduce_adamw",
        in_specs=[VMEM_FULL] * n_in + [ANY] * n_xi, out_specs=[VMEM_FULL] * n_out + [ANY] * n_xo,
        out_shape=w_shapes * 4 + [jax.ShapeDtypeStruct((SUBLANES, 128), F32)] + ex.out_shapes,
        scratch_shapes=[pltpu.VMEM((pack_rows, d), F32), pltpu.VMEM((pack_rows, d), F32),
                        pltpu.VMEM((N_CHIPS, pack_rows, d), F32),
                        pltpu.SemaphoreType.DMA((N_CHIPS,)), pltpu.SemaphoreType.DMA((N_CHIPS,)),
                        pltpu.SemaphoreType.DMA((ex.n_sems,)), pltpu.SemaphoreType.DMA((ex.n_sems,))],
        input_output_aliases={n_in + i: n_out + o for i, o in ex.aliases.items()},
        compiler_params=pltpu.CompilerParams(has_side_effects=True, vmem_limit_bytes=VMEM_LIMIT))(
            *partials, loss_rows, *weights, *m_s, *v_s, *ex.inputs)
    return list(res[:n_out]), list(res[n_out:])


def _tile(t, want):
    return min(t, want)


def kernel(x, norm1_pre_g, w_in, b_in, conv_a_w, conv_a_b, w_a_out, conv_b_w, conv_b_b, ln_b_g, ln_b_b, w_b_out, w_o, norm1_post_g, norm2_pre_g, w_mlp_in, w_mlp_out, norm2_post_g, loss_target, m_norm1_pre_g, m_w_in, m_b_in, m_conv_a_w, m_conv_a_b, m_w_a_out, m_conv_b_w, m_conv_b_b, m_ln_b_g, m_ln_b_b, m_w_b_out, m_w_o, m_norm1_post_g, m_norm2_pre_g, m_w_mlp_in, m_w_mlp_out, m_norm2_post_g, v_norm1_pre_g, v_w_in, v_b_in, v_conv_a_w, v_conv_a_b, v_w_a_out, v_conv_b_w, v_conv_b_b, v_ln_b_g, v_ln_b_b, v_w_b_out, v_w_o, v_norm1_post_g, v_norm2_pre_g, v_w_mlp_in, v_w_mlp_out, v_norm2_post_g):
    _, t, d = x.shape
    xt = x.reshape(t, d)
    tgt = loss_target.reshape(t, d)
    row = lambda vec: vec.reshape(1, -1)
    cx, cy, cc = _place()
    core = cc.astype(jnp.int32).reshape(1)
    chip = (2 * cx + cy).astype(jnp.int32).reshape(1)

    big = dict(w_in=w_in, w_a_out=w_a_out, w_b_out=w_b_out, w_o=w_o, w_mlp_in=w_mlp_in, w_mlp_out=w_mlp_out)
    names = list(big)
    chip_core = jnp.concatenate([chip, core])
    mixer_w, mlp_w = ["w_a_out", "w_b_out", "w_o"], ["w_mlp_in", "w_mlp_out"]
    rows_of = lambda buf: buf.reshape(-1, buf.shape[-1])

    def pair_sums(keys, full, from_sibling):
        return [_pair_sum_call(g, p, core, "pair_sum_" + k) for k, g, p in zip(keys, full, from_sibling)]

    def chip_sums(keys, pairs, received):
        return [_chip_sum_call(p, r, chip_core, "chip_sum_" + k) for k, p, r in zip(keys, pairs, received)]

    w_in_g, conv_a_full, conv_b_full, slots = _gather_first_call(
        _cast_to_slot(w_in, chip, "cast_w_in"), [big[k] for k in mixer_w + mlp_w], conv_a_w, conv_b_w, d)

    g1pre, g1post, g2pre, g2post = row(norm1_pre_g), row(norm1_post_g), row(norm2_pre_g), row(norm2_post_g)
    lng, lnb, ba, bb = row(ln_b_g), row(ln_b_b), row(conv_a_b), row(conv_b_b)

    tm_proj = _tile(t, 512)
    n_proj = t // tm_proj
    if n_proj >= 3:
        (h, ua, ub, bg, cg, ha, a, sg, sa, sb), landed = _proj_call(
            xt, g1pre, w_in_g, row(b_in), tm_proj, ex=_ex_gather_ici(slots),
            then=(min(n_proj - 2, (7 * n_proj) // 8), _ex_gather_forward(slots[:3])))
        w_a_g, w_b_g, w_o_g = landed[:3]
    else:
        (h, ua, ub, bg, cg, ha, a, sg, sa, sb), landed = _proj_call(
            xt, g1pre, w_in_g, row(b_in), tm_proj, ex=_ex_gather_ici(slots))
        w_a_g, w_b_g, w_o_g = _exchange_call("forward_mixer_weights", [_ex_gather_forward(landed[:3])])
    w_a_full, w_b_full, w_o_full = rows_of(w_a_g), rows_of(w_b_g), rows_of(w_o_g)
    (x1, va, pa, cb, sbo, ya, yb, mg, mix), (w1_g, w2_g) = _mixer_fwd_call(
        ua, ub, bg, sa, sb, xt, conv_a_full, ba, conv_b_full, bb, lng, lnb,
        w_a_full, w_b_full, w_o_full, g1post, _tile(t, 256), ex=_ex_gather_forward(landed[3:]))
    (dx1, f, df2, h2, df1, dmix, dg2post, dg2pre, dg1post, loss_rows), _ = _mlp_call(
        x1, tgt, mix, g2pre, g2post, g1post, w1_g, rows_of(w2_g), _tile(t, 256))

    tt = _tile(t, 2048)
    n4, fq, dq = w_in.shape[1], w_mlp_in.shape[1], d // N_CHIPS
    g_mlp = [_tn_matmul(h2, df1, N_CHIPS, d, fq, False, True, tt, "dw_mlp_in")[0],
             _tn_matmul(f, df2, N_CHIPS, fq, d, True, False, tt, "dw_mlp_out")[0]]
    (dya, dyb, dva, dcb, dbg, dza, dzb, dlng, dlnb, dba, dbb, sbg, sza, szb), sib_mlp = _mixer_bwd_call(
        dmix, sa, sb, ya, yb, bg, va, cb, lng, lnb, w_a_full, w_b_full, w_o_full, _tile(t, 512),
        ex=_ex_sibling_halves(g_mlp))
    p_mlp = pair_sums(mlp_w, g_mlp, sib_mlp)
    g_mix = [g.reshape(N_CHIPS, dq, d)
             for g in _tn_matmuls([(pa, dya), (sbo, dyb), (mg, dmix)], _tile(t, 1024), "dw_mixer")]
    ex_a, ex_b = _ex_scatter_to_owner(p_mlp), _ex_sibling_halves(g_mix)
    (dproj, dwa_conv, dwb_conv, dbin), xo = _conv_bwd_call(
        dva, dcb, ua, ub, cg, ha, a, sg, dbg, dza, dzb, (sbg, sza, szb), conv_a_full, conv_b_full, _tile(t, 256),
        ex=_merge(ex_a, ex_b))
    recv_mlp, sib_mix = _split(xo, ex_a, ex_b)
    r_mlp = chip_sums(mlp_w, p_mlp, recv_mlp)
    p_mix = pair_sums(mixer_w, g_mix, sib_mix)
    ex_a, ex_b = _ex_share_halves(r_mlp), _ex_scatter_to_owner(p_mix)
    g_in, xo = _tn_matmul(h, dproj, N_CHIPS, d, n4, False, True, tt, "dw_in", ex=_merge(ex_a, ex_b))
    red_mlp, recv_mix = _split(xo, ex_a, ex_b)
    r_mix = chip_sums(mixer_w, p_mix, recv_mix)
    ex_a, ex_b = _ex_sibling_halves([g_in]), _ex_share_halves(r_mix)
    tm_dx = _tile(t, 512)
    n_dx = t // tm_dx
    n_a = max(1, (3 * n_dx) // 8)
    dx_done, xo = _dx_call(dproj, xt, dx1, g1pre, w_in_g, tm_dx, 0, n_a, None, ex=_merge(ex_a, ex_b))
    sib_in, red_mix = _split(xo, ex_a, ex_b)
    p_in = pair_sums(["w_in"], [g_in], sib_in)
    (grad_x, dg1pre), recv_in = _dx_call(dproj, xt, dx1, g1pre, w_in_g, tm_dx, n_a, n_dx - n_a, dx_done,
                                         ex=_ex_scatter_to_owner(p_in))
    r_in = chip_sums(["w_in"], p_in, recv_in)
    out = {}

    small = [
        ("conv_b_w", dwb_conv, conv_b_w, m_conv_b_w, v_conv_b_w, True),
        ("conv_b_b", dbb, bb, row(m_conv_b_b), row(v_conv_b_b), False),
        ("b_in", dbin.reshape(7, d), b_in.reshape(7, d), m_b_in.reshape(7, d), v_b_in.reshape(7, d), False),
        ("norm1_pre_g", dg1pre, row(norm1_pre_g), row(m_norm1_pre_g), row(v_norm1_pre_g), False),
        ("conv_a_w", dwa_conv, conv_a_w, m_conv_a_w, v_conv_a_w, True),
        ("conv_a_b", dba, ba, row(m_conv_a_b), row(v_conv_a_b), False),
        ("ln_b_g", dlng, lng, row(m_ln_b_g), row(v_ln_b_g), False),
        ("ln_b_b", dlnb, lnb, row(m_ln_b_b), row(v_ln_b_b), False),
        ("norm1_post_g", dg1post, g1post, row(m_norm1_post_g), row(v_norm1_post_g), False),
        ("norm2_pre_g", dg2pre, g2pre, row(m_norm2_pre_g), row(v_norm2_pre_g), False),
        ("norm2_post_g", dg2post, g2post, row(m_norm2_post_g), row(v_norm2_post_g), False),
    ]
    res, red_in = _small_step_call([s[1] for s in small], loss_rows, [s[2] for s in small], [s[3] for s in small],
                                   [s[4] for s in small], [s[5] for s in small], d, _ex_share_halves(r_in))
    reduced = dict(zip(mlp_w + mixer_w + ["w_in"], red_mlp + red_mix + red_in))
    moments = dict(w_in=(m_w_in, v_w_in), w_a_out=(m_w_a_out, v_w_a_out), w_b_out=(m_w_b_out, v_w_b_out),
                   w_o=(m_w_o, v_w_o), w_mlp_in=(m_w_mlp_in, v_w_mlp_in), w_mlp_out=(m_w_mlp_out, v_w_mlp_out))
    for k in names:
        adam = _adam_sc_call if k != "w_in" and big[k].shape[0] % (SC_TILES * SUBLANES) == 0 else _adam_call
        out[k] = tuple(adam(big[k], reduced[k], *moments[k], "adamw_" + k))
    ns = len(small)
    loss = res[4 * ns][0, 0]
    shapes = dict(norm1_pre_g=norm1_pre_g.shape, b_in=b_in.shape, conv_a_w=conv_a_w.shape,
                  conv_a_b=conv_a_b.shape, conv_b_w=conv_b_w.shape, conv_b_b=conv_b_b.shape,
                  ln_b_g=ln_b_g.shape, ln_b_b=ln_b_b.shape, norm1_post_g=norm1_post_g.shape,
                  norm2_pre_g=norm2_pre_g.shape, norm2_post_g=norm2_post_g.shape)
    for i, s in enumerate(small):
        out[s[0]] = tuple(res[q * ns + i].reshape(shapes[s[0]]) for q in range(4))

    order = ["norm1_pre_g", "w_in", "b_in", "conv_a_w", "conv_a_b", "w_a_out", "conv_b_w", "conv_b_b",
             "ln_b_g", "ln_b_b", "w_b_out", "w_o", "norm1_post_g", "norm2_pre_g", "w_mlp_in", "w_mlp_out",
             "norm2_post_g"]
    return (loss, grad_x.reshape(x.shape), *[out[k][0] for k in order], *[out[k][1] for k in order],
            *[out[k][2] for k in order], *[out[k][3] for k in order])
```

```python
import functools

import jax
import jax.numpy as jnp
from jax import lax
from jax.experimental import pallas as pl
from jax.experimental.pallas import tpu as pltpu
from jax.experimental.pallas import tpu_sc as plsc

RMS_EPS = 1e-6
LN_EPS = 1e-5
ADAM_LR = 0.001
ADAM_B1 = 0.9
ADAM_B2 = 0.999
ADAM_EPS = 1e-08
ADAM_WD = 0.01
ADAM_STEP = 10

F32 = jnp.float32
BF16 = jnp.bfloat16
MESH = pl.DeviceIdType.MESH
ANY = pl.BlockSpec(memory_space=pl.ANY)
VMEM_FULL = pl.BlockSpec(memory_space=pltpu.VMEM)

V7X_VMEM_BYTES = 64 * 1024 * 1024
VMEM_LIMIT = V7X_VMEM_BYTES - 8 * 1024 * 1024
SUBLANES = 8
N_CHIPS = 4
N_DEV = 8
SC_TILES = 32
SC_LANES = 16
SC_ROWS = 16
HALO_A = 8
HALO_B = 16
CONV_ROWS = 16
ROW_CHUNK = 32

NT_DIMS = (((1,), (1,)), ((), ()))
TN_DIMS = (((0,), (0,)), ((), ()))


def _params(*sem):
    return pltpu.CompilerParams(dimension_semantics=sem, vmem_limit_bytes=VMEM_LIMIT)


def _rows(tm, d):
    return pl.BlockSpec((tm, d), lambda i: (i, 0))


def _const(shape):
    return pl.BlockSpec(shape, lambda i: (0,) * len(shape))


def _halo_prev(tm, hb, d):
    return pl.BlockSpec((hb, d), lambda i: (jnp.maximum(i * (tm // hb) - 1, 0), 0))


def _halo_next(tm, hb, d, t):
    return pl.BlockSpec((hb, d), lambda i: (jnp.minimum((i + 1) * (tm // hb), t // hb - 1), 0))


def _for_chunks(n_rows, rc, fn):
    for r0 in range(0, n_rows, rc):
        fn(pl.ds(r0, rc))


def _fold8(v):
    return v.reshape(v.shape[0] // SUBLANES, SUBLANES, v.shape[1]).sum(axis=0)


def _mean_lanes(v):
    return jnp.mean(v, axis=-1, keepdims=True)


def _load_blocks_once(w_hbm, w_vmem, sem):
    nb, _, n = w_hbm.shape

    @pl.when(pl.program_id(0) == 0)
    def _():
        copies = [pltpu.make_async_copy(w_hbm.at[j], w_vmem.at[:, pl.ds(j * n, n)], sem.at[j])
                  for j in range(nb)]
        for cp in copies:
            cp.start()
        for cp in copies:
            cp.wait()


def _load_once(w_hbm, w_vmem, sem):
    @pl.when(pl.program_id(0) == 0)
    def _():
        cp = pltpu.make_async_copy(w_hbm, w_vmem, sem)
        cp.start()
        cp.wait()


def _write_row_sums(acc_ref, out_ref, n_steps):
    @pl.when(pl.program_id(0) == n_steps - 1)
    def _():
        out_ref[...] = jnp.sum(acc_ref[...], axis=0, keepdims=True)


def _place():
    return lax.axis_index("x"), lax.axis_index("y"), lax.axis_index("c")


def _other_chips(x, y):
    rel = [(x, 1 - y), (1 - x, y), (1 - x, 1 - y)]
    return [(px, py, 2 * px + py) for px, py in rel]


class _Exchange:
    def __init__(self, inputs, out_shapes, aliases, n_sems, copies):
        self.inputs = list(inputs)
        self.out_shapes = list(out_shapes)
        self.aliases = dict(aliases)
        self.n_sems = n_sems
        self.copies = copies


def _remote(src, dst, send, recv, device):
    return pltpu.make_async_remote_copy(src_ref=src, dst_ref=dst, send_sem=send, recv_sem=recv,
                                        device_id=device, device_id_type=MESH)


def _sds(a):
    return jax.ShapeDtypeStruct(a.shape, a.dtype)


def _ex_gather_ici(bufs):
    n = len(bufs)

    def copies(xin, xout, send, recv):
        x, y, c = _place()
        me = 2 * x + y
        out = []
        for a in range(n):
            hr = xin[a].shape[1] // 2
            rows = pl.ds(c * hr, hr)
            for j, (px, py, _) in enumerate(_other_chips(x, y)):
                k = a * (N_CHIPS - 1) + j
                out.append(_remote(xin[a].at[me, rows, :], xout[a].at[me, rows, :], send(k), recv(k), (px, py, c)))
        return out

    return _Exchange(bufs, [_sds(b) for b in bufs], {a: a for a in range(n)}, n * (N_CHIPS - 1), copies)


def _ex_gather_forward(bufs):
    n = len(bufs)

    def copies(xin, xout, send, recv):
        x, y, c = _place()
        out = []
        for a in range(n):
            hr = xin[a].shape[1] // 2
            rows = pl.ds(c * hr, hr)
            for j, (_, _, pk) in enumerate(_other_chips(x, y)):
                k = a * (N_CHIPS - 1) + j
                out.append(_remote(xin[a].at[pk, rows, :], xout[a].at[pk, rows, :], send(k), recv(k), (x, y, 1 - c)))
        return out

    return _Exchange(bufs, [_sds(b) for b in bufs], {a: a for a in range(n)}, n * (N_CHIPS - 1), copies)


def _ex_sibling_halves(grads):
    n = len(grads)

    def copies(xin, xout, send, recv):
        x, y, c = _place()
        out = []
        for a in range(n):
            hr = xin[a].shape[1] // 2
            out.append(_remote(xin[a].at[:, pl.ds((1 - c) * hr, hr), :], xout[a], send(a), recv(a), (x, y, 1 - c)))
        return out

    shapes = [jax.ShapeDtypeStruct((g.shape[0], g.shape[1] // 2, g.shape[2]), g.dtype) for g in grads]
    return _Exchange(grads, shapes, {}, n, copies)


def _ex_scatter_to_owner(pairs):
    n = len(pairs)

    def copies(xin, xout, send, recv):
        x, y, c = _place()
        out = []
        for a in range(n):
            for j, (px, py, pk) in enumerate(_other_chips(x, y)):
                k = a * (N_CHIPS - 1) + j
                out.append(_remote(xin[a].at[pk], xout[a].at[j], send(k), recv(k), (px, py, c)))
        return out

    shapes = [jax.ShapeDtypeStruct((N_CHIPS - 1,) + p.shape[1:], p.dtype) for p in pairs]
    return _Exchange(pairs, shapes, {}, n * (N_CHIPS - 1), copies)


def _ex_share_halves(reduced):
    n = len(reduced)

    def copies(xin, xout, send, recv):
        x, y, c = _place()
        out = []
        for a in range(n):
            hr = xin[a].shape[0] // 2
            rows = pl.ds(c * hr, hr)
            out.append(_remote(xin[a].at[rows, :], xout[a].at[rows, :], send(a), recv(a), (x, y, 1 - c)))
        return out

    return _Exchange(reduced, [_sds(r) for r in reduced], {a: a for a in range(n)}, n, copies)


def _merge(*exs):
    exs = [e for e in exs if e is not None]
    if not exs:
        return None
    inputs, shapes, aliases = [], [], {}
    in_off, out_off, sem_off = [], [], []
    n_sems = 0
    for e in exs:
        in_off.append(len(inputs))
        out_off.append(len(shapes))
        sem_off.append(n_sems)
        aliases.update({len(inputs) + i: len(shapes) + o for i, o in e.aliases.items()})
        inputs += e.inputs
        shapes += e.out_shapes
        n_sems += e.n_sems

    def copies(xin, xout, send, recv):
        out = []
        for e, io, oo, so in zip(exs, in_off, out_off, sem_off):
            out += e.copies(xin[io:io + len(e.inputs)], xout[oo:oo + len(e.out_shapes)],
                            lambda i, so=so: send(so + i), lambda i, so=so: recv(so + i))
        return out

    return _Exchange(inputs, shapes, aliases, n_sems, copies)


def _split(ex_outs, *exs):
    parts, o = [], 0
    for e in exs:
        parts.append(list(ex_outs[o:o + len(e.out_shapes)]))
        o += len(e.out_shapes)
    return parts


def _call(body, *, name, grid, in_specs, out_specs, out_shape, scratch_shapes, args, ex=None, aliases=None,
          then=None):
    n_in, n_out, n_scr = len(in_specs), len(out_specs), len(scratch_shapes)
    seq = ("arbitrary",) * len(grid)
    aliases = dict(aliases or {})
    if ex is None:
        outs = pl.pallas_call(
            body, name=name, grid=grid, in_specs=list(in_specs), out_specs=list(out_specs),
            out_shape=list(out_shape), scratch_shapes=list(scratch_shapes), input_output_aliases=aliases,
            compiler_params=_params(*seq))(*args)
        return list(outs), []
    n_xi, n_xo = len(ex.inputs), len(ex.out_shapes)

    def full(*refs):
        ins, xin = refs[:n_in], refs[n_in:n_in + n_xi]
        o = n_in + n_xi
        outs, xout = refs[o:o + n_out], refs[o + n_out:o + n_out + n_xo]
        s = o + n_out + n_xo
        scr = refs[s:s + n_scr]
        send_sems, recv_sems = refs[s + n_scr], refs[s + n_scr + 1]
        send = lambda i: send_sems.at[i]
        recv = lambda i: recv_sems.at[i]
        first = functools.reduce(jnp.logical_and, [pl.program_id(a) == 0 for a in range(len(grid))])
        last = functools.reduce(jnp.logical_and, [pl.program_id(a) == grid[a] - 1 for a in range(len(grid))])

        @pl.when(first)
        def _():
            for cp in ex.copies(xin, xout, send, recv):
                cp.start()

        body(*ins, *outs, *scr)

        if then is None:
            @pl.when(last)
            def _():
                for cp in ex.copies(xin, xout, send, recv):
                    cp.wait()
        else:
            step, ex2 = then
            send2_sems, recv2_sems = refs[s + n_scr + 2], refs[s + n_scr + 3]
            send2 = lambda i: send2_sems.at[i]
            recv2 = lambda i: recv2_sems.at[i]

            @pl.when(pl.program_id(0) == step)
            def _():
                for cp in ex.copies(xin, xout, send, recv):
                    cp.wait()
                for cp in ex2.copies(xin, xout, send2, recv2):
                    cp.start()

            @pl.when(last)
            def _():
                for cp in ex2.copies(xin, xout, send2, recv2):
                    cp.wait()

    sems = [pltpu.SemaphoreType.DMA((ex.n_sems,)), pltpu.SemaphoreType.DMA((ex.n_sems,))]
    if then is not None:
        assert len(grid) == 1 and 0 < then[0] < grid[0] - 1
        sems += [pltpu.SemaphoreType.DMA((then[1].n_sems,)), pltpu.SemaphoreType.DMA((then[1].n_sems,))]
    res = pl.pallas_call(
        full, name=name, grid=grid, in_specs=list(in_specs) + [ANY] * n_xi,
        out_specs=list(out_specs) + [ANY] * n_xo, out_shape=list(out_shape) + ex.out_shapes,
        scratch_shapes=list(scratch_shapes) + sems,
        input_output_aliases={**aliases, **{n_in + i: n_out + o for i, o in ex.aliases.items()}},
        compiler_params=pltpu.CompilerParams(dimension_semantics=seq, vmem_limit_bytes=VMEM_LIMIT,
                                             has_side_effects=True))(*args, *ex.inputs)
    return list(res[:n_out]), list(res[n_out:])


def _exchange_call(name, phases):
    first = phases[0]
    n_xi, n_xo = len(first.inputs), len(first.out_shapes)

    def body(*refs):
        xin, xout = refs[:n_xi], refs[n_xi:n_xi + n_xo]
        sems = refs[n_xi + n_xo:]
        for p, ex in enumerate(phases):
            send_sems, recv_sems = sems[2 * p], sems[2 * p + 1]
            cps = ex.copies(xin, xout, lambda i: send_sems.at[i], lambda i: recv_sems.at[i])
            for cp in cps:
                cp.start()
            for cp in cps:
                cp.wait()

    sems = []
    for ex in phases:
        sems += [pltpu.SemaphoreType.DMA((ex.n_sems,)), pltpu.SemaphoreType.DMA((ex.n_sems,))]
    return list(pl.pallas_call(
        body, name=name, in_specs=[ANY] * n_xi, out_specs=[ANY] * n_xo, out_shape=first.out_shapes,
        scratch_shapes=sems, input_output_aliases=dict(first.aliases),
        compiler_params=pltpu.CompilerParams(has_side_effects=True))(*first.inputs))


def _cast_to_slot(w, chip, name):
    r, c = w.shape
    tr = min(r, 256)

    def body(chip_ref, w_ref, o_ref):
        o_ref[0] = w_ref[...].astype(BF16)

    return pl.pallas_call(
        body, name=name,
        grid_spec=pltpu.PrefetchScalarGridSpec(
            num_scalar_prefetch=1, grid=(r // tr,),
            in_specs=[pl.BlockSpec((tr, c), lambda i, k: (i, 0))],
            out_specs=pl.BlockSpec((1, tr, c), lambda i, k: (k[0], i, 0))),
        out_shape=jax.ShapeDtypeStruct((N_CHIPS, r, c), BF16),
        compiler_params=_params("parallel"))(chip, w)


def _proj_call(x, g1pre, w_in_g, b_in, tm, ex=None, then=None):
    t, d = x.shape
    nb, _, n4 = w_in_g.shape
    ni = nb * n4
    assert ni == 7 * d

    def body(x_ref, g_ref, b_ref, w_hbm, h_ref, ua_ref, ub_ref, bg_ref, cg_ref, ha_ref, a_ref,
             sg_ref, sa_ref, sb_ref, w_v, p0, p1, sem):
        _load_blocks_once(w_hbm, w_v, sem)

        def norm(rows):
            xv = x_ref[rows, :]
            r = lax.rsqrt(_mean_lanes(xv * xv) + RMS_EPS)
            h_ref[rows, :] = (xv * r * g_ref[...]).astype(BF16)
        _for_chunks(tm, ROW_CHUNK, norm)

        def group(i, dst):
            cols = pl.ds(i * d, d)
            dst[...] = jnp.dot(h_ref[...], w_v[:, cols], preferred_element_type=F32) + b_ref[:, cols]

        group(0, p0)

        def bgate(rows):
            bg_ref[rows, :] = p0[rows, :].astype(BF16)
        _for_chunks(tm, ROW_CHUNK, bgate)

        group(1, p0)
        group(2, p1)

        def branch_a(rows):
            cg, ha = p0[rows, :], p1[rows, :]
            ua_ref[rows, :] = cg * ha
            cg_ref[rows, :] = cg.astype(BF16)
            ha_ref[rows, :] = ha.astype(BF16)
        _for_chunks(tm, ROW_CHUNK, branch_a)

        group(3, p0)
        group(4, p1)

        def branch_b(rows):
            a, sg = p0[rows, :], jax.nn.sigmoid(p1[rows, :])
            ub_ref[rows, :] = a * sg
            a_ref[rows, :] = a.astype(BF16)
            sg_ref[rows, :] = sg.astype(BF16)
        _for_chunks(tm, ROW_CHUNK, branch_b)

        group(5, p0)
        group(6, p1)

        def gates(rows):
            sa_ref[rows, :] = jax.nn.sigmoid(p0[rows, :]).astype(BF16)
            sb_ref[rows, :] = jax.nn.sigmoid(p1[rows, :]).astype(BF16)
        _for_chunks(tm, ROW_CHUNK, gates)

    bf = jax.ShapeDtypeStruct((t, d), BF16)
    f32 = jax.ShapeDtypeStruct((t, d), F32)
    return _call(
        body, name="proj_fwd", grid=(t // tm,),
        in_specs=[_rows(tm, d), _const((1, d)), _const((1, ni)), ANY],
        out_specs=[_rows(tm, d)] * 10,
        out_shape=[bf, f32, f32, bf, bf, bf, bf, bf, bf, bf],
        scratch_shapes=[pltpu.VMEM((d, ni), BF16), pltpu.VMEM((tm, d), F32), pltpu.VMEM((tm, d), F32),
                        pltpu.SemaphoreType.DMA((nb,))],
        args=(x, g1pre, b_in, w_in_g), ex=ex, then=then)


def _fill_ext(ext, prev_ref, cur_ref, next_ref, hb, tm, i, n_steps):
    ext[pl.ds(0, hb), :] = jnp.where(i > 0, prev_ref[...], 0.0)
    ext[pl.ds(hb, tm), :] = cur_ref[...]
    ext[pl.ds(hb + tm, hb), :] = jnp.where(i < n_steps - 1, next_ref[...], 0.0)


def _shift_plan(offsets):
    shifts = sorted({o % SUBLANES for o in offsets if o % SUBLANES})
    return {s: i for i, s in enumerate(shifts)}


def _shifted_rows(tm, offsets):
    return tm + SUBLANES * max(o // SUBLANES for o in offsets)


def _fill_shifted(ext, sh, plan):
    n = sh.shape[1]
    for s, i in plan.items():
        sh[i, :, :] = ext[pl.ds(s, n), :]


def _fill_tap_rows(w_ref, rows8):
    @pl.when(pl.program_id(0) == 0)
    def _():
        for k in range(w_ref.shape[0]):
            rows8[pl.ds(k * SUBLANES, SUBLANES), :] = jnp.broadcast_to(w_ref[k:k + 1, :], (SUBLANES, w_ref.shape[1]))


def _tap(rows8, k):
    w8 = rows8[pl.ds(k * SUBLANES, SUBLANES), :]
    return jnp.concatenate([w8] * (CONV_ROWS // SUBLANES), axis=0)


def _window(ext, sh, plan, offset, r0):
    q, s = divmod(offset, SUBLANES)
    if s == 0:
        return ext[pl.ds(offset + r0, CONV_ROWS), :]
    return sh[plan[s], pl.ds(SUBLANES * q + r0, CONV_ROWS), :]


def _mixer_fwd_call(ua, ub, bg, sa, sb, x, conv_a_w, conv_a_b, conv_b_w, conv_b_b, ln_g, ln_b,
                    w_a, w_b, w_o, g1post, tm, ex=None):
    t, d = x.shape
    n_steps = t // tm
    ka, kb = conv_a_w.shape[0], conv_b_w.shape[0]
    off_a = [HALO_A - (ka - 1) // 2 + k for k in range(ka)]
    off_b = [HALO_B - (kb - 1) // 2 + k for k in range(kb)]
    plan_a, plan_b = _shift_plan(off_a), _shift_plan(off_b)

    def body(uap, uac, uan, ubp, ubc, ubn, bg_ref, sa_ref, sb_ref, x_ref, wa_c, ba_c, wb_c, bb_c,
             lng, lnb, wa_hbm, wb_hbm, wo_hbm, g_ref,
             x1_ref, va_ref, pa_ref, cb_ref, sbo_ref, ya_ref, yb_ref, mg_ref, mix_ref,
             ext_a, ext_b, sh_a, sh_b, wa8, wb8, wa_v, wb_v, wo_v, y0, y1, sem):
        i = pl.program_id(0)
        _fill_tap_rows(wa_c, wa8)
        _fill_tap_rows(wb_c, wb8)
        _load_once(wa_hbm, wa_v, sem.at[0])
        _load_once(wb_hbm, wb_v, sem.at[1])
        _load_once(wo_hbm, wo_v, sem.at[2])
        _fill_ext(ext_a, uap, uac, uan, HALO_A, tm, i, n_steps)
        _fill_ext(ext_b, ubp, ubc, ubn, HALO_B, tm, i, n_steps)
        _fill_shifted(ext_a, sh_a, plan_a)
        _fill_shifted(ext_b, sh_b, plan_b)

        for r0 in range(0, tm, CONV_ROWS):
            rows = pl.ds(r0, CONV_ROWS)
            va = jnp.broadcast_to(ba_c[...], (CONV_ROWS, d))
            for k in range(ka):
                va = va + _tap(wa8, k) * _window(ext_a, sh_a, plan_a, off_a[k], r0)
            va_ref[rows, :] = va.astype(BF16)
            pa_ref[rows, :] = (bg_ref[rows, :].astype(F32) * va).astype(BF16)
            cb = jnp.broadcast_to(bb_c[...], (CONV_ROWS, d))
            for k in range(kb):
                cb = cb + _tap(wb8, k) * _window(ext_b, sh_b, plan_b, off_b[k], r0)
            cb_ref[rows, :] = cb
            mu = _mean_lanes(cb)
            cen = cb - mu
            rstd = lax.rsqrt(_mean_lanes(cen * cen) + LN_EPS)
            ln = cen * rstd * lng[...] + lnb[...]
            sbo_ref[rows, :] = (ln * jax.nn.sigmoid(ln)).astype(BF16)

        y0[...] = jnp.dot(pa_ref[...], wa_v[...], preferred_element_type=F32)
        y1[...] = jnp.dot(sbo_ref[...], wb_v[...], preferred_element_type=F32)

        def merge(rows):
            ya, yb = y0[rows, :], y1[rows, :]
            ya_ref[rows, :] = ya.astype(BF16)
            yb_ref[rows, :] = yb.astype(BF16)
            mg_ref[rows, :] = (sa_ref[rows, :].astype(F32) * ya + sb_ref[rows, :].astype(F32) * yb).astype(BF16)
        _for_chunks(tm, ROW_CHUNK, merge)

        mix_ref[...] = jnp.dot(mg_ref[...], wo_v[...], preferred_element_type=F32)

        def resid(rows):
            mix = mix_ref[rows, :]
            r = lax.rsqrt(_mean_lanes(mix * mix) + RMS_EPS)
            x1_ref[rows, :] = x_ref[rows, :] + mix * r * g_ref[...]
        _for_chunks(tm, ROW_CHUNK, resid)

    bf = jax.ShapeDtypeStruct((t, d), BF16)
    f32 = jax.ShapeDtypeStruct((t, d), F32)
    return _call(
        body, name="mixer_fwd", grid=(n_steps,),
        in_specs=[_halo_prev(tm, HALO_A, d), _rows(tm, d), _halo_next(tm, HALO_A, d, t),
                  _halo_prev(tm, HALO_B, d), _rows(tm, d), _halo_next(tm, HALO_B, d, t),
                  _rows(tm, d), _rows(tm, d), _rows(tm, d), _rows(tm, d),
                  _const((ka, d)), _const((1, d)), _const((kb, d)), _const((1, d)),
                  _const((1, d)), _const((1, d)), ANY, ANY, ANY, _const((1, d))],
        out_specs=[_rows(tm, d)] * 9,
        out_shape=[f32, bf, bf, f32, bf, bf, bf, bf, f32],
        scratch_shapes=[pltpu.VMEM((tm + 2 * HALO_A, d), F32), pltpu.VMEM((tm + 2 * HALO_B, d), F32),
                        pltpu.VMEM((len(plan_a), _shifted_rows(tm, off_a), d), F32),
                        pltpu.VMEM((len(plan_b), _shifted_rows(tm, off_b), d), F32),
                        pltpu.VMEM((ka * SUBLANES, d), F32), pltpu.VMEM((kb * SUBLANES, d), F32),
                        pltpu.VMEM((d, d), BF16), pltpu.VMEM((d, d), BF16), pltpu.VMEM((d, d), BF16),
                        pltpu.VMEM((tm, d), F32), pltpu.VMEM((tm, d), F32),
                        pltpu.SemaphoreType.DMA((3,))],
        args=(ua, ua, ua, ub, ub, ub, bg, sa, sb, x, conv_a_w, conv_a_b, conv_b_w, conv_b_b,
              ln_g, ln_b, w_a, w_b, w_o, g1post), ex=ex)


def _mlp_call(x1, target, mix, g2pre, g2post, g1post, w1_g, w2, tm, ex=None):
    t, d = x1.shape
    nb, _, fq = w1_g.shape
    f = nb * fq
    n_steps = t // tm
    inv_d = 1.0 / d

    def body(x1_ref, t_ref, mix_ref, gpre, gpost, gmix, w1_hbm, w2_hbm,
             dx1_ref, f_ref, df2_ref, h2_ref, df1_ref, dmix_ref, dgpost_ref, dgpre_ref, dgmix_ref, loss_ref,
             w1_v, w2_v, f1_s, blk_s, f2_s, acc_post, acc_pre, acc_mix, acc_loss, sem):
        _load_blocks_once(w1_hbm, w1_v, sem)
        _load_once(w2_hbm, w2_v, sem.at[nb])

        @pl.when(pl.program_id(0) == 0)
        def _():
            acc_post[...] = jnp.zeros_like(acc_post)
            acc_pre[...] = jnp.zeros_like(acc_pre)
            acc_mix[...] = jnp.zeros_like(acc_mix)
            acc_loss[...] = jnp.zeros_like(acc_loss)

        def norm(rows):
            xv = x1_ref[rows, :]
            r = lax.rsqrt(_mean_lanes(xv * xv) + RMS_EPS)
            h2_ref[rows, :] = (xv * r * gpre[...]).astype(BF16)
        _for_chunks(tm, ROW_CHUNK, norm)

        for j in range(nb):
            cols = pl.ds(j * fq, fq)
            f1_s[:, cols] = jnp.dot(h2_ref[...], w1_v[:, cols], preferred_element_type=F32)

        def act(rows):
            relu = jnp.maximum(f1_s[rows, :], 0.0)
            f_ref[rows, :] = (relu * relu).astype(BF16)
        _for_chunks(tm, ROW_CHUNK, act)

        f2_s[...] = jnp.dot(f_ref[...], w2_v[...], preferred_element_type=F32)

        def head(rows):
            f2 = f2_s[rows, :]
            rf = lax.rsqrt(_mean_lanes(f2 * f2) + RMS_EPS)
            y = x1_ref[rows, :] + f2 * rf * gpost[...]
            err = y - t_ref[rows, :]
            acc_loss[...] += _fold8(err * err)
            dy = err * inv_d
            gdy = dy * gpost[...]
            df2 = rf * gdy - f2 * (rf * rf * rf * _mean_lanes(gdy * f2))
            df2_ref[rows, :] = df2.astype(BF16)
            acc_post[...] += _fold8(dy * f2 * rf)
            dx1_ref[rows, :] = dy
        _for_chunks(tm, ROW_CHUNK, head)

        for j in range(nb):
            cols = pl.ds(j * fq, fq)
            blk_s[...] = lax.dot_general(df2_ref[...], w2_v[cols, :], NT_DIMS, preferred_element_type=F32)

            def dact(rows):
                relu = jnp.maximum(f1_s[rows, cols], 0.0)
                df1_ref[rows, cols] = (blk_s[rows, :] * (2.0 * relu)).astype(BF16)
            _for_chunks(tm, ROW_CHUNK, dact)

        f2_s[...] = lax.dot_general(df1_ref[...], w1_v[...], NT_DIMS, preferred_element_type=F32)

        def dnorm(rows):
            dh2 = f2_s[rows, :]
            xv = x1_ref[rows, :]
            r = lax.rsqrt(_mean_lanes(xv * xv) + RMS_EPS)
            gd = dh2 * gpre[...]
            dxv = dx1_ref[rows, :] + r * gd - xv * (r * r * r * _mean_lanes(gd * xv))
            dx1_ref[rows, :] = dxv
            acc_pre[...] += _fold8(dh2 * xv * r)
            mix = mix_ref[rows, :]
            rm = lax.rsqrt(_mean_lanes(mix * mix) + RMS_EPS)
            gm = dxv * gmix[...]
            dmix_ref[rows, :] = (rm * gm - mix * (rm * rm * rm * _mean_lanes(gm * mix))).astype(BF16)
            acc_mix[...] += _fold8(dxv * mix * rm)
        _for_chunks(tm, ROW_CHUNK, dnorm)

        _write_row_sums(acc_post, dgpost_ref, n_steps)
        _write_row_sums(acc_pre, dgpre_ref, n_steps)
        _write_row_sums(acc_mix, dgmix_ref, n_steps)
        _write_row_sums(acc_loss, loss_ref, n_steps)

    row = jax.ShapeDtypeStruct((1, d), F32)
    return _call(
        body, name="mlp_fwd_bwd", grid=(n_steps,),
        in_specs=[_rows(tm, d), _rows(tm, d), _rows(tm, d), _const((1, d)), _const((1, d)), _const((1, d)), ANY, ANY],
        out_specs=[_rows(tm, d), _rows(tm, f), _rows(tm, d), _rows(tm, d), _rows(tm, f), _rows(tm, d),
                   _const((1, d)), _const((1, d)), _const((1, d)), _const((1, d))],
        out_shape=[jax.ShapeDtypeStruct((t, d), F32), jax.ShapeDtypeStruct((t, f), BF16),
                   jax.ShapeDtypeStruct((t, d), BF16), jax.ShapeDtypeStruct((t, d), BF16),
                   jax.ShapeDtypeStruct((t, f), BF16), jax.ShapeDtypeStruct((t, d), BF16), row, row, row, row],
        scratch_shapes=[pltpu.VMEM((d, f), BF16), pltpu.VMEM((f, d), BF16),
                        pltpu.VMEM((tm, f), F32), pltpu.VMEM((tm, fq), F32), pltpu.VMEM((tm, d), F32),
                        pltpu.VMEM((SUBLANES, d), F32), pltpu.VMEM((SUBLANES, d), F32),
                        pltpu.VMEM((SUBLANES, d), F32), pltpu.VMEM((SUBLANES, d), F32),
                        pltpu.SemaphoreType.DMA((nb + 1,))],
        args=(x1, target, mix, g2pre, g2post, g1post, w1_g, w2), ex=ex)


def _mixer_bwd_call(dmix, sa, sb, ya, yb, bg, va, cb, ln_g, ln_b, w_a, w_b, w_o, tm, ex=None):
    t, d = dmix.shape
    n_steps = t // tm

    def body(dmix_ref, sa_ref, sb_ref, ya_ref, yb_ref, bg_ref, va_ref, cb_ref, lng, lnb,
             wa_hbm, wb_hbm, wo_hbm,
             dya_ref, dyb_ref, dva_ref, dcb_ref, dbg_ref, dza_ref, dzb_ref,
             dlng_ref, dlnb_ref, dba_ref, dbb_ref, sbg_ref, sza_ref, szb_ref,
             wa_v, wb_v, wo_v, s0, s1, acc_lng, acc_lnb, acc_ba, acc_bb, acc_bg, acc_za, acc_zb, sem):
        _load_once(wa_hbm, wa_v, sem.at[0])
        _load_once(wb_hbm, wb_v, sem.at[1])
        _load_once(wo_hbm, wo_v, sem.at[2])
        accs = (acc_lng, acc_lnb, acc_ba, acc_bb, acc_bg, acc_za, acc_zb)

        @pl.when(pl.program_id(0) == 0)
        def _():
            for acc in accs:
                acc[...] = jnp.zeros_like(acc)

        s0[...] = lax.dot_general(dmix_ref[...], wo_v[...], NT_DIMS, preferred_element_type=F32)

        def dmerge(rows):
            dm = s0[rows, :]
            sav, sbv = sa_ref[rows, :].astype(F32), sb_ref[rows, :].astype(F32)
            dya_ref[rows, :] = (dm * sav).astype(BF16)
            dyb_ref[rows, :] = (dm * sbv).astype(BF16)
            dza = dm * ya_ref[rows, :].astype(F32) * sav * (1.0 - sav)
            dzb = dm * yb_ref[rows, :].astype(F32) * sbv * (1.0 - sbv)
            dza_ref[rows, :] = dza.astype(BF16)
            dzb_ref[rows, :] = dzb.astype(BF16)
            acc_za[...] += _fold8(dza)
            acc_zb[...] += _fold8(dzb)
        _for_chunks(tm, ROW_CHUNK, dmerge)

        s0[...] = lax.dot_general(dya_ref[...], wa_v[...], NT_DIMS, preferred_element_type=F32)
        s1[...] = lax.dot_general(dyb_ref[...], wb_v[...], NT_DIMS, preferred_element_type=F32)

        def dbranches(rows):
            dpa = s0[rows, :]
            dbg = dpa * va_ref[rows, :].astype(F32)
            dbg_ref[rows, :] = dbg.astype(BF16)
            acc_bg[...] += _fold8(dbg)
            dva = dpa * bg_ref[rows, :].astype(F32)
            dva_ref[rows, :] = dva
            acc_ba[...] += _fold8(dva)
            cbv = cb_ref[rows, :]
            mu = _mean_lanes(cbv)
            cen = cbv - mu
            rstd = lax.rsqrt(_mean_lanes(cen * cen) + LN_EPS)
            xhat = cen * rstd
            ln = xhat * lng[...] + lnb[...]
            sig = jax.nn.sigmoid(ln)
            dln = s1[rows, :] * (sig * (1.0 + ln * (1.0 - sig)))
            acc_lng[...] += _fold8(dln * xhat)
            acc_lnb[...] += _fold8(dln)
            dxh = dln * lng[...]
            dcb = rstd * (dxh - _mean_lanes(dxh) - xhat * _mean_lanes(dxh * xhat))
            dcb_ref[rows, :] = dcb
            acc_bb[...] += _fold8(dcb)
        _for_chunks(tm, ROW_CHUNK, dbranches)

        _write_row_sums(acc_lng, dlng_ref, n_steps)
        _write_row_sums(acc_lnb, dlnb_ref, n_steps)
        _write_row_sums(acc_ba, dba_ref, n_steps)
        _write_row_sums(acc_bb, dbb_ref, n_steps)
        _write_row_sums(acc_bg, sbg_ref, n_steps)
        _write_row_sums(acc_za, sza_ref, n_steps)
        _write_row_sums(acc_zb, szb_ref, n_steps)

    bf = jax.ShapeDtypeStruct((t, d), BF16)
    f32 = jax.ShapeDtypeStruct((t, d), F32)
    row = jax.ShapeDtypeStruct((1, d), F32)
    return _call(
        body, name="mixer_bwd", grid=(n_steps,),
        in_specs=[_rows(tm, d)] * 8 + [_const((1, d))] * 2 + [ANY, ANY, ANY],
        out_specs=[_rows(tm, d)] * 7 + [_const((1, d))] * 7,
        out_shape=[bf, bf, f32, f32, bf, bf, bf] + [row] * 7,
        scratch_shapes=[pltpu.VMEM((d, d), BF16), pltpu.VMEM((d, d), BF16), pltpu.VMEM((d, d), BF16),
                        pltpu.VMEM((tm, d), F32), pltpu.VMEM((tm, d), F32)]
        + [pltpu.VMEM((SUBLANES, d), F32)] * 7 + [pltpu.SemaphoreType.DMA((3,))],
        args=(dmix, sa, sb, ya, yb, bg, va, cb, ln_g, ln_b, w_a, w_b, w_o), ex=ex)


def _conv_bwd_call(dva, dcb, ua, ub, cg, ha, a, sg, dbg, dza, dzb, through_sums, conv_a_w, conv_b_w, tm, ex=None):
    t, d = dva.shape
    n_steps = t // tm
    ka, kb = conv_a_w.shape[0], conv_b_w.shape[0]
    off_a = [HALO_A + (ka - 1) // 2 - k for k in range(ka)]
    off_b = [HALO_B + (kb - 1) // 2 - k for k in range(kb)]
    plan_a, plan_b = _shift_plan(off_a), _shift_plan(off_b)

    def body(dvap, dvac, dvan, dcbp, dcbc, dcbn, ua_ref, ub_ref,
             cg_ref, ha_ref, a_ref, sg_ref, dbg_ref, dza_ref, dzb_ref, wa_c, wb_c, sbg_ref, sza_ref, szb_ref,
             dproj_ref, dwa_ref, dwb_ref, dbin_ref,
             e_dva, e_dcb, sh_a, sh_b, wa8, wb8, acc_wa, acc_wb, acc_bin):
        i = pl.program_id(0)
        _fill_tap_rows(wa_c, wa8)
        _fill_tap_rows(wb_c, wb8)

        @pl.when(i == 0)
        def _():
            acc_wa[...] = jnp.zeros_like(acc_wa)
            acc_wb[...] = jnp.zeros_like(acc_wb)
            acc_bin[...] = jnp.zeros_like(acc_bin)
            for col, s_ref in ((0, sbg_ref), (5, sza_ref), (6, szb_ref)):
                acc_bin[0:1, pl.ds(col * d, d)] = s_ref[...]

        _fill_ext(e_dva, dvap, dvac, dvan, HALO_A, tm, i, n_steps)
        _fill_ext(e_dcb, dcbp, dcbc, dcbn, HALO_B, tm, i, n_steps)
        _fill_shifted(e_dva, sh_a, plan_a)
        _fill_shifted(e_dcb, sh_b, plan_b)

        def put(col, rows, val_f32):
            dproj_ref[rows, pl.ds(col * d, d)] = val_f32.astype(BF16)
            acc_bin[:, pl.ds(col * d, d)] += _fold8(val_f32)

        for r0 in range(0, tm, CONV_ROWS):
            rows = pl.ds(r0, CONV_ROWS)
            ua_c, ub_c = ua_ref[rows, :], ub_ref[rows, :]
            dua = jnp.zeros((CONV_ROWS, d), F32)
            for k in range(ka):
                xk = _window(e_dva, sh_a, plan_a, off_a[k], r0)
                dua = dua + _tap(wa8, k) * xk
                acc_wa[pl.ds(k * SUBLANES, SUBLANES), :] += _fold8(ua_c * xk)
            dub = jnp.zeros((CONV_ROWS, d), F32)
            for k in range(kb):
                xk = _window(e_dcb, sh_b, plan_b, off_b[k], r0)
                dub = dub + _tap(wb8, k) * xk
                acc_wb[pl.ds(k * SUBLANES, SUBLANES), :] += _fold8(ub_c * xk)
            cgv, hav = cg_ref[rows, :].astype(F32), ha_ref[rows, :].astype(F32)
            av, sgv = a_ref[rows, :].astype(F32), sg_ref[rows, :].astype(F32)
            put(1, rows, dua * hav)
            put(2, rows, dua * cgv)
            put(3, rows, dub * sgv)
            put(4, rows, dub * av * sgv * (1.0 - sgv))
            for col, through in ((0, dbg_ref), (5, dza_ref), (6, dzb_ref)):
                dproj_ref[rows, pl.ds(col * d, d)] = through[rows, :]

        @pl.when(i == n_steps - 1)
        def _():
            for k in range(ka):
                dwa_ref[k:k + 1, :] = jnp.sum(acc_wa[pl.ds(k * SUBLANES, SUBLANES), :], axis=0, keepdims=True)
            for k in range(kb):
                dwb_ref[k:k + 1, :] = jnp.sum(acc_wb[pl.ds(k * SUBLANES, SUBLANES), :], axis=0, keepdims=True)
            dbin_ref[...] = jnp.sum(acc_bin[...], axis=0, keepdims=True)

    halo_a = [_halo_prev(tm, HALO_A, d), _rows(tm, d), _halo_next(tm, HALO_A, d, t)]
    halo_b = [_halo_prev(tm, HALO_B, d), _rows(tm, d), _halo_next(tm, HALO_B, d, t)]
    return _call(
        body, name="conv_bwd", grid=(n_steps,),
        in_specs=halo_a + halo_b + [_rows(tm, d)] * 9 + [_const((ka, d)), _const((kb, d))] + [_const((1, d))] * 3,
        out_specs=[_rows(tm, 7 * d), _const((ka, d)), _const((kb, d)), _const((1, 7 * d))],
        out_shape=[jax.ShapeDtypeStruct((t, 7 * d), BF16), jax.ShapeDtypeStruct((ka, d), F32),
                   jax.ShapeDtypeStruct((kb, d), F32), jax.ShapeDtypeStruct((1, 7 * d), F32)],
        scratch_shapes=[pltpu.VMEM((tm + 2 * HALO_A, d), F32), pltpu.VMEM((tm + 2 * HALO_B, d), F32),
                        pltpu.VMEM((len(plan_a), _shifted_rows(tm, off_a), d), F32),
                        pltpu.VMEM((len(plan_b), _shifted_rows(tm, off_b), d), F32),
                        pltpu.VMEM((ka * SUBLANES, d), F32), pltpu.VMEM((kb * SUBLANES, d), F32),
                        pltpu.VMEM((ka * SUBLANES, d), F32), pltpu.VMEM((kb * SUBLANES, d), F32),
                        pltpu.VMEM((SUBLANES, 7 * d), F32)],
        args=(dva, dva, dva, dcb, dcb, dcb, ua, ub, cg, ha, a, sg, dbg, dza, dzb,
              conv_a_w, conv_b_w, *through_sums), ex=ex)


def _dx_call(dproj, x, dx1, g1pre, w_in_g, tm, first, n_steps, prev, ex=None):
    t, d = x.shape
    nb, _, n4 = w_in_g.shape
    ni = nb * n4
    rows = lambda width: pl.BlockSpec((tm, width), lambda i: (i + first, 0))
    if prev is None:
        prev = (jnp.zeros((SUBLANES, 128), F32), jnp.zeros((1, d), F32))
    prev_dx, prev_dg = prev

    def body(dp_ref, x_ref, dx1_ref, g_ref, w_hbm, prev_dx_hbm, prev_dg_ref, dx_ref, dg_ref, w_v, dh_s, acc_g, sem):
        at_start = pl.program_id(0) == 0
        loads = [pltpu.make_async_copy(w_hbm.at[j], w_v.at[:, pl.ds(j * n4, n4)], sem.at[j]) for j in range(nb)]

        @pl.when(at_start)
        def _():
            for cp in loads:
                cp.start()
            acc_g[...] = jnp.zeros_like(acc_g)
            acc_g[0:1, :] = prev_dg_ref[...]

        for j in range(nb):
            @pl.when(at_start)
            def _():
                loads[j].wait()
            cols = pl.ds(j * n4, n4)
            part = lax.dot_general(dp_ref[:, cols], w_v[:, cols], NT_DIMS, preferred_element_type=F32)
            if j == 0:
                dh_s[...] = part
            else:
                dh_s[...] += part

        def dnorm(rows):
            dh = dh_s[rows, :]
            xv = x_ref[rows, :]
            r = lax.rsqrt(_mean_lanes(xv * xv) + RMS_EPS)
            gd = dh * g_ref[...]
            dx_ref[rows, :] = dx1_ref[rows, :] + r * gd - xv * (r * r * r * _mean_lanes(gd * xv))
            acc_g[...] += _fold8(dh * xv * r)
        _for_chunks(tm, ROW_CHUNK, dnorm)
        _write_row_sums(acc_g, dg_ref, n_steps)

    return _call(
        body, name="dx_bwd_from_%d" % first, grid=(n_steps,),
        in_specs=[rows(ni), rows(d), rows(d), _const((1, d)), ANY, ANY, _const((1, d))],
        out_specs=[rows(d), _const((1, d))],
        out_shape=[jax.ShapeDtypeStruct((t, d), F32), jax.ShapeDtypeStruct((1, d), F32)],
        scratch_shapes=[pltpu.VMEM((d, ni), BF16), pltpu.VMEM((tm, d), F32),
                        pltpu.VMEM((SUBLANES, d), F32), pltpu.SemaphoreType.DMA((nb,))],
        args=(dproj, x, dx1, g1pre, w_in_g, prev_dx, prev_dg), ex=ex,
        aliases={5: 0} if first > 0 else None)


def _tn_matmul(a, g, nblk, a_cols, g_cols, a_blocked, g_blocked, tt, name, ex=None):
    t = a.shape[0]

    def body(a_ref, g_ref, o_ref):
        @pl.when(pl.program_id(1) == 0)
        def _():
            o_ref[...] = jnp.zeros_like(o_ref)
        o_ref[0] += lax.dot_general(a_ref[...], g_ref[...], TN_DIMS, preferred_element_type=F32)

    (out,), xouts = _call(
        body, name=name, grid=(nblk, t // tt),
        in_specs=[pl.BlockSpec((tt, a_cols), (lambda b, s: (s, b)) if a_blocked else (lambda b, s: (s, 0))),
                  pl.BlockSpec((tt, g_cols), (lambda b, s: (s, b)) if g_blocked else (lambda b, s: (s, 0)))],
        out_specs=[pl.BlockSpec((1, a_cols, g_cols), lambda b, s: (b, 0, 0))],
        out_shape=[jax.ShapeDtypeStruct((nblk, a_cols, g_cols), F32)],
        scratch_shapes=[], args=(a, g), ex=ex)
    return out, xouts


def _tn_matmuls(pairs, tt, name):
    t, d = pairs[0][0].shape
    k = len(pairs)

    def body(*refs):
        ins, outs = refs[:2 * k], refs[2 * k:]

        @pl.when(pl.program_id(0) == 0)
        def _():
            for o_ref in outs:
                o_ref[...] = jnp.zeros_like(o_ref)
        for j in range(k):
            outs[j][...] += lax.dot_general(ins[2 * j][...], ins[2 * j + 1][...], TN_DIMS, preferred_element_type=F32)

    outs, _ = _call(
        body, name=name, grid=(t // tt,), in_specs=[_rows(tt, d)] * (2 * k), out_specs=[_const((d, d))] * k,
        out_shape=[jax.ShapeDtypeStruct((d, d), F32)] * k, scratch_shapes=[],
        args=[m for pair in pairs for m in pair])
    return outs


def _pair_sum_call(g_full, from_sibling, core, name):
    nblk, r, c = g_full.shape
    hr = r // 2
    tr = min(hr, 256)
    n = hr // tr

    def body(core_ref, g_ref, p_ref, o_ref):
        o_ref[...] = (g_ref[...] + p_ref[...]).astype(BF16)

    return pl.pallas_call(
        body, name=name,
        grid_spec=pltpu.PrefetchScalarGridSpec(
            num_scalar_prefetch=1, grid=(nblk, n),
            in_specs=[pl.BlockSpec((1, tr, c), lambda j, i, cr: (j, cr[0] * n + i, 0)),
                      pl.BlockSpec((1, tr, c), lambda j, i, cr: (j, i, 0))],
            out_specs=pl.BlockSpec((1, tr, c), lambda j, i, cr: (j, i, 0))),
        out_shape=jax.ShapeDtypeStruct((nblk, hr, c), BF16),
        compiler_params=_params("parallel", "parallel"))(core, g_full, from_sibling)


def _chip_sum_call(pair, received, chip_core, name):
    _, hr, c = pair.shape
    tr = min(hr, 256)
    n = hr // tr

    def body(cc_ref, own_ref, r_ref, o_ref):
        o_ref[...] = ((own_ref[0].astype(F32) + r_ref[0].astype(F32)) + r_ref[1].astype(F32)) + r_ref[2].astype(F32)

    return pl.pallas_call(
        body, name=name,
        grid_spec=pltpu.PrefetchScalarGridSpec(
            num_scalar_prefetch=1, grid=(n,),
            in_specs=[pl.BlockSpec((1, tr, c), lambda i, cc: (cc[0], i, 0)),
                      pl.BlockSpec((N_CHIPS - 1, tr, c), lambda i, cc: (0, i, 0))],
            out_specs=pl.BlockSpec((tr, c), lambda i, cc: (cc[1] * n + i, 0))),
        out_shape=jax.ShapeDtypeStruct((2 * hr, c), F32),
        compiler_params=_params("parallel"))(chip_core, pair, received)


def _adamw(w, g, m, v):
    m = ADAM_B1 * m + (1.0 - ADAM_B1) * g
    v = ADAM_B2 * v + (1.0 - ADAM_B2) * (g * g)
    m_hat = m / (1.0 - ADAM_B1 ** ADAM_STEP)
    v_hat = v / (1.0 - ADAM_B2 ** ADAM_STEP)
    delta = -ADAM_LR * (m_hat / (jnp.sqrt(v_hat) + ADAM_EPS) + ADAM_WD * w)
    return delta, m, v


def _adam_call(w, g, m, v, name):
    r, c = w.shape
    tr = min(r, 256)

    def body(w_ref, g_ref, m_ref, v_ref, go_ref, d_ref, mo_ref, vo_ref):
        go_ref[...] = g_ref[...]
        d_ref[...], mo_ref[...], vo_ref[...] = _adamw(w_ref[...], g_ref[...], m_ref[...], v_ref[...])

    shape = jax.ShapeDtypeStruct((r, c), F32)
    return pl.pallas_call(
        body, name=name, grid=(r // tr,), in_specs=[_rows(tr, c)] * 4, out_specs=[_rows(tr, c)] * 4,
        out_shape=[shape] * 4, compiler_params=_params("parallel"))(w, g, m, v)


def _adam_sc_call(w, g, m, v, name):
    r, c = w.shape
    rows_tile = r // SC_TILES
    rr = min(rows_tile, SC_ROWS)

    def body(w_hbm, g_hbm, m_hbm, v_hbm, go_hbm, d_hbm, mo_hbm, vo_hbm, wb, gb, mb, vb):
        tile = lax.axis_index("sc_tile") * 2 + lax.axis_index("sc_core")

        @pl.loop(0, rows_tile, step=rr)
        def _(p):
            rows = pl.ds(tile * rows_tile + p, rr)
            pltpu.sync_copy(w_hbm.at[rows, :], wb)
            pltpu.sync_copy(g_hbm.at[rows, :], gb)
            pltpu.sync_copy(m_hbm.at[rows, :], mb)
            pltpu.sync_copy(v_hbm.at[rows, :], vb)

            @pl.loop(0, rr)
            def _(i):
                @pl.loop(0, c, step=SC_LANES)
                def _(j):
                    at = (i, pl.ds(j, SC_LANES))
                    delta, m2, v2 = _adamw(wb[at], gb[at], mb[at], vb[at])
                    wb[at] = delta
                    mb[at] = m2
                    vb[at] = v2

            pltpu.sync_copy(gb, go_hbm.at[rows, :])
            pltpu.sync_copy(wb, d_hbm.at[rows, :])
            pltpu.sync_copy(mb, mo_hbm.at[rows, :])
            pltpu.sync_copy(vb, vo_hbm.at[rows, :])

    shape = jax.ShapeDtypeStruct((r, c), F32)
    return pl.kernel(
        body, name=name, out_type=[shape] * 4,
        mesh=plsc.VectorSubcoreMesh(core_axis_name="sc_core", subcore_axis_name="sc_tile"),
        scratch_types=[pltpu.VMEM((rr, c), F32)] * 4)(w, g, m, v)


def _gather_first_call(buf, others, conv_a_w, conv_b_w, d):
    ka, dq = conv_a_w.shape
    kb = conv_b_w.shape[0]
    ra = -(-ka // SUBLANES) * SUBLANES
    rb = -(-kb // SUBLANES) * SUBLANES
    a_pad = jnp.pad(conv_a_w, ((0, ra - ka), (0, 0)))
    b_pad = jnp.pad(conv_b_w, ((0, rb - kb), (0, 0)))
    hr = buf.shape[1] // 2
    qr = hr // 2
    n_o = len(others)
    stage_rows = max(o.shape[0] for o in others)
    stage_cols = others[0].shape[1]
    assert all(o.shape[1] == stage_cols for o in others)

    def body(w_in, a_ref, b_ref, *rest):
        others_in, w_out, oa_ref, ob_ref = rest[:n_o], rest[n_o], rest[n_o + 1], rest[n_o + 2]
        others_out = rest[n_o + 3:2 * n_o + 3]
        pack, slots, stage_f, stage_b, send, recv, csend, crecv, osem = rest[2 * n_o + 3:]
        x, y, c = _place()
        me, x_chip, y_chip, d_chip = 2 * x + y, 2 * (1 - x) + y, 2 * x + (1 - y), 2 * (1 - x) + (1 - y)
        x_nbr, y_nbr, sibling = (1 - x, y, c), (x, 1 - y, c), (x, y, 1 - c)
        mine, theirs = pl.ds(c * hr, hr), pl.ds((1 - c) * hr, hr)
        first, second = pl.ds(c * hr, qr), pl.ds(c * hr + qr, qr)

        def rows_of(chip, rows):
            return w_out.at[chip, rows, :]

        def copy(k, src, dst, peer):
            return _remote(src, dst, send.at[k], recv.at[k], peer)

        def landed(k, dst):
            copy(k, dst, dst, sibling).wait_recv()

        pack[pl.ds(0, ra), :] = a_ref[...]
        pack[pl.ds(ra, rb), :] = b_ref[...]
        chips = _other_chips(x, y)
        conv = [_remote(pack, slots.at[me], csend.at[j], crecv.at[j], (px, py, c)) for j, (px, py, _) in enumerate(chips)]
        started = [copy(0, w_in.at[me, mine, :], rows_of(me, mine), x_nbr),
                   copy(1, w_in.at[me, mine, :], rows_of(me, mine), y_nbr)]
        for cp in conv + started:
            cp.start()

        def go(cp):
            cp.start()
            started.append(cp)

        for src, dst in zip(others_in, others_out):
            r = src.shape[0]
            f_view, b_view = stage_f.at[pl.ds(0, r), :], stage_b.at[pl.ds(0, r), :]
            fetch = pltpu.make_async_copy(src, f_view, osem.at[0])
            fetch.start()
            fetch.wait()
            b_view[...] = f_view[...].astype(BF16)
            place = pltpu.make_async_copy(b_view, dst.at[me], osem.at[1])
            place.start()
            place.wait()

        landed(0, rows_of(x_chip, mine))
        go(copy(2, rows_of(x_chip, first), rows_of(x_chip, first), y_nbr))
        go(copy(4, rows_of(x_chip, mine), rows_of(x_chip, mine), sibling))
        landed(1, rows_of(y_chip, mine))
        go(copy(3, rows_of(y_chip, second), rows_of(y_chip, second), x_nbr))
        go(copy(5, rows_of(y_chip, mine), rows_of(y_chip, mine), sibling))
        landed(2, rows_of(d_chip, first))
        landed(3, rows_of(d_chip, second))
        go(copy(6, rows_of(d_chip, mine), rows_of(d_chip, mine), sibling))
        for k, chip in ((4, x_chip), (5, y_chip), (6, d_chip)):
            landed(k, rows_of(chip, theirs))
        for cp in started:
            cp.wait_send()

        for j, (px, py, pk) in enumerate(chips):
            _remote(pack, slots.at[pk], csend.at[j], crecv.at[j], (px, py, c)).wait_recv()
        for cp in conv:
            cp.wait_send()
        slots[me] = pack[...]
        for k in range(N_CHIPS):
            oa_ref[:, pl.ds(k * dq, dq)] = slots[k, pl.ds(0, ra), :]
            ob_ref[:, pl.ds(k * dq, dq)] = slots[k, pl.ds(ra, rb), :]

    n_w = 7
    res = pl.pallas_call(
        body, name="gather_first", in_specs=[ANY, VMEM_FULL, VMEM_FULL] + [ANY] * n_o,
        out_specs=[ANY, VMEM_FULL, VMEM_FULL] + [ANY] * n_o,
        out_shape=[_sds(buf), jax.ShapeDtypeStruct((ra, d), F32), jax.ShapeDtypeStruct((rb, d), F32)]
        + [jax.ShapeDtypeStruct((N_CHIPS,) + o.shape, BF16) for o in others],
        scratch_shapes=[pltpu.VMEM((ra + rb, dq), F32), pltpu.VMEM((N_CHIPS, ra + rb, dq), F32),
                        pltpu.VMEM((stage_rows, stage_cols), F32), pltpu.VMEM((stage_rows, stage_cols), BF16),
                        pltpu.SemaphoreType.DMA((n_w,)), pltpu.SemaphoreType.DMA((n_w,)),
                        pltpu.SemaphoreType.DMA((N_CHIPS - 1,)), pltpu.SemaphoreType.DMA((N_CHIPS - 1,)),
                        pltpu.SemaphoreType.DMA((2,))],
        input_output_aliases={0: 0},
        compiler_params=pltpu.CompilerParams(has_side_effects=True, vmem_limit_bytes=VMEM_LIMIT))(
            buf, a_pad, b_pad, *others)
    return res[0], res[1][:ka], res[2][:kb], list(res[3:])


def _small_step_call(partials, loss_rows, weights, m_s, v_s, sharded, d, ex):
    n = len(partials)
    n_xi, n_xo = len(ex.inputs), len(ex.out_shapes)
    row_counts = [p.shape[0] for p in partials]
    starts = [sum(row_counts[:i]) for i in range(n)]
    loss_row = sum(row_counts)
    pack_rows = -(-(loss_row + 1) // SUBLANES) * SUBLANES
    dq = d // N_CHIPS

    def body(*refs):
        p_refs = refs[:n]
        loss_in = refs[n]
        w_refs = refs[n + 1:2 * n + 1]
        m_refs = refs[2 * n + 1:3 * n + 1]
        v_refs = refs[3 * n + 1:4 * n + 1]
        xin = refs[4 * n + 1:4 * n + 1 + n_xi]
        o = 4 * n + 1 + n_xi
        g_out = refs[o:o + n]
        d_out = refs[o + n:o + 2 * n]
        m_out = refs[o + 2 * n:o + 3 * n]
        v_out = refs[o + 3 * n:o + 4 * n]
        loss_out = refs[o + 4 * n]
        xout = refs[o + 4 * n + 1:o + 4 * n + 1 + n_xo]
        pack, from_sibling, slots, send_sem, recv_sem, xsend, xrecv = refs[o + 4 * n + 1 + n_xo:]
        x, y, c = _place()
        me = 2 * x + y
        riding = ex.copies(xin, xout, lambda i: xsend.at[i], lambda i: xrecv.at[i])
        for cp in riding:
            cp.start()

        pack[...] = jnp.zeros_like(pack)
        for i in range(n):
            pack[pl.ds(starts[i], row_counts[i]), :] = p_refs[i][...]
        pack[pl.ds(loss_row, 1), :] = loss_in[...]

        pair = _remote(pack, from_sibling, send_sem.at[0], recv_sem.at[0], (x, y, 1 - c))
        pair.start()
        pair.wait()
        pack[...] = pack[...] + from_sibling[...]
        chips = _other_chips(x, y)
        copies = [_remote(pack, slots.at[me], send_sem.at[1 + j], recv_sem.at[1 + j], (px, py, c))
                  for j, (px, py, _) in enumerate(chips)]
        for cp in copies:
            cp.start()
        for j, (px, py, pk) in enumerate(chips):
            _remote(pack, slots.at[pk], send_sem.at[1 + j], recv_sem.at[1 + j], (px, py, c)).wait_recv()
        for cp in copies:
            cp.wait_send()

        slots[me] = pack[...]
        total = slots[0]
        for k in range(1, N_CHIPS):
            total = total + slots[k]
        pack[...] = total

        loss_out[...] = jnp.broadcast_to(
            (0.5 / d) * jnp.sum(pack[pl.ds(loss_row, 1), :], axis=-1, keepdims=True), loss_out.shape)
        chip = 2 * x + y
        for i in range(n):
            rows = pl.ds(starts[i], row_counts[i])
            if sharded[i]:
                for k in range(N_CHIPS):
                    @pl.when(chip == k)
                    def _():
                        g_out[i][...] = pack[rows, pl.ds(k * dq, dq)]
            else:
                g_out[i][...] = pack[rows, :]
            d_out[i][...], m_out[i][...], v_out[i][...] = _adamw(
                w_refs[i][...], g_out[i][...], m_refs[i][...], v_refs[i][...])
        for cp in riding:
            cp.wait()

    w_shapes = [jax.ShapeDtypeStruct(w.shape, F32) for w in weights]
    n_in, n_out = 4 * n + 1, 4 * n + 1
    res = pl.pallas_call(
        body, name="small_grads_allreduce_adamw",
        in_specs=[VMEM_FULL] * n_in + [ANY] * n_xi, out_specs=[VMEM_FULL] * n_out + [ANY] * n_xo,
        out_shape=w_shapes * 4 + [jax.ShapeDtypeStruct((SUBLANES, 128), F32)] + ex.out_shapes,
        scratch_shapes=[pltpu.VMEM((pack_rows, d), F32), pltpu.VMEM((pack_rows, d), F32),
                        pltpu.VMEM((N_CHIPS, pack_rows, d), F32),
                        pltpu.SemaphoreType.DMA((N_CHIPS,)), pltpu.SemaphoreType.DMA((N_CHIPS,)),
                        pltpu.SemaphoreType.DMA((ex.n_sems,)), pltpu.SemaphoreType.DMA((ex.n_sems,))],
        input_output_aliases={n_in + i: n_out + o for i, o in ex.aliases.items()},
        compiler_params=pltpu.CompilerParams(has_side_effects=True, vmem_limit_bytes=VMEM_LIMIT))(
            *partials, loss_rows, *weights, *m_s, *v_s, *ex.inputs)
    return list(res[:n_out]), list(res[n_out:])


def _tile(t, want):
    return min(t, want)


def kernel(x, norm1_pre_g, w_in, b_in, conv_a_w, conv_a_b, w_a_out, conv_b_w, conv_b_b, ln_b_g, ln_b_b, w_b_out, w_o, norm1_post_g, norm2_pre_g, w_mlp_in, w_mlp_out, norm2_post_g, loss_target, m_norm1_pre_g, m_w_in, m_b_in, m_conv_a_w, m_conv_a_b, m_w_a_out, m_conv_b_w, m_conv_b_b, m_ln_b_g, m_ln_b_b, m_w_b_out, m_w_o, m_norm1_post_g, m_norm2_pre_g, m_w_mlp_in, m_w_mlp_out, m_norm2_post_g, v_norm1_pre_g, v_w_in, v_b_in, v_conv_a_w, v_conv_a_b, v_w_a_out, v_conv_b_w, v_conv_b_b, v_ln_b_g, v_ln_b_b, v_w_b_out, v_w_o, v_norm1_post_g, v_norm2_pre_g, v_w_mlp_in, v_w_mlp_out, v_norm2_post_g):
    _, t, d = x.shape
    xt = x.reshape(t, d)
    tgt = loss_target.reshape(t, d)
    row = lambda vec: vec.reshape(1, -1)
    cx, cy, cc = _place()
    core = cc.astype(jnp.int32).reshape(1)
    chip = (2 * cx + cy).astype(jnp.int32).reshape(1)

    big = dict(w_in=w_in, w_a_out=w_a_out, w_b_out=w_b_out, w_o=w_o, w_mlp_in=w_mlp_in, w_mlp_out=w_mlp_out)
    names = list(big)
    chip_core = jnp.concatenate([chip, core])
    mixer_w, mlp_w = ["w_a_out", "w_b_out", "w_o"], ["w_mlp_in", "w_mlp_out"]
    rows_of = lambda buf: buf.reshape(-1, buf.shape[-1])

    def pair_sums(keys, full, from_sibling):
        return [_pair_sum_call(g, p, core, "pair_sum_" + k) for k, g, p in zip(keys, full, from_sibling)]

    def chip_sums(keys, pairs, received):
        return [_chip_sum_call(p, r, chip_core, "chip_sum_" + k) for k, p, r in zip(keys, pairs, received)]

    w_in_g, conv_a_full, conv_b_full, slots = _gather_first_call(
        _cast_to_slot(w_in, chip, "cast_w_in"), [big[k] for k in mixer_w + mlp_w], conv_a_w, conv_b_w, d)

    g1pre, g1post, g2pre, g2post = row(norm1_pre_g), row(norm1_post_g), row(norm2_pre_g), row(norm2_post_g)
    lng, lnb, ba, bb = row(ln_b_g), row(ln_b_b), row(conv_a_b), row(conv_b_b)

    tm_proj = _tile(t, 512)
    n_proj = t // tm_proj
    if n_proj >= 3:
        (h, ua, ub, bg, cg, ha, a, sg, sa, sb), landed = _proj_call(
            xt, g1pre, w_in_g, row(b_in), tm_proj, ex=_ex_gather_ici(slots),
            then=(min(n_proj - 2, (7 * n_proj) // 8), _ex_gather_forward(slots[:3])))
        w_a_g, w_b_g, w_o_g = landed[:3]
    else:
        (h, ua, ub, bg, cg, ha, a, sg, sa, sb), landed = _proj_call(
            xt, g1pre, w_in_g, row(b_in), tm_proj, ex=_ex_gather_ici(slots))
        w_a_g, w_b_g, w_o_g = _exchange_call("forward_mixer_weights", [_ex_gather_forward(landed[:3])])
    w_a_full, w_b_full, w_o_full = rows_of(w_a_g), rows_of(w_b_g), rows_of(w_o_g)
    (x1, va, pa, cb, sbo, ya, yb, mg, mix), (w1_g, w2_g) = _mixer_fwd_call(
        ua, ub, bg, sa, sb, xt, conv_a_full, ba, conv_b_full, bb, lng, lnb,
        w_a_full, w_b_full, w_o_full, g1post, _tile(t, 256), ex=_ex_gather_forward(landed[3:]))
    (dx1, f, df2, h2, df1, dmix, dg2post, dg2pre, dg1post, loss_rows), _ = _mlp_call(
        x1, tgt, mix, g2pre, g2post, g1post, w1_g, rows_of(w2_g), _tile(t, 256))

    tt = _tile(t, 2048)
    n4, fq, dq = w_in.shape[1], w_mlp_in.shape[1], d // N_CHIPS
    g_mlp = [_tn_matmul(h2, df1, N_CHIPS, d, fq, False, True, tt, "dw_mlp_in")[0],
             _tn_matmul(f, df2, N_CHIPS, fq, d, True, False, tt, "dw_mlp_out")[0]]
    (dya, dyb, dva, dcb, dbg, dza, dzb, dlng, dlnb, dba, dbb, sbg, sza, szb), sib_mlp = _mixer_bwd_call(
        dmix, sa, sb, ya, yb, bg, va, cb, lng, lnb, w_a_full, w_b_full, w_o_full, _tile(t, 512),
        ex=_ex_sibling_halves(g_mlp))
    p_mlp = pair_sums(mlp_w, g_mlp, sib_mlp)
    g_mix = [g.reshape(N_CHIPS, dq, d)
             for g in _tn_matmuls([(pa, dya), (sbo, dyb), (mg, dmix)], _tile(t, 1024), "dw_mixer")]
    ex_a, ex_b = _ex_scatter_to_owner(p_mlp), _ex_sibling_halves(g_mix)
    (dproj, dwa_conv, dwb_conv, dbin), xo = _conv_bwd_call(
        dva, dcb, ua, ub, cg, ha, a, sg, dbg, dza, dzb, (sbg, sza, szb), conv_a_full, conv_b_full, _tile(t, 256),
        ex=_merge(ex_a, ex_b))
    recv_mlp, sib_mix = _split(xo, ex_a, ex_b)
    r_mlp = chip_sums(mlp_w, p_mlp, recv_mlp)
    p_mix = pair_sums(mixer_w, g_mix, sib_mix)
    ex_a, ex_b = _ex_share_halves(r_mlp), _ex_scatter_to_owner(p_mix)
    g_in, xo = _tn_matmul(h, dproj, N_CHIPS, d, n4, False, True, tt, "dw_in", ex=_merge(ex_a, ex_b))
    red_mlp, recv_mix = _split(xo, ex_a, ex_b)
    r_mix = chip_sums(mixer_w, p_mix, recv_mix)
    ex_a, ex_b = _ex_sibling_halves([g_in]), _ex_share_halves(r_mix)
    tm_dx = _tile(t, 512)
    n_dx = t // tm_dx
    n_a = max(1, (3 * n_dx) // 8)
    dx_done, xo = _dx_call(dproj, xt, dx1, g1pre, w_in_g, tm_dx, 0, n_a, None, ex=_merge(ex_a, ex_b))
    sib_in, red_mix = _split(xo, ex_a, ex_b)
    p_in = pair_sums(["w_in"], [g_in], sib_in)
    (grad_x, dg1pre), recv_in = _dx_call(dproj, xt, dx1, g1pre, w_in_g, tm_dx, n_a, n_dx - n_a, dx_done,
                                         ex=_ex_scatter_to_owner(p_in))
    r_in = chip_sums(["w_in"], p_in, recv_in)
    out = {}

    small = [
        ("conv_b_w", dwb_conv, conv_b_w, m_conv_b_w, v_conv_b_w, True),
        ("conv_b_b", dbb, bb, row(m_conv_b_b), row(v_conv_b_b), False),
        ("b_in", dbin.reshape(7, d), b_in.reshape(7, d), m_b_in.reshape(7, d), v_b_in.reshape(7, d), False),
        ("norm1_pre_g", dg1pre, row(norm1_pre_g), row(m_norm1_pre_g), row(v_norm1_pre_g), False),
        ("conv_a_w", dwa_conv, conv_a_w, m_conv_a_w, v_conv_a_w, True),
        ("conv_a_b", dba, ba, row(m_conv_a_b), row(v_conv_a_b), False),
        ("ln_b_g", dlng, lng, row(m_ln_b_g), row(v_ln_b_g), False),
        ("ln_b_b", dlnb, lnb, row(m_ln_b_b), row(v_ln_b_b), False),
        ("norm1_post_g", dg1post, g1post, row(m_norm1_post_g), row(v_norm1_post_g), False),
        ("norm2_pre_g", dg2pre, g2pre, row(m_norm2_pre_g), row(v_norm2_pre_g), False),
        ("norm2_post_g", dg2post, g2post, row(m_norm2_post_g), row(v_norm2_post_g), False),
    ]
    res, red_in = _small_step_call([s[1] for s in small], loss_rows, [s[2] for s in small], [s[3] for s in small],
                                   [s[4] for s in small], [s[5] for s in small], d, _ex_share_halves(r_in))
    reduced = dict(zip(mlp_w + mixer_w + ["w_in"], red_mlp + red_mix + red_in))
    moments = dict(w_in=(m_w_in, v_w_in), w_a_out=(m_w_a_out, v_w_a_out), w_b_out=(m_w_b_out, v_w_b_out),
                   w_o=(m_w_o, v_w_o), w_mlp_in=(m_w_mlp_in, v_w_mlp_in), w_mlp_out=(m_w_mlp_out, v_w_mlp_out))
    for k in names:
        adam = _adam_sc_call if k != "w_in" and big[k].shape[0] % (SC_TILES * SUBLANES) == 0 else _adam_call
        out[k] = tuple(adam(big[k], reduced[k], *moments[k], "adamw_" + k))
    ns = len(small)
    loss = res[4 * ns][0, 0]
    shapes = dict(norm1_pre_g=norm1_pre_g.shape, b_in=b_in.shape, conv_a_w=conv_a_w.shape,
                  conv_a_b=conv_a_b.shape, conv_b_w=conv_b_w.shape, conv_b_b=conv_b_b.shape,
                  ln_b_g=ln_b_g.shape, ln_b_b=ln_b_b.shape, norm1_post_g=norm1_post_g.shape,
                  norm2_pre_g=norm2_pre_g.shape, norm2_post_g=norm2_post_g.shape)
    for i, s in enumerate(small):
        out[s[0]] = tuple(res[q * ns + i].reshape(shapes[s[0]]) for q in range(4))

    order = ["norm1_pre_g", "w_in", "b_in", "conv_a_w", "conv_a_b", "w_a_out", "conv_b_w", "conv_b_b",
             "ln_b_g", "ln_b_b", "w_b_out", "w_o", "norm1_post_g", "norm2_pre_g", "w_mlp_in", "w_mlp_out",
             "norm2_post_g"]
    return (loss, grad_x.reshape(x.shape), *[out[k][0] for k in order], *[out[k][1] for k in order],
            *[out[k][2] for k in order], *[out[k][3] for k in order])
```

```python
import functools

import jax
import jax.numpy as jnp
from jax import lax
from jax.experimental import pallas as pl
from jax.experimental.pallas import tpu as pltpu
from jax.experimental.pallas import tpu_sc as plsc

RMS_EPS = 1e-6
LN_EPS = 1e-5
ADAM_LR = 0.001
ADAM_B1 = 0.9
ADAM_B2 = 0.999
ADAM_EPS = 1e-08
ADAM_WD = 0.01
ADAM_STEP = 10

F32 = jnp.float32
BF16 = jnp.bfloat16
MESH = pl.DeviceIdType.MESH
ANY = pl.BlockSpec(memory_space=pl.ANY)
VMEM_FULL = pl.BlockSpec(memory_space=pltpu.VMEM)

V7X_VMEM_BYTES = 64 * 1024 * 1024
VMEM_LIMIT = V7X_VMEM_BYTES - 8 * 1024 * 1024
SUBLANES = 8
N_CHIPS = 4
N_DEV = 8
SC_TILES = 32
SC_LANES = 16
SC_ROWS = 16
HALO_A = 8
HALO_B = 16
CONV_ROWS = 16
ROW_CHUNK = 32

NT_DIMS = (((1,), (1,)), ((), ()))
TN_DIMS = (((0,), (0,)), ((), ()))


def _params(*sem):
    return pltpu.CompilerParams(dimension_semantics=sem, vmem_limit_bytes=VMEM_LIMIT)


def _rows(tm, d):
    return pl.BlockSpec((tm, d), lambda i: (i, 0))


def _const(shape):
    return pl.BlockSpec(shape, lambda i: (0,) * len(shape))


def _halo_prev(tm, hb, d):
    return pl.BlockSpec((hb, d), lambda i: (jnp.maximum(i * (tm // hb) - 1, 0), 0))


def _halo_next(tm, hb, d, t):
    return pl.BlockSpec((hb, d), lambda i: (jnp.minimum((i + 1) * (tm // hb), t // hb - 1), 0))


def _for_chunks(n_rows, rc, fn):
    for r0 in range(0, n_rows, rc):
        fn(pl.ds(r0, rc))


def _fold8(v):
    return v.reshape(v.shape[0] // SUBLANES, SUBLANES, v.shape[1]).sum(axis=0)


def _mean_lanes(v):
    return jnp.mean(v, axis=-1, keepdims=True)


def _load_blocks_once(w_hbm, w_vmem, sem):
    nb, _, n = w_hbm.shape

    @pl.when(pl.program_id(0) == 0)
    def _():
        copies = [pltpu.make_async_copy(w_hbm.at[j], w_vmem.at[:, pl.ds(j * n, n)], sem.at[j])
                  for j in range(nb)]
        for cp in copies:
            cp.start()
        for cp in copies:
            cp.wait()


def _load_once(w_hbm, w_vmem, sem):
    @pl.when(pl.program_id(0) == 0)
    def _():
        cp = pltpu.make_async_copy(w_hbm, w_vmem, sem)
        cp.start()
        cp.wait()


def _write_row_sums(acc_ref, out_ref, n_steps):
    @pl.when(pl.program_id(0) == n_steps - 1)
    def _():
        out_ref[...] = jnp.sum(acc_ref[...], axis=0, keepdims=True)


def _place():
    return lax.axis_index("x"), lax.axis_index("y"), lax.axis_index("c")


def _other_chips(x, y):
    rel = [(x, 1 - y), (1 - x, y), (1 - x, 1 - y)]
    return [(px, py, 2 * px + py) for px, py in rel]


class _Exchange:
    def __init__(self, inputs, out_shapes, aliases, n_sems, copies):
        self.inputs = list(inputs)
        self.out_shapes = list(out_shapes)
        self.aliases = dict(aliases)
        self.n_sems = n_sems
        self.copies = copies


def _remote(src, dst, send, recv, device):
    return pltpu.make_async_remote_copy(src_ref=src, dst_ref=dst, send_sem=send, recv_sem=recv,
                                        device_id=device, device_id_type=MESH)


def _sds(a):
    return jax.ShapeDtypeStruct(a.shape, a.dtype)


def _ex_gather_ici(bufs):
    n = len(bufs)

    def copies(xin, xout, send, recv):
        x, y, c = _place()
        me = 2 * x + y
        out = []
        for a in range(n):
            hr = xin[a].shape[1] // 2
            rows = pl.ds(c * hr, hr)
            for j, (px, py, _) in enumerate(_other_chips(x, y)):
                k = a * (N_CHIPS - 1) + j
                out.append(_remote(xin[a].at[me, rows, :], xout[a].at[me, rows, :], send(k), recv(k), (px, py, c)))
        return out

    return _Exchange(bufs, [_sds(b) for b in bufs], {a: a for a in range(n)}, n * (N_CHIPS - 1), copies)


def _ex_gather_forward(bufs):
    n = len(bufs)

    def copies(xin, xout, send, recv):
        x, y, c = _place()
        out = []
        for a in range(n):
            hr = xin[a].shape[1] // 2
            rows = pl.ds(c * hr, hr)
            for j, (_, _, pk) in enumerate(_other_chips(x, y)):
                k = a * (N_CHIPS - 1) + j
                out.append(_remote(xin[a].at[pk, rows, :], xout[a].at[pk, rows, :], send(k), recv(k), (x, y, 1 - c)))
        return out

    return _Exchange(bufs, [_sds(b) for b in bufs], {a: a for a in range(n)}, n * (N_CHIPS - 1), copies)


def _ex_sibling_halves(grads):
    n = len(grads)

    def copies(xin, xout, send, recv):
        x, y, c = _place()
        out = []
        for a in range(n):
            hr = xin[a].shape[1] // 2
            out.append(_remote(xin[a].at[:, pl.ds((1 - c) * hr, hr), :], xout[a], send(a), recv(a), (x, y, 1 - c)))
        return out

    shapes = [jax.ShapeDtypeStruct((g.shape[0], g.shape[1] // 2, g.shape[2]), g.dtype) for g in grads]
    return _Exchange(grads, shapes, {}, n, copies)


def _ex_scatter_to_owner(pairs):
    n = len(pairs)

    def copies(xin, xout, send, recv):
        x, y, c = _place()
        out = []
        for a in range(n):
            for j, (px, py, pk) in enumerate(_other_chips(x, y)):
                k = a * (N_CHIPS - 1) + j
                out.append(_remote(xin[a].at[pk], xout[a].at[j], send(k), recv(k), (px, py, c)))
        return out

    shapes = [jax.ShapeDtypeStruct((N_CHIPS - 1,) + p.shape[1:], p.dtype) for p in pairs]
    return _Exchange(pairs, shapes, {}, n * (N_CHIPS - 1), copies)


def _ex_share_halves(reduced):
    n = len(reduced)

    def copies(xin, xout, send, recv):
        x, y, c = _place()
        out = []
        for a in range(n):
            hr = xin[a].shape[0] // 2
            rows = pl.ds(c * hr, hr)
            out.append(_remote(xin[a].at[rows, :], xout[a].at[rows, :], send(a), recv(a), (x, y, 1 - c)))
        return out

    return _Exchange(reduced, [_sds(r) for r in reduced], {a: a for a in range(n)}, n, copies)


def _merge(*exs):
    exs = [e for e in exs if e is not None]
    if not exs:
        return None
    inputs, shapes, aliases = [], [], {}
    in_off, out_off, sem_off = [], [], []
    n_sems = 0
    for e in exs:
        in_off.append(len(inputs))
        out_off.append(len(shapes))
        sem_off.append(n_sems)
        aliases.update({len(inputs) + i: len(shapes) + o for i, o in e.aliases.items()})
        inputs += e.inputs
        shapes += e.out_shapes
        n_sems += e.n_sems

    def copies(xin, xout, send, recv):
        out = []
        for e, io, oo, so in zip(exs, in_off, out_off, sem_off):
            out += e.copies(xin[io:io + len(e.inputs)], xout[oo:oo + len(e.out_shapes)],
                            lambda i, so=so: send(so + i), lambda i, so=so: recv(so + i))
        return out

    return _Exchange(inputs, shapes, aliases, n_sems, copies)


def _split(ex_outs, *exs):
    parts, o = [], 0
    for e in exs:
        parts.append(list(ex_outs[o:o + len(e.out_shapes)]))
        o += len(e.out_shapes)
    return parts


def _call(body, *, name, grid, in_specs, out_specs, out_shape, scratch_shapes, args, ex=None, aliases=None,
          then=None):
    n_in, n_out, n_scr = len(in_specs), len(out_specs), len(scratch_shapes)
    seq = ("arbitrary",) * len(grid)
    aliases = dict(aliases or {})
    if ex is None:
        outs = pl.pallas_call(
            body, name=name, grid=grid, in_specs=list(in_specs), out_specs=list(out_specs),
            out_shape=list(out_shape), scratch_shapes=list(scratch_shapes), input_output_aliases=aliases,
            compiler_params=_params(*seq))(*args)
        return list(outs), []
    n_xi, n_xo = len(ex.inputs), len(ex.out_shapes)

    def full(*refs):
        ins, xin = refs[:n_in], refs[n_in:n_in + n_xi]
        o = n_in + n_xi
        outs, xout = refs[o:o + n_out], refs[o + n_out:o + n_out + n_xo]
        s = o + n_out + n_xo
        scr = refs[s:s + n_scr]
        send_sems, recv_sems = refs[s + n_scr], refs[s + n_scr + 1]
        send = lambda i: send_sems.at[i]
        recv = lambda i: recv_sems.at[i]
        first = functools.reduce(jnp.logical_and, [pl.program_id(a) == 0 for a in range(len(grid))])
        last = functools.reduce(jnp.logical_and, [pl.program_id(a) == grid[a] - 1 for a in range(len(grid))])

        @pl.when(first)
        def _():
            for cp in ex.copies(xin, xout, send, recv):
                cp.start()

        body(*ins, *outs, *scr)

        if then is None:
            @pl.when(last)
            def _():
                for cp in ex.copies(xin, xout, send, recv):
                    cp.wait()
        else:
            step, ex2 = then
            send2_sems, recv2_sems = refs[s + n_scr + 2], refs[s + n_scr + 3]
            send2 = lambda i: send2_sems.at[i]
            recv2 = lambda i: recv2_sems.at[i]

            @pl.when(pl.program_id(0) == step)
            def _():
                for cp in ex.copies(xin, xout, send, recv):
                    cp.wait()
                for cp in ex2.copies(xin, xout, send2, recv2):
                    cp.start()

            @pl.when(last)
            def _():
                for cp in ex2.copies(xin, xout, send2, recv2):
                    cp.wait()

    sems = [pltpu.SemaphoreType.DMA((ex.n_sems,)), pltpu.SemaphoreType.DMA((ex.n_sems,))]
    if then is not None:
        assert len(grid) == 1 and 0 < then[0] < grid[0] - 1
        sems += [pltpu.SemaphoreType.DMA((then[1].n_sems,)), pltpu.SemaphoreType.DMA((then[1].n_sems,))]
    res = pl.pallas_call(
        full, name=name, grid=grid, in_specs=list(in_specs) + [ANY] * n_xi,
        out_specs=list(out_specs) + [ANY] * n_xo, out_shape=list(out_shape) + ex.out_shapes,
        scratch_shapes=list(scratch_shapes) + sems,
        input_output_aliases={**aliases, **{n_in + i: n_out + o for i, o in ex.aliases.items()}},
        compiler_params=pltpu.CompilerParams(dimension_semantics=seq, vmem_limit_bytes=VMEM_LIMIT,
                                             has_side_effects=True))(*args, *ex.inputs)
    return list(res[:n_out]), list(res[n_out:])


def _exchange_call(name, phases):
    first = phases[0]
    n_xi, n_xo = len(first.inputs), len(first.out_shapes)

    def body(*refs):
        xin, xout = refs[:n_xi], refs[n_xi:n_xi + n_xo]
        sems = refs[n_xi + n_xo:]
        for p, ex in enumerate(phases):
            send_sems, recv_sems = sems[2 * p], sems[2 * p + 1]
            cps = ex.copies(xin, xout, lambda i: send_sems.at[i], lambda i: recv_sems.at[i])
            for cp in cps:
                cp.start()
            for cp in cps:
                cp.wait()

    sems = []
    for ex in phases:
        sems += [pltpu.SemaphoreType.DMA((ex.n_sems,)), pltpu.SemaphoreType.DMA((ex.n_sems,))]
    return list(pl.pallas_call(
        body, name=name, in_specs=[ANY] * n_xi, out_specs=[ANY] * n_xo, out_shape=first.out_shapes,
        scratch_shapes=sems, input_output_aliases=dict(first.aliases),
        compiler_params=pltpu.CompilerParams(has_side_effects=True))(*first.inputs))


def _cast_to_slot(w, chip, name):
    r, c = w.shape
    tr = min(r, 256)

    def body(chip_ref, w_ref, o_ref):
        o_ref[0] = w_ref[...].astype(BF16)

    return pl.pallas_call(
        body, name=name,
        grid_spec=pltpu.PrefetchScalarGridSpec(
            num_scalar_prefetch=1, grid=(r // tr,),
            in_specs=[pl.BlockSpec((tr, c), lambda i, k: (i, 0))],
            out_specs=pl.BlockSpec((1, tr, c), lambda i, k: (k[0], i, 0))),
        out_shape=jax.ShapeDtypeStruct((N_CHIPS, r, c), BF16),
        compiler_params=_params("parallel"))(chip, w)


def _proj_call(x, g1pre, w_in_g, b_in, tm, ex=None, then=None):
    t, d = x.shape
    nb, _, n4 = w_in_g.shape
    ni = nb * n4
    assert ni == 7 * d

    def body(x_ref, g_ref, b_ref, w_hbm, h_ref, ua_ref, ub_ref, bg_ref, cg_ref, ha_ref, a_ref,
             sg_ref, sa_ref, sb_ref, w_v, p0, p1, sem):
        _load_blocks_once(w_hbm, w_v, sem)

        def norm(rows):
            xv = x_ref[rows, :]
            r = lax.rsqrt(_mean_lanes(xv * xv) + RMS_EPS)
            h_ref[rows, :] = (xv * r * g_ref[...]).astype(BF16)
        _for_chunks(tm, ROW_CHUNK, norm)

        def group(i, dst):
            cols = pl.ds(i * d, d)
            dst[...] = jnp.dot(h_ref[...], w_v[:, cols], preferred_element_type=F32) + b_ref[:, cols]

        group(0, p0)

        def bgate(rows):
            bg_ref[rows, :] = p0[rows, :].astype(BF16)
        _for_chunks(tm, ROW_CHUNK, bgate)

        group(1, p0)
        group(2, p1)

        def branch_a(rows):
            cg, ha = p0[rows, :], p1[rows, :]
            ua_ref[rows, :] = cg * ha
            cg_ref[rows, :] = cg.astype(BF16)
            ha_ref[rows, :] = ha.astype(BF16)
        _for_chunks(tm, ROW_CHUNK, branch_a)

        group(3, p0)
        group(4, p1)

        def branch_b(rows):
            a, sg = p0[rows, :], jax.nn.sigmoid(p1[rows, :])
            ub_ref[rows, :] = a * sg
            a_ref[rows, :] = a.astype(BF16)
            sg_ref[rows, :] = sg.astype(BF16)
        _for_chunks(tm, ROW_CHUNK, branch_b)

        group(5, p0)
        group(6, p1)

        def gates(rows):
            sa_ref[rows, :] = jax.nn.sigmoid(p0[rows, :]).astype(BF16)
            sb_ref[rows, :] = jax.nn.sigmoid(p1[rows, :]).astype(BF16)
        _for_chunks(tm, ROW_CHUNK, gates)

    bf = jax.ShapeDtypeStruct((t, d), BF16)
    f32 = jax.ShapeDtypeStruct((t, d), F32)
    return _call(
        body, name="proj_fwd", grid=(t // tm,),
        in_specs=[_rows(tm, d), _const((1, d)), _const((1, ni)), ANY],
        out_specs=[_rows(tm, d)] * 10,
        out_shape=[bf, f32, f32, bf, bf, bf, bf, bf, bf, bf],
        scratch_shapes=[pltpu.VMEM((d, ni), BF16), pltpu.VMEM((tm, d), F32), pltpu.VMEM((tm, d), F32),
                        pltpu.SemaphoreType.DMA((nb,))],
        args=(x, g1pre, b_in, w_in_g), ex=ex, then=then)


def _fill_ext(ext, prev_ref, cur_ref, next_ref, hb, tm, i, n_steps):
    ext[pl.ds(0, hb), :] = jnp.where(i > 0, prev_ref[...], 0.0)
    ext[pl.ds(hb, tm), :] = cur_ref[...]
    ext[pl.ds(hb + tm, hb), :] = jnp.where(i < n_steps - 1, next_ref[...], 0.0)


def _shift_plan(offsets):
    shifts = sorted({o % SUBLANES for o in offsets if o % SUBLANES})
    return {s: i for i, s in enumerate(shifts)}


def _shifted_rows(tm, offsets):
    return tm + SUBLANES * max(o // SUBLANES for o in offsets)


def _fill_shifted(ext, sh, plan):
    n = sh.shape[1]
    for s, i in plan.items():
        sh[i, :, :] = ext[pl.ds(s, n), :]


def _fill_tap_rows(w_ref, rows8):
    @pl.when(pl.program_id(0) == 0)
    def _():
        for k in range(w_ref.shape[0]):
            rows8[pl.ds(k * SUBLANES, SUBLANES), :] = jnp.broadcast_to(w_ref[k:k + 1, :], (SUBLANES, w_ref.shape[1]))


def _tap(rows8, k):
    w8 = rows8[pl.ds(k * SUBLANES, SUBLANES), :]
    return jnp.concatenate([w8] * (CONV_ROWS // SUBLANES), axis=0)


def _window(ext, sh, plan, offset, r0):
    q, s = divmod(offset, SUBLANES)
    if s == 0:
        return ext[pl.ds(offset + r0, CONV_ROWS), :]
    return sh[plan[s], pl.ds(SUBLANES * q + r0, CONV_ROWS), :]


def _mixer_fwd_call(ua, ub, bg, sa, sb, x, conv_a_w, conv_a_b, conv_b_w, conv_b_b, ln_g, ln_b,
                    w_a, w_b, w_o, g1post, tm, ex=None):
    t, d = x.shape
    n_steps = t // tm
    ka, kb = conv_a_w.shape[0], conv_b_w.shape[0]
    off_a = [HALO_A - (ka - 1) // 2 + k for k in range(ka)]
    off_b = [HALO_B - (kb - 1) // 2 + k for k in range(kb)]
    plan_a, plan_b = _shift_plan(off_a), _shift_plan(off_b)

    def body(uap, uac, uan, ubp, ubc, ubn, bg_ref, sa_ref, sb_ref, x_ref, wa_c, ba_c, wb_c, bb_c,
             lng, lnb, wa_hbm, wb_hbm, wo_hbm, g_ref,
             x1_ref, va_ref, pa_ref, cb_ref, sbo_ref, ya_ref, yb_ref, mg_ref, mix_ref,
             ext_a, ext_b, sh_a, sh_b, wa8, wb8, wa_v, wb_v, wo_v, y0, y1, sem):
        i = pl.program_id(0)
        _fill_tap_rows(wa_c, wa8)
        _fill_tap_rows(wb_c, wb8)
        _load_once(wa_hbm, wa_v, sem.at[0])
        _load_once(wb_hbm, wb_v, sem.at[1])
        _load_once(wo_hbm, wo_v, sem.at[2])
        _fill_ext(ext_a, uap, uac, uan, HALO_A, tm, i, n_steps)
        _fill_ext(ext_b, ubp, ubc, ubn, HALO_B, tm, i, n_steps)
        _fill_shifted(ext_a, sh_a, plan_a)
        _fill_shifted(ext_b, sh_b, plan_b)

        for r0 in range(0, tm, CONV_ROWS):
            rows = pl.ds(r0, CONV_ROWS)
            va = jnp.broadcast_to(ba_c[...], (CONV_ROWS, d))
            for k in range(ka):
                va = va + _tap(wa8, k) * _window(ext_a, sh_a, plan_a, off_a[k], r0)
            va_ref[rows, :] = va.astype(BF16)
            pa_ref[rows, :] = (bg_ref[rows, :].astype(F32) * va).astype(BF16)
            cb = jnp.broadcast_to(bb_c[...], (CONV_ROWS, d))
            for k in range(kb):
                cb = cb + _tap(wb8, k) * _window(ext_b, sh_b, plan_b, off_b[k], r0)
            cb_ref[rows, :] = cb
            mu = _mean_lanes(cb)
            cen = cb - mu
            rstd = lax.rsqrt(_mean_lanes(cen * cen) + LN_EPS)
            ln = cen * rstd * lng[...] + lnb[...]
            sbo_ref[rows, :] = (ln * jax.nn.sigmoid(ln)).astype(BF16)

        y0[...] = jnp.dot(pa_ref[...], wa_v[...], preferred_element_type=F32)
        y1[...] = jnp.dot(sbo_ref[...], wb_v[...], preferred_element_type=F32)

        def merge(rows):
            ya, yb = y0[rows, :], y1[rows, :]
            ya_ref[rows, :] = ya.astype(BF16)
            yb_ref[rows, :] = yb.astype(BF16)
            mg_ref[rows, :] = (sa_ref[rows, :].astype(F32) * ya + sb_ref[rows, :].astype(F32) * yb).astype(BF16)
        _for_chunks(tm, ROW_CHUNK, merge)

        mix_ref[...] = jnp.dot(mg_ref[...], wo_v[...], preferred_element_type=F32)

        def resid(rows):
            mix = mix_ref[rows, :]
            r = lax.rsqrt(_mean_lanes(mix * mix) + RMS_EPS)
            x1_ref[rows, :] = x_ref[rows, :] + mix * r * g_ref[...]
        _for_chunks(tm, ROW_CHUNK, resid)

    bf = jax.ShapeDtypeStruct((t, d), BF16)
    f32 = jax.ShapeDtypeStruct((t, d), F32)
    return _call(
        body, name="mixer_fwd", grid=(n_steps,),
        in_specs=[_halo_prev(tm, HALO_A, d), _rows(tm, d), _halo_next(tm, HALO_A, d, t),
                  _halo_prev(tm, HALO_B, d), _rows(tm, d), _halo_next(tm, HALO_B, d, t),
                  _rows(tm, d), _rows(tm, d), _rows(tm, d), _rows(tm, d),
                  _const((ka, d)), _const((1, d)), _const((kb, d)), _const((1, d)),
                  _const((1, d)), _const((1, d)), ANY, ANY, ANY, _const((1, d))],
        out_specs=[_rows(tm, d)] * 9,
        out_shape=[f32, bf, bf, f32, bf, bf, bf, bf, f32],
        scratch_shapes=[pltpu.VMEM((tm + 2 * HALO_A, d), F32), pltpu.VMEM((tm + 2 * HALO_B, d), F32),
                        pltpu.VMEM((len(plan_a), _shifted_rows(tm, off_a), d), F32),
                        pltpu.VMEM((len(plan_b), _shifted_rows(tm, off_b), d), F32),
                        pltpu.VMEM((ka * SUBLANES, d), F32), pltpu.VMEM((kb * SUBLANES, d), F32),
                        pltpu.VMEM((d, d), BF16), pltpu.VMEM((d, d), BF16), pltpu.VMEM((d, d), BF16),
                        pltpu.VMEM((tm, d), F32), pltpu.VMEM((tm, d), F32),
                        pltpu.SemaphoreType.DMA((3,))],
        args=(ua, ua, ua, ub, ub, ub, bg, sa, sb, x, conv_a_w, conv_a_b, conv_b_w, conv_b_b,
              ln_g, ln_b, w_a, w_b, w_o, g1post), ex=ex)


def _mlp_call(x1, target, mix, g2pre, g2post, g1post, w1_g, w2, tm, ex=None):
    t, d = x1.shape
    nb, _, fq = w1_g.shape
    f = nb * fq
    n_steps = t // tm
    inv_d = 1.0 / d

    def body(x1_ref, t_ref, mix_ref, gpre, gpost, gmix, w1_hbm, w2_hbm,
             dx1_ref, f_ref, df2_ref, h2_ref, df1_ref, dmix_ref, dgpost_ref, dgpre_ref, dgmix_ref, loss_ref,
             w1_v, w2_v, f1_s, blk_s, f2_s, acc_post, acc_pre, acc_mix, acc_loss, sem):
        _load_blocks_once(w1_hbm, w1_v, sem)
        _load_once(w2_hbm, w2_v, sem.at[nb])

        @pl.when(pl.program_id(0) == 0)
        def _():
            acc_post[...] = jnp.zeros_like(acc_post)
            acc_pre[...] = jnp.zeros_like(acc_pre)
            acc_mix[...] = jnp.zeros_like(acc_mix)
            acc_loss[...] = jnp.zeros_like(acc_loss)

        def norm(rows):
            xv = x1_ref[rows, :]
            r = lax.rsqrt(_mean_lanes(xv * xv) + RMS_EPS)
            h2_ref[rows, :] = (xv * r * gpre[...]).astype(BF16)
        _for_chunks(tm, ROW_CHUNK, norm)

        for j in range(nb):
            cols = pl.ds(j * fq, fq)
            f1_s[:, cols] = jnp.dot(h2_ref[...], w1_v[:, cols], preferred_element_type=F32)

        def act(rows):
            relu = jnp.maximum(f1_s[rows, :], 0.0)
            f_ref[rows, :] = (relu * relu).astype(BF16)
        _for_chunks(tm, ROW_CHUNK, act)

        f2_s[...] = jnp.dot(f_ref[...], w2_v[...], preferred_element_type=F32)

        def head(rows):
            f2 = f2_s[rows, :]
            rf = lax.rsqrt(_mean_lanes(f2 * f2) + RMS_EPS)
            y = x1_ref[rows, :] + f2 * rf * gpost[...]
            err = y - t_ref[rows, :]
            acc_loss[...] += _fold8(err * err)
            dy = err * inv_d
            gdy = dy * gpost[...]
            df2 = rf * gdy - f2 * (rf * rf * rf * _mean_lanes(gdy * f2))
            df2_ref[rows, :] = df2.astype(BF16)
            acc_post[...] += _fold8(dy * f2 * rf)
            dx1_ref[rows, :] = dy
        _for_chunks(tm, ROW_CHUNK, head)

        for j in range(nb):
            cols = pl.ds(j * fq, fq)
            blk_s[...] = lax.dot_general(df2_ref[...], w2_v[cols, :], NT_DIMS, preferred_element_type=F32)

            def dact(rows):
                relu = jnp.maximum(f1_s[rows, cols], 0.0)
                df1_ref[rows, cols] = (blk_s[rows, :] * (2.0 * relu)).astype(BF16)
            _for_chunks(tm, ROW_CHUNK, dact)

        f2_s[...] = lax.dot_general(df1_ref[...], w1_v[...], NT_DIMS, preferred_element_type=F32)

        def dnorm(rows):
            dh2 = f2_s[rows, :]
            xv = x1_ref[rows, :]
            r = lax.rsqrt(_mean_lanes(xv * xv) + RMS_EPS)
            gd = dh2 * gpre[...]
            dxv = dx1_ref[rows, :] + r * gd - xv * (r * r * r * _mean_lanes(gd * xv))
            dx1_ref[rows, :] = dxv
            acc_pre[...] += _fold8(dh2 * xv * r)
            mix = mix_ref[rows, :]
            rm = lax.rsqrt(_mean_lanes(mix * mix) + RMS_EPS)
            gm = dxv * gmix[...]
            dmix_ref[rows, :] = (rm * gm - mix * (rm * rm * rm * _mean_lanes(gm * mix))).astype(BF16)
            acc_mix[...] += _fold8(dxv * mix * rm)
        _for_chunks(tm, ROW_CHUNK, dnorm)

        _write_row_sums(acc_post, dgpost_ref, n_steps)
        _write_row_sums(acc_pre, dgpre_ref, n_steps)
        _write_row_sums(acc_mix, dgmix_ref, n_steps)
        _write_row_sums(acc_loss, loss_ref, n_steps)

    row = jax.ShapeDtypeStruct((1, d), F32)
    wide = pl.BlockSpec((tm, f), lambda i: (i, 0), pipeline_mode=pl.Buffered(1))
    once = pl.BlockSpec((tm, d), lambda i: (i, 0), pipeline_mode=pl.Buffered(1))
    return _call(
        body, name="mlp_fwd_bwd", grid=(n_steps,),
        in_specs=[_rows(tm, d), once, once, _const((1, d)), _const((1, d)), _const((1, d)), ANY, ANY],
        out_specs=[_rows(tm, d), wide, _rows(tm, d), _rows(tm, d), wide, _rows(tm, d),
                   _const((1, d)), _const((1, d)), _const((1, d)), _const((1, d))],
        out_shape=[jax.ShapeDtypeStruct((t, d), F32), jax.ShapeDtypeStruct((t, f), BF16),
                   jax.ShapeDtypeStruct((t, d), BF16), jax.ShapeDtypeStruct((t, d), BF16),
                   jax.ShapeDtypeStruct((t, f), BF16), jax.ShapeDtypeStruct((t, d), BF16), row, row, row, row],
        scratch_shapes=[pltpu.VMEM((d, f), BF16), pltpu.VMEM((f, d), BF16),
                        pltpu.VMEM((tm, f), F32), pltpu.VMEM((tm, fq), F32), pltpu.VMEM((tm, d), F32),
                        pltpu.VMEM((SUBLANES, d), F32), pltpu.VMEM((SUBLANES, d), F32),
                        pltpu.VMEM((SUBLANES, d), F32), pltpu.VMEM((SUBLANES, d), F32),
                        pltpu.SemaphoreType.DMA((nb + 1,))],
        args=(x1, target, mix, g2pre, g2post, g1post, w1_g, w2), ex=ex)


def _mixer_bwd_call(dmix, sa, sb, ya, yb, bg, va, cb, ln_g, ln_b, w_a, w_b, w_o, tm, ex=None):
    t, d = dmix.shape
    n_steps = t // tm

    def body(dmix_ref, sa_ref, sb_ref, ya_ref, yb_ref, bg_ref, va_ref, cb_ref, lng, lnb,
             wa_hbm, wb_hbm, wo_hbm,
             dya_ref, dyb_ref, dva_ref, dcb_ref, dbg_ref, dza_ref, dzb_ref,
             dlng_ref, dlnb_ref, dba_ref, dbb_ref, sbg_ref, sza_ref, szb_ref,
             wa_v, wb_v, wo_v, s0, s1, acc_lng, acc_lnb, acc_ba, acc_bb, acc_bg, acc_za, acc_zb, sem):
        _load_once(wa_hbm, wa_v, sem.at[0])
        _load_once(wb_hbm, wb_v, sem.at[1])
        _load_once(wo_hbm, wo_v, sem.at[2])
        accs = (acc_lng, acc_lnb, acc_ba, acc_bb, acc_bg, acc_za, acc_zb)

        @pl.when(pl.program_id(0) == 0)
        def _():
            for acc in accs:
                acc[...] = jnp.zeros_like(acc)

        s0[...] = lax.dot_general(dmix_ref[...], wo_v[...], NT_DIMS, preferred_element_type=F32)

        def dmerge(rows):
            dm = s0[rows, :]
            sav, sbv = sa_ref[rows, :].astype(F32), sb_ref[rows, :].astype(F32)
            dya_ref[rows, :] = (dm * sav).astype(BF16)
            dyb_ref[rows, :] = (dm * sbv).astype(BF16)
            dza = dm * ya_ref[rows, :].astype(F32) * sav * (1.0 - sav)
            dzb = dm * yb_ref[rows, :].astype(F32) * sbv * (1.0 - sbv)
            dza_ref[rows, :] = dza.astype(BF16)
            dzb_ref[rows, :] = dzb.astype(BF16)
            acc_za[...] += _fold8(dza)
            acc_zb[...] += _fold8(dzb)
        _for_chunks(tm, ROW_CHUNK, dmerge)

        s0[...] = lax.dot_general(dya_ref[...], wa_v[...], NT_DIMS, preferred_element_type=F32)
        s1[...] = lax.dot_general(dyb_ref[...], wb_v[...], NT_DIMS, preferred_element_type=F32)

        def dbranches(rows):
            dpa = s0[rows, :]
            dbg = dpa * va_ref[rows, :].astype(F32)
            dbg_ref[rows, :] = dbg.astype(BF16)
            acc_bg[...] += _fold8(dbg)
            dva = dpa * bg_ref[rows, :].astype(F32)
            dva_ref[rows, :] = dva
            acc_ba[...] += _fold8(dva)
            cbv = cb_ref[rows, :]
            mu = _mean_lanes(cbv)
            cen = cbv - mu
            rstd = lax.rsqrt(_mean_lanes(cen * cen) + LN_EPS)
            xhat = cen * rstd
            ln = xhat * lng[...] + lnb[...]
            sig = jax.nn.sigmoid(ln)
            dln = s1[rows, :] * (sig * (1.0 + ln * (1.0 - sig)))
            acc_lng[...] += _fold8(dln * xhat)
            acc_lnb[...] += _fold8(dln)
            dxh = dln * lng[...]
            dcb = rstd * (dxh - _mean_lanes(dxh) - xhat * _mean_lanes(dxh * xhat))
            dcb_ref[rows, :] = dcb
            acc_bb[...] += _fold8(dcb)
        _for_chunks(tm, ROW_CHUNK, dbranches)

        _write_row_sums(acc_lng, dlng_ref, n_steps)
        _write_row_sums(acc_lnb, dlnb_ref, n_steps)
        _write_row_sums(acc_ba, dba_ref, n_steps)
        _write_row_sums(acc_bb, dbb_ref, n_steps)
        _write_row_sums(acc_bg, sbg_ref, n_steps)
        _write_row_sums(acc_za, sza_ref, n_steps)
        _write_row_sums(acc_zb, szb_ref, n_steps)

    bf = jax.ShapeDtypeStruct((t, d), BF16)
    f32 = jax.ShapeDtypeStruct((t, d), F32)
    row = jax.ShapeDtypeStruct((1, d), F32)
    return _call(
        body, name="mixer_bwd", grid=(n_steps,),
        in_specs=[_rows(tm, d)] * 8 + [_const((1, d))] * 2 + [ANY, ANY, ANY],
        out_specs=[_rows(tm, d)] * 7 + [_const((1, d))] * 7,
        out_shape=[bf, bf, f32, f32, bf, bf, bf] + [row] * 7,
        scratch_shapes=[pltpu.VMEM((d, d), BF16), pltpu.VMEM((d, d), BF16), pltpu.VMEM((d, d), BF16),
                        pltpu.VMEM((tm, d), F32), pltpu.VMEM((tm, d), F32)]
        + [pltpu.VMEM((SUBLANES, d), F32)] * 7 + [pltpu.SemaphoreType.DMA((3,))],
        args=(dmix, sa, sb, ya, yb, bg, va, cb, ln_g, ln_b, w_a, w_b, w_o), ex=ex)


def _conv_bwd_call(dva, dcb, ua, ub, cg, ha, a, sg, dbg, dza, dzb, through_sums, conv_a_w, conv_b_w, tm, ex=None):
    t, d = dva.shape
    n_steps = t // tm
    ka, kb = conv_a_w.shape[0], conv_b_w.shape[0]
    off_a = [HALO_A + (ka - 1) // 2 - k for k in range(ka)]
    off_b = [HALO_B + (kb - 1) // 2 - k for k in range(kb)]
    plan_a, plan_b = _shift_plan(off_a), _shift_plan(off_b)

    def body(dvap, dvac, dvan, dcbp, dcbc, dcbn, ua_ref, ub_ref,
             cg_ref, ha_ref, a_ref, sg_ref, dbg_ref, dza_ref, dzb_ref, wa_c, wb_c, sbg_ref, sza_ref, szb_ref,
             dproj_ref, dwa_ref, dwb_ref, dbin_ref,
             e_dva, e_dcb, sh_a, sh_b, wa8, wb8, acc_wa, acc_wb, acc_bin):
        i = pl.program_id(0)
        _fill_tap_rows(wa_c, wa8)
        _fill_tap_rows(wb_c, wb8)

        @pl.when(i == 0)
        def _():
            acc_wa[...] = jnp.zeros_like(acc_wa)
            acc_wb[...] = jnp.zeros_like(acc_wb)
            acc_bin[...] = jnp.zeros_like(acc_bin)
            for col, s_ref in ((0, sbg_ref), (5, sza_ref), (6, szb_ref)):
                acc_bin[0:1, pl.ds(col * d, d)] = s_ref[...]

        _fill_ext(e_dva, dvap, dvac, dvan, HALO_A, tm, i, n_steps)
        _fill_ext(e_dcb, dcbp, dcbc, dcbn, HALO_B, tm, i, n_steps)
        _fill_shifted(e_dva, sh_a, plan_a)
        _fill_shifted(e_dcb, sh_b, plan_b)

        def put(col, rows, val_f32):
            dproj_ref[rows, pl.ds(col * d, d)] = val_f32.astype(BF16)
            acc_bin[:, pl.ds(col * d, d)] += _fold8(val_f32)

        for r0 in range(0, tm, CONV_ROWS):
            rows = pl.ds(r0, CONV_ROWS)
            ua_c, ub_c = ua_ref[rows, :], ub_ref[rows, :]
            dua = jnp.zeros((CONV_ROWS, d), F32)
            for k in range(ka):
                xk = _window(e_dva, sh_a, plan_a, off_a[k], r0)
                dua = dua + _tap(wa8, k) * xk
                acc_wa[pl.ds(k * SUBLANES, SUBLANES), :] += _fold8(ua_c * xk)
            dub = jnp.zeros((CONV_ROWS, d), F32)
            for k in range(kb):
                xk = _window(e_dcb, sh_b, plan_b, off_b[k], r0)
                dub = dub + _tap(wb8, k) * xk
                acc_wb[pl.ds(k * SUBLANES, SUBLANES), :] += _fold8(ub_c * xk)
            cgv, hav = cg_ref[rows, :].astype(F32), ha_ref[rows, :].astype(F32)
            av, sgv = a_ref[rows, :].astype(F32), sg_ref[rows, :].astype(F32)
            put(1, rows, dua * hav)
            put(2, rows, dua * cgv)
            put(3, rows, dub * sgv)
            put(4, rows, dub * av * sgv * (1.0 - sgv))
            for col, through in ((0, dbg_ref), (5, dza_ref), (6, dzb_ref)):
                dproj_ref[rows, pl.ds(col * d, d)] = through[rows, :]

        @pl.when(i == n_steps - 1)
        def _():
            for k in range(ka):
                dwa_ref[k:k + 1, :] = jnp.sum(acc_wa[pl.ds(k * SUBLANES, SUBLANES), :], axis=0, keepdims=True)
            for k in range(kb):
                dwb_ref[k:k + 1, :] = jnp.sum(acc_wb[pl.ds(k * SUBLANES, SUBLANES), :], axis=0, keepdims=True)
            dbin_ref[...] = jnp.sum(acc_bin[...], axis=0, keepdims=True)

    halo_a = [_halo_prev(tm, HALO_A, d), _rows(tm, d), _halo_next(tm, HALO_A, d, t)]
    halo_b = [_halo_prev(tm, HALO_B, d), _rows(tm, d), _halo_next(tm, HALO_B, d, t)]
    return _call(
        body, name="conv_bwd", grid=(n_steps,),
        in_specs=halo_a + halo_b + [_rows(tm, d)] * 9 + [_const((ka, d)), _const((kb, d))] + [_const((1, d))] * 3,
        out_specs=[_rows(tm, 7 * d), _const((ka, d)), _const((kb, d)), _const((1, 7 * d))],
        out_shape=[jax.ShapeDtypeStruct((t, 7 * d), BF16), jax.ShapeDtypeStruct((ka, d), F32),
                   jax.ShapeDtypeStruct((kb, d), F32), jax.ShapeDtypeStruct((1, 7 * d), F32)],
        scratch_shapes=[pltpu.VMEM((tm + 2 * HALO_A, d), F32), pltpu.VMEM((tm + 2 * HALO_B, d), F32),
                        pltpu.VMEM((len(plan_a), _shifted_rows(tm, off_a), d), F32),
                        pltpu.VMEM((len(plan_b), _shifted_rows(tm, off_b), d), F32),
                        pltpu.VMEM((ka * SUBLANES, d), F32), pltpu.VMEM((kb * SUBLANES, d), F32),
                        pltpu.VMEM((ka * SUBLANES, d), F32), pltpu.VMEM((kb * SUBLANES, d), F32),
                        pltpu.VMEM((SUBLANES, 7 * d), F32)],
        args=(dva, dva, dva, dcb, dcb, dcb, ua, ub, cg, ha, a, sg, dbg, dza, dzb,
              conv_a_w, conv_b_w, *through_sums), ex=ex)


def _dx_call(dproj, x, dx1, g1pre, w_in_g, tm, first, n_steps, prev, ex=None):
    t, d = x.shape
    nb, _, n4 = w_in_g.shape
    ni = nb * n4
    rows = lambda width: pl.BlockSpec((tm, width), lambda i: (i + first, 0))
    if prev is None:
        prev = (jnp.zeros((SUBLANES, 128), F32), jnp.zeros((1, d), F32))
    prev_dx, prev_dg = prev

    def body(dp_ref, x_ref, dx1_ref, g_ref, w_hbm, prev_dx_hbm, prev_dg_ref, dx_ref, dg_ref, w_v, dh_s, acc_g, sem):
        at_start = pl.program_id(0) == 0
        loads = [pltpu.make_async_copy(w_hbm.at[j], w_v.at[:, pl.ds(j * n4, n4)], sem.at[j]) for j in range(nb)]

        @pl.when(at_start)
        def _():
            for cp in loads:
                cp.start()
            acc_g[...] = jnp.zeros_like(acc_g)
            acc_g[0:1, :] = prev_dg_ref[...]

        for j in range(nb):
            @pl.when(at_start)
            def _():
                loads[j].wait()
            cols = pl.ds(j * n4, n4)
            part = lax.dot_general(dp_ref[:, cols], w_v[:, cols], NT_DIMS, preferred_element_type=F32)
            if j == 0:
                dh_s[...] = part
            else:
                dh_s[...] += part

        def dnorm(rows):
            dh = dh_s[rows, :]
            xv = x_ref[rows, :]
            r = lax.rsqrt(_mean_lanes(xv * xv) + RMS_EPS)
            gd = dh * g_ref[...]
            dx_ref[rows, :] = dx1_ref[rows, :] + r * gd - xv * (r * r * r * _mean_lanes(gd * xv))
            acc_g[...] += _fold8(dh * xv * r)
        _for_chunks(tm, ROW_CHUNK, dnorm)
        _write_row_sums(acc_g, dg_ref, n_steps)

    return _call(
        body, name="dx_bwd_from_%d" % first, grid=(n_steps,),
        in_specs=[rows(ni), rows(d), rows(d), _const((1, d)), ANY, ANY, _const((1, d))],
        out_specs=[rows(d), _const((1, d))],
        out_shape=[jax.ShapeDtypeStruct((t, d), F32), jax.ShapeDtypeStruct((1, d), F32)],
        scratch_shapes=[pltpu.VMEM((d, ni), BF16), pltpu.VMEM((tm, d), F32),
                        pltpu.VMEM((SUBLANES, d), F32), pltpu.SemaphoreType.DMA((nb,))],
        args=(dproj, x, dx1, g1pre, w_in_g, prev_dx, prev_dg), ex=ex,
        aliases={5: 0} if first > 0 else None)


def _tn_matmul(a, g, nblk, a_cols, g_cols, a_blocked, g_blocked, tt, name, ex=None):
    t = a.shape[0]

    def body(a_ref, g_ref, o_ref):
        @pl.when(pl.program_id(1) == 0)
        def _():
            o_ref[...] = jnp.zeros_like(o_ref)
        o_ref[0] += lax.dot_general(a_ref[...], g_ref[...], TN_DIMS, preferred_element_type=F32)

    (out,), xouts = _call(
        body, name=name, grid=(nblk, t // tt),
        in_specs=[pl.BlockSpec((tt, a_cols), (lambda b, s: (s, b)) if a_blocked else (lambda b, s: (s, 0))),
                  pl.BlockSpec((tt, g_cols), (lambda b, s: (s, b)) if g_blocked else (lambda b, s: (s, 0)))],
        out_specs=[pl.BlockSpec((1, a_cols, g_cols), lambda b, s: (b, 0, 0))],
        out_shape=[jax.ShapeDtypeStruct((nblk, a_cols, g_cols), F32)],
        scratch_shapes=[], args=(a, g), ex=ex)
    return out, xouts


def _tn_matmuls(pairs, tt, name):
    t, d = pairs[0][0].shape
    k = len(pairs)

    def body(*refs):
        ins, outs = refs[:2 * k], refs[2 * k:]

        @pl.when(pl.program_id(0) == 0)
        def _():
            for o_ref in outs:
                o_ref[...] = jnp.zeros_like(o_ref)
        for j in range(k):
            outs[j][...] += lax.dot_general(ins[2 * j][...], ins[2 * j + 1][...], TN_DIMS, preferred_element_type=F32)

    outs, _ = _call(
        body, name=name, grid=(t // tt,), in_specs=[_rows(tt, d)] * (2 * k), out_specs=[_const((d, d))] * k,
        out_shape=[jax.ShapeDtypeStruct((d, d), F32)] * k, scratch_shapes=[],
        args=[m for pair in pairs for m in pair])
    return outs


def _pair_sum_call(g_full, from_sibling, core, name):
    nblk, r, c = g_full.shape
    hr = r // 2
    tr = min(hr, 256)
    n = hr // tr

    def body(core_ref, g_ref, p_ref, o_ref):
        o_ref[...] = (g_ref[...] + p_ref[...]).astype(BF16)

    return pl.pallas_call(
        body, name=name,
        grid_spec=pltpu.PrefetchScalarGridSpec(
            num_scalar_prefetch=1, grid=(nblk, n),
            in_specs=[pl.BlockSpec((1, tr, c), lambda j, i, cr: (j, cr[0] * n + i, 0)),
                      pl.BlockSpec((1, tr, c), lambda j, i, cr: (j, i, 0))],
            out_specs=pl.BlockSpec((1, tr, c), lambda j, i, cr: (j, i, 0))),
        out_shape=jax.ShapeDtypeStruct((nblk, hr, c), BF16),
        compiler_params=_params("parallel", "parallel"))(core, g_full, from_sibling)


def _chip_sum_call(pair, received, chip_core, name):
    _, hr, c = pair.shape
    tr = min(hr, 256)
    n = hr // tr

    def body(cc_ref, own_ref, r_ref, o_ref):
        o_ref[...] = ((own_ref[0].astype(F32) + r_ref[0].astype(F32)) + r_ref[1].astype(F32)) + r_ref[2].astype(F32)

    return pl.pallas_call(
        body, name=name,
        grid_spec=pltpu.PrefetchScalarGridSpec(
            num_scalar_prefetch=1, grid=(n,),
            in_specs=[pl.BlockSpec((1, tr, c), lambda i, cc: (cc[0], i, 0)),
                      pl.BlockSpec((N_CHIPS - 1, tr, c), lambda i, cc: (0, i, 0))],
            out_specs=pl.BlockSpec((tr, c), lambda i, cc: (cc[1] * n + i, 0))),
        out_shape=jax.ShapeDtypeStruct((2 * hr, c), F32),
        compiler_params=_params("parallel"))(chip_core, pair, received)


def _adamw(w, g, m, v):
    m = ADAM_B1 * m + (1.0 - ADAM_B1) * g
    v = ADAM_B2 * v + (1.0 - ADAM_B2) * (g * g)
    m_hat = m / (1.0 - ADAM_B1 ** ADAM_STEP)
    v_hat = v / (1.0 - ADAM_B2 ** ADAM_STEP)
    delta = -ADAM_LR * (m_hat / (jnp.sqrt(v_hat) + ADAM_EPS) + ADAM_WD * w)
    return delta, m, v


def _adam_call(w, g, m, v, name):
    r, c = w.shape
    tr = min(r, 256)

    def body(w_ref, g_ref, m_ref, v_ref, go_ref, d_ref, mo_ref, vo_ref):
        go_ref[...] = g_ref[...]
        d_ref[...], mo_ref[...], vo_ref[...] = _adamw(w_ref[...], g_ref[...], m_ref[...], v_ref[...])

    shape = jax.ShapeDtypeStruct((r, c), F32)
    return pl.pallas_call(
        body, name=name, grid=(r // tr,), in_specs=[_rows(tr, c)] * 4, out_specs=[_rows(tr, c)] * 4,
        out_shape=[shape] * 4, compiler_params=_params("parallel"))(w, g, m, v)


def _adam_sc_call(w, g, m, v, name):
    r, c = w.shape
    rows_tile = r // SC_TILES
    rr = min(rows_tile, SC_ROWS)

    def body(w_hbm, g_hbm, m_hbm, v_hbm, go_hbm, d_hbm, mo_hbm, vo_hbm, wb, gb, mb, vb):
        tile = lax.axis_index("sc_tile") * 2 + lax.axis_index("sc_core")

        @pl.loop(0, rows_tile, step=rr)
        def _(p):
            rows = pl.ds(tile * rows_tile + p, rr)
            pltpu.sync_copy(w_hbm.at[rows, :], wb)
            pltpu.sync_copy(g_hbm.at[rows, :], gb)
            pltpu.sync_copy(m_hbm.at[rows, :], mb)
            pltpu.sync_copy(v_hbm.at[rows, :], vb)

            @pl.loop(0, rr)
            def _(i):
                @pl.loop(0, c, step=SC_LANES)
                def _(j):
                    at = (i, pl.ds(j, SC_LANES))
                    delta, m2, v2 = _adamw(wb[at], gb[at], mb[at], vb[at])
                    wb[at] = delta
                    mb[at] = m2
                    vb[at] = v2

            pltpu.sync_copy(gb, go_hbm.at[rows, :])
            pltpu.sync_copy(wb, d_hbm.at[rows, :])
            pltpu.sync_copy(mb, mo_hbm.at[rows, :])
            pltpu.sync_copy(vb, vo_hbm.at[rows, :])

    shape = jax.ShapeDtypeStruct((r, c), F32)
    return pl.kernel(
        body, name=name, out_type=[shape] * 4,
        mesh=plsc.VectorSubcoreMesh(core_axis_name="sc_core", subcore_axis_name="sc_tile"),
        scratch_types=[pltpu.VMEM((rr, c), F32)] * 4)(w, g, m, v)


def _gather_first_call(buf, others, conv_a_w, conv_b_w, d):
    ka, dq = conv_a_w.shape
    kb = conv_b_w.shape[0]
    ra = -(-ka // SUBLANES) * SUBLANES
    rb = -(-kb // SUBLANES) * SUBLANES
    a_pad = jnp.pad(conv_a_w, ((0, ra - ka), (0, 0)))
    b_pad = jnp.pad(conv_b_w, ((0, rb - kb), (0, 0)))
    hr = buf.shape[1] // 2
    qr = hr // 2
    n_o = len(others)
    stage_rows = max(o.shape[0] for o in others)
    stage_cols = others[0].shape[1]
    assert all(o.shape[1] == stage_cols for o in others)

    def body(w_in, a_ref, b_ref, *rest):
        others_in, w_out, oa_ref, ob_ref = rest[:n_o], rest[n_o], rest[n_o + 1], rest[n_o + 2]
        others_out = rest[n_o + 3:2 * n_o + 3]
        pack, slots, stage_f, stage_b, send, recv, csend, crecv, osem = rest[2 * n_o + 3:]
        x, y, c = _place()
        me, x_chip, y_chip, d_chip = 2 * x + y, 2 * (1 - x) + y, 2 * x + (1 - y), 2 * (1 - x) + (1 - y)
        x_nbr, y_nbr, sibling = (1 - x, y, c), (x, 1 - y, c), (x, y, 1 - c)
        mine, theirs = pl.ds(c * hr, hr), pl.ds((1 - c) * hr, hr)
        first, second = pl.ds(c * hr, qr), pl.ds(c * hr + qr, qr)

        def rows_of(chip, rows):
            return w_out.at[chip, rows, :]

        def copy(k, src, dst, peer):
            return _remote(src, dst, send.at[k], recv.at[k], peer)

        def landed(k, dst):
            copy(k, dst, dst, sibling).wait_recv()

        pack[pl.ds(0, ra), :] = a_ref[...]
        pack[pl.ds(ra, rb), :] = b_ref[...]
        chips = _other_chips(x, y)
        conv = [_remote(pack, slots.at[me], csend.at[j], crecv.at[j], (px, py, c)) for j, (px, py, _) in enumerate(chips)]
        started = [copy(0, w_in.at[me, mine, :], rows_of(me, mine), x_nbr),
                   copy(1, w_in.at[me, mine, :], rows_of(me, mine), y_nbr)]
        for cp in conv + started:
            cp.start()

        def go(cp):
            cp.start()
            started.append(cp)

        for src, dst in zip(others_in, others_out):
            r = src.shape[0]
            f_view, b_view = stage_f.at[pl.ds(0, r), :], stage_b.at[pl.ds(0, r), :]
            fetch = pltpu.make_async_copy(src, f_view, osem.at[0])
            fetch.start()
            fetch.wait()
            b_view[...] = f_view[...].astype(BF16)
            place = pltpu.make_async_copy(b_view, dst.at[me], osem.at[1])
            place.start()
            place.wait()

        landed(0, rows_of(x_chip, mine))
        go(copy(2, rows_of(x_chip, first), rows_of(x_chip, first), y_nbr))
        go(copy(4, rows_of(x_chip, mine), rows_of(x_chip, mine), sibling))
        landed(1, rows_of(y_chip, mine))
        go(copy(3, rows_of(y_chip, second), rows_of(y_chip, second), x_nbr))
        go(copy(5, rows_of(y_chip, mine), rows_of(y_chip, mine), sibling))
        landed(2, rows_of(d_chip, first))
        landed(3, rows_of(d_chip, second))
        go(copy(6, rows_of(d_chip, mine), rows_of(d_chip, mine), sibling))
        for k, chip in ((4, x_chip), (5, y_chip), (6, d_chip)):
            landed(k, rows_of(chip, theirs))
        for cp in started:
            cp.wait_send()

        for j, (px, py, pk) in enumerate(chips):
            _remote(pack, slots.at[pk], csend.at[j], crecv.at[j], (px, py, c)).wait_recv()
        for cp in conv:
            cp.wait_send()
        slots[me] = pack[...]
        for k in range(N_CHIPS):
            oa_ref[:, pl.ds(k * dq, dq)] = slots[k, pl.ds(0, ra), :]
            ob_ref[:, pl.ds(k * dq, dq)] = slots[k, pl.ds(ra, rb), :]

    n_w = 7
    res = pl.pallas_call(
        body, name="gather_first", in_specs=[ANY, VMEM_FULL, VMEM_FULL] + [ANY] * n_o,
        out_specs=[ANY, VMEM_FULL, VMEM_FULL] + [ANY] * n_o,
        out_shape=[_sds(buf), jax.ShapeDtypeStruct((ra, d), F32), jax.ShapeDtypeStruct((rb, d), F32)]
        + [jax.ShapeDtypeStruct((N_CHIPS,) + o.shape, BF16) for o in others],
        scratch_shapes=[pltpu.VMEM((ra + rb, dq), F32), pltpu.VMEM((N_CHIPS, ra + rb, dq), F32),
                        pltpu.VMEM((stage_rows, stage_cols), F32), pltpu.VMEM((stage_rows, stage_cols), BF16),
                        pltpu.SemaphoreType.DMA((n_w,)), pltpu.SemaphoreType.DMA((n_w,)),
                        pltpu.SemaphoreType.DMA((N_CHIPS - 1,)), pltpu.SemaphoreType.DMA((N_CHIPS - 1,)),
                        pltpu.SemaphoreType.DMA((2,))],
        input_output_aliases={0: 0},
        compiler_params=pltpu.CompilerParams(has_side_effects=True, vmem_limit_bytes=VMEM_LIMIT))(
            buf, a_pad, b_pad, *others)
    return res[0], res[1][:ka], res[2][:kb], list(res[3:])


def _small_step_call(partials, loss_rows, weights, m_s, v_s, sharded, d, ex):
    n = len(partials)
    n_xi, n_xo = len(ex.inputs), len(ex.out_shapes)
    row_counts = [p.shape[0] for p in partials]
    starts = [sum(row_counts[:i]) for i in range(n)]
    loss_row = sum(row_counts)
    pack_rows = -(-(loss_row + 1) // SUBLANES) * SUBLANES
    dq = d // N_CHIPS

    def body(*refs):
        p_refs = refs[:n]
        loss_in = refs[n]
        w_refs = refs[n + 1:2 * n + 1]
        m_refs = refs[2 * n + 1:3 * n + 1]
        v_refs = refs[3 * n + 1:4 * n + 1]
        xin = refs[4 * n + 1:4 * n + 1 + n_xi]
        o = 4 * n + 1 + n_xi
        g_out = refs[o:o + n]
        d_out = refs[o + n:o + 2 * n]
        m_out = refs[o + 2 * n:o + 3 * n]
        v_out = refs[o + 3 * n:o + 4 * n]
        loss_out = refs[o + 4 * n]
        xout = refs[o + 4 * n + 1:o + 4 * n + 1 + n_xo]
        pack, from_sibling, slots, send_sem, recv_sem, xsend, xrecv = refs[o + 4 * n + 1 + n_xo:]
        x, y, c = _place()
        me = 2 * x + y
        riding = ex.copies(xin, xout, lambda i: xsend.at[i], lambda i: xrecv.at[i])
        for cp in riding:
            cp.start()

        pack[...] = jnp.zeros_like(pack)
        for i in range(n):
            pack[pl.ds(starts[i], row_counts[i]), :] = p_refs[i][...]
        pack[pl.ds(loss_row, 1), :] = loss_in[...]

        pair = _remote(pack, from_sibling, send_sem.at[0], recv_sem.at[0], (x, y, 1 - c))
        pair.start()
        pair.wait()
        pack[...] = pack[...] + from_sibling[...]
        chips = _other_chips(x, y)
        copies = [_remote(pack, slots.at[me], send_sem.at[1 + j], recv_sem.at[1 + j], (px, py, c))
                  for j, (px, py, _) in enumerate(chips)]
        for cp in copies:
            cp.start()
        for j, (px, py, pk) in enumerate(chips):
            _remote(pack, slots.at[pk], send_sem.at[1 + j], recv_sem.at[1 + j], (px, py, c)).wait_recv()
        for cp in copies:
            cp.wait_send()

        slots[me] = pack[...]
        total = slots[0]
        for k in range(1, N_CHIPS):
            total = total + slots[k]
        pack[...] = total

        loss_out[...] = jnp.broadcast_to(
            (0.5 / d) * jnp.sum(pack[pl.ds(loss_row, 1), :], axis=-1, keepdims=True), loss_out.shape)
        chip = 2 * x + y
        for i in range(n):
            rows = pl.ds(starts[i], row_counts[i])
            if sharded[i]:
                for k in range(N_CHIPS):
                    @pl.when(chip == k)
                    def _():
                        g_out[i][...] = pack[rows, pl.ds(k * dq, dq)]
            else:
                g_out[i][...] = pack[rows, :]
            d_out[i][...], m_out[i][...], v_out[i][...] = _adamw(
                w_refs[i][...], g_out[i][...], m_refs[i][...], v_refs[i][...])
        for cp in riding:
            cp.wait()

    w_shapes = [jax.ShapeDtypeStruct(w.shape, F32) for w in weights]
    n_in, n_out = 4 * n + 1, 4 * n + 1
    res = pl.pallas_call(
        body, name="small_grads_allreduce_adamw",
        in_specs=[VMEM_FULL] * n_in + [ANY] * n_xi, out_specs=[VMEM_FULL] * n_out + [ANY] * n_xo,
        out_shape=w_shapes * 4 + [jax.ShapeDtypeStruct((SUBLANES, 128), F32)] + ex.out_shapes,
        scratch_shapes=[pltpu.VMEM((pack_rows, d), F32), pltpu.VMEM((pack_rows, d), F32),
                        pltpu.VMEM((N_CHIPS, pack_rows, d), F32),
                        pltpu.SemaphoreType.DMA((N_CHIPS,)), pltpu.SemaphoreType.DMA((N_CHIPS,)),
                        pltpu.SemaphoreType.DMA((ex.n_sems,)), pltpu.SemaphoreType.DMA((ex.n_sems,))],
        input_output_aliases={n_in + i: n_out + o for i, o in ex.aliases.items()},
        compiler_params=pltpu.CompilerParams(has_side_effects=True, vmem_limit_bytes=VMEM_LIMIT))(
            *partials, loss_rows, *weights, *m_s, *v_s, *ex.inputs)
    return list(res[:n_out]), list(res[n_out:])


def _tile(t, want):
    return min(t, want)


def kernel(x, norm1_pre_g, w_in, b_in, conv_a_w, conv_a_b, w_a_out, conv_b_w, conv_b_b, ln_b_g, ln_b_b, w_b_out, w_o, norm1_post_g, norm2_pre_g, w_mlp_in, w_mlp_out, norm2_post_g, loss_target, m_norm1_pre_g, m_w_in, m_b_in, m_conv_a_w, m_conv_a_b, m_w_a_out, m_conv_b_w, m_conv_b_b, m_ln_b_g, m_ln_b_b, m_w_b_out, m_w_o, m_norm1_post_g, m_norm2_pre_g, m_w_mlp_in, m_w_mlp_out, m_norm2_post_g, v_norm1_pre_g, v_w_in, v_b_in, v_conv_a_w, v_conv_a_b, v_w_a_out, v_conv_b_w, v_conv_b_b, v_ln_b_g, v_ln_b_b, v_w_b_out, v_w_o, v_norm1_post_g, v_norm2_pre_g, v_w_mlp_in, v_w_mlp_out, v_norm2_post_g):
    _, t, d = x.shape
    xt = x.reshape(t, d)
    tgt = loss_target.reshape(t, d)
    row = lambda vec: vec.reshape(1, -1)
    cx, cy, cc = _place()
    core = cc.astype(jnp.int32).reshape(1)
    chip = (2 * cx + cy).astype(jnp.int32).reshape(1)

    big = dict(w_in=w_in, w_a_out=w_a_out, w_b_out=w_b_out, w_o=w_o, w_mlp_in=w_mlp_in, w_mlp_out=w_mlp_out)
    names = list(big)
    chip_core = jnp.concatenate([chip, core])
    mixer_w, mlp_w = ["w_a_out", "w_b_out", "w_o"], ["w_mlp_in", "w_mlp_out"]
    rows_of = lambda buf: buf.reshape(-1, buf.shape[-1])

    def pair_sums(keys, full, from_sibling):
        return [_pair_sum_call(g, p, core, "pair_sum_" + k) for k, g, p in zip(keys, full, from_sibling)]

    def chip_sums(keys, pairs, received):
        return [_chip_sum_call(p, r, chip_core, "chip_sum_" + k) for k, p, r in zip(keys, pairs, received)]

    w_in_g, conv_a_full, conv_b_full, slots = _gather_first_call(
        _cast_to_slot(w_in, chip, "cast_w_in"), [big[k] for k in mixer_w + mlp_w], conv_a_w, conv_b_w, d)

    g1pre, g1post, g2pre, g2post = row(norm1_pre_g), row(norm1_post_g), row(norm2_pre_g), row(norm2_post_g)
    lng, lnb, ba, bb = row(ln_b_g), row(ln_b_b), row(conv_a_b), row(conv_b_b)

    tm_proj = _tile(t, 512)
    n_proj = t // tm_proj
    if n_proj >= 3:
        (h, ua, ub, bg, cg, ha, a, sg, sa, sb), landed = _proj_call(
            xt, g1pre, w_in_g, row(b_in), tm_proj, ex=_ex_gather_ici(slots),
            then=(min(n_proj - 2, (7 * n_proj) // 8), _ex_gather_forward(slots[:3])))
        w_a_g, w_b_g, w_o_g = landed[:3]
    else:
        (h, ua, ub, bg, cg, ha, a, sg, sa, sb), landed = _proj_call(
            xt, g1pre, w_in_g, row(b_in), tm_proj, ex=_ex_gather_ici(slots))
        w_a_g, w_b_g, w_o_g = _exchange_call("forward_mixer_weights", [_ex_gather_forward(landed[:3])])
    w_a_full, w_b_full, w_o_full = rows_of(w_a_g), rows_of(w_b_g), rows_of(w_o_g)
    (x1, va, pa, cb, sbo, ya, yb, mg, mix), (w1_g, w2_g) = _mixer_fwd_call(
        ua, ub, bg, sa, sb, xt, conv_a_full, ba, conv_b_full, bb, lng, lnb,
        w_a_full, w_b_full, w_o_full, g1post, _tile(t, 256), ex=_ex_gather_forward(landed[3:]))
    (dx1, f, df2, h2, df1, dmix, dg2post, dg2pre, dg1post, loss_rows), _ = _mlp_call(
        x1, tgt, mix, g2pre, g2post, g1post, w1_g, rows_of(w2_g), _tile(t, 512))

    tt = _tile(t, 2048)
    n4, fq, dq = w_in.shape[1], w_mlp_in.shape[1], d // N_CHIPS
    g_mlp = [_tn_matmul(h2, df1, N_CHIPS, d, fq, False, True, tt, "dw_mlp_in")[0],
             _tn_matmul(f, df2, N_CHIPS, fq, d, True, False, tt, "dw_mlp_out")[0]]
    (dya, dyb, dva, dcb, dbg, dza, dzb, dlng, dlnb, dba, dbb, sbg, sza, szb), sib_mlp = _mixer_bwd_call(
        dmix, sa, sb, ya, yb, bg, va, cb, lng, lnb, w_a_full, w_b_full, w_o_full, _tile(t, 512),
        ex=_ex_sibling_halves(g_mlp))
    p_mlp = pair_sums(mlp_w, g_mlp, sib_mlp)
    g_mix = [g.reshape(N_CHIPS, dq, d)
             for g in _tn_matmuls([(pa, dya), (sbo, dyb), (mg, dmix)], _tile(t, 1024), "dw_mixer")]
    ex_a, ex_b = _ex_scatter_to_owner(p_mlp), _ex_sibling_halves(g_mix)
    (dproj, dwa_conv, dwb_conv, dbin), xo = _conv_bwd_call(
        dva, dcb, ua, ub, cg, ha, a, sg, dbg, dza, dzb, (sbg, sza, szb), conv_a_full, conv_b_full, _tile(t, 256),
        ex=_merge(ex_a, ex_b))
    recv_mlp, sib_mix = _split(xo, ex_a, ex_b)
    r_mlp = chip_sums(mlp_w, p_mlp, recv_mlp)
    p_mix = pair_sums(mixer_w, g_mix, sib_mix)
    ex_a, ex_b = _ex_share_halves(r_mlp), _ex_scatter_to_owner(p_mix)
    g_in, xo = _tn_matmul(h, dproj, N_CHIPS, d, n4, False, True, tt, "dw_in", ex=_merge(ex_a, ex_b))
    red_mlp, recv_mix = _split(xo, ex_a, ex_b)
    r_mix = chip_sums(mixer_w, p_mix, recv_mix)
    ex_a, ex_b = _ex_sibling_halves([g_in]), _ex_share_halves(r_mix)
    tm_dx = _tile(t, 512)
    n_dx = t // tm_dx
    n_a = max(1, (3 * n_dx) // 8)
    dx_done, xo = _dx_call(dproj, xt, dx1, g1pre, w_in_g, tm_dx, 0, n_a, None, ex=_merge(ex_a, ex_b))
    sib_in, red_mix = _split(xo, ex_a, ex_b)
    p_in = pair_sums(["w_in"], [g_in], sib_in)
    (grad_x, dg1pre), recv_in = _dx_call(dproj, xt, dx1, g1pre, w_in_g, tm_dx, n_a, n_dx - n_a, dx_done,
                                         ex=_ex_scatter_to_owner(p_in))
    r_in = chip_sums(["w_in"], p_in, recv_in)
    out = {}

    small = [
        ("conv_b_w", dwb_conv, conv_b_w, m_conv_b_w, v_conv_b_w, True),
        ("conv_b_b", dbb, bb, row(m_conv_b_b), row(v_conv_b_b), False),
        ("b_in", dbin.reshape(7, d), b_in.reshape(7, d), m_b_in.reshape(7, d), v_b_in.reshape(7, d), False),
        ("norm1_pre_g", dg1pre, row(norm1_pre_g), row(m_norm1_pre_g), row(v_norm1_pre_g), False),
        ("conv_a_w", dwa_conv, conv_a_w, m_conv_a_w, v_conv_a_w, True),
        ("conv_a_b", dba, ba, row(m_conv_a_b), row(v_conv_a_b), False),
        ("ln_b_g", dlng, lng, row(m_ln_b_g), row(v_ln_b_g), False),
        ("ln_b_b", dlnb, lnb, row(m_ln_b_b), row(v_ln_b_b), False),
        ("norm1_post_g", dg1post, g1post, row(m_norm1_post_g), row(v_norm1_post_g), False),
        ("norm2_pre_g", dg2pre, g2pre, row(m_norm2_pre_g), row(v_norm2_pre_g), False),
        ("norm2_post_g", dg2post, g2post, row(m_norm2_post_g), row(v_norm2_post_g), False),
    ]
    res, red_in = _small_step_call([s[1] for s in small], loss_rows, [s[2] for s in small], [s[3] for s in small],
                                   [s[4] for s in small], [s[5] for s in small], d, _ex_share_halves(r_in))
    reduced = dict(zip(mlp_w + mixer_w + ["w_in"], red_mlp + red_mix + red_in))
    moments = dict(w_in=(m_w_in, v_w_in), w_a_out=(m_w_a_out, v_w_a_out), w_b_out=(m_w_b_out, v_w_b_out),
                   w_o=(m_w_o, v_w_o), w_mlp_in=(m_w_mlp_in, v_w_mlp_in), w_mlp_out=(m_w_mlp_out, v_w_mlp_out))
    for k in names:
        adam = _adam_sc_call if k != "w_in" and big[k].shape[0] % (SC_TILES * SUBLANES) == 0 else _adam_call
        out[k] = tuple(adam(big[k], reduced[k], *moments[k], "adamw_" + k))
    ns = len(small)
    loss = res[4 * ns][0, 0]
    shapes = dict(norm1_pre_g=norm1_pre_g.shape, b_in=b_in.shape, conv_a_w=conv_a_w.shape,
                  conv_a_b=conv_a_b.shape, conv_b_w=conv_b_w.shape, conv_b_b=conv_b_b.shape,
                  ln_b_g=ln_b_g.shape, ln_b_b=ln_b_b.shape, norm1_post_g=norm1_post_g.shape,
                  norm2_pre_g=norm2_pre_g.shape, norm2_post_g=norm2_post_g.shape)
    for i, s in enumerate(small):
        out[s[0]] = tuple(res[q * ns + i].reshape(shapes[s[0]]) for q in range(4))

    order = ["norm1_pre_g", "w_in", "b_in", "conv_a_w", "conv_a_b", "w_a_out", "conv_b_w", "conv_b_b",
             "ln_b_g", "ln_b_b", "w_b_out", "w_o", "norm1_post_g", "norm2_pre_g", "w_mlp_in", "w_mlp_out",
             "norm2_post_g"]
    return (loss, grad_x.reshape(x.shape), *[out[k][0] for k in order], *[out[k][1] for k in order],
            *[out[k][2] for k in order], *[out[k][3] for k in order])
```
